```python
import math
import jax
import jax.numpy as jnp
from jax import lax
import numpy as np

D_MODEL = 1024
BATCH = 16
SEQ = 256
DEPTH = 2
DEC_BATCH = 8
DEC_SEQ = 2048
PAST_LEN = 512

GRID_W = 64
ROPE_BASE = 10000.0
HEAD_DIM = 64
ATTN_WIDTH = D_MODEL // 2
N_HEADS = ATTN_WIDTH // HEAD_DIM
N_KV_HEADS = 2
KV_GROUP = N_HEADS // N_KV_HEADS
WINDOW = 128
ATTN_BLOCK = 128
FOURIER_WIDTH = D_MODEL // 4
FOURIER_GROUPS = 4
HGRN_WIDTH = D_MODEL // 4
HGRN_HEADS = 4
HGRN_DK = HGRN_WIDTH // HGRN_HEADS
HGRN_DV = HGRN_WIDTH // HGRN_HEADS
HGRN_CHUNK = 64
IN_SPLIT_SIZES = (ATTN_WIDTH, N_KV_HEADS * HEAD_DIM, N_KV_HEADS * HEAD_DIM, FOURIER_WIDTH, HGRN_WIDTH, HGRN_WIDTH, HGRN_WIDTH, HGRN_WIDTH, HGRN_WIDTH)
IN_WIDTH = ATTN_WIDTH + 2 * N_KV_HEADS * HEAD_DIM + FOURIER_WIDTH + 5 * HGRN_WIDTH
MIX_WIDTH = ATTN_WIDTH + FOURIER_WIDTH + HGRN_WIDTH
N_EXPERTS = 64
TOP_K = 8
EXPERT_FF = D_MODEL // 4
SHARED_FF = D_MODEL // 4
ROUTED_SCALE = 2.5
MOE_TOKEN_BLOCK = 1024
N_MOD = 6
LN_EPS = 1e-5
ADA_EPS = 1e-6
GN_EPS = 1e-6
DEEPNORM_ALPHA = (2 * DEPTH) ** 0.25
DEEPNORM_BETA = (8 * DEPTH) ** -0.25

kernel_name = 'hybrid_diffusion_trunk_step'

F32 = jnp.float32


def _ln_plain(x, eps):
    xf = x.astype(F32)
    mu = jnp.mean(xf, -1, keepdims=True)
    var = jnp.mean(jnp.square(xf - mu), -1, keepdims=True)
    return (xf - mu) * lax.rsqrt(var + eps)


def _layernorm(x, g, b):
    return (_ln_plain(x, LN_EPS) * g.astype(F32) + b.astype(F32)).astype(x.dtype)


def _modulate(x, shift, scale):
    return (_ln_plain(x, ADA_EPS) * (1.0 + scale.astype(F32)) + shift.astype(F32)).astype(x.dtype)


def _adaln_params(cond, w, b):
    m = jax.nn.silu(cond) @ w + b
    return jnp.split(m[:, None, :], N_MOD, axis=-1)


def _post_norm(x, y, gate, g, b):
    return _layernorm(DEEPNORM_ALPHA * x + gate * y, g, b)


def _split_in(h, w_in):
    proj = h @ w_in
    idx = np.cumsum(IN_SPLIT_SIZES)[:-1].tolist()
    return jnp.split(proj, idx, axis=-1)


def _grid_angles(n_tok):
    rows = n_tok // GRID_W
    row = jnp.repeat(jnp.arange(rows), GRID_W).astype(F32)
    col = jnp.tile(jnp.arange(GRID_W), rows).astype(F32)
    n_freq = HEAD_DIM // 4
    inv = ROPE_BASE ** (-jnp.arange(n_freq, dtype=F32) / n_freq)
    return jnp.stack([row[:, None] * inv, col[:, None] * inv], axis=1)


def _rope_2d(x, ang):
    B, T, H, _ = x.shape
    xr = x.astype(F32).reshape(B, T, H, 2, 2, HEAD_DIM // 4)
    cos = jnp.cos(ang)[None, :, None]
    sin = jnp.sin(ang)[None, :, None]
    x1, x2 = xr[..., 0, :], xr[..., 1, :]
    out = jnp.stack([x1 * cos - x2 * sin, x2 * cos + x1 * sin], axis=-2)
    return out.reshape(B, T, H, HEAD_DIM).astype(x.dtype)


def _attend(qb, kk, vv, valid, sink):
    s = jnp.einsum('bqhgd,bkhd->bhgqk', qb, kk).astype(F32) * (HEAD_DIM ** -0.5)
    s = jnp.where(valid, s, -jnp.inf)
    sink_col = jnp.broadcast_to(sink.astype(F32).reshape(N_KV_HEADS, KV_GROUP, 1, 1), s.shape[:-1] + (1,))
    p = jax.nn.softmax(jnp.concatenate([s, sink_col], axis=-1), axis=-1)[..., :-1]
    return jnp.einsum('bhgqk,bkhd->bqhgd', p.astype(vv.dtype), vv)


def _context_attention(q, k, v, sink):
    B, L = q.shape[:2]
    valid = jnp.ones((ATTN_BLOCK, L), bool)

    def block(i):
        qb = lax.dynamic_slice_in_dim(q, i * ATTN_BLOCK, ATTN_BLOCK, axis=1)
        return _attend(qb, k, v, valid, sink)

    o = lax.map(block, jnp.arange(L // ATTN_BLOCK))
    return jnp.moveaxis(o, 0, 1).reshape(B, L, ATTN_WIDTH)


def _window_attention(q, k, v, k_ctx, v_ctx, sink):
    B, T = q.shape[:2]
    L = k_ctx.shape[1]
    span = ATTN_BLOCK + 2 * WINDOW
    pad = ((0, 0), (WINDOW, WINDOW), (0, 0), (0, 0))
    kp = jnp.pad(k, pad)
    vp = jnp.pad(v, pad)
    k_ctx = k_ctx.astype(k.dtype)
    v_ctx = v_ctx.astype(v.dtype)
    q_off = jnp.arange(ATTN_BLOCK)
    k_off = jnp.arange(span) - WINDOW
    ctx_valid = jnp.ones((ATTN_BLOCK, L), bool)

    def block(i):
        start = i * ATTN_BLOCK
        qb = lax.dynamic_slice_in_dim(q, start, ATTN_BLOCK, axis=1)
        kw = lax.dynamic_slice_in_dim(kp, start, span, axis=1)
        vw = lax.dynamic_slice_in_dim(vp, start, span, axis=1)
        qpos = start + q_off
        kpos = start + k_off
        local = (jnp.abs(qpos[:, None] - kpos[None, :]) <= WINDOW) & (kpos >= 0)[None, :] & (kpos < T)[None, :]
        valid = jnp.concatenate([local, ctx_valid], axis=1)
        return _attend(qb, jnp.concatenate([kw, k_ctx], axis=1), jnp.concatenate([vw, v_ctx], axis=1), valid, sink)

    o = lax.map(block, jnp.arange(T // ATTN_BLOCK))
    return jnp.moveaxis(o, 0, 1).reshape(B, T, ATTN_WIDTH)


def _fourier_mix(u):
    B, T, _ = u.shape
    z = u.astype(F32).reshape(B, T, FOURIER_GROUPS, FOURIER_WIDTH // FOURIER_GROUPS)
    z = jnp.fft.fft2(z, axes=(1, 3), norm='ortho').real
    return z.reshape(B, T, FOURIER_WIDTH).astype(u.dtype)


def _hgrn_gates(z, lb):
    z = z.astype(F32)
    lb = lb.reshape(HGRN_HEADS, HGRN_DK)
    log_f = jnp.logaddexp(jnp.log(lb), jnp.log1p(-lb) + jax.nn.log_sigmoid(z))
    k = (1.0 - lb) * jax.nn.sigmoid(-z)
    return log_f, k


def _hgrn_scan(q, log_f, k, v, s0):
    B, T, H, _ = q.shape
    n = T // HGRN_CHUNK

    def to_chunks(a):
        return a.reshape(B, n, HGRN_CHUNK, H, a.shape[-1]).transpose(1, 0, 3, 2, 4)

    causal = jnp.tril(jnp.ones((HGRN_CHUNK, HGRN_CHUNK), bool))[None, None, :, :, None]

    def step(S, inp):
        qc, lfc, kc, vc = inp
        b = jnp.cumsum(lfc, axis=2)
        o_inter = jnp.einsum('bhtk,bhkv->bhtv', qc * jnp.exp(b), S)
        rel = jnp.where(causal, b[:, :, :, None, :] - b[:, :, None, :, :], -jnp.inf)
        a = jnp.einsum('bhtk,bhtsk,bhsk->bhts', qc, jnp.exp(rel), kc)
        o = o_inter + jnp.einsum('bhts,bhsv->bhtv', a, vc)
        b_last = b[:, :, -1:, :]
        S_new = jnp.exp(b_last[:, :, 0, :])[..., None] * S + jnp.einsum('bhsk,bhsv->bhkv', kc * jnp.exp(b_last - b), vc)
        return S_new, o

    S_fin, o = lax.scan(step, s0, (to_chunks(q), to_chunks(log_f), to_chunks(k), to_chunks(v)))
    return o.transpose(1, 0, 3, 2, 4).reshape(B, T, H, HGRN_DV), S_fin


def _hgrn_bidir(hq, hf_f, hf_b, hi, lb, s0):
    B, T, _ = hq.shape

    def heads(a):
        return a.astype(F32).reshape(B, T, HGRN_HEADS, -1)

    def flip(a):
        return jnp.flip(a, axis=1)

    q = heads(hq)
    v = heads(hi)
    lf_f, k_f = _hgrn_gates(heads(hf_f), lb[0])
    lf_b, k_b = _hgrn_gates(heads(hf_b), lb[1])
    o_f, s_f = _hgrn_scan(q, lf_f, k_f, v, s0[:, 0])
    o_b, s_b = _hgrn_scan(flip(q), flip(lf_b), flip(k_b), flip(v), s0[:, 1])
    return o_f + flip(o_b), jnp.stack([s_f, s_b], axis=1)


def _hgrn_readout(o, hg, gn):
    B, T = o.shape[:2]
    o = o * lax.rsqrt(jnp.mean(o * o, -1, keepdims=True) + GN_EPS) * gn.astype(F32)
    o = o * jax.nn.silu(hg.astype(F32).reshape(B, T, HGRN_HEADS, HGRN_DV))
    return o.reshape(B, T, HGRN_WIDTH).astype(hg.dtype)


def _mixer_context(h, w_in, w_out, sink, lb, gn):
    B, L, _ = h.shape
    q, k, v, u, hq, hf_f, hf_b, hi, hg = _split_in(h, w_in)
    q = q.reshape(B, L, N_KV_HEADS, KV_GROUP, HEAD_DIM)
    k = k.reshape(B, L, N_KV_HEADS, HEAD_DIM)
    v = v.reshape(B, L, N_KV_HEADS, HEAD_DIM)
    attn = _context_attention(q, k, v, sink)
    four = _fourier_mix(u)
    s0 = jnp.zeros((B, 2, HGRN_HEADS, HGRN_DK, HGRN_DV), F32)
    o, s_fin = _hgrn_bidir(hq, hf_f, hf_b, hi, lb, s0)
    rec = _hgrn_readout(o, hg, gn)
    out = jnp.concatenate([attn, four, rec], axis=-1) @ w_out
    return out, k, v, s_fin


def _mixer_latent(h, k_ctx, v_ctx, s_ctx, w_in, w_out, sink, lb, gn):
    B, T, _ = h.shape
    q, k, v, u, hq, hf_f, hf_b, hi, hg = _split_in(h, w_in)
    ang = _grid_angles(T)
    q = _rope_2d(q.reshape(B, T, N_HEADS, HEAD_DIM), ang).reshape(B, T, N_KV_HEADS, KV_GROUP, HEAD_DIM)
    k = _rope_2d(k.reshape(B, T, N_KV_HEADS, HEAD_DIM), ang)
    v = v.reshape(B, T, N_KV_HEADS, HEAD_DIM)
    attn = _window_attention(q, k, v, k_ctx, v_ctx, sink)
    four = _fourier_mix(u)
    o, _ = _hgrn_bidir(hq, hf_f, hf_b, hi, lb, s_ctx.astype(F32))
    rec = _hgrn_readout(o, hg, gn)
    return jnp.concatenate([attn, four, rec], axis=-1) @ w_out


def _moe(h, router_w, router_b, w1, w3, w2, sw1, sw3, sw2):
    B, T, D = h.shape
    n_tok = B * T
    x2 = h.reshape(n_tok, D)
    scores = jax.nn.sigmoid((x2 @ router_w).astype(F32))
    _, idx = lax.top_k(scores + router_b.astype(F32), TOP_K)
    sel = jnp.take_along_axis(scores, idx, axis=-1)
    g = sel / jnp.sum(sel, -1, keepdims=True) * ROUTED_SCALE
    gates = jnp.einsum('nk,nke->ne', g, jax.nn.one_hot(idx, N_EXPERTS, dtype=F32)).astype(h.dtype)
    blk = math.gcd(n_tok, MOE_TOKEN_BLOCK)

    def expert_block(args):
        xb, gb = args
        a = jnp.einsum('nd,edf->nef', xb, w1)
        b = jnp.einsum('nd,edf->nef', xb, w3)
        return jnp.einsum('nef,efd->nd', jax.nn.silu(a) * b * gb[..., None], w2)

    routed = lax.map(expert_block, (x2.reshape(n_tok // blk, blk, D), gates.reshape(n_tok // blk, blk, N_EXPERTS)))
    shared = (jax.nn.silu(x2 @ sw1) * (x2 @ sw3)) @ sw2
    return (routed.reshape(n_tok, D) + shared).reshape(B, T, D)


def setup_inputs(seed: int = 0) -> dict:
    key = jax.random.key(seed)
    ks = jax.random.split(key, 26)

    def nrm(i, shape, scale):
        return jax.random.normal(ks[i], shape, jnp.float32) * scale

    kv_shape = (DEC_BATCH, DEPTH, PAST_LEN, N_KV_HEADS, HEAD_DIM)
    return {
        'x_prompt': nrm(0, (BATCH, SEQ, D_MODEL), 1.0),
        'x_sample': nrm(1, (DEC_BATCH, DEC_SEQ, D_MODEL), 1.0),
        'cache_k': nrm(2, kv_shape, 1.0),
        'cache_v': nrm(3, kv_shape, 1.0),
        'state_hgrn': nrm(4, (DEC_BATCH, DEPTH, 2, HGRN_HEADS, HGRN_DK, HGRN_DV), 0.5),
        'c': nrm(5, (DEC_BATCH, D_MODEL), 1.0),
        'c_ctx': nrm(6, (D_MODEL,), 1.0),
        'w_ada': nrm(7, (DEPTH, D_MODEL, N_MOD * D_MODEL), 0.5 * D_MODEL ** -0.5),
        'b_ada': nrm(8, (DEPTH, N_MOD * D_MODEL), 0.02),
        'w_in': nrm(9, (DEPTH, D_MODEL, IN_WIDTH), D_MODEL ** -0.5),
        'w_out': nrm(10, (DEPTH, MIX_WIDTH, D_MODEL), DEEPNORM_BETA * MIX_WIDTH ** -0.5),
        'attn_sink': nrm(11, (DEPTH, N_HEADS), 0.5),
        'hgrn_lb': nrm(12, (DEPTH, 2, HGRN_WIDTH), 0.5),
        'hgrn_norm': 1.0 + nrm(13, (DEPTH, HGRN_DV), 0.02),
        'ln1_g': 1.0 + nrm(14, (DEPTH, D_MODEL), 0.02),
        'ln1_b': nrm(15, (DEPTH, D_MODEL), 0.02),
        'ln2_g': 1.0 + nrm(16, (DEPTH, D_MODEL), 0.02),
        'ln2_b': nrm(17, (DEPTH, D_MODEL), 0.02),
        'router_w': nrm(18, (DEPTH, D_MODEL, N_EXPERTS), D_MODEL ** -0.5),
        'router_b': nrm(19, (DEPTH, N_EXPERTS), 0.01),
        'moe_w1': nrm(20, (DEPTH, N_EXPERTS, D_MODEL, EXPERT_FF), D_MODEL ** -0.5),
        'moe_w3': nrm(21, (DEPTH, N_EXPERTS, D_MODEL, EXPERT_FF), D_MODEL ** -0.5),
        'moe_w2': nrm(22, (DEPTH, N_EXPERTS, EXPERT_FF, D_MODEL), DEEPNORM_BETA * EXPERT_FF ** -0.5),
        'shared_w1': nrm(23, (DEPTH, D_MODEL, SHARED_FF), D_MODEL ** -0.5),
        'shared_w3': nrm(24, (DEPTH, D_MODEL, SHARED_FF), D_MODEL ** -0.5),
        'shared_w2': nrm(25, (DEPTH, SHARED_FF, D_MODEL), DEEPNORM_BETA * SHARED_FF ** -0.5),
    }


def reference(x_prompt, x_sample, cache_k, cache_v, state_hgrn, c, c_ctx, w_ada, b_ada, w_in, w_out, attn_sink, hgrn_lb, hgrn_norm, ln1_g, ln1_b, ln2_g, ln2_b, router_w, router_b, moe_w1, moe_w3, moe_w2, shared_w1, shared_w3, shared_w2):
    lb_all = jnp.cumsum(jax.nn.softmax(hgrn_lb.astype(F32), axis=0), axis=0)
    lb_all = lb_all - lb_all[:1]
    xp = x_prompt
    xs = x_sample
    ks_out, vs_out, ss_out = [], [], []
    for l in range(DEPTH):
        mix_args = (w_in[l], w_out[l], attn_sink[l], lb_all[l], hgrn_norm[l])
        moe_args = (router_w[l], router_b[l], moe_w1[l], moe_w3[l], moe_w2[l], shared_w1[l], shared_w3[l], shared_w2[l])
        sh1, sc1, g1, sh2, sc2, g2 = _adaln_params(c_ctx[None, :], w_ada[l], b_ada[l])
        out, k_l, v_l, s_l = _mixer_context(_modulate(xp, sh1, sc1), *mix_args)
        xp = _post_norm(xp, out, g1, ln1_g[l], ln1_b[l])
        xp = _post_norm(xp, _moe(_modulate(xp, sh2, sc2), *moe_args), g2, ln2_g[l], ln2_b[l])
        ks_out.append(k_l)
        vs_out.append(v_l)
        ss_out.append(s_l)
        sh1, sc1, g1, sh2, sc2, g2 = _adaln_params(c, w_ada[l], b_ada[l])
        out = _mixer_latent(_modulate(xs, sh1, sc1), cache_k[:, l], cache_v[:, l], state_hgrn[:, l], *mix_args)
        xs = _post_norm(xs, out, g1, ln1_g[l], ln1_b[l])
        xs = _post_norm(xs, _moe(_modulate(xs, sh2, sc2), *moe_args), g2, ln2_g[l], ln2_b[l])
    new_cache_k = jnp.stack(ks_out, axis=1)
    new_cache_v = jnp.stack(vs_out, axis=1)
    new_state_hgrn = jnp.stack(ss_out, axis=1).astype(x_prompt.dtype)
    return (xp, xs, new_cache_k, new_cache_v, new_state_hgrn)
```

```python
import functools
import math

import numpy as np
import jax
import jax.numpy as jnp
from jax import lax
from jax.experimental import pallas as pl
from jax.experimental.pallas import tpu as pltpu
from jax.experimental.pallas import tpu_sc as plsc

F32 = jnp.float32
BF16 = jnp.bfloat16
I32 = jnp.int32

D_MODEL = 1024
HALF_D = D_MODEL // 2
DEPTH = 2
GRID_W = 64
ROPE_BASE = 10000.0
HEAD_DIM = 64
ATTN_WIDTH = 512
N_HEADS = 8
N_KV_HEADS = 2
KV_GROUP = 4
KV_WIDTH = N_KV_HEADS * HEAD_DIM
WINDOW = 128
ATTN_BLOCK = 128
FOURIER_WIDTH = 256
FOURIER_GROUPS = 4
HGRN_WIDTH = 256
HGRN_HEADS = 4
HGRN_DK = 64
HGRN_CHUNK = 64
IN_WIDTH = 2304
N_EXPERTS = 64
TOP_K = 8
EXPERT_FF = 256
ROUTED_SCALE = 2.5
N_MOD = 6
LN_EPS = 1e-5
ADA_EPS = 1e-6
GN_EPS = 1e-6
DEEPNORM_ALPHA = (2 * DEPTH) ** 0.25

COL_Q = 0
COL_K = 512
COL_V = 640
COL_U = 768
COL_HQ = 1024
COL_FF = 1280
COL_FB = 1536
COL_HI = 1792
COL_HG = 2048
ROPE_COLS = COL_V

V7X_LANES = 128
COND_ROWS = 16
NEG_BIG = -1e30
TOKEN_TILE = 512
EXPERT_TILE = 512
SC_BATCH = 128
META_RANK_LANE = 8

VMEM_LIMIT = 56 * 1024 * 1024


def _cparams(sem):
    return pltpu.CompilerParams(dimension_semantics=sem, vmem_limit_bytes=VMEM_LIMIT)


def _dot(a, b):
    return jnp.dot(a, b, preferred_element_type=F32)


def _dot_nt(a, b):
    return lax.dot_general(a, b, (((1,), (1,)), ((), ())), preferred_element_type=F32)


def _dot_tn(a, b):
    return lax.dot_general(a, b, (((0,), (0,)), ((), ())), preferred_element_type=F32)


def _split3(x):
    hi = x.astype(BF16)
    r1 = x - hi.astype(F32)
    mid = r1.astype(BF16)
    lo = (r1 - mid.astype(F32)).astype(BF16)
    return hi, mid, lo


def _dot_exact_lhs(m_bf16, x):
    hi, mid, lo = _split3(x)
    return _dot(m_bf16, hi) + _dot(m_bf16, mid) + _dot(m_bf16, lo)


def _dot_exact_rhs(x, m_bf16):
    hi, mid, lo = _split3(x)
    return _dot(hi, m_bf16) + _dot(mid, m_bf16) + _dot(lo, m_bf16)


def _dot_hp(a, b):
    a_hi = a.astype(BF16)
    a_lo = (a - a_hi.astype(F32)).astype(BF16)
    b_hi = b.astype(BF16)
    b_lo = (b - b_hi.astype(F32)).astype(BF16)
    return _dot(a_hi, b_hi) + _dot(a_hi, b_lo) + _dot(a_lo, b_hi)


def _pack_bf16_pair(lo, hi):
    def rne(x):
        b = lax.bitcast_convert_type(x, I32)
        return b + 0x7FFF + (lax.shift_right_logical(b, 16) & 1)
    return lax.shift_right_logical(rne(lo), 16) | (rne(hi) & -65536)


def _unpack_bf16_pair(w):
    lo = lax.bitcast_convert_type(lax.shift_left(w, 16), F32)
    hi = lax.bitcast_convert_type(w & -65536, F32)
    return lo, hi


def _ln_plain(x, eps):
    mu = jnp.mean(x, axis=-1, keepdims=True)
    xc = x - mu
    var = jnp.mean(xc * xc, axis=-1, keepdims=True)
    return xc * lax.rsqrt(var + eps)


def _silu(x):
    return x * jax.nn.sigmoid(x)


def _adaln_kernel(c_ref, w_ref, b_ref, o_ref):
    s = _silu(c_ref[...])
    o_ref[0] = _dot_hp(s, w_ref[0]) + b_ref[0]


def _adaln(cond, w_ada, b_ada):
    return pl.pallas_call(
        _adaln_kernel,
        grid=(DEPTH, N_MOD),
        in_specs=[
            pl.BlockSpec((COND_ROWS, D_MODEL), lambda l, j: (0, 0)),
            pl.BlockSpec((1, D_MODEL, D_MODEL), lambda l, j: (l, 0, j)),
            pl.BlockSpec((1, 1, D_MODEL), lambda l, j: (l, 0, j)),
        ],
        out_specs=pl.BlockSpec((1, COND_ROWS, D_MODEL), lambda l, j: (l, 0, j)),
        out_shape=jax.ShapeDtypeStruct((DEPTH, COND_ROWS, N_MOD * D_MODEL), F32),
        compiler_params=_cparams(("arbitrary", "arbitrary")),
        name="adaln",
    )(cond, w_ada, b_ada.reshape(DEPTH, 1, N_MOD * D_MODEL))


class _Layout:
    def __init__(self, b_ctx, t_ctx, b_lat, t_lat):
        self.b_ctx, self.t_ctx, self.b_lat, self.t_lat = b_ctx, t_ctx, b_lat, t_lat
        self.n_ctx = b_ctx * t_ctx
        self.n_lat = b_lat * t_lat
        self.n = self.n_ctx + self.n_lat

    def cond_row(self, tile, tm):
        n_ctx_tiles = self.n_ctx // tm
        per_batch = self.t_lat // tm
        return jnp.where(tile < n_ctx_tiles, 0, 1 + (tile - n_ctx_tiles) // per_batch)


def _inproj_kernel(x_ref, mod_ref, w_ref, cos_ref, sin_ref, o_ref, wb_ref):
    @pl.when(pl.program_id(0) == 0)
    def _():
        wb_ref[...] = w_ref[...].astype(BF16)

    x = x_ref[...]
    shift = mod_ref[0, 0:1, :]
    scale = mod_ref[0, 1:2, :]
    h = (_ln_plain(x, ADA_EPS) * (1.0 + scale) + shift).astype(BF16)
    p = _dot(h, wb_ref[...])
    cos = cos_ref[...]
    sin = sin_ref[...]
    lane = lax.broadcasted_iota(I32, cos.shape, 1)
    first_half = (lane & 31) < 16
    for cb in range(ROPE_COLS // V7X_LANES):
        seg = p[:, cb * V7X_LANES:(cb + 1) * V7X_LANES]
        partner = jnp.where(first_half, pltpu.roll(seg, V7X_LANES - 16, 1), pltpu.roll(seg, 16, 1))
        o_ref[:, cb * V7X_LANES:(cb + 1) * V7X_LANES] = seg * cos + partner * sin
    o_ref[:, ROPE_COLS:] = p[:, ROPE_COLS:]


def _rope_tables(lay, tm):
    t = lay.t_lat
    pos = jnp.arange(t)
    row = (pos // GRID_W).astype(F32)
    col = (pos % GRID_W).astype(F32)
    n_freq = HEAD_DIM // 4
    inv = ROPE_BASE ** (-jnp.arange(n_freq, dtype=F32) / n_freq)
    ang_r = row[:, None] * inv
    ang_c = col[:, None] * inv
    ang = jnp.concatenate([ang_r, ang_r, ang_c, ang_c], axis=1)
    sign = jnp.concatenate([-jnp.ones(n_freq), jnp.ones(n_freq), -jnp.ones(n_freq), jnp.ones(n_freq)]).astype(F32)
    cos = jnp.cos(ang)
    sin = jnp.sin(ang) * sign
    cos = jnp.concatenate([jnp.ones((tm, HEAD_DIM), F32), cos], axis=0)
    sin = jnp.concatenate([jnp.zeros((tm, HEAD_DIM), F32), sin], axis=0)
    return jnp.tile(cos, (1, 2)), jnp.tile(sin, (1, 2))


def _inproj(x, mod_l, w_in_l, cos_t, sin_t, lay, tm):
    n_tiles = lay.n // tm
    n_ctx_tiles = lay.n_ctx // tm
    per_batch = lay.t_lat // tm

    def tbl(i):
        return jnp.where(i < n_ctx_tiles, 0, 1 + (i - n_ctx_tiles) % per_batch)

    return pl.pallas_call(
        _inproj_kernel,
        grid=(n_tiles,),
        in_specs=[
            pl.BlockSpec((tm, D_MODEL), lambda i: (i, 0)),
            pl.BlockSpec((1, N_MOD, D_MODEL), lambda i: (lay.cond_row(i, tm), 0, 0)),
            pl.BlockSpec((D_MODEL, IN_WIDTH), lambda i: (0, 0), pipeline_mode=pl.Buffered(1)),
            pl.BlockSpec((tm, V7X_LANES), lambda i: (tbl(i), 0)),
            pl.BlockSpec((tm, V7X_LANES), lambda i: (tbl(i), 0)),
        ],
        out_specs=pl.BlockSpec((tm, IN_WIDTH), lambda i: (i, 0)),
        out_shape=jax.ShapeDtypeStruct((lay.n, IN_WIDTH), F32),
        scratch_shapes=[pltpu.VMEM((D_MODEL, IN_WIDTH), BF16)],
        compiler_params=_cparams(("arbitrary",)),
        name="inproj",
    )(x, mod_l, w_in_l, cos_t, sin_t)


def _attn_kernel(sink_ref, q_ref, *refs, n_local, has_ctx, t_total):
    o_ref = refs[-1]
    k_refs = refs[:n_local]
    v_refs = refs[n_local:2 * n_local]
    tq = q_ref.shape[0]
    scale = HEAD_DIM ** -0.5
    if n_local > 1:
        i = pl.program_id(1)
        r = lax.broadcasted_iota(I32, (KV_GROUP * tq, n_local * tq), 0) & (tq - 1)
        c = lax.broadcasted_iota(I32, (KV_GROUP * tq, n_local * tq), 1)
        kpos = (i - 1) * tq + c
        rel = c - tq - r
        valid = (jnp.abs(rel) <= WINDOW) & (kpos >= 0) & (kpos < t_total)
    else:
        valid = None
    for g in range(N_KV_HEADS):
        ksl = slice(g * HEAD_DIM, (g + 1) * HEAD_DIM)
        k_loc = jnp.concatenate([kr[:, ksl] for kr in k_refs], axis=0).astype(BF16)
        v_loc = jnp.concatenate([vr[:, ksl] for vr in v_refs], axis=0).astype(BF16)
        heads = [g * KV_GROUP + h for h in range(KV_GROUP)]
        qs = jnp.concatenate([q_ref[:, hh * HEAD_DIM:(hh + 1) * HEAD_DIM] for hh in heads], axis=0)
        qs = (qs * scale).astype(BF16)
        sink = jnp.concatenate([jnp.full((tq, 1), sink_ref[0, hh], F32) for hh in heads], axis=0)
        s_loc = _dot_nt(qs, k_loc)
        if valid is not None:
            s_loc = jnp.where(valid, s_loc, NEG_BIG)
        m = jnp.maximum(jnp.max(s_loc, axis=1, keepdims=True), sink)
        if has_ctx:
            kc = refs[2 * n_local][0][:, ksl].astype(BF16)
            vc = refs[2 * n_local + 1][0][:, ksl].astype(BF16)
            s_ctx = _dot_nt(qs, kc)
            m = jnp.maximum(m, jnp.max(s_ctx, axis=1, keepdims=True))
        p_loc = jnp.exp(s_loc - m)
        den = jnp.sum(p_loc, axis=1, keepdims=True) + jnp.exp(sink - m)
        acc = _dot(p_loc.astype(BF16), v_loc)
        if has_ctx:
            p_ctx = jnp.exp(s_ctx - m)
            den = den + jnp.sum(p_ctx, axis=1, keepdims=True)
            acc = acc + _dot(p_ctx.astype(BF16), vc)
        o = acc / den
        for h, hh in enumerate(heads):
            o_ref[:, hh * HEAD_DIM:(hh + 1) * HEAD_DIM] = o[h * tq:(h + 1) * tq].astype(o_ref.dtype)


def _attn_context(proj, sink_l, lay):
    t = lay.t_ctx
    kb, vb = COL_K // KV_WIDTH, COL_V // KV_WIDTH
    body = functools.partial(_attn_kernel, n_local=1, has_ctx=False, t_total=t)
    return pl.pallas_call(
        body,
        grid=(lay.b_ctx,),
        in_specs=[
            pl.BlockSpec(memory_space=pltpu.SMEM),
            pl.BlockSpec((t, ATTN_WIDTH), lambda b: (b, 0)),
            pl.BlockSpec((t, KV_WIDTH), lambda b: (b, kb)),
            pl.BlockSpec((t, KV_WIDTH), lambda b: (b, vb)),
        ],
        out_specs=pl.BlockSpec((t, ATTN_WIDTH), lambda b: (b, 0)),
        out_shape=jax.ShapeDtypeStruct((lay.n_ctx, ATTN_WIDTH), BF16),
        compiler_params=_cparams(("arbitrary",)),
        name="attn_ctx",
    )(sink_l, proj, proj, proj)


def _attn_latent(proj, k_ctx, v_ctx, sink_l, lay):
    t = lay.t_lat
    tq = ATTN_BLOCK
    nq = t // tq
    base = lay.n_ctx // tq
    kb, vb = COL_K // KV_WIDTH, COL_V // KV_WIDTH
    past = k_ctx.shape[1]

    def rows(off):
        return lambda b, i: base + b * nq + jnp.clip(i + off, 0, nq - 1)

    def kv_specs(col):
        return [pl.BlockSpec((tq, KV_WIDTH), (lambda b, i, f=rows(off): (f(b, i), col))) for off in (-1, 0, 1)]

    body = functools.partial(_attn_kernel, n_local=3, has_ctx=True, t_total=t)
    return pl.pallas_call(
        body,
        grid=(lay.b_lat, nq),
        in_specs=[
            pl.BlockSpec(memory_space=pltpu.SMEM),
            pl.BlockSpec((tq, ATTN_WIDTH), lambda b, i: (base + b * nq + i, 0)),
            *kv_specs(kb),
            *kv_specs(vb),
            pl.BlockSpec((1, past, KV_WIDTH), lambda b, i: (b, 0, 0)),
            pl.BlockSpec((1, past, KV_WIDTH), lambda b, i: (b, 0, 0)),
        ],
        out_specs=pl.BlockSpec((tq, ATTN_WIDTH), lambda b, i: (b * nq + i, 0)),
        out_shape=jax.ShapeDtypeStruct((lay.n_lat, ATTN_WIDTH), BF16),
        compiler_params=_cparams(("arbitrary", "arbitrary")),
        name="attn_lat",
    )(sink_l, proj, proj, proj, proj, proj, proj, proj, k_ctx, v_ctx)


def _fourier_kernel(cs_ref, u_ref, cc_ref, sc_ref, o_ref, csb_ref, *, scale):
    @pl.when(pl.program_id(1) == 0)
    def _():
        csb_ref[...] = cs_ref[...].astype(BF16)

    z = u_ref[...].astype(BF16)
    zc = _dot(z, cc_ref[...].astype(BF16)).astype(BF16)
    zs = _dot(z, sc_ref[...].astype(BF16)).astype(BF16)
    zz = jnp.concatenate([zc, zs], axis=0)
    o_ref[...] = (_dot(csb_ref[...], zz) * scale).astype(o_ref.dtype)


@functools.lru_cache(maxsize=None)
def _dft_tables(t):
    idx = np.arange(t, dtype=np.int64)
    ang = 2.0 * np.pi * ((idx[:, None] * idx[None, :]) % t).astype(np.float64) / t
    cs = np.concatenate([np.cos(ang), -np.sin(ang)], axis=1).astype(np.float32)
    cw = FOURIER_WIDTH // FOURIER_GROUPS
    cidx = np.arange(cw, dtype=np.int64)
    cang = 2.0 * np.pi * ((cidx[:, None] * cidx[None, :]) % cw).astype(np.float64) / cw
    eye = np.eye(FOURIER_GROUPS)
    cc = np.kron(eye, np.cos(cang)).astype(np.float32)
    sc = np.kron(eye, np.sin(cang)).astype(np.float32)
    return cs, cc, sc


def _fourier(proj, row0, b, t, tm, name):
    cs, cc, sc = _dft_tables(t)
    cw = FOURIER_WIDTH // FOURIER_GROUPS
    nt = t // tm
    ub = COL_U // FOURIER_WIDTH
    base = row0 // t
    body = functools.partial(_fourier_kernel, scale=1.0 / math.sqrt(t * cw))
    return pl.pallas_call(
        body,
        grid=(nt, b),
        in_specs=[
            pl.BlockSpec((tm, 2 * t), lambda i, bb: (i, 0)),
            pl.BlockSpec((t, FOURIER_WIDTH), lambda i, bb: (base + bb, ub)),
            pl.BlockSpec((FOURIER_WIDTH, FOURIER_WIDTH), lambda i, bb: (0, 0)),
            pl.BlockSpec((FOURIER_WIDTH, FOURIER_WIDTH), lambda i, bb: (0, 0)),
        ],
        out_specs=pl.BlockSpec((tm, FOURIER_WIDTH), lambda i, bb: (bb * nt + i, 0)),
        out_shape=jax.ShapeDtypeStruct((b * t, FOURIER_WIDTH), BF16),
        scratch_shapes=[pltpu.VMEM((tm, 2 * t), BF16)],
        compiler_params=_cparams(("arbitrary", "arbitrary")),
        name=name,
    )(jnp.asarray(cs), proj, jnp.asarray(cc), jnp.asarray(sc))


HGRN_LEVELS = (64, 32, 16, 8, 4, 2)


@functools.lru_cache(maxsize=None)
def _hgrn_tables():
    c = HGRN_CHUNK
    out = []
    for reverse in (False, True):
        cum = np.triu(np.ones((c, c))) if reverse else np.tril(np.ones((c, c)))
        mats = [cum]
        for m in HGRN_LEVELS:
            t = np.arange(c)
            bnd = (t // m) * m + (m // 2 if reverse else m // 2 - 1)
            mats.append(cum[bnd])
        out.append(np.concatenate(mats, axis=0).astype(np.float32))
    return np.stack(out)


def _hgrn_chunk(q, z, v, loglb, log1mlb, onemlb, m_all, st_ref, d, reverse):
    c = HGRN_CHUNK
    log_sig = jnp.minimum(z, 0.0) - jnp.log1p(jnp.exp(-jnp.abs(z)))
    bb = log1mlb + log_sig
    mx = jnp.maximum(loglb, bb)
    lf = mx + jnp.log1p(jnp.exp(-jnp.abs(loglb - bb)))
    kk = onemlb * jax.nn.sigmoid(-z)

    r_all = _dot_exact_lhs(m_all, lf)
    b = r_all[0:c]
    b_end = b[0:1] if reverse else b[c - 1:c]
    row = lax.broadcasted_iota(I32, (c, 1), 0)
    ti = lax.broadcasted_iota(I32, (c, c), 0)
    si = lax.broadcasted_iota(I32, (c, c), 1)

    qt = (q * jnp.exp(b)).astype(BF16)
    kt = (kk * jnp.exp(b_end - b)).astype(BF16)
    decay = jnp.exp(b_end)
    vb = v.astype(BF16)

    qf, kf, masks = [], [], []
    for li, m in enumerate(HGRN_LEVELS):
        r = r_all[(li + 1) * c:(li + 2) * c]
        upper = (row & (m - 1)) >= (m // 2)
        q_side = jnp.logical_not(upper) if reverse else upper
        e = jnp.exp(jnp.where(q_side, b - r, r - b))
        qf.append(jnp.where(q_side, q * e, 0.0).astype(BF16))
        kf.append(jnp.where(q_side, 0.0, kk * e).astype(BF16))
        masks.append((ti & -m) == (si & -m))
    qb = q.astype(BF16)
    kb = kk.astype(BF16)
    eye = ti == si

    outs = []
    for h in range(HGRN_HEADS):
        sl = slice(h * HGRN_DK, (h + 1) * HGRN_DK)
        a = jnp.where(eye, _dot_nt(qb[:, sl], kb[:, sl]), 0.0)
        for li in range(len(HGRN_LEVELS)):
            a = a + jnp.where(masks[li], _dot_nt(qf[li][:, sl], kf[li][:, sl]), 0.0)
        st = st_ref[d, h]
        o = _dot_nt(qt[:, sl], st.astype(BF16)) + _dot(a.astype(BF16), vb[:, sl])
        st_ref[d, h] = st * decay[:, sl] + _dot_tn(vb[:, sl], kt[:, sl])
        outs.append(o)
    return jnp.concatenate(outs, axis=1)


def _hgrn_kernel(hq_ref, ff_ref, fb_ref, hi_ref, hg_ref, lbp_ref, gn_ref, mall_ref, ones_ref, s0_ref,
                 rec_ref, sfin_ref, st_ref, of_ref, ob_ref, *, t):
    c = HGRN_CHUNK
    n = t // c
    st_ref[...] = s0_ref[0]

    def body(ci, carry):
        rf = pl.ds(pl.multiple_of(ci * c, c), c)
        rb = pl.ds(pl.multiple_of((n - 1 - ci) * c, c), c)
        of_ref[rf, :] = _hgrn_chunk(hq_ref[rf, :], ff_ref[rf, :], hi_ref[rf, :], lbp_ref[0, 0:1, :],
                                    lbp_ref[0, 1:2, :], lbp_ref[0, 2:3, :], mall_ref[0].astype(BF16),
                                    st_ref, 0, False)
        ob_ref[rb, :] = _hgrn_chunk(hq_ref[rb, :], fb_ref[rb, :], hi_ref[rb, :], lbp_ref[1, 0:1, :],
                                    lbp_ref[1, 1:2, :], lbp_ref[1, 2:3, :], mall_ref[1].astype(BF16),
                                    st_ref, 1, True)
        return carry

    lax.fori_loop(0, n, body, 0)
    sfin_ref[0] = st_ref[...]
    o = of_ref[...] + ob_ref[...]
    ms = _dot_exact_rhs(o * o, ones_ref[...].astype(BF16)) * (1.0 / HGRN_DK)
    o = o * lax.rsqrt(ms + GN_EPS) * gn_ref[...]
    rec_ref[...] = (o * _silu(hg_ref[...])).astype(rec_ref.dtype)


def _hgrn(proj, row0, b, t, lbp, gn_row, s0t, name):
    base = row0 // t
    m_all = jnp.asarray(_hgrn_tables())
    ones_bd = jnp.asarray(np.kron(np.eye(HGRN_HEADS), np.ones((HGRN_DK, HGRN_DK))).astype(np.float32))

    def col(cstart):
        return pl.BlockSpec((t, HGRN_WIDTH), lambda bb, cb=cstart // HGRN_WIDTH: (base + bb, cb))

    const2 = lambda bb: (0, 0)
    const3 = lambda bb: (0, 0, 0)
    st_shape = (2, HGRN_HEADS, HGRN_DK, HGRN_DK)
    body = functools.partial(_hgrn_kernel, t=t)
    return pl.pallas_call(
        body,
        grid=(b,),
        in_specs=[
            col(COL_HQ), col(COL_FF), col(COL_FB), col(COL_HI), col(COL_HG),
            pl.BlockSpec((2, 3, HGRN_WIDTH), const3),
            pl.BlockSpec((1, HGRN_WIDTH), const2),
            pl.BlockSpec(m_all.shape, const3),
            pl.BlockSpec(ones_bd.shape, const2),
            pl.BlockSpec((1,) + st_shape, lambda bb: (bb, 0, 0, 0, 0)),
        ],
        out_specs=[
            pl.BlockSpec((t, HGRN_WIDTH), lambda bb: (bb, 0)),
            pl.BlockSpec((1,) + st_shape, lambda bb: (bb, 0, 0, 0, 0)),
        ],
        out_shape=[
            jax.ShapeDtypeStruct((b * t, HGRN_WIDTH), BF16),
            jax.ShapeDtypeStruct((b,) + st_shape, F32),
        ],
        scratch_shapes=[
            pltpu.VMEM(st_shape, F32),
            pltpu.VMEM((t, HGRN_WIDTH), F32),
            pltpu.VMEM((t, HGRN_WIDTH), F32),
        ],
        compiler_params=_cparams(("arbitrary",)),
        name=name,
    )(proj, proj, proj, proj, proj, lbp, gn_row, m_all, ones_bd, s0t)


def _outproj_kernel(attn_ref, four_ref, rec_ref, x_ref, mod_ref, w_ref, g_ref, b_ref, rw_ref, rb_ref,
                    x1_ref, hp_ref, meta_ref, gate_ref, cnt_ref, wb_ref, tril_ref, run_ref):
    tm = x_ref.shape[0]

    @pl.when(pl.program_id(0) == 0)
    def _():
        wb_ref[...] = w_ref[...].astype(BF16)
        r = lax.broadcasted_iota(I32, (tm, tm), 0)
        c = lax.broadcasted_iota(I32, (tm, tm), 1)
        tril_ref[...] = jnp.where(r > c, 1.0, 0.0).astype(BF16)
        run_ref[...] = jnp.zeros_like(run_ref)

    out = _dot(attn_ref[...], wb_ref[0:ATTN_WIDTH, :])
    out = out + _dot(four_ref[...], wb_ref[ATTN_WIDTH:ATTN_WIDTH + FOURIER_WIDTH, :])
    out = out + _dot(rec_ref[...], wb_ref[ATTN_WIDTH + FOURIER_WIDTH:, :])
    gate1 = mod_ref[0, 2:3, :]
    y = DEEPNORM_ALPHA * x_ref[...] + gate1 * out
    x1 = _ln_plain(y, LN_EPS) * g_ref[...] + b_ref[...]
    x1_ref[...] = x1
    h2 = _ln_plain(x1, ADA_EPS) * (1.0 + mod_ref[0, 4:5, :]) + mod_ref[0, 3:4, :]
    hp_ref[...] = _pack_bf16_pair(h2[:, :HALF_D], h2[:, HALF_D:])

    scores = jax.nn.sigmoid(_dot_hp(h2, rw_ref[...]))
    remaining = scores + rb_ref[...]
    lane = lax.broadcasted_iota(I32, scores.shape, 1)
    chosen = jnp.zeros(scores.shape, jnp.bool_)
    picks = []
    for _ in range(TOP_K):
        mx = jnp.max(remaining, axis=1, keepdims=True)
        first = jnp.min(jnp.where(remaining == mx, lane, N_EXPERTS), axis=1, keepdims=True)
        pick = lane == first
        picks.append((pick, first))
        chosen = jnp.logical_or(chosen, pick)
        remaining = jnp.where(pick, -jnp.inf, remaining)
    sel = jnp.where(chosen, scores, 0.0)
    gates = sel / jnp.sum(sel, axis=1, keepdims=True) * ROUTED_SCALE

    onehot = jnp.where(chosen, 1.0, 0.0)
    rank = run_ref[...] + _dot(tril_ref[...], onehot.astype(BF16))
    run_ref[...] += jnp.sum(onehot, axis=0, keepdims=True)
    cnt_ref[...] = run_ref[...]

    lane_w = lax.broadcasted_iota(I32, (tm, V7X_LANES), 1)
    meta = jnp.zeros((tm, V7X_LANES), I32)
    gate_k = jnp.zeros((tm, V7X_LANES), F32)
    for k, (pick, first) in enumerate(picks):
        rk = jnp.sum(jnp.where(pick, rank, 0.0), axis=1, keepdims=True).astype(I32)
        gk = jnp.sum(jnp.where(pick, gates, 0.0), axis=1, keepdims=True)
        meta = jnp.where(lane_w == k, first, meta)
        meta = jnp.where(lane_w == META_RANK_LANE + k, rk, meta)
        gate_k = jnp.where(lane_w == k, gk, gate_k)
    meta_ref[...] = meta
    gate_ref[...] = gate_k


def _outproj(attn, four, rec, x, mod_l, w_out_l, g1, b1, rw, rb, lay, tm):
    n_tiles = lay.n // tm
    row = lambda i: (i, 0)
    const = lambda i: (0, 0)
    return pl.pallas_call(
        _outproj_kernel,
        grid=(n_tiles,),
        in_specs=[
            pl.BlockSpec((tm, ATTN_WIDTH), row),
            pl.BlockSpec((tm, FOURIER_WIDTH), row),
            pl.BlockSpec((tm, HGRN_WIDTH), row),
            pl.BlockSpec((tm, D_MODEL), row),
            pl.BlockSpec((1, N_MOD, D_MODEL), lambda i: (lay.cond_row(i, tm), 0, 0)),
            pl.BlockSpec((D_MODEL, D_MODEL), const),
            pl.BlockSpec((1, D_MODEL), const),
            pl.BlockSpec((1, D_MODEL), const),
            pl.BlockSpec((D_MODEL, N_EXPERTS), const),
            pl.BlockSpec((1, N_EXPERTS), const),
        ],
        out_specs=[
            pl.BlockSpec((tm, D_MODEL), row),
            pl.BlockSpec((tm, HALF_D), row),
            pl.BlockSpec((tm, V7X_LANES), row),
            pl.BlockSpec((tm, V7X_LANES), row),
            pl.BlockSpec((1, N_EXPERTS), const),
        ],
        out_shape=[
            jax.ShapeDtypeStruct((lay.n, D_MODEL), F32),
            jax.ShapeDtypeStruct((lay.n, HALF_D), I32),
            jax.ShapeDtypeStruct((lay.n, V7X_LANES), I32),
            jax.ShapeDtypeStruct((lay.n, V7X_LANES), F32),
            jax.ShapeDtypeStruct((1, N_EXPERTS), F32),
        ],
        scratch_shapes=[
            pltpu.VMEM((D_MODEL, D_MODEL), BF16),
            pltpu.VMEM((tm, tm), BF16),
            pltpu.VMEM((1, N_EXPERTS), F32),
        ],
        compiler_params=_cparams(("arbitrary",)),
        name="outproj_router",
    )(attn, four, rec, x, mod_l, w_out_l, g1, b1, rw, rb)


def _sc_workers():
    info = plsc.get_sparse_core_info()
    return info.num_cores, info.num_cores * info.num_subcores


def _sc_scatter_rows(rows, pos_b, r_out):
    nc, nw = _sc_workers()
    n, w = rows.shape
    nbt, copies, _ = pos_b.shape
    assert nbt * SC_BATCH == n and nbt % nw == 0
    per_w = nbt // nw
    mesh = plsc.VectorSubcoreMesh(core_axis_name="c", subcore_axis_name="s")

    @functools.partial(
        pl.kernel, mesh=mesh, out_type=jax.ShapeDtypeStruct((r_out, w), rows.dtype),
        scratch_types=[pltpu.VMEM((copies, SC_BATCH), I32), pltpu.VMEM((SC_BATCH, w), rows.dtype),
                       pltpu.SemaphoreType.DMA],
        name="sc_dispatch")
    def k(rows_hbm, pos_hbm, out_hbm, idx_v, rows_v, sem):
        wid = lax.axis_index("s") * nc + lax.axis_index("c")

        @pl.loop(0, per_w)
        def _(j):
            b = wid * per_w + j
            pltpu.sync_copy(pos_hbm.at[b], idx_v)
            pltpu.sync_copy(rows_hbm.at[pl.ds(b * SC_BATCH, SC_BATCH)], rows_v)
            for q in range(copies):
                pltpu.async_copy(rows_v, out_hbm.at[idx_v.at[q]], sem).wait()

    return k(rows, pos_b)


def _sc_gather_rows(table, idx):
    nc, nw = _sc_workers()
    r = idx.shape[0]
    w = table.shape[1]
    assert r % (nw * SC_BATCH) == 0
    per_w = r // nw
    nb = per_w // SC_BATCH
    mesh = plsc.VectorSubcoreMesh(core_axis_name="c", subcore_axis_name="s")

    @functools.partial(
        pl.kernel, mesh=mesh, out_type=jax.ShapeDtypeStruct((r, w), table.dtype),
        scratch_types=[pltpu.VMEM((per_w,), I32), pltpu.VMEM((SC_BATCH, w), table.dtype),
                       pltpu.SemaphoreType.DMA],
        name="sc_combine")
    def k(table_hbm, idx_hbm, out_hbm, idx_v, rows_v, sem):
        wid = lax.axis_index("s") * nc + lax.axis_index("c")
        base = wid * per_w
        pltpu.sync_copy(idx_hbm.at[pl.ds(base, per_w)], idx_v)

        @pl.loop(0, nb)
        def _(j):
            off = j * SC_BATCH
            pltpu.async_copy(table_hbm.at[idx_v.at[pl.ds(off, SC_BATCH)]], rows_v, sem).wait()
            pltpu.sync_copy(rows_v, out_hbm.at[pl.ds(base + off, SC_BATCH)])

    return k(table, idx)


def _experts_kernel(te_ref, na_ref, x_ref, w1_ref, w3_ref, w2_ref, o_ref, w1b_ref, w3b_ref, w2b_ref):
    j = pl.program_id(0)
    na = na_ref[0]
    jj = jnp.minimum(j, na - 1)
    e = te_ref[jj]
    prev = te_ref[jnp.maximum(jj - 1, 0)]
    active = j < na

    @pl.when(jnp.logical_and(active, jnp.logical_or(j == 0, e != prev)))
    def _():
        w1b_ref[...] = w1_ref[0].astype(BF16)
        w3b_ref[...] = w3_ref[0].astype(BF16)
        w2b_ref[...] = w2_ref[0].astype(BF16)

    @pl.when(active)
    def _():
        lo, hi = _unpack_bf16_pair(x_ref[...])
        lo = lo.astype(BF16)
        hi = hi.astype(BF16)
        a = _dot(lo, w1b_ref[0:HALF_D, :]) + _dot(hi, w1b_ref[HALF_D:, :])
        b = _dot(lo, w3b_ref[0:HALF_D, :]) + _dot(hi, w3b_ref[HALF_D:, :])
        y = _dot((_silu(a) * b).astype(BF16), w2b_ref[...])
        o_ref[...] = _pack_bf16_pair(y[:, :HALF_D], y[:, HALF_D:])


def _experts(xs, tile_expert, n_active, w1, w3, w2, tm):
    r = xs.shape[0]
    n_tiles = r // tm

    def xmap(j, te, na):
        return (jnp.minimum(j, na[0] - 1), 0)

    def wmap(j, te, na):
        return (te[jnp.minimum(j, na[0] - 1)], 0, 0)

    grid_spec = pltpu.PrefetchScalarGridSpec(
        num_scalar_prefetch=2,
        grid=(n_tiles,),
        in_specs=[
            pl.BlockSpec((tm, HALF_D), xmap),
            pl.BlockSpec((1, D_MODEL, EXPERT_FF), wmap),
            pl.BlockSpec((1, D_MODEL, EXPERT_FF), wmap),
            pl.BlockSpec((1, EXPERT_FF, D_MODEL), wmap),
        ],
        out_specs=pl.BlockSpec((tm, HALF_D), xmap),
        scratch_shapes=[
            pltpu.VMEM((D_MODEL, EXPERT_FF), BF16),
            pltpu.VMEM((D_MODEL, EXPERT_FF), BF16),
            pltpu.VMEM((EXPERT_FF, D_MODEL), BF16),
        ],
    )
    return pl.pallas_call(
        _experts_kernel,
        grid_spec=grid_spec,
        out_shape=jax.ShapeDtypeStruct((r, HALF_D), I32),
        compiler_params=_cparams(("arbitrary",)),
        name="experts",
    )(tile_expert, n_active, xs, w1, w3, w2)


def _combine_kernel(yp_ref, gate_ref, hp_ref, sw1_ref, sw3_ref, sw2_ref, x_ref, mod_ref, g_ref, b_ref, o_ref,
                    w1b_ref, w3b_ref, w2b_ref):
    @pl.when(pl.program_id(0) == 0)
    def _():
        w1b_ref[...] = sw1_ref[...].astype(BF16)
        w3b_ref[...] = sw3_ref[...].astype(BF16)
        w2b_ref[...] = sw2_ref[...].astype(BF16)

    lo, hi = _unpack_bf16_pair(hp_ref[...])
    lo = lo.astype(BF16)
    hi = hi.astype(BF16)
    a = _dot(lo, w1b_ref[0:HALF_D, :]) + _dot(hi, w1b_ref[HALF_D:, :])
    b = _dot(lo, w3b_ref[0:HALF_D, :]) + _dot(hi, w3b_ref[HALF_D:, :])
    shared = _dot((_silu(a) * b).astype(BF16), w2b_ref[...])
    acc_lo = shared[:, :HALF_D]
    acc_hi = shared[:, HALF_D:]
    gates = gate_ref[...]
    for k in range(TOP_K):
        ylo, yhi = _unpack_bf16_pair(yp_ref[k])
        gk = gates[:, k:k + 1]
        acc_lo = acc_lo + gk * ylo
        acc_hi = acc_hi + gk * yhi
    moe = jnp.concatenate([acc_lo, acc_hi], axis=1)
    y = DEEPNORM_ALPHA * x_ref[...] + mod_ref[0, 5:6, :] * moe
    o_ref[...] = _ln_plain(y, LN_EPS) * g_ref[...] + b_ref[...]


def _combine(yp, gate8, hp, sw1, sw3, sw2, x1, mod_l, g2, b2, lay, tm):
    n_tiles = lay.n // tm
    row = lambda i: (i, 0)
    const = lambda i: (0, 0)
    return pl.pallas_call(
        _combine_kernel,
        grid=(n_tiles,),
        in_specs=[
            pl.BlockSpec((TOP_K, tm, HALF_D), lambda i: (0, i, 0)),
            pl.BlockSpec((tm, V7X_LANES), row),
            pl.BlockSpec((tm, HALF_D), row),
            pl.BlockSpec((D_MODEL, EXPERT_FF), const),
            pl.BlockSpec((D_MODEL, EXPERT_FF), const),
            pl.BlockSpec((EXPERT_FF, D_MODEL), const),
            pl.BlockSpec((tm, D_MODEL), row),
            pl.BlockSpec((1, N_MOD, D_MODEL), lambda i: (lay.cond_row(i, tm), 0, 0)),
            pl.BlockSpec((1, D_MODEL), const),
            pl.BlockSpec((1, D_MODEL), const),
        ],
        out_specs=pl.BlockSpec((tm, D_MODEL), row),
        out_shape=jax.ShapeDtypeStruct((lay.n, D_MODEL), F32),
        scratch_shapes=[
            pltpu.VMEM((D_MODEL, EXPERT_FF), BF16),
            pltpu.VMEM((D_MODEL, EXPERT_FF), BF16),
            pltpu.VMEM((EXPERT_FF, D_MODEL), BF16),
        ],
        compiler_params=_cparams(("arbitrary",)),
        name="combine_norm",
    )(yp, gate8, hp, sw1, sw3, sw2, x1, mod_l, g2, b2)


def _moe(hp, meta, gate8, counts, w1, w3, w2, sw1, sw3, sw2, x1, mod_l, g2, b2, lay):
    n = lay.n
    r_max = n * TOP_K + N_EXPERTS * EXPERT_TILE
    n_tiles = r_max // EXPERT_TILE
    cnt = counts.reshape(N_EXPERTS).astype(I32)
    padded = ((cnt + EXPERT_TILE - 1) // EXPERT_TILE) * EXPERT_TILE
    ends = jnp.cumsum(padded)
    offsets = ends - padded
    idx8 = meta[:, :TOP_K]
    rank8 = meta[:, META_RANK_LANE:META_RANK_LANE + TOP_K]
    base8 = jnp.sum(jnp.where(idx8[:, :, None] == jnp.arange(N_EXPERTS, dtype=I32), offsets, 0), axis=-1)
    pos = (base8 + rank8).astype(I32)
    tile_start = jnp.arange(n_tiles, dtype=I32) * EXPERT_TILE
    tile_expert = jnp.minimum(jnp.sum(tile_start[:, None] >= ends[None, :], axis=1), N_EXPERTS - 1).astype(I32)
    n_active = (ends[-1] // EXPERT_TILE).astype(I32).reshape(1)

    pos_b = pos.reshape(n // SC_BATCH, SC_BATCH, TOP_K).transpose(0, 2, 1)
    xs = _sc_scatter_rows(hp, pos_b, r_max)
    ys = _experts(xs, tile_expert, n_active, w1, w3, w2, EXPERT_TILE)
    yp = _sc_gather_rows(ys, pos.T.reshape(n * TOP_K)).reshape(TOP_K, n, HALF_D)
    return _combine(yp, gate8, hp, sw1, sw3, sw2, x1, mod_l, g2, b2, lay, TOKEN_TILE)


def kernel(x_prompt, x_sample, cache_k, cache_v, state_hgrn, c, c_ctx, w_ada, b_ada, w_in, w_out, attn_sink, hgrn_lb, hgrn_norm, ln1_g, ln1_b, ln2_g, ln2_b, router_w, router_b, moe_w1, moe_w3, moe_w2, shared_w1, shared_w3, shared_w2):
    b_ctx, t_ctx, _ = x_prompt.shape
    b_lat, t_lat, _ = x_sample.shape
    past = cache_k.shape[2]
    lay = _Layout(b_ctx, t_ctx, b_lat, t_lat)
    tm = TOKEN_TILE
    assert 1 + b_lat <= COND_ROWS
    assert lay.n_ctx % tm == 0 and lay.t_lat % tm == 0 and lay.n_ctx % lay.t_lat == 0

    x = jnp.concatenate([x_prompt.reshape(lay.n_ctx, D_MODEL), x_sample.reshape(lay.n_lat, D_MODEL)], axis=0)
    cond = jnp.concatenate([c_ctx[None, :], c, jnp.zeros((COND_ROWS - 1 - b_lat, D_MODEL), F32)], axis=0)
    mod = _adaln(cond, w_ada, b_ada).reshape(DEPTH, COND_ROWS, N_MOD, D_MODEL)

    lb_all = jnp.cumsum(jax.nn.softmax(hgrn_lb.astype(F32), axis=0), axis=0)
    lb_all = lb_all - lb_all[:1]
    lbp = jnp.stack([jnp.log(lb_all), jnp.log1p(-lb_all), 1.0 - lb_all], axis=2)

    cos_t, sin_t = _rope_tables(lay, tm)
    zero_state = jnp.zeros((b_ctx, 2, HGRN_HEADS, HGRN_DK, HGRN_DK), F32)

    ks_out, vs_out, ss_out = [], [], []
    for l in range(DEPTH):
        proj = _inproj(x, mod[l], w_in[l], cos_t, sin_t, lay, tm)
        ks_out.append(proj[:lay.n_ctx, COL_K:COL_K + KV_WIDTH].reshape(b_ctx, t_ctx, N_KV_HEADS, HEAD_DIM))
        vs_out.append(proj[:lay.n_ctx, COL_V:COL_V + KV_WIDTH].reshape(b_ctx, t_ctx, N_KV_HEADS, HEAD_DIM))
        sink_l = attn_sink[l].reshape(1, N_HEADS)
        attn_c = _attn_context(proj, sink_l, lay)
        attn_l = _attn_latent(proj, cache_k[:, l].reshape(b_lat, past, KV_WIDTH),
                              cache_v[:, l].reshape(b_lat, past, KV_WIDTH), sink_l, lay)
        four_c = _fourier(proj, 0, b_ctx, t_ctx, t_ctx, "fourier_ctx")
        four_l = _fourier(proj, lay.n_ctx, b_lat, t_lat, min(t_lat, 512), "fourier_lat")
        gn_row = jnp.tile(hgrn_norm[l], HGRN_HEADS).reshape(1, HGRN_WIDTH)
        rec_c, s_fin = _hgrn(proj, 0, b_ctx, t_ctx, lbp[l], gn_row, zero_state, "hgrn_ctx")
        s0t = jnp.swapaxes(state_hgrn[:, l].astype(F32), -1, -2)
        rec_l, _ = _hgrn(proj, lay.n_ctx, b_lat, t_lat, lbp[l], gn_row, s0t, "hgrn_lat")
        ss_out.append(jnp.swapaxes(s_fin, -1, -2))

        attn = jnp.concatenate([attn_c, attn_l], axis=0)
        four = jnp.concatenate([four_c, four_l], axis=0)
        rec = jnp.concatenate([rec_c, rec_l], axis=0)
        x1, hp, meta, gate8, counts = _outproj(
            attn, four, rec, x, mod[l], w_out[l], ln1_g[l].reshape(1, -1), ln1_b[l].reshape(1, -1),
            router_w[l], router_b[l].reshape(1, -1), lay, tm)
        x = _moe(hp, meta, gate8, counts, moe_w1[l], moe_w3[l], moe_w2[l], shared_w1[l], shared_w3[l],
                 shared_w2[l], x1, mod[l], ln2_g[l].reshape(1, -1), ln2_b[l].reshape(1, -1), lay)

    y_prompt = x[:lay.n_ctx].reshape(b_ctx, t_ctx, D_MODEL)
    y_sample = x[lay.n_ctx:].reshape(b_lat, t_lat, D_MODEL)
    new_cache_k = jnp.stack(ks_out, axis=1)
    new_cache_v = jnp.stack(vs_out, axis=1)
    new_state = jnp.stack(ss_out, axis=1).astype(x_prompt.dtype)
    return (y_prompt, y_sample, new_cache_k, new_cache_v, new_state)
```

```python
import functools
import math

import numpy as np
import jax
import jax.numpy as jnp
from jax import lax
from jax.experimental import pallas as pl
from jax.experimental.pallas import tpu as pltpu
from jax.experimental.pallas import tpu_sc as plsc

F32 = jnp.float32
BF16 = jnp.bfloat16
I32 = jnp.int32

D_MODEL = 1024
HALF_D = D_MODEL // 2
DEPTH = 2
GRID_W = 64
ROPE_BASE = 10000.0
HEAD_DIM = 64
ATTN_WIDTH = 512
N_HEADS = 8
N_KV_HEADS = 2
KV_GROUP = 4
KV_WIDTH = N_KV_HEADS * HEAD_DIM
WINDOW = 128
ATTN_BLOCK = 128
FOURIER_WIDTH = 256
FOURIER_GROUPS = 4
HGRN_WIDTH = 256
HGRN_HEADS = 4
HGRN_DK = 64
HGRN_CHUNK = 64
IN_WIDTH = 2304
N_EXPERTS = 64
TOP_K = 8
EXPERT_FF = 256
ROUTED_SCALE = 2.5
N_MOD = 6
LN_EPS = 1e-5
ADA_EPS = 1e-6
GN_EPS = 1e-6
DEEPNORM_ALPHA = (2 * DEPTH) ** 0.25

COL_Q = 0
COL_K = 512
COL_V = 640
COL_U = 768
COL_HQ = 1024
COL_FF = 1280
COL_FB = 1536
COL_HI = 1792
COL_HG = 2048
ROPE_COLS = COL_V

V7X_LANES = 128
COND_ROWS = 16
NEG_BIG = -1e30
TOKEN_TILE = 512
EXPERT_TILE = 512
SC_BATCH = 128

VMEM_LIMIT = 56 * 1024 * 1024


def _cparams(sem):
    return pltpu.CompilerParams(dimension_semantics=sem, vmem_limit_bytes=VMEM_LIMIT)


def _dot(a, b):
    return jnp.dot(a, b, preferred_element_type=F32)


def _dot_nt(a, b):
    return lax.dot_general(a, b, (((1,), (1,)), ((), ())), preferred_element_type=F32)


def _dot_tn(a, b):
    return lax.dot_general(a, b, (((0,), (0,)), ((), ())), preferred_element_type=F32)


def _split3(x):
    hi = x.astype(BF16)
    r1 = x - hi.astype(F32)
    mid = r1.astype(BF16)
    lo = (r1 - mid.astype(F32)).astype(BF16)
    return hi, mid, lo


def _dot_exact_lhs(m_bf16, x):
    hi, mid, lo = _split3(x)
    return _dot(m_bf16, hi) + _dot(m_bf16, mid) + _dot(m_bf16, lo)


def _dot_exact_rhs(x, m_bf16):
    hi, mid, lo = _split3(x)
    return _dot(hi, m_bf16) + _dot(mid, m_bf16) + _dot(lo, m_bf16)


def _dot_hp(a, b):
    a_hi = a.astype(BF16)
    a_lo = (a - a_hi.astype(F32)).astype(BF16)
    b_hi = b.astype(BF16)
    b_lo = (b - b_hi.astype(F32)).astype(BF16)
    return _dot(a_hi, b_hi) + _dot(a_hi, b_lo) + _dot(a_lo, b_hi)


def _pack_bf16_pair(lo, hi):
    def rne(x):
        b = lax.bitcast_convert_type(x, I32)
        return b + 0x7FFF + (lax.shift_right_logical(b, 16) & 1)
    return lax.shift_right_logical(rne(lo), 16) | (rne(hi) & -65536)


def _unpack_bf16_pair(w):
    lo = lax.bitcast_convert_type(lax.shift_left(w, 16), F32)
    hi = lax.bitcast_convert_type(w & -65536, F32)
    return lo, hi


def _ln_plain(x, eps):
    mu = jnp.mean(x, axis=-1, keepdims=True)
    xc = x - mu
    var = jnp.mean(xc * xc, axis=-1, keepdims=True)
    return xc * lax.rsqrt(var + eps)


def _silu(x):
    return x * jax.nn.sigmoid(x)


def _adaln_kernel(c_ref, w_ref, b_ref, o_ref):
    s = _silu(c_ref[...])
    o_ref[0] = _dot_hp(s, w_ref[0]) + b_ref[0]


def _adaln(cond, w_ada, b_ada):
    return pl.pallas_call(
        _adaln_kernel,
        grid=(DEPTH, N_MOD),
        in_specs=[
            pl.BlockSpec((COND_ROWS, D_MODEL), lambda l, j: (0, 0)),
            pl.BlockSpec((1, D_MODEL, D_MODEL), lambda l, j: (l, 0, j)),
            pl.BlockSpec((1, 1, D_MODEL), lambda l, j: (l, 0, j)),
        ],
        out_specs=pl.BlockSpec((1, COND_ROWS, D_MODEL), lambda l, j: (l, 0, j)),
        out_shape=jax.ShapeDtypeStruct((DEPTH, COND_ROWS, N_MOD * D_MODEL), F32),
        compiler_params=_cparams(("arbitrary", "arbitrary")),
        name="adaln",
    )(cond, w_ada, b_ada.reshape(DEPTH, 1, N_MOD * D_MODEL))


class _Layout:
    def __init__(self, b_ctx, t_ctx, b_lat, t_lat):
        self.b_ctx, self.t_ctx, self.b_lat, self.t_lat = b_ctx, t_ctx, b_lat, t_lat
        self.n_ctx = b_ctx * t_ctx
        self.n_lat = b_lat * t_lat
        self.n = self.n_ctx + self.n_lat

    def cond_row(self, tile, tm):
        n_ctx_tiles = self.n_ctx // tm
        per_batch = self.t_lat // tm
        return jnp.where(tile < n_ctx_tiles, 0, 1 + (tile - n_ctx_tiles) // per_batch)


def _group_specs(n_arrays, tm, width, n_ctx_tiles):
    if n_arrays == 1:
        return [pl.BlockSpec((tm, width), lambda i: (i, 0))]
    return [pl.BlockSpec((tm, width), lambda i: (jnp.minimum(i, n_ctx_tiles - 1), 0)),
            pl.BlockSpec((tm, width), lambda i: (jnp.maximum(i - n_ctx_tiles, 0), 0))]


def _inproj_kernel(*refs, n_ctx_tiles):
    xs = refs[:-6]
    mod_ref, w_ref, cos_ref, sin_ref, o_ref, wb_ref = refs[-6:]

    @pl.when(pl.program_id(0) == 0)
    def _():
        wb_ref[...] = w_ref[...].astype(BF16)

    if len(xs) == 2:
        x = jnp.where(pl.program_id(0) < n_ctx_tiles, xs[0][...], xs[1][...])
    else:
        x = xs[0][...]
    shift = mod_ref[0, 0:1, :]
    scale = mod_ref[0, 1:2, :]
    h = (_ln_plain(x, ADA_EPS) * (1.0 + scale) + shift).astype(BF16)
    p = _dot(h, wb_ref[...])
    cos = cos_ref[...]
    sin = sin_ref[...]
    lane = lax.broadcasted_iota(I32, cos.shape, 1)
    first_half = (lane & 31) < 16
    for cb in range(ROPE_COLS // V7X_LANES):
        seg = p[:, cb * V7X_LANES:(cb + 1) * V7X_LANES]
        partner = jnp.where(first_half, pltpu.roll(seg, V7X_LANES - 16, 1), pltpu.roll(seg, 16, 1))
        o_ref[:, cb * V7X_LANES:(cb + 1) * V7X_LANES] = seg * cos + partner * sin
    o_ref[:, ROPE_COLS:] = p[:, ROPE_COLS:]


def _rope_tables(lay, tm):
    t = lay.t_lat
    pos = jnp.arange(t)
    row = (pos // GRID_W).astype(F32)
    col = (pos % GRID_W).astype(F32)
    n_freq = HEAD_DIM // 4
    inv = ROPE_BASE ** (-jnp.arange(n_freq, dtype=F32) / n_freq)
    ang_r = row[:, None] * inv
    ang_c = col[:, None] * inv
    ang = jnp.concatenate([ang_r, ang_r, ang_c, ang_c], axis=1)
    sign = jnp.concatenate([-jnp.ones(n_freq), jnp.ones(n_freq), -jnp.ones(n_freq), jnp.ones(n_freq)]).astype(F32)
    cos = jnp.cos(ang)
    sin = jnp.sin(ang) * sign
    cos = jnp.concatenate([jnp.ones((tm, HEAD_DIM), F32), cos], axis=0)
    sin = jnp.concatenate([jnp.zeros((tm, HEAD_DIM), F32), sin], axis=0)
    return jnp.tile(cos, (1, 2)), jnp.tile(sin, (1, 2))


def _inproj(x, mod_l, w_in_l, cos_t, sin_t, lay, tm):
    n_tiles = lay.n // tm
    n_ctx_tiles = lay.n_ctx // tm
    per_batch = lay.t_lat // tm

    def tbl(i):
        return jnp.where(i < n_ctx_tiles, 0, 1 + (i - n_ctx_tiles) % per_batch)

    xs = x if isinstance(x, tuple) else (x,)
    return pl.pallas_call(
        functools.partial(_inproj_kernel, n_ctx_tiles=n_ctx_tiles),
        grid=(n_tiles,),
        in_specs=[
            *_group_specs(len(xs), tm, D_MODEL, n_ctx_tiles),
            pl.BlockSpec((1, N_MOD, D_MODEL), lambda i: (lay.cond_row(i, tm), 0, 0)),
            pl.BlockSpec((D_MODEL, IN_WIDTH), lambda i: (0, 0), pipeline_mode=pl.Buffered(1)),
            pl.BlockSpec((tm, V7X_LANES), lambda i: (tbl(i), 0)),
            pl.BlockSpec((tm, V7X_LANES), lambda i: (tbl(i), 0)),
        ],
        out_specs=pl.BlockSpec((tm, IN_WIDTH), lambda i: (i, 0)),
        out_shape=jax.ShapeDtypeStruct((lay.n, IN_WIDTH), F32),
        scratch_shapes=[pltpu.VMEM((D_MODEL, IN_WIDTH), BF16)],
        compiler_params=_cparams(("arbitrary",)),
        name="inproj",
    )(*xs, mod_l, w_in_l, cos_t, sin_t)


def _attn_kernel(sink_ref, q_ref, *refs, n_local, has_ctx, t_total):
    o_ref = refs[-1]
    k_refs = refs[:n_local]
    v_refs = refs[n_local:2 * n_local]
    tq = q_ref.shape[0]
    scale = HEAD_DIM ** -0.5
    k_parts = [kr[...] for kr in k_refs]
    v_parts = [vr[...] for vr in v_refs]
    if has_ctx:
        k_parts.append(refs[2 * n_local][0])
        v_parts.append(refs[2 * n_local + 1][0])
    kall = jnp.concatenate(k_parts, axis=0) if len(k_parts) > 1 else k_parts[0]
    vall = jnp.concatenate(v_parts, axis=0) if len(v_parts) > 1 else v_parts[0]
    nk = kall.shape[0]
    k_sw = pltpu.roll(kall, HEAD_DIM, 1)
    v_sw = pltpu.roll(vall, HEAD_DIM, 1)
    lo_half = lax.broadcasted_iota(I32, (1, V7X_LANES), 1) < HEAD_DIM
    er = jnp.where(lax.broadcasted_iota(I32, (2 * nk, V7X_LANES), 0) < nk, 0, 1)
    el = jnp.where(lax.broadcasted_iota(I32, (2 * nk, V7X_LANES), 1) < HEAD_DIM, 0, 1)
    ones_blk = jnp.where(er == el, 1.0, 0.0).astype(BF16)
    rows = 2 * tq
    if n_local > 1:
        i = pl.program_id(1)
        r = lax.broadcasted_iota(I32, (rows, nk), 0) & (tq - 1)
        c = lax.broadcasted_iota(I32, (rows, nk), 1)
        kpos = (i - 1) * tq + c
        rel = c - tq - r
        local = (jnp.abs(rel) <= WINDOW) & (kpos >= 0) & (kpos < t_total)
        valid = jnp.logical_or(local, c >= n_local * tq)
    else:
        valid = None
    for g in range(N_KV_HEADS):
        k_own, k_oth = (kall, k_sw) if g == 0 else (k_sw, kall)
        v_own, v_oth = (vall, v_sw) if g == 0 else (v_sw, vall)
        k2 = jnp.concatenate([jnp.where(lo_half, k_own, 0.0), jnp.where(lo_half, 0.0, k_oth)], axis=0).astype(BF16)
        v2 = jnp.concatenate([jnp.where(lo_half, v_own, 0.0), jnp.where(lo_half, 0.0, v_oth)], axis=0).astype(BF16)
        v2e = jnp.concatenate([v2, ones_blk], axis=1)
        pairs = [2 * g, 2 * g + 1]
        qq = jnp.concatenate([q_ref[:, p * V7X_LANES:(p + 1) * V7X_LANES] for p in pairs], axis=0)
        qq = (qq * scale).astype(BF16)
        sink_a = jnp.concatenate([jnp.full((tq, 1), sink_ref[0, 2 * p], F32) for p in pairs], axis=0)
        sink_b = jnp.concatenate([jnp.full((tq, 1), sink_ref[0, 2 * p + 1], F32) for p in pairs], axis=0)
        s = _dot_nt(qq, k2)
        s_a = s[:, :nk]
        s_b = s[:, nk:]
        if valid is not None:
            s_a = jnp.where(valid, s_a, NEG_BIG)
            s_b = jnp.where(valid, s_b, NEG_BIG)
        m_a = jnp.maximum(jnp.max(s_a, axis=1, keepdims=True), sink_a)
        m_b = jnp.maximum(jnp.max(s_b, axis=1, keepdims=True), sink_b)
        pe = jnp.concatenate([jnp.exp(s_a - m_a).astype(BF16), jnp.exp(s_b - m_b).astype(BF16)], axis=1)
        acc = _dot(pe, v2e)
        sink_term = jnp.where(lo_half, jnp.exp(sink_a - m_a), jnp.exp(sink_b - m_b))
        o = acc[:, :V7X_LANES] / (acc[:, V7X_LANES:] + sink_term)
        for j, p in enumerate(pairs):
            o_ref[:, p * V7X_LANES:(p + 1) * V7X_LANES] = o[j * tq:(j + 1) * tq].astype(o_ref.dtype)


def _attn_context(proj, sink_l, lay):
    t = lay.t_ctx
    kb, vb = COL_K // KV_WIDTH, COL_V // KV_WIDTH
    body = functools.partial(_attn_kernel, n_local=1, has_ctx=False, t_total=t)
    return pl.pallas_call(
        body,
        grid=(lay.b_ctx,),
        in_specs=[
            pl.BlockSpec(memory_space=pltpu.SMEM),
            pl.BlockSpec((t, ATTN_WIDTH), lambda b: (b, 0)),
            pl.BlockSpec((t, KV_WIDTH), lambda b: (b, kb)),
            pl.BlockSpec((t, KV_WIDTH), lambda b: (b, vb)),
        ],
        out_specs=pl.BlockSpec((t, ATTN_WIDTH), lambda b: (b, 0)),
        out_shape=jax.ShapeDtypeStruct((lay.n_ctx, ATTN_WIDTH), BF16),
        compiler_params=_cparams(("arbitrary",)),
        name="attn_ctx",
    )(sink_l, proj, proj, proj)


def _attn_latent(proj, k_ctx, v_ctx, sink_l, lay):
    t = lay.t_lat
    tq = ATTN_BLOCK
    nq = t // tq
    base = lay.n_ctx // tq
    kb, vb = COL_K // KV_WIDTH, COL_V // KV_WIDTH
    past = k_ctx.shape[1]

    def rows(off):
        return lambda b, i: base + b * nq + jnp.clip(i + off, 0, nq - 1)

    def kv_specs(col):
        return [pl.BlockSpec((tq, KV_WIDTH), (lambda b, i, f=rows(off): (f(b, i), col))) for off in (-1, 0, 1)]

    body = functools.partial(_attn_kernel, n_local=3, has_ctx=True, t_total=t)
    return pl.pallas_call(
        body,
        grid=(lay.b_lat, nq),
        in_specs=[
            pl.BlockSpec(memory_space=pltpu.SMEM),
            pl.BlockSpec((tq, ATTN_WIDTH), lambda b, i: (base + b * nq + i, 0)),
            *kv_specs(kb),
            *kv_specs(vb),
            pl.BlockSpec((1, past, KV_WIDTH), lambda b, i: (b, 0, 0)),
            pl.BlockSpec((1, past, KV_WIDTH), lambda b, i: (b, 0, 0)),
        ],
        out_specs=pl.BlockSpec((tq, ATTN_WIDTH), lambda b, i: (b * nq + i, 0)),
        out_shape=jax.ShapeDtypeStruct((lay.n_lat, ATTN_WIDTH), BF16),
        compiler_params=_cparams(("arbitrary", "arbitrary")),
        name="attn_lat",
    )(sink_l, proj, proj, proj, proj, proj, proj, proj, k_ctx, v_ctx)


def _fourier_kernel(cs_ref, u_ref, cc_ref, sc_ref, o_ref, csb_ref, *, scale):
    @pl.when(pl.program_id(1) == 0)
    def _():
        csb_ref[...] = cs_ref[...].astype(BF16)

    z = u_ref[...].astype(BF16)
    zc = _dot(z, cc_ref[...].astype(BF16)).astype(BF16)
    zs = _dot(z, sc_ref[...].astype(BF16)).astype(BF16)
    zz = jnp.concatenate([zc, zs], axis=0)
    o_ref[...] = (_dot(csb_ref[...], zz) * scale).astype(o_ref.dtype)


@functools.lru_cache(maxsize=None)
def _dft_tables(t):
    idx = np.arange(t, dtype=np.int64)
    ang = 2.0 * np.pi * ((idx[:, None] * idx[None, :]) % t).astype(np.float64) / t
    cs = np.concatenate([np.cos(ang), -np.sin(ang)], axis=1).astype(np.float32)
    cw = FOURIER_WIDTH // FOURIER_GROUPS
    cidx = np.arange(cw, dtype=np.int64)
    cang = 2.0 * np.pi * ((cidx[:, None] * cidx[None, :]) % cw).astype(np.float64) / cw
    eye = np.eye(FOURIER_GROUPS)
    cc = np.kron(eye, np.cos(cang)).astype(np.float32)
    sc = np.kron(eye, np.sin(cang)).astype(np.float32)
    return cs, cc, sc


def _fourier(proj, row0, b, t, tm, name):
    cs, cc, sc = _dft_tables(t)
    cw = FOURIER_WIDTH // FOURIER_GROUPS
    nt = t // tm
    ub = COL_U // FOURIER_WIDTH
    base = row0 // t
    body = functools.partial(_fourier_kernel, scale=1.0 / math.sqrt(t * cw))
    return pl.pallas_call(
        body,
        grid=(nt, b),
        in_specs=[
            pl.BlockSpec((tm, 2 * t), lambda i, bb: (i, 0)),
            pl.BlockSpec((t, FOURIER_WIDTH), lambda i, bb: (base + bb, ub)),
            pl.BlockSpec((FOURIER_WIDTH, FOURIER_WIDTH), lambda i, bb: (0, 0)),
            pl.BlockSpec((FOURIER_WIDTH, FOURIER_WIDTH), lambda i, bb: (0, 0)),
        ],
        out_specs=pl.BlockSpec((tm, FOURIER_WIDTH), lambda i, bb: (bb * nt + i, 0)),
        out_shape=jax.ShapeDtypeStruct((b * t, FOURIER_WIDTH), BF16),
        scratch_shapes=[pltpu.VMEM((tm, 2 * t), BF16)],
        compiler_params=_cparams(("arbitrary", "arbitrary")),
        name=name,
    )(jnp.asarray(cs), proj, jnp.asarray(cc), jnp.asarray(sc))


HGRN_LEVELS = (64, 32, 16, 8, 4, 2)


@functools.lru_cache(maxsize=None)
def _hgrn_tables():
    c = HGRN_CHUNK
    return np.stack([np.tril(np.ones((c, c))), np.triu(np.ones((c, c)))]).astype(np.float32)


def _boundary_rows(b, m, reverse):
    c, w = b.shape
    half = m // 2
    off = half if reverse else half - 1
    if m >= 16:
        return jnp.concatenate(
            [jnp.broadcast_to(b[s + off:s + off + 1], (m, w)) for s in range(0, c, m)], axis=0)
    sub = lax.broadcasted_iota(I32, (c, w), 0) & 7
    b3 = b.reshape(c // 8, 8, w)

    def bcast(j):
        return jnp.broadcast_to(b3[:, j:j + 1, :], (c // 8, 8, w)).reshape(c, w)

    if m == 8:
        return bcast(off)
    if m == 4:
        return jnp.where(sub < 4, bcast(off), bcast(4 + off))
    assert m == 2
    if reverse:
        return jnp.where((sub & 1) == 1, b, pltpu.roll(b, c - 1, 0))
    return jnp.where((sub & 1) == 0, b, pltpu.roll(b, 1, 0))


def _hgrn_chunk(q, z, v, loglb, log1mlb, onemlb, m_all, st_ref, d, reverse):
    c = HGRN_CHUNK
    log_sig = jnp.minimum(z, 0.0) - jnp.log1p(jnp.exp(-jnp.abs(z)))
    bb = log1mlb + log_sig
    mx = jnp.maximum(loglb, bb)
    lf = mx + jnp.log1p(jnp.exp(-jnp.abs(loglb - bb)))
    kk = onemlb * jax.nn.sigmoid(-z)

    b = _dot_exact_lhs(m_all, lf)
    b_end = b[0:1] if reverse else b[c - 1:c]
    row = lax.broadcasted_iota(I32, (c, 1), 0)
    ti = lax.broadcasted_iota(I32, (c, c), 0)
    si = lax.broadcasted_iota(I32, (c, c), 1)

    qt = (q * jnp.exp(b)).astype(BF16)
    kt = (kk * jnp.exp(b_end - b)).astype(BF16)
    decay = jnp.exp(b_end)
    vb = v.astype(BF16)

    qf, kf, masks = [], [], []
    for li, m in enumerate(HGRN_LEVELS):
        r = _boundary_rows(b, m, reverse)
        upper = (row & (m - 1)) >= (m // 2)
        q_side = jnp.logical_not(upper) if reverse else upper
        e = jnp.exp(jnp.where(q_side, b - r, r - b))
        qf.append(jnp.where(q_side, q * e, 0.0).astype(BF16))
        kf.append(jnp.where(q_side, 0.0, kk * e).astype(BF16))
        masks.append((ti & -m) == (si & -m))
    qb = q.astype(BF16)
    kb = kk.astype(BF16)
    eye = ti == si

    outs = []
    for h in range(HGRN_HEADS):
        sl = slice(h * HGRN_DK, (h + 1) * HGRN_DK)
        a = jnp.where(eye, _dot_nt(qb[:, sl], kb[:, sl]), 0.0)
        for li in range(len(HGRN_LEVELS)):
            a = a + jnp.where(masks[li], _dot_nt(qf[li][:, sl], kf[li][:, sl]), 0.0)
        st = st_ref[d, h]
        o = _dot_nt(qt[:, sl], st.astype(BF16)) + _dot(a.astype(BF16), vb[:, sl])
        st_ref[d, h] = st * decay[:, sl] + _dot_tn(vb[:, sl], kt[:, sl])
        outs.append(o)
    return jnp.concatenate(outs, axis=1)


def _hgrn_kernel(hq_ref, ff_ref, fb_ref, hi_ref, hg_ref, lbp_ref, gn_ref, mall_ref, ones_ref, s0_ref,
                 rec_ref, sfin_ref, st_ref, of_ref, ob_ref, *, t):
    c = HGRN_CHUNK
    n = t // c
    st_ref[...] = s0_ref[0]

    def body(ci, carry):
        rf = pl.ds(pl.multiple_of(ci * c, c), c)
        rb = pl.ds(pl.multiple_of((n - 1 - ci) * c, c), c)
        of_ref[rf, :] = _hgrn_chunk(hq_ref[rf, :], ff_ref[rf, :], hi_ref[rf, :], lbp_ref[0, 0:1, :],
                                    lbp_ref[0, 1:2, :], lbp_ref[0, 2:3, :], mall_ref[0].astype(BF16),
                                    st_ref, 0, False)
        ob_ref[rb, :] = _hgrn_chunk(hq_ref[rb, :], fb_ref[rb, :], hi_ref[rb, :], lbp_ref[1, 0:1, :],
                                    lbp_ref[1, 1:2, :], lbp_ref[1, 2:3, :], mall_ref[1].astype(BF16),
                                    st_ref, 1, True)
        return carry

    lax.fori_loop(0, n, body, 0)
    sfin_ref[0] = st_ref[...]
    o = of_ref[...] + ob_ref[...]
    ms = _dot_exact_rhs(o * o, ones_ref[...].astype(BF16)) * (1.0 / HGRN_DK)
    o = o * lax.rsqrt(ms + GN_EPS) * gn_ref[...]
    rec_ref[...] = (o * _silu(hg_ref[...])).astype(rec_ref.dtype)


def _hgrn(proj, row0, b, t, lbp, gn_row, s0t, name):
    base = row0 // t
    m_all = jnp.asarray(_hgrn_tables())
    ones_bd = jnp.asarray(np.kron(np.eye(HGRN_HEADS), np.ones((HGRN_DK, HGRN_DK))).astype(np.float32))

    def col(cstart):
        return pl.BlockSpec((t, HGRN_WIDTH), lambda bb, cb=cstart // HGRN_WIDTH: (base + bb, cb))

    const2 = lambda bb: (0, 0)
    const3 = lambda bb: (0, 0, 0)
    st_shape = (2, HGRN_HEADS, HGRN_DK, HGRN_DK)
    body = functools.partial(_hgrn_kernel, t=t)
    return pl.pallas_call(
        body,
        grid=(b,),
        in_specs=[
            col(COL_HQ), col(COL_FF), col(COL_FB), col(COL_HI), col(COL_HG),
            pl.BlockSpec((2, 3, HGRN_WIDTH), const3),
            pl.BlockSpec((1, HGRN_WIDTH), const2),
            pl.BlockSpec(m_all.shape, const3),
            pl.BlockSpec(ones_bd.shape, const2),
            pl.BlockSpec((1,) + st_shape, lambda bb: (bb, 0, 0, 0, 0)),
        ],
        out_specs=[
            pl.BlockSpec((t, HGRN_WIDTH), lambda bb: (bb, 0)),
            pl.BlockSpec((1,) + st_shape, lambda bb: (bb, 0, 0, 0, 0)),
        ],
        out_shape=[
            jax.ShapeDtypeStruct((b * t, HGRN_WIDTH), BF16),
            jax.ShapeDtypeStruct((b,) + st_shape, F32),
        ],
        scratch_shapes=[
            pltpu.VMEM(st_shape, F32),
            pltpu.VMEM((t, HGRN_WIDTH), F32),
            pltpu.VMEM((t, HGRN_WIDTH), F32),
        ],
        compiler_params=_cparams(("arbitrary",)),
        name=name,
    )(proj, proj, proj, proj, proj, lbp, gn_row, m_all, ones_bd, s0t)


def _outproj_kernel(attn_c_ref, attn_l_ref, four_c_ref, four_l_ref, rec_c_ref, rec_l_ref, *refs, n_ctx_tiles):
    xs = refs[:-14]
    (mod_ref, w_ref, g_ref, b_ref, rw_ref, rb_ref, x1_ref, hp_ref, meta_ref, gate_ref, cnt_ref, wb_ref, tri_ref,
     run_ref) = refs[-14:]
    tm = x1_ref.shape[0]
    is_ctx = pl.program_id(0) < n_ctx_tiles
    x_in = jnp.where(is_ctx, xs[0][...], xs[1][...]) if len(xs) == 2 else xs[0][...]
    attn = jnp.where(is_ctx, attn_c_ref[...], attn_l_ref[...])
    four = jnp.where(is_ctx, four_c_ref[...], four_l_ref[...])
    rec = jnp.where(is_ctx, rec_c_ref[...], rec_l_ref[...])

    @pl.when(pl.program_id(0) == 0)
    def _():
        wb_ref[...] = w_ref[...].astype(BF16)
        r = lax.broadcasted_iota(I32, (tm, tm), 0)
        c = lax.broadcasted_iota(I32, (tm, tm), 1)
        tri_ref[...] = jnp.where(r < c, 1.0, 0.0).astype(BF16)
        run_ref[...] = jnp.zeros_like(run_ref)

    out = _dot(attn, wb_ref[0:ATTN_WIDTH, :])
    out = out + _dot(four, wb_ref[ATTN_WIDTH:ATTN_WIDTH + FOURIER_WIDTH, :])
    out = out + _dot(rec, wb_ref[ATTN_WIDTH + FOURIER_WIDTH:, :])
    gate1 = mod_ref[0, 2:3, :]
    y = DEEPNORM_ALPHA * x_in + gate1 * out
    x1 = _ln_plain(y, LN_EPS) * g_ref[...] + b_ref[...]
    x1_ref[...] = x1
    h2 = _ln_plain(x1, ADA_EPS) * (1.0 + mod_ref[0, 4:5, :]) + mod_ref[0, 3:4, :]
    hp_ref[...] = _pack_bf16_pair(h2[:, :HALF_D], h2[:, HALF_D:])

    h_hi = h2.astype(BF16)
    h_lo = (h2 - h_hi.astype(F32)).astype(BF16)
    rwt = rw_ref[...]
    w_hi = rwt.astype(BF16)
    w_lo = (rwt - w_hi.astype(F32)).astype(BF16)
    scores = jax.nn.sigmoid(_dot_nt(w_hi, h_hi) + _dot_nt(w_hi, h_lo) + _dot_nt(w_lo, h_hi))
    remaining = scores + rb_ref[...]
    eidx = lax.broadcasted_iota(I32, scores.shape, 0).astype(F32)
    chosen = jnp.zeros(scores.shape, jnp.bool_)
    picks = []
    for _ in range(TOP_K):
        mx = jnp.max(remaining, axis=0, keepdims=True)
        first = jnp.min(jnp.where(remaining == mx, eidx, float(N_EXPERTS)), axis=0, keepdims=True)
        pick = eidx == first
        picks.append((pick, first))
        chosen = jnp.logical_or(chosen, pick)
        remaining = jnp.where(pick, -jnp.inf, remaining)
    sel = jnp.where(chosen, scores, 0.0)
    gates = sel / jnp.sum(sel, axis=0, keepdims=True) * ROUTED_SCALE

    onehot = jnp.where(chosen, 1.0, 0.0)
    rank = run_ref[...] + _dot(onehot.astype(BF16), tri_ref[...])
    run_ref[...] += jnp.sum(onehot, axis=1, keepdims=True)
    cnt_ref[...] = run_ref[...]

    ids, rks, gks = [], [], []
    for pick, first in picks:
        ids.append(first.astype(I32))
        rks.append(jnp.sum(jnp.where(pick, rank, 0.0), axis=0, keepdims=True).astype(I32))
        gks.append(jnp.sum(jnp.where(pick, gates, 0.0), axis=0, keepdims=True))
    meta_ref[...] = jnp.concatenate(ids + rks, axis=0)
    gate_ref[...] = jnp.concatenate(gks, axis=0)


def _outproj(attn, four, rec, x, mod_l, w_out_l, g1, b1, rw, rb, lay, tm):
    n_tiles = lay.n // tm
    n_ctx_tiles = lay.n_ctx // tm
    row = lambda i: (i, 0)
    const = lambda i: (0, 0)
    xs = x if isinstance(x, tuple) else (x,)
    return pl.pallas_call(
        functools.partial(_outproj_kernel, n_ctx_tiles=n_ctx_tiles),
        grid=(n_tiles,),
        in_specs=[
            *_group_specs(2, tm, ATTN_WIDTH, n_ctx_tiles),
            *_group_specs(2, tm, FOURIER_WIDTH, n_ctx_tiles),
            *_group_specs(2, tm, HGRN_WIDTH, n_ctx_tiles),
            *_group_specs(len(xs), tm, D_MODEL, n_ctx_tiles),
            pl.BlockSpec((1, N_MOD, D_MODEL), lambda i: (lay.cond_row(i, tm), 0, 0)),
            pl.BlockSpec((D_MODEL, D_MODEL), const),
            pl.BlockSpec((1, D_MODEL), const),
            pl.BlockSpec((1, D_MODEL), const),
            pl.BlockSpec((N_EXPERTS, D_MODEL), const),
            pl.BlockSpec((N_EXPERTS, 1), const),
        ],
        out_specs=[
            pl.BlockSpec((tm, D_MODEL), row),
            pl.BlockSpec((tm, HALF_D), row),
            pl.BlockSpec((2 * TOP_K, tm), lambda i: (0, i)),
            pl.BlockSpec((TOP_K, tm), lambda i: (0, i)),
            pl.BlockSpec((N_EXPERTS, 1), const),
        ],
        out_shape=[
            jax.ShapeDtypeStruct((lay.n, D_MODEL), F32),
            jax.ShapeDtypeStruct((lay.n, HALF_D), I32),
            jax.ShapeDtypeStruct((2 * TOP_K, lay.n), I32),
            jax.ShapeDtypeStruct((TOP_K, lay.n), F32),
            jax.ShapeDtypeStruct((N_EXPERTS, 1), F32),
        ],
        scratch_shapes=[
            pltpu.VMEM((D_MODEL, D_MODEL), BF16),
            pltpu.VMEM((tm, tm), BF16),
            pltpu.VMEM((N_EXPERTS, 1), F32),
        ],
        compiler_params=_cparams(("arbitrary",)),
        name="outproj_router",
    )(*attn, *four, *rec, *xs, mod_l, w_out_l, g1, b1, rw, rb)


def _sc_workers():
    info = plsc.get_sparse_core_info()
    return info.num_cores, info.num_cores * info.num_subcores


def _sc_scatter_rows(rows, pos_b, r_out):
    nc, nw = _sc_workers()
    n, w = rows.shape
    nbt, copies, _ = pos_b.shape
    assert nbt * SC_BATCH == n and nbt % nw == 0
    per_w = nbt // nw
    mesh = plsc.VectorSubcoreMesh(core_axis_name="c", subcore_axis_name="s")

    @functools.partial(
        pl.kernel, mesh=mesh, out_type=jax.ShapeDtypeStruct((r_out, w), rows.dtype),
        scratch_types=[pltpu.VMEM((copies, SC_BATCH), I32), pltpu.VMEM((SC_BATCH, w), rows.dtype),
                       pltpu.SemaphoreType.DMA],
        name="sc_dispatch")
    def k(rows_hbm, pos_hbm, out_hbm, idx_v, rows_v, sem):
        wid = lax.axis_index("s") * nc + lax.axis_index("c")

        @pl.loop(0, per_w)
        def _(j):
            b = wid * per_w + j
            pltpu.sync_copy(pos_hbm.at[b], idx_v)
            pltpu.sync_copy(rows_hbm.at[pl.ds(b * SC_BATCH, SC_BATCH)], rows_v)
            for q in range(copies):
                pltpu.async_copy(rows_v, out_hbm.at[idx_v.at[q]], sem).wait()

    return k(rows, pos_b)


def _sc_gather_rows(table, idx):
    nc, nw = _sc_workers()
    r = idx.shape[0]
    w = table.shape[1]
    assert r % (nw * SC_BATCH) == 0
    per_w = r // nw
    nb = per_w // SC_BATCH
    mesh = plsc.VectorSubcoreMesh(core_axis_name="c", subcore_axis_name="s")

    @functools.partial(
        pl.kernel, mesh=mesh, out_type=jax.ShapeDtypeStruct((r, w), table.dtype),
        scratch_types=[pltpu.VMEM((per_w,), I32), pltpu.VMEM((SC_BATCH, w), table.dtype),
                       pltpu.SemaphoreType.DMA],
        name="sc_combine")
    def k(table_hbm, idx_hbm, out_hbm, idx_v, rows_v, sem):
        wid = lax.axis_index("s") * nc + lax.axis_index("c")
        base = wid * per_w
        pltpu.sync_copy(idx_hbm.at[pl.ds(base, per_w)], idx_v)

        @pl.loop(0, nb)
        def _(j):
            off = j * SC_BATCH
            pltpu.async_copy(table_hbm.at[idx_v.at[pl.ds(off, SC_BATCH)]], rows_v, sem).wait()
            pltpu.sync_copy(rows_v, out_hbm.at[pl.ds(base + off, SC_BATCH)])

    return k(table, idx)


def _experts_kernel(te_ref, na_ref, x_ref, w1_ref, w3_ref, w2_ref, o_ref, w1b_ref, w3b_ref, w2b_ref):
    j = pl.program_id(0)
    na = na_ref[0]
    jj = jnp.minimum(j, na - 1)
    e = te_ref[jj]
    prev = te_ref[jnp.maximum(jj - 1, 0)]
    active = j < na

    @pl.when(jnp.logical_and(active, jnp.logical_or(j == 0, e != prev)))
    def _():
        w1b_ref[...] = w1_ref[0].astype(BF16)
        w3b_ref[...] = w3_ref[0].astype(BF16)
        w2b_ref[...] = w2_ref[0].astype(BF16)

    @pl.when(active)
    def _():
        lo, hi = _unpack_bf16_pair(x_ref[...])
        lo = lo.astype(BF16)
        hi = hi.astype(BF16)
        a = _dot(lo, w1b_ref[0:HALF_D, :]) + _dot(hi, w1b_ref[HALF_D:, :])
        b = _dot(lo, w3b_ref[0:HALF_D, :]) + _dot(hi, w3b_ref[HALF_D:, :])
        y = _dot((_silu(a) * b).astype(BF16), w2b_ref[...])
        o_ref[...] = _pack_bf16_pair(y[:, :HALF_D], y[:, HALF_D:])


def _experts(xs, tile_expert, n_active, w1, w3, w2, tm):
    r = xs.shape[0]
    n_tiles = r // tm

    def xmap(j, te, na):
        return (jnp.minimum(j, na[0] - 1), 0)

    def wmap(j, te, na):
        return (te[jnp.minimum(j, na[0] - 1)], 0, 0)

    grid_spec = pltpu.PrefetchScalarGridSpec(
        num_scalar_prefetch=2,
        grid=(n_tiles,),
        in_specs=[
            pl.BlockSpec((tm, HALF_D), xmap),
            pl.BlockSpec((1, D_MODEL, EXPERT_FF), wmap),
            pl.BlockSpec((1, D_MODEL, EXPERT_FF), wmap),
            pl.BlockSpec((1, EXPERT_FF, D_MODEL), wmap),
        ],
        out_specs=pl.BlockSpec((tm, HALF_D), xmap),
        scratch_shapes=[
            pltpu.VMEM((D_MODEL, EXPERT_FF), BF16),
            pltpu.VMEM((D_MODEL, EXPERT_FF), BF16),
            pltpu.VMEM((EXPERT_FF, D_MODEL), BF16),
        ],
    )
    return pl.pallas_call(
        _experts_kernel,
        grid_spec=grid_spec,
        out_shape=jax.ShapeDtypeStruct((r, HALF_D), I32),
        compiler_params=_cparams(("arbitrary",)),
        name="experts",
    )(tile_expert, n_active, xs, w1, w3, w2)


def _combine_kernel(yp_ref, gate_ref, hp_ref, sw1_ref, sw3_ref, sw2_ref, x_ref, mod_ref, g_ref, b_ref, *refs,
                    n_ctx_tiles):
    outs = refs[:-3]
    w1b_ref, w3b_ref, w2b_ref = refs[-3:]

    @pl.when(pl.program_id(0) == 0)
    def _():
        w1b_ref[...] = sw1_ref[...].astype(BF16)
        w3b_ref[...] = sw3_ref[...].astype(BF16)
        w2b_ref[...] = sw2_ref[...].astype(BF16)

    lo, hi = _unpack_bf16_pair(hp_ref[...])
    lo = lo.astype(BF16)
    hi = hi.astype(BF16)
    a = _dot(lo, w1b_ref[0:HALF_D, :]) + _dot(hi, w1b_ref[HALF_D:, :])
    b = _dot(lo, w3b_ref[0:HALF_D, :]) + _dot(hi, w3b_ref[HALF_D:, :])
    shared = _dot((_silu(a) * b).astype(BF16), w2b_ref[...])
    acc_lo = shared[:, :HALF_D]
    acc_hi = shared[:, HALF_D:]
    gates = gate_ref[...]
    for k in range(TOP_K):
        ylo, yhi = _unpack_bf16_pair(yp_ref[k])
        gk = gates[:, k:k + 1]
        acc_lo = acc_lo + gk * ylo
        acc_hi = acc_hi + gk * yhi
    moe = jnp.concatenate([acc_lo, acc_hi], axis=1)
    y = DEEPNORM_ALPHA * x_ref[...] + mod_ref[0, 5:6, :] * moe
    res = _ln_plain(y, LN_EPS) * g_ref[...] + b_ref[...]
    if len(outs) == 1:
        outs[0][...] = res
    else:
        @pl.when(pl.program_id(0) < n_ctx_tiles)
        def _():
            outs[0][...] = res

        @pl.when(pl.program_id(0) >= n_ctx_tiles)
        def _():
            outs[1][...] = res


def _combine(yp, gate8, hp, sw1, sw3, sw2, x1, mod_l, g2, b2, lay, tm, split_out):
    n_tiles = lay.n // tm
    n_ctx_tiles = lay.n_ctx // tm
    row = lambda i: (i, 0)
    const = lambda i: (0, 0)
    if split_out:
        out_specs = _group_specs(2, tm, D_MODEL, n_ctx_tiles)
        out_shape = [jax.ShapeDtypeStruct((lay.n_ctx, D_MODEL), F32), jax.ShapeDtypeStruct((lay.n_lat, D_MODEL), F32)]
    else:
        out_specs = pl.BlockSpec((tm, D_MODEL), row)
        out_shape = jax.ShapeDtypeStruct((lay.n, D_MODEL), F32)
    return pl.pallas_call(
        functools.partial(_combine_kernel, n_ctx_tiles=n_ctx_tiles),
        grid=(n_tiles,),
        in_specs=[
            pl.BlockSpec((TOP_K, tm, HALF_D), lambda i: (0, i, 0)),
            pl.BlockSpec((tm, TOP_K), row),
            pl.BlockSpec((tm, HALF_D), row),
            pl.BlockSpec((D_MODEL, EXPERT_FF), const),
            pl.BlockSpec((D_MODEL, EXPERT_FF), const),
            pl.BlockSpec((EXPERT_FF, D_MODEL), const),
            pl.BlockSpec((tm, D_MODEL), row),
            pl.BlockSpec((1, N_MOD, D_MODEL), lambda i: (lay.cond_row(i, tm), 0, 0)),
            pl.BlockSpec((1, D_MODEL), const),
            pl.BlockSpec((1, D_MODEL), const),
        ],
        out_specs=out_specs,
        out_shape=out_shape,
        scratch_shapes=[
            pltpu.VMEM((D_MODEL, EXPERT_FF), BF16),
            pltpu.VMEM((D_MODEL, EXPERT_FF), BF16),
            pltpu.VMEM((EXPERT_FF, D_MODEL), BF16),
        ],
        compiler_params=_cparams(("arbitrary",)),
        name="combine_norm",
    )(yp, gate8, hp, sw1, sw3, sw2, x1, mod_l, g2, b2)


def _moe(hp, meta, gate8, counts, w1, w3, w2, sw1, sw3, sw2, x1, mod_l, g2, b2, lay, split_out):
    n = lay.n
    r_max = n * TOP_K + N_EXPERTS * EXPERT_TILE
    n_tiles = r_max // EXPERT_TILE
    cnt = counts.reshape(N_EXPERTS).astype(I32)
    padded = ((cnt + EXPERT_TILE - 1) // EXPERT_TILE) * EXPERT_TILE
    ends = jnp.cumsum(padded)
    offsets = ends - padded
    idx8 = meta[:TOP_K]
    base8 = jnp.sum(jnp.where(idx8[:, :, None] == jnp.arange(N_EXPERTS, dtype=I32), offsets, 0), axis=-1)
    pos = (base8 + meta[TOP_K:]).astype(I32)
    tile_start = jnp.arange(n_tiles, dtype=I32) * EXPERT_TILE
    tile_expert = jnp.minimum(jnp.sum(tile_start[:, None] >= ends[None, :], axis=1), N_EXPERTS - 1).astype(I32)
    n_active = (ends[-1] // EXPERT_TILE).astype(I32).reshape(1)

    pos_b = pos.reshape(TOP_K, n // SC_BATCH, SC_BATCH).transpose(1, 0, 2)
    xs = _sc_scatter_rows(hp, pos_b, r_max)
    ys = _experts(xs, tile_expert, n_active, w1, w3, w2, EXPERT_TILE)
    yp = _sc_gather_rows(ys, pos.reshape(n * TOP_K)).reshape(TOP_K, n, HALF_D)
    return _combine(yp, gate8.T, hp, sw1, sw3, sw2, x1, mod_l, g2, b2, lay, TOKEN_TILE, split_out)


def kernel(x_prompt, x_sample, cache_k, cache_v, state_hgrn, c, c_ctx, w_ada, b_ada, w_in, w_out, attn_sink, hgrn_lb, hgrn_norm, ln1_g, ln1_b, ln2_g, ln2_b, router_w, router_b, moe_w1, moe_w3, moe_w2, shared_w1, shared_w3, shared_w2):
    b_ctx, t_ctx, _ = x_prompt.shape
    b_lat, t_lat, _ = x_sample.shape
    past = cache_k.shape[2]
    lay = _Layout(b_ctx, t_ctx, b_lat, t_lat)
    tm = TOKEN_TILE
    assert 1 + b_lat <= COND_ROWS
    assert lay.n_ctx % tm == 0 and lay.t_lat % tm == 0 and lay.n_ctx % lay.t_lat == 0

    x = (x_prompt.reshape(lay.n_ctx, D_MODEL), x_sample.reshape(lay.n_lat, D_MODEL))
    cond = jnp.concatenate([c_ctx[None, :], c, jnp.zeros((COND_ROWS - 1 - b_lat, D_MODEL), F32)], axis=0)
    mod = _adaln(cond, w_ada, b_ada).reshape(DEPTH, COND_ROWS, N_MOD, D_MODEL)

    lb_all = jnp.cumsum(jax.nn.softmax(hgrn_lb.astype(F32), axis=0), axis=0)
    lb_all = lb_all - lb_all[:1]
    lbp = jnp.stack([jnp.log(lb_all), jnp.log1p(-lb_all), 1.0 - lb_all], axis=2)

    cos_t, sin_t = _rope_tables(lay, tm)
    zero_state = jnp.zeros((b_ctx, 2, HGRN_HEADS, HGRN_DK, HGRN_DK), F32)

    ks_out, vs_out, ss_out = [], [], []
    for l in range(DEPTH):
        proj = _inproj(x, mod[l], w_in[l], cos_t, sin_t, lay, tm)
        ks_out.append(proj[:lay.n_ctx, COL_K:COL_K + KV_WIDTH].reshape(b_ctx, t_ctx, N_KV_HEADS, HEAD_DIM))
        vs_out.append(proj[:lay.n_ctx, COL_V:COL_V + KV_WIDTH].reshape(b_ctx, t_ctx, N_KV_HEADS, HEAD_DIM))
        sink_l = attn_sink[l].reshape(1, N_HEADS)
        attn_c = _attn_context(proj, sink_l, lay)
        attn_l = _attn_latent(proj, cache_k[:, l].reshape(b_lat, past, KV_WIDTH),
                              cache_v[:, l].reshape(b_lat, past, KV_WIDTH), sink_l, lay)
        four_c = _fourier(proj, 0, b_ctx, t_ctx, t_ctx, "fourier_ctx")
        four_l = _fourier(proj, lay.n_ctx, b_lat, t_lat, min(t_lat, 512), "fourier_lat")
        gn_row = jnp.tile(hgrn_norm[l], HGRN_HEADS).reshape(1, HGRN_WIDTH)
        rec_c, s_fin = _hgrn(proj, 0, b_ctx, t_ctx, lbp[l], gn_row, zero_state, "hgrn_ctx")
        s0t = jnp.swapaxes(state_hgrn[:, l].astype(F32), -1, -2)
        rec_l, _ = _hgrn(proj, lay.n_ctx, b_lat, t_lat, lbp[l], gn_row, s0t, "hgrn_lat")
        ss_out.append(jnp.swapaxes(s_fin, -1, -2))

        x1, hp, meta, gate8, counts = _outproj(
            (attn_c, attn_l), (four_c, four_l), (rec_c, rec_l), x, mod[l], w_out[l], ln1_g[l].reshape(1, -1), ln1_b[l].reshape(1, -1),
            router_w[l].T, router_b[l].reshape(-1, 1), lay, tm)
        x = _moe(hp, meta, gate8, counts, moe_w1[l], moe_w3[l], moe_w2[l], shared_w1[l], shared_w3[l],
                 shared_w2[l], x1, mod[l], ln2_g[l].reshape(1, -1), ln2_b[l].reshape(1, -1), lay,
                 split_out=(l == DEPTH - 1))

    y_prompt = x[0].reshape(b_ctx, t_ctx, D_MODEL)
    y_sample = x[1].reshape(b_lat, t_lat, D_MODEL)
    new_cache_k = jnp.stack(ks_out, axis=1)
    new_cache_v = jnp.stack(vs_out, axis=1)
    new_state = jnp.stack(ss_out, axis=1).astype(x_prompt.dtype)
    return (y_prompt, y_sample, new_cache_k, new_cache_v, new_state)
```

```python
import functools
import math

import numpy as np
import jax
import jax.numpy as jnp
from jax import lax
from jax.experimental import pallas as pl
from jax.experimental.pallas import tpu as pltpu
from jax.experimental.pallas import tpu_sc as plsc

F32 = jnp.float32
BF16 = jnp.bfloat16
I32 = jnp.int32

D_MODEL = 1024
HALF_D = D_MODEL // 2
DEPTH = 2
GRID_W = 64
ROPE_BASE = 10000.0
HEAD_DIM = 64
ATTN_WIDTH = 512
N_HEADS = 8
N_KV_HEADS = 2
KV_GROUP = 4
KV_WIDTH = N_KV_HEADS * HEAD_DIM
WINDOW = 128
ATTN_BLOCK = 128
FOURIER_WIDTH = 256
FOURIER_GROUPS = 4
HGRN_WIDTH = 256
HGRN_HEADS = 4
HGRN_DK = 64
HGRN_CHUNK = 64
IN_WIDTH = 2304
N_EXPERTS = 64
TOP_K = 8
EXPERT_FF = 256
ROUTED_SCALE = 2.5
N_MOD = 6
LN_EPS = 1e-5
ADA_EPS = 1e-6
GN_EPS = 1e-6
DEEPNORM_ALPHA = (2 * DEPTH) ** 0.25

COL_Q = 0
COL_K = 512
COL_V = 640
COL_U = 768
COL_HQ = 1024
COL_FF = 1280
COL_FB = 1536
COL_HI = 1792
COL_HG = 2048
ROPE_COLS = COL_V

V7X_LANES = 128
COND_ROWS = 16
NEG_BIG = -1e30
TOKEN_TILE = 512
EXPERT_TILE = 512
SC_BATCH = 64

VMEM_LIMIT = 56 * 1024 * 1024


def _cparams(sem):
    return pltpu.CompilerParams(dimension_semantics=sem, vmem_limit_bytes=VMEM_LIMIT)


def _dot(a, b):
    return jnp.dot(a, b, preferred_element_type=F32)


def _dot_nt(a, b):
    return lax.dot_general(a, b, (((1,), (1,)), ((), ())), preferred_element_type=F32)


def _dot_tn(a, b):
    return lax.dot_general(a, b, (((0,), (0,)), ((), ())), preferred_element_type=F32)


def _split3(x):
    hi = x.astype(BF16)
    r1 = x - hi.astype(F32)
    mid = r1.astype(BF16)
    lo = (r1 - mid.astype(F32)).astype(BF16)
    return hi, mid, lo


def _dot_exact_lhs(m_bf16, x):
    hi, mid, lo = _split3(x)
    return _dot(m_bf16, hi) + _dot(m_bf16, mid) + _dot(m_bf16, lo)


def _dot_exact_rhs(x, m_bf16):
    hi, mid, lo = _split3(x)
    return _dot(hi, m_bf16) + _dot(mid, m_bf16) + _dot(lo, m_bf16)


def _dot_hp(a, b):
    a_hi = a.astype(BF16)
    a_lo = (a - a_hi.astype(F32)).astype(BF16)
    b_hi = b.astype(BF16)
    b_lo = (b - b_hi.astype(F32)).astype(BF16)
    return _dot(a_hi, b_hi) + _dot(a_hi, b_lo) + _dot(a_lo, b_hi)


def _pack_bf16_pair(lo, hi):
    return lax.bitcast_convert_type(pltpu.pack_elementwise([lo, hi], packed_dtype=BF16), I32)


def _unpack_bf16_pair(w):
    u = lax.bitcast_convert_type(w, jnp.uint32)
    lo = pltpu.unpack_elementwise(u, index=0, packed_dtype=BF16, unpacked_dtype=F32)
    hi = pltpu.unpack_elementwise(u, index=1, packed_dtype=BF16, unpacked_dtype=F32)
    return lo, hi


def _ln_plain(x, eps):
    mu = jnp.mean(x, axis=-1, keepdims=True)
    xc = x - mu
    var = jnp.mean(xc * xc, axis=-1, keepdims=True)
    return xc * lax.rsqrt(var + eps)


def _silu(x):
    return x * jax.nn.sigmoid(x)


def _adaln_kernel(c_ref, w_ref, b_ref, o_ref):
    s = _silu(c_ref[...])
    o_ref[0] = _dot_hp(s, w_ref[0]) + b_ref[0]


def _adaln(cond, w_ada, b_ada):
    return pl.pallas_call(
        _adaln_kernel,
        grid=(DEPTH, N_MOD),
        in_specs=[
            pl.BlockSpec((COND_ROWS, D_MODEL), lambda l, j: (0, 0)),
            pl.BlockSpec((1, D_MODEL, D_MODEL), lambda l, j: (l, 0, j)),
            pl.BlockSpec((1, 1, D_MODEL), lambda l, j: (l, 0, j)),
        ],
        out_specs=pl.BlockSpec((1, COND_ROWS, D_MODEL), lambda l, j: (l, 0, j)),
        out_shape=jax.ShapeDtypeStruct((DEPTH, COND_ROWS, N_MOD * D_MODEL), F32),
        compiler_params=_cparams(("arbitrary", "arbitrary")),
        name="adaln",
    )(cond, w_ada, b_ada.reshape(DEPTH, 1, N_MOD * D_MODEL))


class _Layout:
    def __init__(self, b_ctx, t_ctx, b_lat, t_lat):
        self.b_ctx, self.t_ctx, self.b_lat, self.t_lat = b_ctx, t_ctx, b_lat, t_lat
        self.n_ctx = b_ctx * t_ctx
        self.n_lat = b_lat * t_lat
        self.n = self.n_ctx + self.n_lat

    def cond_row(self, tile, tm):
        n_ctx_tiles = self.n_ctx // tm
        per_batch = self.t_lat // tm
        return jnp.where(tile < n_ctx_tiles, 0, 1 + (tile - n_ctx_tiles) // per_batch)


def _group_specs(n_arrays, tm, width, n_ctx_tiles):
    if n_arrays == 1:
        return [pl.BlockSpec((tm, width), lambda i: (i, 0))]
    return [pl.BlockSpec((tm, width), lambda i: (jnp.minimum(i, n_ctx_tiles - 1), 0)),
            pl.BlockSpec((tm, width), lambda i: (jnp.maximum(i - n_ctx_tiles, 0), 0))]


def _inproj_kernel(*refs, n_ctx_tiles):
    xs = refs[:-8]
    mod_ref, w_ref, cos_ref, sin_ref, o_ref, kc_ref, vc_ref, wb_ref = refs[-8:]

    @pl.when(pl.program_id(0) == 0)
    def _():
        wb_ref[...] = w_ref[...].astype(BF16)

    if len(xs) == 2:
        x = jnp.where(pl.program_id(0) < n_ctx_tiles, xs[0][...], xs[1][...])
    else:
        x = xs[0][...]
    shift = mod_ref[0, 0:1, :]
    scale = mod_ref[0, 1:2, :]
    h = (_ln_plain(x, ADA_EPS) * (1.0 + scale) + shift).astype(BF16)
    p = _dot(h, wb_ref[...])
    cos = cos_ref[...]
    sin = sin_ref[...]
    lane = lax.broadcasted_iota(I32, cos.shape, 1)
    first_half = (lane & 31) < 16
    for cb in range(ROPE_COLS // V7X_LANES):
        seg = p[:, cb * V7X_LANES:(cb + 1) * V7X_LANES]
        partner = jnp.where(first_half, pltpu.roll(seg, V7X_LANES - 16, 1), pltpu.roll(seg, 16, 1))
        o_ref[:, cb * V7X_LANES:(cb + 1) * V7X_LANES] = seg * cos + partner * sin
    o_ref[:, ROPE_COLS:] = p[:, ROPE_COLS:]

    @pl.when(pl.program_id(0) < n_ctx_tiles)
    def _():
        kc_ref[...] = p[:, COL_K:COL_K + KV_WIDTH]
        vc_ref[...] = p[:, COL_V:COL_V + KV_WIDTH]


def _rope_tables(lay, tm):
    t = lay.t_lat
    pos = jnp.arange(t)
    row = (pos // GRID_W).astype(F32)
    col = (pos % GRID_W).astype(F32)
    n_freq = HEAD_DIM // 4
    inv = ROPE_BASE ** (-jnp.arange(n_freq, dtype=F32) / n_freq)
    ang_r = row[:, None] * inv
    ang_c = col[:, None] * inv
    ang = jnp.concatenate([ang_r, ang_r, ang_c, ang_c], axis=1)
    sign = jnp.concatenate([-jnp.ones(n_freq), jnp.ones(n_freq), -jnp.ones(n_freq), jnp.ones(n_freq)]).astype(F32)
    cos = jnp.cos(ang)
    sin = jnp.sin(ang) * sign
    cos = jnp.concatenate([jnp.ones((tm, HEAD_DIM), F32), cos], axis=0)
    sin = jnp.concatenate([jnp.zeros((tm, HEAD_DIM), F32), sin], axis=0)
    return jnp.tile(cos, (1, 2)), jnp.tile(sin, (1, 2))


def _inproj(x, mod_l, w_in_l, cos_t, sin_t, lay, tm):
    n_tiles = lay.n // tm
    n_ctx_tiles = lay.n_ctx // tm
    per_batch = lay.t_lat // tm

    def tbl(i):
        return jnp.where(i < n_ctx_tiles, 0, 1 + (i - n_ctx_tiles) % per_batch)

    xs = x if isinstance(x, tuple) else (x,)
    return pl.pallas_call(
        functools.partial(_inproj_kernel, n_ctx_tiles=n_ctx_tiles),
        grid=(n_tiles,),
        in_specs=[
            *_group_specs(len(xs), tm, D_MODEL, n_ctx_tiles),
            pl.BlockSpec((1, N_MOD, D_MODEL), lambda i: (lay.cond_row(i, tm), 0, 0)),
            pl.BlockSpec((D_MODEL, IN_WIDTH), lambda i: (0, 0), pipeline_mode=pl.Buffered(1)),
            pl.BlockSpec((tm, V7X_LANES), lambda i: (tbl(i), 0)),
            pl.BlockSpec((tm, V7X_LANES), lambda i: (tbl(i), 0)),
        ],
        out_specs=[
            pl.BlockSpec((tm, IN_WIDTH), lambda i: (i, 0)),
            pl.BlockSpec((tm, KV_WIDTH), lambda i: (jnp.minimum(i, n_ctx_tiles - 1), 0)),
            pl.BlockSpec((tm, KV_WIDTH), lambda i: (jnp.minimum(i, n_ctx_tiles - 1), 0)),
        ],
        out_shape=[
            jax.ShapeDtypeStruct((lay.n, IN_WIDTH), F32),
            jax.ShapeDtypeStruct((lay.n_ctx, KV_WIDTH), F32),
            jax.ShapeDtypeStruct((lay.n_ctx, KV_WIDTH), F32),
        ],
        scratch_shapes=[pltpu.VMEM((D_MODEL, IN_WIDTH), BF16)],
        compiler_params=_cparams(("arbitrary",)),
        name="inproj",
    )(*xs, mod_l, w_in_l, cos_t, sin_t)


def _attn_kernel(sink_ref, q_ref, *refs, n_local, has_ctx, t_total):
    o_ref = refs[-1]
    k_refs = refs[:n_local]
    v_refs = refs[n_local:2 * n_local]
    tq = q_ref.shape[0]
    scale = HEAD_DIM ** -0.5
    k_parts = [kr[...] for kr in k_refs]
    v_parts = [vr[...] for vr in v_refs]
    if has_ctx:
        k_parts.append(refs[2 * n_local][0])
        v_parts.append(refs[2 * n_local + 1][0])
    kall = jnp.concatenate(k_parts, axis=0) if len(k_parts) > 1 else k_parts[0]
    vall = jnp.concatenate(v_parts, axis=0) if len(v_parts) > 1 else v_parts[0]
    nk = kall.shape[0]
    k_sw = pltpu.roll(kall, HEAD_DIM, 1)
    v_sw = pltpu.roll(vall, HEAD_DIM, 1)
    lo_half = lax.broadcasted_iota(I32, (1, V7X_LANES), 1) < HEAD_DIM
    er = jnp.where(lax.broadcasted_iota(I32, (2 * nk, V7X_LANES), 0) < nk, 0, 1)
    el = jnp.where(lax.broadcasted_iota(I32, (2 * nk, V7X_LANES), 1) < HEAD_DIM, 0, 1)
    ones_blk = jnp.where(er == el, 1.0, 0.0).astype(BF16)
    rows = 2 * tq
    if n_local > 1:
        i = pl.program_id(1)
        r = lax.broadcasted_iota(I32, (rows, nk), 0) & (tq - 1)
        c = lax.broadcasted_iota(I32, (rows, nk), 1)
        kpos = (i - 1) * tq + c
        rel = c - tq - r
        local = (jnp.abs(rel) <= WINDOW) & (kpos >= 0) & (kpos < t_total)
        valid = jnp.logical_or(local, c >= n_local * tq)
    else:
        valid = None
    for g in range(N_KV_HEADS):
        k_own, k_oth = (kall, k_sw) if g == 0 else (k_sw, kall)
        v_own, v_oth = (vall, v_sw) if g == 0 else (v_sw, vall)
        k2 = jnp.concatenate([jnp.where(lo_half, k_own, 0.0), jnp.where(lo_half, 0.0, k_oth)], axis=0).astype(BF16)
        v2 = jnp.concatenate([jnp.where(lo_half, v_own, 0.0), jnp.where(lo_half, 0.0, v_oth)], axis=0).astype(BF16)
        v2e = jnp.concatenate([v2, ones_blk], axis=1)
        pairs = [2 * g, 2 * g + 1]
        qq = jnp.concatenate([q_ref[:, p * V7X_LANES:(p + 1) * V7X_LANES] for p in pairs], axis=0)
        qq = (qq * scale).astype(BF16)
        sink_a = jnp.concatenate([jnp.full((tq, 1), sink_ref[0, 2 * p], F32) for p in pairs], axis=0)
        sink_b = jnp.concatenate([jnp.full((tq, 1), sink_ref[0, 2 * p + 1], F32) for p in pairs], axis=0)
        s = _dot_nt(qq, k2)
        s_a = s[:, :nk]
        s_b = s[:, nk:]
        if valid is not None:
            s_a = jnp.where(valid, s_a, NEG_BIG)
            s_b = jnp.where(valid, s_b, NEG_BIG)
        m_a = jnp.maximum(jnp.max(s_a, axis=1, keepdims=True), sink_a)
        m_b = jnp.maximum(jnp.max(s_b, axis=1, keepdims=True), sink_b)
        pe = jnp.concatenate([jnp.exp(s_a - m_a).astype(BF16), jnp.exp(s_b - m_b).astype(BF16)], axis=1)
        acc = _dot(pe, v2e)
        sink_term = jnp.where(lo_half, jnp.exp(sink_a - m_a), jnp.exp(sink_b - m_b))
        o = acc[:, :V7X_LANES] / (acc[:, V7X_LANES:] + sink_term)
        for j, p in enumerate(pairs):
            o_ref[:, p * V7X_LANES:(p + 1) * V7X_LANES] = o[j * tq:(j + 1) * tq].astype(o_ref.dtype)


def _attn_context(proj, sink_l, lay):
    t = lay.t_ctx
    kb, vb = COL_K // KV_WIDTH, COL_V // KV_WIDTH
    body = functools.partial(_attn_kernel, n_local=1, has_ctx=False, t_total=t)
    return pl.pallas_call(
        body,
        grid=(lay.b_ctx,),
        in_specs=[
            pl.BlockSpec(memory_space=pltpu.SMEM),
            pl.BlockSpec((t, ATTN_WIDTH), lambda b: (b, 0)),
            pl.BlockSpec((t, KV_WIDTH), lambda b: (b, kb)),
            pl.BlockSpec((t, KV_WIDTH), lambda b: (b, vb)),
        ],
        out_specs=pl.BlockSpec((t, ATTN_WIDTH), lambda b: (b, 0)),
        out_shape=jax.ShapeDtypeStruct((lay.n_ctx, ATTN_WIDTH), BF16),
        compiler_params=_cparams(("arbitrary",)),
        name="attn_ctx",
    )(sink_l, proj, proj, proj)


def _attn_latent(proj, k_ctx, v_ctx, sink_l, lay):
    t = lay.t_lat
    tq = ATTN_BLOCK
    nq = t // tq
    base = lay.n_ctx // tq
    kb, vb = COL_K // KV_WIDTH, COL_V // KV_WIDTH
    past = k_ctx.shape[1]

    def rows(off):
        return lambda b, i: base + b * nq + jnp.clip(i + off, 0, nq - 1)

    def kv_specs(col):
        return [pl.BlockSpec((tq, KV_WIDTH), (lambda b, i, f=rows(off): (f(b, i), col))) for off in (-1, 0, 1)]

    body = functools.partial(_attn_kernel, n_local=3, has_ctx=True, t_total=t)
    return pl.pallas_call(
        body,
        grid=(lay.b_lat, nq),
        in_specs=[
            pl.BlockSpec(memory_space=pltpu.SMEM),
            pl.BlockSpec((tq, ATTN_WIDTH), lambda b, i: (base + b * nq + i, 0)),
            *kv_specs(kb),
            *kv_specs(vb),
            pl.BlockSpec((1, past, KV_WIDTH), lambda b, i: (b, 0, 0)),
            pl.BlockSpec((1, past, KV_WIDTH), lambda b, i: (b, 0, 0)),
        ],
        out_specs=pl.BlockSpec((tq, ATTN_WIDTH), lambda b, i: (b * nq + i, 0)),
        out_shape=jax.ShapeDtypeStruct((lay.n_lat, ATTN_WIDTH), BF16),
        compiler_params=_cparams(("arbitrary", "arbitrary")),
        name="attn_lat",
    )(sink_l, proj, proj, proj, proj, proj, proj, proj, k_ctx, v_ctx)


def _fourier_kernel(cs_ref, u_ref, cc_ref, sc_ref, o_ref, csb_ref, *, scale):
    @pl.when(pl.program_id(1) == 0)
    def _():
        csb_ref[...] = cs_ref[...].astype(BF16)

    z = u_ref[...].astype(BF16)
    zc = _dot(z, cc_ref[...].astype(BF16)).astype(BF16)
    zs = _dot(z, sc_ref[...].astype(BF16)).astype(BF16)
    zz = jnp.concatenate([zc, zs], axis=0)
    o_ref[...] = (_dot(csb_ref[...], zz) * scale).astype(o_ref.dtype)


@functools.lru_cache(maxsize=None)
def _dft_tables(t):
    idx = np.arange(t, dtype=np.int64)
    ang = 2.0 * np.pi * ((idx[:, None] * idx[None, :]) % t).astype(np.float64) / t
    cs = np.concatenate([np.cos(ang), -np.sin(ang)], axis=1).astype(np.float32)
    cw = FOURIER_WIDTH // FOURIER_GROUPS
    cidx = np.arange(cw, dtype=np.int64)
    cang = 2.0 * np.pi * ((cidx[:, None] * cidx[None, :]) % cw).astype(np.float64) / cw
    eye = np.eye(FOURIER_GROUPS)
    cc = np.kron(eye, np.cos(cang)).astype(np.float32)
    sc = np.kron(eye, np.sin(cang)).astype(np.float32)
    return cs, cc, sc


def _fourier(proj, row0, b, t, tm, name):
    cs, cc, sc = _dft_tables(t)
    cw = FOURIER_WIDTH // FOURIER_GROUPS
    nt = t // tm
    ub = COL_U // FOURIER_WIDTH
    base = row0 // t
    body = functools.partial(_fourier_kernel, scale=1.0 / math.sqrt(t * cw))
    return pl.pallas_call(
        body,
        grid=(nt, b),
        in_specs=[
            pl.BlockSpec((tm, 2 * t), lambda i, bb: (i, 0)),
            pl.BlockSpec((t, FOURIER_WIDTH), lambda i, bb: (base + bb, ub)),
            pl.BlockSpec((FOURIER_WIDTH, FOURIER_WIDTH), lambda i, bb: (0, 0)),
            pl.BlockSpec((FOURIER_WIDTH, FOURIER_WIDTH), lambda i, bb: (0, 0)),
        ],
        out_specs=pl.BlockSpec((tm, FOURIER_WIDTH), lambda i, bb: (bb * nt + i, 0)),
        out_shape=jax.ShapeDtypeStruct((b * t, FOURIER_WIDTH), BF16),
        scratch_shapes=[pltpu.VMEM((tm, 2 * t), BF16)],
        compiler_params=_cparams(("arbitrary", "arbitrary")),
        name=name,
    )(jnp.asarray(cs), proj, jnp.asarray(cc), jnp.asarray(sc))


HGRN_LEVELS = (64, 32, 16, 8, 4, 2)
HGRN_SAFE_RANGE = 80.0


@functools.lru_cache(maxsize=None)
def _hgrn_tables():
    c = HGRN_CHUNK
    return np.stack([np.tril(np.ones((c, c))), np.triu(np.ones((c, c)))]).astype(np.float32)


def _boundary_rows(b, m, reverse):
    c, w = b.shape
    half = m // 2
    off = half if reverse else half - 1
    if m >= 16:
        return jnp.concatenate(
            [jnp.broadcast_to(b[s + off:s + off + 1], (m, w)) for s in range(0, c, m)], axis=0)
    sub = lax.broadcasted_iota(I32, (c, w), 0) & 7
    b3 = b.reshape(c // 8, 8, w)

    def bcast(j):
        return jnp.broadcast_to(b3[:, j:j + 1, :], (c // 8, 8, w)).reshape(c, w)

    if m == 8:
        return bcast(off)
    if m == 4:
        return jnp.where(sub < 4, bcast(off), bcast(4 + off))
    assert m == 2
    if reverse:
        return jnp.where((sub & 1) == 1, b, pltpu.roll(b, c - 1, 0))
    return jnp.where((sub & 1) == 0, b, pltpu.roll(b, 1, 0))


class _HgrnDir:
    def __init__(self, q, z, v, loglb, log1mlb, onemlb, cum, reverse):
        c = HGRN_CHUNK
        self.q, self.reverse = q, reverse
        log_sig = jnp.minimum(z, 0.0) - jnp.log1p(jnp.exp(-jnp.abs(z)))
        bb = log1mlb + log_sig
        mx = jnp.maximum(loglb, bb)
        lf = mx + jnp.log1p(jnp.exp(-jnp.abs(loglb - bb)))
        self.kk = onemlb * jax.nn.sigmoid(-z)
        self.b = _dot_exact_lhs(cum, lf)
        b_end = self.b[0:1] if reverse else self.b[c - 1:c]
        self.qt = (q * jnp.exp(self.b)).astype(BF16)
        self.kt = (self.kk * jnp.exp(b_end - self.b)).astype(BF16)
        self.decay = jnp.exp(b_end)
        self.vb = v.astype(BF16)
        mid = c // 2 if reverse else c // 2 - 1
        self.rel = self.b - self.b[mid:mid + 1]
        self.span = jnp.max(jnp.abs(self.rel))

    def tree_decay_matrices(self):
        c = HGRN_CHUNK
        q, kk, b = self.q, self.kk, self.b
        row = lax.broadcasted_iota(I32, (c, 1), 0)
        ti = lax.broadcasted_iota(I32, (c, c), 0)
        si = lax.broadcasted_iota(I32, (c, c), 1)
        qb = q.astype(BF16)
        kb = kk.astype(BF16)
        heads = [slice(h * HGRN_DK, (h + 1) * HGRN_DK) for h in range(HGRN_HEADS)]
        acc = [jnp.where(ti == si, _dot_nt(qb[:, sl], kb[:, sl]), 0.0) for sl in heads]
        for m in HGRN_LEVELS:
            r = _boundary_rows(b, m, self.reverse)
            upper = (row & (m - 1)) >= (m // 2)
            q_side = jnp.logical_not(upper) if self.reverse else upper
            e = jnp.exp(jnp.where(q_side, b - r, r - b))
            qf = jnp.where(q_side, q * e, 0.0).astype(BF16)
            kf = jnp.where(q_side, 0.0, kk * e).astype(BF16)
            same_block = (ti & -m) == (si & -m)
            for h, sl in enumerate(heads):
                acc[h] = acc[h] + jnp.where(same_block, _dot_nt(qf[:, sl], kf[:, sl]), 0.0)
        return acc

    def midpoint_decay_matrices(self):
        c = HGRN_CHUNK
        ti = lax.broadcasted_iota(I32, (c, c), 0)
        si = lax.broadcasted_iota(I32, (c, c), 1)
        qm = (self.q * jnp.exp(self.rel)).astype(BF16)
        km = (self.kk * jnp.exp(-self.rel)).astype(BF16)
        causal = (si >= ti) if self.reverse else (si <= ti)
        return [jnp.where(causal, _dot_nt(qm[:, h * HGRN_DK:(h + 1) * HGRN_DK], km[:, h * HGRN_DK:(h + 1) * HGRN_DK]), 0.0)
                for h in range(HGRN_HEADS)]

    def outputs(self, a_heads, st_ref, d):
        outs = []
        for h in range(HGRN_HEADS):
            sl = slice(h * HGRN_DK, (h + 1) * HGRN_DK)
            st = st_ref[d, h]
            o = _dot_nt(self.qt[:, sl], st.astype(BF16)) + _dot(a_heads[h].astype(BF16), self.vb[:, sl])
            st_ref[d, h] = st * self.decay[:, sl] + _dot_tn(self.vb[:, sl], self.kt[:, sl])
            outs.append(o)
        return jnp.concatenate(outs, axis=1)


def _hgrn_kernel(hq_ref, ff_ref, fb_ref, hi_ref, hg_ref, lbp_ref, gn_ref, mall_ref, ones_ref, s0_ref,
                 rec_ref, sfin_ref, st_ref, of_ref, ob_ref, *, t):
    c = HGRN_CHUNK
    n = t // c
    st_ref[...] = s0_ref[0]

    def body(ci, carry):
        rf = pl.ds(pl.multiple_of(ci * c, c), c)
        rb = pl.ds(pl.multiple_of((n - 1 - ci) * c, c), c)
        fwd = _HgrnDir(hq_ref[rf, :], ff_ref[rf, :], hi_ref[rf, :], lbp_ref[0, 0:1, :], lbp_ref[0, 1:2, :],
                       lbp_ref[0, 2:3, :], mall_ref[0].astype(BF16), False)
        bwd = _HgrnDir(hq_ref[rb, :], fb_ref[rb, :], hi_ref[rb, :], lbp_ref[1, 0:1, :], lbp_ref[1, 1:2, :],
                       lbp_ref[1, 2:3, :], mall_ref[1].astype(BF16), True)
        a_f, a_b = lax.cond(
            jnp.maximum(fwd.span, bwd.span) <= HGRN_SAFE_RANGE,
            lambda: (fwd.midpoint_decay_matrices(), bwd.midpoint_decay_matrices()),
            lambda: (fwd.tree_decay_matrices(), bwd.tree_decay_matrices()))
        of_ref[rf, :] = fwd.outputs(a_f, st_ref, 0)
        ob_ref[rb, :] = bwd.outputs(a_b, st_ref, 1)
        return carry

    lax.fori_loop(0, n, body, 0)
    sfin_ref[0] = st_ref[...]
    o = of_ref[...] + ob_ref[...]
    ms = _dot_exact_rhs(o * o, ones_ref[...].astype(BF16)) * (1.0 / HGRN_DK)
    o = o * lax.rsqrt(ms + GN_EPS) * gn_ref[...]
    rec_ref[...] = (o * _silu(hg_ref[...])).astype(rec_ref.dtype)


def _hgrn(proj, row0, b, t, lbp, gn_row, s0t, name):
    base = row0 // t
    m_all = jnp.asarray(_hgrn_tables())
    ones_bd = jnp.asarray(np.kron(np.eye(HGRN_HEADS), np.ones((HGRN_DK, HGRN_DK))).astype(np.float32))

    def col(cstart):
        return pl.BlockSpec((t, HGRN_WIDTH), lambda bb, cb=cstart // HGRN_WIDTH: (base + bb, cb))

    const2 = lambda bb: (0, 0)
    const3 = lambda bb: (0, 0, 0)
    st_shape = (2, HGRN_HEADS, HGRN_DK, HGRN_DK)
    body = functools.partial(_hgrn_kernel, t=t)
    return pl.pallas_call(
        body,
        grid=(b,),
        in_specs=[
            col(COL_HQ), col(COL_FF), col(COL_FB), col(COL_HI), col(COL_HG),
            pl.BlockSpec((2, 3, HGRN_WIDTH), const3),
            pl.BlockSpec((1, HGRN_WIDTH), const2),
            pl.BlockSpec(m_all.shape, const3),
            pl.BlockSpec(ones_bd.shape, const2),
            pl.BlockSpec((1,) + st_shape, lambda bb: (bb, 0, 0, 0, 0)),
        ],
        out_specs=[
            pl.BlockSpec((t, HGRN_WIDTH), lambda bb: (bb, 0)),
            pl.BlockSpec((1,) + st_shape, lambda bb: (bb, 0, 0, 0, 0)),
        ],
        out_shape=[
            jax.ShapeDtypeStruct((b * t, HGRN_WIDTH), BF16),
            jax.ShapeDtypeStruct((b,) + st_shape, F32),
        ],
        scratch_shapes=[
            pltpu.VMEM(st_shape, F32),
            pltpu.VMEM((t, HGRN_WIDTH), F32),
            pltpu.VMEM((t, HGRN_WIDTH), F32),
        ],
        compiler_params=_cparams(("arbitrary",)),
        name=name,
    )(proj, proj, proj, proj, proj, lbp, gn_row, m_all, ones_bd, s0t)


def _outproj_kernel(attn_c_ref, attn_l_ref, four_c_ref, four_l_ref, rec_c_ref, rec_l_ref, *refs, n_ctx_tiles):
    xs = refs[:-14]
    (mod_ref, w_ref, g_ref, b_ref, rw_ref, rb_ref, x1_ref, hp_ref, meta_ref, gate_ref, cnt_ref, wb_ref, tri_ref,
     run_ref) = refs[-14:]
    tm = x1_ref.shape[0]
    is_ctx = pl.program_id(0) < n_ctx_tiles
    x_in = jnp.where(is_ctx, xs[0][...], xs[1][...]) if len(xs) == 2 else xs[0][...]
    attn = jnp.where(is_ctx, attn_c_ref[...], attn_l_ref[...])
    four = jnp.where(is_ctx, four_c_ref[...], four_l_ref[...])
    rec = jnp.where(is_ctx, rec_c_ref[...], rec_l_ref[...])

    @pl.when(pl.program_id(0) == 0)
    def _():
        wb_ref[...] = w_ref[...].astype(BF16)
        r = lax.broadcasted_iota(I32, (tm, tm), 0)
        c = lax.broadcasted_iota(I32, (tm, tm), 1)
        tri_ref[...] = jnp.where(r < c, 1.0, 0.0).astype(BF16)
        run_ref[...] = jnp.zeros_like(run_ref)

    out = _dot(attn, wb_ref[0:ATTN_WIDTH, :])
    out = out + _dot(four, wb_ref[ATTN_WIDTH:ATTN_WIDTH + FOURIER_WIDTH, :])
    out = out + _dot(rec, wb_ref[ATTN_WIDTH + FOURIER_WIDTH:, :])
    gate1 = mod_ref[0, 2:3, :]
    y = DEEPNORM_ALPHA * x_in + gate1 * out
    x1 = _ln_plain(y, LN_EPS) * g_ref[...] + b_ref[...]
    x1_ref[...] = x1
    h2 = _ln_plain(x1, ADA_EPS) * (1.0 + mod_ref[0, 4:5, :]) + mod_ref[0, 3:4, :]
    hp_ref[...] = _pack_bf16_pair(h2[:, :HALF_D], h2[:, HALF_D:])

    h_hi = h2.astype(BF16)
    h_lo = (h2 - h_hi.astype(F32)).astype(BF16)
    rwt = rw_ref[...]
    w_hi = rwt.astype(BF16)
    w_lo = (rwt - w_hi.astype(F32)).astype(BF16)
    scores = jax.nn.sigmoid(_dot_nt(w_hi, h_hi) + _dot_nt(w_hi, h_lo) + _dot_nt(w_lo, h_hi))
    remaining = scores + rb_ref[...]
    eidx = lax.broadcasted_iota(I32, scores.shape, 0).astype(F32)
    chosen = jnp.zeros(scores.shape, jnp.bool_)
    picks = []
    for _ in range(TOP_K):
        mx = jnp.max(remaining, axis=0, keepdims=True)
        first = jnp.min(jnp.where(remaining == mx, eidx, float(N_EXPERTS)), axis=0, keepdims=True)
        pick = eidx == first
        picks.append((pick, first))
        chosen = jnp.logical_or(chosen, pick)
        remaining = jnp.where(pick, -jnp.inf, remaining)
    sel = jnp.where(chosen, scores, 0.0)
    gates = sel / jnp.sum(sel, axis=0, keepdims=True) * ROUTED_SCALE

    onehot = jnp.where(chosen, 1.0, 0.0)
    rank = run_ref[...] + _dot(onehot.astype(BF16), tri_ref[...])
    run_ref[...] += jnp.sum(onehot, axis=1, keepdims=True)
    cnt_ref[...] = run_ref[...]

    ids, rks, gks = [], [], []
    for pick, first in picks:
        ids.append(first.astype(I32))
        rks.append(jnp.sum(jnp.where(pick, rank, 0.0), axis=0, keepdims=True).astype(I32))
        gks.append(jnp.sum(jnp.where(pick, gates, 0.0), axis=0, keepdims=True))
    meta_ref[...] = jnp.concatenate(ids + rks, axis=0)
    gate_ref[...] = jnp.concatenate(gks, axis=0)


def _outproj(attn, four, rec, x, mod_l, w_out_l, g1, b1, rw, rb, lay, tm):
    n_tiles = lay.n // tm
    n_ctx_tiles = lay.n_ctx // tm
    row = lambda i: (i, 0)
    const = lambda i: (0, 0)
    xs = x if isinstance(x, tuple) else (x,)
    return pl.pallas_call(
        functools.partial(_outproj_kernel, n_ctx_tiles=n_ctx_tiles),
        grid=(n_tiles,),
        in_specs=[
            *_group_specs(2, tm, ATTN_WIDTH, n_ctx_tiles),
            *_group_specs(2, tm, FOURIER_WIDTH, n_ctx_tiles),
            *_group_specs(2, tm, HGRN_WIDTH, n_ctx_tiles),
            *_group_specs(len(xs), tm, D_MODEL, n_ctx_tiles),
            pl.BlockSpec((1, N_MOD, D_MODEL), lambda i: (lay.cond_row(i, tm), 0, 0)),
            pl.BlockSpec((D_MODEL, D_MODEL), const),
            pl.BlockSpec((1, D_MODEL), const),
            pl.BlockSpec((1, D_MODEL), const),
            pl.BlockSpec((N_EXPERTS, D_MODEL), const),
            pl.BlockSpec((N_EXPERTS, 1), const),
        ],
        out_specs=[
            pl.BlockSpec((tm, D_MODEL), row),
            pl.BlockSpec((tm, HALF_D), row),
            pl.BlockSpec((2 * TOP_K, tm), lambda i: (0, i)),
            pl.BlockSpec((TOP_K, tm), lambda i: (0, i)),
            pl.BlockSpec((N_EXPERTS, 1), const),
        ],
        out_shape=[
            jax.ShapeDtypeStruct((lay.n, D_MODEL), F32),
            jax.ShapeDtypeStruct((lay.n, HALF_D), I32),
            jax.ShapeDtypeStruct((2 * TOP_K, lay.n), I32),
            jax.ShapeDtypeStruct((TOP_K, lay.n), F32),
            jax.ShapeDtypeStruct((N_EXPERTS, 1), F32),
        ],
        scratch_shapes=[
            pltpu.VMEM((D_MODEL, D_MODEL), BF16),
            pltpu.VMEM((tm, tm), BF16),
            pltpu.VMEM((N_EXPERTS, 1), F32),
        ],
        compiler_params=_cparams(("arbitrary",)),
        name="outproj_router",
    )(*attn, *four, *rec, *xs, mod_l, w_out_l, g1, b1, rw, rb)


def _sc_workers():
    info = plsc.get_sparse_core_info()
    return info.num_cores, info.num_cores * info.num_subcores


def _sc_scatter_rows(rows, pos_b, r_out):
    nc, nw = _sc_workers()
    n, w = rows.shape
    nbt, copies, _ = pos_b.shape
    assert nbt * SC_BATCH == n and nbt % (2 * nw) == 0
    per_w = nbt // nw
    mesh = plsc.VectorSubcoreMesh(core_axis_name="c", subcore_axis_name="s")

    @functools.partial(
        pl.kernel, mesh=mesh, out_type=jax.ShapeDtypeStruct((r_out, w), rows.dtype),
        scratch_types=[pltpu.VMEM((copies, SC_BATCH), I32), pltpu.VMEM((copies, SC_BATCH), I32),
                       pltpu.VMEM((SC_BATCH, w), rows.dtype), pltpu.VMEM((SC_BATCH, w), rows.dtype),
                       pltpu.SemaphoreType.DMA, pltpu.SemaphoreType.DMA,
                       pltpu.SemaphoreType.DMA, pltpu.SemaphoreType.DMA],
        name="sc_dispatch")
    def k(rows_hbm, pos_hbm, out_hbm, idx_a, idx_b, rows_a, rows_b, sem_ra, sem_rb, sem_sa, sem_sb):
        wid = lax.axis_index("s") * nc + lax.axis_index("c")
        first = wid * per_w

        def reads(j, idx_v, rows_v, sem):
            bt = first + j
            return (pltpu.make_async_copy(pos_hbm.at[bt], idx_v, sem),
                    pltpu.make_async_copy(rows_hbm.at[pl.ds(bt * SC_BATCH, SC_BATCH)], rows_v, sem))

        def scatters(idx_v, rows_v, sem):
            return [pltpu.make_async_copy(rows_v, out_hbm.at[idx_v.at[q]], sem) for q in range(copies)]

        def start(descs):
            for d in descs:
                d.start()

        def wait(descs):
            for d in descs:
                d.wait()

        start(reads(0, idx_a, rows_a, sem_ra))

        @pl.loop(0, per_w // 2)
        def _(p):
            j0 = 2 * p
            j1 = j0 + 1

            @pl.when(p > 0)
            def _():
                wait(scatters(idx_b, rows_b, sem_sb))

            start(reads(j1, idx_b, rows_b, sem_rb))
            wait(reads(j0, idx_a, rows_a, sem_ra))
            start(scatters(idx_a, rows_a, sem_sa))
            wait(reads(j1, idx_b, rows_b, sem_rb))
            start(scatters(idx_b, rows_b, sem_sb))
            wait(scatters(idx_a, rows_a, sem_sa))

            @pl.when(p + 1 < per_w // 2)
            def _():
                start(reads(j0 + 2, idx_a, rows_a, sem_ra))

        wait(scatters(idx_b, rows_b, sem_sb))

    return k(rows, pos_b)


def _sc_gather_rows(table, idx):
    nc, nw = _sc_workers()
    r = idx.shape[0]
    w = table.shape[1]
    assert r % (2 * nw * SC_BATCH) == 0
    per_w = r // nw
    nb = per_w // SC_BATCH
    mesh = plsc.VectorSubcoreMesh(core_axis_name="c", subcore_axis_name="s")

    @functools.partial(
        pl.kernel, mesh=mesh, out_type=jax.ShapeDtypeStruct((r, w), table.dtype),
        scratch_types=[pltpu.VMEM((per_w,), I32),
                       pltpu.VMEM((SC_BATCH, w), table.dtype), pltpu.VMEM((SC_BATCH, w), table.dtype),
                       pltpu.SemaphoreType.DMA, pltpu.SemaphoreType.DMA,
                       pltpu.SemaphoreType.DMA, pltpu.SemaphoreType.DMA],
        name="sc_combine")
    def k(table_hbm, idx_hbm, out_hbm, idx_v, rows_a, rows_b, sem_ga, sem_gb, sem_wa, sem_wb):
        wid = lax.axis_index("s") * nc + lax.axis_index("c")
        base = wid * per_w
        pltpu.sync_copy(idx_hbm.at[pl.ds(base, per_w)], idx_v)

        def gather(j, rows_v, sem):
            return pltpu.make_async_copy(table_hbm.at[idx_v.at[pl.ds(j * SC_BATCH, SC_BATCH)]], rows_v, sem)

        def write(j, rows_v, sem):
            return pltpu.make_async_copy(rows_v, out_hbm.at[pl.ds(base + j * SC_BATCH, SC_BATCH)], sem)

        gather(0, rows_a, sem_ga).start()

        @pl.loop(0, nb // 2)
        def _(p):
            j0 = 2 * p
            j1 = j0 + 1

            @pl.when(p > 0)
            def _():
                write(j1 - 2, rows_b, sem_wb).wait()

            gather(j1, rows_b, sem_gb).start()
            gather(j0, rows_a, sem_ga).wait()
            write(j0, rows_a, sem_wa).start()
            gather(j1, rows_b, sem_gb).wait()
            write(j1, rows_b, sem_wb).start()
            write(j0, rows_a, sem_wa).wait()

            @pl.when(p + 1 < nb // 2)
            def _():
                gather(j0 + 2, rows_a, sem_ga).start()

        write(nb - 1, rows_b, sem_wb).wait()

    return k(table, idx)


def _experts_kernel(te_ref, na_ref, x_ref, w1_ref, w3_ref, w2_ref, o_ref, w1b_ref, w3b_ref, w2b_ref):
    j = pl.program_id(0)
    na = na_ref[0]
    jj = jnp.minimum(j, na - 1)
    e = te_ref[jj]
    prev = te_ref[jnp.maximum(jj - 1, 0)]
    active = j < na

    @pl.when(jnp.logical_and(active, jnp.logical_or(j == 0, e != prev)))
    def _():
        w1b_ref[...] = w1_ref[0].astype(BF16)
        w3b_ref[...] = w3_ref[0].astype(BF16)
        w2b_ref[...] = w2_ref[0].astype(BF16)

    @pl.when(active)
    def _():
        lo, hi = _unpack_bf16_pair(x_ref[...])
        lo = lo.astype(BF16)
        hi = hi.astype(BF16)
        a = _dot(lo, w1b_ref[0:HALF_D, :]) + _dot(hi, w1b_ref[HALF_D:, :])
        b = _dot(lo, w3b_ref[0:HALF_D, :]) + _dot(hi, w3b_ref[HALF_D:, :])
        y = _dot((_silu(a) * b).astype(BF16), w2b_ref[...])
        o_ref[...] = _pack_bf16_pair(y[:, :HALF_D], y[:, HALF_D:])


def _experts(xs, tile_expert, n_active, w1, w3, w2, tm):
    r = xs.shape[0]
    n_tiles = r // tm

    def xmap(j, te, na):
        return (jnp.minimum(j, na[0] - 1), 0)

    def wmap(j, te, na):
        return (te[jnp.minimum(j, na[0] - 1)], 0, 0)

    grid_spec = pltpu.PrefetchScalarGridSpec(
        num_scalar_prefetch=2,
        grid=(n_tiles,),
        in_specs=[
            pl.BlockSpec((tm, HALF_D), xmap),
            pl.BlockSpec((1, D_MODEL, EXPERT_FF), wmap),
            pl.BlockSpec((1, D_MODEL, EXPERT_FF), wmap),
            pl.BlockSpec((1, EXPERT_FF, D_MODEL), wmap),
        ],
        out_specs=pl.BlockSpec((tm, HALF_D), xmap),
        scratch_shapes=[
            pltpu.VMEM((D_MODEL, EXPERT_FF), BF16),
            pltpu.VMEM((D_MODEL, EXPERT_FF), BF16),
            pltpu.VMEM((EXPERT_FF, D_MODEL), BF16),
        ],
    )
    return pl.pallas_call(
        _experts_kernel,
        grid_spec=grid_spec,
        out_shape=jax.ShapeDtypeStruct((r, HALF_D), I32),
        compiler_params=_cparams(("arbitrary",)),
        name="experts",
    )(tile_expert, n_active, xs, w1, w3, w2)


def _combine_kernel(yp_ref, gate_ref, hp_ref, sw1_ref, sw3_ref, sw2_ref, x_ref, mod_ref, g_ref, b_ref, *refs,
                    n_ctx_tiles):
    outs = refs[:-3]
    w1b_ref, w3b_ref, w2b_ref = refs[-3:]

    @pl.when(pl.program_id(0) == 0)
    def _():
        w1b_ref[...] = sw1_ref[...].astype(BF16)
        w3b_ref[...] = sw3_ref[...].astype(BF16)
        w2b_ref[...] = sw2_ref[...].astype(BF16)

    lo, hi = _unpack_bf16_pair(hp_ref[...])
    lo = lo.astype(BF16)
    hi = hi.astype(BF16)
    a = _dot(lo, w1b_ref[0:HALF_D, :]) + _dot(hi, w1b_ref[HALF_D:, :])
    b = _dot(lo, w3b_ref[0:HALF_D, :]) + _dot(hi, w3b_ref[HALF_D:, :])
    shared = _dot((_silu(a) * b).astype(BF16), w2b_ref[...])
    acc_lo = shared[:, :HALF_D]
    acc_hi = shared[:, HALF_D:]
    gates = gate_ref[...]
    for k in range(TOP_K):
        ylo, yhi = _unpack_bf16_pair(yp_ref[k])
        gk = gates[:, k:k + 1]
        acc_lo = acc_lo + gk * ylo
        acc_hi = acc_hi + gk * yhi
    moe = jnp.concatenate([acc_lo, acc_hi], axis=1)
    y = DEEPNORM_ALPHA * x_ref[...] + mod_ref[0, 5:6, :] * moe
    res = _ln_plain(y, LN_EPS) * g_ref[...] + b_ref[...]
    if len(outs) == 1:
        outs[0][...] = res
    else:
        @pl.when(pl.program_id(0) < n_ctx_tiles)
        def _():
            outs[0][...] = res

        @pl.when(pl.program_id(0) >= n_ctx_tiles)
        def _():
            outs[1][...] = res


def _combine(yp, gate8, hp, sw1, sw3, sw2, x1, mod_l, g2, b2, lay, tm, split_out):
    n_tiles = lay.n // tm
    n_ctx_tiles = lay.n_ctx // tm
    row = lambda i: (i, 0)
    const = lambda i: (0, 0)
    if split_out:
        out_specs = _group_specs(2, tm, D_MODEL, n_ctx_tiles)
        out_shape = [jax.ShapeDtypeStruct((lay.n_ctx, D_MODEL), F32), jax.ShapeDtypeStruct((lay.n_lat, D_MODEL), F32)]
    else:
        out_specs = pl.BlockSpec((tm, D_MODEL), row)
        out_shape = jax.ShapeDtypeStruct((lay.n, D_MODEL), F32)
    return pl.pallas_call(
        functools.partial(_combine_kernel, n_ctx_tiles=n_ctx_tiles),
        grid=(n_tiles,),
        in_specs=[
            pl.BlockSpec((TOP_K, tm, HALF_D), lambda i: (0, i, 0)),
            pl.BlockSpec((tm, TOP_K), row),
            pl.BlockSpec((tm, HALF_D), row),
            pl.BlockSpec((D_MODEL, EXPERT_FF), const),
            pl.BlockSpec((D_MODEL, EXPERT_FF), const),
            pl.BlockSpec((EXPERT_FF, D_MODEL), const),
            pl.BlockSpec((tm, D_MODEL), row),
            pl.BlockSpec((1, N_MOD, D_MODEL), lambda i: (lay.cond_row(i, tm), 0, 0)),
            pl.BlockSpec((1, D_MODEL), const),
            pl.BlockSpec((1, D_MODEL), const),
        ],
        out_specs=out_specs,
        out_shape=out_shape,
        scratch_shapes=[
            pltpu.VMEM((D_MODEL, EXPERT_FF), BF16),
            pltpu.VMEM((D_MODEL, EXPERT_FF), BF16),
            pltpu.VMEM((EXPERT_FF, D_MODEL), BF16),
        ],
        compiler_params=_cparams(("arbitrary",)),
        name="combine_norm",
    )(yp, gate8, hp, sw1, sw3, sw2, x1, mod_l, g2, b2)


def _moe(hp, meta, gate8, counts, w1, w3, w2, sw1, sw3, sw2, x1, mod_l, g2, b2, lay, split_out):
    n = lay.n
    r_max = n * TOP_K + N_EXPERTS * EXPERT_TILE
    n_tiles = r_max // EXPERT_TILE
    cnt = counts.reshape(N_EXPERTS).astype(I32)
    padded = ((cnt + EXPERT_TILE - 1) // EXPERT_TILE) * EXPERT_TILE
    ends = jnp.cumsum(padded)
    offsets = ends - padded
    idx8 = meta[:TOP_K]
    base8 = jnp.sum(jnp.where(idx8[:, :, None] == jnp.arange(N_EXPERTS, dtype=I32), offsets, 0), axis=-1)
    pos = (base8 + meta[TOP_K:]).astype(I32)
    tile_start = jnp.arange(n_tiles, dtype=I32) * EXPERT_TILE
    tile_expert = jnp.minimum(jnp.sum(tile_start[:, None] >= ends[None, :], axis=1), N_EXPERTS - 1).astype(I32)
    n_active = (ends[-1] // EXPERT_TILE).astype(I32).reshape(1)

    pos_b = pos.reshape(TOP_K, n // SC_BATCH, SC_BATCH).transpose(1, 0, 2)
    xs = _sc_scatter_rows(hp, pos_b, r_max)
    ys = _experts(xs, tile_expert, n_active, w1, w3, w2, EXPERT_TILE)
    yp = _sc_gather_rows(ys, pos.reshape(n * TOP_K)).reshape(TOP_K, n, HALF_D)
    return _combine(yp, gate8.T, hp, sw1, sw3, sw2, x1, mod_l, g2, b2, lay, TOKEN_TILE, split_out)


def kernel(x_prompt, x_sample, cache_k, cache_v, state_hgrn, c, c_ctx, w_ada, b_ada, w_in, w_out, attn_sink, hgrn_lb, hgrn_norm, ln1_g, ln1_b, ln2_g, ln2_b, router_w, router_b, moe_w1, moe_w3, moe_w2, shared_w1, shared_w3, shared_w2):
    b_ctx, t_ctx, _ = x_prompt.shape
    b_lat, t_lat, _ = x_sample.shape
    past = cache_k.shape[2]
    lay = _Layout(b_ctx, t_ctx, b_lat, t_lat)
    tm = TOKEN_TILE
    assert 1 + b_lat <= COND_ROWS
    assert lay.n_ctx % tm == 0 and lay.t_lat % tm == 0 and lay.n_ctx % lay.t_lat == 0

    x = (x_prompt.reshape(lay.n_ctx, D_MODEL), x_sample.reshape(lay.n_lat, D_MODEL))
    cond = jnp.concatenate([c_ctx[None, :], c, jnp.zeros((COND_ROWS - 1 - b_lat, D_MODEL), F32)], axis=0)
    mod = _adaln(cond, w_ada, b_ada).reshape(DEPTH, COND_ROWS, N_MOD, D_MODEL)

    lb_all = jnp.cumsum(jax.nn.softmax(hgrn_lb.astype(F32), axis=0), axis=0)
    lb_all = lb_all - lb_all[:1]
    lbp = jnp.stack([jnp.log(lb_all), jnp.log1p(-lb_all), 1.0 - lb_all], axis=2)

    cos_t, sin_t = _rope_tables(lay, tm)
    zero_state = jnp.zeros((b_ctx, 2, HGRN_HEADS, HGRN_DK, HGRN_DK), F32)

    ks_out, vs_out, ss_out = [], [], []
    for l in range(DEPTH):
        proj, k_new, v_new = _inproj(x, mod[l], w_in[l], cos_t, sin_t, lay, tm)
        ks_out.append(k_new.reshape(b_ctx, t_ctx, N_KV_HEADS, HEAD_DIM))
        vs_out.append(v_new.reshape(b_ctx, t_ctx, N_KV_HEADS, HEAD_DIM))
        sink_l = attn_sink[l].reshape(1, N_HEADS)
        attn_c = _attn_context(proj, sink_l, lay)
        attn_l = _attn_latent(proj, cache_k[:, l].reshape(b_lat, past, KV_WIDTH),
                              cache_v[:, l].reshape(b_lat, past, KV_WIDTH), sink_l, lay)
        four_c = _fourier(proj, 0, b_ctx, t_ctx, t_ctx, "fourier_ctx")
        four_l = _fourier(proj, lay.n_ctx, b_lat, t_lat, min(t_lat, 512), "fourier_lat")
        gn_row = jnp.tile(hgrn_norm[l], HGRN_HEADS).reshape(1, HGRN_WIDTH)
        rec_c, s_fin = _hgrn(proj, 0, b_ctx, t_ctx, lbp[l], gn_row, zero_state, "hgrn_ctx")
        s0t = jnp.swapaxes(state_hgrn[:, l].astype(F32), -1, -2)
        rec_l, _ = _hgrn(proj, lay.n_ctx, b_lat, t_lat, lbp[l], gn_row, s0t, "hgrn_lat")
        ss_out.append(jnp.swapaxes(s_fin, -1, -2))

        x1, hp, meta, gate8, counts = _outproj(
            (attn_c, attn_l), (four_c, four_l), (rec_c, rec_l), x, mod[l], w_out[l], ln1_g[l].reshape(1, -1), ln1_b[l].reshape(1, -1),
            router_w[l].T, router_b[l].reshape(-1, 1), lay, tm)
        x = _moe(hp, meta, gate8, counts, moe_w1[l], moe_w3[l], moe_w2[l], shared_w1[l], shared_w3[l],
                 shared_w2[l], x1, mod[l], ln2_g[l].reshape(1, -1), ln2_b[l].reshape(1, -1), lay,
                 split_out=(l == DEPTH - 1))

    y_prompt = x[0].reshape(b_ctx, t_ctx, D_MODEL)
    y_sample = x[1].reshape(b_lat, t_lat, D_MODEL)
    new_cache_k = jnp.stack(ks_out, axis=1)
    new_cache_v = jnp.stack(vs_out, axis=1)
    new_state = jnp.stack(ss_out, axis=1).astype(x_prompt.dtype)
    return (y_prompt, y_sample, new_cache_k, new_cache_v, new_state)
```

```python
import functools
import math

import numpy as np
import jax
import jax.numpy as jnp
from jax import lax
from jax.experimental import pallas as pl
from jax.experimental.pallas import tpu as pltpu
from jax.experimental.pallas import tpu_sc as plsc

F32 = jnp.float32
BF16 = jnp.bfloat16
I32 = jnp.int32

D_MODEL = 1024
HALF_D = D_MODEL // 2
DEPTH = 2
GRID_W = 64
ROPE_BASE = 10000.0
HEAD_DIM = 64
ATTN_WIDTH = 512
N_HEADS = 8
N_KV_HEADS = 2
KV_GROUP = 4
KV_WIDTH = N_KV_HEADS * HEAD_DIM
WINDOW = 128
ATTN_BLOCK = 128
FOURIER_WIDTH = 256
FOURIER_GROUPS = 4
HGRN_WIDTH = 256
HGRN_HEADS = 4
HGRN_DK = 64
HGRN_CHUNK = 64
IN_WIDTH = 2304
N_EXPERTS = 64
TOP_K = 8
EXPERT_FF = 256
ROUTED_SCALE = 2.5
N_MOD = 6
LN_EPS = 1e-5
ADA_EPS = 1e-6
GN_EPS = 1e-6
DEEPNORM_ALPHA = (2 * DEPTH) ** 0.25

COL_Q = 0
COL_K = 512
COL_V = 640
COL_U = 768
COL_HQ = 1024
COL_FF = 1280
COL_FB = 1536
COL_HI = 1792
COL_HG = 2048
ROPE_COLS = COL_V

V7X_LANES = 128
COND_ROWS = 16
NEG_BIG = -1e30
TOKEN_TILE = 512
EXPERT_TILE = 512
SC_BATCH = 64

VMEM_LIMIT = 56 * 1024 * 1024


def _cparams(sem):
    return pltpu.CompilerParams(dimension_semantics=sem, vmem_limit_bytes=VMEM_LIMIT)


def _dot(a, b):
    return jnp.dot(a, b, preferred_element_type=F32)


def _dot_nt(a, b):
    return lax.dot_general(a, b, (((1,), (1,)), ((), ())), preferred_element_type=F32)


def _dot_tn(a, b):
    return lax.dot_general(a, b, (((0,), (0,)), ((), ())), preferred_element_type=F32)


def _split3(x):
    hi = x.astype(BF16)
    r1 = x - hi.astype(F32)
    mid = r1.astype(BF16)
    lo = (r1 - mid.astype(F32)).astype(BF16)
    return hi, mid, lo


def _dot_exact_lhs(m_bf16, x):
    hi, mid, lo = _split3(x)
    return _dot(m_bf16, hi) + _dot(m_bf16, mid) + _dot(m_bf16, lo)


def _dot_exact_rhs(x, m_bf16):
    hi, mid, lo = _split3(x)
    return _dot(hi, m_bf16) + _dot(mid, m_bf16) + _dot(lo, m_bf16)


def _dot_hp(a, b):
    a_hi = a.astype(BF16)
    a_lo = (a - a_hi.astype(F32)).astype(BF16)
    b_hi = b.astype(BF16)
    b_lo = (b - b_hi.astype(F32)).astype(BF16)
    return _dot(a_hi, b_hi) + _dot(a_hi, b_lo) + _dot(a_lo, b_hi)


def _pack_bf16_pair(lo, hi):
    return lax.bitcast_convert_type(pltpu.pack_elementwise([lo, hi], packed_dtype=BF16), I32)


def _unpack_bf16_pair(w):
    u = lax.bitcast_convert_type(w, jnp.uint32)
    lo = pltpu.unpack_elementwise(u, index=0, packed_dtype=BF16, unpacked_dtype=F32)
    hi = pltpu.unpack_elementwise(u, index=1, packed_dtype=BF16, unpacked_dtype=F32)
    return lo, hi


def _ln_plain(x, eps):
    mu = jnp.mean(x, axis=-1, keepdims=True)
    xc = x - mu
    var = jnp.mean(xc * xc, axis=-1, keepdims=True)
    return xc * lax.rsqrt(var + eps)


def _silu(x):
    return x * jax.nn.sigmoid(x)


def _adaln_kernel(c_ref, w_ref, b_ref, o_ref):
    s = _silu(c_ref[...])
    o_ref[0] = _dot_hp(s, w_ref[0]) + b_ref[0]


def _adaln(cond, w_ada, b_ada):
    return pl.pallas_call(
        _adaln_kernel,
        grid=(DEPTH, N_MOD),
        in_specs=[
            pl.BlockSpec((COND_ROWS, D_MODEL), lambda l, j: (0, 0)),
            pl.BlockSpec((1, D_MODEL, D_MODEL), lambda l, j: (l, 0, j)),
            pl.BlockSpec((1, 1, D_MODEL), lambda l, j: (l, 0, j)),
        ],
        out_specs=pl.BlockSpec((1, COND_ROWS, D_MODEL), lambda l, j: (l, 0, j)),
        out_shape=jax.ShapeDtypeStruct((DEPTH, COND_ROWS, N_MOD * D_MODEL), F32),
        compiler_params=_cparams(("arbitrary", "arbitrary")),
        name="adaln",
    )(cond, w_ada, b_ada.reshape(DEPTH, 1, N_MOD * D_MODEL))


class _Layout:
    def __init__(self, b_ctx, t_ctx, b_lat, t_lat):
        self.b_ctx, self.t_ctx, self.b_lat, self.t_lat = b_ctx, t_ctx, b_lat, t_lat
        self.n_ctx = b_ctx * t_ctx
        self.n_lat = b_lat * t_lat
        self.n = self.n_ctx + self.n_lat

    def cond_row(self, tile, tm):
        n_ctx_tiles = self.n_ctx // tm
        per_batch = self.t_lat // tm
        return jnp.where(tile < n_ctx_tiles, 0, 1 + (tile - n_ctx_tiles) // per_batch)


def _group_specs(n_arrays, tm, width, n_ctx_tiles):
    if n_arrays == 1:
        return [pl.BlockSpec((tm, width), lambda i: (i, 0))]
    return [pl.BlockSpec((tm, width), lambda i: (jnp.minimum(i, n_ctx_tiles - 1), 0)),
            pl.BlockSpec((tm, width), lambda i: (jnp.maximum(i - n_ctx_tiles, 0), 0))]


def _inproj_kernel(*refs, n_ctx_tiles):
    xs = refs[:-8]
    mod_ref, w_ref, cos_ref, sin_ref, o_ref, kc_ref, vc_ref, wb_ref = refs[-8:]

    @pl.when(pl.program_id(0) == 0)
    def _():
        wb_ref[...] = w_ref[0].astype(BF16)

    if len(xs) == 2:
        x = jnp.where(pl.program_id(0) < n_ctx_tiles, xs[0][...], xs[1][...])
    else:
        x = xs[0][...]
    shift = mod_ref[0, 0:1, :]
    scale = mod_ref[0, 1:2, :]
    h = (_ln_plain(x, ADA_EPS) * (1.0 + scale) + shift).astype(BF16)
    p = _dot(h, wb_ref[...])
    cos = cos_ref[...]
    sin = sin_ref[...]
    lane = lax.broadcasted_iota(I32, cos.shape, 1)
    first_half = (lane & 31) < 16
    for cb in range(ROPE_COLS // V7X_LANES):
        seg = p[:, cb * V7X_LANES:(cb + 1) * V7X_LANES]
        partner = jnp.where(first_half, pltpu.roll(seg, V7X_LANES - 16, 1), pltpu.roll(seg, 16, 1))
        o_ref[:, cb * V7X_LANES:(cb + 1) * V7X_LANES] = seg * cos + partner * sin
    o_ref[:, ROPE_COLS:] = p[:, ROPE_COLS:]

    @pl.when(pl.program_id(0) < n_ctx_tiles)
    def _():
        kc_ref[...] = p[:, COL_K:COL_K + KV_WIDTH]
        vc_ref[...] = p[:, COL_V:COL_V + KV_WIDTH]


def _rope_tables(lay, tm):
    t = lay.t_lat
    pos = jnp.arange(t)
    row = (pos // GRID_W).astype(F32)
    col = (pos % GRID_W).astype(F32)
    n_freq = HEAD_DIM // 4
    inv = ROPE_BASE ** (-jnp.arange(n_freq, dtype=F32) / n_freq)
    ang_r = row[:, None] * inv
    ang_c = col[:, None] * inv
    ang = jnp.concatenate([ang_r, ang_r, ang_c, ang_c], axis=1)
    sign = jnp.concatenate([-jnp.ones(n_freq), jnp.ones(n_freq), -jnp.ones(n_freq), jnp.ones(n_freq)]).astype(F32)
    cos = jnp.cos(ang)
    sin = jnp.sin(ang) * sign
    cos = jnp.concatenate([jnp.ones((tm, HEAD_DIM), F32), cos], axis=0)
    sin = jnp.concatenate([jnp.zeros((tm, HEAD_DIM), F32), sin], axis=0)
    return jnp.tile(cos, (1, 2)), jnp.tile(sin, (1, 2))


def _inproj(x, mod_l, w_in, layer, cos_t, sin_t, lay, tm):
    n_tiles = lay.n // tm
    n_ctx_tiles = lay.n_ctx // tm
    per_batch = lay.t_lat // tm

    def tbl(i):
        return jnp.where(i < n_ctx_tiles, 0, 1 + (i - n_ctx_tiles) % per_batch)

    xs = x if isinstance(x, tuple) else (x,)
    return pl.pallas_call(
        functools.partial(_inproj_kernel, n_ctx_tiles=n_ctx_tiles),
        grid=(n_tiles,),
        in_specs=[
            *_group_specs(len(xs), tm, D_MODEL, n_ctx_tiles),
            pl.BlockSpec((1, N_MOD, D_MODEL), lambda i: (lay.cond_row(i, tm), 0, 0)),
            pl.BlockSpec((1, D_MODEL, IN_WIDTH), lambda i: (layer, 0, 0), pipeline_mode=pl.Buffered(1)),
            pl.BlockSpec((tm, V7X_LANES), lambda i: (tbl(i), 0)),
            pl.BlockSpec((tm, V7X_LANES), lambda i: (tbl(i), 0)),
        ],
        out_specs=[
            pl.BlockSpec((tm, IN_WIDTH), lambda i: (i, 0)),
            pl.BlockSpec((tm, KV_WIDTH), lambda i: (jnp.minimum(i, n_ctx_tiles - 1), 0)),
            pl.BlockSpec((tm, KV_WIDTH), lambda i: (jnp.minimum(i, n_ctx_tiles - 1), 0)),
        ],
        out_shape=[
            jax.ShapeDtypeStruct((lay.n, IN_WIDTH), F32),
            jax.ShapeDtypeStruct((lay.n_ctx, KV_WIDTH), F32),
            jax.ShapeDtypeStruct((lay.n_ctx, KV_WIDTH), F32),
        ],
        scratch_shapes=[pltpu.VMEM((D_MODEL, IN_WIDTH), BF16)],
        compiler_params=_cparams(("arbitrary",)),
        name="inproj",
    )(*xs, mod_l, w_in, cos_t, sin_t)


def _attn_kernel(sink_ref, q_ref, *refs, n_local, has_ctx, t_total):
    o_ref = refs[-1]
    k_refs = refs[:n_local]
    v_refs = refs[n_local:2 * n_local]
    tq = q_ref.shape[0]
    scale = HEAD_DIM ** -0.5
    k_parts = [kr[...] for kr in k_refs]
    v_parts = [vr[...] for vr in v_refs]
    if has_ctx:
        k_parts.append(refs[2 * n_local][0])
        v_parts.append(refs[2 * n_local + 1][0])
    kall = jnp.concatenate(k_parts, axis=0) if len(k_parts) > 1 else k_parts[0]
    vall = jnp.concatenate(v_parts, axis=0) if len(v_parts) > 1 else v_parts[0]
    nk = kall.shape[0]
    k_sw = pltpu.roll(kall, HEAD_DIM, 1)
    v_sw = pltpu.roll(vall, HEAD_DIM, 1)
    lo_half = lax.broadcasted_iota(I32, (1, V7X_LANES), 1) < HEAD_DIM
    er = jnp.where(lax.broadcasted_iota(I32, (2 * nk, V7X_LANES), 0) < nk, 0, 1)
    el = jnp.where(lax.broadcasted_iota(I32, (2 * nk, V7X_LANES), 1) < HEAD_DIM, 0, 1)
    ones_blk = jnp.where(er == el, 1.0, 0.0).astype(BF16)
    rows = 2 * tq
    if n_local > 1:
        i = pl.program_id(1)
        r = lax.broadcasted_iota(I32, (rows, nk), 0) & (tq - 1)
        c = lax.broadcasted_iota(I32, (rows, nk), 1)
        kpos = (i - 1) * tq + c
        rel = c - tq - r
        local = (jnp.abs(rel) <= WINDOW) & (kpos >= 0) & (kpos < t_total)
        valid = jnp.logical_or(local, c >= n_local * tq)
    else:
        valid = None
    for g in range(N_KV_HEADS):
        k_own, k_oth = (kall, k_sw) if g == 0 else (k_sw, kall)
        v_own, v_oth = (vall, v_sw) if g == 0 else (v_sw, vall)
        k2 = jnp.concatenate([jnp.where(lo_half, k_own, 0.0), jnp.where(lo_half, 0.0, k_oth)], axis=0).astype(BF16)
        v2 = jnp.concatenate([jnp.where(lo_half, v_own, 0.0), jnp.where(lo_half, 0.0, v_oth)], axis=0).astype(BF16)
        v2e = jnp.concatenate([v2, ones_blk], axis=1)
        pairs = [2 * g, 2 * g + 1]
        qq = jnp.concatenate([q_ref[:, p * V7X_LANES:(p + 1) * V7X_LANES] for p in pairs], axis=0)
        qq = (qq * scale).astype(BF16)
        sink_a = jnp.concatenate([jnp.full((tq, 1), sink_ref[0, 2 * p], F32) for p in pairs], axis=0)
        sink_b = jnp.concatenate([jnp.full((tq, 1), sink_ref[0, 2 * p + 1], F32) for p in pairs], axis=0)
        s = _dot_nt(qq, k2)
        s_a = s[:, :nk]
        s_b = s[:, nk:]
        if valid is not None:
            s_a = jnp.where(valid, s_a, NEG_BIG)
            s_b = jnp.where(valid, s_b, NEG_BIG)
        m_a = jnp.maximum(jnp.max(s_a, axis=1, keepdims=True), sink_a)
        m_b = jnp.maximum(jnp.max(s_b, axis=1, keepdims=True), sink_b)
        pe = jnp.concatenate([jnp.exp(s_a - m_a).astype(BF16), jnp.exp(s_b - m_b).astype(BF16)], axis=1)
        acc = _dot(pe, v2e)
        sink_term = jnp.where(lo_half, jnp.exp(sink_a - m_a), jnp.exp(sink_b - m_b))
        o = acc[:, :V7X_LANES] / (acc[:, V7X_LANES:] + sink_term)
        for j, p in enumerate(pairs):
            o_ref[:, p * V7X_LANES:(p + 1) * V7X_LANES] = o[j * tq:(j + 1) * tq].astype(o_ref.dtype)


def _attn_context(proj, sink_l, lay):
    t = lay.t_ctx
    kb, vb = COL_K // KV_WIDTH, COL_V // KV_WIDTH
    body = functools.partial(_attn_kernel, n_local=1, has_ctx=False, t_total=t)
    return pl.pallas_call(
        body,
        grid=(lay.b_ctx,),
        in_specs=[
            pl.BlockSpec(memory_space=pltpu.SMEM),
            pl.BlockSpec((t, ATTN_WIDTH), lambda b: (b, 0)),
            pl.BlockSpec((t, KV_WIDTH), lambda b: (b, kb)),
            pl.BlockSpec((t, KV_WIDTH), lambda b: (b, vb)),
        ],
        out_specs=pl.BlockSpec((t, ATTN_WIDTH), lambda b: (b, 0)),
        out_shape=jax.ShapeDtypeStruct((lay.n_ctx, ATTN_WIDTH), BF16),
        compiler_params=_cparams(("arbitrary",)),
        name="attn_ctx",
    )(sink_l, proj, proj, proj)


def _attn_latent(proj, k_ctx, v_ctx, sink_l, lay):
    t = lay.t_lat
    tq = ATTN_BLOCK
    nq = t // tq
    base = lay.n_ctx // tq
    kb, vb = COL_K // KV_WIDTH, COL_V // KV_WIDTH
    past = k_ctx.shape[1]

    def rows(off):
        return lambda b, i: base + b * nq + jnp.clip(i + off, 0, nq - 1)

    def kv_specs(col):
        return [pl.BlockSpec((tq, KV_WIDTH), (lambda b, i, f=rows(off): (f(b, i), col))) for off in (-1, 0, 1)]

    body = functools.partial(_attn_kernel, n_local=3, has_ctx=True, t_total=t)
    return pl.pallas_call(
        body,
        grid=(lay.b_lat, nq),
        in_specs=[
            pl.BlockSpec(memory_space=pltpu.SMEM),
            pl.BlockSpec((tq, ATTN_WIDTH), lambda b, i: (base + b * nq + i, 0)),
            *kv_specs(kb),
            *kv_specs(vb),
            pl.BlockSpec((1, past, KV_WIDTH), lambda b, i: (b, 0, 0)),
            pl.BlockSpec((1, past, KV_WIDTH), lambda b, i: (b, 0, 0)),
        ],
        out_specs=pl.BlockSpec((tq, ATTN_WIDTH), lambda b, i: (b * nq + i, 0)),
        out_shape=jax.ShapeDtypeStruct((lay.n_lat, ATTN_WIDTH), BF16),
        compiler_params=_cparams(("arbitrary", "arbitrary")),
        name="attn_lat",
    )(sink_l, proj, proj, proj, proj, proj, proj, proj, k_ctx, v_ctx)


def _fourier_kernel(cs_ref, u_ref, cc_ref, sc_ref, o_ref, csb_ref, *, scale):
    @pl.when(pl.program_id(1) == 0)
    def _():
        csb_ref[...] = cs_ref[...].astype(BF16)

    z = u_ref[...].astype(BF16)
    zc = _dot(z, cc_ref[...].astype(BF16)).astype(BF16)
    zs = _dot(z, sc_ref[...].astype(BF16)).astype(BF16)
    zz = jnp.concatenate([zc, zs], axis=0)
    o_ref[...] = (_dot(csb_ref[...], zz) * scale).astype(o_ref.dtype)


@functools.lru_cache(maxsize=None)
def _dft_tables(t):
    idx = np.arange(t, dtype=np.int64)
    ang = 2.0 * np.pi * ((idx[:, None] * idx[None, :]) % t).astype(np.float64) / t
    cs = np.concatenate([np.cos(ang), -np.sin(ang)], axis=1).astype(np.float32)
    cw = FOURIER_WIDTH // FOURIER_GROUPS
    cidx = np.arange(cw, dtype=np.int64)
    cang = 2.0 * np.pi * ((cidx[:, None] * cidx[None, :]) % cw).astype(np.float64) / cw
    eye = np.eye(FOURIER_GROUPS)
    cc = np.kron(eye, np.cos(cang)).astype(np.float32)
    sc = np.kron(eye, np.sin(cang)).astype(np.float32)
    return cs, cc, sc


def _fourier(proj, row0, b, t, tm, name):
    cs, cc, sc = _dft_tables(t)
    cw = FOURIER_WIDTH // FOURIER_GROUPS
    nt = t // tm
    ub = COL_U // FOURIER_WIDTH
    base = row0 // t
    body = functools.partial(_fourier_kernel, scale=1.0 / math.sqrt(t * cw))
    return pl.pallas_call(
        body,
        grid=(nt, b),
        in_specs=[
            pl.BlockSpec((tm, 2 * t), lambda i, bb: (i, 0)),
            pl.BlockSpec((t, FOURIER_WIDTH), lambda i, bb: (base + bb, ub)),
            pl.BlockSpec((FOURIER_WIDTH, FOURIER_WIDTH), lambda i, bb: (0, 0)),
            pl.BlockSpec((FOURIER_WIDTH, FOURIER_WIDTH), lambda i, bb: (0, 0)),
        ],
        out_specs=pl.BlockSpec((tm, FOURIER_WIDTH), lambda i, bb: (bb * nt + i, 0)),
        out_shape=jax.ShapeDtypeStruct((b * t, FOURIER_WIDTH), BF16),
        scratch_shapes=[pltpu.VMEM((tm, 2 * t), BF16)],
        compiler_params=_cparams(("arbitrary", "arbitrary")),
        name=name,
    )(jnp.asarray(cs), proj, jnp.asarray(cc), jnp.asarray(sc))


HGRN_LEVELS = (64, 32, 16, 8, 4, 2)
HGRN_SAFE_RANGE = 80.0


@functools.lru_cache(maxsize=None)
def _hgrn_tables():
    c = HGRN_CHUNK
    return np.stack([np.tril(np.ones((c, c))), np.triu(np.ones((c, c)))]).astype(np.float32)


def _boundary_rows(b, m, reverse):
    c, w = b.shape
    half = m // 2
    off = half if reverse else half - 1
    if m >= 16:
        return jnp.concatenate(
            [jnp.broadcast_to(b[s + off:s + off + 1], (m, w)) for s in range(0, c, m)], axis=0)
    sub = lax.broadcasted_iota(I32, (c, w), 0) & 7
    b3 = b.reshape(c // 8, 8, w)

    def bcast(j):
        return jnp.broadcast_to(b3[:, j:j + 1, :], (c // 8, 8, w)).reshape(c, w)

    if m == 8:
        return bcast(off)
    if m == 4:
        return jnp.where(sub < 4, bcast(off), bcast(4 + off))
    assert m == 2
    if reverse:
        return jnp.where((sub & 1) == 1, b, pltpu.roll(b, c - 1, 0))
    return jnp.where((sub & 1) == 0, b, pltpu.roll(b, 1, 0))


class _HgrnDir:
    def __init__(self, q, z, v, loglb, log1mlb, onemlb, cum, reverse):
        c = HGRN_CHUNK
        self.q, self.reverse = q, reverse
        log_sig = jnp.minimum(z, 0.0) - jnp.log1p(jnp.exp(-jnp.abs(z)))
        bb = log1mlb + log_sig
        mx = jnp.maximum(loglb, bb)
        lf = mx + jnp.log1p(jnp.exp(-jnp.abs(loglb - bb)))
        self.kk = onemlb * jax.nn.sigmoid(-z)
        self.b = _dot_exact_lhs(cum, lf)
        b_end = self.b[0:1] if reverse else self.b[c - 1:c]
        self.qt = (q * jnp.exp(self.b)).astype(BF16)
        self.kt = (self.kk * jnp.exp(b_end - self.b)).astype(BF16)
        self.decay = jnp.exp(b_end)
        self.vb = v.astype(BF16)
        mid = c // 2 if reverse else c // 2 - 1
        self.rel = self.b - self.b[mid:mid + 1]
        self.span = jnp.max(jnp.abs(self.rel))

    def tree_decay_matrices(self):
        c = HGRN_CHUNK
        q, kk, b = self.q, self.kk, self.b
        row = lax.broadcasted_iota(I32, (c, 1), 0)
        ti = lax.broadcasted_iota(I32, (c, c), 0)
        si = lax.broadcasted_iota(I32, (c, c), 1)
        qb = q.astype(BF16)
        kb = kk.astype(BF16)
        heads = [slice(h * HGRN_DK, (h + 1) * HGRN_DK) for h in range(HGRN_HEADS)]
        acc = [jnp.where(ti == si, _dot_nt(qb[:, sl], kb[:, sl]), 0.0) for sl in heads]
        for m in HGRN_LEVELS:
            r = _boundary_rows(b, m, self.reverse)
            upper = (row & (m - 1)) >= (m // 2)
            q_side = jnp.logical_not(upper) if self.reverse else upper
            e = jnp.exp(jnp.where(q_side, b - r, r - b))
            qf = jnp.where(q_side, q * e, 0.0).astype(BF16)
            kf = jnp.where(q_side, 0.0, kk * e).astype(BF16)
            same_block = (ti & -m) == (si & -m)
            for h, sl in enumerate(heads):
                acc[h] = acc[h] + jnp.where(same_block, _dot_nt(qf[:, sl], kf[:, sl]), 0.0)
        return acc

    def midpoint_decay_matrices(self):
        c = HGRN_CHUNK
        ti = lax.broadcasted_iota(I32, (c, c), 0)
        si = lax.broadcasted_iota(I32, (c, c), 1)
        qm = (self.q * jnp.exp(self.rel)).astype(BF16)
        km = (self.kk * jnp.exp(-self.rel)).astype(BF16)
        causal = (si >= ti) if self.reverse else (si <= ti)
        return [jnp.where(causal, _dot_nt(qm[:, h * HGRN_DK:(h + 1) * HGRN_DK], km[:, h * HGRN_DK:(h + 1) * HGRN_DK]), 0.0)
                for h in range(HGRN_HEADS)]

    def outputs(self, a_heads, st_ref, d):
        outs = []
        for h in range(HGRN_HEADS):
            sl = slice(h * HGRN_DK, (h + 1) * HGRN_DK)
            st = st_ref[d, h]
            o = _dot_nt(self.qt[:, sl], st.astype(BF16)) + _dot(a_heads[h].astype(BF16), self.vb[:, sl])
            st_ref[d, h] = st * self.decay[:, sl] + _dot_tn(self.vb[:, sl], self.kt[:, sl])
            outs.append(o)
        return jnp.concatenate(outs, axis=1)


def _hgrn_kernel(hq_ref, ff_ref, fb_ref, hi_ref, hg_ref, lbp_ref, gn_ref, mall_ref, ones_ref, s0_ref,
                 rec_ref, sfin_ref, st_ref, of_ref, ob_ref, *, t):
    c = HGRN_CHUNK
    n = t // c
    st_ref[...] = s0_ref[0]

    def body(ci, carry):
        rf = pl.ds(pl.multiple_of(ci * c, c), c)
        rb = pl.ds(pl.multiple_of((n - 1 - ci) * c, c), c)
        fwd = _HgrnDir(hq_ref[rf, :], ff_ref[rf, :], hi_ref[rf, :], lbp_ref[0, 0:1, :], lbp_ref[0, 1:2, :],
                       lbp_ref[0, 2:3, :], mall_ref[0].astype(BF16), False)
        bwd = _HgrnDir(hq_ref[rb, :], fb_ref[rb, :], hi_ref[rb, :], lbp_ref[1, 0:1, :], lbp_ref[1, 1:2, :],
                       lbp_ref[1, 2:3, :], mall_ref[1].astype(BF16), True)
        a_f, a_b = lax.cond(
            jnp.maximum(fwd.span, bwd.span) <= HGRN_SAFE_RANGE,
            lambda: (fwd.midpoint_decay_matrices(), bwd.midpoint_decay_matrices()),
            lambda: (fwd.tree_decay_matrices(), bwd.tree_decay_matrices()))
        of_ref[rf, :] = fwd.outputs(a_f, st_ref, 0)
        ob_ref[rb, :] = bwd.outputs(a_b, st_ref, 1)
        return carry

    lax.fori_loop(0, n, body, 0)
    sfin_ref[0] = st_ref[...]
    o = of_ref[...] + ob_ref[...]
    ms = _dot_exact_rhs(o * o, ones_ref[...].astype(BF16)) * (1.0 / HGRN_DK)
    o = o * lax.rsqrt(ms + GN_EPS) * gn_ref[...]
    rec_ref[...] = (o * _silu(hg_ref[...])).astype(rec_ref.dtype)


def _hgrn(proj, row0, b, t, lbp, gn_row, s0t, name):
    base = row0 // t
    m_all = jnp.asarray(_hgrn_tables())
    ones_bd = jnp.asarray(np.kron(np.eye(HGRN_HEADS), np.ones((HGRN_DK, HGRN_DK))).astype(np.float32))

    def col(cstart):
        return pl.BlockSpec((t, HGRN_WIDTH), lambda bb, cb=cstart // HGRN_WIDTH: (base + bb, cb))

    const2 = lambda bb: (0, 0)
    const3 = lambda bb: (0, 0, 0)
    st_shape = (2, HGRN_HEADS, HGRN_DK, HGRN_DK)
    body = functools.partial(_hgrn_kernel, t=t)
    return pl.pallas_call(
        body,
        grid=(b,),
        in_specs=[
            col(COL_HQ), col(COL_FF), col(COL_FB), col(COL_HI), col(COL_HG),
            pl.BlockSpec((2, 3, HGRN_WIDTH), const3),
            pl.BlockSpec((1, HGRN_WIDTH), const2),
            pl.BlockSpec(m_all.shape, const3),
            pl.BlockSpec(ones_bd.shape, const2),
            pl.BlockSpec((1,) + st_shape, lambda bb: (bb, 0, 0, 0, 0)),
        ],
        out_specs=[
            pl.BlockSpec((t, HGRN_WIDTH), lambda bb: (bb, 0)),
            pl.BlockSpec((1,) + st_shape, lambda bb: (bb, 0, 0, 0, 0)),
        ],
        out_shape=[
            jax.ShapeDtypeStruct((b * t, HGRN_WIDTH), BF16),
            jax.ShapeDtypeStruct((b,) + st_shape, F32),
        ],
        scratch_shapes=[
            pltpu.VMEM(st_shape, F32),
            pltpu.VMEM((t, HGRN_WIDTH), F32),
            pltpu.VMEM((t, HGRN_WIDTH), F32),
        ],
        compiler_params=_cparams(("arbitrary",)),
        name=name,
    )(proj, proj, proj, proj, proj, lbp, gn_row, m_all, ones_bd, s0t)


def _outproj_kernel(attn_c_ref, attn_l_ref, four_c_ref, four_l_ref, rec_c_ref, rec_l_ref, *refs, n_ctx_tiles):
    xs = refs[:-14]
    (mod_ref, w_ref, g_ref, b_ref, rw_ref, rb_ref, x1_ref, hp_ref, meta_ref, gate_ref, cnt_ref, wb_ref, tri_ref,
     run_ref) = refs[-14:]
    tm = x1_ref.shape[0]
    is_ctx = pl.program_id(0) < n_ctx_tiles
    x_in = jnp.where(is_ctx, xs[0][...], xs[1][...]) if len(xs) == 2 else xs[0][...]
    attn = jnp.where(is_ctx, attn_c_ref[...], attn_l_ref[...])
    four = jnp.where(is_ctx, four_c_ref[...], four_l_ref[...])
    rec = jnp.where(is_ctx, rec_c_ref[...], rec_l_ref[...])

    @pl.when(pl.program_id(0) == 0)
    def _():
        wb_ref[...] = w_ref[0].astype(BF16)
        r = lax.broadcasted_iota(I32, (tm, tm), 0)
        c = lax.broadcasted_iota(I32, (tm, tm), 1)
        tri_ref[...] = jnp.where(r < c, 1.0, 0.0).astype(BF16)
        run_ref[...] = jnp.zeros_like(run_ref)

    out = _dot(attn, wb_ref[0:ATTN_WIDTH, :])
    out = out + _dot(four, wb_ref[ATTN_WIDTH:ATTN_WIDTH + FOURIER_WIDTH, :])
    out = out + _dot(rec, wb_ref[ATTN_WIDTH + FOURIER_WIDTH:, :])
    gate1 = mod_ref[0, 2:3, :]
    y = DEEPNORM_ALPHA * x_in + gate1 * out
    x1 = _ln_plain(y, LN_EPS) * g_ref[...] + b_ref[...]
    x1_ref[...] = x1
    h2 = _ln_plain(x1, ADA_EPS) * (1.0 + mod_ref[0, 4:5, :]) + mod_ref[0, 3:4, :]
    hp_ref[...] = _pack_bf16_pair(h2[:, :HALF_D], h2[:, HALF_D:])

    h_hi = h2.astype(BF16)
    h_lo = (h2 - h_hi.astype(F32)).astype(BF16)
    rwt = rw_ref[...]
    w_hi = rwt.astype(BF16)
    w_lo = (rwt - w_hi.astype(F32)).astype(BF16)
    scores = jax.nn.sigmoid(_dot_nt(w_hi, h_hi) + _dot_nt(w_hi, h_lo) + _dot_nt(w_lo, h_hi))
    remaining = scores + rb_ref[...]
    eidx = lax.broadcasted_iota(I32, scores.shape, 0).astype(F32)
    chosen = jnp.zeros(scores.shape, jnp.bool_)
    picks = []
    for _ in range(TOP_K):
        mx = jnp.max(remaining, axis=0, keepdims=True)
        first = jnp.min(jnp.where(remaining == mx, eidx, float(N_EXPERTS)), axis=0, keepdims=True)
        pick = eidx == first
        picks.append((pick, first))
        chosen = jnp.logical_or(chosen, pick)
        remaining = jnp.where(pick, -jnp.inf, remaining)
    sel = jnp.where(chosen, scores, 0.0)
    gates = sel / jnp.sum(sel, axis=0, keepdims=True) * ROUTED_SCALE

    onehot = jnp.where(chosen, 1.0, 0.0)
    rank = run_ref[...] + _dot(onehot.astype(BF16), tri_ref[...])
    run_ref[...] += jnp.sum(onehot, axis=1, keepdims=True)
    cnt_ref[...] = run_ref[...]

    ids, rks, gks = [], [], []
    for pick, first in picks:
        ids.append(first.astype(I32))
        rks.append(jnp.sum(jnp.where(pick, rank, 0.0), axis=0, keepdims=True).astype(I32))
        gks.append(jnp.sum(jnp.where(pick, gates, 0.0), axis=0, keepdims=True))
    meta_ref[...] = jnp.concatenate(ids + rks, axis=0)
    gate_ref[...] = jnp.concatenate(gks, axis=0)


def _outproj(attn, four, rec, x, mod_l, w_out, layer, g1, b1, rw, rb, lay, tm):
    n_tiles = lay.n // tm
    n_ctx_tiles = lay.n_ctx // tm
    row = lambda i: (i, 0)
    const = lambda i: (0, 0)
    xs = x if isinstance(x, tuple) else (x,)
    return pl.pallas_call(
        functools.partial(_outproj_kernel, n_ctx_tiles=n_ctx_tiles),
        grid=(n_tiles,),
        in_specs=[
            *_group_specs(2, tm, ATTN_WIDTH, n_ctx_tiles),
            *_group_specs(2, tm, FOURIER_WIDTH, n_ctx_tiles),
            *_group_specs(2, tm, HGRN_WIDTH, n_ctx_tiles),
            *_group_specs(len(xs), tm, D_MODEL, n_ctx_tiles),
            pl.BlockSpec((1, N_MOD, D_MODEL), lambda i: (lay.cond_row(i, tm), 0, 0)),
            pl.BlockSpec((1, D_MODEL, D_MODEL), lambda i: (layer, 0, 0)),
            pl.BlockSpec((1, D_MODEL), const),
            pl.BlockSpec((1, D_MODEL), const),
            pl.BlockSpec((N_EXPERTS, D_MODEL), const),
            pl.BlockSpec((N_EXPERTS, 1), const),
        ],
        out_specs=[
            pl.BlockSpec((tm, D_MODEL), row),
            pl.BlockSpec((tm, HALF_D), row),
            pl.BlockSpec((2 * TOP_K, tm), lambda i: (0, i)),
            pl.BlockSpec((TOP_K, tm), lambda i: (0, i)),
            pl.BlockSpec((N_EXPERTS, 1), const),
        ],
        out_shape=[
            jax.ShapeDtypeStruct((lay.n, D_MODEL), F32),
            jax.ShapeDtypeStruct((lay.n, HALF_D), I32),
            jax.ShapeDtypeStruct((2 * TOP_K, lay.n), I32),
            jax.ShapeDtypeStruct((TOP_K, lay.n), F32),
            jax.ShapeDtypeStruct((N_EXPERTS, 1), F32),
        ],
        scratch_shapes=[
            pltpu.VMEM((D_MODEL, D_MODEL), BF16),
            pltpu.VMEM((tm, tm), BF16),
            pltpu.VMEM((N_EXPERTS, 1), F32),
        ],
        compiler_params=_cparams(("arbitrary",)),
        name="outproj_router",
    )(*attn, *four, *rec, *xs, mod_l, w_out, g1, b1, rw, rb)


def _sc_workers():
    info = plsc.get_sparse_core_info()
    return info.num_cores, info.num_cores * info.num_subcores


def _sc_scatter_rows(rows, pos_b, r_out):
    nc, nw = _sc_workers()
    n, w = rows.shape
    nbt, copies, _ = pos_b.shape
    assert nbt * SC_BATCH == n and nbt % (2 * nw) == 0
    per_w = nbt // nw
    mesh = plsc.VectorSubcoreMesh(core_axis_name="c", subcore_axis_name="s")

    @functools.partial(
        pl.kernel, mesh=mesh, out_type=jax.ShapeDtypeStruct((r_out, w), rows.dtype),
        scratch_types=[pltpu.VMEM((copies, SC_BATCH), I32), pltpu.VMEM((copies, SC_BATCH), I32),
                       pltpu.VMEM((SC_BATCH, w), rows.dtype), pltpu.VMEM((SC_BATCH, w), rows.dtype),
                       pltpu.SemaphoreType.DMA, pltpu.SemaphoreType.DMA,
                       pltpu.SemaphoreType.DMA, pltpu.SemaphoreType.DMA],
        name="sc_dispatch")
    def k(rows_hbm, pos_hbm, out_hbm, idx_a, idx_b, rows_a, rows_b, sem_ra, sem_rb, sem_sa, sem_sb):
        wid = lax.axis_index("s") * nc + lax.axis_index("c")
        first = wid * per_w

        def reads(j, idx_v, rows_v, sem):
            bt = first + j
            return (pltpu.make_async_copy(pos_hbm.at[bt], idx_v, sem),
                    pltpu.make_async_copy(rows_hbm.at[pl.ds(bt * SC_BATCH, SC_BATCH)], rows_v, sem))

        def scatters(idx_v, rows_v, sem):
            return [pltpu.make_async_copy(rows_v, out_hbm.at[idx_v.at[q]], sem) for q in range(copies)]

        def start(descs):
            for d in descs:
                d.start()

        def wait(descs):
            for d in descs:
                d.wait()

        start(reads(0, idx_a, rows_a, sem_ra))

        @pl.loop(0, per_w // 2)
        def _(p):
            j0 = 2 * p
            j1 = j0 + 1

            @pl.when(p > 0)
            def _():
                wait(scatters(idx_b, rows_b, sem_sb))

            start(reads(j1, idx_b, rows_b, sem_rb))
            wait(reads(j0, idx_a, rows_a, sem_ra))
            start(scatters(idx_a, rows_a, sem_sa))
            wait(reads(j1, idx_b, rows_b, sem_rb))
            start(scatters(idx_b, rows_b, sem_sb))
            wait(scatters(idx_a, rows_a, sem_sa))

            @pl.when(p + 1 < per_w // 2)
            def _():
                start(reads(j0 + 2, idx_a, rows_a, sem_ra))

        wait(scatters(idx_b, rows_b, sem_sb))

    return k(rows, pos_b)


def _sc_gather_rows(table, idx):
    nc, nw = _sc_workers()
    r = idx.shape[0]
    w = table.shape[1]
    assert r % (2 * nw * SC_BATCH) == 0
    per_w = r // nw
    nb = per_w // SC_BATCH
    mesh = plsc.VectorSubcoreMesh(core_axis_name="c", subcore_axis_name="s")

    @functools.partial(
        pl.kernel, mesh=mesh, out_type=jax.ShapeDtypeStruct((r, w), table.dtype),
        scratch_types=[pltpu.VMEM((per_w,), I32),
                       pltpu.VMEM((SC_BATCH, w), table.dtype), pltpu.VMEM((SC_BATCH, w), table.dtype),
                       pltpu.SemaphoreType.DMA, pltpu.SemaphoreType.DMA,
                       pltpu.SemaphoreType.DMA, pltpu.SemaphoreType.DMA],
        name="sc_combine")
    def k(table_hbm, idx_hbm, out_hbm, idx_v, rows_a, rows_b, sem_ga, sem_gb, sem_wa, sem_wb):
        wid = lax.axis_index("s") * nc + lax.axis_index("c")
        base = wid * per_w
        pltpu.sync_copy(idx_hbm.at[pl.ds(base, per_w)], idx_v)

        def gather(j, rows_v, sem):
            return pltpu.make_async_copy(table_hbm.at[idx_v.at[pl.ds(j * SC_BATCH, SC_BATCH)]], rows_v, sem)

        def write(j, rows_v, sem):
            return pltpu.make_async_copy(rows_v, out_hbm.at[pl.ds(base + j * SC_BATCH, SC_BATCH)], sem)

        gather(0, rows_a, sem_ga).start()

        @pl.loop(0, nb // 2)
        def _(p):
            j0 = 2 * p
            j1 = j0 + 1

            @pl.when(p > 0)
            def _():
                write(j1 - 2, rows_b, sem_wb).wait()

            gather(j1, rows_b, sem_gb).start()
            gather(j0, rows_a, sem_ga).wait()
            write(j0, rows_a, sem_wa).start()
            gather(j1, rows_b, sem_gb).wait()
            write(j1, rows_b, sem_wb).start()
            write(j0, rows_a, sem_wa).wait()

            @pl.when(p + 1 < nb // 2)
            def _():
                gather(j0 + 2, rows_a, sem_ga).start()

        write(nb - 1, rows_b, sem_wb).wait()

    return k(table, idx)


def _experts_kernel(te_ref, na_ref, x_ref, w1_ref, w3_ref, w2_ref, o_ref, w1b_ref, w3b_ref, w2b_ref):
    j = pl.program_id(0)
    na = na_ref[0]
    jj = jnp.minimum(j, na - 1)
    e = te_ref[jj]
    prev = te_ref[jnp.maximum(jj - 1, 0)]
    active = j < na

    @pl.when(jnp.logical_and(active, jnp.logical_or(j == 0, e != prev)))
    def _():
        w1b_ref[...] = w1_ref[0, 0].astype(BF16)
        w3b_ref[...] = w3_ref[0, 0].astype(BF16)
        w2b_ref[...] = w2_ref[0, 0].astype(BF16)

    @pl.when(active)
    def _():
        lo, hi = _unpack_bf16_pair(x_ref[...])
        lo = lo.astype(BF16)
        hi = hi.astype(BF16)
        a = _dot(lo, w1b_ref[0:HALF_D, :]) + _dot(hi, w1b_ref[HALF_D:, :])
        b = _dot(lo, w3b_ref[0:HALF_D, :]) + _dot(hi, w3b_ref[HALF_D:, :])
        y = _dot((_silu(a) * b).astype(BF16), w2b_ref[...])
        o_ref[...] = _pack_bf16_pair(y[:, :HALF_D], y[:, HALF_D:])


def _experts(xs, tile_expert, n_active, w1, w3, w2, layer, tm):
    r = xs.shape[0]
    n_tiles = r // tm

    def xmap(j, te, na):
        return (jnp.minimum(j, na[0] - 1), 0)

    def wmap(j, te, na):
        return (layer, te[jnp.minimum(j, na[0] - 1)], 0, 0)

    grid_spec = pltpu.PrefetchScalarGridSpec(
        num_scalar_prefetch=2,
        grid=(n_tiles,),
        in_specs=[
            pl.BlockSpec((tm, HALF_D), xmap),
            pl.BlockSpec((1, 1, D_MODEL, EXPERT_FF), wmap),
            pl.BlockSpec((1, 1, D_MODEL, EXPERT_FF), wmap),
            pl.BlockSpec((1, 1, EXPERT_FF, D_MODEL), wmap),
        ],
        out_specs=pl.BlockSpec((tm, HALF_D), xmap),
        scratch_shapes=[
            pltpu.VMEM((D_MODEL, EXPERT_FF), BF16),
            pltpu.VMEM((D_MODEL, EXPERT_FF), BF16),
            pltpu.VMEM((EXPERT_FF, D_MODEL), BF16),
        ],
    )
    return pl.pallas_call(
        _experts_kernel,
        grid_spec=grid_spec,
        out_shape=jax.ShapeDtypeStruct((r, HALF_D), I32),
        compiler_params=_cparams(("arbitrary",)),
        name="experts",
    )(tile_expert, n_active, xs, w1, w3, w2)


def _combine_kernel(yp_ref, gate_ref, hp_ref, sw1_ref, sw3_ref, sw2_ref, x_ref, mod_ref, g_ref, b_ref, *refs,
                    n_ctx_tiles):
    outs = refs[:-3]
    w1b_ref, w3b_ref, w2b_ref = refs[-3:]

    @pl.when(pl.program_id(0) == 0)
    def _():
        w1b_ref[...] = sw1_ref[...].astype(BF16)
        w3b_ref[...] = sw3_ref[...].astype(BF16)
        w2b_ref[...] = sw2_ref[...].astype(BF16)

    lo, hi = _unpack_bf16_pair(hp_ref[...])
    lo = lo.astype(BF16)
    hi = hi.astype(BF16)
    a = _dot(lo, w1b_ref[0:HALF_D, :]) + _dot(hi, w1b_ref[HALF_D:, :])
    b = _dot(lo, w3b_ref[0:HALF_D, :]) + _dot(hi, w3b_ref[HALF_D:, :])
    shared = _dot((_silu(a) * b).astype(BF16), w2b_ref[...])
    acc_lo = shared[:, :HALF_D]
    acc_hi = shared[:, HALF_D:]
    gates = gate_ref[...]
    for k in range(TOP_K):
        ylo, yhi = _unpack_bf16_pair(yp_ref[k])
        gk = gates[:, k:k + 1]
        acc_lo = acc_lo + gk * ylo
        acc_hi = acc_hi + gk * yhi
    moe = jnp.concatenate([acc_lo, acc_hi], axis=1)
    y = DEEPNORM_ALPHA * x_ref[...] + mod_ref[0, 5:6, :] * moe
    res = _ln_plain(y, LN_EPS) * g_ref[...] + b_ref[...]
    if len(outs) == 1:
        outs[0][...] = res
    else:
        @pl.when(pl.program_id(0) < n_ctx_tiles)
        def _():
            outs[0][...] = res

        @pl.when(pl.program_id(0) >= n_ctx_tiles)
        def _():
            outs[1][...] = res


def _combine(yp, gate8, hp, sw1, sw3, sw2, x1, mod_l, g2, b2, lay, tm, split_out):
    n_tiles = lay.n // tm
    n_ctx_tiles = lay.n_ctx // tm
    row = lambda i: (i, 0)
    const = lambda i: (0, 0)
    if split_out:
        out_specs = _group_specs(2, tm, D_MODEL, n_ctx_tiles)
        out_shape = [jax.ShapeDtypeStruct((lay.n_ctx, D_MODEL), F32), jax.ShapeDtypeStruct((lay.n_lat, D_MODEL), F32)]
    else:
        out_specs = pl.BlockSpec((tm, D_MODEL), row)
        out_shape = jax.ShapeDtypeStruct((lay.n, D_MODEL), F32)
    return pl.pallas_call(
        functools.partial(_combine_kernel, n_ctx_tiles=n_ctx_tiles),
        grid=(n_tiles,),
        in_specs=[
            pl.BlockSpec((TOP_K, tm, HALF_D), lambda i: (0, i, 0)),
            pl.BlockSpec((tm, TOP_K), row),
            pl.BlockSpec((tm, HALF_D), row),
            pl.BlockSpec((D_MODEL, EXPERT_FF), const),
            pl.BlockSpec((D_MODEL, EXPERT_FF), const),
            pl.BlockSpec((EXPERT_FF, D_MODEL), const),
            pl.BlockSpec((tm, D_MODEL), row),
            pl.BlockSpec((1, N_MOD, D_MODEL), lambda i: (lay.cond_row(i, tm), 0, 0)),
            pl.BlockSpec((1, D_MODEL), const),
            pl.BlockSpec((1, D_MODEL), const),
        ],
        out_specs=out_specs,
        out_shape=out_shape,
        scratch_shapes=[
            pltpu.VMEM((D_MODEL, EXPERT_FF), BF16),
            pltpu.VMEM((D_MODEL, EXPERT_FF), BF16),
            pltpu.VMEM((EXPERT_FF, D_MODEL), BF16),
        ],
        compiler_params=_cparams(("arbitrary",)),
        name="combine_norm",
    )(yp, gate8, hp, sw1, sw3, sw2, x1, mod_l, g2, b2)


def _moe(hp, meta, gate8, counts, w1, w3, w2, layer, sw1, sw3, sw2, x1, mod_l, g2, b2, lay, split_out):
    n = lay.n
    r_max = n * TOP_K + N_EXPERTS * EXPERT_TILE
    n_tiles = r_max // EXPERT_TILE
    cnt = counts.reshape(N_EXPERTS).astype(I32)
    padded = ((cnt + EXPERT_TILE - 1) // EXPERT_TILE) * EXPERT_TILE
    ends = jnp.cumsum(padded)
    offsets = ends - padded
    idx8 = meta[:TOP_K]
    base8 = jnp.sum(jnp.where(idx8[:, :, None] == jnp.arange(N_EXPERTS, dtype=I32), offsets, 0), axis=-1)
    pos = (base8 + meta[TOP_K:]).astype(I32)
    tile_start = jnp.arange(n_tiles, dtype=I32) * EXPERT_TILE
    tile_expert = jnp.minimum(jnp.sum(tile_start[:, None] >= ends[None, :], axis=1), N_EXPERTS - 1).astype(I32)
    n_active = (ends[-1] // EXPERT_TILE).astype(I32).reshape(1)

    pos_b = pos.reshape(TOP_K, n // SC_BATCH, SC_BATCH).transpose(1, 0, 2)
    xs = _sc_scatter_rows(hp, pos_b, r_max)
    ys = _experts(xs, tile_expert, n_active, w1, w3, w2, layer, EXPERT_TILE)
    yp = _sc_gather_rows(ys, pos.reshape(n * TOP_K)).reshape(TOP_K, n, HALF_D)
    return _combine(yp, gate8.T, hp, sw1, sw3, sw2, x1, mod_l, g2, b2, lay, TOKEN_TILE, split_out)


def kernel(x_prompt, x_sample, cache_k, cache_v, state_hgrn, c, c_ctx, w_ada, b_ada, w_in, w_out, attn_sink, hgrn_lb, hgrn_norm, ln1_g, ln1_b, ln2_g, ln2_b, router_w, router_b, moe_w1, moe_w3, moe_w2, shared_w1, shared_w3, shared_w2):
    b_ctx, t_ctx, _ = x_prompt.shape
    b_lat, t_lat, _ = x_sample.shape
    past = cache_k.shape[2]
    lay = _Layout(b_ctx, t_ctx, b_lat, t_lat)
    tm = TOKEN_TILE
    assert 1 + b_lat <= COND_ROWS
    assert lay.n_ctx % tm == 0 and lay.t_lat % tm == 0 and lay.n_ctx % lay.t_lat == 0

    x = (x_prompt.reshape(lay.n_ctx, D_MODEL), x_sample.reshape(lay.n_lat, D_MODEL))
    cond = jnp.concatenate([c_ctx[None, :], c, jnp.zeros((COND_ROWS - 1 - b_lat, D_MODEL), F32)], axis=0)
    mod = _adaln(cond, w_ada, b_ada).reshape(DEPTH, COND_ROWS, N_MOD, D_MODEL)

    lb_all = jnp.cumsum(jax.nn.softmax(hgrn_lb.astype(F32), axis=0), axis=0)
    lb_all = lb_all - lb_all[:1]
    lbp = jnp.stack([jnp.log(lb_all), jnp.log1p(-lb_all), 1.0 - lb_all], axis=2)

    cos_t, sin_t = _rope_tables(lay, tm)
    zero_state = jnp.zeros((b_ctx, 2, HGRN_HEADS, HGRN_DK, HGRN_DK), F32)

    ks_out, vs_out, ss_out = [], [], []
    for l in range(DEPTH):
        proj, k_new, v_new = _inproj(x, mod[l], w_in, l, cos_t, sin_t, lay, tm)
        ks_out.append(k_new.reshape(b_ctx, t_ctx, N_KV_HEADS, HEAD_DIM))
        vs_out.append(v_new.reshape(b_ctx, t_ctx, N_KV_HEADS, HEAD_DIM))
        sink_l = attn_sink[l].reshape(1, N_HEADS)
        attn_c = _attn_context(proj, sink_l, lay)
        attn_l = _attn_latent(proj, cache_k[:, l].reshape(b_lat, past, KV_WIDTH),
                              cache_v[:, l].reshape(b_lat, past, KV_WIDTH), sink_l, lay)
        four_c = _fourier(proj, 0, b_ctx, t_ctx, t_ctx, "fourier_ctx")
        four_l = _fourier(proj, lay.n_ctx, b_lat, t_lat, min(t_lat, 512), "fourier_lat")
        gn_row = jnp.tile(hgrn_norm[l], HGRN_HEADS).reshape(1, HGRN_WIDTH)
        rec_c, s_fin = _hgrn(proj, 0, b_ctx, t_ctx, lbp[l], gn_row, zero_state, "hgrn_ctx")
        s0t = jnp.swapaxes(state_hgrn[:, l].astype(F32), -1, -2)
        rec_l, _ = _hgrn(proj, lay.n_ctx, b_lat, t_lat, lbp[l], gn_row, s0t, "hgrn_lat")
        ss_out.append(jnp.swapaxes(s_fin, -1, -2))

        x1, hp, meta, gate8, counts = _outproj(
            (attn_c, attn_l), (four_c, four_l), (rec_c, rec_l), x, mod[l], w_out, l, ln1_g[l].reshape(1, -1), ln1_b[l].reshape(1, -1),
            router_w[l].T, router_b[l].reshape(-1, 1), lay, tm)
        x = _moe(hp, meta, gate8, counts, moe_w1, moe_w3, moe_w2, l, shared_w1[l], shared_w3[l],
                 shared_w2[l], x1, mod[l], ln2_g[l].reshape(1, -1), ln2_b[l].reshape(1, -1), lay,
                 split_out=(l == DEPTH - 1))

    y_prompt = x[0].reshape(b_ctx, t_ctx, D_MODEL)
    y_sample = x[1].reshape(b_lat, t_lat, D_MODEL)
    new_cache_k = jnp.stack(ks_out, axis=1)
    new_cache_v = jnp.stack(vs_out, axis=1)
    new_state = jnp.stack(ss_out, axis=1).astype(x_prompt.dtype)
    return (y_prompt, y_sample, new_cache_k, new_cache_v, new_state)
```

```python
import functools
import math

import numpy as np
import jax
import jax.numpy as jnp
from jax import lax
from jax.experimental import pallas as pl
from jax.experimental.pallas import tpu as pltpu
from jax.experimental.pallas import tpu_sc as plsc

F32 = jnp.float32
BF16 = jnp.bfloat16
I32 = jnp.int32

D_MODEL = 1024
HALF_D = D_MODEL // 2
DEPTH = 2
GRID_W = 64
ROPE_BASE = 10000.0
HEAD_DIM = 64
ATTN_WIDTH = 512
N_HEADS = 8
N_KV_HEADS = 2
KV_GROUP = 4
KV_WIDTH = N_KV_HEADS * HEAD_DIM
WINDOW = 128
ATTN_BLOCK = 128
FOURIER_WIDTH = 256
FOURIER_GROUPS = 4
HGRN_WIDTH = 256
HGRN_HEADS = 4
HGRN_DK = 64
HGRN_CHUNK = 64
IN_WIDTH = 2304
N_EXPERTS = 64
TOP_K = 8
EXPERT_FF = 256
ROUTED_SCALE = 2.5
N_MOD = 6
LN_EPS = 1e-5
ADA_EPS = 1e-6
GN_EPS = 1e-6
DEEPNORM_ALPHA = (2 * DEPTH) ** 0.25

COL_Q = 0
COL_K = 512
COL_V = 640
COL_U = 768
COL_HQ = 1024
COL_FF = 1280
COL_FB = 1536
COL_HI = 1792
COL_HG = 2048
ROPE_COLS = COL_V

V7X_LANES = 128
COND_ROWS = 16
NEG_BIG = -1e30
TOKEN_TILE = 512
EXPERT_TILE = 1024
SC_BATCH = 64

VMEM_LIMIT = 56 * 1024 * 1024


def _cparams(sem):
    return pltpu.CompilerParams(dimension_semantics=sem, vmem_limit_bytes=VMEM_LIMIT)


def _dot(a, b):
    return jnp.dot(a, b, preferred_element_type=F32)


def _dot_nt(a, b):
    return lax.dot_general(a, b, (((1,), (1,)), ((), ())), preferred_element_type=F32)


def _dot_tn(a, b):
    return lax.dot_general(a, b, (((0,), (0,)), ((), ())), preferred_element_type=F32)


def _split3(x):
    hi = x.astype(BF16)
    r1 = x - hi.astype(F32)
    mid = r1.astype(BF16)
    lo = (r1 - mid.astype(F32)).astype(BF16)
    return hi, mid, lo


def _dot_exact_lhs(m_bf16, x):
    hi, mid, lo = _split3(x)
    return _dot(m_bf16, hi) + _dot(m_bf16, mid) + _dot(m_bf16, lo)


def _dot_exact_rhs(x, m_bf16):
    hi, mid, lo = _split3(x)
    return _dot(hi, m_bf16) + _dot(mid, m_bf16) + _dot(lo, m_bf16)


def _dot_hp(a, b):
    a_hi = a.astype(BF16)
    a_lo = (a - a_hi.astype(F32)).astype(BF16)
    b_hi = b.astype(BF16)
    b_lo = (b - b_hi.astype(F32)).astype(BF16)
    return _dot(a_hi, b_hi) + _dot(a_hi, b_lo) + _dot(a_lo, b_hi)


def _pack_bf16_pair(lo, hi):
    return lax.bitcast_convert_type(pltpu.pack_elementwise([lo, hi], packed_dtype=BF16), I32)


def _unpack_bf16_pair(w):
    u = lax.bitcast_convert_type(w, jnp.uint32)
    lo = pltpu.unpack_elementwise(u, index=0, packed_dtype=BF16, unpacked_dtype=F32)
    hi = pltpu.unpack_elementwise(u, index=1, packed_dtype=BF16, unpacked_dtype=F32)
    return lo, hi


def _ln_plain(x, eps):
    mu = jnp.mean(x, axis=-1, keepdims=True)
    xc = x - mu
    var = jnp.mean(xc * xc, axis=-1, keepdims=True)
    return xc * lax.rsqrt(var + eps)


def _silu(x):
    return x * jax.nn.sigmoid(x)


def _adaln_kernel(c_ref, w_ref, b_ref, o_ref):
    s = _silu(c_ref[...])
    o_ref[0] = _dot_hp(s, w_ref[0]) + b_ref[0]


def _adaln(cond, w_ada, b_ada):
    return pl.pallas_call(
        _adaln_kernel,
        grid=(DEPTH, N_MOD),
        in_specs=[
            pl.BlockSpec((COND_ROWS, D_MODEL), lambda l, j: (0, 0)),
            pl.BlockSpec((1, D_MODEL, D_MODEL), lambda l, j: (l, 0, j)),
            pl.BlockSpec((1, 1, D_MODEL), lambda l, j: (l, 0, j)),
        ],
        out_specs=pl.BlockSpec((1, COND_ROWS, D_MODEL), lambda l, j: (l, 0, j)),
        out_shape=jax.ShapeDtypeStruct((DEPTH, COND_ROWS, N_MOD * D_MODEL), F32),
        compiler_params=_cparams(("arbitrary", "arbitrary")),
        name="adaln",
    )(cond, w_ada, b_ada.reshape(DEPTH, 1, N_MOD * D_MODEL))


class _Layout:
    def __init__(self, b_ctx, t_ctx, b_lat, t_lat):
        self.b_ctx, self.t_ctx, self.b_lat, self.t_lat = b_ctx, t_ctx, b_lat, t_lat
        self.n_ctx = b_ctx * t_ctx
        self.n_lat = b_lat * t_lat
        self.n = self.n_ctx + self.n_lat
        self.tag = "" if (b_ctx and b_lat) else ("_ctx" if b_ctx else "_lat")

    def cond_row(self, tile, tm):
        n_ctx_tiles = self.n_ctx // tm
        per_batch = self.t_lat // tm
        return jnp.where(tile < n_ctx_tiles, 0, 1 + (tile - n_ctx_tiles) // per_batch)


def _group_specs(n_arrays, tm, width, n_ctx_tiles):
    if n_arrays == 1:
        return [pl.BlockSpec((tm, width), lambda i: (i, 0))]
    return [pl.BlockSpec((tm, width), lambda i: (jnp.minimum(i, n_ctx_tiles - 1), 0)),
            pl.BlockSpec((tm, width), lambda i: (jnp.maximum(i - n_ctx_tiles, 0), 0))]


def _inproj_kernel(*refs, n_ctx_tiles):
    n_tail = 8 if n_ctx_tiles > 0 else 6
    xs = refs[:-n_tail]
    mod_ref, w_ref, cos_ref, sin_ref, o_ref = refs[-n_tail:-n_tail + 5]
    wb_ref = refs[-1]

    @pl.when(pl.program_id(0) == 0)
    def _():
        wb_ref[...] = w_ref[0].astype(BF16)

    if len(xs) == 2:
        x = jnp.where(pl.program_id(0) < n_ctx_tiles, xs[0][...], xs[1][...])
    else:
        x = xs[0][...]
    shift = mod_ref[0, 0:1, :]
    scale = mod_ref[0, 1:2, :]
    h = (_ln_plain(x, ADA_EPS) * (1.0 + scale) + shift).astype(BF16)
    p = _dot(h, wb_ref[...])
    cos = cos_ref[...]
    sin = sin_ref[...]
    lane = lax.broadcasted_iota(I32, cos.shape, 1)
    first_half = (lane & 31) < 16
    for cb in range(ROPE_COLS // V7X_LANES):
        seg = p[:, cb * V7X_LANES:(cb + 1) * V7X_LANES]
        partner = jnp.where(first_half, pltpu.roll(seg, V7X_LANES - 16, 1), pltpu.roll(seg, 16, 1))
        o_ref[:, cb * V7X_LANES:(cb + 1) * V7X_LANES] = seg * cos + partner * sin
    o_ref[:, ROPE_COLS:] = p[:, ROPE_COLS:]

    if n_ctx_tiles > 0:
        kc_ref, vc_ref = refs[-3], refs[-2]

        @pl.when(pl.program_id(0) < n_ctx_tiles)
        def _():
            kc_ref[...] = p[:, COL_K:COL_K + KV_WIDTH]
            vc_ref[...] = p[:, COL_V:COL_V + KV_WIDTH]


def _rope_tables(lay, tm):
    t = lay.t_lat
    pos = jnp.arange(t)
    row = (pos // GRID_W).astype(F32)
    col = (pos % GRID_W).astype(F32)
    n_freq = HEAD_DIM // 4
    inv = ROPE_BASE ** (-jnp.arange(n_freq, dtype=F32) / n_freq)
    ang_r = row[:, None] * inv
    ang_c = col[:, None] * inv
    ang = jnp.concatenate([ang_r, ang_r, ang_c, ang_c], axis=1)
    sign = jnp.concatenate([-jnp.ones(n_freq), jnp.ones(n_freq), -jnp.ones(n_freq), jnp.ones(n_freq)]).astype(F32)
    cos = jnp.cos(ang)
    sin = jnp.sin(ang) * sign
    cos = jnp.concatenate([jnp.ones((tm, HEAD_DIM), F32), cos], axis=0)
    sin = jnp.concatenate([jnp.zeros((tm, HEAD_DIM), F32), sin], axis=0)
    return jnp.tile(cos, (1, 2)), jnp.tile(sin, (1, 2))


def _inproj(x, mod_l, w_in, layer, cos_t, sin_t, lay, tm):
    n_tiles = lay.n // tm
    n_ctx_tiles = lay.n_ctx // tm
    per_batch = lay.t_lat // tm

    def tbl(i):
        return jnp.where(i < n_ctx_tiles, 0, 1 + (i - n_ctx_tiles) % per_batch)

    xs = x if isinstance(x, tuple) else (x,)
    kv_specs, kv_shapes = [], []
    if n_ctx_tiles > 0:
        kv_specs = [pl.BlockSpec((tm, KV_WIDTH), lambda i: (jnp.minimum(i, n_ctx_tiles - 1), 0))] * 2
        kv_shapes = [jax.ShapeDtypeStruct((lay.n_ctx, KV_WIDTH), F32)] * 2
    return pl.pallas_call(
        functools.partial(_inproj_kernel, n_ctx_tiles=n_ctx_tiles),
        grid=(n_tiles,),
        in_specs=[
            *_group_specs(len(xs), tm, D_MODEL, n_ctx_tiles),
            pl.BlockSpec((1, N_MOD, D_MODEL), lambda i: (lay.cond_row(i, tm), 0, 0)),
            pl.BlockSpec((1, D_MODEL, IN_WIDTH), lambda i: (layer, 0, 0), pipeline_mode=pl.Buffered(1)),
            pl.BlockSpec((tm, V7X_LANES), lambda i: (tbl(i), 0)),
            pl.BlockSpec((tm, V7X_LANES), lambda i: (tbl(i), 0)),
        ],
        out_specs=[pl.BlockSpec((tm, IN_WIDTH), lambda i: (i, 0))] + kv_specs,
        out_shape=[jax.ShapeDtypeStruct((lay.n, IN_WIDTH), F32)] + kv_shapes,
        scratch_shapes=[pltpu.VMEM((D_MODEL, IN_WIDTH), BF16)],
        compiler_params=_cparams(("arbitrary",)),
        name="inproj" + lay.tag,
    )(*xs, mod_l, w_in, cos_t, sin_t)


def _attn_kernel(sink_ref, q_ref, *refs, n_local, has_ctx, t_total):
    o_ref = refs[-1]
    k_refs = refs[:n_local]
    v_refs = refs[n_local:2 * n_local]
    tq = q_ref.shape[0]
    scale = HEAD_DIM ** -0.5
    k_parts = [kr[...] for kr in k_refs]
    v_parts = [vr[...] for vr in v_refs]
    if has_ctx:
        k_parts.append(refs[2 * n_local][0])
        v_parts.append(refs[2 * n_local + 1][0])
    kall = jnp.concatenate(k_parts, axis=0) if len(k_parts) > 1 else k_parts[0]
    vall = jnp.concatenate(v_parts, axis=0) if len(v_parts) > 1 else v_parts[0]
    nk = kall.shape[0]
    k_sw = pltpu.roll(kall, HEAD_DIM, 1)
    v_sw = pltpu.roll(vall, HEAD_DIM, 1)
    lo_half = lax.broadcasted_iota(I32, (1, V7X_LANES), 1) < HEAD_DIM
    er = jnp.where(lax.broadcasted_iota(I32, (2 * nk, V7X_LANES), 0) < nk, 0, 1)
    el = jnp.where(lax.broadcasted_iota(I32, (2 * nk, V7X_LANES), 1) < HEAD_DIM, 0, 1)
    ones_blk = jnp.where(er == el, 1.0, 0.0).astype(BF16)
    if n_local > 1:
        i = pl.program_id(1)
        band = refs[-2][...]
        first_blk = jnp.where(i == 0, NEG_BIG, 0.0)
        last_blk = jnp.where(i == t_total // tq - 1, NEG_BIG, 0.0)

        def mask_local(sc):
            loc = sc[:, :n_local * tq] + band
            parts = [loc[:, :tq] + first_blk, loc[:, tq:(n_local - 1) * tq], loc[:, (n_local - 1) * tq:] + last_blk]
            return jnp.concatenate(parts + [sc[:, n_local * tq:]], axis=1)
    else:
        mask_local = None
    for g in range(N_KV_HEADS):
        k_own, k_oth = (kall, k_sw) if g == 0 else (k_sw, kall)
        v_own, v_oth = (vall, v_sw) if g == 0 else (v_sw, vall)
        k2 = jnp.concatenate([jnp.where(lo_half, k_own, 0.0), jnp.where(lo_half, 0.0, k_oth)], axis=0).astype(BF16)
        v2 = jnp.concatenate([jnp.where(lo_half, v_own, 0.0), jnp.where(lo_half, 0.0, v_oth)], axis=0).astype(BF16)
        v2e = jnp.concatenate([v2, ones_blk], axis=1)
        pairs = [2 * g, 2 * g + 1]
        qq = jnp.concatenate([q_ref[:, p * V7X_LANES:(p + 1) * V7X_LANES] for p in pairs], axis=0)
        qq = (qq * scale).astype(BF16)
        sink_a = jnp.concatenate([jnp.full((tq, 1), sink_ref[0, 2 * p], F32) for p in pairs], axis=0)
        sink_b = jnp.concatenate([jnp.full((tq, 1), sink_ref[0, 2 * p + 1], F32) for p in pairs], axis=0)
        s = _dot_nt(qq, k2)
        s_a = s[:, :nk]
        s_b = s[:, nk:]
        if mask_local is not None:
            s_a = mask_local(s_a)
            s_b = mask_local(s_b)
        m_a = jnp.maximum(jnp.max(s_a, axis=1, keepdims=True), sink_a)
        m_b = jnp.maximum(jnp.max(s_b, axis=1, keepdims=True), sink_b)
        pe = jnp.concatenate([jnp.exp(s_a - m_a).astype(BF16), jnp.exp(s_b - m_b).astype(BF16)], axis=1)
        acc = _dot(pe, v2e)
        sink_term = jnp.where(lo_half, jnp.exp(sink_a - m_a), jnp.exp(sink_b - m_b))
        o = acc[:, :V7X_LANES] / (acc[:, V7X_LANES:] + sink_term)
        for j, p in enumerate(pairs):
            o_ref[:, p * V7X_LANES:(p + 1) * V7X_LANES] = o[j * tq:(j + 1) * tq].astype(o_ref.dtype)


def _attn_context(proj, sink_l, lay):
    t = lay.t_ctx
    kb, vb = COL_K // KV_WIDTH, COL_V // KV_WIDTH
    body = functools.partial(_attn_kernel, n_local=1, has_ctx=False, t_total=t)
    return pl.pallas_call(
        body,
        grid=(lay.b_ctx,),
        in_specs=[
            pl.BlockSpec(memory_space=pltpu.SMEM),
            pl.BlockSpec((t, ATTN_WIDTH), lambda b: (b, 0)),
            pl.BlockSpec((t, KV_WIDTH), lambda b: (b, kb)),
            pl.BlockSpec((t, KV_WIDTH), lambda b: (b, vb)),
        ],
        out_specs=pl.BlockSpec((t, ATTN_WIDTH), lambda b: (b, 0)),
        out_shape=jax.ShapeDtypeStruct((lay.n_ctx, ATTN_WIDTH), BF16),
        compiler_params=_cparams(("arbitrary",)),
        name="attn_ctx",
    )(sink_l, proj, proj, proj)


def _attn_latent(proj, k_ctx, v_ctx, sink_l, lay):
    t = lay.t_lat
    tq = ATTN_BLOCK
    nq = t // tq
    base = lay.n_ctx // tq
    kb, vb = COL_K // KV_WIDTH, COL_V // KV_WIDTH
    past = k_ctx.shape[1]

    def rows(off):
        return lambda b, i: base + b * nq + jnp.clip(i + off, 0, nq - 1)

    def kv_specs(col):
        return [pl.BlockSpec((tq, KV_WIDTH), (lambda b, i, f=rows(off): (f(b, i), col))) for off in (-1, 0, 1)]

    body = functools.partial(_attn_kernel, n_local=3, has_ctx=True, t_total=t)
    rel = np.arange(3 * tq)[None, :] - tq - (np.arange(2 * tq)[:, None] % tq)
    band = jnp.asarray(np.where(np.abs(rel) <= WINDOW, 0.0, NEG_BIG).astype(np.float32))
    return pl.pallas_call(
        body,
        grid=(lay.b_lat, nq),
        in_specs=[
            pl.BlockSpec(memory_space=pltpu.SMEM),
            pl.BlockSpec((tq, ATTN_WIDTH), lambda b, i: (base + b * nq + i, 0)),
            *kv_specs(kb),
            *kv_specs(vb),
            pl.BlockSpec((1, past, KV_WIDTH), lambda b, i: (b, 0, 0)),
            pl.BlockSpec((1, past, KV_WIDTH), lambda b, i: (b, 0, 0)),
            pl.BlockSpec(band.shape, lambda b, i: (0, 0)),
        ],
        out_specs=pl.BlockSpec((tq, ATTN_WIDTH), lambda b, i: (b * nq + i, 0)),
        out_shape=jax.ShapeDtypeStruct((lay.n_lat, ATTN_WIDTH), BF16),
        compiler_params=_cparams(("arbitrary", "arbitrary")),
        name="attn_lat",
    )(sink_l, proj, proj, proj, proj, proj, proj, proj, k_ctx, v_ctx, band)


def _fourier_kernel(cs_ref, u_ref, cc_ref, sc_ref, o_ref, csb_ref, *, scale):
    @pl.when(pl.program_id(1) == 0)
    def _():
        csb_ref[...] = cs_ref[...].astype(BF16)

    z = u_ref[...].astype(BF16)
    zc = _dot(z, cc_ref[...].astype(BF16)).astype(BF16)
    zs = _dot(z, sc_ref[...].astype(BF16)).astype(BF16)
    zz = jnp.concatenate([zc, zs], axis=0)
    o_ref[...] = (_dot(csb_ref[...], zz) * scale).astype(o_ref.dtype)


@functools.lru_cache(maxsize=None)
def _dft_tables(t):
    idx = np.arange(t, dtype=np.int64)
    ang = 2.0 * np.pi * ((idx[:, None] * idx[None, :]) % t).astype(np.float64) / t
    cs = np.concatenate([np.cos(ang), -np.sin(ang)], axis=1).astype(np.float32)
    cw = FOURIER_WIDTH // FOURIER_GROUPS
    cidx = np.arange(cw, dtype=np.int64)
    cang = 2.0 * np.pi * ((cidx[:, None] * cidx[None, :]) % cw).astype(np.float64) / cw
    eye = np.eye(FOURIER_GROUPS)
    cc = np.kron(eye, np.cos(cang)).astype(np.float32)
    sc = np.kron(eye, np.sin(cang)).astype(np.float32)
    return cs, cc, sc


def _fourier(proj, row0, b, t, tm, name):
    cs, cc, sc = _dft_tables(t)
    cw = FOURIER_WIDTH // FOURIER_GROUPS
    nt = t // tm
    ub = COL_U // FOURIER_WIDTH
    base = row0 // t
    body = functools.partial(_fourier_kernel, scale=1.0 / math.sqrt(t * cw))
    return pl.pallas_call(
        body,
        grid=(nt, b),
        in_specs=[
            pl.BlockSpec((tm, 2 * t), lambda i, bb: (i, 0)),
            pl.BlockSpec((t, FOURIER_WIDTH), lambda i, bb: (base + bb, ub)),
            pl.BlockSpec((FOURIER_WIDTH, FOURIER_WIDTH), lambda i, bb: (0, 0)),
            pl.BlockSpec((FOURIER_WIDTH, FOURIER_WIDTH), lambda i, bb: (0, 0)),
        ],
        out_specs=pl.BlockSpec((tm, FOURIER_WIDTH), lambda i, bb: (bb * nt + i, 0)),
        out_shape=jax.ShapeDtypeStruct((b * t, FOURIER_WIDTH), BF16),
        scratch_shapes=[pltpu.VMEM((tm, 2 * t), BF16)],
        compiler_params=_cparams(("arbitrary", "arbitrary")),
        name=name,
    )(jnp.asarray(cs), proj, jnp.asarray(cc), jnp.asarray(sc))


HGRN_LEVELS = (64, 32, 16, 8, 4, 2)
HGRN_SAFE_RANGE = 80.0


@functools.lru_cache(maxsize=None)
def _hgrn_tables():
    c = HGRN_CHUNK
    return np.stack([np.tril(np.ones((c, c))), np.triu(np.ones((c, c)))]).astype(np.float32)


def _boundary_rows(b, m, reverse):
    c, w = b.shape
    half = m // 2
    off = half if reverse else half - 1
    if m >= 16:
        return jnp.concatenate(
            [jnp.broadcast_to(b[s + off:s + off + 1], (m, w)) for s in range(0, c, m)], axis=0)
    sub = lax.broadcasted_iota(I32, (c, w), 0) & 7
    b3 = b.reshape(c // 8, 8, w)

    def bcast(j):
        return jnp.broadcast_to(b3[:, j:j + 1, :], (c // 8, 8, w)).reshape(c, w)

    if m == 8:
        return bcast(off)
    if m == 4:
        return jnp.where(sub < 4, bcast(off), bcast(4 + off))
    assert m == 2
    if reverse:
        return jnp.where((sub & 1) == 1, b, pltpu.roll(b, c - 1, 0))
    return jnp.where((sub & 1) == 0, b, pltpu.roll(b, 1, 0))


class _HgrnDir:
    def __init__(self, q, z, v, loglb, log1mlb, onemlb, cum, reverse):
        c = HGRN_CHUNK
        self.q, self.reverse = q, reverse
        log_sig = jnp.minimum(z, 0.0) - jnp.log1p(jnp.exp(-jnp.abs(z)))
        bb = log1mlb + log_sig
        mx = jnp.maximum(loglb, bb)
        lf = mx + jnp.log1p(jnp.exp(-jnp.abs(loglb - bb)))
        self.kk = onemlb * jax.nn.sigmoid(-z)
        self.b = _dot_exact_lhs(cum, lf)
        b_end = self.b[0:1] if reverse else self.b[c - 1:c]
        self.qt = (q * jnp.exp(self.b)).astype(BF16)
        self.kt = (self.kk * jnp.exp(b_end - self.b)).astype(BF16)
        self.decay = jnp.exp(b_end)
        self.vb = v.astype(BF16)
        mid = c // 2 if reverse else c // 2 - 1
        self.rel = self.b - self.b[mid:mid + 1]
        self.span = jnp.max(jnp.abs(self.rel))

    def tree_decay_matrices(self):
        c = HGRN_CHUNK
        q, kk, b = self.q, self.kk, self.b
        row = lax.broadcasted_iota(I32, (c, 1), 0)
        ti = lax.broadcasted_iota(I32, (c, c), 0)
        si = lax.broadcasted_iota(I32, (c, c), 1)
        qb = q.astype(BF16)
        kb = kk.astype(BF16)
        heads = [slice(h * HGRN_DK, (h + 1) * HGRN_DK) for h in range(HGRN_HEADS)]
        acc = [jnp.where(ti == si, _dot_nt(qb[:, sl], kb[:, sl]), 0.0) for sl in heads]
        for m in HGRN_LEVELS:
            r = _boundary_rows(b, m, self.reverse)
            upper = (row & (m - 1)) >= (m // 2)
            q_side = jnp.logical_not(upper) if self.reverse else upper
            e = jnp.exp(jnp.where(q_side, b - r, r - b))
            qf = jnp.where(q_side, q * e, 0.0).astype(BF16)
            kf = jnp.where(q_side, 0.0, kk * e).astype(BF16)
            same_block = (ti & -m) == (si & -m)
            for h, sl in enumerate(heads):
                acc[h] = acc[h] + jnp.where(same_block, _dot_nt(qf[:, sl], kf[:, sl]), 0.0)
        return acc

    def midpoint_decay_matrices(self):
        c = HGRN_CHUNK
        ti = lax.broadcasted_iota(I32, (c, c), 0)
        si = lax.broadcasted_iota(I32, (c, c), 1)
        qm = (self.q * jnp.exp(self.rel)).astype(BF16)
        km = (self.kk * jnp.exp(-self.rel)).astype(BF16)
        causal = (si >= ti) if self.reverse else (si <= ti)
        return [jnp.where(causal, _dot_nt(qm[:, h * HGRN_DK:(h + 1) * HGRN_DK], km[:, h * HGRN_DK:(h + 1) * HGRN_DK]), 0.0)
                for h in range(HGRN_HEADS)]

    def outputs(self, a_heads, st_ref, d):
        outs = []
        for h in range(HGRN_HEADS):
            sl = slice(h * HGRN_DK, (h + 1) * HGRN_DK)
            st = st_ref[d, h]
            o = _dot_nt(self.qt[:, sl], st.astype(BF16)) + _dot(a_heads[h].astype(BF16), self.vb[:, sl])
            st_ref[d, h] = st * self.decay[:, sl] + _dot_tn(self.vb[:, sl], self.kt[:, sl])
            outs.append(o)
        return jnp.concatenate(outs, axis=1)


def _hgrn_kernel(hq_ref, ff_ref, fb_ref, hi_ref, hg_ref, lbp_ref, gn_ref, mall_ref, ones_ref, s0_ref,
                 rec_ref, sfin_ref, st_ref, of_ref, ob_ref, *, t):
    c = HGRN_CHUNK
    n = t // c
    st_ref[...] = s0_ref[0]

    def body(ci, carry):
        rf = pl.ds(pl.multiple_of(ci * c, c), c)
        rb = pl.ds(pl.multiple_of((n - 1 - ci) * c, c), c)
        fwd = _HgrnDir(hq_ref[rf, :], ff_ref[rf, :], hi_ref[rf, :], lbp_ref[0, 0:1, :], lbp_ref[0, 1:2, :],
                       lbp_ref[0, 2:3, :], mall_ref[0].astype(BF16), False)
        bwd = _HgrnDir(hq_ref[rb, :], fb_ref[rb, :], hi_ref[rb, :], lbp_ref[1, 0:1, :], lbp_ref[1, 1:2, :],
                       lbp_ref[1, 2:3, :], mall_ref[1].astype(BF16), True)
        a_f, a_b = lax.cond(
            jnp.maximum(fwd.span, bwd.span) <= HGRN_SAFE_RANGE,
            lambda: (fwd.midpoint_decay_matrices(), bwd.midpoint_decay_matrices()),
            lambda: (fwd.tree_decay_matrices(), bwd.tree_decay_matrices()))
        of_ref[rf, :] = fwd.outputs(a_f, st_ref, 0)
        ob_ref[rb, :] = bwd.outputs(a_b, st_ref, 1)
        return carry

    lax.fori_loop(0, n, body, 0)
    sfin_ref[0] = st_ref[...]
    o = of_ref[...] + ob_ref[...]
    ms = _dot_exact_rhs(o * o, ones_ref[...].astype(BF16)) * (1.0 / HGRN_DK)
    o = o * lax.rsqrt(ms + GN_EPS) * gn_ref[...]
    rec_ref[...] = (o * _silu(hg_ref[...])).astype(rec_ref.dtype)


def _hgrn(proj, row0, b, t, lbp, gn_row, s0t, name):
    base = row0 // t
    m_all = jnp.asarray(_hgrn_tables())
    ones_bd = jnp.asarray(np.kron(np.eye(HGRN_HEADS), np.ones((HGRN_DK, HGRN_DK))).astype(np.float32))

    def col(cstart):
        return pl.BlockSpec((t, HGRN_WIDTH), lambda bb, cb=cstart // HGRN_WIDTH: (base + bb, cb))

    const2 = lambda bb: (0, 0)
    const3 = lambda bb: (0, 0, 0)
    st_shape = (2, HGRN_HEADS, HGRN_DK, HGRN_DK)
    body = functools.partial(_hgrn_kernel, t=t)
    return pl.pallas_call(
        body,
        grid=(b,),
        in_specs=[
            col(COL_HQ), col(COL_FF), col(COL_FB), col(COL_HI), col(COL_HG),
            pl.BlockSpec((2, 3, HGRN_WIDTH), const3),
            pl.BlockSpec((1, HGRN_WIDTH), const2),
            pl.BlockSpec(m_all.shape, const3),
            pl.BlockSpec(ones_bd.shape, const2),
            pl.BlockSpec((1,) + st_shape, lambda bb: (bb, 0, 0, 0, 0)),
        ],
        out_specs=[
            pl.BlockSpec((t, HGRN_WIDTH), lambda bb: (bb, 0)),
            pl.BlockSpec((1,) + st_shape, lambda bb: (bb, 0, 0, 0, 0)),
        ],
        out_shape=[
            jax.ShapeDtypeStruct((b * t, HGRN_WIDTH), BF16),
            jax.ShapeDtypeStruct((b,) + st_shape, F32),
        ],
        scratch_shapes=[
            pltpu.VMEM(st_shape, F32),
            pltpu.VMEM((t, HGRN_WIDTH), F32),
            pltpu.VMEM((t, HGRN_WIDTH), F32),
        ],
        compiler_params=_cparams(("arbitrary",)),
        name=name,
    )(proj, proj, proj, proj, proj, lbp, gn_row, m_all, ones_bd, s0t)


def _outproj_kernel(*refs, n_ctx_tiles, n_mix):
    mix = refs[:3 * n_mix]
    refs = refs[3 * n_mix:]
    xs = refs[:-14]
    (mod_ref, w_ref, g_ref, b_ref, rw_ref, rb_ref, x1_ref, hp_ref, meta_ref, gate_ref, cnt_ref, wb_ref, tri_ref,
     run_ref) = refs[-14:]
    tm = x1_ref.shape[0]
    is_ctx = pl.program_id(0) < n_ctx_tiles
    x_in = jnp.where(is_ctx, xs[0][...], xs[1][...]) if len(xs) == 2 else xs[0][...]
    if n_mix == 2:
        attn, four, rec = [jnp.where(is_ctx, mix[2 * j][...], mix[2 * j + 1][...]) for j in range(3)]
    else:
        attn, four, rec = [r[...] for r in mix]

    @pl.when(pl.program_id(0) == 0)
    def _():
        wb_ref[...] = w_ref[0].astype(BF16)
        r = lax.broadcasted_iota(I32, (tm, tm), 0)
        c = lax.broadcasted_iota(I32, (tm, tm), 1)
        tri_ref[...] = jnp.where(r < c, 1.0, 0.0).astype(BF16)
        run_ref[...] = jnp.zeros_like(run_ref)

    out = _dot(attn, wb_ref[0:ATTN_WIDTH, :])
    out = out + _dot(four, wb_ref[ATTN_WIDTH:ATTN_WIDTH + FOURIER_WIDTH, :])
    out = out + _dot(rec, wb_ref[ATTN_WIDTH + FOURIER_WIDTH:, :])
    gate1 = mod_ref[0, 2:3, :]
    y = DEEPNORM_ALPHA * x_in + gate1 * out
    x1 = _ln_plain(y, LN_EPS) * g_ref[...] + b_ref[...]
    x1_ref[...] = x1
    h2 = _ln_plain(x1, ADA_EPS) * (1.0 + mod_ref[0, 4:5, :]) + mod_ref[0, 3:4, :]
    hp_ref[...] = _pack_bf16_pair(h2[:, :HALF_D], h2[:, HALF_D:])

    h_hi = h2.astype(BF16)
    h_lo = (h2 - h_hi.astype(F32)).astype(BF16)
    rwt = rw_ref[...]
    w_hi = rwt.astype(BF16)
    w_lo = (rwt - w_hi.astype(F32)).astype(BF16)
    scores = jax.nn.sigmoid(_dot_nt(w_hi, h_hi) + _dot_nt(w_hi, h_lo) + _dot_nt(w_lo, h_hi))
    remaining = scores + rb_ref[...]
    eidx = lax.broadcasted_iota(I32, scores.shape, 0).astype(F32)
    chosen = jnp.zeros(scores.shape, jnp.bool_)
    picks = []
    for _ in range(TOP_K):
        mx = jnp.max(remaining, axis=0, keepdims=True)
        first = jnp.min(jnp.where(remaining == mx, eidx, float(N_EXPERTS)), axis=0, keepdims=True)
        pick = eidx == first
        picks.append((pick, first))
        chosen = jnp.logical_or(chosen, pick)
        remaining = jnp.where(pick, -jnp.inf, remaining)
    sel = jnp.where(chosen, scores, 0.0)
    gates = sel / jnp.sum(sel, axis=0, keepdims=True) * ROUTED_SCALE

    onehot = jnp.where(chosen, 1.0, 0.0)
    rank = run_ref[...] + _dot(onehot.astype(BF16), tri_ref[...])
    run_ref[...] += jnp.sum(onehot, axis=1, keepdims=True)
    cnt_ref[...] = run_ref[...]

    ids, rks, gks = [], [], []
    for pick, first in picks:
        ids.append(first.astype(I32))
        rks.append(jnp.sum(jnp.where(pick, rank, 0.0), axis=0, keepdims=True).astype(I32))
        gks.append(jnp.sum(jnp.where(pick, gates, 0.0), axis=0, keepdims=True))
    meta_ref[...] = jnp.concatenate(ids + rks, axis=0)
    gate_ref[...] = jnp.concatenate(gks, axis=0)


def _outproj(attn, four, rec, x, mod_l, w_out, layer, g1, b1, rw, rb, lay, tm):
    n_tiles = lay.n // tm
    n_ctx_tiles = lay.n_ctx // tm
    row = lambda i: (i, 0)
    const = lambda i: (0, 0)
    xs = x if isinstance(x, tuple) else (x,)
    return pl.pallas_call(
        functools.partial(_outproj_kernel, n_ctx_tiles=n_ctx_tiles, n_mix=len(attn)),
        grid=(n_tiles,),
        in_specs=[
            *_group_specs(len(attn), tm, ATTN_WIDTH, n_ctx_tiles),
            *_group_specs(len(four), tm, FOURIER_WIDTH, n_ctx_tiles),
            *_group_specs(len(rec), tm, HGRN_WIDTH, n_ctx_tiles),
            *_group_specs(len(xs), tm, D_MODEL, n_ctx_tiles),
            pl.BlockSpec((1, N_MOD, D_MODEL), lambda i: (lay.cond_row(i, tm), 0, 0)),
            pl.BlockSpec((1, D_MODEL, D_MODEL), lambda i: (layer, 0, 0)),
            pl.BlockSpec((1, D_MODEL), const),
            pl.BlockSpec((1, D_MODEL), const),
            pl.BlockSpec((N_EXPERTS, D_MODEL), const),
            pl.BlockSpec((N_EXPERTS, 1), const),
        ],
        out_specs=[
            pl.BlockSpec((tm, D_MODEL), row),
            pl.BlockSpec((tm, HALF_D), row),
            pl.BlockSpec((2 * TOP_K, tm), lambda i: (0, i)),
            pl.BlockSpec((TOP_K, tm), lambda i: (0, i)),
            pl.BlockSpec((N_EXPERTS, 1), const),
        ],
        out_shape=[
            jax.ShapeDtypeStruct((lay.n, D_MODEL), F32),
            jax.ShapeDtypeStruct((lay.n, HALF_D), I32),
            jax.ShapeDtypeStruct((2 * TOP_K, lay.n), I32),
            jax.ShapeDtypeStruct((TOP_K, lay.n), F32),
            jax.ShapeDtypeStruct((N_EXPERTS, 1), F32),
        ],
        scratch_shapes=[
            pltpu.VMEM((D_MODEL, D_MODEL), BF16),
            pltpu.VMEM((tm, tm), BF16),
            pltpu.VMEM((N_EXPERTS, 1), F32),
        ],
        compiler_params=_cparams(("arbitrary",)),
        name="outproj_router" + lay.tag,
    )(*attn, *four, *rec, *xs, mod_l, w_out, g1, b1, rw, rb)


def _sc_workers():
    info = plsc.get_sparse_core_info()
    return info.num_cores, info.num_cores * info.num_subcores


def _sc_scatter_rows(rows, pos_b, r_out, tag=""):
    nc, nw = _sc_workers()
    n, w = rows.shape
    nbt, copies, _ = pos_b.shape
    assert nbt * SC_BATCH == n and nbt % (2 * nw) == 0
    per_w = nbt // nw
    mesh = plsc.VectorSubcoreMesh(core_axis_name="c", subcore_axis_name="s")

    @functools.partial(
        pl.kernel, mesh=mesh, out_type=jax.ShapeDtypeStruct((r_out, w), rows.dtype),
        scratch_types=[pltpu.VMEM((copies, SC_BATCH), I32), pltpu.VMEM((copies, SC_BATCH), I32),
                       pltpu.VMEM((SC_BATCH, w), rows.dtype), pltpu.VMEM((SC_BATCH, w), rows.dtype),
                       pltpu.SemaphoreType.DMA, pltpu.SemaphoreType.DMA,
                       pltpu.SemaphoreType.DMA, pltpu.SemaphoreType.DMA],
        name="sc_dispatch" + tag)
    def k(rows_hbm, pos_hbm, out_hbm, idx_a, idx_b, rows_a, rows_b, sem_ra, sem_rb, sem_sa, sem_sb):
        wid = lax.axis_index("s") * nc + lax.axis_index("c")
        first = wid * per_w

        def reads(j, idx_v, rows_v, sem):
            bt = first + j
            return (pltpu.make_async_copy(pos_hbm.at[bt], idx_v, sem),
                    pltpu.make_async_copy(rows_hbm.at[pl.ds(bt * SC_BATCH, SC_BATCH)], rows_v, sem))

        def scatters(idx_v, rows_v, sem):
            return [pltpu.make_async_copy(rows_v, out_hbm.at[idx_v.at[q]], sem) for q in range(copies)]

        def start(descs):
            for d in descs:
                d.start()

        def wait(descs):
            for d in descs:
                d.wait()

        start(reads(0, idx_a, rows_a, sem_ra))

        @pl.loop(0, per_w // 2)
        def _(p):
            j0 = 2 * p
            j1 = j0 + 1

            @pl.when(p > 0)
            def _():
                wait(scatters(idx_b, rows_b, sem_sb))

            start(reads(j1, idx_b, rows_b, sem_rb))
            wait(reads(j0, idx_a, rows_a, sem_ra))
            start(scatters(idx_a, rows_a, sem_sa))
            wait(reads(j1, idx_b, rows_b, sem_rb))
            start(scatters(idx_b, rows_b, sem_sb))
            wait(scatters(idx_a, rows_a, sem_sa))

            @pl.when(p + 1 < per_w // 2)
            def _():
                start(reads(j0 + 2, idx_a, rows_a, sem_ra))

        wait(scatters(idx_b, rows_b, sem_sb))

    return k(rows, pos_b)


def _sc_gather_rows(table, idx, tag=""):
    nc, nw = _sc_workers()
    r = idx.shape[0]
    w = table.shape[1]
    assert r % (2 * nw * SC_BATCH) == 0
    per_w = r // nw
    nb = per_w // SC_BATCH
    mesh = plsc.VectorSubcoreMesh(core_axis_name="c", subcore_axis_name="s")

    @functools.partial(
        pl.kernel, mesh=mesh, out_type=jax.ShapeDtypeStruct((r, w), table.dtype),
        scratch_types=[pltpu.VMEM((per_w,), I32),
                       pltpu.VMEM((SC_BATCH, w), table.dtype), pltpu.VMEM((SC_BATCH, w), table.dtype),
                       pltpu.SemaphoreType.DMA, pltpu.SemaphoreType.DMA,
                       pltpu.SemaphoreType.DMA, pltpu.SemaphoreType.DMA],
        name="sc_combine" + tag)
    def k(table_hbm, idx_hbm, out_hbm, idx_v, rows_a, rows_b, sem_ga, sem_gb, sem_wa, sem_wb):
        wid = lax.axis_index("s") * nc + lax.axis_index("c")
        base = wid * per_w
        pltpu.sync_copy(idx_hbm.at[pl.ds(base, per_w)], idx_v)

        def gather(j, rows_v, sem):
            return pltpu.make_async_copy(table_hbm.at[idx_v.at[pl.ds(j * SC_BATCH, SC_BATCH)]], rows_v, sem)

        def write(j, rows_v, sem):
            return pltpu.make_async_copy(rows_v, out_hbm.at[pl.ds(base + j * SC_BATCH, SC_BATCH)], sem)

        gather(0, rows_a, sem_ga).start()

        @pl.loop(0, nb // 2)
        def _(p):
            j0 = 2 * p
            j1 = j0 + 1

            @pl.when(p > 0)
            def _():
                write(j1 - 2, rows_b, sem_wb).wait()

            gather(j1, rows_b, sem_gb).start()
            gather(j0, rows_a, sem_ga).wait()
            write(j0, rows_a, sem_wa).start()
            gather(j1, rows_b, sem_gb).wait()
            write(j1, rows_b, sem_wb).start()
            write(j0, rows_a, sem_wa).wait()

            @pl.when(p + 1 < nb // 2)
            def _():
                gather(j0 + 2, rows_a, sem_ga).start()

        write(nb - 1, rows_b, sem_wb).wait()

    return k(table, idx)


def _experts_kernel(te_ref, na_ref, x_ref, w1_ref, w3_ref, w2_ref, o_ref, w1b_ref, w3b_ref, w2b_ref):
    j = pl.program_id(0)
    na = na_ref[0]
    jj = jnp.minimum(j, na - 1)
    e = te_ref[jj]
    prev = te_ref[jnp.maximum(jj - 1, 0)]
    active = j < na

    @pl.when(jnp.logical_and(active, jnp.logical_or(j == 0, e != prev)))
    def _():
        w1b_ref[...] = w1_ref[0, 0].astype(BF16)
        w3b_ref[...] = w3_ref[0, 0].astype(BF16)
        w2b_ref[...] = w2_ref[0, 0].astype(BF16)

    @pl.when(active)
    def _():
        lo, hi = _unpack_bf16_pair(x_ref[...])
        lo = lo.astype(BF16)
        hi = hi.astype(BF16)
        a = _dot(lo, w1b_ref[0:HALF_D, :]) + _dot(hi, w1b_ref[HALF_D:, :])
        b = _dot(lo, w3b_ref[0:HALF_D, :]) + _dot(hi, w3b_ref[HALF_D:, :])
        y = _dot((_silu(a) * b).astype(BF16), w2b_ref[...])
        o_ref[...] = _pack_bf16_pair(y[:, :HALF_D], y[:, HALF_D:])


def _experts(xs, tile_expert, n_active, w1, w3, w2, layer, tm, tag):
    r = xs.shape[0]
    n_tiles = r // tm

    def xmap(j, te, na):
        return (jnp.minimum(j, na[0] - 1), 0)

    def wmap(j, te, na):
        return (layer, te[jnp.minimum(j, na[0] - 1)], 0, 0)

    grid_spec = pltpu.PrefetchScalarGridSpec(
        num_scalar_prefetch=2,
        grid=(n_tiles,),
        in_specs=[
            pl.BlockSpec((tm, HALF_D), xmap),
            pl.BlockSpec((1, 1, D_MODEL, EXPERT_FF), wmap),
            pl.BlockSpec((1, 1, D_MODEL, EXPERT_FF), wmap),
            pl.BlockSpec((1, 1, EXPERT_FF, D_MODEL), wmap),
        ],
        out_specs=pl.BlockSpec((tm, HALF_D), xmap),
        scratch_shapes=[
            pltpu.VMEM((D_MODEL, EXPERT_FF), BF16),
            pltpu.VMEM((D_MODEL, EXPERT_FF), BF16),
            pltpu.VMEM((EXPERT_FF, D_MODEL), BF16),
        ],
    )
    return pl.pallas_call(
        _experts_kernel,
        grid_spec=grid_spec,
        out_shape=jax.ShapeDtypeStruct((r, HALF_D), I32),
        compiler_params=_cparams(("arbitrary",)),
        name="experts" + tag,
    )(tile_expert, n_active, xs, w1, w3, w2)


def _combine_kernel(yp_ref, gate_ref, hp_ref, sw1_ref, sw3_ref, sw2_ref, x_ref, mod_ref, g_ref, b_ref, *refs,
                    n_ctx_tiles):
    outs = refs[:-3]
    w1b_ref, w3b_ref, w2b_ref = refs[-3:]

    @pl.when(pl.program_id(0) == 0)
    def _():
        w1b_ref[...] = sw1_ref[...].astype(BF16)
        w3b_ref[...] = sw3_ref[...].astype(BF16)
        w2b_ref[...] = sw2_ref[...].astype(BF16)

    lo, hi = _unpack_bf16_pair(hp_ref[...])
    lo = lo.astype(BF16)
    hi = hi.astype(BF16)
    a = _dot(lo, w1b_ref[0:HALF_D, :]) + _dot(hi, w1b_ref[HALF_D:, :])
    b = _dot(lo, w3b_ref[0:HALF_D, :]) + _dot(hi, w3b_ref[HALF_D:, :])
    shared = _dot((_silu(a) * b).astype(BF16), w2b_ref[...])
    acc_lo = shared[:, :HALF_D]
    acc_hi = shared[:, HALF_D:]
    gates = gate_ref[...]
    for k in range(TOP_K):
        ylo, yhi = _unpack_bf16_pair(yp_ref[k])
        gk = gates[:, k:k + 1]
        acc_lo = acc_lo + gk * ylo
        acc_hi = acc_hi + gk * yhi
    moe = jnp.concatenate([acc_lo, acc_hi], axis=1)
    y = DEEPNORM_ALPHA * x_ref[...] + mod_ref[0, 5:6, :] * moe
    res = _ln_plain(y, LN_EPS) * g_ref[...] + b_ref[...]
    if len(outs) == 1:
        outs[0][...] = res
    else:
        @pl.when(pl.program_id(0) < n_ctx_tiles)
        def _():
            outs[0][...] = res

        @pl.when(pl.program_id(0) >= n_ctx_tiles)
        def _():
            outs[1][...] = res


def _combine(yp, gate8, hp, sw1, sw3, sw2, x1, mod_l, g2, b2, lay, tm, split_out):
    n_tiles = lay.n // tm
    n_ctx_tiles = lay.n_ctx // tm
    row = lambda i: (i, 0)
    const = lambda i: (0, 0)
    if split_out:
        out_specs = _group_specs(2, tm, D_MODEL, n_ctx_tiles)
        out_shape = [jax.ShapeDtypeStruct((lay.n_ctx, D_MODEL), F32), jax.ShapeDtypeStruct((lay.n_lat, D_MODEL), F32)]
    else:
        out_specs = pl.BlockSpec((tm, D_MODEL), row)
        out_shape = jax.ShapeDtypeStruct((lay.n, D_MODEL), F32)
    return pl.pallas_call(
        functools.partial(_combine_kernel, n_ctx_tiles=n_ctx_tiles),
        grid=(n_tiles,),
        in_specs=[
            pl.BlockSpec((TOP_K, tm, HALF_D), lambda i: (0, i, 0)),
            pl.BlockSpec((tm, TOP_K), row),
            pl.BlockSpec((tm, HALF_D), row),
            pl.BlockSpec((D_MODEL, EXPERT_FF), const),
            pl.BlockSpec((D_MODEL, EXPERT_FF), const),
            pl.BlockSpec((EXPERT_FF, D_MODEL), const),
            pl.BlockSpec((tm, D_MODEL), row),
            pl.BlockSpec((1, N_MOD, D_MODEL), lambda i: (lay.cond_row(i, tm), 0, 0)),
            pl.BlockSpec((1, D_MODEL), const),
            pl.BlockSpec((1, D_MODEL), const),
        ],
        out_specs=out_specs,
        out_shape=out_shape,
        scratch_shapes=[
            pltpu.VMEM((D_MODEL, EXPERT_FF), BF16),
            pltpu.VMEM((D_MODEL, EXPERT_FF), BF16),
            pltpu.VMEM((EXPERT_FF, D_MODEL), BF16),
        ],
        compiler_params=_cparams(("arbitrary",)),
        name="combine_norm" + lay.tag,
    )(yp, gate8, hp, sw1, sw3, sw2, x1, mod_l, g2, b2)


def _moe_dispatch(hp, meta, counts, lay, tile):
    n = lay.n
    r_max = n * TOP_K + N_EXPERTS * tile
    n_tiles = r_max // tile
    cnt = counts.reshape(N_EXPERTS).astype(I32)
    padded = ((cnt + tile - 1) // tile) * tile
    ends = jnp.cumsum(padded)
    offsets = ends - padded
    idx8 = meta[:TOP_K]
    base8 = jnp.sum(jnp.where(idx8[:, :, None] == jnp.arange(N_EXPERTS, dtype=I32), offsets, 0), axis=-1)
    pos = (base8 + meta[TOP_K:]).astype(I32)
    tile_start = jnp.arange(n_tiles, dtype=I32) * tile
    tile_expert = jnp.minimum(jnp.sum(tile_start[:, None] >= ends[None, :], axis=1), N_EXPERTS - 1).astype(I32)
    n_active = (ends[-1] // tile).astype(I32).reshape(1)
    pos_b = pos.reshape(TOP_K, n // SC_BATCH, SC_BATCH).transpose(1, 0, 2)
    xs = _sc_scatter_rows(hp, pos_b, r_max, lay.tag)
    return xs, tile_expert, n_active, pos


def _moe_combine(ys, pos, gate8, hp, sw1, sw3, sw2, x1, mod_l, g2, b2, lay, split_out):
    n = lay.n
    yp = _sc_gather_rows(ys, pos.reshape(n * TOP_K), lay.tag).reshape(TOP_K, n, HALF_D)
    return _combine(yp, gate8.T, hp, sw1, sw3, sw2, x1, mod_l, g2, b2, lay, TOKEN_TILE, split_out)


def kernel(x_prompt, x_sample, cache_k, cache_v, state_hgrn, c, c_ctx, w_ada, b_ada, w_in, w_out, attn_sink, hgrn_lb, hgrn_norm, ln1_g, ln1_b, ln2_g, ln2_b, router_w, router_b, moe_w1, moe_w3, moe_w2, shared_w1, shared_w3, shared_w2):
    b_ctx, t_ctx, _ = x_prompt.shape
    b_lat, t_lat, _ = x_sample.shape
    past = cache_k.shape[2]
    lay = _Layout(b_ctx, t_ctx, b_lat, t_lat)
    tm = TOKEN_TILE
    assert 1 + b_lat <= COND_ROWS
    assert lay.n_ctx % tm == 0 and t_lat % tm == 0 and lay.n_ctx % t_lat == 0

    cond = jnp.concatenate([c_ctx[None, :], c, jnp.zeros((COND_ROWS - 1 - b_lat, D_MODEL), F32)], axis=0)
    mod = _adaln(cond, w_ada, b_ada).reshape(DEPTH, COND_ROWS, N_MOD, D_MODEL)

    lb_all = jnp.cumsum(jax.nn.softmax(hgrn_lb.astype(F32), axis=0), axis=0)
    lb_all = lb_all - lb_all[:1]
    lbp = jnp.stack([jnp.log(lb_all), jnp.log1p(-lb_all), 1.0 - lb_all], axis=2)

    cos_t, sin_t = _rope_tables(lay, tm)
    zero_state = jnp.zeros((b_ctx, 2, HGRN_HEADS, HGRN_DK, HGRN_DK), F32)

    x = (x_prompt.reshape(lay.n_ctx, D_MODEL), x_sample.reshape(lay.n_lat, D_MODEL))
    ks_out, vs_out, ss_out = [], [], []
    for l in range(DEPTH):
        proj, k_new, v_new = _inproj(x, mod[l], w_in, l, cos_t, sin_t, lay, tm)
        ks_out.append(k_new.reshape(b_ctx, t_ctx, N_KV_HEADS, HEAD_DIM))
        vs_out.append(v_new.reshape(b_ctx, t_ctx, N_KV_HEADS, HEAD_DIM))
        sink_l = attn_sink[l].reshape(1, N_HEADS)
        attn_c = _attn_context(proj, sink_l, lay)
        attn_l = _attn_latent(proj, cache_k[:, l].reshape(b_lat, past, KV_WIDTH),
                              cache_v[:, l].reshape(b_lat, past, KV_WIDTH), sink_l, lay)
        four_c = _fourier(proj, 0, b_ctx, t_ctx, t_ctx, "fourier_ctx")
        four_l = _fourier(proj, lay.n_ctx, b_lat, t_lat, min(t_lat, 512), "fourier_lat")
        gn_row = jnp.tile(hgrn_norm[l], HGRN_HEADS).reshape(1, HGRN_WIDTH)
        rec_c, s_fin = _hgrn(proj, 0, b_ctx, t_ctx, lbp[l], gn_row, zero_state, "hgrn_ctx")
        s0t = jnp.swapaxes(state_hgrn[:, l].astype(F32), -1, -2)
        rec_l, _ = _hgrn(proj, lay.n_ctx, b_lat, t_lat, lbp[l], gn_row, s0t, "hgrn_lat")
        ss_out.append(jnp.swapaxes(s_fin, -1, -2))

        x1, hp, meta, gate8, counts = _outproj(
            (attn_c, attn_l), (four_c, four_l), (rec_c, rec_l), x, mod[l], w_out, l, ln1_g[l].reshape(1, -1),
            ln1_b[l].reshape(1, -1), router_w[l].T, router_b[l].reshape(-1, 1), lay, tm)
        xs, tile_expert, n_active, pos = _moe_dispatch(hp, meta, counts, lay, EXPERT_TILE)
        ys = _experts(xs, tile_expert, n_active, moe_w1, moe_w3, moe_w2, l, EXPERT_TILE, lay.tag)
        x = _moe_combine(ys, pos, gate8, hp, shared_w1[l], shared_w3[l], shared_w2[l], x1, mod[l],
                         ln2_g[l].reshape(1, -1), ln2_b[l].reshape(1, -1), lay, split_out=(l == DEPTH - 1))

    y_prompt = x[0].reshape(b_ctx, t_ctx, D_MODEL)
    y_sample = x[1].reshape(b_lat, t_lat, D_MODEL)
    new_cache_k = jnp.stack(ks_out, axis=1)
    new_cache_v = jnp.stack(vs_out, axis=1)
    new_state = jnp.stack(ss_out, axis=1).astype(x_prompt.dtype)
    return (y_prompt, y_sample, new_cache_k, new_cache_v, new_state)
```

```python
import functools
import math

import numpy as np
import jax
import jax.numpy as jnp
from jax import lax
from jax.experimental import pallas as pl
from jax.experimental.pallas import tpu as pltpu
from jax.experimental.pallas import tpu_sc as plsc

F32 = jnp.float32
BF16 = jnp.bfloat16
I32 = jnp.int32

D_MODEL = 1024
HALF_D = D_MODEL // 2
DEPTH = 2
GRID_W = 64
ROPE_BASE = 10000.0
HEAD_DIM = 64
ATTN_WIDTH = 512
N_HEADS = 8
N_KV_HEADS = 2
KV_GROUP = 4
KV_WIDTH = N_KV_HEADS * HEAD_DIM
WINDOW = 128
ATTN_BLOCK = 128
FOURIER_WIDTH = 256
FOURIER_GROUPS = 4
HGRN_WIDTH = 256
HGRN_HEADS = 4
HGRN_DK = 64
HGRN_CHUNK = 64
IN_WIDTH = 2304
N_EXPERTS = 64
TOP_K = 8
EXPERT_FF = 256
ROUTED_SCALE = 2.5
N_MOD = 6
LN_EPS = 1e-5
ADA_EPS = 1e-6
GN_EPS = 1e-6
DEEPNORM_ALPHA = (2 * DEPTH) ** 0.25

COL_Q = 0
COL_K = 512
COL_V = 640
COL_U = 768
COL_HQ = 1024
COL_FF = 1280
COL_FB = 1536
COL_HI = 1792
COL_HG = 2048
ROPE_COLS = COL_V

V7X_LANES = 128
COND_ROWS = 16
NEG_BIG = -1e30
TOKEN_TILE = 512
EXPERT_TILE = 1024
SC_BATCH = 64

VMEM_LIMIT = 56 * 1024 * 1024


def _cparams(sem):
    return pltpu.CompilerParams(dimension_semantics=sem, vmem_limit_bytes=VMEM_LIMIT)


def _dot(a, b):
    return jnp.dot(a, b, preferred_element_type=F32)


def _dot_nt(a, b):
    return lax.dot_general(a, b, (((1,), (1,)), ((), ())), preferred_element_type=F32)


def _dot_tn(a, b):
    return lax.dot_general(a, b, (((0,), (0,)), ((), ())), preferred_element_type=F32)


def _split3(x):
    hi = x.astype(BF16)
    r1 = x - hi.astype(F32)
    mid = r1.astype(BF16)
    lo = (r1 - mid.astype(F32)).astype(BF16)
    return hi, mid, lo


def _dot_exact_lhs(m_bf16, x):
    hi, mid, lo = _split3(x)
    return _dot(m_bf16, hi) + _dot(m_bf16, mid) + _dot(m_bf16, lo)


def _dot_exact_rhs(x, m_bf16):
    hi, mid, lo = _split3(x)
    return _dot(hi, m_bf16) + _dot(mid, m_bf16) + _dot(lo, m_bf16)


def _dot_hp(a, b):
    a_hi = a.astype(BF16)
    a_lo = (a - a_hi.astype(F32)).astype(BF16)
    b_hi = b.astype(BF16)
    b_lo = (b - b_hi.astype(F32)).astype(BF16)
    return _dot(a_hi, b_hi) + _dot(a_hi, b_lo) + _dot(a_lo, b_hi)


def _pack_bf16_pair(lo, hi):
    return lax.bitcast_convert_type(pltpu.pack_elementwise([lo, hi], packed_dtype=BF16), I32)


def _unpack_bf16_pair(w):
    u = lax.bitcast_convert_type(w, jnp.uint32)
    lo = pltpu.unpack_elementwise(u, index=0, packed_dtype=BF16, unpacked_dtype=F32)
    hi = pltpu.unpack_elementwise(u, index=1, packed_dtype=BF16, unpacked_dtype=F32)
    return lo, hi


def _ln_plain(x, eps):
    mu = jnp.mean(x, axis=-1, keepdims=True)
    xc = x - mu
    var = jnp.mean(xc * xc, axis=-1, keepdims=True)
    return xc * lax.rsqrt(var + eps)


def _silu(x):
    return x * jax.nn.sigmoid(x)


def _adaln_kernel(c_ref, w_ref, b_ref, o_ref):
    s = _silu(c_ref[...])
    o_ref[0] = _dot_hp(s, w_ref[0]) + b_ref[0]


def _adaln(cond, w_ada, b_ada):
    return pl.pallas_call(
        _adaln_kernel,
        grid=(DEPTH, N_MOD),
        in_specs=[
            pl.BlockSpec((COND_ROWS, D_MODEL), lambda l, j: (0, 0)),
            pl.BlockSpec((1, D_MODEL, D_MODEL), lambda l, j: (l, 0, j)),
            pl.BlockSpec((1, 1, D_MODEL), lambda l, j: (l, 0, j)),
        ],
        out_specs=pl.BlockSpec((1, COND_ROWS, D_MODEL), lambda l, j: (l, 0, j)),
        out_shape=jax.ShapeDtypeStruct((DEPTH, COND_ROWS, N_MOD * D_MODEL), F32),
        compiler_params=_cparams(("arbitrary", "arbitrary")),
        name="adaln",
    )(cond, w_ada, b_ada.reshape(DEPTH, 1, N_MOD * D_MODEL))


class _Layout:
    def __init__(self, b_ctx, t_ctx, b_lat, t_lat, lat_first=0, tag=""):
        self.b_ctx, self.t_ctx, self.b_lat, self.t_lat = b_ctx, t_ctx, b_lat, t_lat
        self.lat_first = lat_first
        self.n_ctx = b_ctx * t_ctx
        self.n_lat = b_lat * t_lat
        self.n = self.n_ctx + self.n_lat
        self.tag = tag

    def cond_row(self, tile, tm):
        n_ctx_tiles = self.n_ctx // tm
        per_batch = self.t_lat // tm
        return jnp.where(tile < n_ctx_tiles, 0, 1 + self.lat_first + (tile - n_ctx_tiles) // per_batch)


def _group_specs(n_arrays, tm, width, n_ctx_tiles):
    if n_arrays == 1:
        return [pl.BlockSpec((tm, width), lambda i: (i, 0))]
    return [pl.BlockSpec((tm, width), lambda i: (jnp.minimum(i, n_ctx_tiles - 1), 0)),
            pl.BlockSpec((tm, width), lambda i: (jnp.maximum(i - n_ctx_tiles, 0), 0))]


def _inproj_kernel(*refs, n_ctx_tiles):
    n_tail = 8 if n_ctx_tiles > 0 else 6
    xs = refs[:-n_tail]
    mod_ref, w_ref, cos_ref, sin_ref, o_ref = refs[-n_tail:-n_tail + 5]
    wb_ref = refs[-1]

    @pl.when(pl.program_id(0) == 0)
    def _():
        wb_ref[...] = w_ref[0].astype(BF16)

    if len(xs) == 2:
        x = jnp.where(pl.program_id(0) < n_ctx_tiles, xs[0][...], xs[1][...])
    else:
        x = xs[0][...]
    shift = mod_ref[0, 0:1, :]
    scale = mod_ref[0, 1:2, :]
    h = (_ln_plain(x, ADA_EPS) * (1.0 + scale) + shift).astype(BF16)
    p = _dot(h, wb_ref[...])
    cos = cos_ref[...]
    sin = sin_ref[...]
    lane = lax.broadcasted_iota(I32, cos.shape, 1)
    first_half = (lane & 31) < 16
    for cb in range(ROPE_COLS // V7X_LANES):
        seg = p[:, cb * V7X_LANES:(cb + 1) * V7X_LANES]
        partner = jnp.where(first_half, pltpu.roll(seg, V7X_LANES - 16, 1), pltpu.roll(seg, 16, 1))
        o_ref[:, cb * V7X_LANES:(cb + 1) * V7X_LANES] = seg * cos + partner * sin
    o_ref[:, ROPE_COLS:] = p[:, ROPE_COLS:]

    if n_ctx_tiles > 0:
        kc_ref, vc_ref = refs[-3], refs[-2]

        @pl.when(pl.program_id(0) < n_ctx_tiles)
        def _():
            kc_ref[...] = p[:, COL_K:COL_K + KV_WIDTH]
            vc_ref[...] = p[:, COL_V:COL_V + KV_WIDTH]


def _rope_tables(lay, tm):
    t = lay.t_lat
    pos = jnp.arange(t)
    row = (pos // GRID_W).astype(F32)
    col = (pos % GRID_W).astype(F32)
    n_freq = HEAD_DIM // 4
    inv = ROPE_BASE ** (-jnp.arange(n_freq, dtype=F32) / n_freq)
    ang_r = row[:, None] * inv
    ang_c = col[:, None] * inv
    ang = jnp.concatenate([ang_r, ang_r, ang_c, ang_c], axis=1)
    sign = jnp.concatenate([-jnp.ones(n_freq), jnp.ones(n_freq), -jnp.ones(n_freq), jnp.ones(n_freq)]).astype(F32)
    cos = jnp.cos(ang)
    sin = jnp.sin(ang) * sign
    cos = jnp.concatenate([jnp.ones((tm, HEAD_DIM), F32), cos], axis=0)
    sin = jnp.concatenate([jnp.zeros((tm, HEAD_DIM), F32), sin], axis=0)
    return jnp.tile(cos, (1, 2)), jnp.tile(sin, (1, 2))


def _inproj(x, mod_l, w_in, layer, cos_t, sin_t, lay, tm):
    n_tiles = lay.n // tm
    n_ctx_tiles = lay.n_ctx // tm
    per_batch = lay.t_lat // tm

    def tbl(i):
        return jnp.where(i < n_ctx_tiles, 0, 1 + (i - n_ctx_tiles) % per_batch)

    xs = x if isinstance(x, tuple) else (x,)
    kv_specs, kv_shapes = [], []
    if n_ctx_tiles > 0:
        kv_specs = [pl.BlockSpec((tm, KV_WIDTH), lambda i: (jnp.minimum(i, n_ctx_tiles - 1), 0))] * 2
        kv_shapes = [jax.ShapeDtypeStruct((lay.n_ctx, KV_WIDTH), F32)] * 2
    return pl.pallas_call(
        functools.partial(_inproj_kernel, n_ctx_tiles=n_ctx_tiles),
        grid=(n_tiles,),
        in_specs=[
            *_group_specs(len(xs), tm, D_MODEL, n_ctx_tiles),
            pl.BlockSpec((1, N_MOD, D_MODEL), lambda i: (lay.cond_row(i, tm), 0, 0)),
            pl.BlockSpec((1, D_MODEL, IN_WIDTH), lambda i: (layer, 0, 0), pipeline_mode=pl.Buffered(1)),
            pl.BlockSpec((tm, V7X_LANES), lambda i: (tbl(i), 0)),
            pl.BlockSpec((tm, V7X_LANES), lambda i: (tbl(i), 0)),
        ],
        out_specs=[pl.BlockSpec((tm, IN_WIDTH), lambda i: (i, 0))] + kv_specs,
        out_shape=[jax.ShapeDtypeStruct((lay.n, IN_WIDTH), F32)] + kv_shapes,
        scratch_shapes=[pltpu.VMEM((D_MODEL, IN_WIDTH), BF16)],
        compiler_params=_cparams(("arbitrary",)),
        name="inproj" + lay.tag,
    )(*xs, mod_l, w_in, cos_t, sin_t)


def _attn_kernel(sink_ref, q_ref, *refs, n_local, has_ctx, t_total):
    o_ref = refs[-1]
    k_refs = refs[:n_local]
    v_refs = refs[n_local:2 * n_local]
    tq = q_ref.shape[0]
    scale = HEAD_DIM ** -0.5
    k_parts = [kr[...] for kr in k_refs]
    v_parts = [vr[...] for vr in v_refs]
    if has_ctx:
        k_parts.append(refs[2 * n_local][0])
        v_parts.append(refs[2 * n_local + 1][0])
    kall = jnp.concatenate(k_parts, axis=0) if len(k_parts) > 1 else k_parts[0]
    vall = jnp.concatenate(v_parts, axis=0) if len(v_parts) > 1 else v_parts[0]
    nk = kall.shape[0]
    k_sw = pltpu.roll(kall, HEAD_DIM, 1)
    v_sw = pltpu.roll(vall, HEAD_DIM, 1)
    lo_half = lax.broadcasted_iota(I32, (1, V7X_LANES), 1) < HEAD_DIM
    er = jnp.where(lax.broadcasted_iota(I32, (2 * nk, V7X_LANES), 0) < nk, 0, 1)
    el = jnp.where(lax.broadcasted_iota(I32, (2 * nk, V7X_LANES), 1) < HEAD_DIM, 0, 1)
    ones_blk = jnp.where(er == el, 1.0, 0.0).astype(BF16)
    if n_local > 1:
        i = pl.program_id(1)
        band = refs[-2][...]
        first_blk = jnp.where(i == 0, NEG_BIG, 0.0)
        last_blk = jnp.where(i == t_total // tq - 1, NEG_BIG, 0.0)

        def mask_local(sc):
            loc = sc[:, :n_local * tq] + band
            parts = [loc[:, :tq] + first_blk, loc[:, tq:(n_local - 1) * tq], loc[:, (n_local - 1) * tq:] + last_blk]
            return jnp.concatenate(parts + [sc[:, n_local * tq:]], axis=1)
    else:
        mask_local = None
    for g in range(N_KV_HEADS):
        k_own, k_oth = (kall, k_sw) if g == 0 else (k_sw, kall)
        v_own, v_oth = (vall, v_sw) if g == 0 else (v_sw, vall)
        k2 = jnp.concatenate([jnp.where(lo_half, k_own, 0.0), jnp.where(lo_half, 0.0, k_oth)], axis=0).astype(BF16)
        v2 = jnp.concatenate([jnp.where(lo_half, v_own, 0.0), jnp.where(lo_half, 0.0, v_oth)], axis=0).astype(BF16)
        v2e = jnp.concatenate([v2, ones_blk], axis=1)
        pairs = [2 * g, 2 * g + 1]
        qq = jnp.concatenate([q_ref[:, p * V7X_LANES:(p + 1) * V7X_LANES] for p in pairs], axis=0)
        qq = (qq * scale).astype(BF16)
        sink_a = jnp.concatenate([jnp.full((tq, 1), sink_ref[0, 2 * p], F32) for p in pairs], axis=0)
        sink_b = jnp.concatenate([jnp.full((tq, 1), sink_ref[0, 2 * p + 1], F32) for p in pairs], axis=0)
        s = _dot_nt(qq, k2)
        s_a = s[:, :nk]
        s_b = s[:, nk:]
        if mask_local is not None:
            s_a = mask_local(s_a)
            s_b = mask_local(s_b)
        m_a = jnp.maximum(jnp.max(s_a, axis=1, keepdims=True), sink_a)
        m_b = jnp.maximum(jnp.max(s_b, axis=1, keepdims=True), sink_b)
        pe = jnp.concatenate([jnp.exp(s_a - m_a).astype(BF16), jnp.exp(s_b - m_b).astype(BF16)], axis=1)
        acc = _dot(pe, v2e)
        sink_term = jnp.where(lo_half, jnp.exp(sink_a - m_a), jnp.exp(sink_b - m_b))
        o = acc[:, :V7X_LANES] / (acc[:, V7X_LANES:] + sink_term)
        for j, p in enumerate(pairs):
            o_ref[:, p * V7X_LANES:(p + 1) * V7X_LANES] = o[j * tq:(j + 1) * tq].astype(o_ref.dtype)


def _attn_context(proj, sink_l, lay):
    t = lay.t_ctx
    kb, vb = COL_K // KV_WIDTH, COL_V // KV_WIDTH
    body = functools.partial(_attn_kernel, n_local=1, has_ctx=False, t_total=t)
    return pl.pallas_call(
        body,
        grid=(lay.b_ctx,),
        in_specs=[
            pl.BlockSpec(memory_space=pltpu.SMEM),
            pl.BlockSpec((t, ATTN_WIDTH), lambda b: (b, 0)),
            pl.BlockSpec((t, KV_WIDTH), lambda b: (b, kb)),
            pl.BlockSpec((t, KV_WIDTH), lambda b: (b, vb)),
        ],
        out_specs=pl.BlockSpec((t, ATTN_WIDTH), lambda b: (b, 0)),
        out_shape=jax.ShapeDtypeStruct((lay.n_ctx, ATTN_WIDTH), BF16),
        compiler_params=_cparams(("arbitrary",)),
        name="attn_ctx",
    )(sink_l, proj, proj, proj)


def _attn_latent(proj, k_ctx, v_ctx, sink_l, lay):
    t = lay.t_lat
    tq = ATTN_BLOCK
    nq = t // tq
    base = lay.n_ctx // tq
    kb, vb = COL_K // KV_WIDTH, COL_V // KV_WIDTH
    past = k_ctx.shape[1]

    def rows(off):
        return lambda b, i: base + b * nq + jnp.clip(i + off, 0, nq - 1)

    def kv_specs(col):
        return [pl.BlockSpec((tq, KV_WIDTH), (lambda b, i, f=rows(off): (f(b, i), col))) for off in (-1, 0, 1)]

    body = functools.partial(_attn_kernel, n_local=3, has_ctx=True, t_total=t)
    rel = np.arange(3 * tq)[None, :] - tq - (np.arange(2 * tq)[:, None] % tq)
    band = jnp.asarray(np.where(np.abs(rel) <= WINDOW, 0.0, NEG_BIG).astype(np.float32))
    return pl.pallas_call(
        body,
        grid=(lay.b_lat, nq),
        in_specs=[
            pl.BlockSpec(memory_space=pltpu.SMEM),
            pl.BlockSpec((tq, ATTN_WIDTH), lambda b, i: (base + b * nq + i, 0)),
            *kv_specs(kb),
            *kv_specs(vb),
            pl.BlockSpec((1, past, KV_WIDTH), lambda b, i: (b, 0, 0)),
            pl.BlockSpec((1, past, KV_WIDTH), lambda b, i: (b, 0, 0)),
            pl.BlockSpec(band.shape, lambda b, i: (0, 0)),
        ],
        out_specs=pl.BlockSpec((tq, ATTN_WIDTH), lambda b, i: (b * nq + i, 0)),
        out_shape=jax.ShapeDtypeStruct((lay.n_lat, ATTN_WIDTH), BF16),
        compiler_params=_cparams(("arbitrary", "arbitrary")),
        name="attn_lat" + lay.tag,
    )(sink_l, proj, proj, proj, proj, proj, proj, proj, k_ctx, v_ctx, band)


def _fourier_kernel(cs_ref, u_ref, cc_ref, sc_ref, o_ref, csb_ref, *, scale):
    @pl.when(pl.program_id(1) == 0)
    def _():
        csb_ref[...] = cs_ref[...].astype(BF16)

    z = u_ref[...].astype(BF16)
    zc = _dot(z, cc_ref[...].astype(BF16)).astype(BF16)
    zs = _dot(z, sc_ref[...].astype(BF16)).astype(BF16)
    zz = jnp.concatenate([zc, zs], axis=0)
    o_ref[...] = (_dot(csb_ref[...], zz) * scale).astype(o_ref.dtype)


@functools.lru_cache(maxsize=None)
def _dft_tables(t):
    idx = np.arange(t, dtype=np.int64)
    ang = 2.0 * np.pi * ((idx[:, None] * idx[None, :]) % t).astype(np.float64) / t
    cs = np.concatenate([np.cos(ang), -np.sin(ang)], axis=1).astype(np.float32)
    cw = FOURIER_WIDTH // FOURIER_GROUPS
    cidx = np.arange(cw, dtype=np.int64)
    cang = 2.0 * np.pi * ((cidx[:, None] * cidx[None, :]) % cw).astype(np.float64) / cw
    eye = np.eye(FOURIER_GROUPS)
    cc = np.kron(eye, np.cos(cang)).astype(np.float32)
    sc = np.kron(eye, np.sin(cang)).astype(np.float32)
    return cs, cc, sc


def _fourier(proj, row0, b, t, tm, name):
    cs, cc, sc = _dft_tables(t)
    cw = FOURIER_WIDTH // FOURIER_GROUPS
    nt = t // tm
    ub = COL_U // FOURIER_WIDTH
    base = row0 // t
    body = functools.partial(_fourier_kernel, scale=1.0 / math.sqrt(t * cw))
    return pl.pallas_call(
        body,
        grid=(nt, b),
        in_specs=[
            pl.BlockSpec((tm, 2 * t), lambda i, bb: (i, 0)),
            pl.BlockSpec((t, FOURIER_WIDTH), lambda i, bb: (base + bb, ub)),
            pl.BlockSpec((FOURIER_WIDTH, FOURIER_WIDTH), lambda i, bb: (0, 0)),
            pl.BlockSpec((FOURIER_WIDTH, FOURIER_WIDTH), lambda i, bb: (0, 0)),
        ],
        out_specs=pl.BlockSpec((tm, FOURIER_WIDTH), lambda i, bb: (bb * nt + i, 0)),
        out_shape=jax.ShapeDtypeStruct((b * t, FOURIER_WIDTH), BF16),
        scratch_shapes=[pltpu.VMEM((tm, 2 * t), BF16)],
        compiler_params=_cparams(("arbitrary", "arbitrary")),
        name=name,
    )(jnp.asarray(cs), proj, jnp.asarray(cc), jnp.asarray(sc))


HGRN_LEVELS = (64, 32, 16, 8, 4, 2)
HGRN_SAFE_RANGE = 80.0


@functools.lru_cache(maxsize=None)
def _hgrn_tables():
    c = HGRN_CHUNK
    return np.stack([np.tril(np.ones((c, c))), np.triu(np.ones((c, c)))]).astype(np.float32)


def _boundary_rows(b, m, reverse):
    c, w = b.shape
    half = m // 2
    off = half if reverse else half - 1
    if m >= 16:
        return jnp.concatenate(
            [jnp.broadcast_to(b[s + off:s + off + 1], (m, w)) for s in range(0, c, m)], axis=0)
    sub = lax.broadcasted_iota(I32, (c, w), 0) & 7
    b3 = b.reshape(c // 8, 8, w)

    def bcast(j):
        return jnp.broadcast_to(b3[:, j:j + 1, :], (c // 8, 8, w)).reshape(c, w)

    if m == 8:
        return bcast(off)
    if m == 4:
        return jnp.where(sub < 4, bcast(off), bcast(4 + off))
    assert m == 2
    if reverse:
        return jnp.where((sub & 1) == 1, b, pltpu.roll(b, c - 1, 0))
    return jnp.where((sub & 1) == 0, b, pltpu.roll(b, 1, 0))


def _hgrn_gates(q, z, v, loglb, log1mlb, onemlb, cum, reverse):
    c = HGRN_CHUNK
    log_sig = jnp.minimum(z, 0.0) - jnp.log1p(jnp.exp(-jnp.abs(z)))
    bb = log1mlb + log_sig
    mx = jnp.maximum(loglb, bb)
    lf = mx + jnp.log1p(jnp.exp(-jnp.abs(loglb - bb)))
    kk = onemlb * jax.nn.sigmoid(-z)
    b = _dot_exact_lhs(cum, lf)
    b_end = b[0:1] if reverse else b[c - 1:c]
    qt = (q * jnp.exp(b)).astype(BF16)
    kt = (kk * jnp.exp(b_end - b)).astype(BF16)
    return (q, kk, b), (qt, kt, v.astype(BF16))


class _HgrnDir:
    def __init__(self, f32_parts, bf16_parts, reverse):
        c = HGRN_CHUNK
        self.q, self.kk, self.b = f32_parts
        self.qt, self.kt, self.vb = bf16_parts
        self.reverse = reverse
        b_end = self.b[0:1] if reverse else self.b[c - 1:c]
        self.decay = jnp.exp(b_end)
        mid = c // 2 if reverse else c // 2 - 1
        self.rel = self.b - self.b[mid:mid + 1]
        self.span = jnp.max(jnp.abs(self.rel))

    def tree_decay_matrices(self):
        c = HGRN_CHUNK
        q, kk, b = self.q, self.kk, self.b
        row = lax.broadcasted_iota(I32, (c, 1), 0)
        ti = lax.broadcasted_iota(I32, (c, c), 0)
        si = lax.broadcasted_iota(I32, (c, c), 1)
        qb = q.astype(BF16)
        kb = kk.astype(BF16)
        heads = [slice(h * HGRN_DK, (h + 1) * HGRN_DK) for h in range(HGRN_HEADS)]
        acc = [jnp.where(ti == si, _dot_nt(qb[:, sl], kb[:, sl]), 0.0) for sl in heads]
        for m in HGRN_LEVELS:
            r = _boundary_rows(b, m, self.reverse)
            upper = (row & (m - 1)) >= (m // 2)
            q_side = jnp.logical_not(upper) if self.reverse else upper
            e = jnp.exp(jnp.where(q_side, b - r, r - b))
            qf = jnp.where(q_side, q * e, 0.0).astype(BF16)
            kf = jnp.where(q_side, 0.0, kk * e).astype(BF16)
            same_block = (ti & -m) == (si & -m)
            for h, sl in enumerate(heads):
                acc[h] = acc[h] + jnp.where(same_block, _dot_nt(qf[:, sl], kf[:, sl]), 0.0)
        return acc

    def midpoint_decay_matrices(self):
        c = HGRN_CHUNK
        ti = lax.broadcasted_iota(I32, (c, c), 0)
        si = lax.broadcasted_iota(I32, (c, c), 1)
        qm = (self.q * jnp.exp(self.rel)).astype(BF16)
        km = (self.kk * jnp.exp(-self.rel)).astype(BF16)
        causal = (si >= ti) if self.reverse else (si <= ti)
        return [jnp.where(causal, _dot_nt(qm[:, h * HGRN_DK:(h + 1) * HGRN_DK], km[:, h * HGRN_DK:(h + 1) * HGRN_DK]), 0.0)
                for h in range(HGRN_HEADS)]

    def outputs(self, a_heads, st_ref, d):
        outs = []
        for h in range(HGRN_HEADS):
            sl = slice(h * HGRN_DK, (h + 1) * HGRN_DK)
            st = st_ref[d, h]
            o = _dot_nt(self.qt[:, sl], st.astype(BF16)) + _dot(a_heads[h].astype(BF16), self.vb[:, sl])
            st_ref[d, h] = st * self.decay[:, sl] + _dot_tn(self.vb[:, sl], self.kt[:, sl])
            outs.append(o)
        return jnp.concatenate(outs, axis=1)


def _hgrn_kernel(hq_ref, ff_ref, fb_ref, hi_ref, hg_ref, lbp_ref, gn_ref, mall_ref, ones_ref, s0_ref,
                 rec_ref, sfin_ref, st_ref, of_ref, ob_ref, gf_ref, gb_ref, a_ref, *, t):
    c = HGRN_CHUNK
    n = t // c
    st_ref[...] = s0_ref[0]

    def chunk_rows(ci):
        return pl.ds(pl.multiple_of(ci * c, c), c), pl.ds(pl.multiple_of((n - 1 - ci) * c, c), c)

    def gates_to(slot, ci):
        rf, rb = chunk_rows(ci)
        for d, (rows, f_ref) in enumerate(((rf, ff_ref), (rb, fb_ref))):
            f32_parts, bf16_parts = _hgrn_gates(
                hq_ref[rows, :], f_ref[rows, :], hi_ref[rows, :], lbp_ref[d, 0:1, :], lbp_ref[d, 1:2, :],
                lbp_ref[d, 2:3, :], mall_ref[d].astype(BF16), d == 1)
            for j in range(3):
                gf_ref[slot, d, j] = f32_parts[j]
                gb_ref[slot, d, j] = bf16_parts[j]

    gates_to(0, 0)

    def body(ci, carry):
        slot = ci & 1
        rf, rb = chunk_rows(ci)
        fwd = _HgrnDir([gf_ref[slot, 0, j] for j in range(3)], [gb_ref[slot, 0, j] for j in range(3)], False)
        bwd = _HgrnDir([gf_ref[slot, 1, j] for j in range(3)], [gb_ref[slot, 1, j] for j in range(3)], True)
        for d, hd in enumerate((fwd, bwd)):
            for h, a in enumerate(hd.midpoint_decay_matrices()):
                a_ref[d, h] = a
        gates_to(1 - slot, jnp.minimum(ci + 1, n - 1))

        @pl.when(jnp.maximum(fwd.span, bwd.span) > HGRN_SAFE_RANGE)
        def _():
            for d, hd in enumerate((fwd, bwd)):
                for h, a in enumerate(hd.tree_decay_matrices()):
                    a_ref[d, h] = a

        of_ref[rf, :] = fwd.outputs([a_ref[0, h] for h in range(HGRN_HEADS)], st_ref, 0)
        ob_ref[rb, :] = bwd.outputs([a_ref[1, h] for h in range(HGRN_HEADS)], st_ref, 1)
        return carry

    lax.fori_loop(0, n, body, 0)
    sfin_ref[0] = st_ref[...]
    o = of_ref[...] + ob_ref[...]
    ms = _dot_exact_rhs(o * o, ones_ref[...].astype(BF16)) * (1.0 / HGRN_DK)
    o = o * lax.rsqrt(ms + GN_EPS) * gn_ref[...]
    rec_ref[...] = (o * _silu(hg_ref[...])).astype(rec_ref.dtype)


def _hgrn(proj, row0, b, t, lbp, gn_row, s0t, name):
    base = row0 // t
    m_all = jnp.asarray(_hgrn_tables())
    ones_bd = jnp.asarray(np.kron(np.eye(HGRN_HEADS), np.ones((HGRN_DK, HGRN_DK))).astype(np.float32))

    def col(cstart):
        return pl.BlockSpec((t, HGRN_WIDTH), lambda bb, cb=cstart // HGRN_WIDTH: (base + bb, cb))

    const2 = lambda bb: (0, 0)
    const3 = lambda bb: (0, 0, 0)
    st_shape = (2, HGRN_HEADS, HGRN_DK, HGRN_DK)
    body = functools.partial(_hgrn_kernel, t=t)
    return pl.pallas_call(
        body,
        grid=(b,),
        in_specs=[
            col(COL_HQ), col(COL_FF), col(COL_FB), col(COL_HI), col(COL_HG),
            pl.BlockSpec((2, 3, HGRN_WIDTH), const3),
            pl.BlockSpec((1, HGRN_WIDTH), const2),
            pl.BlockSpec(m_all.shape, const3),
            pl.BlockSpec(ones_bd.shape, const2),
            pl.BlockSpec((1,) + st_shape, lambda bb: (bb, 0, 0, 0, 0)),
        ],
        out_specs=[
            pl.BlockSpec((t, HGRN_WIDTH), lambda bb: (bb, 0)),
            pl.BlockSpec((1,) + st_shape, lambda bb: (bb, 0, 0, 0, 0)),
        ],
        out_shape=[
            jax.ShapeDtypeStruct((b * t, HGRN_WIDTH), BF16),
            jax.ShapeDtypeStruct((b,) + st_shape, F32),
        ],
        scratch_shapes=[
            pltpu.VMEM(st_shape, F32),
            pltpu.VMEM((t, HGRN_WIDTH), F32),
            pltpu.VMEM((t, HGRN_WIDTH), F32),
            pltpu.VMEM((2, 2, 3, HGRN_CHUNK, HGRN_WIDTH), F32),
            pltpu.VMEM((2, 2, 3, HGRN_CHUNK, HGRN_WIDTH), BF16),
            pltpu.VMEM((2, HGRN_HEADS, HGRN_CHUNK, HGRN_CHUNK), F32),
        ],
        compiler_params=_cparams(("arbitrary",)),
        name=name,
    )(proj, proj, proj, proj, proj, lbp, gn_row, m_all, ones_bd, s0t)


def _outproj_kernel(*refs, n_ctx_tiles, n_mix):
    mix = refs[:3 * n_mix]
    refs = refs[3 * n_mix:]
    xs = refs[:-14]
    (mod_ref, w_ref, g_ref, b_ref, rw_ref, rb_ref, x1_ref, hp_ref, meta_ref, gate_ref, cnt_ref, wb_ref, tri_ref,
     run_ref) = refs[-14:]
    tm = x1_ref.shape[0]
    is_ctx = pl.program_id(0) < n_ctx_tiles
    x_in = jnp.where(is_ctx, xs[0][...], xs[1][...]) if len(xs) == 2 else xs[0][...]
    if n_mix == 2:
        attn, four, rec = [jnp.where(is_ctx, mix[2 * j][...], mix[2 * j + 1][...]) for j in range(3)]
    else:
        attn, four, rec = [r[...] for r in mix]

    @pl.when(pl.program_id(0) == 0)
    def _():
        wb_ref[...] = w_ref[0].astype(BF16)
        r = lax.broadcasted_iota(I32, (tm, tm), 0)
        c = lax.broadcasted_iota(I32, (tm, tm), 1)
        tri_ref[...] = jnp.where(r < c, 1.0, 0.0).astype(BF16)
        run_ref[...] = jnp.zeros_like(run_ref)

    out = _dot(attn, wb_ref[0:ATTN_WIDTH, :])
    out = out + _dot(four, wb_ref[ATTN_WIDTH:ATTN_WIDTH + FOURIER_WIDTH, :])
    out = out + _dot(rec, wb_ref[ATTN_WIDTH + FOURIER_WIDTH:, :])
    gate1 = mod_ref[0, 2:3, :]
    y = DEEPNORM_ALPHA * x_in + gate1 * out
    x1 = _ln_plain(y, LN_EPS) * g_ref[...] + b_ref[...]
    x1_ref[...] = x1
    h2 = _ln_plain(x1, ADA_EPS) * (1.0 + mod_ref[0, 4:5, :]) + mod_ref[0, 3:4, :]
    hp_ref[...] = _pack_bf16_pair(h2[:, :HALF_D], h2[:, HALF_D:])

    h_hi = h2.astype(BF16)
    h_lo = (h2 - h_hi.astype(F32)).astype(BF16)
    rwt = rw_ref[...]
    w_hi = rwt.astype(BF16)
    w_lo = (rwt - w_hi.astype(F32)).astype(BF16)
    scores = jax.nn.sigmoid(_dot_nt(w_hi, h_hi) + _dot_nt(w_hi, h_lo) + _dot_nt(w_lo, h_hi))
    remaining = scores + rb_ref[...]
    eidx = lax.broadcasted_iota(I32, scores.shape, 0).astype(F32)
    chosen = jnp.zeros(scores.shape, jnp.bool_)
    picks = []
    for _ in range(TOP_K):
        mx = jnp.max(remaining, axis=0, keepdims=True)
        first = jnp.min(jnp.where(remaining == mx, eidx, float(N_EXPERTS)), axis=0, keepdims=True)
        pick = eidx == first
        picks.append((pick, first))
        chosen = jnp.logical_or(chosen, pick)
        remaining = jnp.where(pick, -jnp.inf, remaining)
    sel = jnp.where(chosen, scores, 0.0)
    gates = sel / jnp.sum(sel, axis=0, keepdims=True) * ROUTED_SCALE

    onehot = jnp.where(chosen, 1.0, 0.0)
    rank = run_ref[...] + _dot(onehot.astype(BF16), tri_ref[...])
    run_ref[...] += jnp.sum(onehot, axis=1, keepdims=True)
    cnt_ref[...] = run_ref[...]

    ids, rks, gks = [], [], []
    for pick, first in picks:
        ids.append(first.astype(I32))
        rks.append(jnp.sum(jnp.where(pick, rank, 0.0), axis=0, keepdims=True).astype(I32))
        gks.append(jnp.sum(jnp.where(pick, gates, 0.0), axis=0, keepdims=True))
    meta_ref[...] = jnp.concatenate(ids + rks, axis=0)
    gate_ref[...] = jnp.concatenate(gks, axis=0)


def _outproj(attn, four, rec, x, mod_l, w_out, layer, g1, b1, rw, rb, lay, tm):
    n_tiles = lay.n // tm
    n_ctx_tiles = lay.n_ctx // tm
    row = lambda i: (i, 0)
    const = lambda i: (0, 0)
    xs = x if isinstance(x, tuple) else (x,)
    return pl.pallas_call(
        functools.partial(_outproj_kernel, n_ctx_tiles=n_ctx_tiles, n_mix=len(attn)),
        grid=(n_tiles,),
        in_specs=[
            *_group_specs(len(attn), tm, ATTN_WIDTH, n_ctx_tiles),
            *_group_specs(len(four), tm, FOURIER_WIDTH, n_ctx_tiles),
            *_group_specs(len(rec), tm, HGRN_WIDTH, n_ctx_tiles),
            *_group_specs(len(xs), tm, D_MODEL, n_ctx_tiles),
            pl.BlockSpec((1, N_MOD, D_MODEL), lambda i: (lay.cond_row(i, tm), 0, 0)),
            pl.BlockSpec((1, D_MODEL, D_MODEL), lambda i: (layer, 0, 0)),
            pl.BlockSpec((1, D_MODEL), const),
            pl.BlockSpec((1, D_MODEL), const),
            pl.BlockSpec((N_EXPERTS, D_MODEL), const),
            pl.BlockSpec((N_EXPERTS, 1), const),
        ],
        out_specs=[
            pl.BlockSpec((tm, D_MODEL), row),
            pl.BlockSpec((tm, HALF_D), row),
            pl.BlockSpec((2 * TOP_K, tm), lambda i: (0, i)),
            pl.BlockSpec((TOP_K, tm), lambda i: (0, i)),
            pl.BlockSpec((N_EXPERTS, 1), const),
        ],
        out_shape=[
            jax.ShapeDtypeStruct((lay.n, D_MODEL), F32),
            jax.ShapeDtypeStruct((lay.n, HALF_D), I32),
            jax.ShapeDtypeStruct((2 * TOP_K, lay.n), I32),
            jax.ShapeDtypeStruct((TOP_K, lay.n), F32),
            jax.ShapeDtypeStruct((N_EXPERTS, 1), F32),
        ],
        scratch_shapes=[
            pltpu.VMEM((D_MODEL, D_MODEL), BF16),
            pltpu.VMEM((tm, tm), BF16),
            pltpu.VMEM((N_EXPERTS, 1), F32),
        ],
        compiler_params=_cparams(("arbitrary",)),
        name="outproj_router" + lay.tag,
    )(*attn, *four, *rec, *xs, mod_l, w_out, g1, b1, rw, rb)


def _sc_workers():
    info = plsc.get_sparse_core_info()
    return info.num_cores, info.num_cores * info.num_subcores


def _sc_scatter_rows(rows, pos_b, r_out, tag=""):
    nc, nw = _sc_workers()
    n, w = rows.shape
    nbt, copies, _ = pos_b.shape
    assert nbt * SC_BATCH == n and nbt % (2 * nw) == 0
    per_w = nbt // nw
    mesh = plsc.VectorSubcoreMesh(core_axis_name="c", subcore_axis_name="s")

    @functools.partial(
        pl.kernel, mesh=mesh, out_type=jax.ShapeDtypeStruct((r_out, w), rows.dtype),
        scratch_types=[pltpu.VMEM((copies, SC_BATCH), I32), pltpu.VMEM((copies, SC_BATCH), I32),
                       pltpu.VMEM((SC_BATCH, w), rows.dtype), pltpu.VMEM((SC_BATCH, w), rows.dtype),
                       pltpu.SemaphoreType.DMA, pltpu.SemaphoreType.DMA,
                       pltpu.SemaphoreType.DMA, pltpu.SemaphoreType.DMA],
        name="sc_dispatch" + tag)
    def k(rows_hbm, pos_hbm, out_hbm, idx_a, idx_b, rows_a, rows_b, sem_ra, sem_rb, sem_sa, sem_sb):
        wid = lax.axis_index("s") * nc + lax.axis_index("c")
        first = wid * per_w

        def reads(j, idx_v, rows_v, sem):
            bt = first + j
            return (pltpu.make_async_copy(pos_hbm.at[bt], idx_v, sem),
                    pltpu.make_async_copy(rows_hbm.at[pl.ds(bt * SC_BATCH, SC_BATCH)], rows_v, sem))

        def scatters(idx_v, rows_v, sem):
            return [pltpu.make_async_copy(rows_v, out_hbm.at[idx_v.at[q]], sem) for q in range(copies)]

        def start(descs):
            for d in descs:
                d.start()

        def wait(descs):
            for d in descs:
                d.wait()

        start(reads(0, idx_a, rows_a, sem_ra))

        @pl.loop(0, per_w // 2)
        def _(p):
            j0 = 2 * p
            j1 = j0 + 1

            @pl.when(p > 0)
            def _():
                wait(scatters(idx_b, rows_b, sem_sb))

            start(reads(j1, idx_b, rows_b, sem_rb))
            wait(reads(j0, idx_a, rows_a, sem_ra))
            start(scatters(idx_a, rows_a, sem_sa))
            wait(reads(j1, idx_b, rows_b, sem_rb))
            start(scatters(idx_b, rows_b, sem_sb))
            wait(scatters(idx_a, rows_a, sem_sa))

            @pl.when(p + 1 < per_w // 2)
            def _():
                start(reads(j0 + 2, idx_a, rows_a, sem_ra))

        wait(scatters(idx_b, rows_b, sem_sb))

    return k(rows, pos_b)


def _sc_gather_rows(table, idx, tag=""):
    nc, nw = _sc_workers()
    r = idx.shape[0]
    w = table.shape[1]
    assert r % (2 * nw * SC_BATCH) == 0
    per_w = r // nw
    nb = per_w // SC_BATCH
    mesh = plsc.VectorSubcoreMesh(core_axis_name="c", subcore_axis_name="s")

    @functools.partial(
        pl.kernel, mesh=mesh, out_type=jax.ShapeDtypeStruct((r, w), table.dtype),
        scratch_types=[pltpu.VMEM((per_w,), I32),
                       pltpu.VMEM((SC_BATCH, w), table.dtype), pltpu.VMEM((SC_BATCH, w), table.dtype),
                       pltpu.SemaphoreType.DMA, pltpu.SemaphoreType.DMA,
                       pltpu.SemaphoreType.DMA, pltpu.SemaphoreType.DMA],
        name="sc_combine" + tag)
    def k(table_hbm, idx_hbm, out_hbm, idx_v, rows_a, rows_b, sem_ga, sem_gb, sem_wa, sem_wb):
        wid = lax.axis_index("s") * nc + lax.axis_index("c")
        base = wid * per_w
        pltpu.sync_copy(idx_hbm.at[pl.ds(base, per_w)], idx_v)

        def gather(j, rows_v, sem):
            return pltpu.make_async_copy(table_hbm.at[idx_v.at[pl.ds(j * SC_BATCH, SC_BATCH)]], rows_v, sem)

        def write(j, rows_v, sem):
            return pltpu.make_async_copy(rows_v, out_hbm.at[pl.ds(base + j * SC_BATCH, SC_BATCH)], sem)

        gather(0, rows_a, sem_ga).start()

        @pl.loop(0, nb // 2)
        def _(p):
            j0 = 2 * p
            j1 = j0 + 1

            @pl.when(p > 0)
            def _():
                write(j1 - 2, rows_b, sem_wb).wait()

            gather(j1, rows_b, sem_gb).start()
            gather(j0, rows_a, sem_ga).wait()
            write(j0, rows_a, sem_wa).start()
            gather(j1, rows_b, sem_gb).wait()
            write(j1, rows_b, sem_wb).start()
            write(j0, rows_a, sem_wa).wait()

            @pl.when(p + 1 < nb // 2)
            def _():
                gather(j0 + 2, rows_a, sem_ga).start()

        write(nb - 1, rows_b, sem_wb).wait()

    return k(table, idx)


def _experts_kernel(te_ref, na_ref, x_ref, w1_ref, w3_ref, w2_ref, o_ref, w1b_ref, w3b_ref, w2b_ref):
    j = pl.program_id(0)
    na = na_ref[0]
    jj = jnp.minimum(j, na - 1)
    e = te_ref[jj]
    prev = te_ref[jnp.maximum(jj - 1, 0)]
    active = j < na

    @pl.when(jnp.logical_and(active, jnp.logical_or(j == 0, e != prev)))
    def _():
        w1b_ref[...] = w1_ref[0, 0].astype(BF16)
        w3b_ref[...] = w3_ref[0, 0].astype(BF16)
        w2b_ref[...] = w2_ref[0, 0].astype(BF16)

    @pl.when(active)
    def _():
        lo, hi = _unpack_bf16_pair(x_ref[...])
        lo = lo.astype(BF16)
        hi = hi.astype(BF16)
        a = _dot(lo, w1b_ref[0:HALF_D, :]) + _dot(hi, w1b_ref[HALF_D:, :])
        b = _dot(lo, w3b_ref[0:HALF_D, :]) + _dot(hi, w3b_ref[HALF_D:, :])
        y = _dot((_silu(a) * b).astype(BF16), w2b_ref[...])
        o_ref[...] = _pack_bf16_pair(y[:, :HALF_D], y[:, HALF_D:])


def _experts(xs, tile_expert, n_active, w1, w3, w2, layer, tm, tag):
    r = xs.shape[0]
    n_tiles = r // tm

    def xmap(j, te, na):
        return (jnp.minimum(j, na[0] - 1), 0)

    def wmap(j, te, na):
        return (layer, te[jnp.minimum(j, na[0] - 1)], 0, 0)

    grid_spec = pltpu.PrefetchScalarGridSpec(
        num_scalar_prefetch=2,
        grid=(n_tiles,),
        in_specs=[
            pl.BlockSpec((tm, HALF_D), xmap),
            pl.BlockSpec((1, 1, D_MODEL, EXPERT_FF), wmap),
            pl.BlockSpec((1, 1, D_MODEL, EXPERT_FF), wmap),
            pl.BlockSpec((1, 1, EXPERT_FF, D_MODEL), wmap),
        ],
        out_specs=pl.BlockSpec((tm, HALF_D), xmap),
        scratch_shapes=[
            pltpu.VMEM((D_MODEL, EXPERT_FF), BF16),
            pltpu.VMEM((D_MODEL, EXPERT_FF), BF16),
            pltpu.VMEM((EXPERT_FF, D_MODEL), BF16),
        ],
    )
    return pl.pallas_call(
        _experts_kernel,
        grid_spec=grid_spec,
        out_shape=jax.ShapeDtypeStruct((r, HALF_D), I32),
        compiler_params=_cparams(("arbitrary",)),
        name="experts" + tag,
    )(tile_expert, n_active, xs, w1, w3, w2)


def _combine_kernel(yp_ref, gate_ref, hp_ref, sw1_ref, sw3_ref, sw2_ref, x_ref, mod_ref, g_ref, b_ref, *refs,
                    n_ctx_tiles):
    outs = refs[:-3]
    w1b_ref, w3b_ref, w2b_ref = refs[-3:]

    @pl.when(pl.program_id(0) == 0)
    def _():
        w1b_ref[...] = sw1_ref[...].astype(BF16)
        w3b_ref[...] = sw3_ref[...].astype(BF16)
        w2b_ref[...] = sw2_ref[...].astype(BF16)

    lo, hi = _unpack_bf16_pair(hp_ref[...])
    lo = lo.astype(BF16)
    hi = hi.astype(BF16)
    a = _dot(lo, w1b_ref[0:HALF_D, :]) + _dot(hi, w1b_ref[HALF_D:, :])
    b = _dot(lo, w3b_ref[0:HALF_D, :]) + _dot(hi, w3b_ref[HALF_D:, :])
    shared = _dot((_silu(a) * b).astype(BF16), w2b_ref[...])
    acc_lo = shared[:, :HALF_D]
    acc_hi = shared[:, HALF_D:]
    gates = gate_ref[...]
    for k in range(TOP_K):
        ylo, yhi = _unpack_bf16_pair(yp_ref[k])
        gk = gates[:, k:k + 1]
        acc_lo = acc_lo + gk * ylo
        acc_hi = acc_hi + gk * yhi
    moe = jnp.concatenate([acc_lo, acc_hi], axis=1)
    y = DEEPNORM_ALPHA * x_ref[...] + mod_ref[0, 5:6, :] * moe
    res = _ln_plain(y, LN_EPS) * g_ref[...] + b_ref[...]
    if len(outs) == 1:
        outs[0][...] = res
    else:
        @pl.when(pl.program_id(0) < n_ctx_tiles)
        def _():
            outs[0][...] = res

        @pl.when(pl.program_id(0) >= n_ctx_tiles)
        def _():
            outs[1][...] = res


def _combine(yp, gate8, hp, sw1, sw3, sw2, x1, mod_l, g2, b2, lay, tm, split_out):
    n_tiles = lay.n // tm
    n_ctx_tiles = lay.n_ctx // tm
    row = lambda i: (i, 0)
    const = lambda i: (0, 0)
    if split_out:
        out_specs = _group_specs(2, tm, D_MODEL, n_ctx_tiles)
        out_shape = [jax.ShapeDtypeStruct((lay.n_ctx, D_MODEL), F32), jax.ShapeDtypeStruct((lay.n_lat, D_MODEL), F32)]
    else:
        out_specs = pl.BlockSpec((tm, D_MODEL), row)
        out_shape = jax.ShapeDtypeStruct((lay.n, D_MODEL), F32)
    return pl.pallas_call(
        functools.partial(_combine_kernel, n_ctx_tiles=n_ctx_tiles),
        grid=(n_tiles,),
        in_specs=[
            pl.BlockSpec((TOP_K, tm, HALF_D), lambda i: (0, i, 0)),
            pl.BlockSpec((tm, TOP_K), row),
            pl.BlockSpec((tm, HALF_D), row),
            pl.BlockSpec((D_MODEL, EXPERT_FF), const),
            pl.BlockSpec((D_MODEL, EXPERT_FF), const),
            pl.BlockSpec((EXPERT_FF, D_MODEL), const),
            pl.BlockSpec((tm, D_MODEL), row),
            pl.BlockSpec((1, N_MOD, D_MODEL), lambda i: (lay.cond_row(i, tm), 0, 0)),
            pl.BlockSpec((1, D_MODEL), const),
            pl.BlockSpec((1, D_MODEL), const),
        ],
        out_specs=out_specs,
        out_shape=out_shape,
        scratch_shapes=[
            pltpu.VMEM((D_MODEL, EXPERT_FF), BF16),
            pltpu.VMEM((D_MODEL, EXPERT_FF), BF16),
            pltpu.VMEM((EXPERT_FF, D_MODEL), BF16),
        ],
        compiler_params=_cparams(("arbitrary",)),
        name="combine_norm" + lay.tag,
    )(yp, gate8, hp, sw1, sw3, sw2, x1, mod_l, g2, b2)


def _moe_dispatch(hp, meta, counts, lay, tile):
    n = lay.n
    r_max = n * TOP_K + N_EXPERTS * tile
    n_tiles = r_max // tile
    cnt = counts.reshape(N_EXPERTS).astype(I32)
    padded = ((cnt + tile - 1) // tile) * tile
    ends = jnp.cumsum(padded)
    offsets = ends - padded
    idx8 = meta[:TOP_K]
    base8 = jnp.sum(jnp.where(idx8[:, :, None] == jnp.arange(N_EXPERTS, dtype=I32), offsets, 0), axis=-1)
    pos = (base8 + meta[TOP_K:]).astype(I32)
    tile_start = jnp.arange(n_tiles, dtype=I32) * tile
    tile_expert = jnp.minimum(jnp.sum(tile_start[:, None] >= ends[None, :], axis=1), N_EXPERTS - 1).astype(I32)
    n_active = (ends[-1] // tile).astype(I32).reshape(1)
    pos_b = pos.reshape(TOP_K, n // SC_BATCH, SC_BATCH).transpose(1, 0, 2)
    xs = _sc_scatter_rows(hp, pos_b, r_max, lay.tag)
    return xs, tile_expert, n_active, pos


def _moe_combine(ys, pos, gate8, hp, sw1, sw3, sw2, x1, mod_l, g2, b2, lay, split_out):
    n = lay.n
    yp = _sc_gather_rows(ys, pos.reshape(n * TOP_K), lay.tag).reshape(TOP_K, n, HALF_D)
    return _combine(yp, gate8.T, hp, sw1, sw3, sw2, x1, mod_l, g2, b2, lay, TOKEN_TILE, split_out)


def kernel(x_prompt, x_sample, cache_k, cache_v, state_hgrn, c, c_ctx, w_ada, b_ada, w_in, w_out, attn_sink, hgrn_lb, hgrn_norm, ln1_g, ln1_b, ln2_g, ln2_b, router_w, router_b, moe_w1, moe_w3, moe_w2, shared_w1, shared_w3, shared_w2):
    b_ctx, t_ctx, _ = x_prompt.shape
    b_lat, t_lat, _ = x_sample.shape
    past = cache_k.shape[2]
    tm = TOKEN_TILE
    assert 1 + b_lat <= COND_ROWS
    lay = _Layout(b_ctx, t_ctx, b_lat, t_lat)
    assert lay.n_ctx % tm == 0 and t_lat % tm == 0 and lay.n_ctx % t_lat == 0

    cond = jnp.concatenate([c_ctx[None, :], c, jnp.zeros((COND_ROWS - 1 - b_lat, D_MODEL), F32)], axis=0)
    mod = _adaln(cond, w_ada, b_ada).reshape(DEPTH, COND_ROWS, N_MOD, D_MODEL)

    lb_all = jnp.cumsum(jax.nn.softmax(hgrn_lb.astype(F32), axis=0), axis=0)
    lb_all = lb_all - lb_all[:1]
    lbp = jnp.stack([jnp.log(lb_all), jnp.log1p(-lb_all), 1.0 - lb_all], axis=2)

    cos_t, sin_t = _rope_tables(lay, tm)
    zero_state = jnp.zeros((b_ctx, 2, HGRN_HEADS, HGRN_DK, HGRN_DK), F32)

    def layer(l, lay, x, split_out):
        lat = slice(lay.lat_first, lay.lat_first + lay.b_lat)
        outs = _inproj(x, mod[l], w_in, l, cos_t, sin_t, lay, tm)
        proj = outs[0]
        sink_l = attn_sink[l].reshape(1, N_HEADS)
        gn_row = jnp.tile(hgrn_norm[l], HGRN_HEADS).reshape(1, HGRN_WIDTH)
        attn, four, rec, extras = [], [], [], None
        if lay.b_ctx:
            attn.append(_attn_context(proj, sink_l, lay))
            four.append(_fourier(proj, 0, lay.b_ctx, t_ctx, t_ctx, "fourier_ctx"))
            rec_c, s_fin = _hgrn(proj, 0, lay.b_ctx, t_ctx, lbp[l], gn_row, zero_state, "hgrn_ctx")
            rec.append(rec_c)
            extras = (outs[1], outs[2], s_fin)
        attn.append(_attn_latent(proj, cache_k[lat, l].reshape(lay.b_lat, past, KV_WIDTH),
                                 cache_v[lat, l].reshape(lay.b_lat, past, KV_WIDTH), sink_l, lay))
        four.append(_fourier(proj, lay.n_ctx, lay.b_lat, t_lat, min(t_lat, 512), "fourier_lat" + lay.tag))
        s0t = jnp.swapaxes(state_hgrn[lat, l].astype(F32), -1, -2)
        rec.append(_hgrn(proj, lay.n_ctx, lay.b_lat, t_lat, lbp[l], gn_row, s0t, "hgrn_lat" + lay.tag)[0])
        x1, hp, meta, gate8, counts = _outproj(
            tuple(attn), tuple(four), tuple(rec), x, mod[l], w_out, l, ln1_g[l].reshape(1, -1),
            ln1_b[l].reshape(1, -1), router_w[l].T, router_b[l].reshape(-1, 1), lay, tm)
        xs, tile_expert, n_active, pos = _moe_dispatch(hp, meta, counts, lay, EXPERT_TILE)
        ys = _experts(xs, tile_expert, n_active, moe_w1, moe_w3, moe_w2, l, EXPERT_TILE, lay.tag)
        x = _moe_combine(ys, pos, gate8, hp, shared_w1[l], shared_w3[l], shared_w2[l], x1, mod[l],
                         ln2_g[l].reshape(1, -1), ln2_b[l].reshape(1, -1), lay, split_out)
        return x, extras

    x = (x_prompt.reshape(lay.n_ctx, D_MODEL), x_sample.reshape(lay.n_lat, D_MODEL))
    ks_out, vs_out, ss_out = [], [], []
    for l in range(DEPTH):
        x, (k_new, v_new, s_fin) = layer(l, lay, x, split_out=(l == DEPTH - 1))
        ks_out.append(k_new.reshape(b_ctx, t_ctx, N_KV_HEADS, HEAD_DIM))
        vs_out.append(v_new.reshape(b_ctx, t_ctx, N_KV_HEADS, HEAD_DIM))
        ss_out.append(jnp.swapaxes(s_fin, -1, -2))

    y_prompt = x[0].reshape(b_ctx, t_ctx, D_MODEL)
    y_sample = x[1].reshape(b_lat, t_lat, D_MODEL)
    new_cache_k = jnp.stack(ks_out, axis=1)
    new_cache_v = jnp.stack(vs_out, axis=1)
    new_state = jnp.stack(ss_out, axis=1).astype(x_prompt.dtype)
    return (y_prompt, y_sample, new_cache_k, new_cache_v, new_state)
```

```python
import functools
import math

import numpy as np
import jax
import jax.numpy as jnp
from jax import lax
from jax.experimental import pallas as pl
from jax.experimental.pallas import tpu as pltpu
from jax.experimental.pallas import tpu_sc as plsc

F32 = jnp.float32
BF16 = jnp.bfloat16
I32 = jnp.int32

D_MODEL = 1024
HALF_D = D_MODEL // 2
DEPTH = 2
GRID_W = 64
ROPE_BASE = 10000.0
HEAD_DIM = 64
ATTN_WIDTH = 512
N_HEADS = 8
N_KV_HEADS = 2
KV_GROUP = 4
KV_WIDTH = N_KV_HEADS * HEAD_DIM
WINDOW = 128
ATTN_BLOCK = 128
FOURIER_WIDTH = 256
FOURIER_GROUPS = 4
HGRN_WIDTH = 256
HGRN_HEADS = 4
HGRN_DK = 64
HGRN_CHUNK = 64
IN_WIDTH = 2304
N_EXPERTS = 64
TOP_K = 8
EXPERT_FF = 256
ROUTED_SCALE = 2.5
N_MOD = 6
LN_EPS = 1e-5
ADA_EPS = 1e-6
GN_EPS = 1e-6
DEEPNORM_ALPHA = (2 * DEPTH) ** 0.25

COL_Q = 0
COL_K = 512
COL_V = 640
COL_U = 768
COL_HQ = 1024
COL_FF = 1280
COL_FB = 1536
COL_HI = 1792
COL_HG = 2048
ROPE_COLS = COL_V

V7X_LANES = 128
COND_ROWS = 16
NEG_BIG = -1e30
TOKEN_TILE = 512
EXPERT_TILE = 1024
SC_BATCH = 64

VMEM_LIMIT = 56 * 1024 * 1024


def _cparams(sem):
    return pltpu.CompilerParams(dimension_semantics=sem, vmem_limit_bytes=VMEM_LIMIT)


def _dot(a, b):
    return jnp.dot(a, b, preferred_element_type=F32)


def _dot_nt(a, b):
    return lax.dot_general(a, b, (((1,), (1,)), ((), ())), preferred_element_type=F32)


def _dot_tn(a, b):
    return lax.dot_general(a, b, (((0,), (0,)), ((), ())), preferred_element_type=F32)


def _split3(x):
    hi = x.astype(BF16)
    r1 = x - hi.astype(F32)
    mid = r1.astype(BF16)
    lo = (r1 - mid.astype(F32)).astype(BF16)
    return hi, mid, lo


def _dot_exact_lhs(m_bf16, x):
    hi, mid, lo = _split3(x)
    return _dot(m_bf16, hi) + _dot(m_bf16, mid) + _dot(m_bf16, lo)


def _dot_exact_rhs(x, m_bf16):
    hi, mid, lo = _split3(x)
    return _dot(hi, m_bf16) + _dot(mid, m_bf16) + _dot(lo, m_bf16)


def _dot_hp(a, b):
    a_hi = a.astype(BF16)
    a_lo = (a - a_hi.astype(F32)).astype(BF16)
    b_hi = b.astype(BF16)
    b_lo = (b - b_hi.astype(F32)).astype(BF16)
    return _dot(a_hi, b_hi) + _dot(a_hi, b_lo) + _dot(a_lo, b_hi)


def _pack_bf16_pair(lo, hi):
    return lax.bitcast_convert_type(pltpu.pack_elementwise([lo, hi], packed_dtype=BF16), I32)


def _unpack_bf16_pair(w):
    u = lax.bitcast_convert_type(w, jnp.uint32)
    lo = pltpu.unpack_elementwise(u, index=0, packed_dtype=BF16, unpacked_dtype=F32)
    hi = pltpu.unpack_elementwise(u, index=1, packed_dtype=BF16, unpacked_dtype=F32)
    return lo, hi


def _ln_plain(x, eps):
    mu = jnp.mean(x, axis=-1, keepdims=True)
    xc = x - mu
    var = jnp.mean(xc * xc, axis=-1, keepdims=True)
    return xc * lax.rsqrt(var + eps)


def _silu(x):
    return x * jax.nn.sigmoid(x)


def _adaln_kernel(c_ref, w_ref, b_ref, o_ref):
    s = _silu(c_ref[...])
    o_ref[0] = _dot_hp(s, w_ref[0]) + b_ref[0]


def _adaln(cond, w_ada, b_ada):
    return pl.pallas_call(
        _adaln_kernel,
        grid=(DEPTH, N_MOD),
        in_specs=[
            pl.BlockSpec((COND_ROWS, D_MODEL), lambda l, j: (0, 0)),
            pl.BlockSpec((1, D_MODEL, D_MODEL), lambda l, j: (l, 0, j)),
            pl.BlockSpec((1, 1, D_MODEL), lambda l, j: (l, 0, j)),
        ],
        out_specs=pl.BlockSpec((1, COND_ROWS, D_MODEL), lambda l, j: (l, 0, j)),
        out_shape=jax.ShapeDtypeStruct((DEPTH, COND_ROWS, N_MOD * D_MODEL), F32),
        compiler_params=_cparams(("arbitrary", "arbitrary")),
        name="adaln",
    )(cond, w_ada, b_ada.reshape(DEPTH, 1, N_MOD * D_MODEL))


class _Layout:
    def __init__(self, b_ctx, t_ctx, b_lat, t_lat, lat_first=0, tag=""):
        self.b_ctx, self.t_ctx, self.b_lat, self.t_lat = b_ctx, t_ctx, b_lat, t_lat
        self.lat_first = lat_first
        self.n_ctx = b_ctx * t_ctx
        self.n_lat = b_lat * t_lat
        self.n = self.n_ctx + self.n_lat
        self.tag = tag

    def cond_row(self, tile, tm):
        n_ctx_tiles = self.n_ctx // tm
        per_batch = self.t_lat // tm
        return jnp.where(tile < n_ctx_tiles, 0, 1 + self.lat_first + (tile - n_ctx_tiles) // per_batch)


def _group_specs(n_arrays, tm, width, n_ctx_tiles):
    if n_arrays == 1:
        return [pl.BlockSpec((tm, width), lambda i: (i, 0))]
    return [pl.BlockSpec((tm, width), lambda i: (jnp.minimum(i, n_ctx_tiles - 1), 0)),
            pl.BlockSpec((tm, width), lambda i: (jnp.maximum(i - n_ctx_tiles, 0), 0))]


def _inproj_kernel(*refs, n_ctx_tiles):
    n_tail = 8 if n_ctx_tiles > 0 else 6
    xs = refs[:-n_tail]
    mod_ref, w_ref, cos_ref, sin_ref, o_ref = refs[-n_tail:-n_tail + 5]
    wb_ref = refs[-1]

    @pl.when(pl.program_id(0) == 0)
    def _():
        wb_ref[...] = w_ref[0].astype(BF16)

    if len(xs) == 2:
        x = jnp.where(pl.program_id(0) < n_ctx_tiles, xs[0][...], xs[1][...])
    else:
        x = xs[0][...]
    shift = mod_ref[0, 0:1, :]
    scale = mod_ref[0, 1:2, :]
    h = (_ln_plain(x, ADA_EPS) * (1.0 + scale) + shift).astype(BF16)
    p = _dot(h, wb_ref[...])
    cos = cos_ref[...]
    sin = sin_ref[...]
    lane = lax.broadcasted_iota(I32, cos.shape, 1)
    first_half = (lane & 31) < 16
    for cb in range(ROPE_COLS // V7X_LANES):
        seg = p[:, cb * V7X_LANES:(cb + 1) * V7X_LANES]
        partner = jnp.where(first_half, pltpu.roll(seg, V7X_LANES - 16, 1), pltpu.roll(seg, 16, 1))
        o_ref[:, cb * V7X_LANES:(cb + 1) * V7X_LANES] = seg * cos + partner * sin
    o_ref[:, ROPE_COLS:] = p[:, ROPE_COLS:]

    if n_ctx_tiles > 0:
        kc_ref, vc_ref = refs[-3], refs[-2]

        @pl.when(pl.program_id(0) < n_ctx_tiles)
        def _():
            kc_ref[...] = p[:, COL_K:COL_K + KV_WIDTH]
            vc_ref[...] = p[:, COL_V:COL_V + KV_WIDTH]


def _rope_tables(lay, tm):
    t = lay.t_lat
    pos = jnp.arange(t)
    row = (pos // GRID_W).astype(F32)
    col = (pos % GRID_W).astype(F32)
    n_freq = HEAD_DIM // 4
    inv = ROPE_BASE ** (-jnp.arange(n_freq, dtype=F32) / n_freq)
    ang_r = row[:, None] * inv
    ang_c = col[:, None] * inv
    ang = jnp.concatenate([ang_r, ang_r, ang_c, ang_c], axis=1)
    sign = jnp.concatenate([-jnp.ones(n_freq), jnp.ones(n_freq), -jnp.ones(n_freq), jnp.ones(n_freq)]).astype(F32)
    cos = jnp.cos(ang)
    sin = jnp.sin(ang) * sign
    cos = jnp.concatenate([jnp.ones((tm, HEAD_DIM), F32), cos], axis=0)
    sin = jnp.concatenate([jnp.zeros((tm, HEAD_DIM), F32), sin], axis=0)
    return jnp.tile(cos, (1, 2)), jnp.tile(sin, (1, 2))


def _inproj(x, mod_l, w_in, layer, cos_t, sin_t, lay, tm):
    n_tiles = lay.n // tm
    n_ctx_tiles = lay.n_ctx // tm
    per_batch = lay.t_lat // tm

    def tbl(i):
        return jnp.where(i < n_ctx_tiles, 0, 1 + (i - n_ctx_tiles) % per_batch)

    xs = x if isinstance(x, tuple) else (x,)
    kv_specs, kv_shapes = [], []
    if n_ctx_tiles > 0:
        kv_specs = [pl.BlockSpec((tm, KV_WIDTH), lambda i: (jnp.minimum(i, n_ctx_tiles - 1), 0))] * 2
        kv_shapes = [jax.ShapeDtypeStruct((lay.n_ctx, KV_WIDTH), F32)] * 2
    return pl.pallas_call(
        functools.partial(_inproj_kernel, n_ctx_tiles=n_ctx_tiles),
        grid=(n_tiles,),
        in_specs=[
            *_group_specs(len(xs), tm, D_MODEL, n_ctx_tiles),
            pl.BlockSpec((1, N_MOD, D_MODEL), lambda i: (lay.cond_row(i, tm), 0, 0)),
            pl.BlockSpec((1, D_MODEL, IN_WIDTH), lambda i: (layer, 0, 0), pipeline_mode=pl.Buffered(1)),
            pl.BlockSpec((tm, V7X_LANES), lambda i: (tbl(i), 0)),
            pl.BlockSpec((tm, V7X_LANES), lambda i: (tbl(i), 0)),
        ],
        out_specs=[pl.BlockSpec((tm, IN_WIDTH), lambda i: (i, 0))] + kv_specs,
        out_shape=[jax.ShapeDtypeStruct((lay.n, IN_WIDTH), F32)] + kv_shapes,
        scratch_shapes=[pltpu.VMEM((D_MODEL, IN_WIDTH), BF16)],
        compiler_params=_cparams(("arbitrary",)),
        name="inproj" + lay.tag,
    )(*xs, mod_l, w_in, cos_t, sin_t)


def _attn_kernel(sink_ref, q_ref, *refs, n_local, has_ctx, t_total):
    o_ref = refs[-1]
    k_refs = refs[:n_local]
    v_refs = refs[n_local:2 * n_local]
    tq = q_ref.shape[0]
    scale = HEAD_DIM ** -0.5
    k_parts = [kr[...] for kr in k_refs]
    v_parts = [vr[...] for vr in v_refs]
    if has_ctx:
        k_parts.append(refs[2 * n_local][0])
        v_parts.append(refs[2 * n_local + 1][0])
    kall = jnp.concatenate(k_parts, axis=0) if len(k_parts) > 1 else k_parts[0]
    vall = jnp.concatenate(v_parts, axis=0) if len(v_parts) > 1 else v_parts[0]
    nk = kall.shape[0]
    k_sw = pltpu.roll(kall, HEAD_DIM, 1)
    v_sw = pltpu.roll(vall, HEAD_DIM, 1)
    lo_half = lax.broadcasted_iota(I32, (1, V7X_LANES), 1) < HEAD_DIM
    er = jnp.where(lax.broadcasted_iota(I32, (2 * nk, V7X_LANES), 0) < nk, 0, 1)
    el = jnp.where(lax.broadcasted_iota(I32, (2 * nk, V7X_LANES), 1) < HEAD_DIM, 0, 1)
    ones_blk = jnp.where(er == el, 1.0, 0.0).astype(BF16)
    if n_local > 1:
        i = pl.program_id(1)
        band = refs[-2][...]
        first_blk = jnp.where(i == 0, NEG_BIG, 0.0)
        last_blk = jnp.where(i == t_total // tq - 1, NEG_BIG, 0.0)

        def mask_local(sc):
            loc = sc[:, :n_local * tq] + band
            parts = [loc[:, :tq] + first_blk, loc[:, tq:(n_local - 1) * tq], loc[:, (n_local - 1) * tq:] + last_blk]
            return jnp.concatenate(parts + [sc[:, n_local * tq:]], axis=1)
    else:
        mask_local = None
    for g in range(N_KV_HEADS):
        k_own, k_oth = (kall, k_sw) if g == 0 else (k_sw, kall)
        v_own, v_oth = (vall, v_sw) if g == 0 else (v_sw, vall)
        k2 = jnp.concatenate([jnp.where(lo_half, k_own, 0.0), jnp.where(lo_half, 0.0, k_oth)], axis=0).astype(BF16)
        v2 = jnp.concatenate([jnp.where(lo_half, v_own, 0.0), jnp.where(lo_half, 0.0, v_oth)], axis=0).astype(BF16)
        v2e = jnp.concatenate([v2, ones_blk], axis=1)
        pairs = [2 * g, 2 * g + 1]
        qq = jnp.concatenate([q_ref[:, p * V7X_LANES:(p + 1) * V7X_LANES] for p in pairs], axis=0)
        qq = (qq * scale).astype(BF16)
        sink_a = jnp.concatenate([jnp.full((tq, 1), sink_ref[0, 2 * p], F32) for p in pairs], axis=0)
        sink_b = jnp.concatenate([jnp.full((tq, 1), sink_ref[0, 2 * p + 1], F32) for p in pairs], axis=0)
        s = _dot_nt(qq, k2)
        s_a = s[:, :nk]
        s_b = s[:, nk:]
        if mask_local is not None:
            s_a = mask_local(s_a)
            s_b = mask_local(s_b)
        m_a = jnp.maximum(jnp.max(s_a, axis=1, keepdims=True), sink_a)
        m_b = jnp.maximum(jnp.max(s_b, axis=1, keepdims=True), sink_b)
        pe = jnp.concatenate([jnp.exp(s_a - m_a).astype(BF16), jnp.exp(s_b - m_b).astype(BF16)], axis=1)
        acc = _dot(pe, v2e)
        sink_term = jnp.where(lo_half, jnp.exp(sink_a - m_a), jnp.exp(sink_b - m_b))
        o = acc[:, :V7X_LANES] / (acc[:, V7X_LANES:] + sink_term)
        for j, p in enumerate(pairs):
            o_ref[:, p * V7X_LANES:(p + 1) * V7X_LANES] = o[j * tq:(j + 1) * tq].astype(o_ref.dtype)


def _attn_context(proj, sink_l, lay):
    t = lay.t_ctx
    kb, vb = COL_K // KV_WIDTH, COL_V // KV_WIDTH
    body = functools.partial(_attn_kernel, n_local=1, has_ctx=False, t_total=t)
    return pl.pallas_call(
        body,
        grid=(lay.b_ctx,),
        in_specs=[
            pl.BlockSpec(memory_space=pltpu.SMEM),
            pl.BlockSpec((t, ATTN_WIDTH), lambda b: (b, 0)),
            pl.BlockSpec((t, KV_WIDTH), lambda b: (b, kb)),
            pl.BlockSpec((t, KV_WIDTH), lambda b: (b, vb)),
        ],
        out_specs=pl.BlockSpec((t, ATTN_WIDTH), lambda b: (b, 0)),
        out_shape=jax.ShapeDtypeStruct((lay.n_ctx, ATTN_WIDTH), BF16),
        compiler_params=_cparams(("arbitrary",)),
        name="attn_ctx",
    )(sink_l, proj, proj, proj)


def _attn_latent(proj, k_ctx, v_ctx, sink_l, lay):
    t = lay.t_lat
    tq = ATTN_BLOCK
    nq = t // tq
    base = lay.n_ctx // tq
    kb, vb = COL_K // KV_WIDTH, COL_V // KV_WIDTH
    past = k_ctx.shape[1]

    def rows(off):
        return lambda b, i: base + b * nq + jnp.clip(i + off, 0, nq - 1)

    def kv_specs(col):
        return [pl.BlockSpec((tq, KV_WIDTH), (lambda b, i, f=rows(off): (f(b, i), col))) for off in (-1, 0, 1)]

    body = functools.partial(_attn_kernel, n_local=3, has_ctx=True, t_total=t)
    rel = np.arange(3 * tq)[None, :] - tq - (np.arange(2 * tq)[:, None] % tq)
    band = jnp.asarray(np.where(np.abs(rel) <= WINDOW, 0.0, NEG_BIG).astype(np.float32))
    return pl.pallas_call(
        body,
        grid=(lay.b_lat, nq),
        in_specs=[
            pl.BlockSpec(memory_space=pltpu.SMEM),
            pl.BlockSpec((tq, ATTN_WIDTH), lambda b, i: (base + b * nq + i, 0)),
            *kv_specs(kb),
            *kv_specs(vb),
            pl.BlockSpec((1, past, KV_WIDTH), lambda b, i: (b, 0, 0)),
            pl.BlockSpec((1, past, KV_WIDTH), lambda b, i: (b, 0, 0)),
            pl.BlockSpec(band.shape, lambda b, i: (0, 0)),
        ],
        out_specs=pl.BlockSpec((tq, ATTN_WIDTH), lambda b, i: (b * nq + i, 0)),
        out_shape=jax.ShapeDtypeStruct((lay.n_lat, ATTN_WIDTH), BF16),
        compiler_params=_cparams(("arbitrary", "arbitrary")),
        name="attn_lat" + lay.tag,
    )(sink_l, proj, proj, proj, proj, proj, proj, proj, k_ctx, v_ctx, band)


def _fourier_channels_kernel(u_ref, cc_ref, sc_ref, o_ref):
    t = u_ref.shape[0]
    z = u_ref[...].astype(BF16)
    o_ref[0, 0:t, :] = _dot(z, cc_ref[...].astype(BF16)).astype(BF16)
    o_ref[0, t:, :] = _dot(z, sc_ref[...].astype(BF16)).astype(BF16)


def _fourier_kernel(cs_ref, zz_ref, o_ref, csb_ref, *, scale):
    @pl.when(pl.program_id(1) == 0)
    def _():
        csb_ref[...] = cs_ref[...].astype(BF16)

    o_ref[...] = (_dot(csb_ref[...], zz_ref[0]) * scale).astype(o_ref.dtype)


@functools.lru_cache(maxsize=None)
def _dft_tables(t):
    idx = np.arange(t, dtype=np.int64)
    ang = 2.0 * np.pi * ((idx[:, None] * idx[None, :]) % t).astype(np.float64) / t
    cs = np.concatenate([np.cos(ang), -np.sin(ang)], axis=1).astype(np.float32)
    cw = FOURIER_WIDTH // FOURIER_GROUPS
    cidx = np.arange(cw, dtype=np.int64)
    cang = 2.0 * np.pi * ((cidx[:, None] * cidx[None, :]) % cw).astype(np.float64) / cw
    eye = np.eye(FOURIER_GROUPS)
    cc = np.kron(eye, np.cos(cang)).astype(np.float32)
    sc = np.kron(eye, np.sin(cang)).astype(np.float32)
    return cs, cc, sc


def _fourier(proj, row0, b, t, tm, name):
    cs, cc, sc = _dft_tables(t)
    cw = FOURIER_WIDTH // FOURIER_GROUPS
    nt = t // tm
    ub = COL_U // FOURIER_WIDTH
    base = row0 // t
    zz = pl.pallas_call(
        _fourier_channels_kernel,
        grid=(b,),
        in_specs=[
            pl.BlockSpec((t, FOURIER_WIDTH), lambda bb: (base + bb, ub)),
            pl.BlockSpec((FOURIER_WIDTH, FOURIER_WIDTH), lambda bb: (0, 0)),
            pl.BlockSpec((FOURIER_WIDTH, FOURIER_WIDTH), lambda bb: (0, 0)),
        ],
        out_specs=pl.BlockSpec((1, 2 * t, FOURIER_WIDTH), lambda bb: (bb, 0, 0)),
        out_shape=jax.ShapeDtypeStruct((b, 2 * t, FOURIER_WIDTH), BF16),
        compiler_params=_cparams(("arbitrary",)),
        name=name + "_channels",
    )(proj, jnp.asarray(cc), jnp.asarray(sc))
    body = functools.partial(_fourier_kernel, scale=1.0 / math.sqrt(t * cw))
    return pl.pallas_call(
        body,
        grid=(nt, b),
        in_specs=[
            pl.BlockSpec((tm, 2 * t), lambda i, bb: (i, 0)),
            pl.BlockSpec((1, 2 * t, FOURIER_WIDTH), lambda i, bb: (bb, 0, 0)),
        ],
        out_specs=pl.BlockSpec((tm, FOURIER_WIDTH), lambda i, bb: (bb * nt + i, 0)),
        out_shape=jax.ShapeDtypeStruct((b * t, FOURIER_WIDTH), BF16),
        scratch_shapes=[pltpu.VMEM((tm, 2 * t), BF16)],
        compiler_params=_cparams(("arbitrary", "arbitrary")),
        name=name,
    )(jnp.asarray(cs), zz)


HGRN_LEVELS = (64, 32, 16, 8, 4, 2)
HGRN_SAFE_RANGE = 80.0


@functools.lru_cache(maxsize=None)
def _hgrn_tables():
    c = HGRN_CHUNK
    return np.stack([np.tril(np.ones((c, c))), np.triu(np.ones((c, c)))]).astype(np.float32)


def _boundary_rows(b, m, reverse):
    c, w = b.shape
    half = m // 2
    off = half if reverse else half - 1
    if m >= 16:
        return jnp.concatenate(
            [jnp.broadcast_to(b[s + off:s + off + 1], (m, w)) for s in range(0, c, m)], axis=0)
    sub = lax.broadcasted_iota(I32, (c, w), 0) & 7
    b3 = b.reshape(c // 8, 8, w)

    def bcast(j):
        return jnp.broadcast_to(b3[:, j:j + 1, :], (c // 8, 8, w)).reshape(c, w)

    if m == 8:
        return bcast(off)
    if m == 4:
        return jnp.where(sub < 4, bcast(off), bcast(4 + off))
    assert m == 2
    if reverse:
        return jnp.where((sub & 1) == 1, b, pltpu.roll(b, c - 1, 0))
    return jnp.where((sub & 1) == 0, b, pltpu.roll(b, 1, 0))


def _hgrn_gates(q, z, v, loglb, log1mlb, onemlb, cum, reverse):
    c = HGRN_CHUNK
    log_sig = jnp.minimum(z, 0.0) - jnp.log1p(jnp.exp(-jnp.abs(z)))
    bb = log1mlb + log_sig
    mx = jnp.maximum(loglb, bb)
    lf = mx + jnp.log1p(jnp.exp(-jnp.abs(loglb - bb)))
    kk = onemlb * jax.nn.sigmoid(-z)
    b = _dot_exact_lhs(cum, lf)
    b_end = b[0:1] if reverse else b[c - 1:c]
    qt = (q * jnp.exp(b)).astype(BF16)
    kt = (kk * jnp.exp(b_end - b)).astype(BF16)
    return (q, kk, b), (qt, kt, v.astype(BF16))


class _HgrnDir:
    def __init__(self, f32_parts, bf16_parts, reverse):
        c = HGRN_CHUNK
        self.q, self.kk, self.b = f32_parts
        self.qt, self.kt, self.vb = bf16_parts
        self.reverse = reverse
        b_end = self.b[0:1] if reverse else self.b[c - 1:c]
        self.decay = jnp.exp(b_end)
        mid = c // 2 if reverse else c // 2 - 1
        self.rel = self.b - self.b[mid:mid + 1]
        self.span = jnp.max(jnp.abs(self.rel))

    def tree_decay_matrices(self):
        c = HGRN_CHUNK
        q, kk, b = self.q, self.kk, self.b
        row = lax.broadcasted_iota(I32, (c, 1), 0)
        ti = lax.broadcasted_iota(I32, (c, c), 0)
        si = lax.broadcasted_iota(I32, (c, c), 1)
        qb = q.astype(BF16)
        kb = kk.astype(BF16)
        heads = [slice(h * HGRN_DK, (h + 1) * HGRN_DK) for h in range(HGRN_HEADS)]
        acc = [jnp.where(ti == si, _dot_nt(qb[:, sl], kb[:, sl]), 0.0) for sl in heads]
        for m in HGRN_LEVELS:
            r = _boundary_rows(b, m, self.reverse)
            upper = (row & (m - 1)) >= (m // 2)
            q_side = jnp.logical_not(upper) if self.reverse else upper
            e = jnp.exp(jnp.where(q_side, b - r, r - b))
            qf = jnp.where(q_side, q * e, 0.0).astype(BF16)
            kf = jnp.where(q_side, 0.0, kk * e).astype(BF16)
            same_block = (ti & -m) == (si & -m)
            for h, sl in enumerate(heads):
                acc[h] = acc[h] + jnp.where(same_block, _dot_nt(qf[:, sl], kf[:, sl]), 0.0)
        return acc

    def midpoint_decay_matrices(self):
        c = HGRN_CHUNK
        ti = lax.broadcasted_iota(I32, (c, c), 0)
        si = lax.broadcasted_iota(I32, (c, c), 1)
        qm = (self.q * jnp.exp(self.rel)).astype(BF16)
        km = (self.kk * jnp.exp(-self.rel)).astype(BF16)
        causal = (si >= ti) if self.reverse else (si <= ti)
        return [jnp.where(causal, _dot_nt(qm[:, h * HGRN_DK:(h + 1) * HGRN_DK], km[:, h * HGRN_DK:(h + 1) * HGRN_DK]), 0.0)
                for h in range(HGRN_HEADS)]

    def outputs(self, a_heads, st_ref, d):
        outs = []
        for h in range(HGRN_HEADS):
            sl = slice(h * HGRN_DK, (h + 1) * HGRN_DK)
            st = st_ref[d, h]
            o = _dot_nt(self.qt[:, sl], st.astype(BF16)) + _dot(a_heads[h].astype(BF16), self.vb[:, sl])
            st_ref[d, h] = st * self.decay[:, sl] + _dot_tn(self.vb[:, sl], self.kt[:, sl])
            outs.append(o)
        return jnp.concatenate(outs, axis=1)


def _hgrn_kernel(hq_ref, ff_ref, fb_ref, hi_ref, hg_ref, lbp_ref, gn_ref, mall_ref, ones_ref, s0_ref,
                 rec_ref, sfin_ref, st_ref, of_ref, ob_ref, gf_ref, gb_ref, a_ref, *, t):
    c = HGRN_CHUNK
    n = t // c
    st_ref[...] = s0_ref[0]

    def chunk_rows(ci):
        return pl.ds(pl.multiple_of(ci * c, c), c), pl.ds(pl.multiple_of((n - 1 - ci) * c, c), c)

    def gates_to(slot, ci):
        rf, rb = chunk_rows(ci)
        for d, (rows, f_ref) in enumerate(((rf, ff_ref), (rb, fb_ref))):
            f32_parts, bf16_parts = _hgrn_gates(
                hq_ref[rows, :], f_ref[rows, :], hi_ref[rows, :], lbp_ref[d, 0:1, :], lbp_ref[d, 1:2, :],
                lbp_ref[d, 2:3, :], mall_ref[d].astype(BF16), d == 1)
            for j in range(3):
                gf_ref[slot, d, j] = f32_parts[j]
                gb_ref[slot, d, j] = bf16_parts[j]

    gates_to(0, 0)

    def body(ci, carry):
        slot = ci & 1
        rf, rb = chunk_rows(ci)
        fwd = _HgrnDir([gf_ref[slot, 0, j] for j in range(3)], [gb_ref[slot, 0, j] for j in range(3)], False)
        bwd = _HgrnDir([gf_ref[slot, 1, j] for j in range(3)], [gb_ref[slot, 1, j] for j in range(3)], True)
        for d, hd in enumerate((fwd, bwd)):
            for h, a in enumerate(hd.midpoint_decay_matrices()):
                a_ref[d, h] = a
        gates_to(1 - slot, jnp.minimum(ci + 1, n - 1))

        @pl.when(jnp.maximum(fwd.span, bwd.span) > HGRN_SAFE_RANGE)
        def _():
            for d, hd in enumerate((fwd, bwd)):
                for h, a in enumerate(hd.tree_decay_matrices()):
                    a_ref[d, h] = a

        of_ref[rf, :] = fwd.outputs([a_ref[0, h] for h in range(HGRN_HEADS)], st_ref, 0)
        ob_ref[rb, :] = bwd.outputs([a_ref[1, h] for h in range(HGRN_HEADS)], st_ref, 1)
        return carry

    lax.fori_loop(0, n, body, 0)
    sfin_ref[0] = st_ref[...]
    o = of_ref[...] + ob_ref[...]
    ms = _dot_exact_rhs(o * o, ones_ref[...].astype(BF16)) * (1.0 / HGRN_DK)
    o = o * lax.rsqrt(ms + GN_EPS) * gn_ref[...]
    rec_ref[...] = (o * _silu(hg_ref[...])).astype(rec_ref.dtype)


def _hgrn(proj, row0, b, t, lbp, gn_row, s0t, name):
    base = row0 // t
    m_all = jnp.asarray(_hgrn_tables())
    ones_bd = jnp.asarray(np.kron(np.eye(HGRN_HEADS), np.ones((HGRN_DK, HGRN_DK))).astype(np.float32))

    def col(cstart):
        return pl.BlockSpec((t, HGRN_WIDTH), lambda bb, cb=cstart // HGRN_WIDTH: (base + bb, cb))

    const2 = lambda bb: (0, 0)
    const3 = lambda bb: (0, 0, 0)
    st_shape = (2, HGRN_HEADS, HGRN_DK, HGRN_DK)
    body = functools.partial(_hgrn_kernel, t=t)
    return pl.pallas_call(
        body,
        grid=(b,),
        in_specs=[
            col(COL_HQ), col(COL_FF), col(COL_FB), col(COL_HI), col(COL_HG),
            pl.BlockSpec((2, 3, HGRN_WIDTH), const3),
            pl.BlockSpec((1, HGRN_WIDTH), const2),
            pl.BlockSpec(m_all.shape, const3),
            pl.BlockSpec(ones_bd.shape, const2),
            pl.BlockSpec((1,) + st_shape, lambda bb: (bb, 0, 0, 0, 0)),
        ],
        out_specs=[
            pl.BlockSpec((t, HGRN_WIDTH), lambda bb: (bb, 0)),
            pl.BlockSpec((1,) + st_shape, lambda bb: (bb, 0, 0, 0, 0)),
        ],
        out_shape=[
            jax.ShapeDtypeStruct((b * t, HGRN_WIDTH), BF16),
            jax.ShapeDtypeStruct((b,) + st_shape, F32),
        ],
        scratch_shapes=[
            pltpu.VMEM(st_shape, F32),
            pltpu.VMEM((t, HGRN_WIDTH), F32),
            pltpu.VMEM((t, HGRN_WIDTH), F32),
            pltpu.VMEM((2, 2, 3, HGRN_CHUNK, HGRN_WIDTH), F32),
            pltpu.VMEM((2, 2, 3, HGRN_CHUNK, HGRN_WIDTH), BF16),
            pltpu.VMEM((2, HGRN_HEADS, HGRN_CHUNK, HGRN_CHUNK), F32),
        ],
        compiler_params=_cparams(("arbitrary",)),
        name=name,
    )(proj, proj, proj, proj, proj, lbp, gn_row, m_all, ones_bd, s0t)


def _outproj_kernel(*refs, n_ctx_tiles, n_mix):
    mix = refs[:3 * n_mix]
    refs = refs[3 * n_mix:]
    xs = refs[:-14]
    (mod_ref, w_ref, g_ref, b_ref, rw_ref, rb_ref, x1_ref, hp_ref, meta_ref, gate_ref, cnt_ref, wb_ref, tri_ref,
     run_ref) = refs[-14:]
    tm = x1_ref.shape[0]
    is_ctx = pl.program_id(0) < n_ctx_tiles
    x_in = jnp.where(is_ctx, xs[0][...], xs[1][...]) if len(xs) == 2 else xs[0][...]
    if n_mix == 2:
        attn, four, rec = [jnp.where(is_ctx, mix[2 * j][...], mix[2 * j + 1][...]) for j in range(3)]
    else:
        attn, four, rec = [r[...] for r in mix]

    @pl.when(pl.program_id(0) == 0)
    def _():
        wb_ref[...] = w_ref[0].astype(BF16)
        r = lax.broadcasted_iota(I32, (tm, tm), 0)
        c = lax.broadcasted_iota(I32, (tm, tm), 1)
        tri_ref[...] = jnp.where(r < c, 1.0, 0.0).astype(BF16)
        run_ref[...] = jnp.zeros_like(run_ref)

    out = _dot(attn, wb_ref[0:ATTN_WIDTH, :])
    out = out + _dot(four, wb_ref[ATTN_WIDTH:ATTN_WIDTH + FOURIER_WIDTH, :])
    out = out + _dot(rec, wb_ref[ATTN_WIDTH + FOURIER_WIDTH:, :])
    gate1 = mod_ref[0, 2:3, :]
    y = DEEPNORM_ALPHA * x_in + gate1 * out
    x1 = _ln_plain(y, LN_EPS) * g_ref[...] + b_ref[...]
    x1_ref[...] = x1
    h2 = _ln_plain(x1, ADA_EPS) * (1.0 + mod_ref[0, 4:5, :]) + mod_ref[0, 3:4, :]
    hp_ref[...] = _pack_bf16_pair(h2[:, :HALF_D], h2[:, HALF_D:])

    h_hi = h2.astype(BF16)
    h_lo = (h2 - h_hi.astype(F32)).astype(BF16)
    rwt = rw_ref[...]
    w_hi = rwt.astype(BF16)
    w_lo = (rwt - w_hi.astype(F32)).astype(BF16)
    scores = jax.nn.sigmoid(_dot_nt(w_hi, h_hi) + _dot_nt(w_hi, h_lo) + _dot_nt(w_lo, h_hi))
    remaining = scores + rb_ref[...]
    eidx = lax.broadcasted_iota(I32, scores.shape, 0).astype(F32)
    chosen = jnp.zeros(scores.shape, jnp.bool_)
    picks = []
    for _ in range(TOP_K):
        mx = jnp.max(remaining, axis=0, keepdims=True)
        first = jnp.min(jnp.where(remaining == mx, eidx, float(N_EXPERTS)), axis=0, keepdims=True)
        pick = eidx == first
        picks.append((pick, first))
        chosen = jnp.logical_or(chosen, pick)
        remaining = jnp.where(pick, -jnp.inf, remaining)
    sel = jnp.where(chosen, scores, 0.0)
    gates = sel / jnp.sum(sel, axis=0, keepdims=True) * ROUTED_SCALE

    onehot = jnp.where(chosen, 1.0, 0.0)
    rank = run_ref[...] + _dot(onehot.astype(BF16), tri_ref[...])
    run_ref[...] += jnp.sum(onehot, axis=1, keepdims=True)
    cnt_ref[...] = run_ref[...]

    ids, rks, gks = [], [], []
    for pick, first in picks:
        ids.append(first.astype(I32))
        rks.append(jnp.sum(jnp.where(pick, rank, 0.0), axis=0, keepdims=True).astype(I32))
        gks.append(jnp.sum(jnp.where(pick, gates, 0.0), axis=0, keepdims=True))
    meta_ref[...] = jnp.concatenate(ids + rks, axis=0)
    gate_ref[...] = jnp.concatenate(gks, axis=0)


def _outproj(attn, four, rec, x, mod_l, w_out, layer, g1, b1, rw, rb, lay, tm):
    n_tiles = lay.n // tm
    n_ctx_tiles = lay.n_ctx // tm
    row = lambda i: (i, 0)
    const = lambda i: (0, 0)
    xs = x if isinstance(x, tuple) else (x,)
    return pl.pallas_call(
        functools.partial(_outproj_kernel, n_ctx_tiles=n_ctx_tiles, n_mix=len(attn)),
        grid=(n_tiles,),
        in_specs=[
            *_group_specs(len(attn), tm, ATTN_WIDTH, n_ctx_tiles),
            *_group_specs(len(four), tm, FOURIER_WIDTH, n_ctx_tiles),
            *_group_specs(len(rec), tm, HGRN_WIDTH, n_ctx_tiles),
            *_group_specs(len(xs), tm, D_MODEL, n_ctx_tiles),
            pl.BlockSpec((1, N_MOD, D_MODEL), lambda i: (lay.cond_row(i, tm), 0, 0)),
            pl.BlockSpec((1, D_MODEL, D_MODEL), lambda i: (layer, 0, 0)),
            pl.BlockSpec((1, D_MODEL), const),
            pl.BlockSpec((1, D_MODEL), const),
            pl.BlockSpec((N_EXPERTS, D_MODEL), const),
            pl.BlockSpec((N_EXPERTS, 1), const),
        ],
        out_specs=[
            pl.BlockSpec((tm, D_MODEL), row),
            pl.BlockSpec((tm, HALF_D), row),
            pl.BlockSpec((2 * TOP_K, tm), lambda i: (0, i)),
            pl.BlockSpec((TOP_K, tm), lambda i: (0, i)),
            pl.BlockSpec((N_EXPERTS, 1), const),
        ],
        out_shape=[
            jax.ShapeDtypeStruct((lay.n, D_MODEL), F32),
            jax.ShapeDtypeStruct((lay.n, HALF_D), I32),
            jax.ShapeDtypeStruct((2 * TOP_K, lay.n), I32),
            jax.ShapeDtypeStruct((TOP_K, lay.n), F32),
            jax.ShapeDtypeStruct((N_EXPERTS, 1), F32),
        ],
        scratch_shapes=[
            pltpu.VMEM((D_MODEL, D_MODEL), BF16),
            pltpu.VMEM((tm, tm), BF16),
            pltpu.VMEM((N_EXPERTS, 1), F32),
        ],
        compiler_params=_cparams(("arbitrary",)),
        name="outproj_router" + lay.tag,
    )(*attn, *four, *rec, *xs, mod_l, w_out, g1, b1, rw, rb)


def _sc_workers():
    info = plsc.get_sparse_core_info()
    return info.num_cores, info.num_cores * info.num_subcores


def _sc_scatter_rows(rows, pos_b, r_out, tag=""):
    nc, nw = _sc_workers()
    n, w = rows.shape
    nbt, copies, _ = pos_b.shape
    assert nbt * SC_BATCH == n and nbt % (2 * nw) == 0
    per_w = nbt // nw
    mesh = plsc.VectorSubcoreMesh(core_axis_name="c", subcore_axis_name="s")

    @functools.partial(
        pl.kernel, mesh=mesh, out_type=jax.ShapeDtypeStruct((r_out, w), rows.dtype),
        scratch_types=[pltpu.VMEM((copies, SC_BATCH), I32), pltpu.VMEM((copies, SC_BATCH), I32),
                       pltpu.VMEM((SC_BATCH, w), rows.dtype), pltpu.VMEM((SC_BATCH, w), rows.dtype),
                       pltpu.SemaphoreType.DMA, pltpu.SemaphoreType.DMA,
                       pltpu.SemaphoreType.DMA, pltpu.SemaphoreType.DMA],
        name="sc_dispatch" + tag)
    def k(rows_hbm, pos_hbm, out_hbm, idx_a, idx_b, rows_a, rows_b, sem_ra, sem_rb, sem_sa, sem_sb):
        wid = lax.axis_index("s") * nc + lax.axis_index("c")
        first = wid * per_w

        def reads(j, idx_v, rows_v, sem):
            bt = first + j
            return (pltpu.make_async_copy(pos_hbm.at[bt], idx_v, sem),
                    pltpu.make_async_copy(rows_hbm.at[pl.ds(bt * SC_BATCH, SC_BATCH)], rows_v, sem))

        def scatters(idx_v, rows_v, sem):
            return [pltpu.make_async_copy(rows_v, out_hbm.at[idx_v.at[q]], sem) for q in range(copies)]

        def start(descs):
            for d in descs:
                d.start()

        def wait(descs):
            for d in descs:
                d.wait()

        start(reads(0, idx_a, rows_a, sem_ra))

        @pl.loop(0, per_w // 2)
        def _(p):
            j0 = 2 * p
            j1 = j0 + 1

            @pl.when(p > 0)
            def _():
                wait(scatters(idx_b, rows_b, sem_sb))

            start(reads(j1, idx_b, rows_b, sem_rb))
            wait(reads(j0, idx_a, rows_a, sem_ra))
            start(scatters(idx_a, rows_a, sem_sa))
            wait(reads(j1, idx_b, rows_b, sem_rb))
            start(scatters(idx_b, rows_b, sem_sb))
            wait(scatters(idx_a, rows_a, sem_sa))

            @pl.when(p + 1 < per_w // 2)
            def _():
                start(reads(j0 + 2, idx_a, rows_a, sem_ra))

        wait(scatters(idx_b, rows_b, sem_sb))

    return k(rows, pos_b)


def _sc_gather_rows(table, idx, tag=""):
    nc, nw = _sc_workers()
    r = idx.shape[0]
    w = table.shape[1]
    assert r % (2 * nw * SC_BATCH) == 0
    per_w = r // nw
    nb = per_w // SC_BATCH
    mesh = plsc.VectorSubcoreMesh(core_axis_name="c", subcore_axis_name="s")

    @functools.partial(
        pl.kernel, mesh=mesh, out_type=jax.ShapeDtypeStruct((r, w), table.dtype),
        scratch_types=[pltpu.VMEM((per_w,), I32),
                       pltpu.VMEM((SC_BATCH, w), table.dtype), pltpu.VMEM((SC_BATCH, w), table.dtype),
                       pltpu.SemaphoreType.DMA, pltpu.SemaphoreType.DMA,
                       pltpu.SemaphoreType.DMA, pltpu.SemaphoreType.DMA],
        name="sc_combine" + tag)
    def k(table_hbm, idx_hbm, out_hbm, idx_v, rows_a, rows_b, sem_ga, sem_gb, sem_wa, sem_wb):
        wid = lax.axis_index("s") * nc + lax.axis_index("c")
        base = wid * per_w
        pltpu.sync_copy(idx_hbm.at[pl.ds(base, per_w)], idx_v)

        def gather(j, rows_v, sem):
            return pltpu.make_async_copy(table_hbm.at[idx_v.at[pl.ds(j * SC_BATCH, SC_BATCH)]], rows_v, sem)

        def write(j, rows_v, sem):
            return pltpu.make_async_copy(rows_v, out_hbm.at[pl.ds(base + j * SC_BATCH, SC_BATCH)], sem)

        gather(0, rows_a, sem_ga).start()

        @pl.loop(0, nb // 2)
        def _(p):
            j0 = 2 * p
            j1 = j0 + 1

            @pl.when(p > 0)
            def _():
                write(j1 - 2, rows_b, sem_wb).wait()

            gather(j1, rows_b, sem_gb).start()
            gather(j0, rows_a, sem_ga).wait()
            write(j0, rows_a, sem_wa).start()
            gather(j1, rows_b, sem_gb).wait()
            write(j1, rows_b, sem_wb).start()
            write(j0, rows_a, sem_wa).wait()

            @pl.when(p + 1 < nb // 2)
            def _():
                gather(j0 + 2, rows_a, sem_ga).start()

        write(nb - 1, rows_b, sem_wb).wait()

    return k(table, idx)


def _experts_kernel(te_ref, tr_ref, na_ref, x_ref, w1_ref, w3_ref, w2_ref, o_ref, w1b_ref, w3b_ref, w2b_ref):
    j = pl.program_id(0)
    na = na_ref[0]
    jj = jnp.minimum(j, na - 1)
    e = te_ref[jj]
    prev = te_ref[jnp.maximum(jj - 1, 0)]
    active = j < na

    @pl.when(jnp.logical_and(active, jnp.logical_or(j == 0, e != prev)))
    def _():
        w1b_ref[...] = w1_ref[0, 0].astype(BF16)
        w3b_ref[...] = w3_ref[0, 0].astype(BF16)
        w2b_ref[...] = w2_ref[0, 0].astype(BF16)

    def ffn(rows):
        lo, hi = _unpack_bf16_pair(x_ref[0:rows, :])
        lo = lo.astype(BF16)
        hi = hi.astype(BF16)
        a = _dot(lo, w1b_ref[0:HALF_D, :]) + _dot(hi, w1b_ref[HALF_D:, :])
        b = _dot(lo, w3b_ref[0:HALF_D, :]) + _dot(hi, w3b_ref[HALF_D:, :])
        y = _dot((_silu(a) * b).astype(BF16), w2b_ref[...])
        o_ref[0:rows, :] = _pack_bf16_pair(y[:, :HALF_D], y[:, HALF_D:])

    tm = x_ref.shape[0]
    half_only = tr_ref[jj] <= tm // 2

    @pl.when(jnp.logical_and(active, half_only))
    def _():
        ffn(tm // 2)

    @pl.when(jnp.logical_and(active, jnp.logical_not(half_only)))
    def _():
        ffn(tm)


def _experts(xs, tile_expert, tile_rows, n_active, w1, w3, w2, layer, tm, tag):
    r = xs.shape[0]
    n_tiles = r // tm

    def xmap(j, te, tr, na):
        return (jnp.minimum(j, na[0] - 1), 0)

    def wmap(j, te, tr, na):
        return (layer, te[jnp.minimum(j, na[0] - 1)], 0, 0)

    grid_spec = pltpu.PrefetchScalarGridSpec(
        num_scalar_prefetch=3,
        grid=(n_tiles,),
        in_specs=[
            pl.BlockSpec((tm, HALF_D), xmap),
            pl.BlockSpec((1, 1, D_MODEL, EXPERT_FF), wmap),
            pl.BlockSpec((1, 1, D_MODEL, EXPERT_FF), wmap),
            pl.BlockSpec((1, 1, EXPERT_FF, D_MODEL), wmap),
        ],
        out_specs=pl.BlockSpec((tm, HALF_D), xmap),
        scratch_shapes=[
            pltpu.VMEM((D_MODEL, EXPERT_FF), BF16),
            pltpu.VMEM((D_MODEL, EXPERT_FF), BF16),
            pltpu.VMEM((EXPERT_FF, D_MODEL), BF16),
        ],
    )
    return pl.pallas_call(
        _experts_kernel,
        grid_spec=grid_spec,
        out_shape=jax.ShapeDtypeStruct((r, HALF_D), I32),
        compiler_params=_cparams(("arbitrary",)),
        name="experts" + tag,
    )(tile_expert, tile_rows, n_active, xs, w1, w3, w2)


def _combine_kernel(yp_ref, gate_ref, hp_ref, sw1_ref, sw3_ref, sw2_ref, x_ref, mod_ref, g_ref, b_ref, *refs,
                    n_ctx_tiles):
    outs = refs[:-3]
    w1b_ref, w3b_ref, w2b_ref = refs[-3:]

    @pl.when(pl.program_id(0) == 0)
    def _():
        w1b_ref[...] = sw1_ref[...].astype(BF16)
        w3b_ref[...] = sw3_ref[...].astype(BF16)
        w2b_ref[...] = sw2_ref[...].astype(BF16)

    lo, hi = _unpack_bf16_pair(hp_ref[...])
    lo = lo.astype(BF16)
    hi = hi.astype(BF16)
    a = _dot(lo, w1b_ref[0:HALF_D, :]) + _dot(hi, w1b_ref[HALF_D:, :])
    b = _dot(lo, w3b_ref[0:HALF_D, :]) + _dot(hi, w3b_ref[HALF_D:, :])
    shared = _dot((_silu(a) * b).astype(BF16), w2b_ref[...])
    acc_lo = shared[:, :HALF_D]
    acc_hi = shared[:, HALF_D:]
    gates = gate_ref[...]
    for k in range(TOP_K):
        ylo, yhi = _unpack_bf16_pair(yp_ref[k])
        gk = gates[:, k:k + 1]
        acc_lo = acc_lo + gk * ylo
        acc_hi = acc_hi + gk * yhi
    moe = jnp.concatenate([acc_lo, acc_hi], axis=1)
    y = DEEPNORM_ALPHA * x_ref[...] + mod_ref[0, 5:6, :] * moe
    res = _ln_plain(y, LN_EPS) * g_ref[...] + b_ref[...]
    if len(outs) == 1:
        outs[0][...] = res
    else:
        @pl.when(pl.program_id(0) < n_ctx_tiles)
        def _():
            outs[0][...] = res

        @pl.when(pl.program_id(0) >= n_ctx_tiles)
        def _():
            outs[1][...] = res


def _combine(yp, gate8, hp, sw1, sw3, sw2, x1, mod_l, g2, b2, lay, tm, split_out):
    n_tiles = lay.n // tm
    n_ctx_tiles = lay.n_ctx // tm
    row = lambda i: (i, 0)
    const = lambda i: (0, 0)
    if split_out:
        out_specs = _group_specs(2, tm, D_MODEL, n_ctx_tiles)
        out_shape = [jax.ShapeDtypeStruct((lay.n_ctx, D_MODEL), F32), jax.ShapeDtypeStruct((lay.n_lat, D_MODEL), F32)]
    else:
        out_specs = pl.BlockSpec((tm, D_MODEL), row)
        out_shape = jax.ShapeDtypeStruct((lay.n, D_MODEL), F32)
    return pl.pallas_call(
        functools.partial(_combine_kernel, n_ctx_tiles=n_ctx_tiles),
        grid=(n_tiles,),
        in_specs=[
            pl.BlockSpec((TOP_K, tm, HALF_D), lambda i: (0, i, 0)),
            pl.BlockSpec((tm, TOP_K), row),
            pl.BlockSpec((tm, HALF_D), row),
            pl.BlockSpec((D_MODEL, EXPERT_FF), const),
            pl.BlockSpec((D_MODEL, EXPERT_FF), const),
            pl.BlockSpec((EXPERT_FF, D_MODEL), const),
            pl.BlockSpec((tm, D_MODEL), row),
            pl.BlockSpec((1, N_MOD, D_MODEL), lambda i: (lay.cond_row(i, tm), 0, 0)),
            pl.BlockSpec((1, D_MODEL), const),
            pl.BlockSpec((1, D_MODEL), const),
        ],
        out_specs=out_specs,
        out_shape=out_shape,
        scratch_shapes=[
            pltpu.VMEM((D_MODEL, EXPERT_FF), BF16),
            pltpu.VMEM((D_MODEL, EXPERT_FF), BF16),
            pltpu.VMEM((EXPERT_FF, D_MODEL), BF16),
        ],
        compiler_params=_cparams(("arbitrary",)),
        name="combine_norm" + lay.tag,
    )(yp, gate8, hp, sw1, sw3, sw2, x1, mod_l, g2, b2)


def _moe_dispatch(hp, meta, counts, lay, tile):
    n = lay.n
    r_max = n * TOP_K + N_EXPERTS * tile
    n_tiles = r_max // tile
    cnt = counts.reshape(N_EXPERTS).astype(I32)
    padded = ((cnt + tile - 1) // tile) * tile
    ends = jnp.cumsum(padded)
    offsets = ends - padded
    idx8 = meta[:TOP_K]
    base8 = jnp.sum(jnp.where(idx8[:, :, None] == jnp.arange(N_EXPERTS, dtype=I32), offsets, 0), axis=-1)
    pos = (base8 + meta[TOP_K:]).astype(I32)
    tile_start = jnp.arange(n_tiles, dtype=I32) * tile
    tile_expert = jnp.minimum(jnp.sum(tile_start[:, None] >= ends[None, :], axis=1), N_EXPERTS - 1).astype(I32)
    n_active = (ends[-1] // tile).astype(I32).reshape(1)
    in_expert = tile_start - jnp.take(offsets, tile_expert)
    tile_rows = jnp.clip(jnp.take(cnt, tile_expert) - in_expert, 0, tile).astype(I32)
    pos_b = pos.reshape(TOP_K, n // SC_BATCH, SC_BATCH).transpose(1, 0, 2)
    xs = _sc_scatter_rows(hp, pos_b, r_max, lay.tag)
    return xs, tile_expert, tile_rows, n_active, pos


def _moe_combine(ys, pos, gate8, hp, sw1, sw3, sw2, x1, mod_l, g2, b2, lay, split_out):
    n = lay.n
    yp = _sc_gather_rows(ys, pos.reshape(n * TOP_K), lay.tag).reshape(TOP_K, n, HALF_D)
    return _combine(yp, gate8.T, hp, sw1, sw3, sw2, x1, mod_l, g2, b2, lay, TOKEN_TILE, split_out)


def kernel(x_prompt, x_sample, cache_k, cache_v, state_hgrn, c, c_ctx, w_ada, b_ada, w_in, w_out, attn_sink, hgrn_lb, hgrn_norm, ln1_g, ln1_b, ln2_g, ln2_b, router_w, router_b, moe_w1, moe_w3, moe_w2, shared_w1, shared_w3, shared_w2):
    b_ctx, t_ctx, _ = x_prompt.shape
    b_lat, t_lat, _ = x_sample.shape
    past = cache_k.shape[2]
    tm = TOKEN_TILE
    assert 1 + b_lat <= COND_ROWS
    lay = _Layout(b_ctx, t_ctx, b_lat, t_lat)
    assert lay.n_ctx % tm == 0 and t_lat % tm == 0 and lay.n_ctx % t_lat == 0

    cond = jnp.concatenate([c_ctx[None, :], c, jnp.zeros((COND_ROWS - 1 - b_lat, D_MODEL), F32)], axis=0)
    mod = _adaln(cond, w_ada, b_ada).reshape(DEPTH, COND_ROWS, N_MOD, D_MODEL)

    lb_all = jnp.cumsum(jax.nn.softmax(hgrn_lb.astype(F32), axis=0), axis=0)
    lb_all = lb_all - lb_all[:1]
    lbp = jnp.stack([jnp.log(lb_all), jnp.log1p(-lb_all), 1.0 - lb_all], axis=2)

    cos_t, sin_t = _rope_tables(lay, tm)
    zero_state = jnp.zeros((b_ctx, 2, HGRN_HEADS, HGRN_DK, HGRN_DK), F32)

    def layer(l, lay, x, split_out):
        lat = slice(lay.lat_first, lay.lat_first + lay.b_lat)
        outs = _inproj(x, mod[l], w_in, l, cos_t, sin_t, lay, tm)
        proj = outs[0]
        sink_l = attn_sink[l].reshape(1, N_HEADS)
        gn_row = jnp.tile(hgrn_norm[l], HGRN_HEADS).reshape(1, HGRN_WIDTH)
        attn, four, rec, extras = [], [], [], None
        if lay.b_ctx:
            attn.append(_attn_context(proj, sink_l, lay))
            four.append(_fourier(proj, 0, lay.b_ctx, t_ctx, t_ctx, "fourier_ctx"))
            rec_c, s_fin = _hgrn(proj, 0, lay.b_ctx, t_ctx, lbp[l], gn_row, zero_state, "hgrn_ctx")
            rec.append(rec_c)
            extras = (outs[1], outs[2], s_fin)
        attn.append(_attn_latent(proj, cache_k[lat, l].reshape(lay.b_lat, past, KV_WIDTH),
                                 cache_v[lat, l].reshape(lay.b_lat, past, KV_WIDTH), sink_l, lay))
        four.append(_fourier(proj, lay.n_ctx, lay.b_lat, t_lat, min(t_lat, 512), "fourier_lat" + lay.tag))
        s0t = jnp.swapaxes(state_hgrn[lat, l].astype(F32), -1, -2)
        rec.append(_hgrn(proj, lay.n_ctx, lay.b_lat, t_lat, lbp[l], gn_row, s0t, "hgrn_lat" + lay.tag)[0])
        x1, hp, meta, gate8, counts = _outproj(
            tuple(attn), tuple(four), tuple(rec), x, mod[l], w_out, l, ln1_g[l].reshape(1, -1),
            ln1_b[l].reshape(1, -1), router_w[l].T, router_b[l].reshape(-1, 1), lay, tm)
        xs, tile_expert, tile_rows, n_active, pos = _moe_dispatch(hp, meta, counts, lay, EXPERT_TILE)
        ys = _experts(xs, tile_expert, tile_rows, n_active, moe_w1, moe_w3, moe_w2, l, EXPERT_TILE, lay.tag)
        x = _moe_combine(ys, pos, gate8, hp, shared_w1[l], shared_w3[l], shared_w2[l], x1, mod[l],
                         ln2_g[l].reshape(1, -1), ln2_b[l].reshape(1, -1), lay, split_out)
        return x, extras

    x = (x_prompt.reshape(lay.n_ctx, D_MODEL), x_sample.reshape(lay.n_lat, D_MODEL))
    ks_out, vs_out, ss_out = [], [], []
    for l in range(DEPTH):
        x, (k_new, v_new, s_fin) = layer(l, lay, x, split_out=(l == DEPTH - 1))
        ks_out.append(k_new.reshape(b_ctx, t_ctx, N_KV_HEADS, HEAD_DIM))
        vs_out.append(v_new.reshape(b_ctx, t_ctx, N_KV_HEADS, HEAD_DIM))
        ss_out.append(jnp.swapaxes(s_fin, -1, -2))

    y_prompt = x[0].reshape(b_ctx, t_ctx, D_MODEL)
    y_sample = x[1].reshape(b_lat, t_lat, D_MODEL)
    new_cache_k = jnp.stack(ks_out, axis=1)
    new_cache_v = jnp.stack(vs_out, axis=1)
    new_state = jnp.stack(ss_out, axis=1).astype(x_prompt.dtype)
    return (y_prompt, y_sample, new_cache_k, new_cache_v, new_state)
```

```python
import functools
import math

import numpy as np
import jax
import jax.numpy as jnp
from jax import lax
from jax.experimental import pallas as pl
from jax.experimental.pallas import tpu as pltpu
from jax.experimental.pallas import tpu_sc as plsc

F32 = jnp.float32
BF16 = jnp.bfloat16
I32 = jnp.int32

D_MODEL = 1024
HALF_D = D_MODEL // 2
DEPTH = 2
GRID_W = 64
ROPE_BASE = 10000.0
HEAD_DIM = 64
ATTN_WIDTH = 512
N_HEADS = 8
N_KV_HEADS = 2
KV_GROUP = 4
KV_WIDTH = N_KV_HEADS * HEAD_DIM
WINDOW = 128
ATTN_BLOCK = 128
FOURIER_WIDTH = 256
FOURIER_GROUPS = 4
HGRN_WIDTH = 256
HGRN_HEADS = 4
HGRN_DK = 64
HGRN_CHUNK = 64
IN_WIDTH = 2304
N_EXPERTS = 64
TOP_K = 8
EXPERT_FF = 256
ROUTED_SCALE = 2.5
N_MOD = 6
LN_EPS = 1e-5
ADA_EPS = 1e-6
GN_EPS = 1e-6
DEEPNORM_ALPHA = (2 * DEPTH) ** 0.25

COL_Q = 0
COL_K = 512
COL_V = 640
COL_U = 768
COL_HQ = 1024
COL_FF = 1280
COL_FB = 1536
COL_HI = 1792
COL_HG = 2048
ROPE_COLS = COL_V

V7X_LANES = 128
COND_ROWS = 16
NEG_BIG = -1e30
TOKEN_TILE = 512
EXPERT_TILE = 1024
SC_BATCH = 64

VMEM_LIMIT = 56 * 1024 * 1024


def _cparams(sem):
    return pltpu.CompilerParams(dimension_semantics=sem, vmem_limit_bytes=VMEM_LIMIT)


def _dot(a, b):
    return jnp.dot(a, b, preferred_element_type=F32)


def _dot_nt(a, b):
    return lax.dot_general(a, b, (((1,), (1,)), ((), ())), preferred_element_type=F32)


def _dot_tn(a, b):
    return lax.dot_general(a, b, (((0,), (0,)), ((), ())), preferred_element_type=F32)


def _split3(x):
    hi = x.astype(BF16)
    r1 = x - hi.astype(F32)
    mid = r1.astype(BF16)
    lo = (r1 - mid.astype(F32)).astype(BF16)
    return hi, mid, lo


def _dot_exact_lhs(m_bf16, x):
    hi, mid, lo = _split3(x)
    return _dot(m_bf16, hi) + _dot(m_bf16, mid) + _dot(m_bf16, lo)


def _dot_exact_rhs(x, m_bf16):
    hi, mid, lo = _split3(x)
    return _dot(hi, m_bf16) + _dot(mid, m_bf16) + _dot(lo, m_bf16)


def _dot_hp(a, b):
    a_hi = a.astype(BF16)
    a_lo = (a - a_hi.astype(F32)).astype(BF16)
    b_hi = b.astype(BF16)
    b_lo = (b - b_hi.astype(F32)).astype(BF16)
    return _dot(a_hi, b_hi) + _dot(a_hi, b_lo) + _dot(a_lo, b_hi)


def _pack_bf16_pair(lo, hi):
    return lax.bitcast_convert_type(pltpu.pack_elementwise([lo, hi], packed_dtype=BF16), I32)


def _unpack_bf16_pair(w):
    u = lax.bitcast_convert_type(w, jnp.uint32)
    lo = pltpu.unpack_elementwise(u, index=0, packed_dtype=BF16, unpacked_dtype=F32)
    hi = pltpu.unpack_elementwise(u, index=1, packed_dtype=BF16, unpacked_dtype=F32)
    return lo, hi


def _ln_plain(x, eps):
    mu = jnp.mean(x, axis=-1, keepdims=True)
    xc = x - mu
    var = jnp.mean(xc * xc, axis=-1, keepdims=True)
    return xc * lax.rsqrt(var + eps)


def _silu(x):
    return x * jax.nn.sigmoid(x)


def _adaln_kernel(c_ref, w_ref, b_ref, o_ref):
    s = _silu(c_ref[...])
    o_ref[0] = _dot_hp(s, w_ref[0]) + b_ref[0]


def _adaln(cond, w_ada, b_ada):
    return pl.pallas_call(
        _adaln_kernel,
        grid=(DEPTH, N_MOD),
        in_specs=[
            pl.BlockSpec((COND_ROWS, D_MODEL), lambda l, j: (0, 0)),
            pl.BlockSpec((1, D_MODEL, D_MODEL), lambda l, j: (l, 0, j)),
            pl.BlockSpec((1, 1, D_MODEL), lambda l, j: (l, 0, j)),
        ],
        out_specs=pl.BlockSpec((1, COND_ROWS, D_MODEL), lambda l, j: (l, 0, j)),
        out_shape=jax.ShapeDtypeStruct((DEPTH, COND_ROWS, N_MOD * D_MODEL), F32),
        compiler_params=_cparams(("arbitrary", "arbitrary")),
        name="adaln",
    )(cond, w_ada, b_ada.reshape(DEPTH, 1, N_MOD * D_MODEL))


class _Layout:
    def __init__(self, b_ctx, t_ctx, b_lat, t_lat, lat_first=0, tag=""):
        self.b_ctx, self.t_ctx, self.b_lat, self.t_lat = b_ctx, t_ctx, b_lat, t_lat
        self.lat_first = lat_first
        self.n_ctx = b_ctx * t_ctx
        self.n_lat = b_lat * t_lat
        self.n = self.n_ctx + self.n_lat
        self.tag = tag

    def cond_row(self, tile, tm):
        n_ctx_tiles = self.n_ctx // tm
        per_batch = self.t_lat // tm
        return jnp.where(tile < n_ctx_tiles, 0, 1 + self.lat_first + (tile - n_ctx_tiles) // per_batch)


def _group_specs(n_arrays, tm, width, n_ctx_tiles):
    if n_arrays == 1:
        return [pl.BlockSpec((tm, width), lambda i: (i, 0))]
    return [pl.BlockSpec((tm, width), lambda i: (jnp.minimum(i, n_ctx_tiles - 1), 0)),
            pl.BlockSpec((tm, width), lambda i: (jnp.maximum(i - n_ctx_tiles, 0), 0))]


def _inproj_kernel(*refs, n_ctx_tiles):
    n_tail = 8 if n_ctx_tiles > 0 else 6
    xs = refs[:-n_tail]
    mod_ref, w_ref, cos_ref, sin_ref, o_ref = refs[-n_tail:-n_tail + 5]
    wb_ref = refs[-1]

    @pl.when(pl.program_id(0) == 0)
    def _():
        wb_ref[...] = w_ref[0].astype(BF16)

    if len(xs) == 2:
        x = jnp.where(pl.program_id(0) < n_ctx_tiles, xs[0][...], xs[1][...])
    else:
        x = xs[0][...]
    shift = mod_ref[0, 0:1, :]
    scale = mod_ref[0, 1:2, :]
    h = (_ln_plain(x, ADA_EPS) * (1.0 + scale) + shift).astype(BF16)
    p = _dot(h, wb_ref[...])
    cos = cos_ref[...]
    sin = sin_ref[...]
    lane = lax.broadcasted_iota(I32, cos.shape, 1)
    first_half = (lane & 31) < 16
    for cb in range(ROPE_COLS // V7X_LANES):
        seg = p[:, cb * V7X_LANES:(cb + 1) * V7X_LANES]
        partner = jnp.where(first_half, pltpu.roll(seg, V7X_LANES - 16, 1), pltpu.roll(seg, 16, 1))
        o_ref[:, cb * V7X_LANES:(cb + 1) * V7X_LANES] = seg * cos + partner * sin
    o_ref[:, ROPE_COLS:] = p[:, ROPE_COLS:]

    if n_ctx_tiles > 0:
        kc_ref, vc_ref = refs[-3], refs[-2]

        @pl.when(pl.program_id(0) < n_ctx_tiles)
        def _():
            kc_ref[...] = p[:, COL_K:COL_K + KV_WIDTH]
            vc_ref[...] = p[:, COL_V:COL_V + KV_WIDTH]


def _rope_tables(lay, tm):
    t = lay.t_lat
    pos = jnp.arange(t)
    row = (pos // GRID_W).astype(F32)
    col = (pos % GRID_W).astype(F32)
    n_freq = HEAD_DIM // 4
    inv = ROPE_BASE ** (-jnp.arange(n_freq, dtype=F32) / n_freq)
    ang_r = row[:, None] * inv
    ang_c = col[:, None] * inv
    ang = jnp.concatenate([ang_r, ang_r, ang_c, ang_c], axis=1)
    sign = jnp.concatenate([-jnp.ones(n_freq), jnp.ones(n_freq), -jnp.ones(n_freq), jnp.ones(n_freq)]).astype(F32)
    cos = jnp.cos(ang)
    sin = jnp.sin(ang) * sign
    cos = jnp.concatenate([jnp.ones((tm, HEAD_DIM), F32), cos], axis=0)
    sin = jnp.concatenate([jnp.zeros((tm, HEAD_DIM), F32), sin], axis=0)
    return jnp.tile(cos, (1, 2)), jnp.tile(sin, (1, 2))


def _inproj(x, mod_l, w_in, layer, cos_t, sin_t, lay, tm):
    n_tiles = lay.n // tm
    n_ctx_tiles = lay.n_ctx // tm
    per_batch = lay.t_lat // tm

    def tbl(i):
        return jnp.where(i < n_ctx_tiles, 0, 1 + (i - n_ctx_tiles) % per_batch)

    xs = x if isinstance(x, tuple) else (x,)
    kv_specs, kv_shapes = [], []
    if n_ctx_tiles > 0:
        kv_specs = [pl.BlockSpec((tm, KV_WIDTH), lambda i: (jnp.minimum(i, n_ctx_tiles - 1), 0))] * 2
        kv_shapes = [jax.ShapeDtypeStruct((lay.n_ctx, KV_WIDTH), F32)] * 2
    return pl.pallas_call(
        functools.partial(_inproj_kernel, n_ctx_tiles=n_ctx_tiles),
        grid=(n_tiles,),
        in_specs=[
            *_group_specs(len(xs), tm, D_MODEL, n_ctx_tiles),
            pl.BlockSpec((1, N_MOD, D_MODEL), lambda i: (lay.cond_row(i, tm), 0, 0)),
            pl.BlockSpec((1, D_MODEL, IN_WIDTH), lambda i: (layer, 0, 0), pipeline_mode=pl.Buffered(1)),
            pl.BlockSpec((tm, V7X_LANES), lambda i: (tbl(i), 0)),
            pl.BlockSpec((tm, V7X_LANES), lambda i: (tbl(i), 0)),
        ],
        out_specs=[pl.BlockSpec((tm, IN_WIDTH), lambda i: (i, 0))] + kv_specs,
        out_shape=[jax.ShapeDtypeStruct((lay.n, IN_WIDTH), F32)] + kv_shapes,
        scratch_shapes=[pltpu.VMEM((D_MODEL, IN_WIDTH), BF16)],
        compiler_params=_cparams(("arbitrary",)),
        name="inproj" + lay.tag,
    )(*xs, mod_l, w_in, cos_t, sin_t)


def _attn_kernel(sink_ref, q_ref, *refs, n_local, has_ctx, t_total):
    o_ref = refs[-1]
    k_refs = refs[:n_local]
    v_refs = refs[n_local:2 * n_local]
    tq = q_ref.shape[0]
    scale = HEAD_DIM ** -0.5
    k_parts = [kr[...] for kr in k_refs]
    v_parts = [vr[...] for vr in v_refs]
    if has_ctx:
        k_parts.append(refs[2 * n_local][0])
        v_parts.append(refs[2 * n_local + 1][0])
    kall = jnp.concatenate(k_parts, axis=0) if len(k_parts) > 1 else k_parts[0]
    vall = jnp.concatenate(v_parts, axis=0) if len(v_parts) > 1 else v_parts[0]
    nk = kall.shape[0]
    k_sw = pltpu.roll(kall, HEAD_DIM, 1)
    v_sw = pltpu.roll(vall, HEAD_DIM, 1)
    lo_half = lax.broadcasted_iota(I32, (1, V7X_LANES), 1) < HEAD_DIM
    er = jnp.where(lax.broadcasted_iota(I32, (2 * nk, V7X_LANES), 0) < nk, 0, 1)
    el = jnp.where(lax.broadcasted_iota(I32, (2 * nk, V7X_LANES), 1) < HEAD_DIM, 0, 1)
    ones_blk = jnp.where(er == el, 1.0, 0.0).astype(BF16)
    if n_local > 1:
        i = pl.program_id(1)
        band = refs[-2][...]
        first_blk = jnp.where(i == 0, NEG_BIG, 0.0)
        last_blk = jnp.where(i == t_total // tq - 1, NEG_BIG, 0.0)

        def mask_local(sc):
            loc = sc[:, :n_local * tq] + band
            parts = [loc[:, :tq] + first_blk, loc[:, tq:(n_local - 1) * tq], loc[:, (n_local - 1) * tq:] + last_blk]
            return jnp.concatenate(parts + [sc[:, n_local * tq:]], axis=1)
    else:
        mask_local = None
    for g in range(N_KV_HEADS):
        k_own, k_oth = (kall, k_sw) if g == 0 else (k_sw, kall)
        v_own, v_oth = (vall, v_sw) if g == 0 else (v_sw, vall)
        k2 = jnp.concatenate([jnp.where(lo_half, k_own, 0.0), jnp.where(lo_half, 0.0, k_oth)], axis=0).astype(BF16)
        v2 = jnp.concatenate([jnp.where(lo_half, v_own, 0.0), jnp.where(lo_half, 0.0, v_oth)], axis=0).astype(BF16)
        v2e = jnp.concatenate([v2, ones_blk], axis=1)
        pairs = [2 * g, 2 * g + 1]
        qq = jnp.concatenate([q_ref[:, p * V7X_LANES:(p + 1) * V7X_LANES] for p in pairs], axis=0)
        qq = (qq * scale).astype(BF16)
        sink_a = jnp.concatenate([jnp.full((tq, 1), sink_ref[0, 2 * p], F32) for p in pairs], axis=0)
        sink_b = jnp.concatenate([jnp.full((tq, 1), sink_ref[0, 2 * p + 1], F32) for p in pairs], axis=0)
        s = _dot_nt(qq, k2)
        s_a = s[:, :nk]
        s_b = s[:, nk:]
        if mask_local is not None:
            s_a = mask_local(s_a)
            s_b = mask_local(s_b)
        m_a = jnp.maximum(jnp.max(s_a, axis=1, keepdims=True), sink_a)
        m_b = jnp.maximum(jnp.max(s_b, axis=1, keepdims=True), sink_b)
        pe = jnp.concatenate([jnp.exp(s_a - m_a).astype(BF16), jnp.exp(s_b - m_b).astype(BF16)], axis=1)
        acc = _dot(pe, v2e)
        sink_term = jnp.where(lo_half, jnp.exp(sink_a - m_a), jnp.exp(sink_b - m_b))
        o = acc[:, :V7X_LANES] / (acc[:, V7X_LANES:] + sink_term)
        for j, p in enumerate(pairs):
            o_ref[:, p * V7X_LANES:(p + 1) * V7X_LANES] = o[j * tq:(j + 1) * tq].astype(o_ref.dtype)


def _attn_context(proj, sink_l, lay):
    t = lay.t_ctx
    kb, vb = COL_K // KV_WIDTH, COL_V // KV_WIDTH
    body = functools.partial(_attn_kernel, n_local=1, has_ctx=False, t_total=t)
    return pl.pallas_call(
        body,
        grid=(lay.b_ctx,),
        in_specs=[
            pl.BlockSpec(memory_space=pltpu.SMEM),
            pl.BlockSpec((t, ATTN_WIDTH), lambda b: (b, 0)),
            pl.BlockSpec((t, KV_WIDTH), lambda b: (b, kb)),
            pl.BlockSpec((t, KV_WIDTH), lambda b: (b, vb)),
        ],
        out_specs=pl.BlockSpec((t, ATTN_WIDTH), lambda b: (b, 0)),
        out_shape=jax.ShapeDtypeStruct((lay.n_ctx, ATTN_WIDTH), BF16),
        compiler_params=_cparams(("arbitrary",)),
        name="attn_ctx",
    )(sink_l, proj, proj, proj)


def _attn_latent(proj, k_ctx, v_ctx, sink_l, lay):
    t = lay.t_lat
    tq = ATTN_BLOCK
    nq = t // tq
    base = lay.n_ctx // tq
    kb, vb = COL_K // KV_WIDTH, COL_V // KV_WIDTH
    past = k_ctx.shape[1]

    def rows(off):
        return lambda b, i: base + b * nq + jnp.clip(i + off, 0, nq - 1)

    def kv_specs(col):
        return [pl.BlockSpec((tq, KV_WIDTH), (lambda b, i, f=rows(off): (f(b, i), col))) for off in (-1, 0, 1)]

    body = functools.partial(_attn_kernel, n_local=3, has_ctx=True, t_total=t)
    rel = np.arange(3 * tq)[None, :] - tq - (np.arange(2 * tq)[:, None] % tq)
    band = jnp.asarray(np.where(np.abs(rel) <= WINDOW, 0.0, NEG_BIG).astype(np.float32))
    return pl.pallas_call(
        body,
        grid=(lay.b_lat, nq),
        in_specs=[
            pl.BlockSpec(memory_space=pltpu.SMEM),
            pl.BlockSpec((tq, ATTN_WIDTH), lambda b, i: (base + b * nq + i, 0)),
            *kv_specs(kb),
            *kv_specs(vb),
            pl.BlockSpec((1, past, KV_WIDTH), lambda b, i: (b, 0, 0)),
            pl.BlockSpec((1, past, KV_WIDTH), lambda b, i: (b, 0, 0)),
            pl.BlockSpec(band.shape, lambda b, i: (0, 0)),
        ],
        out_specs=pl.BlockSpec((tq, ATTN_WIDTH), lambda b, i: (b * nq + i, 0)),
        out_shape=jax.ShapeDtypeStruct((lay.n_lat, ATTN_WIDTH), BF16),
        compiler_params=_cparams(("arbitrary", "arbitrary")),
        name="attn_lat" + lay.tag,
    )(sink_l, proj, proj, proj, proj, proj, proj, proj, k_ctx, v_ctx, band)


def _fourier_kernel(cs_ref, u_ref, cc_ref, sc_ref, o_ref, csb_ref, *, scale):
    @pl.when(pl.program_id(1) == 0)
    def _():
        csb_ref[...] = cs_ref[...].astype(BF16)

    z = u_ref[...].astype(BF16)
    zc = _dot(z, cc_ref[...].astype(BF16)).astype(BF16)
    zs = _dot(z, sc_ref[...].astype(BF16)).astype(BF16)
    zz = jnp.concatenate([zc, zs], axis=0)
    o_ref[...] = (_dot(csb_ref[...], zz) * scale).astype(o_ref.dtype)


@functools.lru_cache(maxsize=None)
def _dft_tables(t):
    idx = np.arange(t, dtype=np.int64)
    ang = 2.0 * np.pi * ((idx[:, None] * idx[None, :]) % t).astype(np.float64) / t
    cs = np.concatenate([np.cos(ang), -np.sin(ang)], axis=1).astype(np.float32)
    cw = FOURIER_WIDTH // FOURIER_GROUPS
    cidx = np.arange(cw, dtype=np.int64)
    cang = 2.0 * np.pi * ((cidx[:, None] * cidx[None, :]) % cw).astype(np.float64) / cw
    eye = np.eye(FOURIER_GROUPS)
    cc = np.kron(eye, np.cos(cang)).astype(np.float32)
    sc = np.kron(eye, np.sin(cang)).astype(np.float32)
    return cs, cc, sc


def _fourier(proj, row0, b, t, tm, name):
    cs, cc, sc = _dft_tables(t)
    cw = FOURIER_WIDTH // FOURIER_GROUPS
    nt = t // tm
    ub = COL_U // FOURIER_WIDTH
    base = row0 // t
    body = functools.partial(_fourier_kernel, scale=1.0 / math.sqrt(t * cw))
    return pl.pallas_call(
        body,
        grid=(nt, b),
        in_specs=[
            pl.BlockSpec((tm, 2 * t), lambda i, bb: (i, 0)),
            pl.BlockSpec((t, FOURIER_WIDTH), lambda i, bb: (base + bb, ub)),
            pl.BlockSpec((FOURIER_WIDTH, FOURIER_WIDTH), lambda i, bb: (0, 0)),
            pl.BlockSpec((FOURIER_WIDTH, FOURIER_WIDTH), lambda i, bb: (0, 0)),
        ],
        out_specs=pl.BlockSpec((tm, FOURIER_WIDTH), lambda i, bb: (bb * nt + i, 0)),
        out_shape=jax.ShapeDtypeStruct((b * t, FOURIER_WIDTH), BF16),
        scratch_shapes=[pltpu.VMEM((tm, 2 * t), BF16)],
        compiler_params=_cparams(("arbitrary", "arbitrary")),
        name=name,
    )(jnp.asarray(cs), proj, jnp.asarray(cc), jnp.asarray(sc))


HGRN_LEVELS = (64, 32, 16, 8, 4, 2)
HGRN_SAFE_RANGE = 80.0


@functools.lru_cache(maxsize=None)
def _hgrn_tables():
    c = HGRN_CHUNK
    return np.stack([np.tril(np.ones((c, c))), np.triu(np.ones((c, c)))]).astype(np.float32)


def _boundary_rows(b, m, reverse):
    c, w = b.shape
    half = m // 2
    off = half if reverse else half - 1
    if m >= 16:
        return jnp.concatenate(
            [jnp.broadcast_to(b[s + off:s + off + 1], (m, w)) for s in range(0, c, m)], axis=0)
    sub = lax.broadcasted_iota(I32, (c, w), 0) & 7
    b3 = b.reshape(c // 8, 8, w)

    def bcast(j):
        return jnp.broadcast_to(b3[:, j:j + 1, :], (c // 8, 8, w)).reshape(c, w)

    if m == 8:
        return bcast(off)
    if m == 4:
        return jnp.where(sub < 4, bcast(off), bcast(4 + off))
    assert m == 2
    if reverse:
        return jnp.where((sub & 1) == 1, b, pltpu.roll(b, c - 1, 0))
    return jnp.where((sub & 1) == 0, b, pltpu.roll(b, 1, 0))


def _hgrn_gates(q, z, v, loglb, log1mlb, onemlb, cum, reverse):
    c = HGRN_CHUNK
    log_sig = jnp.minimum(z, 0.0) - jnp.log1p(jnp.exp(-jnp.abs(z)))
    bb = log1mlb + log_sig
    mx = jnp.maximum(loglb, bb)
    lf = mx + jnp.log1p(jnp.exp(-jnp.abs(loglb - bb)))
    kk = onemlb * jax.nn.sigmoid(-z)
    b = _dot_exact_lhs(cum, lf)
    b_end = b[0:1] if reverse else b[c - 1:c]
    qt = (q * jnp.exp(b)).astype(BF16)
    kt = (kk * jnp.exp(b_end - b)).astype(BF16)
    return (q, kk, b), (qt, kt, v.astype(BF16))


class _HgrnDir:
    def __init__(self, f32_parts, bf16_parts, reverse):
        c = HGRN_CHUNK
        self.q, self.kk, self.b = f32_parts
        self.qt, self.kt, self.vb = bf16_parts
        self.reverse = reverse
        b_end = self.b[0:1] if reverse else self.b[c - 1:c]
        self.decay = jnp.exp(b_end)
        mid = c // 2 if reverse else c // 2 - 1
        self.rel = self.b - self.b[mid:mid + 1]
        self.span = jnp.max(jnp.abs(self.rel))

    def tree_decay_matrices(self):
        c = HGRN_CHUNK
        q, kk, b = self.q, self.kk, self.b
        row = lax.broadcasted_iota(I32, (c, 1), 0)
        ti = lax.broadcasted_iota(I32, (c, c), 0)
        si = lax.broadcasted_iota(I32, (c, c), 1)
        qb = q.astype(BF16)
        kb = kk.astype(BF16)
        heads = [slice(h * HGRN_DK, (h + 1) * HGRN_DK) for h in range(HGRN_HEADS)]
        acc = [jnp.where(ti == si, _dot_nt(qb[:, sl], kb[:, sl]), 0.0) for sl in heads]
        for m in HGRN_LEVELS:
            r = _boundary_rows(b, m, self.reverse)
            upper = (row & (m - 1)) >= (m // 2)
            q_side = jnp.logical_not(upper) if self.reverse else upper
            e = jnp.exp(jnp.where(q_side, b - r, r - b))
            qf = jnp.where(q_side, q * e, 0.0).astype(BF16)
            kf = jnp.where(q_side, 0.0, kk * e).astype(BF16)
            same_block = (ti & -m) == (si & -m)
            for h, sl in enumerate(heads):
                acc[h] = acc[h] + jnp.where(same_block, _dot_nt(qf[:, sl], kf[:, sl]), 0.0)
        return acc

    def midpoint_decay_matrices(self):
        c = HGRN_CHUNK
        ti = lax.broadcasted_iota(I32, (c, c), 0)
        si = lax.broadcasted_iota(I32, (c, c), 1)
        qm = (self.q * jnp.exp(self.rel)).astype(BF16)
        km = (self.kk * jnp.exp(-self.rel)).astype(BF16)
        causal = (si >= ti) if self.reverse else (si <= ti)
        return [jnp.where(causal, _dot_nt(qm[:, h * HGRN_DK:(h + 1) * HGRN_DK], km[:, h * HGRN_DK:(h + 1) * HGRN_DK]), 0.0)
                for h in range(HGRN_HEADS)]

    def outputs(self, a_heads, st_ref, d):
        outs = []
        for h in range(HGRN_HEADS):
            sl = slice(h * HGRN_DK, (h + 1) * HGRN_DK)
            st = st_ref[d, h]
            o = _dot_nt(self.qt[:, sl], st.astype(BF16)) + _dot(a_heads[h].astype(BF16), self.vb[:, sl])
            st_ref[d, h] = st * self.decay[:, sl] + _dot_tn(self.vb[:, sl], self.kt[:, sl])
            outs.append(o)
        return jnp.concatenate(outs, axis=1)


def _hgrn_kernel(hq_ref, ff_ref, fb_ref, hi_ref, hg_ref, lbp_ref, gn_ref, mall_ref, ones_ref, s0_ref,
                 rec_ref, sfin_ref, st_ref, of_ref, ob_ref, gf_ref, gb_ref, a_ref, *, t):
    c = HGRN_CHUNK
    n = t // c
    st_ref[...] = s0_ref[0]

    def chunk_rows(ci):
        return pl.ds(pl.multiple_of(ci * c, c), c), pl.ds(pl.multiple_of((n - 1 - ci) * c, c), c)

    def gates_to(slot, ci):
        rf, rb = chunk_rows(ci)
        for d, (rows, f_ref) in enumerate(((rf, ff_ref), (rb, fb_ref))):
            f32_parts, bf16_parts = _hgrn_gates(
                hq_ref[rows, :], f_ref[rows, :], hi_ref[rows, :], lbp_ref[d, 0:1, :], lbp_ref[d, 1:2, :],
                lbp_ref[d, 2:3, :], mall_ref[d].astype(BF16), d == 1)
            for j in range(3):
                gf_ref[slot, d, j] = f32_parts[j]
                gb_ref[slot, d, j] = bf16_parts[j]

    gates_to(0, 0)

    def body(ci, carry):
        slot = ci & 1
        rf, rb = chunk_rows(ci)
        fwd = _HgrnDir([gf_ref[slot, 0, j] for j in range(3)], [gb_ref[slot, 0, j] for j in range(3)], False)
        bwd = _HgrnDir([gf_ref[slot, 1, j] for j in range(3)], [gb_ref[slot, 1, j] for j in range(3)], True)
        for d, hd in enumerate((fwd, bwd)):
            for h, a in enumerate(hd.midpoint_decay_matrices()):
                a_ref[d, h] = a
        gates_to(1 - slot, jnp.minimum(ci + 1, n - 1))

        @pl.when(jnp.maximum(fwd.span, bwd.span) > HGRN_SAFE_RANGE)
        def _():
            for d, hd in enumerate((fwd, bwd)):
                for h, a in enumerate(hd.tree_decay_matrices()):
                    a_ref[d, h] = a

        of_ref[rf, :] = fwd.outputs([a_ref[0, h] for h in range(HGRN_HEADS)], st_ref, 0)
        ob_ref[rb, :] = bwd.outputs([a_ref[1, h] for h in range(HGRN_HEADS)], st_ref, 1)
        return carry

    lax.fori_loop(0, n, body, 0)
    sfin_ref[0] = st_ref[...]
    o = of_ref[...] + ob_ref[...]
    ms = _dot_exact_rhs(o * o, ones_ref[...].astype(BF16)) * (1.0 / HGRN_DK)
    o = o * lax.rsqrt(ms + GN_EPS) * gn_ref[...]
    rec_ref[...] = (o * _silu(hg_ref[...])).astype(rec_ref.dtype)


def _hgrn(proj, row0, b, t, lbp, gn_row, s0t, name):
    base = row0 // t
    m_all = jnp.asarray(_hgrn_tables())
    ones_bd = jnp.asarray(np.kron(np.eye(HGRN_HEADS), np.ones((HGRN_DK, HGRN_DK))).astype(np.float32))

    def col(cstart):
        return pl.BlockSpec((t, HGRN_WIDTH), lambda bb, cb=cstart // HGRN_WIDTH: (base + bb, cb))

    const2 = lambda bb: (0, 0)
    const3 = lambda bb: (0, 0, 0)
    st_shape = (2, HGRN_HEADS, HGRN_DK, HGRN_DK)
    body = functools.partial(_hgrn_kernel, t=t)
    return pl.pallas_call(
        body,
        grid=(b,),
        in_specs=[
            col(COL_HQ), col(COL_FF), col(COL_FB), col(COL_HI), col(COL_HG),
            pl.BlockSpec((2, 3, HGRN_WIDTH), const3),
            pl.BlockSpec((1, HGRN_WIDTH), const2),
            pl.BlockSpec(m_all.shape, const3),
            pl.BlockSpec(ones_bd.shape, const2),
            pl.BlockSpec((1,) + st_shape, lambda bb: (bb, 0, 0, 0, 0)),
        ],
        out_specs=[
            pl.BlockSpec((t, HGRN_WIDTH), lambda bb: (bb, 0)),
            pl.BlockSpec((1,) + st_shape, lambda bb: (bb, 0, 0, 0, 0)),
        ],
        out_shape=[
            jax.ShapeDtypeStruct((b * t, HGRN_WIDTH), BF16),
            jax.ShapeDtypeStruct((b,) + st_shape, F32),
        ],
        scratch_shapes=[
            pltpu.VMEM(st_shape, F32),
            pltpu.VMEM((t, HGRN_WIDTH), F32),
            pltpu.VMEM((t, HGRN_WIDTH), F32),
            pltpu.VMEM((2, 2, 3, HGRN_CHUNK, HGRN_WIDTH), F32),
            pltpu.VMEM((2, 2, 3, HGRN_CHUNK, HGRN_WIDTH), BF16),
            pltpu.VMEM((2, HGRN_HEADS, HGRN_CHUNK, HGRN_CHUNK), F32),
        ],
        compiler_params=_cparams(("arbitrary",)),
        name=name,
    )(proj, proj, proj, proj, proj, lbp, gn_row, m_all, ones_bd, s0t)


def _outproj_kernel(*refs, n_ctx_tiles, n_mix):
    mix = refs[:3 * n_mix]
    refs = refs[3 * n_mix:]
    xs = refs[:-14]
    (mod_ref, w_ref, g_ref, b_ref, rw_ref, rb_ref, x1_ref, hp_ref, meta_ref, gate_ref, cnt_ref, wb_ref, tri_ref,
     run_ref) = refs[-14:]
    tm = x1_ref.shape[0]
    is_ctx = pl.program_id(0) < n_ctx_tiles
    x_in = jnp.where(is_ctx, xs[0][...], xs[1][...]) if len(xs) == 2 else xs[0][...]
    if n_mix == 2:
        attn, four, rec = [jnp.where(is_ctx, mix[2 * j][...], mix[2 * j + 1][...]) for j in range(3)]
    else:
        attn, four, rec = [r[...] for r in mix]

    @pl.when(pl.program_id(0) == 0)
    def _():
        wb_ref[...] = w_ref[0].astype(BF16)
        r = lax.broadcasted_iota(I32, (tm, tm), 0)
        c = lax.broadcasted_iota(I32, (tm, tm), 1)
        tri_ref[...] = jnp.where(r < c, 1.0, 0.0).astype(BF16)
        run_ref[...] = jnp.zeros_like(run_ref)

    out = _dot(attn, wb_ref[0:ATTN_WIDTH, :])
    out = out + _dot(four, wb_ref[ATTN_WIDTH:ATTN_WIDTH + FOURIER_WIDTH, :])
    out = out + _dot(rec, wb_ref[ATTN_WIDTH + FOURIER_WIDTH:, :])
    gate1 = mod_ref[0, 2:3, :]
    y = DEEPNORM_ALPHA * x_in + gate1 * out
    x1 = _ln_plain(y, LN_EPS) * g_ref[...] + b_ref[...]
    x1_ref[...] = x1
    h2 = _ln_plain(x1, ADA_EPS) * (1.0 + mod_ref[0, 4:5, :]) + mod_ref[0, 3:4, :]
    hp_ref[...] = _pack_bf16_pair(h2[:, :HALF_D], h2[:, HALF_D:])

    h_hi = h2.astype(BF16)
    h_lo = (h2 - h_hi.astype(F32)).astype(BF16)
    rwt = rw_ref[...]
    w_hi = rwt.astype(BF16)
    w_lo = (rwt - w_hi.astype(F32)).astype(BF16)
    scores = jax.nn.sigmoid(_dot_nt(w_hi, h_hi) + _dot_nt(w_hi, h_lo) + _dot_nt(w_lo, h_hi))
    remaining = scores + rb_ref[...]
    eidx = lax.broadcasted_iota(I32, scores.shape, 0).astype(F32)
    chosen = jnp.zeros(scores.shape, jnp.bool_)
    picks = []
    for _ in range(TOP_K):
        mx = jnp.max(remaining, axis=0, keepdims=True)
        first = jnp.min(jnp.where(remaining == mx, eidx, float(N_EXPERTS)), axis=0, keepdims=True)
        pick = eidx == first
        picks.append((pick, first))
        chosen = jnp.logical_or(chosen, pick)
        remaining = jnp.where(pick, -jnp.inf, remaining)
    sel = jnp.where(chosen, scores, 0.0)
    gates = sel / jnp.sum(sel, axis=0, keepdims=True) * ROUTED_SCALE

    onehot = jnp.where(chosen, 1.0, 0.0)
    rank = run_ref[...] + _dot(onehot.astype(BF16), tri_ref[...])
    run_ref[...] += jnp.sum(onehot, axis=1, keepdims=True)
    cnt_ref[...] = run_ref[...]

    ids, rks, gks = [], [], []
    for pick, first in picks:
        ids.append(first.astype(I32))
        rks.append(jnp.sum(jnp.where(pick, rank, 0.0), axis=0, keepdims=True).astype(I32))
        gks.append(jnp.sum(jnp.where(pick, gates, 0.0), axis=0, keepdims=True))
    meta_ref[...] = jnp.concatenate(ids + rks, axis=0)
    gate_ref[...] = jnp.concatenate(gks, axis=0)


def _outproj(attn, four, rec, x, mod_l, w_out, layer, g1, b1, rw, rb, lay, tm):
    n_tiles = lay.n // tm
    n_ctx_tiles = lay.n_ctx // tm
    row = lambda i: (i, 0)
    const = lambda i: (0, 0)
    xs = x if isinstance(x, tuple) else (x,)
    return pl.pallas_call(
        functools.partial(_outproj_kernel, n_ctx_tiles=n_ctx_tiles, n_mix=len(attn)),
        grid=(n_tiles,),
        in_specs=[
            *_group_specs(len(attn), tm, ATTN_WIDTH, n_ctx_tiles),
            *_group_specs(len(four), tm, FOURIER_WIDTH, n_ctx_tiles),
            *_group_specs(len(rec), tm, HGRN_WIDTH, n_ctx_tiles),
            *_group_specs(len(xs), tm, D_MODEL, n_ctx_tiles),
            pl.BlockSpec((1, N_MOD, D_MODEL), lambda i: (lay.cond_row(i, tm), 0, 0)),
            pl.BlockSpec((1, D_MODEL, D_MODEL), lambda i: (layer, 0, 0)),
            pl.BlockSpec((1, D_MODEL), const),
            pl.BlockSpec((1, D_MODEL), const),
            pl.BlockSpec((N_EXPERTS, D_MODEL), const),
            pl.BlockSpec((N_EXPERTS, 1), const),
        ],
        out_specs=[
            pl.BlockSpec((tm, D_MODEL), row),
            pl.BlockSpec((tm, HALF_D), row),
            pl.BlockSpec((2 * TOP_K, tm), lambda i: (0, i)),
            pl.BlockSpec((TOP_K, tm), lambda i: (0, i)),
            pl.BlockSpec((N_EXPERTS, 1), const),
        ],
        out_shape=[
            jax.ShapeDtypeStruct((lay.n, D_MODEL), F32),
            jax.ShapeDtypeStruct((lay.n, HALF_D), I32),
            jax.ShapeDtypeStruct((2 * TOP_K, lay.n), I32),
            jax.ShapeDtypeStruct((TOP_K, lay.n), F32),
            jax.ShapeDtypeStruct((N_EXPERTS, 1), F32),
        ],
        scratch_shapes=[
            pltpu.VMEM((D_MODEL, D_MODEL), BF16),
            pltpu.VMEM((tm, tm), BF16),
            pltpu.VMEM((N_EXPERTS, 1), F32),
        ],
        compiler_params=_cparams(("arbitrary",)),
        name="outproj_router" + lay.tag,
    )(*attn, *four, *rec, *xs, mod_l, w_out, g1, b1, rw, rb)


def _sc_workers():
    info = plsc.get_sparse_core_info()
    return info.num_cores, info.num_cores * info.num_subcores


def _sc_scatter_rows(rows, pos_b, r_out, tag=""):
    nc, nw = _sc_workers()
    n, w = rows.shape
    nbt, copies, _ = pos_b.shape
    assert nbt * SC_BATCH == n and nbt % (2 * nw) == 0
    per_w = nbt // nw
    mesh = plsc.VectorSubcoreMesh(core_axis_name="c", subcore_axis_name="s")

    @functools.partial(
        pl.kernel, mesh=mesh, out_type=jax.ShapeDtypeStruct((r_out, w), rows.dtype),
        scratch_types=[pltpu.VMEM((copies, SC_BATCH), I32), pltpu.VMEM((copies, SC_BATCH), I32),
                       pltpu.VMEM((SC_BATCH, w), rows.dtype), pltpu.VMEM((SC_BATCH, w), rows.dtype),
                       pltpu.SemaphoreType.DMA, pltpu.SemaphoreType.DMA,
                       pltpu.SemaphoreType.DMA, pltpu.SemaphoreType.DMA],
        name="sc_dispatch" + tag)
    def k(rows_hbm, pos_hbm, out_hbm, idx_a, idx_b, rows_a, rows_b, sem_ra, sem_rb, sem_sa, sem_sb):
        wid = lax.axis_index("s") * nc + lax.axis_index("c")
        first = wid * per_w

        def reads(j, idx_v, rows_v, sem):
            bt = first + j
            return (pltpu.make_async_copy(pos_hbm.at[bt], idx_v, sem),
                    pltpu.make_async_copy(rows_hbm.at[pl.ds(bt * SC_BATCH, SC_BATCH)], rows_v, sem))

        def scatters(idx_v, rows_v, sem):
            return [pltpu.make_async_copy(rows_v, out_hbm.at[idx_v.at[q]], sem) for q in range(copies)]

        def start(descs):
            for d in descs:
                d.start()

        def wait(descs):
            for d in descs:
                d.wait()

        start(reads(0, idx_a, rows_a, sem_ra))

        @pl.loop(0, per_w // 2)
        def _(p):
            j0 = 2 * p
            j1 = j0 + 1

            @pl.when(p > 0)
            def _():
                wait(scatters(idx_b, rows_b, sem_sb))

            start(reads(j1, idx_b, rows_b, sem_rb))
            wait(reads(j0, idx_a, rows_a, sem_ra))
            start(scatters(idx_a, rows_a, sem_sa))
            wait(reads(j1, idx_b, rows_b, sem_rb))
            start(scatters(idx_b, rows_b, sem_sb))
            wait(scatters(idx_a, rows_a, sem_sa))

            @pl.when(p + 1 < per_w // 2)
            def _():
                start(reads(j0 + 2, idx_a, rows_a, sem_ra))

        wait(scatters(idx_b, rows_b, sem_sb))

    return k(rows, pos_b)


def _sc_gather_rows(table, idx, tag=""):
    nc, nw = _sc_workers()
    r = idx.shape[0]
    w = table.shape[1]
    assert r % (2 * nw * SC_BATCH) == 0
    per_w = r // nw
    nb = per_w // SC_BATCH
    mesh = plsc.VectorSubcoreMesh(core_axis_name="c", subcore_axis_name="s")

    @functools.partial(
        pl.kernel, mesh=mesh, out_type=jax.ShapeDtypeStruct((r, w), table.dtype),
        scratch_types=[pltpu.VMEM((per_w,), I32),
                       pltpu.VMEM((SC_BATCH, w), table.dtype), pltpu.VMEM((SC_BATCH, w), table.dtype),
                       pltpu.SemaphoreType.DMA, pltpu.SemaphoreType.DMA,
                       pltpu.SemaphoreType.DMA, pltpu.SemaphoreType.DMA],
        name="sc_combine" + tag)
    def k(table_hbm, idx_hbm, out_hbm, idx_v, rows_a, rows_b, sem_ga, sem_gb, sem_wa, sem_wb):
        wid = lax.axis_index("s") * nc + lax.axis_index("c")
        base = wid * per_w
        pltpu.sync_copy(idx_hbm.at[pl.ds(base, per_w)], idx_v)

        def gather(j, rows_v, sem):
            return pltpu.make_async_copy(table_hbm.at[idx_v.at[pl.ds(j * SC_BATCH, SC_BATCH)]], rows_v, sem)

        def write(j, rows_v, sem):
            return pltpu.make_async_copy(rows_v, out_hbm.at[pl.ds(base + j * SC_BATCH, SC_BATCH)], sem)

        gather(0, rows_a, sem_ga).start()

        @pl.loop(0, nb // 2)
        def _(p):
            j0 = 2 * p
            j1 = j0 + 1

            @pl.when(p > 0)
            def _():
                write(j1 - 2, rows_b, sem_wb).wait()

            gather(j1, rows_b, sem_gb).start()
            gather(j0, rows_a, sem_ga).wait()
            write(j0, rows_a, sem_wa).start()
            gather(j1, rows_b, sem_gb).wait()
            write(j1, rows_b, sem_wb).start()
            write(j0, rows_a, sem_wa).wait()

            @pl.when(p + 1 < nb // 2)
            def _():
                gather(j0 + 2, rows_a, sem_ga).start()

        write(nb - 1, rows_b, sem_wb).wait()

    return k(table, idx)


def _experts_kernel(te_ref, na_ref, x_ref, w1_ref, w3_ref, w2_ref, o_ref, w1b_ref, w3b_ref, w2b_ref):
    del te_ref

    @pl.when(pl.program_id(0) < na_ref[0])
    def _():
        w1b_ref[...] = w1_ref[0, 0].astype(BF16)
        w3b_ref[...] = w3_ref[0, 0].astype(BF16)
        w2b_ref[...] = w2_ref[0, 0].astype(BF16)
        lo, hi = _unpack_bf16_pair(x_ref[...])
        lo = lo.astype(BF16)
        hi = hi.astype(BF16)
        a = _dot(lo, w1b_ref[0:HALF_D, :]) + _dot(hi, w1b_ref[HALF_D:, :])
        b = _dot(lo, w3b_ref[0:HALF_D, :]) + _dot(hi, w3b_ref[HALF_D:, :])
        y = _dot((_silu(a) * b).astype(BF16), w2b_ref[...])
        o_ref[...] = _pack_bf16_pair(y[:, :HALF_D], y[:, HALF_D:])


def _experts(xs, tile_expert, n_active, w1, w3, w2, layer, tm, tag):
    r = xs.shape[0]
    n_tiles = r // tm

    def xmap(j, te, na):
        return (jnp.minimum(j, na[0] - 1), 0)

    def wmap(j, te, na):
        return (layer, te[jnp.minimum(j, na[0] - 1)], 0, 0)

    grid_spec = pltpu.PrefetchScalarGridSpec(
        num_scalar_prefetch=2,
        grid=(n_tiles,),
        in_specs=[
            pl.BlockSpec((tm, HALF_D), xmap),
            pl.BlockSpec((1, 1, D_MODEL, EXPERT_FF), wmap),
            pl.BlockSpec((1, 1, D_MODEL, EXPERT_FF), wmap),
            pl.BlockSpec((1, 1, EXPERT_FF, D_MODEL), wmap),
        ],
        out_specs=pl.BlockSpec((tm, HALF_D), xmap),
        scratch_shapes=[
            pltpu.VMEM((D_MODEL, EXPERT_FF), BF16),
            pltpu.VMEM((D_MODEL, EXPERT_FF), BF16),
            pltpu.VMEM((EXPERT_FF, D_MODEL), BF16),
        ],
    )
    return pl.pallas_call(
        _experts_kernel,
        grid_spec=grid_spec,
        out_shape=jax.ShapeDtypeStruct((r, HALF_D), I32),
        compiler_params=_cparams(("arbitrary",)),
        name="experts" + tag,
    )(tile_expert, n_active, xs, w1, w3, w2)


def _combine_kernel(yp_ref, gate_ref, hp_ref, sw1_ref, sw3_ref, sw2_ref, x_ref, mod_ref, g_ref, b_ref, *refs,
                    n_ctx_tiles):
    outs = refs[:-3]
    w1b_ref, w3b_ref, w2b_ref = refs[-3:]

    @pl.when(pl.program_id(0) == 0)
    def _():
        w1b_ref[...] = sw1_ref[...].astype(BF16)
        w3b_ref[...] = sw3_ref[...].astype(BF16)
        w2b_ref[...] = sw2_ref[...].astype(BF16)

    lo, hi = _unpack_bf16_pair(hp_ref[...])
    lo = lo.astype(BF16)
    hi = hi.astype(BF16)
    a = _dot(lo, w1b_ref[0:HALF_D, :]) + _dot(hi, w1b_ref[HALF_D:, :])
    b = _dot(lo, w3b_ref[0:HALF_D, :]) + _dot(hi, w3b_ref[HALF_D:, :])
    shared = _dot((_silu(a) * b).astype(BF16), w2b_ref[...])
    acc_lo = shared[:, :HALF_D]
    acc_hi = shared[:, HALF_D:]
    gates = gate_ref[...]
    for k in range(TOP_K):
        ylo, yhi = _unpack_bf16_pair(yp_ref[k])
        gk = gates[:, k:k + 1]
        acc_lo = acc_lo + gk * ylo
        acc_hi = acc_hi + gk * yhi
    moe = jnp.concatenate([acc_lo, acc_hi], axis=1)
    y = DEEPNORM_ALPHA * x_ref[...] + mod_ref[0, 5:6, :] * moe
    res = _ln_plain(y, LN_EPS) * g_ref[...] + b_ref[...]
    if len(outs) == 1:
        outs[0][...] = res
    else:
        @pl.when(pl.program_id(0) < n_ctx_tiles)
        def _():
            outs[0][...] = res

        @pl.when(pl.program_id(0) >= n_ctx_tiles)
        def _():
            outs[1][...] = res


def _combine(yp, gate8, hp, sw1, sw3, sw2, x1, mod_l, g2, b2, lay, tm, split_out):
    n_tiles = lay.n // tm
    n_ctx_tiles = lay.n_ctx // tm
    row = lambda i: (i, 0)
    const = lambda i: (0, 0)
    if split_out:
        out_specs = _group_specs(2, tm, D_MODEL, n_ctx_tiles)
        out_shape = [jax.ShapeDtypeStruct((lay.n_ctx, D_MODEL), F32), jax.ShapeDtypeStruct((lay.n_lat, D_MODEL), F32)]
    else:
        out_specs = pl.BlockSpec((tm, D_MODEL), row)
        out_shape = jax.ShapeDtypeStruct((lay.n, D_MODEL), F32)
    return pl.pallas_call(
        functools.partial(_combine_kernel, n_ctx_tiles=n_ctx_tiles),
        grid=(n_tiles,),
        in_specs=[
            pl.BlockSpec((TOP_K, tm, HALF_D), lambda i: (0, i, 0)),
            pl.BlockSpec((tm, TOP_K), row),
            pl.BlockSpec((tm, HALF_D), row),
            pl.BlockSpec((D_MODEL, EXPERT_FF), const),
            pl.BlockSpec((D_MODEL, EXPERT_FF), const),
            pl.BlockSpec((EXPERT_FF, D_MODEL), const),
            pl.BlockSpec((tm, D_MODEL), row),
            pl.BlockSpec((1, N_MOD, D_MODEL), lambda i: (lay.cond_row(i, tm), 0, 0)),
            pl.BlockSpec((1, D_MODEL), const),
            pl.BlockSpec((1, D_MODEL), const),
        ],
        out_specs=out_specs,
        out_shape=out_shape,
        scratch_shapes=[
            pltpu.VMEM((D_MODEL, EXPERT_FF), BF16),
            pltpu.VMEM((D_MODEL, EXPERT_FF), BF16),
            pltpu.VMEM((EXPERT_FF, D_MODEL), BF16),
        ],
        compiler_params=_cparams(("arbitrary",)),
        name="combine_norm" + lay.tag,
    )(yp, gate8, hp, sw1, sw3, sw2, x1, mod_l, g2, b2)


def _moe_dispatch(hp, meta, counts, lay, tile):
    n = lay.n
    r_max = n * TOP_K + N_EXPERTS * tile
    n_tiles = r_max // tile
    cnt = counts.reshape(N_EXPERTS).astype(I32)
    padded = ((cnt + tile - 1) // tile) * tile
    ends = jnp.cumsum(padded)
    offsets = ends - padded
    idx8 = meta[:TOP_K]
    base8 = jnp.sum(jnp.where(idx8[:, :, None] == jnp.arange(N_EXPERTS, dtype=I32), offsets, 0), axis=-1)
    pos = (base8 + meta[TOP_K:]).astype(I32)
    tile_start = jnp.arange(n_tiles, dtype=I32) * tile
    tile_expert = jnp.minimum(jnp.sum(tile_start[:, None] >= ends[None, :], axis=1), N_EXPERTS - 1).astype(I32)
    n_active = (ends[-1] // tile).astype(I32).reshape(1)
    pos_b = pos.reshape(TOP_K, n // SC_BATCH, SC_BATCH).transpose(1, 0, 2)
    xs = _sc_scatter_rows(hp, pos_b, r_max, lay.tag)
    return xs, tile_expert, n_active, pos


def _moe_combine(ys, pos, gate8, hp, sw1, sw3, sw2, x1, mod_l, g2, b2, lay, split_out):
    n = lay.n
    yp = _sc_gather_rows(ys, pos.reshape(n * TOP_K), lay.tag).reshape(TOP_K, n, HALF_D)
    return _combine(yp, gate8.T, hp, sw1, sw3, sw2, x1, mod_l, g2, b2, lay, TOKEN_TILE, split_out)


def kernel(x_prompt, x_sample, cache_k, cache_v, state_hgrn, c, c_ctx, w_ada, b_ada, w_in, w_out, attn_sink, hgrn_lb, hgrn_norm, ln1_g, ln1_b, ln2_g, ln2_b, router_w, router_b, moe_w1, moe_w3, moe_w2, shared_w1, shared_w3, shared_w2):
    b_ctx, t_ctx, _ = x_prompt.shape
    b_lat, t_lat, _ = x_sample.shape
    past = cache_k.shape[2]
    tm = TOKEN_TILE
    assert 1 + b_lat <= COND_ROWS
    lay = _Layout(b_ctx, t_ctx, b_lat, t_lat)
    assert lay.n_ctx % tm == 0 and t_lat % tm == 0 and lay.n_ctx % t_lat == 0

    cond = jnp.concatenate([c_ctx[None, :], c, jnp.zeros((COND_ROWS - 1 - b_lat, D_MODEL), F32)], axis=0)
    mod = _adaln(cond, w_ada, b_ada).reshape(DEPTH, COND_ROWS, N_MOD, D_MODEL)

    lb_all = jnp.cumsum(jax.nn.softmax(hgrn_lb.astype(F32), axis=0), axis=0)
    lb_all = lb_all - lb_all[:1]
    lbp = jnp.stack([jnp.log(lb_all), jnp.log1p(-lb_all), 1.0 - lb_all], axis=2)

    cos_t, sin_t = _rope_tables(lay, tm)
    zero_state = jnp.zeros((b_ctx, 2, HGRN_HEADS, HGRN_DK, HGRN_DK), F32)

    def layer(l, lay, x, split_out):
        lat = slice(lay.lat_first, lay.lat_first + lay.b_lat)
        outs = _inproj(x, mod[l], w_in, l, cos_t, sin_t, lay, tm)
        proj = outs[0]
        sink_l = attn_sink[l].reshape(1, N_HEADS)
        gn_row = jnp.tile(hgrn_norm[l], HGRN_HEADS).reshape(1, HGRN_WIDTH)
        attn, four, rec, extras = [], [], [], None
        if lay.b_ctx:
            attn.append(_attn_context(proj, sink_l, lay))
            four.append(_fourier(proj, 0, lay.b_ctx, t_ctx, t_ctx, "fourier_ctx"))
            rec_c, s_fin = _hgrn(proj, 0, lay.b_ctx, t_ctx, lbp[l], gn_row, zero_state, "hgrn_ctx")
            rec.append(rec_c)
            extras = (outs[1], outs[2], s_fin)
        attn.append(_attn_latent(proj, cache_k[lat, l].reshape(lay.b_lat, past, KV_WIDTH),
                                 cache_v[lat, l].reshape(lay.b_lat, past, KV_WIDTH), sink_l, lay))
        four.append(_fourier(proj, lay.n_ctx, lay.b_lat, t_lat, min(t_lat, 512), "fourier_lat" + lay.tag))
        s0t = jnp.swapaxes(state_hgrn[lat, l].astype(F32), -1, -2)
        rec.append(_hgrn(proj, lay.n_ctx, lay.b_lat, t_lat, lbp[l], gn_row, s0t, "hgrn_lat" + lay.tag)[0])
        x1, hp, meta, gate8, counts = _outproj(
            tuple(attn), tuple(four), tuple(rec), x, mod[l], w_out, l, ln1_g[l].reshape(1, -1),
            ln1_b[l].reshape(1, -1), router_w[l].T, router_b[l].reshape(-1, 1), lay, tm)
        xs, tile_expert, n_active, pos = _moe_dispatch(hp, meta, counts, lay, EXPERT_TILE)
        ys = _experts(xs, tile_expert, n_active, moe_w1, moe_w3, moe_w2, l, EXPERT_TILE, lay.tag)
        x = _moe_combine(ys, pos, gate8, hp, shared_w1[l], shared_w3[l], shared_w2[l], x1, mod[l],
                         ln2_g[l].reshape(1, -1), ln2_b[l].reshape(1, -1), lay, split_out)
        return x, extras

    x = (x_prompt.reshape(lay.n_ctx, D_MODEL), x_sample.reshape(lay.n_lat, D_MODEL))
    ks_out, vs_out, ss_out = [], [], []
    for l in range(DEPTH):
        x, (k_new, v_new, s_fin) = layer(l, lay, x, split_out=(l == DEPTH - 1))
        ks_out.append(k_new.reshape(b_ctx, t_ctx, N_KV_HEADS, HEAD_DIM))
        vs_out.append(v_new.reshape(b_ctx, t_ctx, N_KV_HEADS, HEAD_DIM))
        ss_out.append(jnp.swapaxes(s_fin, -1, -2))

    y_prompt = x[0].reshape(b_ctx, t_ctx, D_MODEL)
    y_sample = x[1].reshape(b_lat, t_lat, D_MODEL)
    new_cache_k = jnp.stack(ks_out, axis=1)
    new_cache_v = jnp.stack(vs_out, axis=1)
    new_state = jnp.stack(ss_out, axis=1).astype(x_prompt.dtype)
    return (y_prompt, y_sample, new_cache_k, new_cache_v, new_state)
```

```python
import functools
import math

import numpy as np
import jax
import jax.numpy as jnp
from jax import lax
from jax.experimental import pallas as pl
from jax.experimental.pallas import tpu as pltpu
from jax.experimental.pallas import tpu_sc as plsc

F32 = jnp.float32
BF16 = jnp.bfloat16
I32 = jnp.int32

D_MODEL = 1024
HALF_D = D_MODEL // 2
DEPTH = 2
GRID_W = 64
ROPE_BASE = 10000.0
HEAD_DIM = 64
ATTN_WIDTH = 512
N_HEADS = 8
N_KV_HEADS = 2
KV_GROUP = 4
KV_WIDTH = N_KV_HEADS * HEAD_DIM
WINDOW = 128
ATTN_BLOCK = 128
FOURIER_WIDTH = 256
FOURIER_GROUPS = 4
HGRN_WIDTH = 256
HGRN_HEADS = 4
HGRN_DK = 64
HGRN_CHUNK = 64
IN_WIDTH = 2304
N_EXPERTS = 64
TOP_K = 8
EXPERT_FF = 256
ROUTED_SCALE = 2.5
N_MOD = 6
LN_EPS = 1e-5
ADA_EPS = 1e-6
GN_EPS = 1e-6
DEEPNORM_ALPHA = (2 * DEPTH) ** 0.25

COL_Q = 0
COL_K = 512
COL_V = 640
COL_U = 768
COL_HQ = 1024
COL_FF = 1280
COL_FB = 1536
COL_HI = 1792
COL_HG = 2048
ROPE_COLS = COL_V
MIX_IN_WIDTH = COL_HQ

V7X_LANES = 128
COND_ROWS = 16
NEG_BIG = -1e30
TOKEN_TILE = 512
EXPERT_TILE = 1024
SC_BATCH = 64

VMEM_LIMIT = 56 * 1024 * 1024


def _cparams(sem):
    return pltpu.CompilerParams(dimension_semantics=sem, vmem_limit_bytes=VMEM_LIMIT)


def _dot(a, b):
    return jnp.dot(a, b, preferred_element_type=F32)


def _dot_nt(a, b):
    return lax.dot_general(a, b, (((1,), (1,)), ((), ())), preferred_element_type=F32)


def _dot_tn(a, b):
    return lax.dot_general(a, b, (((0,), (0,)), ((), ())), preferred_element_type=F32)


def _split3(x):
    hi = x.astype(BF16)
    r1 = x - hi.astype(F32)
    mid = r1.astype(BF16)
    lo = (r1 - mid.astype(F32)).astype(BF16)
    return hi, mid, lo


def _dot_exact_lhs(m_bf16, x):
    hi, mid, lo = _split3(x)
    return _dot(m_bf16, hi) + _dot(m_bf16, mid) + _dot(m_bf16, lo)


def _dot_exact_rhs(x, m_bf16):
    hi, mid, lo = _split3(x)
    return _dot(hi, m_bf16) + _dot(mid, m_bf16) + _dot(lo, m_bf16)


def _dot_hp(a, b):
    a_hi = a.astype(BF16)
    a_lo = (a - a_hi.astype(F32)).astype(BF16)
    b_hi = b.astype(BF16)
    b_lo = (b - b_hi.astype(F32)).astype(BF16)
    return _dot(a_hi, b_hi) + _dot(a_hi, b_lo) + _dot(a_lo, b_hi)


def _pack_bf16_pair(lo, hi):
    return lax.bitcast_convert_type(pltpu.pack_elementwise([lo, hi], packed_dtype=BF16), I32)


def _unpack_bf16_pair(w):
    u = lax.bitcast_convert_type(w, jnp.uint32)
    lo = pltpu.unpack_elementwise(u, index=0, packed_dtype=BF16, unpacked_dtype=F32)
    hi = pltpu.unpack_elementwise(u, index=1, packed_dtype=BF16, unpacked_dtype=F32)
    return lo, hi


def _ln_plain(x, eps):
    mu = jnp.mean(x, axis=-1, keepdims=True)
    xc = x - mu
    var = jnp.mean(xc * xc, axis=-1, keepdims=True)
    return xc * lax.rsqrt(var + eps)


def _silu(x):
    return x * jax.nn.sigmoid(x)


def _adaln_kernel(c_ref, w_ref, b_ref, o_ref):
    s = _silu(c_ref[...])
    o_ref[0] = _dot_hp(s, w_ref[0]) + b_ref[0]


def _adaln(cond, w_ada, b_ada):
    return pl.pallas_call(
        _adaln_kernel,
        grid=(DEPTH, N_MOD),
        in_specs=[
            pl.BlockSpec((COND_ROWS, D_MODEL), lambda l, j: (0, 0)),
            pl.BlockSpec((1, D_MODEL, D_MODEL), lambda l, j: (l, 0, j)),
            pl.BlockSpec((1, 1, D_MODEL), lambda l, j: (l, 0, j)),
        ],
        out_specs=pl.BlockSpec((1, COND_ROWS, D_MODEL), lambda l, j: (l, 0, j)),
        out_shape=jax.ShapeDtypeStruct((DEPTH, COND_ROWS, N_MOD * D_MODEL), F32),
        compiler_params=_cparams(("arbitrary", "arbitrary")),
        name="adaln",
    )(cond, w_ada, b_ada.reshape(DEPTH, 1, N_MOD * D_MODEL))


class _Layout:
    def __init__(self, b_ctx, t_ctx, b_lat, t_lat, lat_first=0, tag=""):
        self.b_ctx, self.t_ctx, self.b_lat, self.t_lat = b_ctx, t_ctx, b_lat, t_lat
        self.lat_first = lat_first
        self.n_ctx = b_ctx * t_ctx
        self.n_lat = b_lat * t_lat
        self.n = self.n_ctx + self.n_lat
        self.tag = tag

    def cond_row(self, tile, tm):
        n_ctx_tiles = self.n_ctx // tm
        per_batch = self.t_lat // tm
        return jnp.where(tile < n_ctx_tiles, 0, 1 + self.lat_first + (tile - n_ctx_tiles) // per_batch)


def _group_specs(n_arrays, tm, width, n_ctx_tiles):
    if n_arrays == 1:
        return [pl.BlockSpec((tm, width), lambda i: (i, 0))]
    return [pl.BlockSpec((tm, width), lambda i: (jnp.minimum(i, n_ctx_tiles - 1), 0)),
            pl.BlockSpec((tm, width), lambda i: (jnp.maximum(i - n_ctx_tiles, 0), 0))]


def _inproj_kernel(*refs, n_ctx_tiles):
    n_tail = 9 if n_ctx_tiles > 0 else 7
    xs = refs[:-n_tail]
    mod_ref, w_ref, cos_ref, sin_ref, oa_ref, oh_ref = refs[-n_tail:-n_tail + 6]
    wb_ref = refs[-1]

    @pl.when(pl.program_id(0) == 0)
    def _():
        wb_ref[...] = w_ref[0].astype(BF16)

    if len(xs) == 2:
        x = jnp.where(pl.program_id(0) < n_ctx_tiles, xs[0][...], xs[1][...])
    else:
        x = xs[0][...]
    shift = mod_ref[0, 0:1, :]
    scale = mod_ref[0, 1:2, :]
    h = (_ln_plain(x, ADA_EPS) * (1.0 + scale) + shift).astype(BF16)
    p = _dot(h, wb_ref[...])
    cos = cos_ref[...]
    sin = sin_ref[...]
    lane = lax.broadcasted_iota(I32, cos.shape, 1)
    first_half = (lane & 31) < 16
    for cb in range(ROPE_COLS // V7X_LANES):
        seg = p[:, cb * V7X_LANES:(cb + 1) * V7X_LANES]
        partner = jnp.where(first_half, pltpu.roll(seg, V7X_LANES - 16, 1), pltpu.roll(seg, 16, 1))
        oa_ref[:, cb * V7X_LANES:(cb + 1) * V7X_LANES] = (seg * cos + partner * sin).astype(BF16)
    oa_ref[:, ROPE_COLS:] = p[:, ROPE_COLS:MIX_IN_WIDTH].astype(BF16)
    oh_ref[...] = p[:, MIX_IN_WIDTH:]

    if n_ctx_tiles > 0:
        kc_ref, vc_ref = refs[-3], refs[-2]

        @pl.when(pl.program_id(0) < n_ctx_tiles)
        def _():
            kc_ref[...] = p[:, COL_K:COL_K + KV_WIDTH]
            vc_ref[...] = p[:, COL_V:COL_V + KV_WIDTH]


def _rope_tables(lay, tm):
    t = lay.t_lat
    pos = jnp.arange(t)
    row = (pos // GRID_W).astype(F32)
    col = (pos % GRID_W).astype(F32)
    n_freq = HEAD_DIM // 4
    inv = ROPE_BASE ** (-jnp.arange(n_freq, dtype=F32) / n_freq)
    ang_r = row[:, None] * inv
    ang_c = col[:, None] * inv
    ang = jnp.concatenate([ang_r, ang_r, ang_c, ang_c], axis=1)
    sign = jnp.concatenate([-jnp.ones(n_freq), jnp.ones(n_freq), -jnp.ones(n_freq), jnp.ones(n_freq)]).astype(F32)
    cos = jnp.cos(ang)
    sin = jnp.sin(ang) * sign
    cos = jnp.concatenate([jnp.ones((tm, HEAD_DIM), F32), cos], axis=0)
    sin = jnp.concatenate([jnp.zeros((tm, HEAD_DIM), F32), sin], axis=0)
    return jnp.tile(cos, (1, 2)), jnp.tile(sin, (1, 2))


def _inproj(x, mod_l, w_in, layer, cos_t, sin_t, lay, tm):
    n_tiles = lay.n // tm
    n_ctx_tiles = lay.n_ctx // tm
    per_batch = lay.t_lat // tm

    def tbl(i):
        return jnp.where(i < n_ctx_tiles, 0, 1 + (i - n_ctx_tiles) % per_batch)

    xs = x if isinstance(x, tuple) else (x,)
    kv_specs, kv_shapes = [], []
    if n_ctx_tiles > 0:
        kv_specs = [pl.BlockSpec((tm, KV_WIDTH), lambda i: (jnp.minimum(i, n_ctx_tiles - 1), 0))] * 2
        kv_shapes = [jax.ShapeDtypeStruct((lay.n_ctx, KV_WIDTH), F32)] * 2
    return pl.pallas_call(
        functools.partial(_inproj_kernel, n_ctx_tiles=n_ctx_tiles),
        grid=(n_tiles,),
        in_specs=[
            *_group_specs(len(xs), tm, D_MODEL, n_ctx_tiles),
            pl.BlockSpec((1, N_MOD, D_MODEL), lambda i: (lay.cond_row(i, tm), 0, 0)),
            pl.BlockSpec((1, D_MODEL, IN_WIDTH), lambda i: (layer, 0, 0), pipeline_mode=pl.Buffered(1)),
            pl.BlockSpec((tm, V7X_LANES), lambda i: (tbl(i), 0)),
            pl.BlockSpec((tm, V7X_LANES), lambda i: (tbl(i), 0)),
        ],
        out_specs=[pl.BlockSpec((tm, MIX_IN_WIDTH), lambda i: (i, 0)),
                   pl.BlockSpec((tm, IN_WIDTH - MIX_IN_WIDTH), lambda i: (i, 0))] + kv_specs,
        out_shape=[jax.ShapeDtypeStruct((lay.n, MIX_IN_WIDTH), BF16),
                   jax.ShapeDtypeStruct((lay.n, IN_WIDTH - MIX_IN_WIDTH), F32)] + kv_shapes,
        scratch_shapes=[pltpu.VMEM((D_MODEL, IN_WIDTH), BF16)],
        compiler_params=_cparams(("arbitrary",)),
        name="inproj" + lay.tag,
    )(*xs, mod_l, w_in, cos_t, sin_t)


def _attn_kernel(sink_ref, q_ref, *refs, n_local, has_ctx, t_total):
    o_ref = refs[-1]
    k_refs = refs[:n_local]
    v_refs = refs[n_local:2 * n_local]
    tq = q_ref.shape[0]
    scale = HEAD_DIM ** -0.5
    k_parts = [kr[...] for kr in k_refs]
    v_parts = [vr[...] for vr in v_refs]
    if has_ctx:
        k_parts.append(refs[2 * n_local][0].astype(k_parts[0].dtype))
        v_parts.append(refs[2 * n_local + 1][0].astype(v_parts[0].dtype))
    kall = (jnp.concatenate(k_parts, axis=0) if len(k_parts) > 1 else k_parts[0]).astype(F32)
    vall = (jnp.concatenate(v_parts, axis=0) if len(v_parts) > 1 else v_parts[0]).astype(F32)
    nk = kall.shape[0]
    k_sw = pltpu.roll(kall, HEAD_DIM, 1)
    v_sw = pltpu.roll(vall, HEAD_DIM, 1)
    lo_half = lax.broadcasted_iota(I32, (1, V7X_LANES), 1) < HEAD_DIM
    er = jnp.where(lax.broadcasted_iota(I32, (2 * nk, V7X_LANES), 0) < nk, 0, 1)
    el = jnp.where(lax.broadcasted_iota(I32, (2 * nk, V7X_LANES), 1) < HEAD_DIM, 0, 1)
    ones_blk = jnp.where(er == el, 1.0, 0.0).astype(BF16)
    if n_local > 1:
        i = pl.program_id(1)
        band = refs[-2][...]
        first_blk = jnp.where(i == 0, NEG_BIG, 0.0)
        last_blk = jnp.where(i == t_total // tq - 1, NEG_BIG, 0.0)

        def mask_local(sc):
            loc = sc[:, :n_local * tq] + band
            parts = [loc[:, :tq] + first_blk, loc[:, tq:(n_local - 1) * tq], loc[:, (n_local - 1) * tq:] + last_blk]
            return jnp.concatenate(parts + [sc[:, n_local * tq:]], axis=1)
    else:
        mask_local = None
    for g in range(N_KV_HEADS):
        k_own, k_oth = (kall, k_sw) if g == 0 else (k_sw, kall)
        v_own, v_oth = (vall, v_sw) if g == 0 else (v_sw, vall)
        k2 = jnp.concatenate([jnp.where(lo_half, k_own, 0.0), jnp.where(lo_half, 0.0, k_oth)], axis=0).astype(BF16)
        v2 = jnp.concatenate([jnp.where(lo_half, v_own, 0.0), jnp.where(lo_half, 0.0, v_oth)], axis=0).astype(BF16)
        v2e = jnp.concatenate([v2, ones_blk], axis=1)
        pairs = [2 * g, 2 * g + 1]
        qq = jnp.concatenate([q_ref[:, p * V7X_LANES:(p + 1) * V7X_LANES] for p in pairs], axis=0)
        qq = (qq.astype(F32) * scale).astype(BF16)
        sink_a = jnp.concatenate([jnp.full((tq, 1), sink_ref[0, 2 * p], F32) for p in pairs], axis=0)
        sink_b = jnp.concatenate([jnp.full((tq, 1), sink_ref[0, 2 * p + 1], F32) for p in pairs], axis=0)
        s = _dot_nt(qq, k2)
        s_a = s[:, :nk]
        s_b = s[:, nk:]
        if mask_local is not None:
            s_a = mask_local(s_a)
            s_b = mask_local(s_b)
        m_a = jnp.maximum(jnp.max(s_a, axis=1, keepdims=True), sink_a)
        m_b = jnp.maximum(jnp.max(s_b, axis=1, keepdims=True), sink_b)
        pe = jnp.concatenate([jnp.exp(s_a - m_a).astype(BF16), jnp.exp(s_b - m_b).astype(BF16)], axis=1)
        acc = _dot(pe, v2e)
        sink_term = jnp.where(lo_half, jnp.exp(sink_a - m_a), jnp.exp(sink_b - m_b))
        o = acc[:, :V7X_LANES] / (acc[:, V7X_LANES:] + sink_term)
        for j, p in enumerate(pairs):
            o_ref[:, p * V7X_LANES:(p + 1) * V7X_LANES] = o[j * tq:(j + 1) * tq].astype(o_ref.dtype)


def _attn_context(proj, sink_l, lay):
    t = lay.t_ctx
    kb, vb = COL_K // KV_WIDTH, COL_V // KV_WIDTH
    body = functools.partial(_attn_kernel, n_local=1, has_ctx=False, t_total=t)
    return pl.pallas_call(
        body,
        grid=(lay.b_ctx,),
        in_specs=[
            pl.BlockSpec(memory_space=pltpu.SMEM),
            pl.BlockSpec((t, ATTN_WIDTH), lambda b: (b, 0)),
            pl.BlockSpec((t, KV_WIDTH), lambda b: (b, kb)),
            pl.BlockSpec((t, KV_WIDTH), lambda b: (b, vb)),
        ],
        out_specs=pl.BlockSpec((t, ATTN_WIDTH), lambda b: (b, 0)),
        out_shape=jax.ShapeDtypeStruct((lay.n_ctx, ATTN_WIDTH), BF16),
        compiler_params=_cparams(("arbitrary",)),
        name="attn_ctx",
    )(sink_l, proj, proj, proj)


def _attn_latent(proj, k_ctx, v_ctx, sink_l, lay):
    t = lay.t_lat
    tq = ATTN_BLOCK
    nq = t // tq
    base = lay.n_ctx // tq
    kb, vb = COL_K // KV_WIDTH, COL_V // KV_WIDTH
    past = k_ctx.shape[1]

    def rows(off):
        return lambda b, i: base + b * nq + jnp.clip(i + off, 0, nq - 1)

    def kv_specs(col):
        return [pl.BlockSpec((tq, KV_WIDTH), (lambda b, i, f=rows(off): (f(b, i), col))) for off in (-1, 0, 1)]

    body = functools.partial(_attn_kernel, n_local=3, has_ctx=True, t_total=t)
    rel = np.arange(3 * tq)[None, :] - tq - (np.arange(2 * tq)[:, None] % tq)
    band = jnp.asarray(np.where(np.abs(rel) <= WINDOW, 0.0, NEG_BIG).astype(np.float32))
    return pl.pallas_call(
        body,
        grid=(lay.b_lat, nq),
        in_specs=[
            pl.BlockSpec(memory_space=pltpu.SMEM),
            pl.BlockSpec((tq, ATTN_WIDTH), lambda b, i: (base + b * nq + i, 0)),
            *kv_specs(kb),
            *kv_specs(vb),
            pl.BlockSpec((1, past, KV_WIDTH), lambda b, i: (b, 0, 0)),
            pl.BlockSpec((1, past, KV_WIDTH), lambda b, i: (b, 0, 0)),
            pl.BlockSpec(band.shape, lambda b, i: (0, 0)),
        ],
        out_specs=pl.BlockSpec((tq, ATTN_WIDTH), lambda b, i: (b * nq + i, 0)),
        out_shape=jax.ShapeDtypeStruct((lay.n_lat, ATTN_WIDTH), BF16),
        compiler_params=_cparams(("arbitrary", "arbitrary")),
        name="attn_lat" + lay.tag,
    )(sink_l, proj, proj, proj, proj, proj, proj, proj, k_ctx, v_ctx, band)


def _fourier_kernel(cs_ref, u_ref, cc_ref, sc_ref, o_ref, csb_ref, *, scale):
    @pl.when(pl.program_id(1) == 0)
    def _():
        csb_ref[...] = cs_ref[...].astype(BF16)

    z = u_ref[...].astype(BF16)
    zc = _dot(z, cc_ref[...].astype(BF16)).astype(BF16)
    zs = _dot(z, sc_ref[...].astype(BF16)).astype(BF16)
    zz = jnp.concatenate([zc, zs], axis=0)
    o_ref[...] = (_dot(csb_ref[...], zz) * scale).astype(o_ref.dtype)


@functools.lru_cache(maxsize=None)
def _dft_tables(t):
    idx = np.arange(t, dtype=np.int64)
    ang = 2.0 * np.pi * ((idx[:, None] * idx[None, :]) % t).astype(np.float64) / t
    cs = np.concatenate([np.cos(ang), -np.sin(ang)], axis=1).astype(np.float32)
    cw = FOURIER_WIDTH // FOURIER_GROUPS
    cidx = np.arange(cw, dtype=np.int64)
    cang = 2.0 * np.pi * ((cidx[:, None] * cidx[None, :]) % cw).astype(np.float64) / cw
    eye = np.eye(FOURIER_GROUPS)
    cc = np.kron(eye, np.cos(cang)).astype(np.float32)
    sc = np.kron(eye, np.sin(cang)).astype(np.float32)
    return cs, cc, sc


def _fourier(proj, row0, b, t, tm, name):
    cs, cc, sc = _dft_tables(t)
    cw = FOURIER_WIDTH // FOURIER_GROUPS
    nt = t // tm
    ub = COL_U // FOURIER_WIDTH
    base = row0 // t
    body = functools.partial(_fourier_kernel, scale=1.0 / math.sqrt(t * cw))
    return pl.pallas_call(
        body,
        grid=(nt, b),
        in_specs=[
            pl.BlockSpec((tm, 2 * t), lambda i, bb: (i, 0)),
            pl.BlockSpec((t, FOURIER_WIDTH), lambda i, bb: (base + bb, ub)),
            pl.BlockSpec((FOURIER_WIDTH, FOURIER_WIDTH), lambda i, bb: (0, 0)),
            pl.BlockSpec((FOURIER_WIDTH, FOURIER_WIDTH), lambda i, bb: (0, 0)),
        ],
        out_specs=pl.BlockSpec((tm, FOURIER_WIDTH), lambda i, bb: (bb * nt + i, 0)),
        out_shape=jax.ShapeDtypeStruct((b * t, FOURIER_WIDTH), BF16),
        scratch_shapes=[pltpu.VMEM((tm, 2 * t), BF16)],
        compiler_params=_cparams(("arbitrary", "arbitrary")),
        name=name,
    )(jnp.asarray(cs), proj, jnp.asarray(cc), jnp.asarray(sc))


HGRN_LEVELS = (64, 32, 16, 8, 4, 2)
HGRN_SAFE_RANGE = 80.0


@functools.lru_cache(maxsize=None)
def _hgrn_tables():
    c = HGRN_CHUNK
    return np.stack([np.tril(np.ones((c, c))), np.triu(np.ones((c, c)))]).astype(np.float32)


def _boundary_rows(b, m, reverse):
    c, w = b.shape
    half = m // 2
    off = half if reverse else half - 1
    if m >= 16:
        return jnp.concatenate(
            [jnp.broadcast_to(b[s + off:s + off + 1], (m, w)) for s in range(0, c, m)], axis=0)
    sub = lax.broadcasted_iota(I32, (c, w), 0) & 7
    b3 = b.reshape(c // 8, 8, w)

    def bcast(j):
        return jnp.broadcast_to(b3[:, j:j + 1, :], (c // 8, 8, w)).reshape(c, w)

    if m == 8:
        return bcast(off)
    if m == 4:
        return jnp.where(sub < 4, bcast(off), bcast(4 + off))
    assert m == 2
    if reverse:
        return jnp.where((sub & 1) == 1, b, pltpu.roll(b, c - 1, 0))
    return jnp.where((sub & 1) == 0, b, pltpu.roll(b, 1, 0))


def _hgrn_gates(q, z, v, loglb, log1mlb, onemlb, cum, reverse):
    c = HGRN_CHUNK
    log_sig = jnp.minimum(z, 0.0) - jnp.log1p(jnp.exp(-jnp.abs(z)))
    bb = log1mlb + log_sig
    mx = jnp.maximum(loglb, bb)
    lf = mx + jnp.log1p(jnp.exp(-jnp.abs(loglb - bb)))
    kk = onemlb * jax.nn.sigmoid(-z)
    b = _dot_exact_lhs(cum, lf)
    b_end = b[0:1] if reverse else b[c - 1:c]
    qt = (q * jnp.exp(b)).astype(BF16)
    kt = (kk * jnp.exp(b_end - b)).astype(BF16)
    return (q, kk, b), (qt, kt, v.astype(BF16))


class _HgrnDir:
    def __init__(self, f32_parts, bf16_parts, reverse):
        c = HGRN_CHUNK
        self.q, self.kk, self.b = f32_parts
        self.qt, self.kt, self.vb = bf16_parts
        self.reverse = reverse
        b_end = self.b[0:1] if reverse else self.b[c - 1:c]
        self.decay = jnp.exp(b_end)
        mid = c // 2 if reverse else c // 2 - 1
        self.rel = self.b - self.b[mid:mid + 1]
        self.span = jnp.max(jnp.abs(self.rel))

    def tree_decay_matrices(self):
        c = HGRN_CHUNK
        q, kk, b = self.q, self.kk, self.b
        row = lax.broadcasted_iota(I32, (c, 1), 0)
        ti = lax.broadcasted_iota(I32, (c, c), 0)
        si = lax.broadcasted_iota(I32, (c, c), 1)
        qb = q.astype(BF16)
        kb = kk.astype(BF16)
        heads = [slice(h * HGRN_DK, (h + 1) * HGRN_DK) for h in range(HGRN_HEADS)]
        acc = [jnp.where(ti == si, _dot_nt(qb[:, sl], kb[:, sl]), 0.0) for sl in heads]
        for m in HGRN_LEVELS:
            r = _boundary_rows(b, m, self.reverse)
            upper = (row & (m - 1)) >= (m // 2)
            q_side = jnp.logical_not(upper) if self.reverse else upper
            e = jnp.exp(jnp.where(q_side, b - r, r - b))
            qf = jnp.where(q_side, q * e, 0.0).astype(BF16)
            kf = jnp.where(q_side, 0.0, kk * e).astype(BF16)
            same_block = (ti & -m) == (si & -m)
            for h, sl in enumerate(heads):
                acc[h] = acc[h] + jnp.where(same_block, _dot_nt(qf[:, sl], kf[:, sl]), 0.0)
        return acc

    def midpoint_decay_matrices(self):
        c = HGRN_CHUNK
        ti = lax.broadcasted_iota(I32, (c, c), 0)
        si = lax.broadcasted_iota(I32, (c, c), 1)
        qm = (self.q * jnp.exp(self.rel)).astype(BF16)
        km = (self.kk * jnp.exp(-self.rel)).astype(BF16)
        causal = (si >= ti) if self.reverse else (si <= ti)
        return [jnp.where(causal, _dot_nt(qm[:, h * HGRN_DK:(h + 1) * HGRN_DK], km[:, h * HGRN_DK:(h + 1) * HGRN_DK]), 0.0)
                for h in range(HGRN_HEADS)]

    def outputs(self, a_heads, st_ref, d):
        outs = []
        for h in range(HGRN_HEADS):
            sl = slice(h * HGRN_DK, (h + 1) * HGRN_DK)
            st = st_ref[d, h]
            o = _dot_nt(self.qt[:, sl], st.astype(BF16)) + _dot(a_heads[h].astype(BF16), self.vb[:, sl])
            st_ref[d, h] = st * self.decay[:, sl] + _dot_tn(self.vb[:, sl], self.kt[:, sl])
            outs.append(o)
        return jnp.concatenate(outs, axis=1)


def _hgrn_kernel(hq_ref, ff_ref, fb_ref, hi_ref, hg_ref, lbp_ref, gn_ref, mall_ref, ones_ref, s0_ref,
                 rec_ref, sfin_ref, st_ref, of_ref, ob_ref, gf_ref, gb_ref, a_ref, *, t):
    c = HGRN_CHUNK
    n = t // c
    st_ref[...] = s0_ref[0]

    def chunk_rows(ci):
        return pl.ds(pl.multiple_of(ci * c, c), c), pl.ds(pl.multiple_of((n - 1 - ci) * c, c), c)

    def gates_to(slot, ci):
        rf, rb = chunk_rows(ci)
        for d, (rows, f_ref) in enumerate(((rf, ff_ref), (rb, fb_ref))):
            f32_parts, bf16_parts = _hgrn_gates(
                hq_ref[rows, :], f_ref[rows, :], hi_ref[rows, :], lbp_ref[d, 0:1, :], lbp_ref[d, 1:2, :],
                lbp_ref[d, 2:3, :], mall_ref[d].astype(BF16), d == 1)
            for j in range(3):
                gf_ref[slot, d, j] = f32_parts[j]
                gb_ref[slot, d, j] = bf16_parts[j]

    gates_to(0, 0)

    def body(ci, carry):
        slot = ci & 1
        rf, rb = chunk_rows(ci)
        fwd = _HgrnDir([gf_ref[slot, 0, j] for j in range(3)], [gb_ref[slot, 0, j] for j in range(3)], False)
        bwd = _HgrnDir([gf_ref[slot, 1, j] for j in range(3)], [gb_ref[slot, 1, j] for j in range(3)], True)
        for d, hd in enumerate((fwd, bwd)):
            for h, a in enumerate(hd.midpoint_decay_matrices()):
                a_ref[d, h] = a
        gates_to(1 - slot, jnp.minimum(ci + 1, n - 1))

        @pl.when(jnp.maximum(fwd.span, bwd.span) > HGRN_SAFE_RANGE)
        def _():
            for d, hd in enumerate((fwd, bwd)):
                for h, a in enumerate(hd.tree_decay_matrices()):
                    a_ref[d, h] = a

        of_ref[rf, :] = fwd.outputs([a_ref[0, h] for h in range(HGRN_HEADS)], st_ref, 0)
        ob_ref[rb, :] = bwd.outputs([a_ref[1, h] for h in range(HGRN_HEADS)], st_ref, 1)
        return carry

    lax.fori_loop(0, n, body, 0)
    sfin_ref[0] = st_ref[...]
    o = of_ref[...] + ob_ref[...]
    ms = _dot_exact_rhs(o * o, ones_ref[...].astype(BF16)) * (1.0 / HGRN_DK)
    o = o * lax.rsqrt(ms + GN_EPS) * gn_ref[...]
    rec_ref[...] = (o * _silu(hg_ref[...])).astype(rec_ref.dtype)


def _hgrn(proj, row0, b, t, lbp, gn_row, s0t, name):
    base = row0 // t
    m_all = jnp.asarray(_hgrn_tables())
    ones_bd = jnp.asarray(np.kron(np.eye(HGRN_HEADS), np.ones((HGRN_DK, HGRN_DK))).astype(np.float32))

    def col(cstart):
        return pl.BlockSpec((t, HGRN_WIDTH), lambda bb, cb=(cstart - MIX_IN_WIDTH) // HGRN_WIDTH: (base + bb, cb))

    const2 = lambda bb: (0, 0)
    const3 = lambda bb: (0, 0, 0)
    st_shape = (2, HGRN_HEADS, HGRN_DK, HGRN_DK)
    body = functools.partial(_hgrn_kernel, t=t)
    return pl.pallas_call(
        body,
        grid=(b,),
        in_specs=[
            col(COL_HQ), col(COL_FF), col(COL_FB), col(COL_HI), col(COL_HG),
            pl.BlockSpec((2, 3, HGRN_WIDTH), const3),
            pl.BlockSpec((1, HGRN_WIDTH), const2),
            pl.BlockSpec(m_all.shape, const3),
            pl.BlockSpec(ones_bd.shape, const2),
            pl.BlockSpec((1,) + st_shape, lambda bb: (bb, 0, 0, 0, 0)),
        ],
        out_specs=[
            pl.BlockSpec((t, HGRN_WIDTH), lambda bb: (bb, 0)),
            pl.BlockSpec((1,) + st_shape, lambda bb: (bb, 0, 0, 0, 0)),
        ],
        out_shape=[
            jax.ShapeDtypeStruct((b * t, HGRN_WIDTH), BF16),
            jax.ShapeDtypeStruct((b,) + st_shape, F32),
        ],
        scratch_shapes=[
            pltpu.VMEM(st_shape, F32),
            pltpu.VMEM((t, HGRN_WIDTH), F32),
            pltpu.VMEM((t, HGRN_WIDTH), F32),
            pltpu.VMEM((2, 2, 3, HGRN_CHUNK, HGRN_WIDTH), F32),
            pltpu.VMEM((2, 2, 3, HGRN_CHUNK, HGRN_WIDTH), BF16),
            pltpu.VMEM((2, HGRN_HEADS, HGRN_CHUNK, HGRN_CHUNK), F32),
        ],
        compiler_params=_cparams(("arbitrary",)),
        name=name,
    )(proj, proj, proj, proj, proj, lbp, gn_row, m_all, ones_bd, s0t)


def _outproj_kernel(*refs, n_ctx_tiles, n_mix):
    mix = refs[:3 * n_mix]
    refs = refs[3 * n_mix:]
    xs = refs[:-14]
    (mod_ref, w_ref, g_ref, b_ref, rw_ref, rb_ref, x1_ref, hp_ref, meta_ref, gate_ref, cnt_ref, wb_ref, tri_ref,
     run_ref) = refs[-14:]
    tm = x1_ref.shape[0]
    is_ctx = pl.program_id(0) < n_ctx_tiles
    x_in = jnp.where(is_ctx, xs[0][...], xs[1][...]) if len(xs) == 2 else xs[0][...]
    if n_mix == 2:
        attn, four, rec = [jnp.where(is_ctx, mix[2 * j][...], mix[2 * j + 1][...]) for j in range(3)]
    else:
        attn, four, rec = [r[...] for r in mix]

    @pl.when(pl.program_id(0) == 0)
    def _():
        wb_ref[...] = w_ref[0].astype(BF16)
        r = lax.broadcasted_iota(I32, (tm, tm), 0)
        c = lax.broadcasted_iota(I32, (tm, tm), 1)
        tri_ref[...] = jnp.where(r < c, 1.0, 0.0).astype(BF16)
        run_ref[...] = jnp.zeros_like(run_ref)

    out = _dot(attn, wb_ref[0:ATTN_WIDTH, :])
    out = out + _dot(four, wb_ref[ATTN_WIDTH:ATTN_WIDTH + FOURIER_WIDTH, :])
    out = out + _dot(rec, wb_ref[ATTN_WIDTH + FOURIER_WIDTH:, :])
    gate1 = mod_ref[0, 2:3, :]
    y = DEEPNORM_ALPHA * x_in + gate1 * out
    x1 = _ln_plain(y, LN_EPS) * g_ref[...] + b_ref[...]
    x1_ref[...] = x1
    h2 = _ln_plain(x1, ADA_EPS) * (1.0 + mod_ref[0, 4:5, :]) + mod_ref[0, 3:4, :]
    hp_ref[...] = _pack_bf16_pair(h2[:, :HALF_D], h2[:, HALF_D:])

    h_hi = h2.astype(BF16)
    h_lo = (h2 - h_hi.astype(F32)).astype(BF16)
    rwt = rw_ref[...]
    w_hi = rwt.astype(BF16)
    w_lo = (rwt - w_hi.astype(F32)).astype(BF16)
    scores = jax.nn.sigmoid(_dot_nt(w_hi, h_hi) + _dot_nt(w_hi, h_lo) + _dot_nt(w_lo, h_hi))
    remaining = scores + rb_ref[...]
    eidx = lax.broadcasted_iota(I32, scores.shape, 0).astype(F32)
    chosen = jnp.zeros(scores.shape, jnp.bool_)
    picks = []
    for _ in range(TOP_K):
        mx = jnp.max(remaining, axis=0, keepdims=True)
        first = jnp.min(jnp.where(remaining == mx, eidx, float(N_EXPERTS)), axis=0, keepdims=True)
        pick = eidx == first
        picks.append((pick, first))
        chosen = jnp.logical_or(chosen, pick)
        remaining = jnp.where(pick, -jnp.inf, remaining)
    sel = jnp.where(chosen, scores, 0.0)
    gates = sel / jnp.sum(sel, axis=0, keepdims=True) * ROUTED_SCALE

    onehot = jnp.where(chosen, 1.0, 0.0)
    rank = run_ref[...] + _dot(onehot.astype(BF16), tri_ref[...])
    run_ref[...] += jnp.sum(onehot, axis=1, keepdims=True)
    cnt_ref[...] = run_ref[...]

    ids, rks, gks = [], [], []
    for pick, first in picks:
        ids.append(first.astype(I32))
        rks.append(jnp.sum(jnp.where(pick, rank, 0.0), axis=0, keepdims=True).astype(I32))
        gks.append(jnp.sum(jnp.where(pick, gates, 0.0), axis=0, keepdims=True))
    meta_ref[...] = jnp.concatenate(ids + rks, axis=0)
    gate_ref[...] = jnp.concatenate(gks, axis=0)


def _outproj(attn, four, rec, x, mod_l, w_out, layer, g1, b1, rw, rb, lay, tm):
    n_tiles = lay.n // tm
    n_ctx_tiles = lay.n_ctx // tm
    row = lambda i: (i, 0)
    const = lambda i: (0, 0)
    xs = x if isinstance(x, tuple) else (x,)
    return pl.pallas_call(
        functools.partial(_outproj_kernel, n_ctx_tiles=n_ctx_tiles, n_mix=len(attn)),
        grid=(n_tiles,),
        in_specs=[
            *_group_specs(len(attn), tm, ATTN_WIDTH, n_ctx_tiles),
            *_group_specs(len(four), tm, FOURIER_WIDTH, n_ctx_tiles),
            *_group_specs(len(rec), tm, HGRN_WIDTH, n_ctx_tiles),
            *_group_specs(len(xs), tm, D_MODEL, n_ctx_tiles),
            pl.BlockSpec((1, N_MOD, D_MODEL), lambda i: (lay.cond_row(i, tm), 0, 0)),
            pl.BlockSpec((1, D_MODEL, D_MODEL), lambda i: (layer, 0, 0)),
            pl.BlockSpec((1, D_MODEL), const),
            pl.BlockSpec((1, D_MODEL), const),
            pl.BlockSpec((N_EXPERTS, D_MODEL), const),
            pl.BlockSpec((N_EXPERTS, 1), const),
        ],
        out_specs=[
            pl.BlockSpec((tm, D_MODEL), row),
            pl.BlockSpec((tm, HALF_D), row),
            pl.BlockSpec((2 * TOP_K, tm), lambda i: (0, i)),
            pl.BlockSpec((TOP_K, tm), lambda i: (0, i)),
            pl.BlockSpec((N_EXPERTS, 1), const),
        ],
        out_shape=[
            jax.ShapeDtypeStruct((lay.n, D_MODEL), F32),
            jax.ShapeDtypeStruct((lay.n, HALF_D), I32),
            jax.ShapeDtypeStruct((2 * TOP_K, lay.n), I32),
            jax.ShapeDtypeStruct((TOP_K, lay.n), F32),
            jax.ShapeDtypeStruct((N_EXPERTS, 1), F32),
        ],
        scratch_shapes=[
            pltpu.VMEM((D_MODEL, D_MODEL), BF16),
            pltpu.VMEM((tm, tm), BF16),
            pltpu.VMEM((N_EXPERTS, 1), F32),
        ],
        compiler_params=_cparams(("arbitrary",)),
        name="outproj_router" + lay.tag,
    )(*attn, *four, *rec, *xs, mod_l, w_out, g1, b1, rw, rb)


def _sc_workers():
    info = plsc.get_sparse_core_info()
    return info.num_cores, info.num_cores * info.num_subcores


def _sc_scatter_rows(rows, pos_b, r_out, tag=""):
    nc, nw = _sc_workers()
    n, w = rows.shape
    nbt, copies, _ = pos_b.shape
    assert nbt * SC_BATCH == n and nbt % (2 * nw) == 0
    per_w = nbt // nw
    mesh = plsc.VectorSubcoreMesh(core_axis_name="c", subcore_axis_name="s")

    @functools.partial(
        pl.kernel, mesh=mesh, out_type=jax.ShapeDtypeStruct((r_out, w), rows.dtype),
        scratch_types=[pltpu.VMEM((copies, SC_BATCH), I32), pltpu.VMEM((copies, SC_BATCH), I32),
                       pltpu.VMEM((SC_BATCH, w), rows.dtype), pltpu.VMEM((SC_BATCH, w), rows.dtype),
                       pltpu.SemaphoreType.DMA, pltpu.SemaphoreType.DMA,
                       pltpu.SemaphoreType.DMA, pltpu.SemaphoreType.DMA],
        name="sc_dispatch" + tag)
    def k(rows_hbm, pos_hbm, out_hbm, idx_a, idx_b, rows_a, rows_b, sem_ra, sem_rb, sem_sa, sem_sb):
        wid = lax.axis_index("s") * nc + lax.axis_index("c")
        first = wid * per_w

        def reads(j, idx_v, rows_v, sem):
            bt = first + j
            return (pltpu.make_async_copy(pos_hbm.at[bt], idx_v, sem),
                    pltpu.make_async_copy(rows_hbm.at[pl.ds(bt * SC_BATCH, SC_BATCH)], rows_v, sem))

        def scatters(idx_v, rows_v, sem):
            return [pltpu.make_async_copy(rows_v, out_hbm.at[idx_v.at[q]], sem) for q in range(copies)]

        def start(descs):
            for d in descs:
                d.start()

        def wait(descs):
            for d in descs:
                d.wait()

        start(reads(0, idx_a, rows_a, sem_ra))

        @pl.loop(0, per_w // 2)
        def _(p):
            j0 = 2 * p
            j1 = j0 + 1

            @pl.when(p > 0)
            def _():
                wait(scatters(idx_b, rows_b, sem_sb))

            start(reads(j1, idx_b, rows_b, sem_rb))
            wait(reads(j0, idx_a, rows_a, sem_ra))
            start(scatters(idx_a, rows_a, sem_sa))
            wait(reads(j1, idx_b, rows_b, sem_rb))
            start(scatters(idx_b, rows_b, sem_sb))
            wait(scatters(idx_a, rows_a, sem_sa))

            @pl.when(p + 1 < per_w // 2)
            def _():
                start(reads(j0 + 2, idx_a, rows_a, sem_ra))

        wait(scatters(idx_b, rows_b, sem_sb))

    return k(rows, pos_b)


def _sc_gather_rows(table, idx, tag=""):
    nc, nw = _sc_workers()
    r = idx.shape[0]
    w = table.shape[1]
    assert r % (2 * nw * SC_BATCH) == 0
    per_w = r // nw
    nb = per_w // SC_BATCH
    mesh = plsc.VectorSubcoreMesh(core_axis_name="c", subcore_axis_name="s")

    @functools.partial(
        pl.kernel, mesh=mesh, out_type=jax.ShapeDtypeStruct((r, w), table.dtype),
        scratch_types=[pltpu.VMEM((per_w,), I32),
                       pltpu.VMEM((SC_BATCH, w), table.dtype), pltpu.VMEM((SC_BATCH, w), table.dtype),
                       pltpu.SemaphoreType.DMA, pltpu.SemaphoreType.DMA,
                       pltpu.SemaphoreType.DMA, pltpu.SemaphoreType.DMA],
        name="sc_combine" + tag)
    def k(table_hbm, idx_hbm, out_hbm, idx_v, rows_a, rows_b, sem_ga, sem_gb, sem_wa, sem_wb):
        wid = lax.axis_index("s") * nc + lax.axis_index("c")
        base = wid * per_w
        pltpu.sync_copy(idx_hbm.at[pl.ds(base, per_w)], idx_v)

        def gather(j, rows_v, sem):
            return pltpu.make_async_copy(table_hbm.at[idx_v.at[pl.ds(j * SC_BATCH, SC_BATCH)]], rows_v, sem)

        def write(j, rows_v, sem):
            return pltpu.make_async_copy(rows_v, out_hbm.at[pl.ds(base + j * SC_BATCH, SC_BATCH)], sem)

        gather(0, rows_a, sem_ga).start()

        @pl.loop(0, nb // 2)
        def _(p):
            j0 = 2 * p
            j1 = j0 + 1

            @pl.when(p > 0)
            def _():
                write(j1 - 2, rows_b, sem_wb).wait()

            gather(j1, rows_b, sem_gb).start()
            gather(j0, rows_a, sem_ga).wait()
            write(j0, rows_a, sem_wa).start()
            gather(j1, rows_b, sem_gb).wait()
            write(j1, rows_b, sem_wb).start()
            write(j0, rows_a, sem_wa).wait()

            @pl.when(p + 1 < nb // 2)
            def _():
                gather(j0 + 2, rows_a, sem_ga).start()

        write(nb - 1, rows_b, sem_wb).wait()

    return k(table, idx)


def _experts_kernel(te_ref, na_ref, x_ref, w1_ref, w3_ref, w2_ref, o_ref, w1b_ref, w3b_ref, w2b_ref):
    del te_ref

    @pl.when(pl.program_id(0) < na_ref[0])
    def _():
        w1b_ref[...] = w1_ref[0, 0].astype(BF16)
        w3b_ref[...] = w3_ref[0, 0].astype(BF16)
        w2b_ref[...] = w2_ref[0, 0].astype(BF16)
        lo, hi = _unpack_bf16_pair(x_ref[...])
        lo = lo.astype(BF16)
        hi = hi.astype(BF16)
        a = _dot(lo, w1b_ref[0:HALF_D, :]) + _dot(hi, w1b_ref[HALF_D:, :])
        b = _dot(lo, w3b_ref[0:HALF_D, :]) + _dot(hi, w3b_ref[HALF_D:, :])
        y = _dot((_silu(a) * b).astype(BF16), w2b_ref[...])
        o_ref[...] = _pack_bf16_pair(y[:, :HALF_D], y[:, HALF_D:])


def _experts(xs, tile_expert, n_active, w1, w3, w2, layer, tm, tag):
    r = xs.shape[0]
    n_tiles = r // tm

    def xmap(j, te, na):
        return (jnp.minimum(j, na[0] - 1), 0)

    def wmap(j, te, na):
        return (layer, te[jnp.minimum(j, na[0] - 1)], 0, 0)

    grid_spec = pltpu.PrefetchScalarGridSpec(
        num_scalar_prefetch=2,
        grid=(n_tiles,),
        in_specs=[
            pl.BlockSpec((tm, HALF_D), xmap),
            pl.BlockSpec((1, 1, D_MODEL, EXPERT_FF), wmap),
            pl.BlockSpec((1, 1, D_MODEL, EXPERT_FF), wmap),
            pl.BlockSpec((1, 1, EXPERT_FF, D_MODEL), wmap),
        ],
        out_specs=pl.BlockSpec((tm, HALF_D), xmap),
        scratch_shapes=[
            pltpu.VMEM((D_MODEL, EXPERT_FF), BF16),
            pltpu.VMEM((D_MODEL, EXPERT_FF), BF16),
            pltpu.VMEM((EXPERT_FF, D_MODEL), BF16),
        ],
    )
    return pl.pallas_call(
        _experts_kernel,
        grid_spec=grid_spec,
        out_shape=jax.ShapeDtypeStruct((r, HALF_D), I32),
        compiler_params=_cparams(("arbitrary",)),
        name="experts" + tag,
    )(tile_expert, n_active, xs, w1, w3, w2)


def _combine_kernel(yp_ref, gate_ref, hp_ref, sw1_ref, sw3_ref, sw2_ref, x_ref, mod_ref, g_ref, b_ref, *refs,
                    n_ctx_tiles):
    outs = refs[:-3]
    w1b_ref, w3b_ref, w2b_ref = refs[-3:]

    @pl.when(pl.program_id(0) == 0)
    def _():
        w1b_ref[...] = sw1_ref[...].astype(BF16)
        w3b_ref[...] = sw3_ref[...].astype(BF16)
        w2b_ref[...] = sw2_ref[...].astype(BF16)

    lo, hi = _unpack_bf16_pair(hp_ref[...])
    lo = lo.astype(BF16)
    hi = hi.astype(BF16)
    a = _dot(lo, w1b_ref[0:HALF_D, :]) + _dot(hi, w1b_ref[HALF_D:, :])
    b = _dot(lo, w3b_ref[0:HALF_D, :]) + _dot(hi, w3b_ref[HALF_D:, :])
    shared = _dot((_silu(a) * b).astype(BF16), w2b_ref[...])
    acc_lo = shared[:, :HALF_D]
    acc_hi = shared[:, HALF_D:]
    gates = gate_ref[...]
    for k in range(TOP_K):
        ylo, yhi = _unpack_bf16_pair(yp_ref[k])
        gk = gates[:, k:k + 1]
        acc_lo = acc_lo + gk * ylo
        acc_hi = acc_hi + gk * yhi
    moe = jnp.concatenate([acc_lo, acc_hi], axis=1)
    y = DEEPNORM_ALPHA * x_ref[...] + mod_ref[0, 5:6, :] * moe
    res = _ln_plain(y, LN_EPS) * g_ref[...] + b_ref[...]
    if len(outs) == 1:
        outs[0][...] = res
    else:
        @pl.when(pl.program_id(0) < n_ctx_tiles)
        def _():
            outs[0][...] = res

        @pl.when(pl.program_id(0) >= n_ctx_tiles)
        def _():
            outs[1][...] = res


def _combine(yp, gate8, hp, sw1, sw3, sw2, x1, mod_l, g2, b2, lay, tm, split_out):
    n_tiles = lay.n // tm
    n_ctx_tiles = lay.n_ctx // tm
    row = lambda i: (i, 0)
    const = lambda i: (0, 0)
    if split_out:
        out_specs = _group_specs(2, tm, D_MODEL, n_ctx_tiles)
        out_shape = [jax.ShapeDtypeStruct((lay.n_ctx, D_MODEL), F32), jax.ShapeDtypeStruct((lay.n_lat, D_MODEL), F32)]
    else:
        out_specs = pl.BlockSpec((tm, D_MODEL), row)
        out_shape = jax.ShapeDtypeStruct((lay.n, D_MODEL), F32)
    return pl.pallas_call(
        functools.partial(_combine_kernel, n_ctx_tiles=n_ctx_tiles),
        grid=(n_tiles,),
        in_specs=[
            pl.BlockSpec((TOP_K, tm, HALF_D), lambda i: (0, i, 0)),
            pl.BlockSpec((tm, TOP_K), row),
            pl.BlockSpec((tm, HALF_D), row),
            pl.BlockSpec((D_MODEL, EXPERT_FF), const),
            pl.BlockSpec((D_MODEL, EXPERT_FF), const),
            pl.BlockSpec((EXPERT_FF, D_MODEL), const),
            pl.BlockSpec((tm, D_MODEL), row),
            pl.BlockSpec((1, N_MOD, D_MODEL), lambda i: (lay.cond_row(i, tm), 0, 0)),
            pl.BlockSpec((1, D_MODEL), const),
            pl.BlockSpec((1, D_MODEL), const),
        ],
        out_specs=out_specs,
        out_shape=out_shape,
        scratch_shapes=[
            pltpu.VMEM((D_MODEL, EXPERT_FF), BF16),
            pltpu.VMEM((D_MODEL, EXPERT_FF), BF16),
            pltpu.VMEM((EXPERT_FF, D_MODEL), BF16),
        ],
        compiler_params=_cparams(("arbitrary",)),
        name="combine_norm" + lay.tag,
    )(yp, gate8, hp, sw1, sw3, sw2, x1, mod_l, g2, b2)


def _moe_dispatch(hp, meta, counts, lay, tile):
    n = lay.n
    r_max = n * TOP_K + N_EXPERTS * tile
    n_tiles = r_max // tile
    cnt = counts.reshape(N_EXPERTS).astype(I32)
    padded = ((cnt + tile - 1) // tile) * tile
    ends = jnp.cumsum(padded)
    offsets = ends - padded
    idx8 = meta[:TOP_K]
    base8 = jnp.sum(jnp.where(idx8[:, :, None] == jnp.arange(N_EXPERTS, dtype=I32), offsets, 0), axis=-1)
    pos = (base8 + meta[TOP_K:]).astype(I32)
    tile_start = jnp.arange(n_tiles, dtype=I32) * tile
    tile_expert = jnp.minimum(jnp.sum(tile_start[:, None] >= ends[None, :], axis=1), N_EXPERTS - 1).astype(I32)
    n_active = (ends[-1] // tile).astype(I32).reshape(1)
    pos_b = pos.reshape(TOP_K, n // SC_BATCH, SC_BATCH).transpose(1, 0, 2)
    xs = _sc_scatter_rows(hp, pos_b, r_max, lay.tag)
    return xs, tile_expert, n_active, pos


def _moe_combine(ys, pos, gate8, hp, sw1, sw3, sw2, x1, mod_l, g2, b2, lay, split_out):
    n = lay.n
    yp = _sc_gather_rows(ys, pos.reshape(n * TOP_K), lay.tag).reshape(TOP_K, n, HALF_D)
    return _combine(yp, gate8.T, hp, sw1, sw3, sw2, x1, mod_l, g2, b2, lay, TOKEN_TILE, split_out)


def kernel(x_prompt, x_sample, cache_k, cache_v, state_hgrn, c, c_ctx, w_ada, b_ada, w_in, w_out, attn_sink, hgrn_lb, hgrn_norm, ln1_g, ln1_b, ln2_g, ln2_b, router_w, router_b, moe_w1, moe_w3, moe_w2, shared_w1, shared_w3, shared_w2):
    b_ctx, t_ctx, _ = x_prompt.shape
    b_lat, t_lat, _ = x_sample.shape
    past = cache_k.shape[2]
    tm = TOKEN_TILE
    assert 1 + b_lat <= COND_ROWS
    lay = _Layout(b_ctx, t_ctx, b_lat, t_lat)
    assert lay.n_ctx % tm == 0 and t_lat % tm == 0 and lay.n_ctx % t_lat == 0

    cond = jnp.concatenate([c_ctx[None, :], c, jnp.zeros((COND_ROWS - 1 - b_lat, D_MODEL), F32)], axis=0)
    mod = _adaln(cond, w_ada, b_ada).reshape(DEPTH, COND_ROWS, N_MOD, D_MODEL)

    lb_all = jnp.cumsum(jax.nn.softmax(hgrn_lb.astype(F32), axis=0), axis=0)
    lb_all = lb_all - lb_all[:1]
    lbp = jnp.stack([jnp.log(lb_all), jnp.log1p(-lb_all), 1.0 - lb_all], axis=2)

    cos_t, sin_t = _rope_tables(lay, tm)
    zero_state = jnp.zeros((b_ctx, 2, HGRN_HEADS, HGRN_DK, HGRN_DK), F32)

    def layer(l, lay, x, split_out):
        lat = slice(lay.lat_first, lay.lat_first + lay.b_lat)
        outs = _inproj(x, mod[l], w_in, l, cos_t, sin_t, lay, tm)
        proj, proj_h = outs[0], outs[1]
        sink_l = attn_sink[l].reshape(1, N_HEADS)
        gn_row = jnp.tile(hgrn_norm[l], HGRN_HEADS).reshape(1, HGRN_WIDTH)
        attn, four, rec, extras = [], [], [], None
        if lay.b_ctx:
            attn.append(_attn_context(proj, sink_l, lay))
            four.append(_fourier(proj, 0, lay.b_ctx, t_ctx, t_ctx, "fourier_ctx"))
            rec_c, s_fin = _hgrn(proj_h, 0, lay.b_ctx, t_ctx, lbp[l], gn_row, zero_state, "hgrn_ctx")
            rec.append(rec_c)
            extras = (outs[2], outs[3], s_fin)
        attn.append(_attn_latent(proj, cache_k[lat, l].reshape(lay.b_lat, past, KV_WIDTH),
                                 cache_v[lat, l].reshape(lay.b_lat, past, KV_WIDTH), sink_l, lay))
        four.append(_fourier(proj, lay.n_ctx, lay.b_lat, t_lat, min(t_lat, 512), "fourier_lat" + lay.tag))
        s0t = jnp.swapaxes(state_hgrn[lat, l].astype(F32), -1, -2)
        rec.append(_hgrn(proj_h, lay.n_ctx, lay.b_lat, t_lat, lbp[l], gn_row, s0t, "hgrn_lat" + lay.tag)[0])
        x1, hp, meta, gate8, counts = _outproj(
            tuple(attn), tuple(four), tuple(rec), x, mod[l], w_out, l, ln1_g[l].reshape(1, -1),
            ln1_b[l].reshape(1, -1), router_w[l].T, router_b[l].reshape(-1, 1), lay, tm)
        xs, tile_expert, n_active, pos = _moe_dispatch(hp, meta, counts, lay, EXPERT_TILE)
        ys = _experts(xs, tile_expert, n_active, moe_w1, moe_w3, moe_w2, l, EXPERT_TILE, lay.tag)
        x = _moe_combine(ys, pos, gate8, hp, shared_w1[l], shared_w3[l], shared_w2[l], x1, mod[l],
                         ln2_g[l].reshape(1, -1), ln2_b[l].reshape(1, -1), lay, split_out)
        return x, extras

    x = (x_prompt.reshape(lay.n_ctx, D_MODEL), x_sample.reshape(lay.n_lat, D_MODEL))
    ks_out, vs_out, ss_out = [], [], []
    for l in range(DEPTH):
        x, (k_new, v_new, s_fin) = layer(l, lay, x, split_out=(l == DEPTH - 1))
        ks_out.append(k_new.reshape(b_ctx, t_ctx, N_KV_HEADS, HEAD_DIM))
        vs_out.append(v_new.reshape(b_ctx, t_ctx, N_KV_HEADS, HEAD_DIM))
        ss_out.append(jnp.swapaxes(s_fin, -1, -2))

    y_prompt = x[0].reshape(b_ctx, t_ctx, D_MODEL)
    y_sample = x[1].reshape(b_lat, t_lat, D_MODEL)
    new_cache_k = jnp.stack(ks_out, axis=1)
    new_cache_v = jnp.stack(vs_out, axis=1)
    new_state = jnp.stack(ss_out, axis=1).astype(x_prompt.dtype)
    return (y_prompt, y_sample, new_cache_k, new_cache_v, new_state)
```

```python
import functools
import math

import numpy as np
import jax
import jax.numpy as jnp
from jax import lax
from jax.experimental import pallas as pl
from jax.experimental.pallas import tpu as pltpu
from jax.experimental.pallas import tpu_sc as plsc

F32 = jnp.float32
BF16 = jnp.bfloat16
I32 = jnp.int32

D_MODEL = 1024
HALF_D = D_MODEL // 2
DEPTH = 2
GRID_W = 64
ROPE_BASE = 10000.0
HEAD_DIM = 64
ATTN_WIDTH = 512
N_HEADS = 8
N_KV_HEADS = 2
KV_GROUP = 4
KV_WIDTH = N_KV_HEADS * HEAD_DIM
WINDOW = 128
ATTN_BLOCK = 128
FOURIER_WIDTH = 256
FOURIER_GROUPS = 4
HGRN_WIDTH = 256
HGRN_HEADS = 4
HGRN_DK = 64
HGRN_CHUNK = 64
IN_WIDTH = 2304
N_EXPERTS = 64
TOP_K = 8
EXPERT_FF = 256
ROUTED_SCALE = 2.5
N_MOD = 6
LN_EPS = 1e-5
ADA_EPS = 1e-6
GN_EPS = 1e-6
DEEPNORM_ALPHA = (2 * DEPTH) ** 0.25

COL_Q = 0
COL_K = 512
COL_V = 640
COL_U = 768
COL_HQ = 1024
COL_FF = 1280
COL_FB = 1536
COL_HI = 1792
COL_HG = 2048
ROPE_COLS = COL_V
MIX_IN_WIDTH = COL_HQ

V7X_LANES = 128
COND_ROWS = 16
NEG_BIG = -1e30
TOKEN_TILE = 512
EXPERT_TILE = 1024
SC_BATCH = 64

VMEM_LIMIT = 56 * 1024 * 1024


def _cparams(sem):
    return pltpu.CompilerParams(dimension_semantics=sem, vmem_limit_bytes=VMEM_LIMIT)


def _dot(a, b):
    return jnp.dot(a, b, preferred_element_type=F32)


def _dot_nt(a, b):
    return lax.dot_general(a, b, (((1,), (1,)), ((), ())), preferred_element_type=F32)


def _dot_tn(a, b):
    return lax.dot_general(a, b, (((0,), (0,)), ((), ())), preferred_element_type=F32)


def _split3(x):
    hi = x.astype(BF16)
    r1 = x - hi.astype(F32)
    mid = r1.astype(BF16)
    lo = (r1 - mid.astype(F32)).astype(BF16)
    return hi, mid, lo


def _dot_exact_lhs(m_bf16, x):
    hi, mid, lo = _split3(x)
    return _dot(m_bf16, hi) + _dot(m_bf16, mid) + _dot(m_bf16, lo)


def _dot_exact_rhs(x, m_bf16):
    hi, mid, lo = _split3(x)
    return _dot(hi, m_bf16) + _dot(mid, m_bf16) + _dot(lo, m_bf16)


def _dot_hp(a, b):
    a_hi = a.astype(BF16)
    a_lo = (a - a_hi.astype(F32)).astype(BF16)
    b_hi = b.astype(BF16)
    b_lo = (b - b_hi.astype(F32)).astype(BF16)
    return _dot(a_hi, b_hi) + _dot(a_hi, b_lo) + _dot(a_lo, b_hi)


def _pack_bf16_pair(lo, hi):
    return lax.bitcast_convert_type(pltpu.pack_elementwise([lo, hi], packed_dtype=BF16), I32)


def _unpack_bf16_pair(w):
    u = lax.bitcast_convert_type(w, jnp.uint32)
    lo = pltpu.unpack_elementwise(u, index=0, packed_dtype=BF16, unpacked_dtype=F32)
    hi = pltpu.unpack_elementwise(u, index=1, packed_dtype=BF16, unpacked_dtype=F32)
    return lo, hi


def _ln_plain(x, eps):
    mu = jnp.mean(x, axis=-1, keepdims=True)
    xc = x - mu
    var = jnp.mean(xc * xc, axis=-1, keepdims=True)
    return xc * lax.rsqrt(var + eps)


def _silu(x):
    return x * jax.nn.sigmoid(x)


def _adaln_kernel(c_ref, w_ref, b_ref, o_ref):
    s = _silu(c_ref[...])
    o_ref[0] = _dot_hp(s, w_ref[0]) + b_ref[0]


def _adaln(cond, w_ada, b_ada):
    return pl.pallas_call(
        _adaln_kernel,
        grid=(DEPTH, N_MOD),
        in_specs=[
            pl.BlockSpec((COND_ROWS, D_MODEL), lambda l, j: (0, 0)),
            pl.BlockSpec((1, D_MODEL, D_MODEL), lambda l, j: (l, 0, j)),
            pl.BlockSpec((1, 1, D_MODEL), lambda l, j: (l, 0, j)),
        ],
        out_specs=pl.BlockSpec((1, COND_ROWS, D_MODEL), lambda l, j: (l, 0, j)),
        out_shape=jax.ShapeDtypeStruct((DEPTH, COND_ROWS, N_MOD * D_MODEL), F32),
        compiler_params=_cparams(("arbitrary", "arbitrary")),
        name="adaln",
    )(cond, w_ada, b_ada.reshape(DEPTH, 1, N_MOD * D_MODEL))


class _Layout:
    def __init__(self, b_ctx, t_ctx, b_lat, t_lat, lat_first=0, tag=""):
        self.b_ctx, self.t_ctx, self.b_lat, self.t_lat = b_ctx, t_ctx, b_lat, t_lat
        self.lat_first = lat_first
        self.n_ctx = b_ctx * t_ctx
        self.n_lat = b_lat * t_lat
        self.n = self.n_ctx + self.n_lat
        self.tag = tag

    def cond_row(self, tile, tm):
        n_ctx_tiles = self.n_ctx // tm
        per_batch = self.t_lat // tm
        return jnp.where(tile < n_ctx_tiles, 0, 1 + self.lat_first + (tile - n_ctx_tiles) // per_batch)


def _group_specs(n_arrays, tm, width, n_ctx_tiles):
    if n_arrays == 1:
        return [pl.BlockSpec((tm, width), lambda i: (i, 0))]
    return [pl.BlockSpec((tm, width), lambda i: (jnp.minimum(i, n_ctx_tiles - 1), 0)),
            pl.BlockSpec((tm, width), lambda i: (jnp.maximum(i - n_ctx_tiles, 0), 0))]


def _inproj_kernel(*refs, n_ctx_tiles):
    n_tail = 9 if n_ctx_tiles > 0 else 7
    xs = refs[:-n_tail]
    mod_ref, w_ref, cos_ref, sin_ref, oa_ref, oh_ref = refs[-n_tail:-n_tail + 6]
    wb_ref = refs[-1]

    @pl.when(pl.program_id(0) == 0)
    def _():
        wb_ref[...] = w_ref[0].astype(BF16)

    if len(xs) == 2:
        x = jnp.where(pl.program_id(0) < n_ctx_tiles, xs[0][...], xs[1][...])
    else:
        x = xs[0][...]
    shift = mod_ref[0, 0:1, :]
    scale = mod_ref[0, 1:2, :]
    h = (_ln_plain(x, ADA_EPS) * (1.0 + scale) + shift).astype(BF16)
    p = _dot(h, wb_ref[...])
    cos = cos_ref[...]
    sin = sin_ref[...]
    lane = lax.broadcasted_iota(I32, cos.shape, 1)
    first_half = (lane & 31) < 16
    for cb in range(ROPE_COLS // V7X_LANES):
        seg = p[:, cb * V7X_LANES:(cb + 1) * V7X_LANES]
        partner = jnp.where(first_half, pltpu.roll(seg, V7X_LANES - 16, 1), pltpu.roll(seg, 16, 1))
        oa_ref[:, cb * V7X_LANES:(cb + 1) * V7X_LANES] = (seg * cos + partner * sin).astype(BF16)
    oa_ref[:, ROPE_COLS:] = p[:, ROPE_COLS:MIX_IN_WIDTH].astype(BF16)
    oh_ref[...] = p[:, MIX_IN_WIDTH:]

    if n_ctx_tiles > 0:
        kc_ref, vc_ref = refs[-3], refs[-2]

        @pl.when(pl.program_id(0) < n_ctx_tiles)
        def _():
            kc_ref[...] = p[:, COL_K:COL_K + KV_WIDTH]
            vc_ref[...] = p[:, COL_V:COL_V + KV_WIDTH]


def _rope_tables(lay, tm):
    t = lay.t_lat
    pos = jnp.arange(t)
    row = (pos // GRID_W).astype(F32)
    col = (pos % GRID_W).astype(F32)
    n_freq = HEAD_DIM // 4
    inv = ROPE_BASE ** (-jnp.arange(n_freq, dtype=F32) / n_freq)
    ang_r = row[:, None] * inv
    ang_c = col[:, None] * inv
    ang = jnp.concatenate([ang_r, ang_r, ang_c, ang_c], axis=1)
    sign = jnp.concatenate([-jnp.ones(n_freq), jnp.ones(n_freq), -jnp.ones(n_freq), jnp.ones(n_freq)]).astype(F32)
    cos = jnp.cos(ang)
    sin = jnp.sin(ang) * sign
    cos = jnp.concatenate([jnp.ones((tm, HEAD_DIM), F32), cos], axis=0)
    sin = jnp.concatenate([jnp.zeros((tm, HEAD_DIM), F32), sin], axis=0)
    return jnp.tile(cos, (1, 2)), jnp.tile(sin, (1, 2))


def _inproj(x, mod_l, w_in, layer, cos_t, sin_t, lay, tm):
    n_tiles = lay.n // tm
    n_ctx_tiles = lay.n_ctx // tm
    per_batch = lay.t_lat // tm

    def tbl(i):
        return jnp.where(i < n_ctx_tiles, 0, 1 + (i - n_ctx_tiles) % per_batch)

    xs = x if isinstance(x, tuple) else (x,)
    kv_specs, kv_shapes = [], []
    if n_ctx_tiles > 0:
        kv_specs = [pl.BlockSpec((tm, KV_WIDTH), lambda i: (jnp.minimum(i, n_ctx_tiles - 1), 0))] * 2
        kv_shapes = [jax.ShapeDtypeStruct((lay.n_ctx, KV_WIDTH), F32)] * 2
    return pl.pallas_call(
        functools.partial(_inproj_kernel, n_ctx_tiles=n_ctx_tiles),
        grid=(n_tiles,),
        in_specs=[
            *_group_specs(len(xs), tm, D_MODEL, n_ctx_tiles),
            pl.BlockSpec((1, N_MOD, D_MODEL), lambda i: (lay.cond_row(i, tm), 0, 0)),
            pl.BlockSpec((1, D_MODEL, IN_WIDTH), lambda i: (layer, 0, 0), pipeline_mode=pl.Buffered(1)),
            pl.BlockSpec((tm, V7X_LANES), lambda i: (tbl(i), 0)),
            pl.BlockSpec((tm, V7X_LANES), lambda i: (tbl(i), 0)),
        ],
        out_specs=[pl.BlockSpec((tm, MIX_IN_WIDTH), lambda i: (i, 0)),
                   pl.BlockSpec((tm, IN_WIDTH - MIX_IN_WIDTH), lambda i: (i, 0))] + kv_specs,
        out_shape=[jax.ShapeDtypeStruct((lay.n, MIX_IN_WIDTH), BF16),
                   jax.ShapeDtypeStruct((lay.n, IN_WIDTH - MIX_IN_WIDTH), F32)] + kv_shapes,
        scratch_shapes=[pltpu.VMEM((D_MODEL, IN_WIDTH), BF16)],
        compiler_params=_cparams(("arbitrary",)),
        name="inproj" + lay.tag,
    )(*xs, mod_l, w_in, cos_t, sin_t)


def _attn_kernel(sink_ref, q_ref, *refs, n_local, has_ctx, t_total):
    o_ref = refs[-1]
    k_refs = refs[:n_local]
    v_refs = refs[n_local:2 * n_local]
    tq = q_ref.shape[0]
    scale = HEAD_DIM ** -0.5
    k_parts = [kr[...] for kr in k_refs]
    v_parts = [vr[...] for vr in v_refs]
    if has_ctx:
        k_parts.append(refs[2 * n_local][0].astype(k_parts[0].dtype))
        v_parts.append(refs[2 * n_local + 1][0].astype(v_parts[0].dtype))
    kall = (jnp.concatenate(k_parts, axis=0) if len(k_parts) > 1 else k_parts[0]).astype(F32)
    vall = (jnp.concatenate(v_parts, axis=0) if len(v_parts) > 1 else v_parts[0]).astype(F32)
    nk = kall.shape[0]
    k_sw = pltpu.roll(kall, HEAD_DIM, 1)
    v_sw = pltpu.roll(vall, HEAD_DIM, 1)
    lo_half = lax.broadcasted_iota(I32, (1, V7X_LANES), 1) < HEAD_DIM
    er = jnp.where(lax.broadcasted_iota(I32, (2 * nk, V7X_LANES), 0) < nk, 0, 1)
    el = jnp.where(lax.broadcasted_iota(I32, (2 * nk, V7X_LANES), 1) < HEAD_DIM, 0, 1)
    ones_blk = jnp.where(er == el, 1.0, 0.0).astype(BF16)
    if n_local > 1:
        i = pl.program_id(1)
        band = refs[-2][...]
        first_blk = jnp.where(i == 0, NEG_BIG, 0.0)
        last_blk = jnp.where(i == t_total // tq - 1, NEG_BIG, 0.0)

        def mask_local(sc):
            loc = sc[:, :n_local * tq] + band
            parts = [loc[:, :tq] + first_blk, loc[:, tq:(n_local - 1) * tq], loc[:, (n_local - 1) * tq:] + last_blk]
            return jnp.concatenate(parts + [sc[:, n_local * tq:]], axis=1)
    else:
        mask_local = None
    v2es, sinks, scores = [], [], []
    for g in range(N_KV_HEADS):
        k_own, k_oth = (kall, k_sw) if g == 0 else (k_sw, kall)
        v_own, v_oth = (vall, v_sw) if g == 0 else (v_sw, vall)
        k2 = jnp.concatenate([jnp.where(lo_half, k_own, 0.0), jnp.where(lo_half, 0.0, k_oth)], axis=0).astype(BF16)
        v2 = jnp.concatenate([jnp.where(lo_half, v_own, 0.0), jnp.where(lo_half, 0.0, v_oth)], axis=0).astype(BF16)
        v2es.append(jnp.concatenate([v2, ones_blk], axis=1))
        pairs = [2 * g, 2 * g + 1]
        qq = jnp.concatenate([q_ref[:, p * V7X_LANES:(p + 1) * V7X_LANES] for p in pairs], axis=0)
        qq = (qq.astype(F32) * scale).astype(BF16)
        sinks.append((jnp.concatenate([jnp.full((tq, 1), sink_ref[0, 2 * p], F32) for p in pairs], axis=0),
                      jnp.concatenate([jnp.full((tq, 1), sink_ref[0, 2 * p + 1], F32) for p in pairs], axis=0)))
        scores.append(_dot_nt(qq, k2))
    pes, maxes = [], []
    for g in range(N_KV_HEADS):
        s_a = scores[g][:, :nk]
        s_b = scores[g][:, nk:]
        if mask_local is not None:
            s_a = mask_local(s_a)
            s_b = mask_local(s_b)
        m_a = jnp.maximum(jnp.max(s_a, axis=1, keepdims=True), sinks[g][0])
        m_b = jnp.maximum(jnp.max(s_b, axis=1, keepdims=True), sinks[g][1])
        pes.append(jnp.concatenate([jnp.exp(s_a - m_a).astype(BF16), jnp.exp(s_b - m_b).astype(BF16)], axis=1))
        maxes.append((m_a, m_b))
    for g in range(N_KV_HEADS):
        acc = _dot(pes[g], v2es[g])
        (m_a, m_b), (sink_a, sink_b) = maxes[g], sinks[g]
        sink_term = jnp.where(lo_half, jnp.exp(sink_a - m_a), jnp.exp(sink_b - m_b))
        o = acc[:, :V7X_LANES] / (acc[:, V7X_LANES:] + sink_term)
        for j, p in enumerate([2 * g, 2 * g + 1]):
            o_ref[:, p * V7X_LANES:(p + 1) * V7X_LANES] = o[j * tq:(j + 1) * tq].astype(o_ref.dtype)


def _attn_context(proj, sink_l, lay):
    t = lay.t_ctx
    kb, vb = COL_K // KV_WIDTH, COL_V // KV_WIDTH
    body = functools.partial(_attn_kernel, n_local=1, has_ctx=False, t_total=t)
    return pl.pallas_call(
        body,
        grid=(lay.b_ctx,),
        in_specs=[
            pl.BlockSpec(memory_space=pltpu.SMEM),
            pl.BlockSpec((t, ATTN_WIDTH), lambda b: (b, 0)),
            pl.BlockSpec((t, KV_WIDTH), lambda b: (b, kb)),
            pl.BlockSpec((t, KV_WIDTH), lambda b: (b, vb)),
        ],
        out_specs=pl.BlockSpec((t, ATTN_WIDTH), lambda b: (b, 0)),
        out_shape=jax.ShapeDtypeStruct((lay.n_ctx, ATTN_WIDTH), BF16),
        compiler_params=_cparams(("arbitrary",)),
        name="attn_ctx",
    )(sink_l, proj, proj, proj)


def _attn_latent(proj, k_ctx, v_ctx, sink_l, lay):
    t = lay.t_lat
    tq = ATTN_BLOCK
    nq = t // tq
    base = lay.n_ctx // tq
    kb, vb = COL_K // KV_WIDTH, COL_V // KV_WIDTH
    past = k_ctx.shape[1]

    def rows(off):
        return lambda b, i: base + b * nq + jnp.clip(i + off, 0, nq - 1)

    def kv_specs(col):
        return [pl.BlockSpec((tq, KV_WIDTH), (lambda b, i, f=rows(off): (f(b, i), col))) for off in (-1, 0, 1)]

    body = functools.partial(_attn_kernel, n_local=3, has_ctx=True, t_total=t)
    rel = np.arange(3 * tq)[None, :] - tq - (np.arange(2 * tq)[:, None] % tq)
    band = jnp.asarray(np.where(np.abs(rel) <= WINDOW, 0.0, NEG_BIG).astype(np.float32))
    return pl.pallas_call(
        body,
        grid=(lay.b_lat, nq),
        in_specs=[
            pl.BlockSpec(memory_space=pltpu.SMEM),
            pl.BlockSpec((tq, ATTN_WIDTH), lambda b, i: (base + b * nq + i, 0)),
            *kv_specs(kb),
            *kv_specs(vb),
            pl.BlockSpec((1, past, KV_WIDTH), lambda b, i: (b, 0, 0)),
            pl.BlockSpec((1, past, KV_WIDTH), lambda b, i: (b, 0, 0)),
            pl.BlockSpec(band.shape, lambda b, i: (0, 0)),
        ],
        out_specs=pl.BlockSpec((tq, ATTN_WIDTH), lambda b, i: (b * nq + i, 0)),
        out_shape=jax.ShapeDtypeStruct((lay.n_lat, ATTN_WIDTH), BF16),
        compiler_params=_cparams(("arbitrary", "arbitrary")),
        name="attn_lat" + lay.tag,
    )(sink_l, proj, proj, proj, proj, proj, proj, proj, k_ctx, v_ctx, band)


def _fourier_kernel(cs_ref, u_ref, cc_ref, sc_ref, o_ref, csb_ref, *, scale):
    @pl.when(pl.program_id(1) == 0)
    def _():
        csb_ref[...] = cs_ref[...].astype(BF16)

    z = u_ref[...].astype(BF16)
    zc = _dot(z, cc_ref[...].astype(BF16)).astype(BF16)
    zs = _dot(z, sc_ref[...].astype(BF16)).astype(BF16)
    zz = jnp.concatenate([zc, zs], axis=0)
    o_ref[...] = (_dot(csb_ref[...], zz) * scale).astype(o_ref.dtype)


@functools.lru_cache(maxsize=None)
def _dft_tables(t):
    idx = np.arange(t, dtype=np.int64)
    ang = 2.0 * np.pi * ((idx[:, None] * idx[None, :]) % t).astype(np.float64) / t
    cs = np.concatenate([np.cos(ang), -np.sin(ang)], axis=1).astype(np.float32)
    cw = FOURIER_WIDTH // FOURIER_GROUPS
    cidx = np.arange(cw, dtype=np.int64)
    cang = 2.0 * np.pi * ((cidx[:, None] * cidx[None, :]) % cw).astype(np.float64) / cw
    eye = np.eye(FOURIER_GROUPS)
    cc = np.kron(eye, np.cos(cang)).astype(np.float32)
    sc = np.kron(eye, np.sin(cang)).astype(np.float32)
    return cs, cc, sc


def _fourier(proj, row0, b, t, tm, name):
    cs, cc, sc = _dft_tables(t)
    cw = FOURIER_WIDTH // FOURIER_GROUPS
    nt = t // tm
    ub = COL_U // FOURIER_WIDTH
    base = row0 // t
    body = functools.partial(_fourier_kernel, scale=1.0 / math.sqrt(t * cw))
    return pl.pallas_call(
        body,
        grid=(nt, b),
        in_specs=[
            pl.BlockSpec((tm, 2 * t), lambda i, bb: (i, 0)),
            pl.BlockSpec((t, FOURIER_WIDTH), lambda i, bb: (base + bb, ub)),
            pl.BlockSpec((FOURIER_WIDTH, FOURIER_WIDTH), lambda i, bb: (0, 0)),
            pl.BlockSpec((FOURIER_WIDTH, FOURIER_WIDTH), lambda i, bb: (0, 0)),
        ],
        out_specs=pl.BlockSpec((tm, FOURIER_WIDTH), lambda i, bb: (bb * nt + i, 0)),
        out_shape=jax.ShapeDtypeStruct((b * t, FOURIER_WIDTH), BF16),
        scratch_shapes=[pltpu.VMEM((tm, 2 * t), BF16)],
        compiler_params=_cparams(("arbitrary", "arbitrary")),
        name=name,
    )(jnp.asarray(cs), proj, jnp.asarray(cc), jnp.asarray(sc))


HGRN_LEVELS = (64, 32, 16, 8, 4, 2)
HGRN_SAFE_RANGE = 80.0


@functools.lru_cache(maxsize=None)
def _hgrn_tables():
    c = HGRN_CHUNK
    return np.stack([np.tril(np.ones((c, c))), np.triu(np.ones((c, c)))]).astype(np.float32)


def _boundary_rows(b, m, reverse):
    c, w = b.shape
    half = m // 2
    off = half if reverse else half - 1
    if m >= 16:
        return jnp.concatenate(
            [jnp.broadcast_to(b[s + off:s + off + 1], (m, w)) for s in range(0, c, m)], axis=0)
    sub = lax.broadcasted_iota(I32, (c, w), 0) & 7
    b3 = b.reshape(c // 8, 8, w)

    def bcast(j):
        return jnp.broadcast_to(b3[:, j:j + 1, :], (c // 8, 8, w)).reshape(c, w)

    if m == 8:
        return bcast(off)
    if m == 4:
        return jnp.where(sub < 4, bcast(off), bcast(4 + off))
    assert m == 2
    if reverse:
        return jnp.where((sub & 1) == 1, b, pltpu.roll(b, c - 1, 0))
    return jnp.where((sub & 1) == 0, b, pltpu.roll(b, 1, 0))


def _hgrn_gates(q, z, v, loglb, log1mlb, onemlb, cum, reverse):
    c = HGRN_CHUNK
    log_sig = jnp.minimum(z, 0.0) - jnp.log1p(jnp.exp(-jnp.abs(z)))
    bb = log1mlb + log_sig
    mx = jnp.maximum(loglb, bb)
    lf = mx + jnp.log1p(jnp.exp(-jnp.abs(loglb - bb)))
    kk = onemlb * jax.nn.sigmoid(-z)
    b = _dot_exact_lhs(cum, lf)
    b_end = b[0:1] if reverse else b[c - 1:c]
    qt = (q * jnp.exp(b)).astype(BF16)
    kt = (kk * jnp.exp(b_end - b)).astype(BF16)
    return (q, kk, b), (qt, kt, v.astype(BF16))


class _HgrnDir:
    def __init__(self, f32_parts, bf16_parts, reverse):
        c = HGRN_CHUNK
        self.q, self.kk, self.b = f32_parts
        self.qt, self.kt, self.vb = bf16_parts
        self.reverse = reverse
        b_end = self.b[0:1] if reverse else self.b[c - 1:c]
        self.decay = jnp.exp(b_end)
        mid = c // 2 if reverse else c // 2 - 1
        self.rel = self.b - self.b[mid:mid + 1]
        self.span = jnp.max(jnp.abs(self.rel))

    def tree_decay_matrices(self):
        c = HGRN_CHUNK
        q, kk, b = self.q, self.kk, self.b
        row = lax.broadcasted_iota(I32, (c, 1), 0)
        ti = lax.broadcasted_iota(I32, (c, c), 0)
        si = lax.broadcasted_iota(I32, (c, c), 1)
        qb = q.astype(BF16)
        kb = kk.astype(BF16)
        heads = [slice(h * HGRN_DK, (h + 1) * HGRN_DK) for h in range(HGRN_HEADS)]
        acc = [jnp.where(ti == si, _dot_nt(qb[:, sl], kb[:, sl]), 0.0) for sl in heads]
        for m in HGRN_LEVELS:
            r = _boundary_rows(b, m, self.reverse)
            upper = (row & (m - 1)) >= (m // 2)
            q_side = jnp.logical_not(upper) if self.reverse else upper
            e = jnp.exp(jnp.where(q_side, b - r, r - b))
            qf = jnp.where(q_side, q * e, 0.0).astype(BF16)
            kf = jnp.where(q_side, 0.0, kk * e).astype(BF16)
            same_block = (ti & -m) == (si & -m)
            for h, sl in enumerate(heads):
                acc[h] = acc[h] + jnp.where(same_block, _dot_nt(qf[:, sl], kf[:, sl]), 0.0)
        return acc

    def midpoint_decay_matrices(self):
        c = HGRN_CHUNK
        ti = lax.broadcasted_iota(I32, (c, c), 0)
        si = lax.broadcasted_iota(I32, (c, c), 1)
        qm = (self.q * jnp.exp(self.rel)).astype(BF16)
        km = (self.kk * jnp.exp(-self.rel)).astype(BF16)
        causal = (si >= ti) if self.reverse else (si <= ti)
        return [jnp.where(causal, _dot_nt(qm[:, h * HGRN_DK:(h + 1) * HGRN_DK], km[:, h * HGRN_DK:(h + 1) * HGRN_DK]), 0.0)
                for h in range(HGRN_HEADS)]

    def outputs(self, a_heads, st_ref, d):
        outs = []
        for h in range(HGRN_HEADS):
            sl = slice(h * HGRN_DK, (h + 1) * HGRN_DK)
            st = st_ref[d, h]
            o = _dot_nt(self.qt[:, sl], st.astype(BF16)) + _dot(a_heads[h].astype(BF16), self.vb[:, sl])
            st_ref[d, h] = st * self.decay[:, sl] + _dot_tn(self.vb[:, sl], self.kt[:, sl])
            outs.append(o)
        return jnp.concatenate(outs, axis=1)


def _hgrn_kernel(hq_ref, ff_ref, fb_ref, hi_ref, hg_ref, lbp_ref, gn_ref, mall_ref, ones_ref, s0_ref,
                 rec_ref, sfin_ref, st_ref, of_ref, ob_ref, gf_ref, gb_ref, a_ref, *, t):
    c = HGRN_CHUNK
    n = t // c
    st_ref[...] = s0_ref[0]

    def chunk_rows(ci):
        return pl.ds(pl.multiple_of(ci * c, c), c), pl.ds(pl.multiple_of((n - 1 - ci) * c, c), c)

    def gates_to(slot, ci):
        rf, rb = chunk_rows(ci)
        for d, (rows, f_ref) in enumerate(((rf, ff_ref), (rb, fb_ref))):
            f32_parts, bf16_parts = _hgrn_gates(
                hq_ref[rows, :], f_ref[rows, :], hi_ref[rows, :], lbp_ref[d, 0:1, :], lbp_ref[d, 1:2, :],
                lbp_ref[d, 2:3, :], mall_ref[d].astype(BF16), d == 1)
            for j in range(3):
                gf_ref[slot, d, j] = f32_parts[j]
                gb_ref[slot, d, j] = bf16_parts[j]

    gates_to(0, 0)

    def body(ci, carry):
        slot = ci & 1
        rf, rb = chunk_rows(ci)
        fwd = _HgrnDir([gf_ref[slot, 0, j] for j in range(3)], [gb_ref[slot, 0, j] for j in range(3)], False)
        bwd = _HgrnDir([gf_ref[slot, 1, j] for j in range(3)], [gb_ref[slot, 1, j] for j in range(3)], True)
        for d, hd in enumerate((fwd, bwd)):
            for h, a in enumerate(hd.midpoint_decay_matrices()):
                a_ref[d, h] = a
        gates_to(1 - slot, jnp.minimum(ci + 1, n - 1))

        @pl.when(jnp.maximum(fwd.span, bwd.span) > HGRN_SAFE_RANGE)
        def _():
            for d, hd in enumerate((fwd, bwd)):
                for h, a in enumerate(hd.tree_decay_matrices()):
                    a_ref[d, h] = a

        of_ref[rf, :] = fwd.outputs([a_ref[0, h] for h in range(HGRN_HEADS)], st_ref, 0)
        ob_ref[rb, :] = bwd.outputs([a_ref[1, h] for h in range(HGRN_HEADS)], st_ref, 1)
        return carry

    lax.fori_loop(0, n, body, 0)
    sfin_ref[0] = st_ref[...]
    o = of_ref[...] + ob_ref[...]
    ms = _dot_exact_rhs(o * o, ones_ref[...].astype(BF16)) * (1.0 / HGRN_DK)
    o = o * lax.rsqrt(ms + GN_EPS) * gn_ref[...]
    rec_ref[...] = (o * _silu(hg_ref[...])).astype(rec_ref.dtype)


def _hgrn(proj, row0, b, t, lbp, gn_row, s0t, name):
    base = row0 // t
    m_all = jnp.asarray(_hgrn_tables())
    ones_bd = jnp.asarray(np.kron(np.eye(HGRN_HEADS), np.ones((HGRN_DK, HGRN_DK))).astype(np.float32))

    def col(cstart):
        return pl.BlockSpec((t, HGRN_WIDTH), lambda bb, cb=(cstart - MIX_IN_WIDTH) // HGRN_WIDTH: (base + bb, cb))

    const2 = lambda bb: (0, 0)
    const3 = lambda bb: (0, 0, 0)
    st_shape = (2, HGRN_HEADS, HGRN_DK, HGRN_DK)
    body = functools.partial(_hgrn_kernel, t=t)
    return pl.pallas_call(
        body,
        grid=(b,),
        in_specs=[
            col(COL_HQ), col(COL_FF), col(COL_FB), col(COL_HI), col(COL_HG),
            pl.BlockSpec((2, 3, HGRN_WIDTH), const3),
            pl.BlockSpec((1, HGRN_WIDTH), const2),
            pl.BlockSpec(m_all.shape, const3),
            pl.BlockSpec(ones_bd.shape, const2),
            pl.BlockSpec((1,) + st_shape, lambda bb: (bb, 0, 0, 0, 0)),
        ],
        out_specs=[
            pl.BlockSpec((t, HGRN_WIDTH), lambda bb: (bb, 0)),
            pl.BlockSpec((1,) + st_shape, lambda bb: (bb, 0, 0, 0, 0)),
        ],
        out_shape=[
            jax.ShapeDtypeStruct((b * t, HGRN_WIDTH), BF16),
            jax.ShapeDtypeStruct((b,) + st_shape, F32),
        ],
        scratch_shapes=[
            pltpu.VMEM(st_shape, F32),
            pltpu.VMEM((t, HGRN_WIDTH), F32),
            pltpu.VMEM((t, HGRN_WIDTH), F32),
            pltpu.VMEM((2, 2, 3, HGRN_CHUNK, HGRN_WIDTH), F32),
            pltpu.VMEM((2, 2, 3, HGRN_CHUNK, HGRN_WIDTH), BF16),
            pltpu.VMEM((2, HGRN_HEADS, HGRN_CHUNK, HGRN_CHUNK), F32),
        ],
        compiler_params=_cparams(("arbitrary",)),
        name=name,
    )(proj, proj, proj, proj, proj, lbp, gn_row, m_all, ones_bd, s0t)


def _outproj_kernel(*refs, n_ctx_tiles, n_mix):
    mix = refs[:3 * n_mix]
    refs = refs[3 * n_mix:]
    xs = refs[:-14]
    (mod_ref, w_ref, g_ref, b_ref, rw_ref, rb_ref, x1_ref, hp_ref, meta_ref, gate_ref, cnt_ref, wb_ref, tri_ref,
     run_ref) = refs[-14:]
    tm = x1_ref.shape[0]
    is_ctx = pl.program_id(0) < n_ctx_tiles
    x_in = jnp.where(is_ctx, xs[0][...], xs[1][...]) if len(xs) == 2 else xs[0][...]
    if n_mix == 2:
        attn, four, rec = [jnp.where(is_ctx, mix[2 * j][...], mix[2 * j + 1][...]) for j in range(3)]
    else:
        attn, four, rec = [r[...] for r in mix]

    @pl.when(pl.program_id(0) == 0)
    def _():
        wb_ref[...] = w_ref[0].astype(BF16)
        r = lax.broadcasted_iota(I32, (tm, tm), 0)
        c = lax.broadcasted_iota(I32, (tm, tm), 1)
        tri_ref[...] = jnp.where(r < c, 1.0, 0.0).astype(BF16)
        run_ref[...] = jnp.zeros_like(run_ref)

    out = _dot(attn, wb_ref[0:ATTN_WIDTH, :])
    out = out + _dot(four, wb_ref[ATTN_WIDTH:ATTN_WIDTH + FOURIER_WIDTH, :])
    out = out + _dot(rec, wb_ref[ATTN_WIDTH + FOURIER_WIDTH:, :])
    gate1 = mod_ref[0, 2:3, :]
    y = DEEPNORM_ALPHA * x_in + gate1 * out
    x1 = _ln_plain(y, LN_EPS) * g_ref[...] + b_ref[...]
    x1_ref[...] = x1
    h2 = _ln_plain(x1, ADA_EPS) * (1.0 + mod_ref[0, 4:5, :]) + mod_ref[0, 3:4, :]
    hp_ref[...] = _pack_bf16_pair(h2[:, :HALF_D], h2[:, HALF_D:])

    h_hi = h2.astype(BF16)
    h_lo = (h2 - h_hi.astype(F32)).astype(BF16)
    rwt = rw_ref[...]
    w_hi = rwt.astype(BF16)
    w_lo = (rwt - w_hi.astype(F32)).astype(BF16)
    scores = jax.nn.sigmoid(_dot_nt(w_hi, h_hi) + _dot_nt(w_hi, h_lo) + _dot_nt(w_lo, h_hi))
    remaining = scores + rb_ref[...]
    eidx = lax.broadcasted_iota(I32, scores.shape, 0).astype(F32)
    chosen = jnp.zeros(scores.shape, jnp.bool_)
    picks = []
    for _ in range(TOP_K):
        mx = jnp.max(remaining, axis=0, keepdims=True)
        first = jnp.min(jnp.where(remaining == mx, eidx, float(N_EXPERTS)), axis=0, keepdims=True)
        pick = eidx == first
        picks.append((pick, first))
        chosen = jnp.logical_or(chosen, pick)
        remaining = jnp.where(pick, -jnp.inf, remaining)
    sel = jnp.where(chosen, scores, 0.0)
    gates = sel / jnp.sum(sel, axis=0, keepdims=True) * ROUTED_SCALE

    onehot = jnp.where(chosen, 1.0, 0.0)
    rank = run_ref[...] + _dot(onehot.astype(BF16), tri_ref[...])
    run_ref[...] += jnp.sum(onehot, axis=1, keepdims=True)
    cnt_ref[...] = run_ref[...]

    ids, rks, gks = [], [], []
    for pick, first in picks:
        ids.append(first.astype(I32))
        rks.append(jnp.sum(jnp.where(pick, rank, 0.0), axis=0, keepdims=True).astype(I32))
        gks.append(jnp.sum(jnp.where(pick, gates, 0.0), axis=0, keepdims=True))
    meta_ref[...] = jnp.concatenate(ids + rks, axis=0)
    gate_ref[...] = jnp.concatenate(gks, axis=0)


def _outproj(attn, four, rec, x, mod_l, w_out, layer, g1, b1, rw, rb, lay, tm):
    n_tiles = lay.n // tm
    n_ctx_tiles = lay.n_ctx // tm
    row = lambda i: (i, 0)
    const = lambda i: (0, 0)
    xs = x if isinstance(x, tuple) else (x,)
    return pl.pallas_call(
        functools.partial(_outproj_kernel, n_ctx_tiles=n_ctx_tiles, n_mix=len(attn)),
        grid=(n_tiles,),
        in_specs=[
            *_group_specs(len(attn), tm, ATTN_WIDTH, n_ctx_tiles),
            *_group_specs(len(four), tm, FOURIER_WIDTH, n_ctx_tiles),
            *_group_specs(len(rec), tm, HGRN_WIDTH, n_ctx_tiles),
            *_group_specs(len(xs), tm, D_MODEL, n_ctx_tiles),
            pl.BlockSpec((1, N_MOD, D_MODEL), lambda i: (lay.cond_row(i, tm), 0, 0)),
            pl.BlockSpec((1, D_MODEL, D_MODEL), lambda i: (layer, 0, 0)),
            pl.BlockSpec((1, D_MODEL), const),
            pl.BlockSpec((1, D_MODEL), const),
            pl.BlockSpec((N_EXPERTS, D_MODEL), const),
            pl.BlockSpec((N_EXPERTS, 1), const),
        ],
        out_specs=[
            pl.BlockSpec((tm, D_MODEL), row),
            pl.BlockSpec((tm, HALF_D), row),
            pl.BlockSpec((2 * TOP_K, tm), lambda i: (0, i)),
            pl.BlockSpec((TOP_K, tm), lambda i: (0, i)),
            pl.BlockSpec((N_EXPERTS, 1), const),
        ],
        out_shape=[
            jax.ShapeDtypeStruct((lay.n, D_MODEL), F32),
            jax.ShapeDtypeStruct((lay.n, HALF_D), I32),
            jax.ShapeDtypeStruct((2 * TOP_K, lay.n), I32),
            jax.ShapeDtypeStruct((TOP_K, lay.n), F32),
            jax.ShapeDtypeStruct((N_EXPERTS, 1), F32),
        ],
        scratch_shapes=[
            pltpu.VMEM((D_MODEL, D_MODEL), BF16),
            pltpu.VMEM((tm, tm), BF16),
            pltpu.VMEM((N_EXPERTS, 1), F32),
        ],
        compiler_params=_cparams(("arbitrary",)),
        name="outproj_router" + lay.tag,
    )(*attn, *four, *rec, *xs, mod_l, w_out, g1, b1, rw, rb)


def _sc_workers():
    info = plsc.get_sparse_core_info()
    return info.num_cores, info.num_cores * info.num_subcores


def _sc_scatter_rows(rows, pos_b, r_out, tag=""):
    nc, nw = _sc_workers()
    n, w = rows.shape
    nbt, copies, _ = pos_b.shape
    assert nbt * SC_BATCH == n and nbt % (2 * nw) == 0
    per_w = nbt // nw
    mesh = plsc.VectorSubcoreMesh(core_axis_name="c", subcore_axis_name="s")

    @functools.partial(
        pl.kernel, mesh=mesh, out_type=jax.ShapeDtypeStruct((r_out, w), rows.dtype),
        scratch_types=[pltpu.VMEM((copies, SC_BATCH), I32), pltpu.VMEM((copies, SC_BATCH), I32),
                       pltpu.VMEM((SC_BATCH, w), rows.dtype), pltpu.VMEM((SC_BATCH, w), rows.dtype),
                       pltpu.SemaphoreType.DMA, pltpu.SemaphoreType.DMA,
                       pltpu.SemaphoreType.DMA, pltpu.SemaphoreType.DMA],
        name="sc_dispatch" + tag)
    def k(rows_hbm, pos_hbm, out_hbm, idx_a, idx_b, rows_a, rows_b, sem_ra, sem_rb, sem_sa, sem_sb):
        wid = lax.axis_index("s") * nc + lax.axis_index("c")
        first = wid * per_w

        def reads(j, idx_v, rows_v, sem):
            bt = first + j
            return (pltpu.make_async_copy(pos_hbm.at[bt], idx_v, sem),
                    pltpu.make_async_copy(rows_hbm.at[pl.ds(bt * SC_BATCH, SC_BATCH)], rows_v, sem))

        def scatters(idx_v, rows_v, sem):
            return [pltpu.make_async_copy(rows_v, out_hbm.at[idx_v.at[q]], sem) for q in range(copies)]

        def start(descs):
            for d in descs:
                d.start()

        def wait(descs):
            for d in descs:
                d.wait()

        start(reads(0, idx_a, rows_a, sem_ra))

        @pl.loop(0, per_w // 2)
        def _(p):
            j0 = 2 * p
            j1 = j0 + 1

            @pl.when(p > 0)
            def _():
                wait(scatters(idx_b, rows_b, sem_sb))

            start(reads(j1, idx_b, rows_b, sem_rb))
            wait(reads(j0, idx_a, rows_a, sem_ra))
            start(scatters(idx_a, rows_a, sem_sa))
            wait(reads(j1, idx_b, rows_b, sem_rb))
            start(scatters(idx_b, rows_b, sem_sb))
            wait(scatters(idx_a, rows_a, sem_sa))

            @pl.when(p + 1 < per_w // 2)
            def _():
                start(reads(j0 + 2, idx_a, rows_a, sem_ra))

        wait(scatters(idx_b, rows_b, sem_sb))

    return k(rows, pos_b)


def _sc_gather_rows(table, idx, tag=""):
    nc, nw = _sc_workers()
    r = idx.shape[0]
    w = table.shape[1]
    assert r % (2 * nw * SC_BATCH) == 0
    per_w = r // nw
    nb = per_w // SC_BATCH
    mesh = plsc.VectorSubcoreMesh(core_axis_name="c", subcore_axis_name="s")

    @functools.partial(
        pl.kernel, mesh=mesh, out_type=jax.ShapeDtypeStruct((r, w), table.dtype),
        scratch_types=[pltpu.VMEM((per_w,), I32),
                       pltpu.VMEM((SC_BATCH, w), table.dtype), pltpu.VMEM((SC_BATCH, w), table.dtype),
                       pltpu.SemaphoreType.DMA, pltpu.SemaphoreType.DMA,
                       pltpu.SemaphoreType.DMA, pltpu.SemaphoreType.DMA],
        name="sc_combine" + tag)
    def k(table_hbm, idx_hbm, out_hbm, idx_v, rows_a, rows_b, sem_ga, sem_gb, sem_wa, sem_wb):
        wid = lax.axis_index("s") * nc + lax.axis_index("c")
        base = wid * per_w
        pltpu.sync_copy(idx_hbm.at[pl.ds(base, per_w)], idx_v)

        def gather(j, rows_v, sem):
            return pltpu.make_async_copy(table_hbm.at[idx_v.at[pl.ds(j * SC_BATCH, SC_BATCH)]], rows_v, sem)

        def write(j, rows_v, sem):
            return pltpu.make_async_copy(rows_v, out_hbm.at[pl.ds(base + j * SC_BATCH, SC_BATCH)], sem)

        gather(0, rows_a, sem_ga).start()

        @pl.loop(0, nb // 2)
        def _(p):
            j0 = 2 * p
            j1 = j0 + 1

            @pl.when(p > 0)
            def _():
                write(j1 - 2, rows_b, sem_wb).wait()

            gather(j1, rows_b, sem_gb).start()
            gather(j0, rows_a, sem_ga).wait()
            write(j0, rows_a, sem_wa).start()
            gather(j1, rows_b, sem_gb).wait()
            write(j1, rows_b, sem_wb).start()
            write(j0, rows_a, sem_wa).wait()

            @pl.when(p + 1 < nb // 2)
            def _():
                gather(j0 + 2, rows_a, sem_ga).start()

        write(nb - 1, rows_b, sem_wb).wait()

    return k(table, idx)


def _experts_kernel(te_ref, na_ref, x_ref, w1_ref, w3_ref, w2_ref, o_ref, w1b_ref, w3b_ref, w2b_ref):
    del te_ref

    @pl.when(pl.program_id(0) < na_ref[0])
    def _():
        w1b_ref[...] = w1_ref[0, 0].astype(BF16)
        w3b_ref[...] = w3_ref[0, 0].astype(BF16)
        w2b_ref[...] = w2_ref[0, 0].astype(BF16)
        lo, hi = _unpack_bf16_pair(x_ref[...])
        lo = lo.astype(BF16)
        hi = hi.astype(BF16)
        a = _dot(lo, w1b_ref[0:HALF_D, :]) + _dot(hi, w1b_ref[HALF_D:, :])
        b = _dot(lo, w3b_ref[0:HALF_D, :]) + _dot(hi, w3b_ref[HALF_D:, :])
        y = _dot((_silu(a) * b).astype(BF16), w2b_ref[...])
        o_ref[...] = _pack_bf16_pair(y[:, :HALF_D], y[:, HALF_D:])


def _experts(xs, tile_expert, n_active, w1, w3, w2, layer, tm, tag):
    r = xs.shape[0]
    n_tiles = r // tm

    def xmap(j, te, na):
        return (jnp.minimum(j, na[0] - 1), 0)

    def wmap(j, te, na):
        return (layer, te[jnp.minimum(j, na[0] - 1)], 0, 0)

    grid_spec = pltpu.PrefetchScalarGridSpec(
        num_scalar_prefetch=2,
        grid=(n_tiles,),
        in_specs=[
            pl.BlockSpec((tm, HALF_D), xmap),
            pl.BlockSpec((1, 1, D_MODEL, EXPERT_FF), wmap),
            pl.BlockSpec((1, 1, D_MODEL, EXPERT_FF), wmap),
            pl.BlockSpec((1, 1, EXPERT_FF, D_MODEL), wmap),
        ],
        out_specs=pl.BlockSpec((tm, HALF_D), xmap),
        scratch_shapes=[
            pltpu.VMEM((D_MODEL, EXPERT_FF), BF16),
            pltpu.VMEM((D_MODEL, EXPERT_FF), BF16),
            pltpu.VMEM((EXPERT_FF, D_MODEL), BF16),
        ],
    )
    return pl.pallas_call(
        _experts_kernel,
        grid_spec=grid_spec,
        out_shape=jax.ShapeDtypeStruct((r, HALF_D), I32),
        compiler_params=_cparams(("arbitrary",)),
        name="experts" + tag,
    )(tile_expert, n_active, xs, w1, w3, w2)


def _combine_kernel(yp_ref, gate_ref, hp_ref, sw1_ref, sw3_ref, sw2_ref, x_ref, mod_ref, g_ref, b_ref, *refs,
                    n_ctx_tiles):
    outs = refs[:-3]
    w1b_ref, w3b_ref, w2b_ref = refs[-3:]

    @pl.when(pl.program_id(0) == 0)
    def _():
        w1b_ref[...] = sw1_ref[...].astype(BF16)
        w3b_ref[...] = sw3_ref[...].astype(BF16)
        w2b_ref[...] = sw2_ref[...].astype(BF16)

    lo, hi = _unpack_bf16_pair(hp_ref[...])
    lo = lo.astype(BF16)
    hi = hi.astype(BF16)
    a = _dot(lo, w1b_ref[0:HALF_D, :]) + _dot(hi, w1b_ref[HALF_D:, :])
    b = _dot(lo, w3b_ref[0:HALF_D, :]) + _dot(hi, w3b_ref[HALF_D:, :])
    shared = _dot((_silu(a) * b).astype(BF16), w2b_ref[...])
    acc_lo = shared[:, :HALF_D]
    acc_hi = shared[:, HALF_D:]
    gates = gate_ref[...]
    for k in range(TOP_K):
        ylo, yhi = _unpack_bf16_pair(yp_ref[k])
        gk = gates[:, k:k + 1]
        acc_lo = acc_lo + gk * ylo
        acc_hi = acc_hi + gk * yhi
    moe = jnp.concatenate([acc_lo, acc_hi], axis=1)
    y = DEEPNORM_ALPHA * x_ref[...] + mod_ref[0, 5:6, :] * moe
    res = _ln_plain(y, LN_EPS) * g_ref[...] + b_ref[...]
    if len(outs) == 1:
        outs[0][...] = res
    else:
        @pl.when(pl.program_id(0) < n_ctx_tiles)
        def _():
            outs[0][...] = res

        @pl.when(pl.program_id(0) >= n_ctx_tiles)
        def _():
            outs[1][...] = res


def _combine(yp, gate8, hp, sw1, sw3, sw2, x1, mod_l, g2, b2, lay, tm, split_out):
    n_tiles = lay.n // tm
    n_ctx_tiles = lay.n_ctx // tm
    row = lambda i: (i, 0)
    const = lambda i: (0, 0)
    if split_out:
        out_specs = _group_specs(2, tm, D_MODEL, n_ctx_tiles)
        out_shape = [jax.ShapeDtypeStruct((lay.n_ctx, D_MODEL), F32), jax.ShapeDtypeStruct((lay.n_lat, D_MODEL), F32)]
    else:
        out_specs = pl.BlockSpec((tm, D_MODEL), row)
        out_shape = jax.ShapeDtypeStruct((lay.n, D_MODEL), F32)
    return pl.pallas_call(
        functools.partial(_combine_kernel, n_ctx_tiles=n_ctx_tiles),
        grid=(n_tiles,),
        in_specs=[
            pl.BlockSpec((TOP_K, tm, HALF_D), lambda i: (0, i, 0)),
            pl.BlockSpec((tm, TOP_K), row),
            pl.BlockSpec((tm, HALF_D), row),
            pl.BlockSpec((D_MODEL, EXPERT_FF), const),
            pl.BlockSpec((D_MODEL, EXPERT_FF), const),
            pl.BlockSpec((EXPERT_FF, D_MODEL), const),
            pl.BlockSpec((tm, D_MODEL), row),
            pl.BlockSpec((1, N_MOD, D_MODEL), lambda i: (lay.cond_row(i, tm), 0, 0)),
            pl.BlockSpec((1, D_MODEL), const),
            pl.BlockSpec((1, D_MODEL), const),
        ],
        out_specs=out_specs,
        out_shape=out_shape,
        scratch_shapes=[
            pltpu.VMEM((D_MODEL, EXPERT_FF), BF16),
            pltpu.VMEM((D_MODEL, EXPERT_FF), BF16),
            pltpu.VMEM((EXPERT_FF, D_MODEL), BF16),
        ],
        compiler_params=_cparams(("arbitrary",)),
        name="combine_norm" + lay.tag,
    )(yp, gate8, hp, sw1, sw3, sw2, x1, mod_l, g2, b2)


def _moe_dispatch(hp, meta, counts, lay, tile):
    n = lay.n
    r_max = n * TOP_K + N_EXPERTS * tile
    n_tiles = r_max // tile
    cnt = counts.reshape(N_EXPERTS).astype(I32)
    padded = ((cnt + tile - 1) // tile) * tile
    ends = jnp.cumsum(padded)
    offsets = ends - padded
    idx8 = meta[:TOP_K]
    base8 = jnp.sum(jnp.where(idx8[:, :, None] == jnp.arange(N_EXPERTS, dtype=I32), offsets, 0), axis=-1)
    pos = (base8 + meta[TOP_K:]).astype(I32)
    tile_start = jnp.arange(n_tiles, dtype=I32) * tile
    tile_expert = jnp.minimum(jnp.sum(tile_start[:, None] >= ends[None, :], axis=1), N_EXPERTS - 1).astype(I32)
    n_active = (ends[-1] // tile).astype(I32).reshape(1)
    pos_b = pos.reshape(TOP_K, n // SC_BATCH, SC_BATCH).transpose(1, 0, 2)
    xs = _sc_scatter_rows(hp, pos_b, r_max, lay.tag)
    return xs, tile_expert, n_active, pos


def _moe_combine(ys, pos, gate8, hp, sw1, sw3, sw2, x1, mod_l, g2, b2, lay, split_out):
    n = lay.n
    yp = _sc_gather_rows(ys, pos.reshape(n * TOP_K), lay.tag).reshape(TOP_K, n, HALF_D)
    return _combine(yp, gate8.T, hp, sw1, sw3, sw2, x1, mod_l, g2, b2, lay, TOKEN_TILE, split_out)


def kernel(x_prompt, x_sample, cache_k, cache_v, state_hgrn, c, c_ctx, w_ada, b_ada, w_in, w_out, attn_sink, hgrn_lb, hgrn_norm, ln1_g, ln1_b, ln2_g, ln2_b, router_w, router_b, moe_w1, moe_w3, moe_w2, shared_w1, shared_w3, shared_w2):
    b_ctx, t_ctx, _ = x_prompt.shape
    b_lat, t_lat, _ = x_sample.shape
    past = cache_k.shape[2]
    tm = TOKEN_TILE
    assert 1 + b_lat <= COND_ROWS
    lay = _Layout(b_ctx, t_ctx, b_lat, t_lat)
    assert lay.n_ctx % tm == 0 and t_lat % tm == 0 and lay.n_ctx % t_lat == 0

    cond = jnp.concatenate([c_ctx[None, :], c, jnp.zeros((COND_ROWS - 1 - b_lat, D_MODEL), F32)], axis=0)
    mod = _adaln(cond, w_ada, b_ada).reshape(DEPTH, COND_ROWS, N_MOD, D_MODEL)

    lb_all = jnp.cumsum(jax.nn.softmax(hgrn_lb.astype(F32), axis=0), axis=0)
    lb_all = lb_all - lb_all[:1]
    lbp = jnp.stack([jnp.log(lb_all), jnp.log1p(-lb_all), 1.0 - lb_all], axis=2)

    cos_t, sin_t = _rope_tables(lay, tm)
    zero_state = jnp.zeros((b_ctx, 2, HGRN_HEADS, HGRN_DK, HGRN_DK), F32)

    def layer(l, lay, x, split_out):
        lat = slice(lay.lat_first, lay.lat_first + lay.b_lat)
        outs = _inproj(x, mod[l], w_in, l, cos_t, sin_t, lay, tm)
        proj, proj_h = outs[0], outs[1]
        sink_l = attn_sink[l].reshape(1, N_HEADS)
        gn_row = jnp.tile(hgrn_norm[l], HGRN_HEADS).reshape(1, HGRN_WIDTH)
        attn, four, rec, extras = [], [], [], None
        if lay.b_ctx:
            attn.append(_attn_context(proj, sink_l, lay))
            four.append(_fourier(proj, 0, lay.b_ctx, t_ctx, t_ctx, "fourier_ctx"))
            rec_c, s_fin = _hgrn(proj_h, 0, lay.b_ctx, t_ctx, lbp[l], gn_row, zero_state, "hgrn_ctx")
            rec.append(rec_c)
            extras = (outs[2], outs[3], s_fin)
        attn.append(_attn_latent(proj, cache_k[lat, l].reshape(lay.b_lat, past, KV_WIDTH),
                                 cache_v[lat, l].reshape(lay.b_lat, past, KV_WIDTH), sink_l, lay))
        four.append(_fourier(proj, lay.n_ctx, lay.b_lat, t_lat, min(t_lat, 512), "fourier_lat" + lay.tag))
        s0t = jnp.swapaxes(state_hgrn[lat, l].astype(F32), -1, -2)
        rec.append(_hgrn(proj_h, lay.n_ctx, lay.b_lat, t_lat, lbp[l], gn_row, s0t, "hgrn_lat" + lay.tag)[0])
        x1, hp, meta, gate8, counts = _outproj(
            tuple(attn), tuple(four), tuple(rec), x, mod[l], w_out, l, ln1_g[l].reshape(1, -1),
            ln1_b[l].reshape(1, -1), router_w[l].T, router_b[l].reshape(-1, 1), lay, tm)
        xs, tile_expert, n_active, pos = _moe_dispatch(hp, meta, counts, lay, EXPERT_TILE)
        ys = _experts(xs, tile_expert, n_active, moe_w1, moe_w3, moe_w2, l, EXPERT_TILE, lay.tag)
        x = _moe_combine(ys, pos, gate8, hp, shared_w1[l], shared_w3[l], shared_w2[l], x1, mod[l],
                         ln2_g[l].reshape(1, -1), ln2_b[l].reshape(1, -1), lay, split_out)
        return x, extras

    x = (x_prompt.reshape(lay.n_ctx, D_MODEL), x_sample.reshape(lay.n_lat, D_MODEL))
    ks_out, vs_out, ss_out = [], [], []
    for l in range(DEPTH):
        x, (k_new, v_new, s_fin) = layer(l, lay, x, split_out=(l == DEPTH - 1))
        ks_out.append(k_new.reshape(b_ctx, t_ctx, N_KV_HEADS, HEAD_DIM))
        vs_out.append(v_new.reshape(b_ctx, t_ctx, N_KV_HEADS, HEAD_DIM))
        ss_out.append(jnp.swapaxes(s_fin, -1, -2))

    y_prompt = x[0].reshape(b_ctx, t_ctx, D_MODEL)
    y_sample = x[1].reshape(b_lat, t_lat, D_MODEL)
    new_cache_k = jnp.stack(ks_out, axis=1)
    new_cache_v = jnp.stack(vs_out, axis=1)
    new_state = jnp.stack(ss_out, axis=1).astype(x_prompt.dtype)
    return (y_prompt, y_sample, new_cache_k, new_cache_v, new_state)
```

```python
import functools
import math

import numpy as np
import jax
import jax.numpy as jnp
from jax import lax
from jax.experimental import pallas as pl
from jax.experimental.pallas import tpu as pltpu
from jax.experimental.pallas import tpu_sc as plsc

F32 = jnp.float32
BF16 = jnp.bfloat16
I32 = jnp.int32

D_MODEL = 1024
HALF_D = D_MODEL // 2
DEPTH = 2
GRID_W = 64
ROPE_BASE = 10000.0
HEAD_DIM = 64
ATTN_WIDTH = 512
N_HEADS = 8
N_KV_HEADS = 2
KV_GROUP = 4
KV_WIDTH = N_KV_HEADS * HEAD_DIM
WINDOW = 128
ATTN_BLOCK = 128
FOURIER_WIDTH = 256
FOURIER_GROUPS = 4
HGRN_WIDTH = 256
HGRN_HEADS = 4
HGRN_DK = 64
HGRN_CHUNK = 64
IN_WIDTH = 2304
N_EXPERTS = 64
TOP_K = 8
EXPERT_FF = 256
ROUTED_SCALE = 2.5
N_MOD = 6
LN_EPS = 1e-5
ADA_EPS = 1e-6
GN_EPS = 1e-6
DEEPNORM_ALPHA = (2 * DEPTH) ** 0.25

COL_Q = 0
COL_K = 512
COL_V = 640
COL_U = 768
COL_HQ = 1024
COL_FF = 1280
COL_FB = 1536
COL_HI = 1792
COL_HG = 2048
ROPE_COLS = COL_V
MIX_IN_WIDTH = COL_HQ

V7X_LANES = 128
COND_ROWS = 16
NEG_BIG = -1e30
TOKEN_TILE = 512
EXPERT_TILE = 1024
SC_BATCH = 64

VMEM_LIMIT = 56 * 1024 * 1024


def _cparams(sem):
    return pltpu.CompilerParams(dimension_semantics=sem, vmem_limit_bytes=VMEM_LIMIT)


def _dot(a, b):
    return jnp.dot(a, b, preferred_element_type=F32)


def _dot_nt(a, b):
    return lax.dot_general(a, b, (((1,), (1,)), ((), ())), preferred_element_type=F32)


def _dot_tn(a, b):
    return lax.dot_general(a, b, (((0,), (0,)), ((), ())), preferred_element_type=F32)


def _split3(x):
    hi = x.astype(BF16)
    r1 = x - hi.astype(F32)
    mid = r1.astype(BF16)
    lo = (r1 - mid.astype(F32)).astype(BF16)
    return hi, mid, lo


def _dot_exact_lhs(m_bf16, x):
    hi, mid, lo = _split3(x)
    return _dot(m_bf16, hi) + _dot(m_bf16, mid) + _dot(m_bf16, lo)


def _dot_exact_rhs(x, m_bf16):
    hi, mid, lo = _split3(x)
    return _dot(hi, m_bf16) + _dot(mid, m_bf16) + _dot(lo, m_bf16)


def _dot_hp(a, b):
    a_hi = a.astype(BF16)
    a_lo = (a - a_hi.astype(F32)).astype(BF16)
    b_hi = b.astype(BF16)
    b_lo = (b - b_hi.astype(F32)).astype(BF16)
    return _dot(a_hi, b_hi) + _dot(a_hi, b_lo) + _dot(a_lo, b_hi)


def _pack_bf16_pair(lo, hi):
    return lax.bitcast_convert_type(pltpu.pack_elementwise([lo, hi], packed_dtype=BF16), I32)


def _unpack_bf16_pair(w):
    u = lax.bitcast_convert_type(w, jnp.uint32)
    lo = pltpu.unpack_elementwise(u, index=0, packed_dtype=BF16, unpacked_dtype=F32)
    hi = pltpu.unpack_elementwise(u, index=1, packed_dtype=BF16, unpacked_dtype=F32)
    return lo, hi


def _ln_plain(x, eps):
    mu = jnp.mean(x, axis=-1, keepdims=True)
    xc = x - mu
    var = jnp.mean(xc * xc, axis=-1, keepdims=True)
    return xc * lax.rsqrt(var + eps)


def _silu(x):
    return x * jax.nn.sigmoid(x)


def _adaln_kernel(c_ref, w_ref, b_ref, o_ref):
    s = _silu(c_ref[...])
    o_ref[0] = _dot_hp(s, w_ref[0]) + b_ref[0]


def _adaln(cond, w_ada, b_ada):
    return pl.pallas_call(
        _adaln_kernel,
        grid=(DEPTH, N_MOD),
        in_specs=[
            pl.BlockSpec((COND_ROWS, D_MODEL), lambda l, j: (0, 0)),
            pl.BlockSpec((1, D_MODEL, D_MODEL), lambda l, j: (l, 0, j)),
            pl.BlockSpec((1, 1, D_MODEL), lambda l, j: (l, 0, j)),
        ],
        out_specs=pl.BlockSpec((1, COND_ROWS, D_MODEL), lambda l, j: (l, 0, j)),
        out_shape=jax.ShapeDtypeStruct((DEPTH, COND_ROWS, N_MOD * D_MODEL), F32),
        compiler_params=_cparams(("arbitrary", "arbitrary")),
        name="adaln",
    )(cond, w_ada, b_ada.reshape(DEPTH, 1, N_MOD * D_MODEL))


class _Layout:
    def __init__(self, b_ctx, t_ctx, b_lat, t_lat, lat_first=0, tag=""):
        self.b_ctx, self.t_ctx, self.b_lat, self.t_lat = b_ctx, t_ctx, b_lat, t_lat
        self.lat_first = lat_first
        self.n_ctx = b_ctx * t_ctx
        self.n_lat = b_lat * t_lat
        self.n = self.n_ctx + self.n_lat
        self.tag = tag

    def cond_row(self, tile, tm):
        n_ctx_tiles = self.n_ctx // tm
        per_batch = self.t_lat // tm
        return jnp.where(tile < n_ctx_tiles, 0, 1 + self.lat_first + (tile - n_ctx_tiles) // per_batch)


def _group_specs(n_arrays, tm, width, n_ctx_tiles):
    if n_arrays == 1:
        return [pl.BlockSpec((tm, width), lambda i: (i, 0))]
    return [pl.BlockSpec((tm, width), lambda i: (jnp.minimum(i, n_ctx_tiles - 1), 0)),
            pl.BlockSpec((tm, width), lambda i: (jnp.maximum(i - n_ctx_tiles, 0), 0))]


def _inproj_kernel(*refs, n_ctx_tiles):
    n_tail = 9 if n_ctx_tiles > 0 else 7
    xs = refs[:-n_tail]
    mod_ref, w_ref, cos_ref, sin_ref, oa_ref, oh_ref = refs[-n_tail:-n_tail + 6]
    wb_ref = refs[-1]

    @pl.when(pl.program_id(0) == 0)
    def _():
        wb_ref[...] = w_ref[0].astype(BF16)

    if len(xs) == 2:
        x = jnp.where(pl.program_id(0) < n_ctx_tiles, xs[0][...], xs[1][...])
    else:
        x = xs[0][...]
    shift = mod_ref[0, 0:1, :]
    scale = mod_ref[0, 1:2, :]
    h = (_ln_plain(x, ADA_EPS) * (1.0 + scale) + shift).astype(BF16)
    p = _dot(h, wb_ref[...])
    cos = cos_ref[...]
    sin = sin_ref[...]
    lane = lax.broadcasted_iota(I32, cos.shape, 1)
    first_half = (lane & 31) < 16
    for cb in range(ROPE_COLS // V7X_LANES):
        seg = p[:, cb * V7X_LANES:(cb + 1) * V7X_LANES]
        partner = jnp.where(first_half, pltpu.roll(seg, V7X_LANES - 16, 1), pltpu.roll(seg, 16, 1))
        oa_ref[:, cb * V7X_LANES:(cb + 1) * V7X_LANES] = (seg * cos + partner * sin).astype(BF16)
    oa_ref[:, ROPE_COLS:] = p[:, ROPE_COLS:MIX_IN_WIDTH].astype(BF16)
    oh_ref[...] = p[:, MIX_IN_WIDTH:]

    if n_ctx_tiles > 0:
        kc_ref, vc_ref = refs[-3], refs[-2]

        @pl.when(pl.program_id(0) < n_ctx_tiles)
        def _():
            kc_ref[...] = p[:, COL_K:COL_K + KV_WIDTH]
            vc_ref[...] = p[:, COL_V:COL_V + KV_WIDTH]


def _rope_tables(lay, tm):
    t = lay.t_lat
    pos = jnp.arange(t)
    row = (pos // GRID_W).astype(F32)
    col = (pos % GRID_W).astype(F32)
    n_freq = HEAD_DIM // 4
    inv = ROPE_BASE ** (-jnp.arange(n_freq, dtype=F32) / n_freq)
    ang_r = row[:, None] * inv
    ang_c = col[:, None] * inv
    ang = jnp.concatenate([ang_r, ang_r, ang_c, ang_c], axis=1)
    sign = jnp.concatenate([-jnp.ones(n_freq), jnp.ones(n_freq), -jnp.ones(n_freq), jnp.ones(n_freq)]).astype(F32)
    cos = jnp.cos(ang)
    sin = jnp.sin(ang) * sign
    cos = jnp.concatenate([jnp.ones((tm, HEAD_DIM), F32), cos], axis=0)
    sin = jnp.concatenate([jnp.zeros((tm, HEAD_DIM), F32), sin], axis=0)
    return jnp.tile(cos, (1, 2)), jnp.tile(sin, (1, 2))


def _inproj(x, mod_l, w_in, layer, cos_t, sin_t, lay, tm):
    n_tiles = lay.n // tm
    n_ctx_tiles = lay.n_ctx // tm
    per_batch = lay.t_lat // tm

    def tbl(i):
        return jnp.where(i < n_ctx_tiles, 0, 1 + (i - n_ctx_tiles) % per_batch)

    xs = x if isinstance(x, tuple) else (x,)
    kv_specs, kv_shapes = [], []
    if n_ctx_tiles > 0:
        kv_specs = [pl.BlockSpec((tm, KV_WIDTH), lambda i: (jnp.minimum(i, n_ctx_tiles - 1), 0))] * 2
        kv_shapes = [jax.ShapeDtypeStruct((lay.n_ctx, KV_WIDTH), F32)] * 2
    return pl.pallas_call(
        functools.partial(_inproj_kernel, n_ctx_tiles=n_ctx_tiles),
        grid=(n_tiles,),
        in_specs=[
            *_group_specs(len(xs), tm, D_MODEL, n_ctx_tiles),
            pl.BlockSpec((1, N_MOD, D_MODEL), lambda i: (lay.cond_row(i, tm), 0, 0)),
            pl.BlockSpec((1, D_MODEL, IN_WIDTH), lambda i: (layer, 0, 0), pipeline_mode=pl.Buffered(1)),
            pl.BlockSpec((tm, V7X_LANES), lambda i: (tbl(i), 0)),
            pl.BlockSpec((tm, V7X_LANES), lambda i: (tbl(i), 0)),
        ],
        out_specs=[pl.BlockSpec((tm, MIX_IN_WIDTH), lambda i: (i, 0)),
                   pl.BlockSpec((tm, IN_WIDTH - MIX_IN_WIDTH), lambda i: (i, 0))] + kv_specs,
        out_shape=[jax.ShapeDtypeStruct((lay.n, MIX_IN_WIDTH), BF16),
                   jax.ShapeDtypeStruct((lay.n, IN_WIDTH - MIX_IN_WIDTH), F32)] + kv_shapes,
        scratch_shapes=[pltpu.VMEM((D_MODEL, IN_WIDTH), BF16)],
        compiler_params=_cparams(("arbitrary",)),
        name="inproj" + lay.tag,
    )(*xs, mod_l, w_in, cos_t, sin_t)


def _attn_kernel(sink_ref, q_ref, *refs, n_local, has_ctx, t_total):
    o_ref = refs[-1]
    k_refs = refs[:n_local]
    v_refs = refs[n_local:2 * n_local]
    tq = q_ref.shape[0]
    scale = HEAD_DIM ** -0.5
    k_parts = [kr[...] for kr in k_refs]
    v_parts = [vr[...] for vr in v_refs]
    if has_ctx:
        k_parts.append(refs[2 * n_local][0].astype(k_parts[0].dtype))
        v_parts.append(refs[2 * n_local + 1][0].astype(v_parts[0].dtype))
    kall = (jnp.concatenate(k_parts, axis=0) if len(k_parts) > 1 else k_parts[0]).astype(F32)
    vall = (jnp.concatenate(v_parts, axis=0) if len(v_parts) > 1 else v_parts[0]).astype(F32)
    nk = kall.shape[0]
    k_sw = pltpu.roll(kall, HEAD_DIM, 1)
    v_sw = pltpu.roll(vall, HEAD_DIM, 1)
    lo_half = lax.broadcasted_iota(I32, (1, V7X_LANES), 1) < HEAD_DIM
    er = jnp.where(lax.broadcasted_iota(I32, (2 * nk, V7X_LANES), 0) < nk, 0, 1)
    el = jnp.where(lax.broadcasted_iota(I32, (2 * nk, V7X_LANES), 1) < HEAD_DIM, 0, 1)
    ones_blk = jnp.where(er == el, 1.0, 0.0).astype(BF16)
    if n_local > 1:
        i = pl.program_id(1)
        band = refs[-2][...]
        first_blk = jnp.where(i == 0, NEG_BIG, 0.0)
        last_blk = jnp.where(i == t_total // tq - 1, NEG_BIG, 0.0)

        def mask_local(sc):
            loc = sc[:, :n_local * tq] + band
            parts = [loc[:, :tq] + first_blk, loc[:, tq:(n_local - 1) * tq], loc[:, (n_local - 1) * tq:] + last_blk]
            return jnp.concatenate(parts + [sc[:, n_local * tq:]], axis=1)
    else:
        mask_local = None
    v2es, sinks, scores = [], [], []
    for g in range(N_KV_HEADS):
        k_own, k_oth = (kall, k_sw) if g == 0 else (k_sw, kall)
        v_own, v_oth = (vall, v_sw) if g == 0 else (v_sw, vall)
        k2 = jnp.concatenate([jnp.where(lo_half, k_own, 0.0), jnp.where(lo_half, 0.0, k_oth)], axis=0).astype(BF16)
        v2 = jnp.concatenate([jnp.where(lo_half, v_own, 0.0), jnp.where(lo_half, 0.0, v_oth)], axis=0).astype(BF16)
        v2es.append(jnp.concatenate([v2, ones_blk], axis=1))
        pairs = [2 * g, 2 * g + 1]
        qq = jnp.concatenate([q_ref[:, p * V7X_LANES:(p + 1) * V7X_LANES] for p in pairs], axis=0)
        qq = (qq.astype(F32) * scale).astype(BF16)
        sinks.append((jnp.concatenate([jnp.full((tq, 1), sink_ref[0, 2 * p], F32) for p in pairs], axis=0),
                      jnp.concatenate([jnp.full((tq, 1), sink_ref[0, 2 * p + 1], F32) for p in pairs], axis=0)))
        scores.append(_dot_nt(qq, k2))
    pes, maxes = [], []
    for g in range(N_KV_HEADS):
        s_a = scores[g][:, :nk]
        s_b = scores[g][:, nk:]
        if mask_local is not None:
            s_a = mask_local(s_a)
            s_b = mask_local(s_b)
        m_a = jnp.maximum(jnp.max(s_a, axis=1, keepdims=True), sinks[g][0])
        m_b = jnp.maximum(jnp.max(s_b, axis=1, keepdims=True), sinks[g][1])
        pes.append(jnp.concatenate([jnp.exp(s_a - m_a).astype(BF16), jnp.exp(s_b - m_b).astype(BF16)], axis=1))
        maxes.append((m_a, m_b))
    for g in range(N_KV_HEADS):
        acc = _dot(pes[g], v2es[g])
        (m_a, m_b), (sink_a, sink_b) = maxes[g], sinks[g]
        sink_term = jnp.where(lo_half, jnp.exp(sink_a - m_a), jnp.exp(sink_b - m_b))
        o = acc[:, :V7X_LANES] / (acc[:, V7X_LANES:] + sink_term)
        for j, p in enumerate([2 * g, 2 * g + 1]):
            o_ref[:, p * V7X_LANES:(p + 1) * V7X_LANES] = o[j * tq:(j + 1) * tq].astype(o_ref.dtype)


def _attn_context(proj, sink_l, lay):
    t = lay.t_ctx
    kb, vb = COL_K // KV_WIDTH, COL_V // KV_WIDTH
    body = functools.partial(_attn_kernel, n_local=1, has_ctx=False, t_total=t)
    return pl.pallas_call(
        body,
        grid=(lay.b_ctx,),
        in_specs=[
            pl.BlockSpec(memory_space=pltpu.SMEM),
            pl.BlockSpec((t, ATTN_WIDTH), lambda b: (b, 0)),
            pl.BlockSpec((t, KV_WIDTH), lambda b: (b, kb)),
            pl.BlockSpec((t, KV_WIDTH), lambda b: (b, vb)),
        ],
        out_specs=pl.BlockSpec((t, ATTN_WIDTH), lambda b: (b, 0)),
        out_shape=jax.ShapeDtypeStruct((lay.n_ctx, ATTN_WIDTH), BF16),
        compiler_params=_cparams(("arbitrary",)),
        name="attn_ctx",
    )(sink_l, proj, proj, proj)


def _attn_latent(proj, k_ctx, v_ctx, sink_l, lay):
    t = lay.t_lat
    tq = ATTN_BLOCK
    nq = t // tq
    base = lay.n_ctx // tq
    kb, vb = COL_K // KV_WIDTH, COL_V // KV_WIDTH
    past = k_ctx.shape[1]

    def rows(off):
        return lambda b, i: base + b * nq + jnp.clip(i + off, 0, nq - 1)

    def kv_specs(col):
        return [pl.BlockSpec((tq, KV_WIDTH), (lambda b, i, f=rows(off): (f(b, i), col))) for off in (-1, 0, 1)]

    body = functools.partial(_attn_kernel, n_local=3, has_ctx=True, t_total=t)
    rel = np.arange(3 * tq)[None, :] - tq - (np.arange(2 * tq)[:, None] % tq)
    band = jnp.asarray(np.where(np.abs(rel) <= WINDOW, 0.0, NEG_BIG).astype(np.float32))
    return pl.pallas_call(
        body,
        grid=(lay.b_lat, nq),
        in_specs=[
            pl.BlockSpec(memory_space=pltpu.SMEM),
            pl.BlockSpec((tq, ATTN_WIDTH), lambda b, i: (base + b * nq + i, 0)),
            *kv_specs(kb),
            *kv_specs(vb),
            pl.BlockSpec((1, past, KV_WIDTH), lambda b, i: (b, 0, 0)),
            pl.BlockSpec((1, past, KV_WIDTH), lambda b, i: (b, 0, 0)),
            pl.BlockSpec(band.shape, lambda b, i: (0, 0)),
        ],
        out_specs=pl.BlockSpec((tq, ATTN_WIDTH), lambda b, i: (b * nq + i, 0)),
        out_shape=jax.ShapeDtypeStruct((lay.n_lat, ATTN_WIDTH), BF16),
        compiler_params=_cparams(("arbitrary", "arbitrary")),
        name="attn_lat" + lay.tag,
    )(sink_l, proj, proj, proj, proj, proj, proj, proj, k_ctx, v_ctx, band)


def _fourier_kernel(cs_ref, u_ref, cc_ref, sc_ref, o_ref, csb_ref, *, scale):
    @pl.when(pl.program_id(1) == 0)
    def _():
        csb_ref[...] = cs_ref[...].astype(BF16)

    z = u_ref[...].astype(BF16)
    zc = _dot(z, cc_ref[...].astype(BF16)).astype(BF16)
    zs = _dot(z, sc_ref[...].astype(BF16)).astype(BF16)
    zz = jnp.concatenate([zc, zs], axis=0)
    o_ref[...] = (_dot(csb_ref[...], zz) * scale).astype(o_ref.dtype)


@functools.lru_cache(maxsize=None)
def _dft_tables(t):
    idx = np.arange(t, dtype=np.int64)
    ang = 2.0 * np.pi * ((idx[:, None] * idx[None, :]) % t).astype(np.float64) / t
    cs = np.concatenate([np.cos(ang), -np.sin(ang)], axis=1).astype(np.float32)
    cw = FOURIER_WIDTH // FOURIER_GROUPS
    cidx = np.arange(cw, dtype=np.int64)
    cang = 2.0 * np.pi * ((cidx[:, None] * cidx[None, :]) % cw).astype(np.float64) / cw
    eye = np.eye(FOURIER_GROUPS)
    cc = np.kron(eye, np.cos(cang)).astype(np.float32)
    sc = np.kron(eye, np.sin(cang)).astype(np.float32)
    return cs, cc, sc


def _fourier(proj, row0, b, t, tm, name):
    cs, cc, sc = _dft_tables(t)
    cw = FOURIER_WIDTH // FOURIER_GROUPS
    nt = t // tm
    ub = COL_U // FOURIER_WIDTH
    base = row0 // t
    body = functools.partial(_fourier_kernel, scale=1.0 / math.sqrt(t * cw))
    return pl.pallas_call(
        body,
        grid=(nt, b),
        in_specs=[
            pl.BlockSpec((tm, 2 * t), lambda i, bb: (i, 0)),
            pl.BlockSpec((t, FOURIER_WIDTH), lambda i, bb: (base + bb, ub)),
            pl.BlockSpec((FOURIER_WIDTH, FOURIER_WIDTH), lambda i, bb: (0, 0)),
            pl.BlockSpec((FOURIER_WIDTH, FOURIER_WIDTH), lambda i, bb: (0, 0)),
        ],
        out_specs=pl.BlockSpec((tm, FOURIER_WIDTH), lambda i, bb: (bb * nt + i, 0)),
        out_shape=jax.ShapeDtypeStruct((b * t, FOURIER_WIDTH), BF16),
        scratch_shapes=[pltpu.VMEM((tm, 2 * t), BF16)],
        compiler_params=_cparams(("arbitrary", "arbitrary")),
        name=name,
    )(jnp.asarray(cs), proj, jnp.asarray(cc), jnp.asarray(sc))


HGRN_LEVELS = (64, 32, 16, 8, 4, 2)
HGRN_SAFE_RANGE = 80.0


@functools.lru_cache(maxsize=None)
def _hgrn_tables():
    c = HGRN_CHUNK
    return np.stack([np.tril(np.ones((c, c))), np.triu(np.ones((c, c)))]).astype(np.float32)


def _boundary_rows(b, m, reverse):
    c, w = b.shape
    half = m // 2
    off = half if reverse else half - 1
    if m >= 16:
        return jnp.concatenate(
            [jnp.broadcast_to(b[s + off:s + off + 1], (m, w)) for s in range(0, c, m)], axis=0)
    sub = lax.broadcasted_iota(I32, (c, w), 0) & 7
    b3 = b.reshape(c // 8, 8, w)

    def bcast(j):
        return jnp.broadcast_to(b3[:, j:j + 1, :], (c // 8, 8, w)).reshape(c, w)

    if m == 8:
        return bcast(off)
    if m == 4:
        return jnp.where(sub < 4, bcast(off), bcast(4 + off))
    assert m == 2
    if reverse:
        return jnp.where((sub & 1) == 1, b, pltpu.roll(b, c - 1, 0))
    return jnp.where((sub & 1) == 0, b, pltpu.roll(b, 1, 0))


def _hgrn_gates(q, z, v, loglb, log1mlb, onemlb, cum, reverse):
    c = HGRN_CHUNK
    log_sig = jnp.minimum(z, 0.0) - jnp.log1p(jnp.exp(-jnp.abs(z)))
    bb = log1mlb + log_sig
    mx = jnp.maximum(loglb, bb)
    lf = mx + jnp.log1p(jnp.exp(-jnp.abs(loglb - bb)))
    kk = onemlb * jax.nn.sigmoid(-z)
    b = _dot_exact_lhs(cum, lf)
    b_end = b[0:1] if reverse else b[c - 1:c]
    qt = (q * jnp.exp(b)).astype(BF16)
    kt = (kk * jnp.exp(b_end - b)).astype(BF16)
    return (q, kk, b), (qt, kt, v.astype(BF16))


class _HgrnDir:
    def __init__(self, f32_parts, bf16_parts, reverse):
        c = HGRN_CHUNK
        self.q, self.kk, self.b = f32_parts
        self.qt, self.kt, self.vb = bf16_parts
        self.reverse = reverse
        b_end = self.b[0:1] if reverse else self.b[c - 1:c]
        self.decay = jnp.exp(b_end)
        mid = c // 2 if reverse else c // 2 - 1
        self.rel = self.b - self.b[mid:mid + 1]
        self.span = jnp.max(jnp.abs(self.rel))

    def tree_decay_matrices(self):
        c = HGRN_CHUNK
        q, kk, b = self.q, self.kk, self.b
        row = lax.broadcasted_iota(I32, (c, 1), 0)
        ti = lax.broadcasted_iota(I32, (c, c), 0)
        si = lax.broadcasted_iota(I32, (c, c), 1)
        qb = q.astype(BF16)
        kb = kk.astype(BF16)
        heads = [slice(h * HGRN_DK, (h + 1) * HGRN_DK) for h in range(HGRN_HEADS)]
        acc = [jnp.where(ti == si, _dot_nt(qb[:, sl], kb[:, sl]), 0.0) for sl in heads]
        for m in HGRN_LEVELS:
            r = _boundary_rows(b, m, self.reverse)
            upper = (row & (m - 1)) >= (m // 2)
            q_side = jnp.logical_not(upper) if self.reverse else upper
            e = jnp.exp(jnp.where(q_side, b - r, r - b))
            qf = jnp.where(q_side, q * e, 0.0).astype(BF16)
            kf = jnp.where(q_side, 0.0, kk * e).astype(BF16)
            same_block = (ti & -m) == (si & -m)
            for h, sl in enumerate(heads):
                acc[h] = acc[h] + jnp.where(same_block, _dot_nt(qf[:, sl], kf[:, sl]), 0.0)
        return acc

    def midpoint_decay_matrices(self):
        c = HGRN_CHUNK
        ti = lax.broadcasted_iota(I32, (c, c), 0)
        si = lax.broadcasted_iota(I32, (c, c), 1)
        qm = (self.q * jnp.exp(self.rel)).astype(BF16)
        km = (self.kk * jnp.exp(-self.rel)).astype(BF16)
        causal = (si >= ti) if self.reverse else (si <= ti)
        return [jnp.where(causal, _dot_nt(qm[:, h * HGRN_DK:(h + 1) * HGRN_DK], km[:, h * HGRN_DK:(h + 1) * HGRN_DK]), 0.0)
                for h in range(HGRN_HEADS)]

    def outputs(self, a_heads, st_ref, d):
        outs = []
        for h in range(HGRN_HEADS):
            sl = slice(h * HGRN_DK, (h + 1) * HGRN_DK)
            st = st_ref[d, h]
            o = _dot_nt(self.qt[:, sl], st.astype(BF16)) + _dot(a_heads[h].astype(BF16), self.vb[:, sl])
            st_ref[d, h] = st * self.decay[:, sl] + _dot_tn(self.vb[:, sl], self.kt[:, sl])
            outs.append(o)
        return jnp.concatenate(outs, axis=1)


def _hgrn_kernel(hq_ref, ff_ref, fb_ref, hi_ref, hg_ref, lbp_ref, gn_ref, mall_ref, ones_ref, s0_ref,
                 rec_ref, sfin_ref, st_ref, of_ref, ob_ref, gf_ref, gb_ref, a_ref, *, t):
    c = HGRN_CHUNK
    n = t // c
    st_ref[...] = s0_ref[0]

    def chunk_rows(ci):
        return pl.ds(pl.multiple_of(ci * c, c), c), pl.ds(pl.multiple_of((n - 1 - ci) * c, c), c)

    def gates_to(slot, ci):
        rf, rb = chunk_rows(ci)
        for d, (rows, f_ref) in enumerate(((rf, ff_ref), (rb, fb_ref))):
            f32_parts, bf16_parts = _hgrn_gates(
                hq_ref[rows, :], f_ref[rows, :], hi_ref[rows, :], lbp_ref[d, 0:1, :], lbp_ref[d, 1:2, :],
                lbp_ref[d, 2:3, :], mall_ref[d].astype(BF16), d == 1)
            for j in range(3):
                gf_ref[slot, d, j] = f32_parts[j]
                gb_ref[slot, d, j] = bf16_parts[j]

    gates_to(0, 0)

    def body(ci, carry):
        slot = ci & 1
        rf, rb = chunk_rows(ci)
        fwd = _HgrnDir([gf_ref[slot, 0, j] for j in range(3)], [gb_ref[slot, 0, j] for j in range(3)], False)
        bwd = _HgrnDir([gf_ref[slot, 1, j] for j in range(3)], [gb_ref[slot, 1, j] for j in range(3)], True)
        for d, hd in enumerate((fwd, bwd)):
            for h, a in enumerate(hd.midpoint_decay_matrices()):
                a_ref[d, h] = a
        gates_to(1 - slot, jnp.minimum(ci + 1, n - 1))

        @pl.when(jnp.maximum(fwd.span, bwd.span) > HGRN_SAFE_RANGE)
        def _():
            for d, hd in enumerate((fwd, bwd)):
                for h, a in enumerate(hd.tree_decay_matrices()):
                    a_ref[d, h] = a

        of_ref[rf, :] = fwd.outputs([a_ref[0, h] for h in range(HGRN_HEADS)], st_ref, 0)
        ob_ref[rb, :] = bwd.outputs([a_ref[1, h] for h in range(HGRN_HEADS)], st_ref, 1)
        return carry

    lax.fori_loop(0, n, body, 0)
    sfin_ref[0] = st_ref[...]
    o = of_ref[...] + ob_ref[...]
    ms = _dot_exact_rhs(o * o, ones_ref[...].astype(BF16)) * (1.0 / HGRN_DK)
    o = o * lax.rsqrt(ms + GN_EPS) * gn_ref[...]
    rec_ref[...] = (o * _silu(hg_ref[...])).astype(rec_ref.dtype)


def _hgrn(proj, row0, b, t, lbp, gn_row, s0t, name):
    base = row0 // t
    m_all = jnp.asarray(_hgrn_tables())
    ones_bd = jnp.asarray(np.kron(np.eye(HGRN_HEADS), np.ones((HGRN_DK, HGRN_DK))).astype(np.float32))

    def col(cstart):
        return pl.BlockSpec((t, HGRN_WIDTH), lambda bb, cb=(cstart - MIX_IN_WIDTH) // HGRN_WIDTH: (base + bb, cb))

    const2 = lambda bb: (0, 0)
    const3 = lambda bb: (0, 0, 0)
    st_shape = (2, HGRN_HEADS, HGRN_DK, HGRN_DK)
    body = functools.partial(_hgrn_kernel, t=t)
    return pl.pallas_call(
        body,
        grid=(b,),
        in_specs=[
            col(COL_HQ), col(COL_FF), col(COL_FB), col(COL_HI), col(COL_HG),
            pl.BlockSpec((2, 3, HGRN_WIDTH), const3),
            pl.BlockSpec((1, HGRN_WIDTH), const2),
            pl.BlockSpec(m_all.shape, const3),
            pl.BlockSpec(ones_bd.shape, const2),
            pl.BlockSpec((1,) + st_shape, lambda bb: (bb, 0, 0, 0, 0)),
        ],
        out_specs=[
            pl.BlockSpec((t, HGRN_WIDTH), lambda bb: (bb, 0)),
            pl.BlockSpec((1,) + st_shape, lambda bb: (bb, 0, 0, 0, 0)),
        ],
        out_shape=[
            jax.ShapeDtypeStruct((b * t, HGRN_WIDTH), BF16),
            jax.ShapeDtypeStruct((b,) + st_shape, F32),
        ],
        scratch_shapes=[
            pltpu.VMEM(st_shape, F32),
            pltpu.VMEM((t, HGRN_WIDTH), F32),
            pltpu.VMEM((t, HGRN_WIDTH), F32),
            pltpu.VMEM((2, 2, 3, HGRN_CHUNK, HGRN_WIDTH), F32),
            pltpu.VMEM((2, 2, 3, HGRN_CHUNK, HGRN_WIDTH), BF16),
            pltpu.VMEM((2, HGRN_HEADS, HGRN_CHUNK, HGRN_CHUNK), F32),
        ],
        compiler_params=_cparams(("arbitrary",)),
        name=name,
    )(proj, proj, proj, proj, proj, lbp, gn_row, m_all, ones_bd, s0t)


def _outproj_kernel(*refs, n_ctx_tiles, n_mix):
    mix = refs[:3 * n_mix]
    refs = refs[3 * n_mix:]
    xs = refs[:-14]
    (mod_ref, w_ref, g_ref, b_ref, rw_ref, rb_ref, x1_ref, hp_ref, meta_ref, gate_ref, cnt_ref, wb_ref, tri_ref,
     run_ref) = refs[-14:]
    tm = x1_ref.shape[0]
    is_ctx = pl.program_id(0) < n_ctx_tiles
    x_in = jnp.where(is_ctx, xs[0][...], xs[1][...]) if len(xs) == 2 else xs[0][...]
    if n_mix == 2:
        attn, four, rec = [jnp.where(is_ctx, mix[2 * j][...], mix[2 * j + 1][...]) for j in range(3)]
    else:
        attn, four, rec = [r[...] for r in mix]

    @pl.when(pl.program_id(0) == 0)
    def _():
        wb_ref[...] = w_ref[0].astype(BF16)
        r = lax.broadcasted_iota(I32, (tm, tm), 0)
        c = lax.broadcasted_iota(I32, (tm, tm), 1)
        tri_ref[...] = jnp.where(r < c, 1.0, 0.0).astype(BF16)
        run_ref[...] = jnp.zeros_like(run_ref)

    out = _dot(attn, wb_ref[0:ATTN_WIDTH, :])
    out = out + _dot(four, wb_ref[ATTN_WIDTH:ATTN_WIDTH + FOURIER_WIDTH, :])
    out = out + _dot(rec, wb_ref[ATTN_WIDTH + FOURIER_WIDTH:, :])
    gate1 = mod_ref[0, 2:3, :]
    y = DEEPNORM_ALPHA * x_in + gate1 * out
    x1 = _ln_plain(y, LN_EPS) * g_ref[...] + b_ref[...]
    x1_ref[...] = x1
    h2 = _ln_plain(x1, ADA_EPS) * (1.0 + mod_ref[0, 4:5, :]) + mod_ref[0, 3:4, :]
    hp_ref[...] = _pack_bf16_pair(h2[:, :HALF_D], h2[:, HALF_D:])

    h_hi = h2.astype(BF16)
    h_lo = (h2 - h_hi.astype(F32)).astype(BF16)
    rwt = rw_ref[...]
    w_hi = rwt.astype(BF16)
    w_lo = (rwt - w_hi.astype(F32)).astype(BF16)
    scores = jax.nn.sigmoid(_dot_nt(w_hi, h_hi) + _dot_nt(w_hi, h_lo) + _dot_nt(w_lo, h_hi))
    remaining = scores + rb_ref[...]
    eidx = lax.broadcasted_iota(I32, scores.shape, 0).astype(F32)
    chosen = jnp.zeros(scores.shape, jnp.bool_)
    picks = []
    for _ in range(TOP_K):
        mx = jnp.max(remaining, axis=0, keepdims=True)
        first = jnp.min(jnp.where(remaining == mx, eidx, float(N_EXPERTS)), axis=0, keepdims=True)
        pick = eidx == first
        picks.append((pick, first))
        chosen = jnp.logical_or(chosen, pick)
        remaining = jnp.where(pick, -jnp.inf, remaining)
    sel = jnp.where(chosen, scores, 0.0)
    gates = sel / jnp.sum(sel, axis=0, keepdims=True) * ROUTED_SCALE

    onehot = jnp.where(chosen, 1.0, 0.0)
    rank = run_ref[...] + _dot(onehot.astype(BF16), tri_ref[...])
    run_ref[...] += jnp.sum(onehot, axis=1, keepdims=True)
    cnt_ref[...] = run_ref[...]

    ids, rks, gks = [], [], []
    for pick, first in picks:
        ids.append(first.astype(I32))
        rks.append(jnp.sum(jnp.where(pick, rank, 0.0), axis=0, keepdims=True).astype(I32))
        gks.append(jnp.sum(jnp.where(pick, gates, 0.0), axis=0, keepdims=True))
    meta_ref[...] = jnp.concatenate(ids + rks, axis=0)
    gate_ref[...] = jnp.concatenate(gks, axis=0)


def _outproj(attn, four, rec, x, mod_l, w_out, layer, g1, b1, rw, rb, lay, tm):
    n_tiles = lay.n // tm
    n_ctx_tiles = lay.n_ctx // tm
    row = lambda i: (i, 0)
    const = lambda i: (0, 0)
    xs = x if isinstance(x, tuple) else (x,)
    return pl.pallas_call(
        functools.partial(_outproj_kernel, n_ctx_tiles=n_ctx_tiles, n_mix=len(attn)),
        grid=(n_tiles,),
        in_specs=[
            *_group_specs(len(attn), tm, ATTN_WIDTH, n_ctx_tiles),
            *_group_specs(len(four), tm, FOURIER_WIDTH, n_ctx_tiles),
            *_group_specs(len(rec), tm, HGRN_WIDTH, n_ctx_tiles),
            *_group_specs(len(xs), tm, D_MODEL, n_ctx_tiles),
            pl.BlockSpec((1, N_MOD, D_MODEL), lambda i: (lay.cond_row(i, tm), 0, 0)),
            pl.BlockSpec((1, D_MODEL, D_MODEL), lambda i: (layer, 0, 0)),
            pl.BlockSpec((1, D_MODEL), const),
            pl.BlockSpec((1, D_MODEL), const),
            pl.BlockSpec((N_EXPERTS, D_MODEL), const),
            pl.BlockSpec((N_EXPERTS, 1), const),
        ],
        out_specs=[
            pl.BlockSpec((tm, D_MODEL), row),
            pl.BlockSpec((tm, HALF_D), row),
            pl.BlockSpec((2 * TOP_K, tm), lambda i: (0, i)),
            pl.BlockSpec((TOP_K, tm), lambda i: (0, i)),
            pl.BlockSpec((N_EXPERTS, 1), const),
        ],
        out_shape=[
            jax.ShapeDtypeStruct((lay.n, D_MODEL), F32),
            jax.ShapeDtypeStruct((lay.n, HALF_D), I32),
            jax.ShapeDtypeStruct((2 * TOP_K, lay.n), I32),
            jax.ShapeDtypeStruct((TOP_K, lay.n), F32),
            jax.ShapeDtypeStruct((N_EXPERTS, 1), F32),
        ],
        scratch_shapes=[
            pltpu.VMEM((D_MODEL, D_MODEL), BF16),
            pltpu.VMEM((tm, tm), BF16),
            pltpu.VMEM((N_EXPERTS, 1), F32),
        ],
        compiler_params=_cparams(("arbitrary",)),
        name="outproj_router" + lay.tag,
    )(*attn, *four, *rec, *xs, mod_l, w_out, g1, b1, rw, rb)


def _sc_workers():
    info = plsc.get_sparse_core_info()
    return info.num_cores, info.num_cores * info.num_subcores


def _sc_scatter_rows(rows, pos_b, r_out, tag=""):
    nc, nw = _sc_workers()
    n, w = rows.shape
    nbt, copies, _ = pos_b.shape
    assert nbt * SC_BATCH == n and nbt % (2 * nw) == 0
    per_w = nbt // nw
    mesh = plsc.VectorSubcoreMesh(core_axis_name="c", subcore_axis_name="s")

    @functools.partial(
        pl.kernel, mesh=mesh, out_type=jax.ShapeDtypeStruct((r_out, w), rows.dtype),
        scratch_types=[pltpu.VMEM((copies, SC_BATCH), I32), pltpu.VMEM((copies, SC_BATCH), I32),
                       pltpu.VMEM((SC_BATCH, w), rows.dtype), pltpu.VMEM((SC_BATCH, w), rows.dtype),
                       pltpu.SemaphoreType.DMA, pltpu.SemaphoreType.DMA,
                       pltpu.SemaphoreType.DMA, pltpu.SemaphoreType.DMA],
        name="sc_dispatch" + tag)
    def k(rows_hbm, pos_hbm, out_hbm, idx_a, idx_b, rows_a, rows_b, sem_ra, sem_rb, sem_sa, sem_sb):
        wid = lax.axis_index("s") * nc + lax.axis_index("c")
        first = wid * per_w

        def reads(j, idx_v, rows_v, sem):
            bt = first + j
            return (pltpu.make_async_copy(pos_hbm.at[bt], idx_v, sem),
                    pltpu.make_async_copy(rows_hbm.at[pl.ds(bt * SC_BATCH, SC_BATCH)], rows_v, sem))

        def scatters(idx_v, rows_v, sem):
            return [pltpu.make_async_copy(rows_v, out_hbm.at[idx_v.at[q]], sem) for q in range(copies)]

        def start(descs):
            for d in descs:
                d.start()

        def wait(descs):
            for d in descs:
                d.wait()

        start(reads(0, idx_a, rows_a, sem_ra))

        @pl.loop(0, per_w // 2)
        def _(p):
            j0 = 2 * p
            j1 = j0 + 1

            @pl.when(p > 0)
            def _():
                wait(scatters(idx_b, rows_b, sem_sb))

            start(reads(j1, idx_b, rows_b, sem_rb))
            wait(reads(j0, idx_a, rows_a, sem_ra))
            start(scatters(idx_a, rows_a, sem_sa))
            wait(reads(j1, idx_b, rows_b, sem_rb))
            start(scatters(idx_b, rows_b, sem_sb))
            wait(scatters(idx_a, rows_a, sem_sa))

            @pl.when(p + 1 < per_w // 2)
            def _():
                start(reads(j0 + 2, idx_a, rows_a, sem_ra))

        wait(scatters(idx_b, rows_b, sem_sb))

    return k(rows, pos_b)


def _sc_gather_rows(table, idx, tag=""):
    nc, nw = _sc_workers()
    r = idx.shape[0]
    w = table.shape[1]
    assert r % (2 * nw * SC_BATCH) == 0
    per_w = r // nw
    nb = per_w // SC_BATCH
    mesh = plsc.VectorSubcoreMesh(core_axis_name="c", subcore_axis_name="s")

    @functools.partial(
        pl.kernel, mesh=mesh, out_type=jax.ShapeDtypeStruct((r, w), table.dtype),
        scratch_types=[pltpu.VMEM((per_w,), I32),
                       pltpu.VMEM((SC_BATCH, w), table.dtype), pltpu.VMEM((SC_BATCH, w), table.dtype),
                       pltpu.SemaphoreType.DMA, pltpu.SemaphoreType.DMA,
                       pltpu.SemaphoreType.DMA, pltpu.SemaphoreType.DMA],
        name="sc_combine" + tag)
    def k(table_hbm, idx_hbm, out_hbm, idx_v, rows_a, rows_b, sem_ga, sem_gb, sem_wa, sem_wb):
        wid = lax.axis_index("s") * nc + lax.axis_index("c")
        base = wid * per_w
        pltpu.sync_copy(idx_hbm.at[pl.ds(base, per_w)], idx_v)

        def gather(j, rows_v, sem):
            return pltpu.make_async_copy(table_hbm.at[idx_v.at[pl.ds(j * SC_BATCH, SC_BATCH)]], rows_v, sem)

        def write(j, rows_v, sem):
            return pltpu.make_async_copy(rows_v, out_hbm.at[pl.ds(base + j * SC_BATCH, SC_BATCH)], sem)

        gather(0, rows_a, sem_ga).start()

        @pl.loop(0, nb // 2)
        def _(p):
            j0 = 2 * p
            j1 = j0 + 1

            @pl.when(p > 0)
            def _():
                write(j1 - 2, rows_b, sem_wb).wait()

            gather(j1, rows_b, sem_gb).start()
            gather(j0, rows_a, sem_ga).wait()
            write(j0, rows_a, sem_wa).start()
            gather(j1, rows_b, sem_gb).wait()
            write(j1, rows_b, sem_wb).start()
            write(j0, rows_a, sem_wa).wait()

            @pl.when(p + 1 < nb // 2)
            def _():
                gather(j0 + 2, rows_a, sem_ga).start()

        write(nb - 1, rows_b, sem_wb).wait()

    return k(table, idx)


def _experts_kernel(te_ref, na_ref, x_ref, w1_ref, w3_ref, w2_ref, o_ref, w1b_ref, w3b_ref, w2b_ref):
    del te_ref

    @pl.when(pl.program_id(0) < na_ref[0])
    def _():
        w1b_ref[...] = w1_ref[0, 0].astype(BF16)
        w3b_ref[...] = w3_ref[0, 0].astype(BF16)
        w2b_ref[...] = w2_ref[0, 0].astype(BF16)
        lo, hi = _unpack_bf16_pair(x_ref[...])
        lo = lo.astype(BF16)
        hi = hi.astype(BF16)
        a = _dot(lo, w1b_ref[0:HALF_D, :]) + _dot(hi, w1b_ref[HALF_D:, :])
        b = _dot(lo, w3b_ref[0:HALF_D, :]) + _dot(hi, w3b_ref[HALF_D:, :])
        y = _dot((_silu(a) * b).astype(BF16), w2b_ref[...])
        o_ref[...] = _pack_bf16_pair(y[:, :HALF_D], y[:, HALF_D:])


def _experts(xs, tile_expert, n_active, w1, w3, w2, layer, tm, tag):
    r = xs.shape[0]
    n_tiles = r // tm

    def xmap(j, te, na):
        return (jnp.minimum(j, na[0] - 1), 0)

    def wmap(j, te, na):
        return (layer, te[jnp.minimum(j, na[0] - 1)], 0, 0)

    grid_spec = pltpu.PrefetchScalarGridSpec(
        num_scalar_prefetch=2,
        grid=(n_tiles,),
        in_specs=[
            pl.BlockSpec((tm, HALF_D), xmap),
            pl.BlockSpec((1, 1, D_MODEL, EXPERT_FF), wmap),
            pl.BlockSpec((1, 1, D_MODEL, EXPERT_FF), wmap),
            pl.BlockSpec((1, 1, EXPERT_FF, D_MODEL), wmap),
        ],
        out_specs=pl.BlockSpec((tm, HALF_D), xmap),
        scratch_shapes=[
            pltpu.VMEM((D_MODEL, EXPERT_FF), BF16),
            pltpu.VMEM((D_MODEL, EXPERT_FF), BF16),
            pltpu.VMEM((EXPERT_FF, D_MODEL), BF16),
        ],
    )
    return pl.pallas_call(
        _experts_kernel,
        grid_spec=grid_spec,
        out_shape=jax.ShapeDtypeStruct((r, HALF_D), I32),
        compiler_params=_cparams(("arbitrary",)),
        name="experts" + tag,
    )(tile_expert, n_active, xs, w1, w3, w2)


def _combine_kernel(yp_ref, gate_ref, hp_ref, sw1_ref, sw3_ref, sw2_ref, x_ref, mod_ref, g_ref, b_ref, *refs):
    o_ref, w1b_ref, w3b_ref, w2b_ref = refs[-4:]

    @pl.when(pl.program_id(0) == 0)
    def _():
        w1b_ref[...] = sw1_ref[...].astype(BF16)
        w3b_ref[...] = sw3_ref[...].astype(BF16)
        w2b_ref[...] = sw2_ref[...].astype(BF16)

    lo, hi = _unpack_bf16_pair(hp_ref[...])
    lo = lo.astype(BF16)
    hi = hi.astype(BF16)
    a = _dot(lo, w1b_ref[0:HALF_D, :]) + _dot(hi, w1b_ref[HALF_D:, :])
    b = _dot(lo, w3b_ref[0:HALF_D, :]) + _dot(hi, w3b_ref[HALF_D:, :])
    shared = _dot((_silu(a) * b).astype(BF16), w2b_ref[...])
    acc_lo = shared[:, :HALF_D]
    acc_hi = shared[:, HALF_D:]
    gates = gate_ref[...]
    for k in range(TOP_K):
        ylo, yhi = _unpack_bf16_pair(yp_ref[k])
        gk = gates[:, k:k + 1]
        acc_lo = acc_lo + gk * ylo
        acc_hi = acc_hi + gk * yhi
    moe = jnp.concatenate([acc_lo, acc_hi], axis=1)
    y = DEEPNORM_ALPHA * x_ref[...] + mod_ref[0, 5:6, :] * moe
    o_ref[...] = _ln_plain(y, LN_EPS) * g_ref[...] + b_ref[...]


def _combine(yp, gate8, hp, sw1, sw3, sw2, x1, mod_l, g2, b2, lay, tm, tile0, out_tile0, out_rows, prev):
    n_tiles = yp.shape[1] // tm
    row = lambda i: (i + tile0, 0)
    const = lambda i: (0, 0)
    operands = [yp, gate8, hp, sw1, sw3, sw2, x1, mod_l, g2, b2]
    in_specs = [
        pl.BlockSpec((TOP_K, tm, HALF_D), lambda i: (0, i, 0)),
        pl.BlockSpec((tm, TOP_K), row),
        pl.BlockSpec((tm, HALF_D), row),
        pl.BlockSpec((D_MODEL, EXPERT_FF), const),
        pl.BlockSpec((D_MODEL, EXPERT_FF), const),
        pl.BlockSpec((EXPERT_FF, D_MODEL), const),
        pl.BlockSpec((tm, D_MODEL), row),
        pl.BlockSpec((1, N_MOD, D_MODEL), lambda i: (lay.cond_row(i + tile0, tm), 0, 0)),
        pl.BlockSpec((1, D_MODEL), const),
        pl.BlockSpec((1, D_MODEL), const),
    ]
    aliases = {}
    if prev is not None:
        aliases = {len(operands): 0}
        operands.append(prev)
        in_specs.append(pl.BlockSpec(memory_space=pl.ANY))
    return pl.pallas_call(
        _combine_kernel,
        grid=(n_tiles,),
        in_specs=in_specs,
        out_specs=pl.BlockSpec((tm, D_MODEL), lambda i: (i + out_tile0, 0)),
        out_shape=jax.ShapeDtypeStruct((out_rows, D_MODEL), F32),
        input_output_aliases=aliases,
        scratch_shapes=[
            pltpu.VMEM((D_MODEL, EXPERT_FF), BF16),
            pltpu.VMEM((D_MODEL, EXPERT_FF), BF16),
            pltpu.VMEM((EXPERT_FF, D_MODEL), BF16),
        ],
        compiler_params=_cparams(("arbitrary",)),
        name="combine_norm" + lay.tag,
    )(*operands)


def _moe_dispatch(hp, meta, counts, lay, tile):
    n = lay.n
    r_max = n * TOP_K + N_EXPERTS * tile
    n_tiles = r_max // tile
    cnt = counts.reshape(N_EXPERTS).astype(I32)
    padded = ((cnt + tile - 1) // tile) * tile
    ends = jnp.cumsum(padded)
    offsets = ends - padded
    idx8 = meta[:TOP_K]
    base8 = jnp.sum(jnp.where(idx8[:, :, None] == jnp.arange(N_EXPERTS, dtype=I32), offsets, 0), axis=-1)
    pos = (base8 + meta[TOP_K:]).astype(I32)
    tile_start = jnp.arange(n_tiles, dtype=I32) * tile
    tile_expert = jnp.minimum(jnp.sum(tile_start[:, None] >= ends[None, :], axis=1), N_EXPERTS - 1).astype(I32)
    n_active = (ends[-1] // tile).astype(I32).reshape(1)
    pos_b = pos.reshape(TOP_K, n // SC_BATCH, SC_BATCH).transpose(1, 0, 2)
    xs = _sc_scatter_rows(hp, pos_b, r_max, lay.tag)
    return xs, tile_expert, n_active, pos


def _moe_combine(ys, pos, gate8, hp, sw1, sw3, sw2, x1, mod_l, g2, b2, lay, split_out):
    tm = TOKEN_TILE
    rows = lay.n_ctx
    assert lay.n_lat % rows == 0
    part_tiles = rows // tm
    gate_t = gate8.T
    out_all = out_lat = out_ctx = None
    for p in range(lay.n // rows):
        yp = _sc_gather_rows(ys, pos[:, p * rows:(p + 1) * rows].reshape(rows * TOP_K), lay.tag)
        yp = yp.reshape(TOP_K, rows, HALF_D)
        args = (yp, gate_t, hp, sw1, sw3, sw2, x1, mod_l, g2, b2, lay, tm, p * part_tiles)
        if not split_out:
            out_all = _combine(*args, p * part_tiles, lay.n, out_all)
        elif p == 0:
            out_ctx = _combine(*args, 0, lay.n_ctx, None)
        else:
            out_lat = _combine(*args, (p - 1) * part_tiles, lay.n_lat, out_lat)
    return (out_ctx, out_lat) if split_out else out_all


def kernel(x_prompt, x_sample, cache_k, cache_v, state_hgrn, c, c_ctx, w_ada, b_ada, w_in, w_out, attn_sink, hgrn_lb, hgrn_norm, ln1_g, ln1_b, ln2_g, ln2_b, router_w, router_b, moe_w1, moe_w3, moe_w2, shared_w1, shared_w3, shared_w2):
    b_ctx, t_ctx, _ = x_prompt.shape
    b_lat, t_lat, _ = x_sample.shape
    past = cache_k.shape[2]
    tm = TOKEN_TILE
    assert 1 + b_lat <= COND_ROWS
    lay = _Layout(b_ctx, t_ctx, b_lat, t_lat)
    assert lay.n_ctx % tm == 0 and t_lat % tm == 0 and lay.n_ctx % t_lat == 0

    cond = jnp.concatenate([c_ctx[None, :], c, jnp.zeros((COND_ROWS - 1 - b_lat, D_MODEL), F32)], axis=0)
    mod = _adaln(cond, w_ada, b_ada).reshape(DEPTH, COND_ROWS, N_MOD, D_MODEL)

    lb_all = jnp.cumsum(jax.nn.softmax(hgrn_lb.astype(F32), axis=0), axis=0)
    lb_all = lb_all - lb_all[:1]
    lbp = jnp.stack([jnp.log(lb_all), jnp.log1p(-lb_all), 1.0 - lb_all], axis=2)

    cos_t, sin_t = _rope_tables(lay, tm)
    zero_state = jnp.zeros((b_ctx, 2, HGRN_HEADS, HGRN_DK, HGRN_DK), F32)

    def layer(l, lay, x, split_out):
        lat = slice(lay.lat_first, lay.lat_first + lay.b_lat)
        outs = _inproj(x, mod[l], w_in, l, cos_t, sin_t, lay, tm)
        proj, proj_h = outs[0], outs[1]
        sink_l = attn_sink[l].reshape(1, N_HEADS)
        gn_row = jnp.tile(hgrn_norm[l], HGRN_HEADS).reshape(1, HGRN_WIDTH)
        attn, four, rec, extras = [], [], [], None
        if lay.b_ctx:
            attn.append(_attn_context(proj, sink_l, lay))
            four.append(_fourier(proj, 0, lay.b_ctx, t_ctx, t_ctx, "fourier_ctx"))
            rec_c, s_fin = _hgrn(proj_h, 0, lay.b_ctx, t_ctx, lbp[l], gn_row, zero_state, "hgrn_ctx")
            rec.append(rec_c)
            extras = (outs[2], outs[3], s_fin)
        attn.append(_attn_latent(proj, cache_k[lat, l].reshape(lay.b_lat, past, KV_WIDTH),
                                 cache_v[lat, l].reshape(lay.b_lat, past, KV_WIDTH), sink_l, lay))
        four.append(_fourier(proj, lay.n_ctx, lay.b_lat, t_lat, min(t_lat, 512), "fourier_lat" + lay.tag))
        s0t = jnp.swapaxes(state_hgrn[lat, l].astype(F32), -1, -2)
        rec.append(_hgrn(proj_h, lay.n_ctx, lay.b_lat, t_lat, lbp[l], gn_row, s0t, "hgrn_lat" + lay.tag)[0])
        x1, hp, meta, gate8, counts = _outproj(
            tuple(attn), tuple(four), tuple(rec), x, mod[l], w_out, l, ln1_g[l].reshape(1, -1),
            ln1_b[l].reshape(1, -1), router_w[l].T, router_b[l].reshape(-1, 1), lay, tm)
        xs, tile_expert, n_active, pos = _moe_dispatch(hp, meta, counts, lay, EXPERT_TILE)
        ys = _experts(xs, tile_expert, n_active, moe_w1, moe_w3, moe_w2, l, EXPERT_TILE, lay.tag)
        x = _moe_combine(ys, pos, gate8, hp, shared_w1[l], shared_w3[l], shared_w2[l], x1, mod[l],
                         ln2_g[l].reshape(1, -1), ln2_b[l].reshape(1, -1), lay, split_out)
        return x, extras

    x = (x_prompt.reshape(lay.n_ctx, D_MODEL), x_sample.reshape(lay.n_lat, D_MODEL))
    ks_out, vs_out, ss_out = [], [], []
    for l in range(DEPTH):
        x, (k_new, v_new, s_fin) = layer(l, lay, x, split_out=(l == DEPTH - 1))
        ks_out.append(k_new.reshape(b_ctx, t_ctx, N_KV_HEADS, HEAD_DIM))
        vs_out.append(v_new.reshape(b_ctx, t_ctx, N_KV_HEADS, HEAD_DIM))
        ss_out.append(jnp.swapaxes(s_fin, -1, -2))

    y_prompt = x[0].reshape(b_ctx, t_ctx, D_MODEL)
    y_sample = x[1].reshape(b_lat, t_lat, D_MODEL)
    new_cache_k = jnp.stack(ks_out, axis=1)
    new_cache_v = jnp.stack(vs_out, axis=1)
    new_state = jnp.stack(ss_out, axis=1).astype(x_prompt.dtype)
    return (y_prompt, y_sample, new_cache_k, new_cache_v, new_state)
```

```python
import functools
import math

import numpy as np
import jax
import jax.numpy as jnp
from jax import lax
from jax.experimental import pallas as pl
from jax.experimental.pallas import tpu as pltpu
from jax.experimental.pallas import tpu_sc as plsc

F32 = jnp.float32
BF16 = jnp.bfloat16
I32 = jnp.int32

D_MODEL = 1024
HALF_D = D_MODEL // 2
DEPTH = 2
GRID_W = 64
ROPE_BASE = 10000.0
HEAD_DIM = 64
ATTN_WIDTH = 512
N_HEADS = 8
N_KV_HEADS = 2
KV_GROUP = 4
KV_WIDTH = N_KV_HEADS * HEAD_DIM
WINDOW = 128
ATTN_BLOCK = 128
FOURIER_WIDTH = 256
FOURIER_GROUPS = 4
HGRN_WIDTH = 256
HGRN_HEADS = 4
HGRN_DK = 64
HGRN_CHUNK = 64
IN_WIDTH = 2304
N_EXPERTS = 64
TOP_K = 8
EXPERT_FF = 256
ROUTED_SCALE = 2.5
N_MOD = 6
LN_EPS = 1e-5
ADA_EPS = 1e-6
GN_EPS = 1e-6
DEEPNORM_ALPHA = (2 * DEPTH) ** 0.25

COL_Q = 0
COL_K = 512
COL_V = 640
COL_U = 768
COL_HQ = 1024
COL_FF = 1280
COL_FB = 1536
COL_HI = 1792
COL_HG = 2048
ROPE_COLS = COL_V
MIX_IN_WIDTH = COL_HQ

V7X_LANES = 128
COND_ROWS = 16
NEG_BIG = -1e30
TOKEN_TILE = 512
EXPERT_TILE = 1024
SC_BATCH = 64

VMEM_LIMIT = 56 * 1024 * 1024


def _cparams(sem):
    return pltpu.CompilerParams(dimension_semantics=sem, vmem_limit_bytes=VMEM_LIMIT)


def _dot(a, b):
    return jnp.dot(a, b, preferred_element_type=F32)


def _dot_nt(a, b):
    return lax.dot_general(a, b, (((1,), (1,)), ((), ())), preferred_element_type=F32)


def _dot_tn(a, b):
    return lax.dot_general(a, b, (((0,), (0,)), ((), ())), preferred_element_type=F32)


def _split3(x):
    hi = x.astype(BF16)
    r1 = x - hi.astype(F32)
    mid = r1.astype(BF16)
    lo = (r1 - mid.astype(F32)).astype(BF16)
    return hi, mid, lo


def _dot_exact_lhs(m_bf16, x):
    hi, mid, lo = _split3(x)
    return _dot(m_bf16, hi) + _dot(m_bf16, mid) + _dot(m_bf16, lo)


def _dot_exact_rhs(x, m_bf16):
    hi, mid, lo = _split3(x)
    return _dot(hi, m_bf16) + _dot(mid, m_bf16) + _dot(lo, m_bf16)


def _dot_hp(a, b):
    a_hi = a.astype(BF16)
    a_lo = (a - a_hi.astype(F32)).astype(BF16)
    b_hi = b.astype(BF16)
    b_lo = (b - b_hi.astype(F32)).astype(BF16)
    return _dot(a_hi, b_hi) + _dot(a_hi, b_lo) + _dot(a_lo, b_hi)


def _pack_bf16_pair(lo, hi):
    return lax.bitcast_convert_type(pltpu.pack_elementwise([lo, hi], packed_dtype=BF16), I32)


def _unpack_bf16_pair(w):
    u = lax.bitcast_convert_type(w, jnp.uint32)
    lo = pltpu.unpack_elementwise(u, index=0, packed_dtype=BF16, unpacked_dtype=F32)
    hi = pltpu.unpack_elementwise(u, index=1, packed_dtype=BF16, unpacked_dtype=F32)
    return lo, hi


def _ln_plain(x, eps):
    mu = jnp.mean(x, axis=-1, keepdims=True)
    xc = x - mu
    var = jnp.mean(xc * xc, axis=-1, keepdims=True)
    return xc * lax.rsqrt(var + eps)


def _silu(x):
    return x * jax.nn.sigmoid(x)


def _adaln_kernel(c_ref, w_ref, b_ref, o_ref):
    s = _silu(c_ref[...])
    o_ref[0] = _dot_hp(s, w_ref[0]) + b_ref[0]


def _adaln(cond, w_ada, b_ada):
    return pl.pallas_call(
        _adaln_kernel,
        grid=(DEPTH, N_MOD),
        in_specs=[
            pl.BlockSpec((COND_ROWS, D_MODEL), lambda l, j: (0, 0)),
            pl.BlockSpec((1, D_MODEL, D_MODEL), lambda l, j: (l, 0, j)),
            pl.BlockSpec((1, 1, D_MODEL), lambda l, j: (l, 0, j)),
        ],
        out_specs=pl.BlockSpec((1, COND_ROWS, D_MODEL), lambda l, j: (l, 0, j)),
        out_shape=jax.ShapeDtypeStruct((DEPTH, COND_ROWS, N_MOD * D_MODEL), F32),
        compiler_params=_cparams(("arbitrary", "arbitrary")),
        name="adaln",
    )(cond, w_ada, b_ada.reshape(DEPTH, 1, N_MOD * D_MODEL))


class _Layout:
    def __init__(self, b_ctx, t_ctx, b_lat, t_lat, lat_first=0, tag=""):
        self.b_ctx, self.t_ctx, self.b_lat, self.t_lat = b_ctx, t_ctx, b_lat, t_lat
        self.lat_first = lat_first
        self.n_ctx = b_ctx * t_ctx
        self.n_lat = b_lat * t_lat
        self.n = self.n_ctx + self.n_lat
        self.tag = tag

    def cond_row(self, tile, tm):
        n_ctx_tiles = self.n_ctx // tm
        per_batch = self.t_lat // tm
        return jnp.where(tile < n_ctx_tiles, 0, 1 + self.lat_first + (tile - n_ctx_tiles) // per_batch)


def _group_specs(n_arrays, tm, width, n_ctx_tiles):
    if n_arrays == 1:
        return [pl.BlockSpec((tm, width), lambda i: (i, 0))]
    return [pl.BlockSpec((tm, width), lambda i: (jnp.minimum(i, n_ctx_tiles - 1), 0)),
            pl.BlockSpec((tm, width), lambda i: (jnp.maximum(i - n_ctx_tiles, 0), 0))]


def _inproj_kernel(*refs, n_ctx_tiles):
    n_tail = 9 if n_ctx_tiles > 0 else 7
    xs = refs[:-n_tail]
    mod_ref, w_ref, cos_ref, sin_ref, oa_ref, oh_ref = refs[-n_tail:-n_tail + 6]
    wb_ref = refs[-1]

    @pl.when(pl.program_id(0) == 0)
    def _():
        wb_ref[...] = w_ref[0].astype(BF16)

    if len(xs) == 2:
        x = jnp.where(pl.program_id(0) < n_ctx_tiles, xs[0][...], xs[1][...])
    else:
        x = xs[0][...]
    shift = mod_ref[0, 0:1, :]
    scale = mod_ref[0, 1:2, :]
    h = (_ln_plain(x, ADA_EPS) * (1.0 + scale) + shift).astype(BF16)
    p = _dot(h, wb_ref[...])
    cos = cos_ref[...]
    sin = sin_ref[...]
    lane = lax.broadcasted_iota(I32, cos.shape, 1)
    first_half = (lane & 31) < 16
    for cb in range(ROPE_COLS // V7X_LANES):
        seg = p[:, cb * V7X_LANES:(cb + 1) * V7X_LANES]
        partner = jnp.where(first_half, pltpu.roll(seg, V7X_LANES - 16, 1), pltpu.roll(seg, 16, 1))
        oa_ref[:, cb * V7X_LANES:(cb + 1) * V7X_LANES] = (seg * cos + partner * sin).astype(BF16)
    oa_ref[:, ROPE_COLS:] = p[:, ROPE_COLS:MIX_IN_WIDTH].astype(BF16)
    oh_ref[...] = p[:, MIX_IN_WIDTH:]

    if n_ctx_tiles > 0:
        kc_ref, vc_ref = refs[-3], refs[-2]

        @pl.when(pl.program_id(0) < n_ctx_tiles)
        def _():
            kc_ref[...] = p[:, COL_K:COL_K + KV_WIDTH]
            vc_ref[...] = p[:, COL_V:COL_V + KV_WIDTH]


def _rope_tables(lay, tm):
    t = lay.t_lat
    pos = jnp.arange(t)
    row = (pos // GRID_W).astype(F32)
    col = (pos % GRID_W).astype(F32)
    n_freq = HEAD_DIM // 4
    inv = ROPE_BASE ** (-jnp.arange(n_freq, dtype=F32) / n_freq)
    ang_r = row[:, None] * inv
    ang_c = col[:, None] * inv
    ang = jnp.concatenate([ang_r, ang_r, ang_c, ang_c], axis=1)
    sign = jnp.concatenate([-jnp.ones(n_freq), jnp.ones(n_freq), -jnp.ones(n_freq), jnp.ones(n_freq)]).astype(F32)
    cos = jnp.cos(ang)
    sin = jnp.sin(ang) * sign
    cos = jnp.concatenate([jnp.ones((tm, HEAD_DIM), F32), cos], axis=0)
    sin = jnp.concatenate([jnp.zeros((tm, HEAD_DIM), F32), sin], axis=0)
    return jnp.tile(cos, (1, 2)), jnp.tile(sin, (1, 2))


def _inproj(x, mod_l, w_in, layer, cos_t, sin_t, lay, tm):
    n_tiles = lay.n // tm
    n_ctx_tiles = lay.n_ctx // tm
    per_batch = lay.t_lat // tm

    def tbl(i):
        return jnp.where(i < n_ctx_tiles, 0, 1 + (i - n_ctx_tiles) % per_batch)

    xs = x if isinstance(x, tuple) else (x,)
    kv_specs, kv_shapes = [], []
    if n_ctx_tiles > 0:
        kv_specs = [pl.BlockSpec((tm, KV_WIDTH), lambda i: (jnp.minimum(i, n_ctx_tiles - 1), 0))] * 2
        kv_shapes = [jax.ShapeDtypeStruct((lay.n_ctx, KV_WIDTH), F32)] * 2
    return pl.pallas_call(
        functools.partial(_inproj_kernel, n_ctx_tiles=n_ctx_tiles),
        grid=(n_tiles,),
        in_specs=[
            *_group_specs(len(xs), tm, D_MODEL, n_ctx_tiles),
            pl.BlockSpec((1, N_MOD, D_MODEL), lambda i: (lay.cond_row(i, tm), 0, 0)),
            pl.BlockSpec((1, D_MODEL, IN_WIDTH), lambda i: (layer, 0, 0), pipeline_mode=pl.Buffered(1)),
            pl.BlockSpec((tm, V7X_LANES), lambda i: (tbl(i), 0)),
            pl.BlockSpec((tm, V7X_LANES), lambda i: (tbl(i), 0)),
        ],
        out_specs=[pl.BlockSpec((tm, MIX_IN_WIDTH), lambda i: (i, 0)),
                   pl.BlockSpec((tm, IN_WIDTH - MIX_IN_WIDTH), lambda i: (i, 0))] + kv_specs,
        out_shape=[jax.ShapeDtypeStruct((lay.n, MIX_IN_WIDTH), BF16),
                   jax.ShapeDtypeStruct((lay.n, IN_WIDTH - MIX_IN_WIDTH), F32)] + kv_shapes,
        scratch_shapes=[pltpu.VMEM((D_MODEL, IN_WIDTH), BF16)],
        compiler_params=_cparams(("arbitrary",)),
        name="inproj" + lay.tag,
    )(*xs, mod_l, w_in, cos_t, sin_t)


def _attn_kernel(sink_ref, q_ref, *refs, n_local, has_ctx, t_total):
    o_ref = refs[-1]
    k_refs = refs[:n_local]
    v_refs = refs[n_local:2 * n_local]
    tq = q_ref.shape[0]
    scale = HEAD_DIM ** -0.5
    k_parts = [kr[...] for kr in k_refs]
    v_parts = [vr[...] for vr in v_refs]
    if has_ctx:
        k_parts.append(refs[2 * n_local][0].astype(k_parts[0].dtype))
        v_parts.append(refs[2 * n_local + 1][0].astype(v_parts[0].dtype))
    kall = (jnp.concatenate(k_parts, axis=0) if len(k_parts) > 1 else k_parts[0]).astype(F32)
    vall = (jnp.concatenate(v_parts, axis=0) if len(v_parts) > 1 else v_parts[0]).astype(F32)
    nk = kall.shape[0]
    k_sw = pltpu.roll(kall, HEAD_DIM, 1)
    v_sw = pltpu.roll(vall, HEAD_DIM, 1)
    lo_half = lax.broadcasted_iota(I32, (1, V7X_LANES), 1) < HEAD_DIM
    er = jnp.where(lax.broadcasted_iota(I32, (2 * nk, V7X_LANES), 0) < nk, 0, 1)
    el = jnp.where(lax.broadcasted_iota(I32, (2 * nk, V7X_LANES), 1) < HEAD_DIM, 0, 1)
    ones_blk = jnp.where(er == el, 1.0, 0.0).astype(BF16)
    if n_local > 1:
        i = pl.program_id(1)
        band = refs[-2][...]
        first_blk = jnp.where(i == 0, NEG_BIG, 0.0)
        last_blk = jnp.where(i == t_total // tq - 1, NEG_BIG, 0.0)

        def mask_local(sc):
            loc = sc[:, :n_local * tq] + band
            parts = [loc[:, :tq] + first_blk, loc[:, tq:(n_local - 1) * tq], loc[:, (n_local - 1) * tq:] + last_blk]
            return jnp.concatenate(parts + [sc[:, n_local * tq:]], axis=1)
    else:
        mask_local = None
    v2es, sinks, scores = [], [], []
    for g in range(N_KV_HEADS):
        k_own, k_oth = (kall, k_sw) if g == 0 else (k_sw, kall)
        v_own, v_oth = (vall, v_sw) if g == 0 else (v_sw, vall)
        k2 = jnp.concatenate([jnp.where(lo_half, k_own, 0.0), jnp.where(lo_half, 0.0, k_oth)], axis=0).astype(BF16)
        v2 = jnp.concatenate([jnp.where(lo_half, v_own, 0.0), jnp.where(lo_half, 0.0, v_oth)], axis=0).astype(BF16)
        v2es.append(jnp.concatenate([v2, ones_blk], axis=1))
        pairs = [2 * g, 2 * g + 1]
        qq = jnp.concatenate([q_ref[:, p * V7X_LANES:(p + 1) * V7X_LANES] for p in pairs], axis=0)
        qq = (qq.astype(F32) * scale).astype(BF16)
        sinks.append((jnp.concatenate([jnp.full((tq, 1), sink_ref[0, 2 * p], F32) for p in pairs], axis=0),
                      jnp.concatenate([jnp.full((tq, 1), sink_ref[0, 2 * p + 1], F32) for p in pairs], axis=0)))
        scores.append(_dot_nt(qq, k2))
    pes, maxes = [], []
    for g in range(N_KV_HEADS):
        s_a = scores[g][:, :nk]
        s_b = scores[g][:, nk:]
        if mask_local is not None:
            s_a = mask_local(s_a)
            s_b = mask_local(s_b)
        m_a = jnp.maximum(jnp.max(s_a, axis=1, keepdims=True), sinks[g][0])
        m_b = jnp.maximum(jnp.max(s_b, axis=1, keepdims=True), sinks[g][1])
        pes.append(jnp.concatenate([jnp.exp(s_a - m_a).astype(BF16), jnp.exp(s_b - m_b).astype(BF16)], axis=1))
        maxes.append((m_a, m_b))
    for g in range(N_KV_HEADS):
        acc = _dot(pes[g], v2es[g])
        (m_a, m_b), (sink_a, sink_b) = maxes[g], sinks[g]
        sink_term = jnp.where(lo_half, jnp.exp(sink_a - m_a), jnp.exp(sink_b - m_b))
        o = acc[:, :V7X_LANES] / (acc[:, V7X_LANES:] + sink_term)
        for j, p in enumerate([2 * g, 2 * g + 1]):
            o_ref[:, p * V7X_LANES:(p + 1) * V7X_LANES] = o[j * tq:(j + 1) * tq].astype(o_ref.dtype)


def _attn_context(proj, sink_l, lay):
    t = lay.t_ctx
    kb, vb = COL_K // KV_WIDTH, COL_V // KV_WIDTH
    body = functools.partial(_attn_kernel, n_local=1, has_ctx=False, t_total=t)
    return pl.pallas_call(
        body,
        grid=(lay.b_ctx,),
        in_specs=[
            pl.BlockSpec(memory_space=pltpu.SMEM),
            pl.BlockSpec((t, ATTN_WIDTH), lambda b: (b, 0)),
            pl.BlockSpec((t, KV_WIDTH), lambda b: (b, kb)),
            pl.BlockSpec((t, KV_WIDTH), lambda b: (b, vb)),
        ],
        out_specs=pl.BlockSpec((t, ATTN_WIDTH), lambda b: (b, 0)),
        out_shape=jax.ShapeDtypeStruct((lay.n_ctx, ATTN_WIDTH), BF16),
        compiler_params=_cparams(("arbitrary",)),
        name="attn_ctx",
    )(sink_l, proj, proj, proj)


def _attn_latent(proj, k_ctx, v_ctx, sink_l, lay):
    t = lay.t_lat
    tq = ATTN_BLOCK
    nq = t // tq
    base = lay.n_ctx // tq
    kb, vb = COL_K // KV_WIDTH, COL_V // KV_WIDTH
    past = k_ctx.shape[1]

    def rows(off):
        return lambda b, i: base + b * nq + jnp.clip(i + off, 0, nq - 1)

    def kv_specs(col):
        return [pl.BlockSpec((tq, KV_WIDTH), (lambda b, i, f=rows(off): (f(b, i), col))) for off in (-1, 0, 1)]

    body = functools.partial(_attn_kernel, n_local=3, has_ctx=True, t_total=t)
    rel = np.arange(3 * tq)[None, :] - tq - (np.arange(2 * tq)[:, None] % tq)
    band = jnp.asarray(np.where(np.abs(rel) <= WINDOW, 0.0, NEG_BIG).astype(np.float32))
    return pl.pallas_call(
        body,
        grid=(lay.b_lat, nq),
        in_specs=[
            pl.BlockSpec(memory_space=pltpu.SMEM),
            pl.BlockSpec((tq, ATTN_WIDTH), lambda b, i: (base + b * nq + i, 0)),
            *kv_specs(kb),
            *kv_specs(vb),
            pl.BlockSpec((1, past, KV_WIDTH), lambda b, i: (b, 0, 0)),
            pl.BlockSpec((1, past, KV_WIDTH), lambda b, i: (b, 0, 0)),
            pl.BlockSpec(band.shape, lambda b, i: (0, 0)),
        ],
        out_specs=pl.BlockSpec((tq, ATTN_WIDTH), lambda b, i: (b * nq + i, 0)),
        out_shape=jax.ShapeDtypeStruct((lay.n_lat, ATTN_WIDTH), BF16),
        compiler_params=_cparams(("arbitrary", "arbitrary")),
        name="attn_lat" + lay.tag,
    )(sink_l, proj, proj, proj, proj, proj, proj, proj, k_ctx, v_ctx, band)


def _fourier_kernel(cs_ref, u_ref, cc_ref, sc_ref, o_ref, csb_ref, *, scale):
    @pl.when(pl.program_id(1) == 0)
    def _():
        csb_ref[...] = cs_ref[...].astype(BF16)

    z = u_ref[...].astype(BF16)
    zc = _dot(z, cc_ref[...].astype(BF16)).astype(BF16)
    zs = _dot(z, sc_ref[...].astype(BF16)).astype(BF16)
    zz = jnp.concatenate([zc, zs], axis=0)
    o_ref[...] = (_dot(csb_ref[...], zz) * scale).astype(o_ref.dtype)


@functools.lru_cache(maxsize=None)
def _dft_tables(t):
    idx = np.arange(t, dtype=np.int64)
    ang = 2.0 * np.pi * ((idx[:, None] * idx[None, :]) % t).astype(np.float64) / t
    cs = np.concatenate([np.cos(ang), -np.sin(ang)], axis=1).astype(np.float32)
    cw = FOURIER_WIDTH // FOURIER_GROUPS
    cidx = np.arange(cw, dtype=np.int64)
    cang = 2.0 * np.pi * ((cidx[:, None] * cidx[None, :]) % cw).astype(np.float64) / cw
    eye = np.eye(FOURIER_GROUPS)
    cc = np.kron(eye, np.cos(cang)).astype(np.float32)
    sc = np.kron(eye, np.sin(cang)).astype(np.float32)
    return cs, cc, sc


def _fourier(proj, row0, b, t, tm, name):
    cs, cc, sc = _dft_tables(t)
    cw = FOURIER_WIDTH // FOURIER_GROUPS
    nt = t // tm
    ub = COL_U // FOURIER_WIDTH
    base = row0 // t
    body = functools.partial(_fourier_kernel, scale=1.0 / math.sqrt(t * cw))
    return pl.pallas_call(
        body,
        grid=(nt, b),
        in_specs=[
            pl.BlockSpec((tm, 2 * t), lambda i, bb: (i, 0)),
            pl.BlockSpec((t, FOURIER_WIDTH), lambda i, bb: (base + bb, ub)),
            pl.BlockSpec((FOURIER_WIDTH, FOURIER_WIDTH), lambda i, bb: (0, 0)),
            pl.BlockSpec((FOURIER_WIDTH, FOURIER_WIDTH), lambda i, bb: (0, 0)),
        ],
        out_specs=pl.BlockSpec((tm, FOURIER_WIDTH), lambda i, bb: (bb * nt + i, 0)),
        out_shape=jax.ShapeDtypeStruct((b * t, FOURIER_WIDTH), BF16),
        scratch_shapes=[pltpu.VMEM((tm, 2 * t), BF16)],
        compiler_params=_cparams(("arbitrary", "arbitrary")),
        name=name,
    )(jnp.asarray(cs), proj, jnp.asarray(cc), jnp.asarray(sc))


HGRN_LEVELS = (64, 32, 16, 8, 4, 2)
HGRN_SAFE_RANGE = 80.0


@functools.lru_cache(maxsize=None)
def _hgrn_tables():
    c = HGRN_CHUNK
    return np.stack([np.tril(np.ones((c, c))), np.triu(np.ones((c, c)))]).astype(np.float32)


def _boundary_rows(b, m, reverse):
    c, w = b.shape
    half = m // 2
    off = half if reverse else half - 1
    if m >= 16:
        return jnp.concatenate(
            [jnp.broadcast_to(b[s + off:s + off + 1], (m, w)) for s in range(0, c, m)], axis=0)
    sub = lax.broadcasted_iota(I32, (c, w), 0) & 7
    b3 = b.reshape(c // 8, 8, w)

    def bcast(j):
        return jnp.broadcast_to(b3[:, j:j + 1, :], (c // 8, 8, w)).reshape(c, w)

    if m == 8:
        return bcast(off)
    if m == 4:
        return jnp.where(sub < 4, bcast(off), bcast(4 + off))
    assert m == 2
    if reverse:
        return jnp.where((sub & 1) == 1, b, pltpu.roll(b, c - 1, 0))
    return jnp.where((sub & 1) == 0, b, pltpu.roll(b, 1, 0))


def _hgrn_gates(q, z, v, loglb, log1mlb, onemlb, cum, reverse):
    c = HGRN_CHUNK
    log_sig = jnp.minimum(z, 0.0) - jnp.log1p(jnp.exp(-jnp.abs(z)))
    bb = log1mlb + log_sig
    mx = jnp.maximum(loglb, bb)
    lf = mx + jnp.log1p(jnp.exp(-jnp.abs(loglb - bb)))
    kk = onemlb * jax.nn.sigmoid(-z)
    b = _dot_exact_lhs(cum, lf)
    b_end = b[0:1] if reverse else b[c - 1:c]
    qt = (q * jnp.exp(b)).astype(BF16)
    kt = (kk * jnp.exp(b_end - b)).astype(BF16)
    return (q, kk, b), (qt, kt, v.astype(BF16))


def _head_stack(x):
    lane = lax.broadcasted_iota(I32, x.shape, 1)
    zero = jnp.zeros_like(x)
    return jnp.concatenate([jnp.where((lane >= h * HGRN_DK) & (lane < (h + 1) * HGRN_DK), x, zero)
                            for h in range(HGRN_HEADS)], axis=0)


class _HgrnDir:
    def __init__(self, f32_parts, bf16_parts, reverse):
        c = HGRN_CHUNK
        self.q, self.kk, self.b = f32_parts
        self.qt, self.kt, self.vb = bf16_parts
        self.reverse = reverse
        b_end = self.b[0:1] if reverse else self.b[c - 1:c]
        self.decay = jnp.exp(b_end)
        mid = c // 2 if reverse else c // 2 - 1
        self.rel = self.b - self.b[mid:mid + 1]
        self.span = jnp.max(jnp.abs(self.rel))

    def tree_decay_matrices(self):
        c = HGRN_CHUNK
        q, kk, b = self.q, self.kk, self.b
        row = lax.broadcasted_iota(I32, (c, 1), 0)
        ti = lax.broadcasted_iota(I32, (c, c), 0)
        si = lax.broadcasted_iota(I32, (c, c), 1)
        qb = q.astype(BF16)
        kb = kk.astype(BF16)
        heads = [slice(h * HGRN_DK, (h + 1) * HGRN_DK) for h in range(HGRN_HEADS)]
        acc = [jnp.where(ti == si, _dot_nt(qb[:, sl], kb[:, sl]), 0.0) for sl in heads]
        for m in HGRN_LEVELS:
            r = _boundary_rows(b, m, self.reverse)
            upper = (row & (m - 1)) >= (m // 2)
            q_side = jnp.logical_not(upper) if self.reverse else upper
            e = jnp.exp(jnp.where(q_side, b - r, r - b))
            qf = jnp.where(q_side, q * e, 0.0).astype(BF16)
            kf = jnp.where(q_side, 0.0, kk * e).astype(BF16)
            same_block = (ti & -m) == (si & -m)
            for h, sl in enumerate(heads):
                acc[h] = acc[h] + jnp.where(same_block, _dot_nt(qf[:, sl], kf[:, sl]), 0.0)
        return jnp.concatenate(acc, axis=1)

    def midpoint_decay_matrices(self):
        c = HGRN_CHUNK
        ti = lax.broadcasted_iota(I32, (c, HGRN_HEADS * c), 0)
        si = lax.broadcasted_iota(I32, (c, HGRN_HEADS * c), 1) & (c - 1)
        qm = (self.q * jnp.exp(self.rel)).astype(BF16)
        km = (self.kk * jnp.exp(-self.rel)).astype(BF16)
        causal = (si >= ti) if self.reverse else (si <= ti)
        return jnp.where(causal, _dot_nt(qm, _head_stack(km)), 0.0)

    def outputs(self, a_cat, st_ref, same_head, d):
        bd = st_ref[d]
        o = _dot_nt(self.qt, bd.astype(BF16)) + _dot(a_cat.astype(BF16), _head_stack(self.vb))
        st_ref[d] = jnp.where(same_head, bd * self.decay + _dot_tn(self.vb, self.kt), 0.0)
        return o


def _hgrn_kernel(hq_ref, ff_ref, fb_ref, hi_ref, hg_ref, lbp_ref, gn_ref, mall_ref, ones_ref, s0_ref,
                 rec_ref, sfin_ref, st_ref, of_ref, ob_ref, gf_ref, gb_ref, a_ref, *, t):
    c = HGRN_CHUNK
    n = t // c
    blocks = [slice(h * HGRN_DK, (h + 1) * HGRN_DK) for h in range(HGRN_HEADS)]
    zero_blk = jnp.zeros((HGRN_DK, HGRN_DK), F32)
    for d in range(2):
        st_ref[d] = jnp.concatenate(
            [jnp.concatenate([s0_ref[0, d, h] if j == h else zero_blk for j in range(HGRN_HEADS)], axis=1)
             for h in range(HGRN_HEADS)], axis=0)

    def chunk_rows(ci):
        return pl.ds(pl.multiple_of(ci * c, c), c), pl.ds(pl.multiple_of((n - 1 - ci) * c, c), c)

    def gates_to(slot, ci):
        rf, rb = chunk_rows(ci)
        for d, (rows, f_ref) in enumerate(((rf, ff_ref), (rb, fb_ref))):
            f32_parts, bf16_parts = _hgrn_gates(
                hq_ref[rows, :], f_ref[rows, :], hi_ref[rows, :], lbp_ref[d, 0:1, :], lbp_ref[d, 1:2, :],
                lbp_ref[d, 2:3, :], mall_ref[d].astype(BF16), d == 1)
            for j in range(3):
                gf_ref[slot, d, j] = f32_parts[j]
                gb_ref[slot, d, j] = bf16_parts[j]

    gates_to(0, 0)

    def body(ci, carry):
        slot = ci & 1
        rf, rb = chunk_rows(ci)
        fwd = _HgrnDir([gf_ref[slot, 0, j] for j in range(3)], [gb_ref[slot, 0, j] for j in range(3)], False)
        bwd = _HgrnDir([gf_ref[slot, 1, j] for j in range(3)], [gb_ref[slot, 1, j] for j in range(3)], True)
        for d, hd in enumerate((fwd, bwd)):
            a_ref[d] = hd.midpoint_decay_matrices()
        gates_to(1 - slot, jnp.minimum(ci + 1, n - 1))

        @pl.when(jnp.maximum(fwd.span, bwd.span) > HGRN_SAFE_RANGE)
        def _():
            for d, hd in enumerate((fwd, bwd)):
                a_ref[d] = hd.tree_decay_matrices()

        same_head = ones_ref[...] != 0.0
        of_ref[rf, :] = fwd.outputs(a_ref[0], st_ref, same_head, 0)
        ob_ref[rb, :] = bwd.outputs(a_ref[1], st_ref, same_head, 1)
        return carry

    lax.fori_loop(0, n, body, 0)
    for d in range(2):
        bd = st_ref[d]
        for h, sl in enumerate(blocks):
            sfin_ref[0, d, h] = bd[sl, sl]
    o = of_ref[...] + ob_ref[...]
    ms = _dot_exact_rhs(o * o, ones_ref[...].astype(BF16)) * (1.0 / HGRN_DK)
    o = o * lax.rsqrt(ms + GN_EPS) * gn_ref[...]
    rec_ref[...] = (o * _silu(hg_ref[...])).astype(rec_ref.dtype)


def _hgrn(proj, row0, b, t, lbp, gn_row, s0t, name):
    base = row0 // t
    m_all = jnp.asarray(_hgrn_tables())
    ones_bd = jnp.asarray(np.kron(np.eye(HGRN_HEADS), np.ones((HGRN_DK, HGRN_DK))).astype(np.float32))

    def col(cstart):
        return pl.BlockSpec((t, HGRN_WIDTH), lambda bb, cb=(cstart - MIX_IN_WIDTH) // HGRN_WIDTH: (base + bb, cb))

    const2 = lambda bb: (0, 0)
    const3 = lambda bb: (0, 0, 0)
    st_shape = (2, HGRN_HEADS, HGRN_DK, HGRN_DK)
    body = functools.partial(_hgrn_kernel, t=t)
    return pl.pallas_call(
        body,
        grid=(b,),
        in_specs=[
            col(COL_HQ), col(COL_FF), col(COL_FB), col(COL_HI), col(COL_HG),
            pl.BlockSpec((2, 3, HGRN_WIDTH), const3),
            pl.BlockSpec((1, HGRN_WIDTH), const2),
            pl.BlockSpec(m_all.shape, const3),
            pl.BlockSpec(ones_bd.shape, const2),
            pl.BlockSpec((1,) + st_shape, lambda bb: (bb, 0, 0, 0, 0)),
        ],
        out_specs=[
            pl.BlockSpec((t, HGRN_WIDTH), lambda bb: (bb, 0)),
            pl.BlockSpec((1,) + st_shape, lambda bb: (bb, 0, 0, 0, 0)),
        ],
        out_shape=[
            jax.ShapeDtypeStruct((b * t, HGRN_WIDTH), BF16),
            jax.ShapeDtypeStruct((b,) + st_shape, F32),
        ],
        scratch_shapes=[
            pltpu.VMEM((2, HGRN_WIDTH, HGRN_WIDTH), F32),
            pltpu.VMEM((t, HGRN_WIDTH), F32),
            pltpu.VMEM((t, HGRN_WIDTH), F32),
            pltpu.VMEM((2, 2, 3, HGRN_CHUNK, HGRN_WIDTH), F32),
            pltpu.VMEM((2, 2, 3, HGRN_CHUNK, HGRN_WIDTH), BF16),
            pltpu.VMEM((2, HGRN_CHUNK, HGRN_HEADS * HGRN_CHUNK), F32),
        ],
        compiler_params=_cparams(("arbitrary",)),
        name=name,
    )(proj, proj, proj, proj, proj, lbp, gn_row, m_all, ones_bd, s0t)


def _outproj_kernel(*refs, n_ctx_tiles, n_mix):
    mix = refs[:3 * n_mix]
    refs = refs[3 * n_mix:]
    xs = refs[:-14]
    (mod_ref, w_ref, g_ref, b_ref, rw_ref, rb_ref, x1_ref, hp_ref, meta_ref, gate_ref, cnt_ref, wb_ref, tri_ref,
     run_ref) = refs[-14:]
    tm = x1_ref.shape[0]
    is_ctx = pl.program_id(0) < n_ctx_tiles
    x_in = jnp.where(is_ctx, xs[0][...], xs[1][...]) if len(xs) == 2 else xs[0][...]
    if n_mix == 2:
        attn, four, rec = [jnp.where(is_ctx, mix[2 * j][...], mix[2 * j + 1][...]) for j in range(3)]
    else:
        attn, four, rec = [r[...] for r in mix]

    @pl.when(pl.program_id(0) == 0)
    def _():
        wb_ref[...] = w_ref[0].astype(BF16)
        r = lax.broadcasted_iota(I32, (tm, tm), 0)
        c = lax.broadcasted_iota(I32, (tm, tm), 1)
        tri_ref[...] = jnp.where(r < c, 1.0, 0.0).astype(BF16)
        run_ref[...] = jnp.zeros_like(run_ref)

    out = _dot(attn, wb_ref[0:ATTN_WIDTH, :])
    out = out + _dot(four, wb_ref[ATTN_WIDTH:ATTN_WIDTH + FOURIER_WIDTH, :])
    out = out + _dot(rec, wb_ref[ATTN_WIDTH + FOURIER_WIDTH:, :])
    gate1 = mod_ref[0, 2:3, :]
    y = DEEPNORM_ALPHA * x_in + gate1 * out
    x1 = _ln_plain(y, LN_EPS) * g_ref[...] + b_ref[...]
    x1_ref[...] = x1
    h2 = _ln_plain(x1, ADA_EPS) * (1.0 + mod_ref[0, 4:5, :]) + mod_ref[0, 3:4, :]
    hp_ref[...] = _pack_bf16_pair(h2[:, :HALF_D], h2[:, HALF_D:])

    h_hi = h2.astype(BF16)
    h_lo = (h2 - h_hi.astype(F32)).astype(BF16)
    rwt = rw_ref[...]
    w_hi = rwt.astype(BF16)
    w_lo = (rwt - w_hi.astype(F32)).astype(BF16)
    scores = jax.nn.sigmoid(_dot_nt(w_hi, h_hi) + _dot_nt(w_hi, h_lo) + _dot_nt(w_lo, h_hi))
    remaining = scores + rb_ref[...]
    eidx = lax.broadcasted_iota(I32, scores.shape, 0).astype(F32)
    chosen = jnp.zeros(scores.shape, jnp.bool_)
    picks = []
    for _ in range(TOP_K):
        mx = jnp.max(remaining, axis=0, keepdims=True)
        first = jnp.min(jnp.where(remaining == mx, eidx, float(N_EXPERTS)), axis=0, keepdims=True)
        pick = eidx == first
        picks.append((pick, first))
        chosen = jnp.logical_or(chosen, pick)
        remaining = jnp.where(pick, -jnp.inf, remaining)
    sel = jnp.where(chosen, scores, 0.0)
    gates = sel / jnp.sum(sel, axis=0, keepdims=True) * ROUTED_SCALE

    onehot = jnp.where(chosen, 1.0, 0.0)
    rank = run_ref[...] + _dot(onehot.astype(BF16), tri_ref[...])
    run_ref[...] += jnp.sum(onehot, axis=1, keepdims=True)
    cnt_ref[...] = run_ref[...]

    ids, rks, gks = [], [], []
    for pick, first in picks:
        ids.append(first.astype(I32))
        rks.append(jnp.sum(jnp.where(pick, rank, 0.0), axis=0, keepdims=True).astype(I32))
        gks.append(jnp.sum(jnp.where(pick, gates, 0.0), axis=0, keepdims=True))
    meta_ref[...] = jnp.concatenate(ids + rks, axis=0)
    gate_ref[...] = jnp.concatenate(gks, axis=0)


def _outproj(attn, four, rec, x, mod_l, w_out, layer, g1, b1, rw, rb, lay, tm):
    n_tiles = lay.n // tm
    n_ctx_tiles = lay.n_ctx // tm
    row = lambda i: (i, 0)
    const = lambda i: (0, 0)
    xs = x if isinstance(x, tuple) else (x,)
    return pl.pallas_call(
        functools.partial(_outproj_kernel, n_ctx_tiles=n_ctx_tiles, n_mix=len(attn)),
        grid=(n_tiles,),
        in_specs=[
            *_group_specs(len(attn), tm, ATTN_WIDTH, n_ctx_tiles),
            *_group_specs(len(four), tm, FOURIER_WIDTH, n_ctx_tiles),
            *_group_specs(len(rec), tm, HGRN_WIDTH, n_ctx_tiles),
            *_group_specs(len(xs), tm, D_MODEL, n_ctx_tiles),
            pl.BlockSpec((1, N_MOD, D_MODEL), lambda i: (lay.cond_row(i, tm), 0, 0)),
            pl.BlockSpec((1, D_MODEL, D_MODEL), lambda i: (layer, 0, 0)),
            pl.BlockSpec((1, D_MODEL), const),
            pl.BlockSpec((1, D_MODEL), const),
            pl.BlockSpec((N_EXPERTS, D_MODEL), const),
            pl.BlockSpec((N_EXPERTS, 1), const),
        ],
        out_specs=[
            pl.BlockSpec((tm, D_MODEL), row),
            pl.BlockSpec((tm, HALF_D), row),
            pl.BlockSpec((2 * TOP_K, tm), lambda i: (0, i)),
            pl.BlockSpec((TOP_K, tm), lambda i: (0, i)),
            pl.BlockSpec((N_EXPERTS, 1), const),
        ],
        out_shape=[
            jax.ShapeDtypeStruct((lay.n, D_MODEL), F32),
            jax.ShapeDtypeStruct((lay.n, HALF_D), I32),
            jax.ShapeDtypeStruct((2 * TOP_K, lay.n), I32),
            jax.ShapeDtypeStruct((TOP_K, lay.n), F32),
            jax.ShapeDtypeStruct((N_EXPERTS, 1), F32),
        ],
        scratch_shapes=[
            pltpu.VMEM((D_MODEL, D_MODEL), BF16),
            pltpu.VMEM((tm, tm), BF16),
            pltpu.VMEM((N_EXPERTS, 1), F32),
        ],
        compiler_params=_cparams(("arbitrary",)),
        name="outproj_router" + lay.tag,
    )(*attn, *four, *rec, *xs, mod_l, w_out, g1, b1, rw, rb)


def _sc_workers():
    info = plsc.get_sparse_core_info()
    return info.num_cores, info.num_cores * info.num_subcores


def _sc_scatter_rows(rows, pos_b, r_out, tag=""):
    nc, nw = _sc_workers()
    n, w = rows.shape
    nbt, copies, _ = pos_b.shape
    assert nbt * SC_BATCH == n and nbt % (2 * nw) == 0
    per_w = nbt // nw
    mesh = plsc.VectorSubcoreMesh(core_axis_name="c", subcore_axis_name="s")

    @functools.partial(
        pl.kernel, mesh=mesh, out_type=jax.ShapeDtypeStruct((r_out, w), rows.dtype),
        scratch_types=[pltpu.VMEM((copies, SC_BATCH), I32), pltpu.VMEM((copies, SC_BATCH), I32),
                       pltpu.VMEM((SC_BATCH, w), rows.dtype), pltpu.VMEM((SC_BATCH, w), rows.dtype),
                       pltpu.SemaphoreType.DMA, pltpu.SemaphoreType.DMA,
                       pltpu.SemaphoreType.DMA, pltpu.SemaphoreType.DMA],
        name="sc_dispatch" + tag)
    def k(rows_hbm, pos_hbm, out_hbm, idx_a, idx_b, rows_a, rows_b, sem_ra, sem_rb, sem_sa, sem_sb):
        wid = lax.axis_index("s") * nc + lax.axis_index("c")
        first = wid * per_w

        def reads(j, idx_v, rows_v, sem):
            bt = first + j
            return (pltpu.make_async_copy(pos_hbm.at[bt], idx_v, sem),
                    pltpu.make_async_copy(rows_hbm.at[pl.ds(bt * SC_BATCH, SC_BATCH)], rows_v, sem))

        def scatters(idx_v, rows_v, sem):
            return [pltpu.make_async_copy(rows_v, out_hbm.at[idx_v.at[q]], sem) for q in range(copies)]

        def start(descs):
            for d in descs:
                d.start()

        def wait(descs):
            for d in descs:
                d.wait()

        start(reads(0, idx_a, rows_a, sem_ra))

        @pl.loop(0, per_w // 2)
        def _(p):
            j0 = 2 * p
            j1 = j0 + 1

            @pl.when(p > 0)
            def _():
                wait(scatters(idx_b, rows_b, sem_sb))

            start(reads(j1, idx_b, rows_b, sem_rb))
            wait(reads(j0, idx_a, rows_a, sem_ra))
            start(scatters(idx_a, rows_a, sem_sa))
            wait(reads(j1, idx_b, rows_b, sem_rb))
            start(scatters(idx_b, rows_b, sem_sb))
            wait(scatters(idx_a, rows_a, sem_sa))

            @pl.when(p + 1 < per_w // 2)
            def _():
                start(reads(j0 + 2, idx_a, rows_a, sem_ra))

        wait(scatters(idx_b, rows_b, sem_sb))

    return k(rows, pos_b)


def _sc_gather_rows(table, idx, tag=""):
    nc, nw = _sc_workers()
    r = idx.shape[0]
    w = table.shape[1]
    assert r % (2 * nw * SC_BATCH) == 0
    per_w = r // nw
    nb = per_w // SC_BATCH
    mesh = plsc.VectorSubcoreMesh(core_axis_name="c", subcore_axis_name="s")

    @functools.partial(
        pl.kernel, mesh=mesh, out_type=jax.ShapeDtypeStruct((r, w), table.dtype),
        scratch_types=[pltpu.VMEM((per_w,), I32),
                       pltpu.VMEM((SC_BATCH, w), table.dtype), pltpu.VMEM((SC_BATCH, w), table.dtype),
                       pltpu.SemaphoreType.DMA, pltpu.SemaphoreType.DMA,
                       pltpu.SemaphoreType.DMA, pltpu.SemaphoreType.DMA],
        name="sc_combine" + tag)
    def k(table_hbm, idx_hbm, out_hbm, idx_v, rows_a, rows_b, sem_ga, sem_gb, sem_wa, sem_wb):
        wid = lax.axis_index("s") * nc + lax.axis_index("c")
        base = wid * per_w
        pltpu.sync_copy(idx_hbm.at[pl.ds(base, per_w)], idx_v)

        def gather(j, rows_v, sem):
            return pltpu.make_async_copy(table_hbm.at[idx_v.at[pl.ds(j * SC_BATCH, SC_BATCH)]], rows_v, sem)

        def write(j, rows_v, sem):
            return pltpu.make_async_copy(rows_v, out_hbm.at[pl.ds(base + j * SC_BATCH, SC_BATCH)], sem)

        gather(0, rows_a, sem_ga).start()

        @pl.loop(0, nb // 2)
        def _(p):
            j0 = 2 * p
            j1 = j0 + 1

            @pl.when(p > 0)
            def _():
                write(j1 - 2, rows_b, sem_wb).wait()

            gather(j1, rows_b, sem_gb).start()
            gather(j0, rows_a, sem_ga).wait()
            write(j0, rows_a, sem_wa).start()
            gather(j1, rows_b, sem_gb).wait()
            write(j1, rows_b, sem_wb).start()
            write(j0, rows_a, sem_wa).wait()

            @pl.when(p + 1 < nb // 2)
            def _():
                gather(j0 + 2, rows_a, sem_ga).start()

        write(nb - 1, rows_b, sem_wb).wait()

    return k(table, idx)


def _experts_kernel(te_ref, na_ref, x_ref, w1_ref, w3_ref, w2_ref, o_ref, w1b_ref, w3b_ref, w2b_ref):
    del te_ref

    @pl.when(pl.program_id(0) < na_ref[0])
    def _():
        w1b_ref[...] = w1_ref[0, 0].astype(BF16)
        w3b_ref[...] = w3_ref[0, 0].astype(BF16)
        w2b_ref[...] = w2_ref[0, 0].astype(BF16)
        lo, hi = _unpack_bf16_pair(x_ref[...])
        lo = lo.astype(BF16)
        hi = hi.astype(BF16)
        a = _dot(lo, w1b_ref[0:HALF_D, :]) + _dot(hi, w1b_ref[HALF_D:, :])
        b = _dot(lo, w3b_ref[0:HALF_D, :]) + _dot(hi, w3b_ref[HALF_D:, :])
        y = _dot((_silu(a) * b).astype(BF16), w2b_ref[...])
        o_ref[...] = _pack_bf16_pair(y[:, :HALF_D], y[:, HALF_D:])


def _experts(xs, tile_expert, n_active, w1, w3, w2, layer, tm, tag):
    r = xs.shape[0]
    n_tiles = r // tm

    def xmap(j, te, na):
        return (jnp.minimum(j, na[0] - 1), 0)

    def wmap(j, te, na):
        return (layer, te[jnp.minimum(j, na[0] - 1)], 0, 0)

    grid_spec = pltpu.PrefetchScalarGridSpec(
        num_scalar_prefetch=2,
        grid=(n_tiles,),
        in_specs=[
            pl.BlockSpec((tm, HALF_D), xmap),
            pl.BlockSpec((1, 1, D_MODEL, EXPERT_FF), wmap),
            pl.BlockSpec((1, 1, D_MODEL, EXPERT_FF), wmap),
            pl.BlockSpec((1, 1, EXPERT_FF, D_MODEL), wmap),
        ],
        out_specs=pl.BlockSpec((tm, HALF_D), xmap),
        scratch_shapes=[
            pltpu.VMEM((D_MODEL, EXPERT_FF), BF16),
            pltpu.VMEM((D_MODEL, EXPERT_FF), BF16),
            pltpu.VMEM((EXPERT_FF, D_MODEL), BF16),
        ],
    )
    return pl.pallas_call(
        _experts_kernel,
        grid_spec=grid_spec,
        out_shape=jax.ShapeDtypeStruct((r, HALF_D), I32),
        compiler_params=_cparams(("arbitrary",)),
        name="experts" + tag,
    )(tile_expert, n_active, xs, w1, w3, w2)


def _combine_kernel(yp_ref, gate_ref, hp_ref, sw1_ref, sw3_ref, sw2_ref, x_ref, mod_ref, g_ref, b_ref, *refs,
                    n_ctx_tiles):
    outs = refs[:-3]
    w1b_ref, w3b_ref, w2b_ref = refs[-3:]

    @pl.when(pl.program_id(0) == 0)
    def _():
        w1b_ref[...] = sw1_ref[...].astype(BF16)
        w3b_ref[...] = sw3_ref[...].astype(BF16)
        w2b_ref[...] = sw2_ref[...].astype(BF16)

    lo, hi = _unpack_bf16_pair(hp_ref[...])
    lo = lo.astype(BF16)
    hi = hi.astype(BF16)
    a = _dot(lo, w1b_ref[0:HALF_D, :]) + _dot(hi, w1b_ref[HALF_D:, :])
    b = _dot(lo, w3b_ref[0:HALF_D, :]) + _dot(hi, w3b_ref[HALF_D:, :])
    shared = _dot((_silu(a) * b).astype(BF16), w2b_ref[...])
    acc_lo = shared[:, :HALF_D]
    acc_hi = shared[:, HALF_D:]
    gates = gate_ref[...]
    for k in range(TOP_K):
        ylo, yhi = _unpack_bf16_pair(yp_ref[k])
        gk = gates[:, k:k + 1]
        acc_lo = acc_lo + gk * ylo
        acc_hi = acc_hi + gk * yhi
    moe = jnp.concatenate([acc_lo, acc_hi], axis=1)
    y = DEEPNORM_ALPHA * x_ref[...] + mod_ref[0, 5:6, :] * moe
    res = _ln_plain(y, LN_EPS) * g_ref[...] + b_ref[...]
    if len(outs) == 1:
        outs[0][...] = res
    else:
        @pl.when(pl.program_id(0) < n_ctx_tiles)
        def _():
            outs[0][...] = res

        @pl.when(pl.program_id(0) >= n_ctx_tiles)
        def _():
            outs[1][...] = res


def _combine(yp, gate8, hp, sw1, sw3, sw2, x1, mod_l, g2, b2, lay, tm, split_out):
    n_tiles = lay.n // tm
    n_ctx_tiles = lay.n_ctx // tm
    row = lambda i: (i, 0)
    const = lambda i: (0, 0)
    if split_out:
        out_specs = _group_specs(2, tm, D_MODEL, n_ctx_tiles)
        out_shape = [jax.ShapeDtypeStruct((lay.n_ctx, D_MODEL), F32), jax.ShapeDtypeStruct((lay.n_lat, D_MODEL), F32)]
    else:
        out_specs = pl.BlockSpec((tm, D_MODEL), row)
        out_shape = jax.ShapeDtypeStruct((lay.n, D_MODEL), F32)
    return pl.pallas_call(
        functools.partial(_combine_kernel, n_ctx_tiles=n_ctx_tiles),
        grid=(n_tiles,),
        in_specs=[
            pl.BlockSpec((TOP_K, tm, HALF_D), lambda i: (0, i, 0)),
            pl.BlockSpec((tm, TOP_K), row),
            pl.BlockSpec((tm, HALF_D), row),
            pl.BlockSpec((D_MODEL, EXPERT_FF), const),
            pl.BlockSpec((D_MODEL, EXPERT_FF), const),
            pl.BlockSpec((EXPERT_FF, D_MODEL), const),
            pl.BlockSpec((tm, D_MODEL), row),
            pl.BlockSpec((1, N_MOD, D_MODEL), lambda i: (lay.cond_row(i, tm), 0, 0)),
            pl.BlockSpec((1, D_MODEL), const),
            pl.BlockSpec((1, D_MODEL), const),
        ],
        out_specs=out_specs,
        out_shape=out_shape,
        scratch_shapes=[
            pltpu.VMEM((D_MODEL, EXPERT_FF), BF16),
            pltpu.VMEM((D_MODEL, EXPERT_FF), BF16),
            pltpu.VMEM((EXPERT_FF, D_MODEL), BF16),
        ],
        compiler_params=_cparams(("arbitrary",)),
        name="combine_norm" + lay.tag,
    )(yp, gate8, hp, sw1, sw3, sw2, x1, mod_l, g2, b2)


def _moe_dispatch(hp, meta, counts, lay, tile):
    n = lay.n
    r_max = n * TOP_K + N_EXPERTS * tile
    n_tiles = r_max // tile
    cnt = counts.reshape(N_EXPERTS).astype(I32)
    padded = ((cnt + tile - 1) // tile) * tile
    ends = jnp.cumsum(padded)
    offsets = ends - padded
    idx8 = meta[:TOP_K]
    base8 = jnp.sum(jnp.where(idx8[:, :, None] == jnp.arange(N_EXPERTS, dtype=I32), offsets, 0), axis=-1)
    pos = (base8 + meta[TOP_K:]).astype(I32)
    tile_start = jnp.arange(n_tiles, dtype=I32) * tile
    tile_expert = jnp.minimum(jnp.sum(tile_start[:, None] >= ends[None, :], axis=1), N_EXPERTS - 1).astype(I32)
    n_active = (ends[-1] // tile).astype(I32).reshape(1)
    pos_b = pos.reshape(TOP_K, n // SC_BATCH, SC_BATCH).transpose(1, 0, 2)
    xs = _sc_scatter_rows(hp, pos_b, r_max, lay.tag)
    return xs, tile_expert, n_active, pos


def _moe_combine(ys, pos, gate8, hp, sw1, sw3, sw2, x1, mod_l, g2, b2, lay, split_out):
    n = lay.n
    yp = _sc_gather_rows(ys, pos.reshape(n * TOP_K), lay.tag).reshape(TOP_K, n, HALF_D)
    return _combine(yp, gate8.T, hp, sw1, sw3, sw2, x1, mod_l, g2, b2, lay, TOKEN_TILE, split_out)


def kernel(x_prompt, x_sample, cache_k, cache_v, state_hgrn, c, c_ctx, w_ada, b_ada, w_in, w_out, attn_sink, hgrn_lb, hgrn_norm, ln1_g, ln1_b, ln2_g, ln2_b, router_w, router_b, moe_w1, moe_w3, moe_w2, shared_w1, shared_w3, shared_w2):
    b_ctx, t_ctx, _ = x_prompt.shape
    b_lat, t_lat, _ = x_sample.shape
    past = cache_k.shape[2]
    tm = TOKEN_TILE
    assert 1 + b_lat <= COND_ROWS
    lay = _Layout(b_ctx, t_ctx, b_lat, t_lat)
    assert lay.n_ctx % tm == 0 and t_lat % tm == 0 and lay.n_ctx % t_lat == 0

    cond = jnp.concatenate([c_ctx[None, :], c, jnp.zeros((COND_ROWS - 1 - b_lat, D_MODEL), F32)], axis=0)
    mod = _adaln(cond, w_ada, b_ada).reshape(DEPTH, COND_ROWS, N_MOD, D_MODEL)

    lb_all = jnp.cumsum(jax.nn.softmax(hgrn_lb.astype(F32), axis=0), axis=0)
    lb_all = lb_all - lb_all[:1]
    lbp = jnp.stack([jnp.log(lb_all), jnp.log1p(-lb_all), 1.0 - lb_all], axis=2)

    cos_t, sin_t = _rope_tables(lay, tm)
    zero_state = jnp.zeros((b_ctx, 2, HGRN_HEADS, HGRN_DK, HGRN_DK), F32)

    def layer(l, lay, x, split_out):
        lat = slice(lay.lat_first, lay.lat_first + lay.b_lat)
        outs = _inproj(x, mod[l], w_in, l, cos_t, sin_t, lay, tm)
        proj, proj_h = outs[0], outs[1]
        sink_l = attn_sink[l].reshape(1, N_HEADS)
        gn_row = jnp.tile(hgrn_norm[l], HGRN_HEADS).reshape(1, HGRN_WIDTH)
        attn, four, rec, extras = [], [], [], None
        if lay.b_ctx:
            attn.append(_attn_context(proj, sink_l, lay))
            four.append(_fourier(proj, 0, lay.b_ctx, t_ctx, t_ctx, "fourier_ctx"))
            rec_c, s_fin = _hgrn(proj_h, 0, lay.b_ctx, t_ctx, lbp[l], gn_row, zero_state, "hgrn_ctx")
            rec.append(rec_c)
            extras = (outs[2], outs[3], s_fin)
        attn.append(_attn_latent(proj, cache_k[lat, l].reshape(lay.b_lat, past, KV_WIDTH),
                                 cache_v[lat, l].reshape(lay.b_lat, past, KV_WIDTH), sink_l, lay))
        four.append(_fourier(proj, lay.n_ctx, lay.b_lat, t_lat, min(t_lat, 512), "fourier_lat" + lay.tag))
        s0t = jnp.swapaxes(state_hgrn[lat, l].astype(F32), -1, -2)
        rec.append(_hgrn(proj_h, lay.n_ctx, lay.b_lat, t_lat, lbp[l], gn_row, s0t, "hgrn_lat" + lay.tag)[0])
        x1, hp, meta, gate8, counts = _outproj(
            tuple(attn), tuple(four), tuple(rec), x, mod[l], w_out, l, ln1_g[l].reshape(1, -1),
            ln1_b[l].reshape(1, -1), router_w[l].T, router_b[l].reshape(-1, 1), lay, tm)
        xs, tile_expert, n_active, pos = _moe_dispatch(hp, meta, counts, lay, EXPERT_TILE)
        ys = _experts(xs, tile_expert, n_active, moe_w1, moe_w3, moe_w2, l, EXPERT_TILE, lay.tag)
        x = _moe_combine(ys, pos, gate8, hp, shared_w1[l], shared_w3[l], shared_w2[l], x1, mod[l],
                         ln2_g[l].reshape(1, -1), ln2_b[l].reshape(1, -1), lay, split_out)
        return x, extras

    x = (x_prompt.reshape(lay.n_ctx, D_MODEL), x_sample.reshape(lay.n_lat, D_MODEL))
    ks_out, vs_out, ss_out = [], [], []
    for l in range(DEPTH):
        x, (k_new, v_new, s_fin) = layer(l, lay, x, split_out=(l == DEPTH - 1))
        ks_out.append(k_new.reshape(b_ctx, t_ctx, N_KV_HEADS, HEAD_DIM))
        vs_out.append(v_new.reshape(b_ctx, t_ctx, N_KV_HEADS, HEAD_DIM))
        ss_out.append(jnp.swapaxes(s_fin, -1, -2))

    y_prompt = x[0].reshape(b_ctx, t_ctx, D_MODEL)
    y_sample = x[1].reshape(b_lat, t_lat, D_MODEL)
    new_cache_k = jnp.stack(ks_out, axis=1)
    new_cache_v = jnp.stack(vs_out, axis=1)
    new_state = jnp.stack(ss_out, axis=1).astype(x_prompt.dtype)
    return (y_prompt, y_sample, new_cache_k, new_cache_v, new_state)
```

```python
import functools
import math

import numpy as np
import jax
import jax.numpy as jnp
from jax import lax
from jax.experimental import pallas as pl
from jax.experimental.pallas import tpu as pltpu
from jax.experimental.pallas import tpu_sc as plsc

F32 = jnp.float32
BF16 = jnp.bfloat16
I32 = jnp.int32

D_MODEL = 1024
HALF_D = D_MODEL // 2
DEPTH = 2
GRID_W = 64
ROPE_BASE = 10000.0
HEAD_DIM = 64
ATTN_WIDTH = 512
N_HEADS = 8
N_KV_HEADS = 2
KV_GROUP = 4
KV_WIDTH = N_KV_HEADS * HEAD_DIM
WINDOW = 128
ATTN_BLOCK = 128
FOURIER_WIDTH = 256
FOURIER_GROUPS = 4
HGRN_WIDTH = 256
HGRN_HEADS = 4
HGRN_DK = 64
HGRN_CHUNK = 64
IN_WIDTH = 2304
N_EXPERTS = 64
TOP_K = 8
EXPERT_FF = 256
ROUTED_SCALE = 2.5
N_MOD = 6
LN_EPS = 1e-5
ADA_EPS = 1e-6
GN_EPS = 1e-6
DEEPNORM_ALPHA = (2 * DEPTH) ** 0.25

COL_Q = 0
COL_K = 512
COL_V = 640
COL_U = 768
COL_HQ = 1024
COL_FF = 1280
COL_FB = 1536
COL_HI = 1792
COL_HG = 2048
ROPE_COLS = COL_V
MIX_IN_WIDTH = COL_HQ

V7X_LANES = 128
COND_ROWS = 16
NEG_BIG = -1e30
TOKEN_TILE = 512
EXPERT_TILE = 1024
SC_BATCH = 64

VMEM_LIMIT = 56 * 1024 * 1024


def _cparams(sem):
    return pltpu.CompilerParams(dimension_semantics=sem, vmem_limit_bytes=VMEM_LIMIT)


def _dot(a, b):
    return jnp.dot(a, b, preferred_element_type=F32)


def _dot_nt(a, b):
    return lax.dot_general(a, b, (((1,), (1,)), ((), ())), preferred_element_type=F32)


def _dot_tn(a, b):
    return lax.dot_general(a, b, (((0,), (0,)), ((), ())), preferred_element_type=F32)


def _split3(x):
    hi = x.astype(BF16)
    r1 = x - hi.astype(F32)
    mid = r1.astype(BF16)
    lo = (r1 - mid.astype(F32)).astype(BF16)
    return hi, mid, lo


def _dot_exact_lhs(m_bf16, x):
    hi, mid, lo = _split3(x)
    return _dot(m_bf16, hi) + _dot(m_bf16, mid) + _dot(m_bf16, lo)


def _dot_exact_rhs(x, m_bf16):
    hi, mid, lo = _split3(x)
    return _dot(hi, m_bf16) + _dot(mid, m_bf16) + _dot(lo, m_bf16)


def _dot_hp(a, b):
    a_hi = a.astype(BF16)
    a_lo = (a - a_hi.astype(F32)).astype(BF16)
    b_hi = b.astype(BF16)
    b_lo = (b - b_hi.astype(F32)).astype(BF16)
    return _dot(a_hi, b_hi) + _dot(a_hi, b_lo) + _dot(a_lo, b_hi)


def _pack_bf16_pair(lo, hi):
    return lax.bitcast_convert_type(pltpu.pack_elementwise([lo, hi], packed_dtype=BF16), I32)


def _unpack_bf16_pair(w):
    u = lax.bitcast_convert_type(w, jnp.uint32)
    lo = pltpu.unpack_elementwise(u, index=0, packed_dtype=BF16, unpacked_dtype=F32)
    hi = pltpu.unpack_elementwise(u, index=1, packed_dtype=BF16, unpacked_dtype=F32)
    return lo, hi


def _ln_plain(x, eps):
    mu = jnp.mean(x, axis=-1, keepdims=True)
    xc = x - mu
    var = jnp.mean(xc * xc, axis=-1, keepdims=True)
    return xc * lax.rsqrt(var + eps)


def _silu(x):
    return x * jax.nn.sigmoid(x)


def _adaln_kernel(c_ref, w_ref, b_ref, o_ref):
    s = _silu(c_ref[...])
    o_ref[0] = _dot_hp(s, w_ref[0]) + b_ref[0]


def _adaln(cond, w_ada, b_ada):
    return pl.pallas_call(
        _adaln_kernel,
        grid=(DEPTH, N_MOD),
        in_specs=[
            pl.BlockSpec((COND_ROWS, D_MODEL), lambda l, j: (0, 0)),
            pl.BlockSpec((1, D_MODEL, D_MODEL), lambda l, j: (l, 0, j)),
            pl.BlockSpec((1, 1, D_MODEL), lambda l, j: (l, 0, j)),
        ],
        out_specs=pl.BlockSpec((1, COND_ROWS, D_MODEL), lambda l, j: (l, 0, j)),
        out_shape=jax.ShapeDtypeStruct((DEPTH, COND_ROWS, N_MOD * D_MODEL), F32),
        compiler_params=_cparams(("arbitrary", "arbitrary")),
        name="adaln",
    )(cond, w_ada, b_ada.reshape(DEPTH, 1, N_MOD * D_MODEL))


class _Layout:
    def __init__(self, b_ctx, t_ctx, b_lat, t_lat, lat_first=0, tag=""):
        self.b_ctx, self.t_ctx, self.b_lat, self.t_lat = b_ctx, t_ctx, b_lat, t_lat
        self.lat_first = lat_first
        self.n_ctx = b_ctx * t_ctx
        self.n_lat = b_lat * t_lat
        self.n = self.n_ctx + self.n_lat
        self.tag = tag

    def cond_row(self, tile, tm):
        n_ctx_tiles = self.n_ctx // tm
        per_batch = self.t_lat // tm
        return jnp.where(tile < n_ctx_tiles, 0, 1 + self.lat_first + (tile - n_ctx_tiles) // per_batch)


def _group_specs(n_arrays, tm, width, n_ctx_tiles):
    if n_arrays == 1:
        return [pl.BlockSpec((tm, width), lambda i: (i, 0))]
    return [pl.BlockSpec((tm, width), lambda i: (jnp.minimum(i, n_ctx_tiles - 1), 0)),
            pl.BlockSpec((tm, width), lambda i: (jnp.maximum(i - n_ctx_tiles, 0), 0))]


def _inproj_kernel(*refs, n_ctx_tiles):
    n_tail = 9 if n_ctx_tiles > 0 else 7
    xs = refs[:-n_tail]
    mod_ref, w_ref, cos_ref, sin_ref, oa_ref, oh_ref = refs[-n_tail:-n_tail + 6]
    wb_ref = refs[-1]

    @pl.when(pl.program_id(0) == 0)
    def _():
        wb_ref[...] = w_ref[0].astype(BF16)

    if len(xs) == 2:
        x = jnp.where(pl.program_id(0) < n_ctx_tiles, xs[0][...], xs[1][...])
    else:
        x = xs[0][...]
    shift = mod_ref[0, 0:1, :]
    scale = mod_ref[0, 1:2, :]
    h = (_ln_plain(x, ADA_EPS) * (1.0 + scale) + shift).astype(BF16)
    p = _dot(h, wb_ref[...])
    cos = cos_ref[...]
    sin = sin_ref[...]
    lane = lax.broadcasted_iota(I32, cos.shape, 1)
    first_half = (lane & 31) < 16
    for cb in range(ROPE_COLS // V7X_LANES):
        seg = p[:, cb * V7X_LANES:(cb + 1) * V7X_LANES]
        partner = jnp.where(first_half, pltpu.roll(seg, V7X_LANES - 16, 1), pltpu.roll(seg, 16, 1))
        oa_ref[:, cb * V7X_LANES:(cb + 1) * V7X_LANES] = (seg * cos + partner * sin).astype(BF16)
    oa_ref[:, ROPE_COLS:] = p[:, ROPE_COLS:MIX_IN_WIDTH].astype(BF16)
    oh_ref[...] = p[:, MIX_IN_WIDTH:]

    if n_ctx_tiles > 0:
        kc_ref, vc_ref = refs[-3], refs[-2]

        @pl.when(pl.program_id(0) < n_ctx_tiles)
        def _():
            kc_ref[...] = p[:, COL_K:COL_K + KV_WIDTH]
            vc_ref[...] = p[:, COL_V:COL_V + KV_WIDTH]


def _rope_tables(lay, tm):
    t = lay.t_lat
    pos = jnp.arange(t)
    row = (pos // GRID_W).astype(F32)
    col = (pos % GRID_W).astype(F32)
    n_freq = HEAD_DIM // 4
    inv = ROPE_BASE ** (-jnp.arange(n_freq, dtype=F32) / n_freq)
    ang_r = row[:, None] * inv
    ang_c = col[:, None] * inv
    ang = jnp.concatenate([ang_r, ang_r, ang_c, ang_c], axis=1)
    sign = jnp.concatenate([-jnp.ones(n_freq), jnp.ones(n_freq), -jnp.ones(n_freq), jnp.ones(n_freq)]).astype(F32)
    cos = jnp.cos(ang)
    sin = jnp.sin(ang) * sign
    cos = jnp.concatenate([jnp.ones((tm, HEAD_DIM), F32), cos], axis=0)
    sin = jnp.concatenate([jnp.zeros((tm, HEAD_DIM), F32), sin], axis=0)
    return jnp.tile(cos, (1, 2)), jnp.tile(sin, (1, 2))


def _inproj(x, mod_l, w_in, layer, cos_t, sin_t, lay, tm):
    n_tiles = lay.n // tm
    n_ctx_tiles = lay.n_ctx // tm
    per_batch = lay.t_lat // tm

    def tbl(i):
        return jnp.where(i < n_ctx_tiles, 0, 1 + (i - n_ctx_tiles) % per_batch)

    xs = x if isinstance(x, tuple) else (x,)
    kv_specs, kv_shapes = [], []
    if n_ctx_tiles > 0:
        kv_specs = [pl.BlockSpec((tm, KV_WIDTH), lambda i: (jnp.minimum(i, n_ctx_tiles - 1), 0))] * 2
        kv_shapes = [jax.ShapeDtypeStruct((lay.n_ctx, KV_WIDTH), F32)] * 2
    return pl.pallas_call(
        functools.partial(_inproj_kernel, n_ctx_tiles=n_ctx_tiles),
        grid=(n_tiles,),
        in_specs=[
            *_group_specs(len(xs), tm, D_MODEL, n_ctx_tiles),
            pl.BlockSpec((1, N_MOD, D_MODEL), lambda i: (lay.cond_row(i, tm), 0, 0)),
            pl.BlockSpec((1, D_MODEL, IN_WIDTH), lambda i: (layer, 0, 0), pipeline_mode=pl.Buffered(1)),
            pl.BlockSpec((tm, V7X_LANES), lambda i: (tbl(i), 0)),
            pl.BlockSpec((tm, V7X_LANES), lambda i: (tbl(i), 0)),
        ],
        out_specs=[pl.BlockSpec((tm, MIX_IN_WIDTH), lambda i: (i, 0)),
                   pl.BlockSpec((tm, IN_WIDTH - MIX_IN_WIDTH), lambda i: (i, 0))] + kv_specs,
        out_shape=[jax.ShapeDtypeStruct((lay.n, MIX_IN_WIDTH), BF16),
                   jax.ShapeDtypeStruct((lay.n, IN_WIDTH - MIX_IN_WIDTH), F32)] + kv_shapes,
        scratch_shapes=[pltpu.VMEM((D_MODEL, IN_WIDTH), BF16)],
        compiler_params=_cparams(("arbitrary",)),
        name="inproj" + lay.tag,
    )(*xs, mod_l, w_in, cos_t, sin_t)


def _attn_kernel(sink_ref, q_ref, *refs, n_local, has_ctx, t_total):
    o_ref = refs[-1]
    k_refs = refs[:n_local]
    v_refs = refs[n_local:2 * n_local]
    tq = q_ref.shape[0]
    scale = HEAD_DIM ** -0.5
    k_parts = [kr[...] for kr in k_refs]
    v_parts = [vr[...] for vr in v_refs]
    if has_ctx:
        k_parts.append(refs[2 * n_local][0].astype(k_parts[0].dtype))
        v_parts.append(refs[2 * n_local + 1][0].astype(v_parts[0].dtype))
    kall = (jnp.concatenate(k_parts, axis=0) if len(k_parts) > 1 else k_parts[0]).astype(F32)
    vall = (jnp.concatenate(v_parts, axis=0) if len(v_parts) > 1 else v_parts[0]).astype(F32)
    nk = kall.shape[0]
    k_sw = pltpu.roll(kall, HEAD_DIM, 1)
    v_sw = pltpu.roll(vall, HEAD_DIM, 1)
    lo_half = lax.broadcasted_iota(I32, (1, V7X_LANES), 1) < HEAD_DIM
    er = jnp.where(lax.broadcasted_iota(I32, (2 * nk, V7X_LANES), 0) < nk, 0, 1)
    el = jnp.where(lax.broadcasted_iota(I32, (2 * nk, V7X_LANES), 1) < HEAD_DIM, 0, 1)
    ones_blk = jnp.where(er == el, 1.0, 0.0).astype(BF16)
    if n_local > 1:
        i = pl.program_id(1)
        band = refs[-2][...]
        first_blk = jnp.where(i == 0, NEG_BIG, 0.0)
        last_blk = jnp.where(i == t_total // tq - 1, NEG_BIG, 0.0)

        def mask_local(sc):
            loc = sc[:, :n_local * tq] + band
            parts = [loc[:, :tq] + first_blk, loc[:, tq:(n_local - 1) * tq], loc[:, (n_local - 1) * tq:] + last_blk]
            return jnp.concatenate(parts + [sc[:, n_local * tq:]], axis=1)
    else:
        mask_local = None
    v2es, sinks, scores = [], [], []
    for g in range(N_KV_HEADS):
        k_own, k_oth = (kall, k_sw) if g == 0 else (k_sw, kall)
        v_own, v_oth = (vall, v_sw) if g == 0 else (v_sw, vall)
        k2 = jnp.concatenate([jnp.where(lo_half, k_own, 0.0), jnp.where(lo_half, 0.0, k_oth)], axis=0).astype(BF16)
        v2 = jnp.concatenate([jnp.where(lo_half, v_own, 0.0), jnp.where(lo_half, 0.0, v_oth)], axis=0).astype(BF16)
        v2es.append(jnp.concatenate([v2, ones_blk], axis=1))
        pairs = [2 * g, 2 * g + 1]
        qq = jnp.concatenate([q_ref[:, p * V7X_LANES:(p + 1) * V7X_LANES] for p in pairs], axis=0)
        qq = (qq.astype(F32) * scale).astype(BF16)
        sinks.append((jnp.concatenate([jnp.full((tq, 1), sink_ref[0, 2 * p], F32) for p in pairs], axis=0),
                      jnp.concatenate([jnp.full((tq, 1), sink_ref[0, 2 * p + 1], F32) for p in pairs], axis=0)))
        scores.append(_dot_nt(qq, k2))
    pes, maxes = [], []
    for g in range(N_KV_HEADS):
        s_a = scores[g][:, :nk]
        s_b = scores[g][:, nk:]
        if mask_local is not None:
            s_a = mask_local(s_a)
            s_b = mask_local(s_b)
        m_a = jnp.maximum(jnp.max(s_a, axis=1, keepdims=True), sinks[g][0])
        m_b = jnp.maximum(jnp.max(s_b, axis=1, keepdims=True), sinks[g][1])
        pes.append(jnp.concatenate([jnp.exp(s_a - m_a).astype(BF16), jnp.exp(s_b - m_b).astype(BF16)], axis=1))
        maxes.append((m_a, m_b))
    for g in range(N_KV_HEADS):
        acc = _dot(pes[g], v2es[g])
        (m_a, m_b), (sink_a, sink_b) = maxes[g], sinks[g]
        sink_term = jnp.where(lo_half, jnp.exp(sink_a - m_a), jnp.exp(sink_b - m_b))
        o = acc[:, :V7X_LANES] / (acc[:, V7X_LANES:] + sink_term)
        for j, p in enumerate([2 * g, 2 * g + 1]):
            o_ref[:, p * V7X_LANES:(p + 1) * V7X_LANES] = o[j * tq:(j + 1) * tq].astype(o_ref.dtype)


def _attn_context(proj, sink_l, lay):
    t = lay.t_ctx
    kb, vb = COL_K // KV_WIDTH, COL_V // KV_WIDTH
    body = functools.partial(_attn_kernel, n_local=1, has_ctx=False, t_total=t)
    return pl.pallas_call(
        body,
        grid=(lay.b_ctx,),
        in_specs=[
            pl.BlockSpec(memory_space=pltpu.SMEM),
            pl.BlockSpec((t, ATTN_WIDTH), lambda b: (b, 0)),
            pl.BlockSpec((t, KV_WIDTH), lambda b: (b, kb)),
            pl.BlockSpec((t, KV_WIDTH), lambda b: (b, vb)),
        ],
        out_specs=pl.BlockSpec((t, ATTN_WIDTH), lambda b: (b, 0)),
        out_shape=jax.ShapeDtypeStruct((lay.n_ctx, ATTN_WIDTH), BF16),
        compiler_params=_cparams(("arbitrary",)),
        name="attn_ctx",
    )(sink_l, proj, proj, proj)


def _attn_latent(proj, k_ctx, v_ctx, sink_l, lay):
    t = lay.t_lat
    tq = ATTN_BLOCK
    nq = t // tq
    base = lay.n_ctx // tq
    kb, vb = COL_K // KV_WIDTH, COL_V // KV_WIDTH
    past = k_ctx.shape[1]

    def rows(off):
        return lambda b, i: base + b * nq + jnp.clip(i + off, 0, nq - 1)

    def kv_specs(col):
        return [pl.BlockSpec((tq, KV_WIDTH), (lambda b, i, f=rows(off): (f(b, i), col))) for off in (-1, 0, 1)]

    body = functools.partial(_attn_kernel, n_local=3, has_ctx=True, t_total=t)
    rel = np.arange(3 * tq)[None, :] - tq - (np.arange(2 * tq)[:, None] % tq)
    band = jnp.asarray(np.where(np.abs(rel) <= WINDOW, 0.0, NEG_BIG).astype(np.float32))
    return pl.pallas_call(
        body,
        grid=(lay.b_lat, nq),
        in_specs=[
            pl.BlockSpec(memory_space=pltpu.SMEM),
            pl.BlockSpec((tq, ATTN_WIDTH), lambda b, i: (base + b * nq + i, 0)),
            *kv_specs(kb),
            *kv_specs(vb),
            pl.BlockSpec((1, past, KV_WIDTH), lambda b, i: (b, 0, 0)),
            pl.BlockSpec((1, past, KV_WIDTH), lambda b, i: (b, 0, 0)),
            pl.BlockSpec(band.shape, lambda b, i: (0, 0)),
        ],
        out_specs=pl.BlockSpec((tq, ATTN_WIDTH), lambda b, i: (b * nq + i, 0)),
        out_shape=jax.ShapeDtypeStruct((lay.n_lat, ATTN_WIDTH), BF16),
        compiler_params=_cparams(("arbitrary", "arbitrary")),
        name="attn_lat" + lay.tag,
    )(sink_l, proj, proj, proj, proj, proj, proj, proj, k_ctx, v_ctx, band)


def _fourier_kernel(cs_ref, u_ref, cc_ref, sc_ref, o_ref, csb_ref, *, scale):
    @pl.when(pl.program_id(1) == 0)
    def _():
        csb_ref[...] = cs_ref[...].astype(BF16)

    z = u_ref[...].astype(BF16)
    zc = _dot(z, cc_ref[...].astype(BF16)).astype(BF16)
    zs = _dot(z, sc_ref[...].astype(BF16)).astype(BF16)
    zz = jnp.concatenate([zc, zs], axis=0)
    o_ref[...] = (_dot(csb_ref[...], zz) * scale).astype(o_ref.dtype)


@functools.lru_cache(maxsize=None)
def _dft_tables(t):
    idx = np.arange(t, dtype=np.int64)
    ang = 2.0 * np.pi * ((idx[:, None] * idx[None, :]) % t).astype(np.float64) / t
    cs = np.concatenate([np.cos(ang), -np.sin(ang)], axis=1).astype(np.float32)
    cw = FOURIER_WIDTH // FOURIER_GROUPS
    cidx = np.arange(cw, dtype=np.int64)
    cang = 2.0 * np.pi * ((cidx[:, None] * cidx[None, :]) % cw).astype(np.float64) / cw
    eye = np.eye(FOURIER_GROUPS)
    cc = np.kron(eye, np.cos(cang)).astype(np.float32)
    sc = np.kron(eye, np.sin(cang)).astype(np.float32)
    return cs, cc, sc


def _fourier(proj, row0, b, t, tm, name):
    cs, cc, sc = _dft_tables(t)
    cw = FOURIER_WIDTH // FOURIER_GROUPS
    nt = t // tm
    ub = COL_U // FOURIER_WIDTH
    base = row0 // t
    body = functools.partial(_fourier_kernel, scale=1.0 / math.sqrt(t * cw))
    return pl.pallas_call(
        body,
        grid=(nt, b),
        in_specs=[
            pl.BlockSpec((tm, 2 * t), lambda i, bb: (i, 0)),
            pl.BlockSpec((t, FOURIER_WIDTH), lambda i, bb: (base + bb, ub)),
            pl.BlockSpec((FOURIER_WIDTH, FOURIER_WIDTH), lambda i, bb: (0, 0)),
            pl.BlockSpec((FOURIER_WIDTH, FOURIER_WIDTH), lambda i, bb: (0, 0)),
        ],
        out_specs=pl.BlockSpec((tm, FOURIER_WIDTH), lambda i, bb: (bb * nt + i, 0)),
        out_shape=jax.ShapeDtypeStruct((b * t, FOURIER_WIDTH), BF16),
        scratch_shapes=[pltpu.VMEM((tm, 2 * t), BF16)],
        compiler_params=_cparams(("arbitrary", "arbitrary")),
        name=name,
    )(jnp.asarray(cs), proj, jnp.asarray(cc), jnp.asarray(sc))


HGRN_LEVELS = (64, 32, 16, 8, 4, 2)
HGRN_SAFE_RANGE = 80.0


@functools.lru_cache(maxsize=None)
def _hgrn_tables():
    c = HGRN_CHUNK
    return np.stack([np.tril(np.ones((c, c))), np.triu(np.ones((c, c)))]).astype(np.float32)


def _boundary_rows(b, m, reverse):
    c, w = b.shape
    half = m // 2
    off = half if reverse else half - 1
    if m >= 16:
        return jnp.concatenate(
            [jnp.broadcast_to(b[s + off:s + off + 1], (m, w)) for s in range(0, c, m)], axis=0)
    sub = lax.broadcasted_iota(I32, (c, w), 0) & 7
    b3 = b.reshape(c // 8, 8, w)

    def bcast(j):
        return jnp.broadcast_to(b3[:, j:j + 1, :], (c // 8, 8, w)).reshape(c, w)

    if m == 8:
        return bcast(off)
    if m == 4:
        return jnp.where(sub < 4, bcast(off), bcast(4 + off))
    assert m == 2
    if reverse:
        return jnp.where((sub & 1) == 1, b, pltpu.roll(b, c - 1, 0))
    return jnp.where((sub & 1) == 0, b, pltpu.roll(b, 1, 0))


def _hgrn_gates(q, z, v, loglb, log1mlb, onemlb, cum, reverse):
    c = HGRN_CHUNK
    log_sig = jnp.minimum(z, 0.0) - jnp.log1p(jnp.exp(-jnp.abs(z)))
    bb = log1mlb + log_sig
    mx = jnp.maximum(loglb, bb)
    lf = mx + jnp.log1p(jnp.exp(-jnp.abs(loglb - bb)))
    kk = onemlb * jax.nn.sigmoid(-z)
    b = _dot_exact_lhs(cum, lf)
    b_end = b[0:1] if reverse else b[c - 1:c]
    qt = (q * jnp.exp(b)).astype(BF16)
    kt = (kk * jnp.exp(b_end - b)).astype(BF16)
    return (q, kk, b), (qt, kt, v.astype(BF16))


def _head_stack(x):
    lane = lax.broadcasted_iota(I32, x.shape, 1)
    zero = jnp.zeros_like(x)
    return jnp.concatenate([jnp.where((lane >= h * HGRN_DK) & (lane < (h + 1) * HGRN_DK), x, zero)
                            for h in range(HGRN_HEADS)], axis=0)


class _HgrnDir:
    def __init__(self, f32_parts, bf16_parts, reverse):
        c = HGRN_CHUNK
        self.q, self.kk, self.b = f32_parts
        self.qt, self.kt, self.vb = bf16_parts
        self.reverse = reverse
        b_end = self.b[0:1] if reverse else self.b[c - 1:c]
        self.decay = jnp.exp(b_end)
        mid = c // 2 if reverse else c // 2 - 1
        self.rel = self.b - self.b[mid:mid + 1]
        self.span = jnp.max(jnp.abs(self.rel))

    def tree_decay_matrices(self):
        c = HGRN_CHUNK
        q, kk, b = self.q, self.kk, self.b
        row = lax.broadcasted_iota(I32, (c, 1), 0)
        ti = lax.broadcasted_iota(I32, (c, c), 0)
        si = lax.broadcasted_iota(I32, (c, c), 1)
        qb = q.astype(BF16)
        kb = kk.astype(BF16)
        heads = [slice(h * HGRN_DK, (h + 1) * HGRN_DK) for h in range(HGRN_HEADS)]
        acc = [jnp.where(ti == si, _dot_nt(qb[:, sl], kb[:, sl]), 0.0) for sl in heads]
        for m in HGRN_LEVELS:
            r = _boundary_rows(b, m, self.reverse)
            upper = (row & (m - 1)) >= (m // 2)
            q_side = jnp.logical_not(upper) if self.reverse else upper
            e = jnp.exp(jnp.where(q_side, b - r, r - b))
            qf = jnp.where(q_side, q * e, 0.0).astype(BF16)
            kf = jnp.where(q_side, 0.0, kk * e).astype(BF16)
            same_block = (ti & -m) == (si & -m)
            for h, sl in enumerate(heads):
                acc[h] = acc[h] + jnp.where(same_block, _dot_nt(qf[:, sl], kf[:, sl]), 0.0)
        return jnp.concatenate(acc, axis=1)

    def midpoint_decay_matrices(self):
        c = HGRN_CHUNK
        ti = lax.broadcasted_iota(I32, (c, HGRN_HEADS * c), 0)
        si = lax.broadcasted_iota(I32, (c, HGRN_HEADS * c), 1) & (c - 1)
        qm = (self.q * jnp.exp(self.rel)).astype(BF16)
        km = (self.kk * jnp.exp(-self.rel)).astype(BF16)
        causal = (si >= ti) if self.reverse else (si <= ti)
        return jnp.where(causal, _dot_nt(qm, _head_stack(km)), 0.0)

    def outputs(self, a_cat, st_ref, same_head, d):
        bd = st_ref[d]
        o = _dot_nt(self.qt, bd.astype(BF16)) + _dot(a_cat.astype(BF16), _head_stack(self.vb))
        st_ref[d] = jnp.where(same_head, bd * self.decay + _dot_tn(self.vb, self.kt), 0.0)
        return o


def _hgrn_kernel(hq_ref, ff_ref, fb_ref, hi_ref, hg_ref, lbp_ref, gn_ref, mall_ref, ones_ref, s0_ref,
                 rec_ref, sfin_ref, st_ref, of_ref, ob_ref, gf_ref, gb_ref, *, t):
    c = HGRN_CHUNK
    n = t // c
    blocks = [slice(h * HGRN_DK, (h + 1) * HGRN_DK) for h in range(HGRN_HEADS)]
    zero_blk = jnp.zeros((HGRN_DK, HGRN_DK), F32)

    def init_state():
        for d in range(2):
            st_ref[d] = jnp.concatenate(
                [jnp.concatenate([s0_ref[0, d, h] if j == h else zero_blk for j in range(HGRN_HEADS)], axis=1)
                 for h in range(HGRN_HEADS)], axis=0)

    def chunk_rows(ci):
        return pl.ds(pl.multiple_of(ci * c, c), c), pl.ds(pl.multiple_of((n - 1 - ci) * c, c), c)

    def gates_to(slot, ci):
        rf, rb = chunk_rows(ci)
        for d, (rows, f_ref) in enumerate(((rf, ff_ref), (rb, fb_ref))):
            f32_parts, bf16_parts = _hgrn_gates(
                hq_ref[rows, :], f_ref[rows, :], hi_ref[rows, :], lbp_ref[d, 0:1, :], lbp_ref[d, 1:2, :],
                lbp_ref[d, 2:3, :], mall_ref[d].astype(BF16), d == 1)
            for j in range(3):
                gf_ref[slot, d, j] = f32_parts[j]
                gb_ref[slot, d, j] = bf16_parts[j]

    def run(decay_matrices):
        init_state()
        gates_to(0, 0)

        def body(ci, widest):
            slot = ci & 1
            rf, rb = chunk_rows(ci)
            fwd = _HgrnDir([gf_ref[slot, 0, j] for j in range(3)], [gb_ref[slot, 0, j] for j in range(3)], False)
            bwd = _HgrnDir([gf_ref[slot, 1, j] for j in range(3)], [gb_ref[slot, 1, j] for j in range(3)], True)
            a_f, a_b = decay_matrices(fwd), decay_matrices(bwd)
            gates_to(1 - slot, jnp.minimum(ci + 1, n - 1))
            same_head = ones_ref[...] != 0.0
            of_ref[rf, :] = fwd.outputs(a_f, st_ref, same_head, 0)
            ob_ref[rb, :] = bwd.outputs(a_b, st_ref, same_head, 1)
            return jnp.maximum(widest, jnp.maximum(fwd.span, bwd.span))

        return lax.fori_loop(0, n, body, jnp.float32(0.0))

    widest = run(_HgrnDir.midpoint_decay_matrices)

    @pl.when(widest > HGRN_SAFE_RANGE)
    def _():
        run(_HgrnDir.tree_decay_matrices)

    for d in range(2):
        bd = st_ref[d]
        for h, sl in enumerate(blocks):
            sfin_ref[0, d, h] = bd[sl, sl]
    o = of_ref[...] + ob_ref[...]
    ms = _dot_exact_rhs(o * o, ones_ref[...].astype(BF16)) * (1.0 / HGRN_DK)
    o = o * lax.rsqrt(ms + GN_EPS) * gn_ref[...]
    rec_ref[...] = (o * _silu(hg_ref[...])).astype(rec_ref.dtype)


def _hgrn(proj, row0, b, t, lbp, gn_row, s0t, name):
    base = row0 // t
    m_all = jnp.asarray(_hgrn_tables())
    ones_bd = jnp.asarray(np.kron(np.eye(HGRN_HEADS), np.ones((HGRN_DK, HGRN_DK))).astype(np.float32))

    def col(cstart):
        return pl.BlockSpec((t, HGRN_WIDTH), lambda bb, cb=(cstart - MIX_IN_WIDTH) // HGRN_WIDTH: (base + bb, cb))

    const2 = lambda bb: (0, 0)
    const3 = lambda bb: (0, 0, 0)
    st_shape = (2, HGRN_HEADS, HGRN_DK, HGRN_DK)
    body = functools.partial(_hgrn_kernel, t=t)
    return pl.pallas_call(
        body,
        grid=(b,),
        in_specs=[
            col(COL_HQ), col(COL_FF), col(COL_FB), col(COL_HI), col(COL_HG),
            pl.BlockSpec((2, 3, HGRN_WIDTH), const3),
            pl.BlockSpec((1, HGRN_WIDTH), const2),
            pl.BlockSpec(m_all.shape, const3),
            pl.BlockSpec(ones_bd.shape, const2),
            pl.BlockSpec((1,) + st_shape, lambda bb: (bb, 0, 0, 0, 0)),
        ],
        out_specs=[
            pl.BlockSpec((t, HGRN_WIDTH), lambda bb: (bb, 0)),
            pl.BlockSpec((1,) + st_shape, lambda bb: (bb, 0, 0, 0, 0)),
        ],
        out_shape=[
            jax.ShapeDtypeStruct((b * t, HGRN_WIDTH), BF16),
            jax.ShapeDtypeStruct((b,) + st_shape, F32),
        ],
        scratch_shapes=[
            pltpu.VMEM((2, HGRN_WIDTH, HGRN_WIDTH), F32),
            pltpu.VMEM((t, HGRN_WIDTH), F32),
            pltpu.VMEM((t, HGRN_WIDTH), F32),
            pltpu.VMEM((2, 2, 3, HGRN_CHUNK, HGRN_WIDTH), F32),
            pltpu.VMEM((2, 2, 3, HGRN_CHUNK, HGRN_WIDTH), BF16),
        ],
        compiler_params=_cparams(("arbitrary",)),
        name=name,
    )(proj, proj, proj, proj, proj, lbp, gn_row, m_all, ones_bd, s0t)


def _outproj_kernel(*refs, n_ctx_tiles, n_mix):
    mix = refs[:3 * n_mix]
    refs = refs[3 * n_mix:]
    xs = refs[:-14]
    (mod_ref, w_ref, g_ref, b_ref, rw_ref, rb_ref, x1_ref, hp_ref, meta_ref, gate_ref, cnt_ref, wb_ref, tri_ref,
     run_ref) = refs[-14:]
    tm = x1_ref.shape[0]
    is_ctx = pl.program_id(0) < n_ctx_tiles
    x_in = jnp.where(is_ctx, xs[0][...], xs[1][...]) if len(xs) == 2 else xs[0][...]
    if n_mix == 2:
        attn, four, rec = [jnp.where(is_ctx, mix[2 * j][...], mix[2 * j + 1][...]) for j in range(3)]
    else:
        attn, four, rec = [r[...] for r in mix]

    @pl.when(pl.program_id(0) == 0)
    def _():
        wb_ref[...] = w_ref[0].astype(BF16)
        r = lax.broadcasted_iota(I32, (tm, tm), 0)
        c = lax.broadcasted_iota(I32, (tm, tm), 1)
        tri_ref[...] = jnp.where(r < c, 1.0, 0.0).astype(BF16)
        run_ref[...] = jnp.zeros_like(run_ref)

    out = _dot(attn, wb_ref[0:ATTN_WIDTH, :])
    out = out + _dot(four, wb_ref[ATTN_WIDTH:ATTN_WIDTH + FOURIER_WIDTH, :])
    out = out + _dot(rec, wb_ref[ATTN_WIDTH + FOURIER_WIDTH:, :])
    gate1 = mod_ref[0, 2:3, :]
    y = DEEPNORM_ALPHA * x_in + gate1 * out
    x1 = _ln_plain(y, LN_EPS) * g_ref[...] + b_ref[...]
    x1_ref[...] = x1
    h2 = _ln_plain(x1, ADA_EPS) * (1.0 + mod_ref[0, 4:5, :]) + mod_ref[0, 3:4, :]
    hp_ref[...] = _pack_bf16_pair(h2[:, :HALF_D], h2[:, HALF_D:])

    h_hi = h2.astype(BF16)
    h_lo = (h2 - h_hi.astype(F32)).astype(BF16)
    rwt = rw_ref[...]
    w_hi = rwt.astype(BF16)
    w_lo = (rwt - w_hi.astype(F32)).astype(BF16)
    scores = jax.nn.sigmoid(_dot_nt(w_hi, h_hi) + _dot_nt(w_hi, h_lo) + _dot_nt(w_lo, h_hi))
    remaining = scores + rb_ref[...]
    eidx = lax.broadcasted_iota(I32, scores.shape, 0).astype(F32)
    chosen = jnp.zeros(scores.shape, jnp.bool_)
    picks = []
    for _ in range(TOP_K):
        mx = jnp.max(remaining, axis=0, keepdims=True)
        first = jnp.min(jnp.where(remaining == mx, eidx, float(N_EXPERTS)), axis=0, keepdims=True)
        pick = eidx == first
        picks.append((pick, first))
        chosen = jnp.logical_or(chosen, pick)
        remaining = jnp.where(pick, -jnp.inf, remaining)
    sel = jnp.where(chosen, scores, 0.0)
    gates = sel / jnp.sum(sel, axis=0, keepdims=True) * ROUTED_SCALE

    onehot = jnp.where(chosen, 1.0, 0.0)
    rank = run_ref[...] + _dot(onehot.astype(BF16), tri_ref[...])
    run_ref[...] += jnp.sum(onehot, axis=1, keepdims=True)
    cnt_ref[...] = run_ref[...]

    ids, rks, gks = [], [], []
    for pick, first in picks:
        ids.append(first.astype(I32))
        rks.append(jnp.sum(jnp.where(pick, rank, 0.0), axis=0, keepdims=True).astype(I32))
        gks.append(jnp.sum(jnp.where(pick, gates, 0.0), axis=0, keepdims=True))
    meta_ref[...] = jnp.concatenate(ids + rks, axis=0)
    gate_ref[...] = jnp.concatenate(gks, axis=0)


def _outproj(attn, four, rec, x, mod_l, w_out, layer, g1, b1, rw, rb, lay, tm):
    n_tiles = lay.n // tm
    n_ctx_tiles = lay.n_ctx // tm
    row = lambda i: (i, 0)
    const = lambda i: (0, 0)
    xs = x if isinstance(x, tuple) else (x,)
    return pl.pallas_call(
        functools.partial(_outproj_kernel, n_ctx_tiles=n_ctx_tiles, n_mix=len(attn)),
        grid=(n_tiles,),
        in_specs=[
            *_group_specs(len(attn), tm, ATTN_WIDTH, n_ctx_tiles),
            *_group_specs(len(four), tm, FOURIER_WIDTH, n_ctx_tiles),
            *_group_specs(len(rec), tm, HGRN_WIDTH, n_ctx_tiles),
            *_group_specs(len(xs), tm, D_MODEL, n_ctx_tiles),
            pl.BlockSpec((1, N_MOD, D_MODEL), lambda i: (lay.cond_row(i, tm), 0, 0)),
            pl.BlockSpec((1, D_MODEL, D_MODEL), lambda i: (layer, 0, 0)),
            pl.BlockSpec((1, D_MODEL), const),
            pl.BlockSpec((1, D_MODEL), const),
            pl.BlockSpec((N_EXPERTS, D_MODEL), const),
            pl.BlockSpec((N_EXPERTS, 1), const),
        ],
        out_specs=[
            pl.BlockSpec((tm, D_MODEL), row),
            pl.BlockSpec((tm, HALF_D), row),
            pl.BlockSpec((2 * TOP_K, tm), lambda i: (0, i)),
            pl.BlockSpec((TOP_K, tm), lambda i: (0, i)),
            pl.BlockSpec((N_EXPERTS, 1), const),
        ],
        out_shape=[
            jax.ShapeDtypeStruct((lay.n, D_MODEL), F32),
            jax.ShapeDtypeStruct((lay.n, HALF_D), I32),
            jax.ShapeDtypeStruct((2 * TOP_K, lay.n), I32),
            jax.ShapeDtypeStruct((TOP_K, lay.n), F32),
            jax.ShapeDtypeStruct((N_EXPERTS, 1), F32),
        ],
        scratch_shapes=[
            pltpu.VMEM((D_MODEL, D_MODEL), BF16),
            pltpu.VMEM((tm, tm), BF16),
            pltpu.VMEM((N_EXPERTS, 1), F32),
        ],
        compiler_params=_cparams(("arbitrary",)),
        name="outproj_router" + lay.tag,
    )(*attn, *four, *rec, *xs, mod_l, w_out, g1, b1, rw, rb)


def _sc_workers():
    info = plsc.get_sparse_core_info()
    return info.num_cores, info.num_cores * info.num_subcores


def _sc_scatter_rows(rows, pos_b, r_out, tag=""):
    nc, nw = _sc_workers()
    n, w = rows.shape
    nbt, copies, _ = pos_b.shape
    assert nbt * SC_BATCH == n and nbt % (2 * nw) == 0
    per_w = nbt // nw
    mesh = plsc.VectorSubcoreMesh(core_axis_name="c", subcore_axis_name="s")

    @functools.partial(
        pl.kernel, mesh=mesh, out_type=jax.ShapeDtypeStruct((r_out, w), rows.dtype),
        scratch_types=[pltpu.VMEM((copies, SC_BATCH), I32), pltpu.VMEM((copies, SC_BATCH), I32),
                       pltpu.VMEM((SC_BATCH, w), rows.dtype), pltpu.VMEM((SC_BATCH, w), rows.dtype),
                       pltpu.SemaphoreType.DMA, pltpu.SemaphoreType.DMA,
                       pltpu.SemaphoreType.DMA, pltpu.SemaphoreType.DMA],
        name="sc_dispatch" + tag)
    def k(rows_hbm, pos_hbm, out_hbm, idx_a, idx_b, rows_a, rows_b, sem_ra, sem_rb, sem_sa, sem_sb):
        wid = lax.axis_index("s") * nc + lax.axis_index("c")
        first = wid * per_w

        def reads(j, idx_v, rows_v, sem):
            bt = first + j
            return (pltpu.make_async_copy(pos_hbm.at[bt], idx_v, sem),
                    pltpu.make_async_copy(rows_hbm.at[pl.ds(bt * SC_BATCH, SC_BATCH)], rows_v, sem))

        def scatters(idx_v, rows_v, sem):
            return [pltpu.make_async_copy(rows_v, out_hbm.at[idx_v.at[q]], sem) for q in range(copies)]

        def start(descs):
            for d in descs:
                d.start()

        def wait(descs):
            for d in descs:
                d.wait()

        start(reads(0, idx_a, rows_a, sem_ra))

        @pl.loop(0, per_w // 2)
        def _(p):
            j0 = 2 * p
            j1 = j0 + 1

            @pl.when(p > 0)
            def _():
                wait(scatters(idx_b, rows_b, sem_sb))

            start(reads(j1, idx_b, rows_b, sem_rb))
            wait(reads(j0, idx_a, rows_a, sem_ra))
            start(scatters(idx_a, rows_a, sem_sa))
            wait(reads(j1, idx_b, rows_b, sem_rb))
            start(scatters(idx_b, rows_b, sem_sb))
            wait(scatters(idx_a, rows_a, sem_sa))

            @pl.when(p + 1 < per_w // 2)
            def _():
                start(reads(j0 + 2, idx_a, rows_a, sem_ra))

        wait(scatters(idx_b, rows_b, sem_sb))

    return k(rows, pos_b)


def _sc_gather_rows(table, idx, tag=""):
    nc, nw = _sc_workers()
    r = idx.shape[0]
    w = table.shape[1]
    assert r % (2 * nw * SC_BATCH) == 0
    per_w = r // nw
    nb = per_w // SC_BATCH
    mesh = plsc.VectorSubcoreMesh(core_axis_name="c", subcore_axis_name="s")

    @functools.partial(
        pl.kernel, mesh=mesh, out_type=jax.ShapeDtypeStruct((r, w), table.dtype),
        scratch_types=[pltpu.VMEM((per_w,), I32),
                       pltpu.VMEM((SC_BATCH, w), table.dtype), pltpu.VMEM((SC_BATCH, w), table.dtype),
                       pltpu.SemaphoreType.DMA, pltpu.SemaphoreType.DMA,
                       pltpu.SemaphoreType.DMA, pltpu.SemaphoreType.DMA],
        name="sc_combine" + tag)
    def k(table_hbm, idx_hbm, out_hbm, idx_v, rows_a, rows_b, sem_ga, sem_gb, sem_wa, sem_wb):
        wid = lax.axis_index("s") * nc + lax.axis_index("c")
        base = wid * per_w
        pltpu.sync_copy(idx_hbm.at[pl.ds(base, per_w)], idx_v)

        def gather(j, rows_v, sem):
            return pltpu.make_async_copy(table_hbm.at[idx_v.at[pl.ds(j * SC_BATCH, SC_BATCH)]], rows_v, sem)

        def write(j, rows_v, sem):
            return pltpu.make_async_copy(rows_v, out_hbm.at[pl.ds(base + j * SC_BATCH, SC_BATCH)], sem)

        gather(0, rows_a, sem_ga).start()

        @pl.loop(0, nb // 2)
        def _(p):
            j0 = 2 * p
            j1 = j0 + 1

            @pl.when(p > 0)
            def _():
                write(j1 - 2, rows_b, sem_wb).wait()

            gather(j1, rows_b, sem_gb).start()
            gather(j0, rows_a, sem_ga).wait()
            write(j0, rows_a, sem_wa).start()
            gather(j1, rows_b, sem_gb).wait()
            write(j1, rows_b, sem_wb).start()
            write(j0, rows_a, sem_wa).wait()

            @pl.when(p + 1 < nb // 2)
            def _():
                gather(j0 + 2, rows_a, sem_ga).start()

        write(nb - 1, rows_b, sem_wb).wait()

    return k(table, idx)


def _experts_kernel(te_ref, na_ref, x_ref, w1_ref, w3_ref, w2_ref, o_ref, w1b_ref, w3b_ref, w2b_ref):
    del te_ref

    @pl.when(pl.program_id(0) < na_ref[0])
    def _():
        w1b_ref[...] = w1_ref[0, 0].astype(BF16)
        w3b_ref[...] = w3_ref[0, 0].astype(BF16)
        w2b_ref[...] = w2_ref[0, 0].astype(BF16)
        lo, hi = _unpack_bf16_pair(x_ref[...])
        lo = lo.astype(BF16)
        hi = hi.astype(BF16)
        a = _dot(lo, w1b_ref[0:HALF_D, :]) + _dot(hi, w1b_ref[HALF_D:, :])
        b = _dot(lo, w3b_ref[0:HALF_D, :]) + _dot(hi, w3b_ref[HALF_D:, :])
        y = _dot((_silu(a) * b).astype(BF16), w2b_ref[...])
        o_ref[...] = _pack_bf16_pair(y[:, :HALF_D], y[:, HALF_D:])


def _experts(xs, tile_expert, n_active, w1, w3, w2, layer, tm, tag):
    r = xs.shape[0]
    n_tiles = r // tm

    def xmap(j, te, na):
        return (jnp.minimum(j, na[0] - 1), 0)

    def wmap(j, te, na):
        return (layer, te[jnp.minimum(j, na[0] - 1)], 0, 0)

    grid_spec = pltpu.PrefetchScalarGridSpec(
        num_scalar_prefetch=2,
        grid=(n_tiles,),
        in_specs=[
            pl.BlockSpec((tm, HALF_D), xmap),
            pl.BlockSpec((1, 1, D_MODEL, EXPERT_FF), wmap),
            pl.BlockSpec((1, 1, D_MODEL, EXPERT_FF), wmap),
            pl.BlockSpec((1, 1, EXPERT_FF, D_MODEL), wmap),
        ],
        out_specs=pl.BlockSpec((tm, HALF_D), xmap),
        scratch_shapes=[
            pltpu.VMEM((D_MODEL, EXPERT_FF), BF16),
            pltpu.VMEM((D_MODEL, EXPERT_FF), BF16),
            pltpu.VMEM((EXPERT_FF, D_MODEL), BF16),
        ],
    )
    return pl.pallas_call(
        _experts_kernel,
        grid_spec=grid_spec,
        out_shape=jax.ShapeDtypeStruct((r, HALF_D), I32),
        compiler_params=_cparams(("arbitrary",)),
        name="experts" + tag,
    )(tile_expert, n_active, xs, w1, w3, w2)


def _combine_kernel(yp_ref, gate_ref, hp_ref, sw1_ref, sw3_ref, sw2_ref, x_ref, mod_ref, g_ref, b_ref, *refs,
                    n_ctx_tiles):
    outs = refs[:-3]
    w1b_ref, w3b_ref, w2b_ref = refs[-3:]

    @pl.when(pl.program_id(0) == 0)
    def _():
        w1b_ref[...] = sw1_ref[...].astype(BF16)
        w3b_ref[...] = sw3_ref[...].astype(BF16)
        w2b_ref[...] = sw2_ref[...].astype(BF16)

    lo, hi = _unpack_bf16_pair(hp_ref[...])
    lo = lo.astype(BF16)
    hi = hi.astype(BF16)
    a = _dot(lo, w1b_ref[0:HALF_D, :]) + _dot(hi, w1b_ref[HALF_D:, :])
    b = _dot(lo, w3b_ref[0:HALF_D, :]) + _dot(hi, w3b_ref[HALF_D:, :])
    shared = _dot((_silu(a) * b).astype(BF16), w2b_ref[...])
    acc_lo = shared[:, :HALF_D]
    acc_hi = shared[:, HALF_D:]
    gates = gate_ref[...]
    for k in range(TOP_K):
        ylo, yhi = _unpack_bf16_pair(yp_ref[k])
        gk = gates[:, k:k + 1]
        acc_lo = acc_lo + gk * ylo
        acc_hi = acc_hi + gk * yhi
    moe = jnp.concatenate([acc_lo, acc_hi], axis=1)
    y = DEEPNORM_ALPHA * x_ref[...] + mod_ref[0, 5:6, :] * moe
    res = _ln_plain(y, LN_EPS) * g_ref[...] + b_ref[...]
    if len(outs) == 1:
        outs[0][...] = res
    else:
        @pl.when(pl.program_id(0) < n_ctx_tiles)
        def _():
            outs[0][...] = res

        @pl.when(pl.program_id(0) >= n_ctx_tiles)
        def _():
            outs[1][...] = res


def _combine(yp, gate8, hp, sw1, sw3, sw2, x1, mod_l, g2, b2, lay, tm, split_out):
    n_tiles = lay.n // tm
    n_ctx_tiles = lay.n_ctx // tm
    row = lambda i: (i, 0)
    const = lambda i: (0, 0)
    if split_out:
        out_specs = _group_specs(2, tm, D_MODEL, n_ctx_tiles)
        out_shape = [jax.ShapeDtypeStruct((lay.n_ctx, D_MODEL), F32), jax.ShapeDtypeStruct((lay.n_lat, D_MODEL), F32)]
    else:
        out_specs = pl.BlockSpec((tm, D_MODEL), row)
        out_shape = jax.ShapeDtypeStruct((lay.n, D_MODEL), F32)
    return pl.pallas_call(
        functools.partial(_combine_kernel, n_ctx_tiles=n_ctx_tiles),
        grid=(n_tiles,),
        in_specs=[
            pl.BlockSpec((TOP_K, tm, HALF_D), lambda i: (0, i, 0)),
            pl.BlockSpec((tm, TOP_K), row),
            pl.BlockSpec((tm, HALF_D), row),
            pl.BlockSpec((D_MODEL, EXPERT_FF), const),
            pl.BlockSpec((D_MODEL, EXPERT_FF), const),
            pl.BlockSpec((EXPERT_FF, D_MODEL), const),
            pl.BlockSpec((tm, D_MODEL), row),
            pl.BlockSpec((1, N_MOD, D_MODEL), lambda i: (lay.cond_row(i, tm), 0, 0)),
            pl.BlockSpec((1, D_MODEL), const),
            pl.BlockSpec((1, D_MODEL), const),
        ],
        out_specs=out_specs,
        out_shape=out_shape,
        scratch_shapes=[
            pltpu.VMEM((D_MODEL, EXPERT_FF), BF16),
            pltpu.VMEM((D_MODEL, EXPERT_FF), BF16),
            pltpu.VMEM((EXPERT_FF, D_MODEL), BF16),
        ],
        compiler_params=_cparams(("arbitrary",)),
        name="combine_norm" + lay.tag,
    )(yp, gate8, hp, sw1, sw3, sw2, x1, mod_l, g2, b2)


def _moe_dispatch(hp, meta, counts, lay, tile):
    n = lay.n
    r_max = n * TOP_K + N_EXPERTS * tile
    n_tiles = r_max // tile
    cnt = counts.reshape(N_EXPERTS).astype(I32)
    padded = ((cnt + tile - 1) // tile) * tile
    ends = jnp.cumsum(padded)
    offsets = ends - padded
    idx8 = meta[:TOP_K]
    base8 = jnp.sum(jnp.where(idx8[:, :, None] == jnp.arange(N_EXPERTS, dtype=I32), offsets, 0), axis=-1)
    pos = (base8 + meta[TOP_K:]).astype(I32)
    tile_start = jnp.arange(n_tiles, dtype=I32) * tile
    tile_expert = jnp.minimum(jnp.sum(tile_start[:, None] >= ends[None, :], axis=1), N_EXPERTS - 1).astype(I32)
    n_active = (ends[-1] // tile).astype(I32).reshape(1)
    pos_b = pos.reshape(TOP_K, n // SC_BATCH, SC_BATCH).transpose(1, 0, 2)
    xs = _sc_scatter_rows(hp, pos_b, r_max, lay.tag)
    return xs, tile_expert, n_active, pos


def _moe_combine(ys, pos, gate8, hp, sw1, sw3, sw2, x1, mod_l, g2, b2, lay, split_out):
    n = lay.n
    yp = _sc_gather_rows(ys, pos.reshape(n * TOP_K), lay.tag).reshape(TOP_K, n, HALF_D)
    return _combine(yp, gate8.T, hp, sw1, sw3, sw2, x1, mod_l, g2, b2, lay, TOKEN_TILE, split_out)


def kernel(x_prompt, x_sample, cache_k, cache_v, state_hgrn, c, c_ctx, w_ada, b_ada, w_in, w_out, attn_sink, hgrn_lb, hgrn_norm, ln1_g, ln1_b, ln2_g, ln2_b, router_w, router_b, moe_w1, moe_w3, moe_w2, shared_w1, shared_w3, shared_w2):
    b_ctx, t_ctx, _ = x_prompt.shape
    b_lat, t_lat, _ = x_sample.shape
    past = cache_k.shape[2]
    tm = TOKEN_TILE
    assert 1 + b_lat <= COND_ROWS
    lay = _Layout(b_ctx, t_ctx, b_lat, t_lat)
    assert lay.n_ctx % tm == 0 and t_lat % tm == 0 and lay.n_ctx % t_lat == 0

    cond = jnp.concatenate([c_ctx[None, :], c, jnp.zeros((COND_ROWS - 1 - b_lat, D_MODEL), F32)], axis=0)
    mod = _adaln(cond, w_ada, b_ada).reshape(DEPTH, COND_ROWS, N_MOD, D_MODEL)

    lb_all = jnp.cumsum(jax.nn.softmax(hgrn_lb.astype(F32), axis=0), axis=0)
    lb_all = lb_all - lb_all[:1]
    lbp = jnp.stack([jnp.log(lb_all), jnp.log1p(-lb_all), 1.0 - lb_all], axis=2)

    cos_t, sin_t = _rope_tables(lay, tm)
    zero_state = jnp.zeros((b_ctx, 2, HGRN_HEADS, HGRN_DK, HGRN_DK), F32)

    def layer(l, lay, x, split_out):
        lat = slice(lay.lat_first, lay.lat_first + lay.b_lat)
        outs = _inproj(x, mod[l], w_in, l, cos_t, sin_t, lay, tm)
        proj, proj_h = outs[0], outs[1]
        sink_l = attn_sink[l].reshape(1, N_HEADS)
        gn_row = jnp.tile(hgrn_norm[l], HGRN_HEADS).reshape(1, HGRN_WIDTH)
        attn, four, rec, extras = [], [], [], None
        if lay.b_ctx:
            attn.append(_attn_context(proj, sink_l, lay))
            four.append(_fourier(proj, 0, lay.b_ctx, t_ctx, t_ctx, "fourier_ctx"))
            rec_c, s_fin = _hgrn(proj_h, 0, lay.b_ctx, t_ctx, lbp[l], gn_row, zero_state, "hgrn_ctx")
            rec.append(rec_c)
            extras = (outs[2], outs[3], s_fin)
        attn.append(_attn_latent(proj, cache_k[lat, l].reshape(lay.b_lat, past, KV_WIDTH),
                                 cache_v[lat, l].reshape(lay.b_lat, past, KV_WIDTH), sink_l, lay))
        four.append(_fourier(proj, lay.n_ctx, lay.b_lat, t_lat, min(t_lat, 512), "fourier_lat" + lay.tag))
        s0t = jnp.swapaxes(state_hgrn[lat, l].astype(F32), -1, -2)
        rec.append(_hgrn(proj_h, lay.n_ctx, lay.b_lat, t_lat, lbp[l], gn_row, s0t, "hgrn_lat" + lay.tag)[0])
        x1, hp, meta, gate8, counts = _outproj(
            tuple(attn), tuple(four), tuple(rec), x, mod[l], w_out, l, ln1_g[l].reshape(1, -1),
            ln1_b[l].reshape(1, -1), router_w[l].T, router_b[l].reshape(-1, 1), lay, tm)
        xs, tile_expert, n_active, pos = _moe_dispatch(hp, meta, counts, lay, EXPERT_TILE)
        ys = _experts(xs, tile_expert, n_active, moe_w1, moe_w3, moe_w2, l, EXPERT_TILE, lay.tag)
        x = _moe_combine(ys, pos, gate8, hp, shared_w1[l], shared_w3[l], shared_w2[l], x1, mod[l],
                         ln2_g[l].reshape(1, -1), ln2_b[l].reshape(1, -1), lay, split_out)
        return x, extras

    x = (x_prompt.reshape(lay.n_ctx, D_MODEL), x_sample.reshape(lay.n_lat, D_MODEL))
    ks_out, vs_out, ss_out = [], [], []
    for l in range(DEPTH):
        x, (k_new, v_new, s_fin) = layer(l, lay, x, split_out=(l == DEPTH - 1))
        ks_out.append(k_new.reshape(b_ctx, t_ctx, N_KV_HEADS, HEAD_DIM))
        vs_out.append(v_new.reshape(b_ctx, t_ctx, N_KV_HEADS, HEAD_DIM))
        ss_out.append(jnp.swapaxes(s_fin, -1, -2))

    y_prompt = x[0].reshape(b_ctx, t_ctx, D_MODEL)
    y_sample = x[1].reshape(b_lat, t_lat, D_MODEL)
    new_cache_k = jnp.stack(ks_out, axis=1)
    new_cache_v = jnp.stack(vs_out, axis=1)
    new_state = jnp.stack(ss_out, axis=1).astype(x_prompt.dtype)
    return (y_prompt, y_sample, new_cache_k, new_cache_v, new_state)
```

```python
import functools
import math

import numpy as np
import jax
import jax.numpy as jnp
from jax import lax
from jax.experimental import pallas as pl
from jax.experimental.pallas import tpu as pltpu
from jax.experimental.pallas import tpu_sc as plsc

F32 = jnp.float32
BF16 = jnp.bfloat16
I32 = jnp.int32

D_MODEL = 1024
HALF_D = D_MODEL // 2
DEPTH = 2
GRID_W = 64
ROPE_BASE = 10000.0
HEAD_DIM = 64
ATTN_WIDTH = 512
N_HEADS = 8
N_KV_HEADS = 2
KV_GROUP = 4
KV_WIDTH = N_KV_HEADS * HEAD_DIM
WINDOW = 128
ATTN_BLOCK = 128
FOURIER_WIDTH = 256
FOURIER_GROUPS = 4
HGRN_WIDTH = 256
HGRN_HEADS = 4
HGRN_DK = 64
HGRN_CHUNK = 64
IN_WIDTH = 2304
N_EXPERTS = 64
TOP_K = 8
EXPERT_FF = 256
ROUTED_SCALE = 2.5
N_MOD = 6
LN_EPS = 1e-5
ADA_EPS = 1e-6
GN_EPS = 1e-6
DEEPNORM_ALPHA = (2 * DEPTH) ** 0.25

COL_Q = 0
COL_K = 512
COL_V = 640
COL_U = 768
COL_HQ = 1024
COL_FF = 1280
COL_FB = 1536
COL_HI = 1792
COL_HG = 2048
ROPE_COLS = COL_V
MIX_IN_WIDTH = COL_HQ

V7X_LANES = 128
COND_ROWS = 16
NEG_BIG = -1e30
TOKEN_TILE = 512
EXPERT_TILE = 1024
SC_BATCH = 64

VMEM_LIMIT = 56 * 1024 * 1024


def _cparams(sem):
    return pltpu.CompilerParams(dimension_semantics=sem, vmem_limit_bytes=VMEM_LIMIT)


def _dot(a, b):
    return jnp.dot(a, b, preferred_element_type=F32)


def _dot_nt(a, b):
    return lax.dot_general(a, b, (((1,), (1,)), ((), ())), preferred_element_type=F32)


def _dot_tn(a, b):
    return lax.dot_general(a, b, (((0,), (0,)), ((), ())), preferred_element_type=F32)


def _split3(x):
    hi = x.astype(BF16)
    r1 = x - hi.astype(F32)
    mid = r1.astype(BF16)
    lo = (r1 - mid.astype(F32)).astype(BF16)
    return hi, mid, lo


def _dot_exact_lhs(m_bf16, x):
    hi, mid, lo = _split3(x)
    return _dot(m_bf16, hi) + _dot(m_bf16, mid) + _dot(m_bf16, lo)


def _dot_exact_rhs(x, m_bf16):
    hi, mid, lo = _split3(x)
    return _dot(hi, m_bf16) + _dot(mid, m_bf16) + _dot(lo, m_bf16)


def _dot_hp(a, b):
    a_hi = a.astype(BF16)
    a_lo = (a - a_hi.astype(F32)).astype(BF16)
    b_hi = b.astype(BF16)
    b_lo = (b - b_hi.astype(F32)).astype(BF16)
    return _dot(a_hi, b_hi) + _dot(a_hi, b_lo) + _dot(a_lo, b_hi)


def _pack_bf16_pair(lo, hi):
    return lax.bitcast_convert_type(pltpu.pack_elementwise([lo, hi], packed_dtype=BF16), I32)


def _unpack_bf16_pair(w):
    u = lax.bitcast_convert_type(w, jnp.uint32)
    lo = pltpu.unpack_elementwise(u, index=0, packed_dtype=BF16, unpacked_dtype=F32)
    hi = pltpu.unpack_elementwise(u, index=1, packed_dtype=BF16, unpacked_dtype=F32)
    return lo, hi


def _ln_plain(x, eps):
    mu = jnp.mean(x, axis=-1, keepdims=True)
    xc = x - mu
    var = jnp.mean(xc * xc, axis=-1, keepdims=True)
    return xc * lax.rsqrt(var + eps)


def _silu(x):
    return x * jax.nn.sigmoid(x)


def _adaln_kernel(c_ref, w_ref, b_ref, o_ref):
    s = _silu(c_ref[...])
    o_ref[0] = _dot_hp(s, w_ref[0]) + b_ref[0]


def _adaln(cond, w_ada, b_ada):
    return pl.pallas_call(
        _adaln_kernel,
        grid=(DEPTH, N_MOD),
        in_specs=[
            pl.BlockSpec((COND_ROWS, D_MODEL), lambda l, j: (0, 0)),
            pl.BlockSpec((1, D_MODEL, D_MODEL), lambda l, j: (l, 0, j)),
            pl.BlockSpec((1, 1, D_MODEL), lambda l, j: (l, 0, j)),
        ],
        out_specs=pl.BlockSpec((1, COND_ROWS, D_MODEL), lambda l, j: (l, 0, j)),
        out_shape=jax.ShapeDtypeStruct((DEPTH, COND_ROWS, N_MOD * D_MODEL), F32),
        compiler_params=_cparams(("arbitrary", "arbitrary")),
        name="adaln",
    )(cond, w_ada, b_ada.reshape(DEPTH, 1, N_MOD * D_MODEL))


class _Layout:
    def __init__(self, b_ctx, t_ctx, b_lat, t_lat, lat_first=0, tag=""):
        self.b_ctx, self.t_ctx, self.b_lat, self.t_lat = b_ctx, t_ctx, b_lat, t_lat
        self.lat_first = lat_first
        self.n_ctx = b_ctx * t_ctx
        self.n_lat = b_lat * t_lat
        self.n = self.n_ctx + self.n_lat
        self.tag = tag

    def cond_row(self, tile, tm):
        n_ctx_tiles = self.n_ctx // tm
        per_batch = self.t_lat // tm
        return jnp.where(tile < n_ctx_tiles, 0, 1 + self.lat_first + (tile - n_ctx_tiles) // per_batch)


def _group_specs(n_arrays, tm, width, n_ctx_tiles):
    if n_arrays == 1:
        return [pl.BlockSpec((tm, width), lambda i: (i, 0))]
    return [pl.BlockSpec((tm, width), lambda i: (jnp.minimum(i, n_ctx_tiles - 1), 0)),
            pl.BlockSpec((tm, width), lambda i: (jnp.maximum(i - n_ctx_tiles, 0), 0))]


def _inproj_kernel(*refs, n_ctx_tiles):
    n_tail = 9 if n_ctx_tiles > 0 else 7
    xs = refs[:-n_tail]
    mod_ref, w_ref, cos_ref, sin_ref, oa_ref, oh_ref = refs[-n_tail:-n_tail + 6]
    wb_ref = refs[-1]

    @pl.when(pl.program_id(0) == 0)
    def _():
        wb_ref[...] = w_ref[0].astype(BF16)

    if len(xs) == 2:
        x = jnp.where(pl.program_id(0) < n_ctx_tiles, xs[0][...], xs[1][...])
    else:
        x = xs[0][...]
    shift = mod_ref[0, 0:1, :]
    scale = mod_ref[0, 1:2, :]
    h = (_ln_plain(x, ADA_EPS) * (1.0 + scale) + shift).astype(BF16)
    p = _dot(h, wb_ref[...])
    cos = cos_ref[...]
    sin = sin_ref[...]
    lane = lax.broadcasted_iota(I32, cos.shape, 1)
    first_half = (lane & 31) < 16
    for cb in range(ROPE_COLS // V7X_LANES):
        seg = p[:, cb * V7X_LANES:(cb + 1) * V7X_LANES]
        partner = jnp.where(first_half, pltpu.roll(seg, V7X_LANES - 16, 1), pltpu.roll(seg, 16, 1))
        oa_ref[:, cb * V7X_LANES:(cb + 1) * V7X_LANES] = (seg * cos + partner * sin).astype(BF16)
    oa_ref[:, ROPE_COLS:] = p[:, ROPE_COLS:MIX_IN_WIDTH].astype(BF16)
    oh_ref[...] = p[:, MIX_IN_WIDTH:]

    if n_ctx_tiles > 0:
        kc_ref, vc_ref = refs[-3], refs[-2]

        @pl.when(pl.program_id(0) < n_ctx_tiles)
        def _():
            kc_ref[...] = p[:, COL_K:COL_K + KV_WIDTH]
            vc_ref[...] = p[:, COL_V:COL_V + KV_WIDTH]


def _rope_tables(lay, tm):
    t = lay.t_lat
    pos = jnp.arange(t)
    row = (pos // GRID_W).astype(F32)
    col = (pos % GRID_W).astype(F32)
    n_freq = HEAD_DIM // 4
    inv = ROPE_BASE ** (-jnp.arange(n_freq, dtype=F32) / n_freq)
    ang_r = row[:, None] * inv
    ang_c = col[:, None] * inv
    ang = jnp.concatenate([ang_r, ang_r, ang_c, ang_c], axis=1)
    sign = jnp.concatenate([-jnp.ones(n_freq), jnp.ones(n_freq), -jnp.ones(n_freq), jnp.ones(n_freq)]).astype(F32)
    cos = jnp.cos(ang)
    sin = jnp.sin(ang) * sign
    cos = jnp.concatenate([jnp.ones((tm, HEAD_DIM), F32), cos], axis=0)
    sin = jnp.concatenate([jnp.zeros((tm, HEAD_DIM), F32), sin], axis=0)
    return jnp.tile(cos, (1, 2)), jnp.tile(sin, (1, 2))


def _inproj(x, mod_l, w_in, layer, cos_t, sin_t, lay, tm):
    n_tiles = lay.n // tm
    n_ctx_tiles = lay.n_ctx // tm
    per_batch = lay.t_lat // tm

    def tbl(i):
        return jnp.where(i < n_ctx_tiles, 0, 1 + (i - n_ctx_tiles) % per_batch)

    xs = x if isinstance(x, tuple) else (x,)
    kv_specs, kv_shapes = [], []
    if n_ctx_tiles > 0:
        kv_specs = [pl.BlockSpec((tm, KV_WIDTH), lambda i: (jnp.minimum(i, n_ctx_tiles - 1), 0))] * 2
        kv_shapes = [jax.ShapeDtypeStruct((lay.n_ctx, KV_WIDTH), F32)] * 2
    return pl.pallas_call(
        functools.partial(_inproj_kernel, n_ctx_tiles=n_ctx_tiles),
        grid=(n_tiles,),
        in_specs=[
            *_group_specs(len(xs), tm, D_MODEL, n_ctx_tiles),
            pl.BlockSpec((1, N_MOD, D_MODEL), lambda i: (lay.cond_row(i, tm), 0, 0)),
            pl.BlockSpec((1, D_MODEL, IN_WIDTH), lambda i: (layer, 0, 0), pipeline_mode=pl.Buffered(1)),
            pl.BlockSpec((tm, V7X_LANES), lambda i: (tbl(i), 0)),
            pl.BlockSpec((tm, V7X_LANES), lambda i: (tbl(i), 0)),
        ],
        out_specs=[pl.BlockSpec((tm, MIX_IN_WIDTH), lambda i: (i, 0)),
                   pl.BlockSpec((tm, IN_WIDTH - MIX_IN_WIDTH), lambda i: (i, 0))] + kv_specs,
        out_shape=[jax.ShapeDtypeStruct((lay.n, MIX_IN_WIDTH), BF16),
                   jax.ShapeDtypeStruct((lay.n, IN_WIDTH - MIX_IN_WIDTH), F32)] + kv_shapes,
        scratch_shapes=[pltpu.VMEM((D_MODEL, IN_WIDTH), BF16)],
        compiler_params=_cparams(("arbitrary",)),
        name="inproj" + lay.tag,
    )(*xs, mod_l, w_in, cos_t, sin_t)


def _attn_kernel(sink_ref, q_ref, *refs, n_local, has_ctx, t_total):
    o_ref = refs[-1]
    k_refs = refs[:n_local]
    v_refs = refs[n_local:2 * n_local]
    tq = q_ref.shape[0]
    scale = HEAD_DIM ** -0.5
    k_parts = [kr[...] for kr in k_refs]
    v_parts = [vr[...] for vr in v_refs]
    if has_ctx:
        k_parts.append(refs[2 * n_local][0].astype(k_parts[0].dtype))
        v_parts.append(refs[2 * n_local + 1][0].astype(v_parts[0].dtype))
    kall = (jnp.concatenate(k_parts, axis=0) if len(k_parts) > 1 else k_parts[0]).astype(F32)
    vall = (jnp.concatenate(v_parts, axis=0) if len(v_parts) > 1 else v_parts[0]).astype(F32)
    nk = kall.shape[0]
    k_sw = pltpu.roll(kall, HEAD_DIM, 1)
    v_sw = pltpu.roll(vall, HEAD_DIM, 1)
    lo_half = lax.broadcasted_iota(I32, (1, V7X_LANES), 1) < HEAD_DIM
    er = jnp.where(lax.broadcasted_iota(I32, (2 * nk, V7X_LANES), 0) < nk, 0, 1)
    el = jnp.where(lax.broadcasted_iota(I32, (2 * nk, V7X_LANES), 1) < HEAD_DIM, 0, 1)
    ones_blk = jnp.where(er == el, 1.0, 0.0).astype(BF16)
    if n_local > 1:
        i = pl.program_id(1)
        band = refs[-2][...]
        first_blk = jnp.where(i == 0, NEG_BIG, 0.0)
        last_blk = jnp.where(i == t_total // tq - 1, NEG_BIG, 0.0)

        def mask_local(sc):
            loc = sc[:, :n_local * tq] + band
            parts = [loc[:, :tq] + first_blk, loc[:, tq:(n_local - 1) * tq], loc[:, (n_local - 1) * tq:] + last_blk]
            return jnp.concatenate(parts + [sc[:, n_local * tq:]], axis=1)
    else:
        mask_local = None
    v2es, sinks, scores = [], [], []
    for g in range(N_KV_HEADS):
        k_own, k_oth = (kall, k_sw) if g == 0 else (k_sw, kall)
        v_own, v_oth = (vall, v_sw) if g == 0 else (v_sw, vall)
        k2 = jnp.concatenate([jnp.where(lo_half, k_own, 0.0), jnp.where(lo_half, 0.0, k_oth)], axis=0).astype(BF16)
        v2 = jnp.concatenate([jnp.where(lo_half, v_own, 0.0), jnp.where(lo_half, 0.0, v_oth)], axis=0).astype(BF16)
        v2es.append(jnp.concatenate([v2, ones_blk], axis=1))
        pairs = [2 * g, 2 * g + 1]
        qq = jnp.concatenate([q_ref[:, p * V7X_LANES:(p + 1) * V7X_LANES] for p in pairs], axis=0)
        qq = (qq.astype(F32) * scale).astype(BF16)
        sinks.append((jnp.concatenate([jnp.full((tq, 1), sink_ref[0, 2 * p], F32) for p in pairs], axis=0),
                      jnp.concatenate([jnp.full((tq, 1), sink_ref[0, 2 * p + 1], F32) for p in pairs], axis=0)))
        scores.append(_dot_nt(qq, k2))
    pes, maxes = [], []
    for g in range(N_KV_HEADS):
        s_a = scores[g][:, :nk]
        s_b = scores[g][:, nk:]
        if mask_local is not None:
            s_a = mask_local(s_a)
            s_b = mask_local(s_b)
        m_a = jnp.maximum(jnp.max(s_a, axis=1, keepdims=True), sinks[g][0])
        m_b = jnp.maximum(jnp.max(s_b, axis=1, keepdims=True), sinks[g][1])
        pes.append(jnp.concatenate([jnp.exp(s_a - m_a).astype(BF16), jnp.exp(s_b - m_b).astype(BF16)], axis=1))
        maxes.append((m_a, m_b))
    for g in range(N_KV_HEADS):
        acc = _dot(pes[g], v2es[g])
        (m_a, m_b), (sink_a, sink_b) = maxes[g], sinks[g]
        sink_term = jnp.where(lo_half, jnp.exp(sink_a - m_a), jnp.exp(sink_b - m_b))
        o = acc[:, :V7X_LANES] / (acc[:, V7X_LANES:] + sink_term)
        for j, p in enumerate([2 * g, 2 * g + 1]):
            o_ref[:, p * V7X_LANES:(p + 1) * V7X_LANES] = o[j * tq:(j + 1) * tq].astype(o_ref.dtype)


def _attn_context(proj, sink_l, lay):
    t = lay.t_ctx
    kb, vb = COL_K // KV_WIDTH, COL_V // KV_WIDTH
    body = functools.partial(_attn_kernel, n_local=1, has_ctx=False, t_total=t)
    return pl.pallas_call(
        body,
        grid=(lay.b_ctx,),
        in_specs=[
            pl.BlockSpec(memory_space=pltpu.SMEM),
            pl.BlockSpec((t, ATTN_WIDTH), lambda b: (b, 0)),
            pl.BlockSpec((t, KV_WIDTH), lambda b: (b, kb)),
            pl.BlockSpec((t, KV_WIDTH), lambda b: (b, vb)),
        ],
        out_specs=pl.BlockSpec((t, ATTN_WIDTH), lambda b: (b, 0)),
        out_shape=jax.ShapeDtypeStruct((lay.n_ctx, ATTN_WIDTH), BF16),
        compiler_params=_cparams(("arbitrary",)),
        name="attn_ctx",
    )(sink_l, proj, proj, proj)


def _attn_latent(proj, k_ctx, v_ctx, sink_l, lay):
    t = lay.t_lat
    tq = ATTN_BLOCK
    nq = t // tq
    base = lay.n_ctx // tq
    kb, vb = COL_K // KV_WIDTH, COL_V // KV_WIDTH
    past = k_ctx.shape[1]

    def rows(off):
        return lambda b, i: base + b * nq + jnp.clip(i + off, 0, nq - 1)

    def kv_specs(col):
        return [pl.BlockSpec((tq, KV_WIDTH), (lambda b, i, f=rows(off): (f(b, i), col))) for off in (-1, 0, 1)]

    body = functools.partial(_attn_kernel, n_local=3, has_ctx=True, t_total=t)
    rel = np.arange(3 * tq)[None, :] - tq - (np.arange(2 * tq)[:, None] % tq)
    band = jnp.asarray(np.where(np.abs(rel) <= WINDOW, 0.0, NEG_BIG).astype(np.float32))
    return pl.pallas_call(
        body,
        grid=(lay.b_lat, nq),
        in_specs=[
            pl.BlockSpec(memory_space=pltpu.SMEM),
            pl.BlockSpec((tq, ATTN_WIDTH), lambda b, i: (base + b * nq + i, 0)),
            *kv_specs(kb),
            *kv_specs(vb),
            pl.BlockSpec((1, past, KV_WIDTH), lambda b, i: (b, 0, 0)),
            pl.BlockSpec((1, past, KV_WIDTH), lambda b, i: (b, 0, 0)),
            pl.BlockSpec(band.shape, lambda b, i: (0, 0)),
        ],
        out_specs=pl.BlockSpec((tq, ATTN_WIDTH), lambda b, i: (b * nq + i, 0)),
        out_shape=jax.ShapeDtypeStruct((lay.n_lat, ATTN_WIDTH), BF16),
        compiler_params=_cparams(("arbitrary", "arbitrary")),
        name="attn_lat" + lay.tag,
    )(sink_l, proj, proj, proj, proj, proj, proj, proj, k_ctx, v_ctx, band)


def _fourier_kernel(cs_ref, u_ref, cc_ref, sc_ref, o_ref, csb_ref, *, scale):
    @pl.when(pl.program_id(1) == 0)
    def _():
        csb_ref[...] = cs_ref[...].astype(BF16)

    z = u_ref[...].astype(BF16)
    zc = _dot(z, cc_ref[...].astype(BF16)).astype(BF16)
    zs = _dot(z, sc_ref[...].astype(BF16)).astype(BF16)
    zz = jnp.concatenate([zc, zs], axis=0)
    o_ref[...] = (_dot(csb_ref[...], zz) * scale).astype(o_ref.dtype)


@functools.lru_cache(maxsize=None)
def _dft_tables(t):
    idx = np.arange(t, dtype=np.int64)
    ang = 2.0 * np.pi * ((idx[:, None] * idx[None, :]) % t).astype(np.float64) / t
    cs = np.concatenate([np.cos(ang), -np.sin(ang)], axis=1).astype(np.float32)
    cw = FOURIER_WIDTH // FOURIER_GROUPS
    cidx = np.arange(cw, dtype=np.int64)
    cang = 2.0 * np.pi * ((cidx[:, None] * cidx[None, :]) % cw).astype(np.float64) / cw
    eye = np.eye(FOURIER_GROUPS)
    cc = np.kron(eye, np.cos(cang)).astype(np.float32)
    sc = np.kron(eye, np.sin(cang)).astype(np.float32)
    return cs, cc, sc


def _fourier(proj, row0, b, t, tm, name):
    cs, cc, sc = _dft_tables(t)
    cw = FOURIER_WIDTH // FOURIER_GROUPS
    nt = t // tm
    ub = COL_U // FOURIER_WIDTH
    base = row0 // t
    body = functools.partial(_fourier_kernel, scale=1.0 / math.sqrt(t * cw))
    return pl.pallas_call(
        body,
        grid=(nt, b),
        in_specs=[
            pl.BlockSpec((tm, 2 * t), lambda i, bb: (i, 0)),
            pl.BlockSpec((t, FOURIER_WIDTH), lambda i, bb: (base + bb, ub)),
            pl.BlockSpec((FOURIER_WIDTH, FOURIER_WIDTH), lambda i, bb: (0, 0)),
            pl.BlockSpec((FOURIER_WIDTH, FOURIER_WIDTH), lambda i, bb: (0, 0)),
        ],
        out_specs=pl.BlockSpec((tm, FOURIER_WIDTH), lambda i, bb: (bb * nt + i, 0)),
        out_shape=jax.ShapeDtypeStruct((b * t, FOURIER_WIDTH), BF16),
        scratch_shapes=[pltpu.VMEM((tm, 2 * t), BF16)],
        compiler_params=_cparams(("arbitrary", "arbitrary")),
        name=name,
    )(jnp.asarray(cs), proj, jnp.asarray(cc), jnp.asarray(sc))


HGRN_LEVELS = (64, 32, 16, 8, 4, 2)
HGRN_SAFE_RANGE = 80.0


@functools.lru_cache(maxsize=None)
def _hgrn_tables():
    c = HGRN_CHUNK
    return np.stack([np.tril(np.ones((c, c))), np.triu(np.ones((c, c)))]).astype(np.float32)


def _boundary_rows(b, m, reverse):
    c, w = b.shape
    half = m // 2
    off = half if reverse else half - 1
    if m >= 16:
        return jnp.concatenate(
            [jnp.broadcast_to(b[s + off:s + off + 1], (m, w)) for s in range(0, c, m)], axis=0)
    sub = lax.broadcasted_iota(I32, (c, w), 0) & 7
    b3 = b.reshape(c // 8, 8, w)

    def bcast(j):
        return jnp.broadcast_to(b3[:, j:j + 1, :], (c // 8, 8, w)).reshape(c, w)

    if m == 8:
        return bcast(off)
    if m == 4:
        return jnp.where(sub < 4, bcast(off), bcast(4 + off))
    assert m == 2
    if reverse:
        return jnp.where((sub & 1) == 1, b, pltpu.roll(b, c - 1, 0))
    return jnp.where((sub & 1) == 0, b, pltpu.roll(b, 1, 0))


def _hgrn_gates(q, z, v, loglb, log1mlb, onemlb, cum, reverse):
    c = HGRN_CHUNK
    log_sig = jnp.minimum(z, 0.0) - jnp.log1p(jnp.exp(-jnp.abs(z)))
    bb = log1mlb + log_sig
    mx = jnp.maximum(loglb, bb)
    lf = mx + jnp.log1p(jnp.exp(-jnp.abs(loglb - bb)))
    kk = onemlb * jax.nn.sigmoid(-z)
    b = _dot_exact_lhs(cum, lf)
    b_end = b[0:1] if reverse else b[c - 1:c]
    qt = (q * jnp.exp(b)).astype(BF16)
    kt = (kk * jnp.exp(b_end - b)).astype(BF16)
    return (q, kk, b), (qt, kt, v.astype(BF16))


def _head_stack(x):
    lane = lax.broadcasted_iota(I32, x.shape, 1)
    zero = jnp.zeros_like(x)
    return jnp.concatenate([jnp.where((lane >= h * HGRN_DK) & (lane < (h + 1) * HGRN_DK), x, zero)
                            for h in range(HGRN_HEADS)], axis=0)


class _HgrnDir:
    def __init__(self, f32_parts, bf16_parts, reverse):
        c = HGRN_CHUNK
        self.q, self.kk, self.b = f32_parts
        self.qt, self.kt, self.vb = bf16_parts
        self.reverse = reverse
        b_end = self.b[0:1] if reverse else self.b[c - 1:c]
        self.decay = jnp.exp(b_end)
        mid = c // 2 if reverse else c // 2 - 1
        self.rel = self.b - self.b[mid:mid + 1]
        self.span = jnp.max(jnp.abs(self.rel))

    def tree_decay_matrices(self):
        c = HGRN_CHUNK
        q, kk, b = self.q, self.kk, self.b
        row = lax.broadcasted_iota(I32, (c, 1), 0)
        ti = lax.broadcasted_iota(I32, (c, c), 0)
        si = lax.broadcasted_iota(I32, (c, c), 1)
        qb = q.astype(BF16)
        kb = kk.astype(BF16)
        heads = [slice(h * HGRN_DK, (h + 1) * HGRN_DK) for h in range(HGRN_HEADS)]
        acc = [jnp.where(ti == si, _dot_nt(qb[:, sl], kb[:, sl]), 0.0) for sl in heads]
        for m in HGRN_LEVELS:
            r = _boundary_rows(b, m, self.reverse)
            upper = (row & (m - 1)) >= (m // 2)
            q_side = jnp.logical_not(upper) if self.reverse else upper
            e = jnp.exp(jnp.where(q_side, b - r, r - b))
            qf = jnp.where(q_side, q * e, 0.0).astype(BF16)
            kf = jnp.where(q_side, 0.0, kk * e).astype(BF16)
            same_block = (ti & -m) == (si & -m)
            for h, sl in enumerate(heads):
                acc[h] = acc[h] + jnp.where(same_block, _dot_nt(qf[:, sl], kf[:, sl]), 0.0)
        return jnp.concatenate(acc, axis=1)

    def midpoint_decay_matrices(self):
        c = HGRN_CHUNK
        ti = lax.broadcasted_iota(I32, (c, HGRN_HEADS * c), 0)
        si = lax.broadcasted_iota(I32, (c, HGRN_HEADS * c), 1) & (c - 1)
        qm = (self.q * jnp.exp(self.rel)).astype(BF16)
        km = (self.kk * jnp.exp(-self.rel)).astype(BF16)
        causal = (si >= ti) if self.reverse else (si <= ti)
        return jnp.where(causal, _dot_nt(qm, _head_stack(km)), 0.0)

    def outputs(self, a_cat, st_ref, same_head, d):
        bd = st_ref[d]
        o = _dot_nt(self.qt, bd.astype(BF16)) + _dot(a_cat.astype(BF16), _head_stack(self.vb))
        st_ref[d] = jnp.where(same_head, bd * self.decay + _dot_tn(self.vb, self.kt), 0.0)
        return o


def _hgrn_kernel(hq_ref, ff_ref, fb_ref, hi_ref, hg_ref, lbp_ref, gn_ref, mall_ref, ones_ref, s0_ref,
                 rec_ref, sfin_ref, st_ref, of_ref, ob_ref, gf_ref, gb_ref, *, t):
    c = HGRN_CHUNK
    n = t // c
    blocks = [slice(h * HGRN_DK, (h + 1) * HGRN_DK) for h in range(HGRN_HEADS)]
    zero_blk = jnp.zeros((HGRN_DK, HGRN_DK), F32)

    def init_state():
        for d in range(2):
            st_ref[d] = jnp.concatenate(
                [jnp.concatenate([s0_ref[0, d, h] if j == h else zero_blk for j in range(HGRN_HEADS)], axis=1)
                 for h in range(HGRN_HEADS)], axis=0)

    def chunk_rows(ci):
        return pl.ds(pl.multiple_of(ci * c, c), c), pl.ds(pl.multiple_of((n - 1 - ci) * c, c), c)

    def gates_to(slot, ci):
        rf, rb = chunk_rows(ci)
        for d, (rows, f_ref) in enumerate(((rf, ff_ref), (rb, fb_ref))):
            f32_parts, bf16_parts = _hgrn_gates(
                hq_ref[rows, :], f_ref[rows, :], hi_ref[rows, :], lbp_ref[d, 0:1, :], lbp_ref[d, 1:2, :],
                lbp_ref[d, 2:3, :], mall_ref[d].astype(BF16), d == 1)
            for j in range(3):
                gf_ref[slot, d, j] = f32_parts[j]
                gb_ref[slot, d, j] = bf16_parts[j]

    def run(decay_matrices):
        init_state()
        gates_to(0, 0)

        def body(ci, widest):
            slot = ci & 1
            rf, rb = chunk_rows(ci)
            fwd = _HgrnDir([gf_ref[slot, 0, j] for j in range(3)], [gb_ref[slot, 0, j] for j in range(3)], False)
            bwd = _HgrnDir([gf_ref[slot, 1, j] for j in range(3)], [gb_ref[slot, 1, j] for j in range(3)], True)
            a_f, a_b = decay_matrices(fwd), decay_matrices(bwd)
            gates_to(1 - slot, jnp.minimum(ci + 1, n - 1))
            same_head = ones_ref[...] != 0.0
            of_ref[rf, :] = fwd.outputs(a_f, st_ref, same_head, 0)
            ob_ref[rb, :] = bwd.outputs(a_b, st_ref, same_head, 1)
            return jnp.maximum(widest, jnp.maximum(fwd.span, bwd.span))

        return lax.fori_loop(0, n, body, jnp.float32(0.0), unroll=2)

    widest = run(_HgrnDir.midpoint_decay_matrices)

    @pl.when(widest > HGRN_SAFE_RANGE)
    def _():
        run(_HgrnDir.tree_decay_matrices)

    for d in range(2):
        bd = st_ref[d]
        for h, sl in enumerate(blocks):
            sfin_ref[0, d, h] = bd[sl, sl]
    o = of_ref[...] + ob_ref[...]
    ms = _dot_exact_rhs(o * o, ones_ref[...].astype(BF16)) * (1.0 / HGRN_DK)
    o = o * lax.rsqrt(ms + GN_EPS) * gn_ref[...]
    rec_ref[...] = (o * _silu(hg_ref[...])).astype(rec_ref.dtype)


def _hgrn(proj, row0, b, t, lbp, gn_row, s0t, name):
    base = row0 // t
    m_all = jnp.asarray(_hgrn_tables())
    ones_bd = jnp.asarray(np.kron(np.eye(HGRN_HEADS), np.ones((HGRN_DK, HGRN_DK))).astype(np.float32))

    def col(cstart):
        return pl.BlockSpec((t, HGRN_WIDTH), lambda bb, cb=(cstart - MIX_IN_WIDTH) // HGRN_WIDTH: (base + bb, cb))

    const2 = lambda bb: (0, 0)
    const3 = lambda bb: (0, 0, 0)
    st_shape = (2, HGRN_HEADS, HGRN_DK, HGRN_DK)
    body = functools.partial(_hgrn_kernel, t=t)
    return pl.pallas_call(
        body,
        grid=(b,),
        in_specs=[
            col(COL_HQ), col(COL_FF), col(COL_FB), col(COL_HI), col(COL_HG),
            pl.BlockSpec((2, 3, HGRN_WIDTH), const3),
            pl.BlockSpec((1, HGRN_WIDTH), const2),
            pl.BlockSpec(m_all.shape, const3),
            pl.BlockSpec(ones_bd.shape, const2),
            pl.BlockSpec((1,) + st_shape, lambda bb: (bb, 0, 0, 0, 0)),
        ],
        out_specs=[
            pl.BlockSpec((t, HGRN_WIDTH), lambda bb: (bb, 0)),
            pl.BlockSpec((1,) + st_shape, lambda bb: (bb, 0, 0, 0, 0)),
        ],
        out_shape=[
            jax.ShapeDtypeStruct((b * t, HGRN_WIDTH), BF16),
            jax.ShapeDtypeStruct((b,) + st_shape, F32),
        ],
        scratch_shapes=[
            pltpu.VMEM((2, HGRN_WIDTH, HGRN_WIDTH), F32),
            pltpu.VMEM((t, HGRN_WIDTH), F32),
            pltpu.VMEM((t, HGRN_WIDTH), F32),
            pltpu.VMEM((2, 2, 3, HGRN_CHUNK, HGRN_WIDTH), F32),
            pltpu.VMEM((2, 2, 3, HGRN_CHUNK, HGRN_WIDTH), BF16),
        ],
        compiler_params=_cparams(("arbitrary",)),
        name=name,
    )(proj, proj, proj, proj, proj, lbp, gn_row, m_all, ones_bd, s0t)


def _outproj_kernel(*refs, n_ctx_tiles, n_mix):
    mix = refs[:3 * n_mix]
    refs = refs[3 * n_mix:]
    xs = refs[:-14]
    (mod_ref, w_ref, g_ref, b_ref, rw_ref, rb_ref, x1_ref, hp_ref, meta_ref, gate_ref, cnt_ref, wb_ref, tri_ref,
     run_ref) = refs[-14:]
    tm = x1_ref.shape[0]
    is_ctx = pl.program_id(0) < n_ctx_tiles
    x_in = jnp.where(is_ctx, xs[0][...], xs[1][...]) if len(xs) == 2 else xs[0][...]
    if n_mix == 2:
        attn, four, rec = [jnp.where(is_ctx, mix[2 * j][...], mix[2 * j + 1][...]) for j in range(3)]
    else:
        attn, four, rec = [r[...] for r in mix]

    @pl.when(pl.program_id(0) == 0)
    def _():
        wb_ref[...] = w_ref[0].astype(BF16)
        r = lax.broadcasted_iota(I32, (tm, tm), 0)
        c = lax.broadcasted_iota(I32, (tm, tm), 1)
        tri_ref[...] = jnp.where(r < c, 1.0, 0.0).astype(BF16)
        run_ref[...] = jnp.zeros_like(run_ref)

    out = _dot(attn, wb_ref[0:ATTN_WIDTH, :])
    out = out + _dot(four, wb_ref[ATTN_WIDTH:ATTN_WIDTH + FOURIER_WIDTH, :])
    out = out + _dot(rec, wb_ref[ATTN_WIDTH + FOURIER_WIDTH:, :])
    gate1 = mod_ref[0, 2:3, :]
    y = DEEPNORM_ALPHA * x_in + gate1 * out
    x1 = _ln_plain(y, LN_EPS) * g_ref[...] + b_ref[...]
    x1_ref[...] = x1
    h2 = _ln_plain(x1, ADA_EPS) * (1.0 + mod_ref[0, 4:5, :]) + mod_ref[0, 3:4, :]
    hp_ref[...] = _pack_bf16_pair(h2[:, :HALF_D], h2[:, HALF_D:])

    h_hi = h2.astype(BF16)
    h_lo = (h2 - h_hi.astype(F32)).astype(BF16)
    rwt = rw_ref[...]
    w_hi = rwt.astype(BF16)
    w_lo = (rwt - w_hi.astype(F32)).astype(BF16)
    scores = jax.nn.sigmoid(_dot_nt(w_hi, h_hi) + _dot_nt(w_hi, h_lo) + _dot_nt(w_lo, h_hi))
    remaining = scores + rb_ref[...]
    eidx = lax.broadcasted_iota(I32, scores.shape, 0).astype(F32)
    chosen = jnp.zeros(scores.shape, jnp.bool_)
    picks = []
    for _ in range(TOP_K):
        mx = jnp.max(remaining, axis=0, keepdims=True)
        first = jnp.min(jnp.where(remaining == mx, eidx, float(N_EXPERTS)), axis=0, keepdims=True)
        pick = eidx == first
        picks.append((pick, first))
        chosen = jnp.logical_or(chosen, pick)
        remaining = jnp.where(pick, -jnp.inf, remaining)
    sel = jnp.where(chosen, scores, 0.0)
    gates = sel / jnp.sum(sel, axis=0, keepdims=True) * ROUTED_SCALE

    onehot = jnp.where(chosen, 1.0, 0.0)
    rank = run_ref[...] + _dot(onehot.astype(BF16), tri_ref[...])
    run_ref[...] += jnp.sum(onehot, axis=1, keepdims=True)
    cnt_ref[...] = run_ref[...]

    ids, rks, gks = [], [], []
    for pick, first in picks:
        ids.append(first.astype(I32))
        rks.append(jnp.sum(jnp.where(pick, rank, 0.0), axis=0, keepdims=True).astype(I32))
        gks.append(jnp.sum(jnp.where(pick, gates, 0.0), axis=0, keepdims=True))
    meta_ref[...] = jnp.concatenate(ids + rks, axis=0)
    gate_ref[...] = jnp.concatenate(gks, axis=0)


def _outproj(attn, four, rec, x, mod_l, w_out, layer, g1, b1, rw, rb, lay, tm):
    n_tiles = lay.n // tm
    n_ctx_tiles = lay.n_ctx // tm
    row = lambda i: (i, 0)
    const = lambda i: (0, 0)
    xs = x if isinstance(x, tuple) else (x,)
    return pl.pallas_call(
        functools.partial(_outproj_kernel, n_ctx_tiles=n_ctx_tiles, n_mix=len(attn)),
        grid=(n_tiles,),
        in_specs=[
            *_group_specs(len(attn), tm, ATTN_WIDTH, n_ctx_tiles),
            *_group_specs(len(four), tm, FOURIER_WIDTH, n_ctx_tiles),
            *_group_specs(len(rec), tm, HGRN_WIDTH, n_ctx_tiles),
            *_group_specs(len(xs), tm, D_MODEL, n_ctx_tiles),
            pl.BlockSpec((1, N_MOD, D_MODEL), lambda i: (lay.cond_row(i, tm), 0, 0)),
            pl.BlockSpec((1, D_MODEL, D_MODEL), lambda i: (layer, 0, 0)),
            pl.BlockSpec((1, D_MODEL), const),
            pl.BlockSpec((1, D_MODEL), const),
            pl.BlockSpec((N_EXPERTS, D_MODEL), const),
            pl.BlockSpec((N_EXPERTS, 1), const),
        ],
        out_specs=[
            pl.BlockSpec((tm, D_MODEL), row),
            pl.BlockSpec((tm, HALF_D), row),
            pl.BlockSpec((2 * TOP_K, tm), lambda i: (0, i)),
            pl.BlockSpec((TOP_K, tm), lambda i: (0, i)),
            pl.BlockSpec((N_EXPERTS, 1), const),
        ],
        out_shape=[
            jax.ShapeDtypeStruct((lay.n, D_MODEL), F32),
            jax.ShapeDtypeStruct((lay.n, HALF_D), I32),
            jax.ShapeDtypeStruct((2 * TOP_K, lay.n), I32),
            jax.ShapeDtypeStruct((TOP_K, lay.n), F32),
            jax.ShapeDtypeStruct((N_EXPERTS, 1), F32),
        ],
        scratch_shapes=[
            pltpu.VMEM((D_MODEL, D_MODEL), BF16),
            pltpu.VMEM((tm, tm), BF16),
            pltpu.VMEM((N_EXPERTS, 1), F32),
        ],
        compiler_params=_cparams(("arbitrary",)),
        name="outproj_router" + lay.tag,
    )(*attn, *four, *rec, *xs, mod_l, w_out, g1, b1, rw, rb)


def _sc_workers():
    info = plsc.get_sparse_core_info()
    return info.num_cores, info.num_cores * info.num_subcores


def _sc_scatter_rows(rows, pos_b, r_out, tag=""):
    nc, nw = _sc_workers()
    n, w = rows.shape
    nbt, copies, _ = pos_b.shape
    assert nbt * SC_BATCH == n and nbt % (2 * nw) == 0
    per_w = nbt // nw
    mesh = plsc.VectorSubcoreMesh(core_axis_name="c", subcore_axis_name="s")

    @functools.partial(
        pl.kernel, mesh=mesh, out_type=jax.ShapeDtypeStruct((r_out, w), rows.dtype),
        scratch_types=[pltpu.VMEM((copies, SC_BATCH), I32), pltpu.VMEM((copies, SC_BATCH), I32),
                       pltpu.VMEM((SC_BATCH, w), rows.dtype), pltpu.VMEM((SC_BATCH, w), rows.dtype),
                       pltpu.SemaphoreType.DMA, pltpu.SemaphoreType.DMA,
                       pltpu.SemaphoreType.DMA, pltpu.SemaphoreType.DMA],
        name="sc_dispatch" + tag)
    def k(rows_hbm, pos_hbm, out_hbm, idx_a, idx_b, rows_a, rows_b, sem_ra, sem_rb, sem_sa, sem_sb):
        wid = lax.axis_index("s") * nc + lax.axis_index("c")
        first = wid * per_w

        def reads(j, idx_v, rows_v, sem):
            bt = first + j
            return (pltpu.make_async_copy(pos_hbm.at[bt], idx_v, sem),
                    pltpu.make_async_copy(rows_hbm.at[pl.ds(bt * SC_BATCH, SC_BATCH)], rows_v, sem))

        def scatters(idx_v, rows_v, sem):
            return [pltpu.make_async_copy(rows_v, out_hbm.at[idx_v.at[q]], sem) for q in range(copies)]

        def start(descs):
            for d in descs:
                d.start()

        def wait(descs):
            for d in descs:
                d.wait()

        start(reads(0, idx_a, rows_a, sem_ra))

        @pl.loop(0, per_w // 2)
        def _(p):
            j0 = 2 * p
            j1 = j0 + 1

            @pl.when(p > 0)
            def _():
                wait(scatters(idx_b, rows_b, sem_sb))

            start(reads(j1, idx_b, rows_b, sem_rb))
            wait(reads(j0, idx_a, rows_a, sem_ra))
            start(scatters(idx_a, rows_a, sem_sa))
            wait(reads(j1, idx_b, rows_b, sem_rb))
            start(scatters(idx_b, rows_b, sem_sb))
            wait(scatters(idx_a, rows_a, sem_sa))

            @pl.when(p + 1 < per_w // 2)
            def _():
                start(reads(j0 + 2, idx_a, rows_a, sem_ra))

        wait(scatters(idx_b, rows_b, sem_sb))

    return k(rows, pos_b)


def _sc_gather_rows(table, idx, tag=""):
    nc, nw = _sc_workers()
    r = idx.shape[0]
    w = table.shape[1]
    assert r % (2 * nw * SC_BATCH) == 0
    per_w = r // nw
    nb = per_w // SC_BATCH
    mesh = plsc.VectorSubcoreMesh(core_axis_name="c", subcore_axis_name="s")

    @functools.partial(
        pl.kernel, mesh=mesh, out_type=jax.ShapeDtypeStruct((r, w), table.dtype),
        scratch_types=[pltpu.VMEM((per_w,), I32),
                       pltpu.VMEM((SC_BATCH, w), table.dtype), pltpu.VMEM((SC_BATCH, w), table.dtype),
                       pltpu.SemaphoreType.DMA, pltpu.SemaphoreType.DMA,
                       pltpu.SemaphoreType.DMA, pltpu.SemaphoreType.DMA],
        name="sc_combine" + tag)
    def k(table_hbm, idx_hbm, out_hbm, idx_v, rows_a, rows_b, sem_ga, sem_gb, sem_wa, sem_wb):
        wid = lax.axis_index("s") * nc + lax.axis_index("c")
        base = wid * per_w
        pltpu.sync_copy(idx_hbm.at[pl.ds(base, per_w)], idx_v)

        def gather(j, rows_v, sem):
            return pltpu.make_async_copy(table_hbm.at[idx_v.at[pl.ds(j * SC_BATCH, SC_BATCH)]], rows_v, sem)

        def write(j, rows_v, sem):
            return pltpu.make_async_copy(rows_v, out_hbm.at[pl.ds(base + j * SC_BATCH, SC_BATCH)], sem)

        gather(0, rows_a, sem_ga).start()

        @pl.loop(0, nb // 2)
        def _(p):
            j0 = 2 * p
            j1 = j0 + 1

            @pl.when(p > 0)
            def _():
                write(j1 - 2, rows_b, sem_wb).wait()

            gather(j1, rows_b, sem_gb).start()
            gather(j0, rows_a, sem_ga).wait()
            write(j0, rows_a, sem_wa).start()
            gather(j1, rows_b, sem_gb).wait()
            write(j1, rows_b, sem_wb).start()
            write(j0, rows_a, sem_wa).wait()

            @pl.when(p + 1 < nb // 2)
            def _():
                gather(j0 + 2, rows_a, sem_ga).start()

        write(nb - 1, rows_b, sem_wb).wait()

    return k(table, idx)


def _experts_kernel(te_ref, na_ref, x_ref, w1_ref, w3_ref, w2_ref, o_ref, w1b_ref, w3b_ref, w2b_ref):
    del te_ref

    @pl.when(pl.program_id(0) < na_ref[0])
    def _():
        w1b_ref[...] = w1_ref[0, 0].astype(BF16)
        w3b_ref[...] = w3_ref[0, 0].astype(BF16)
        w2b_ref[...] = w2_ref[0, 0].astype(BF16)
        lo, hi = _unpack_bf16_pair(x_ref[...])
        lo = lo.astype(BF16)
        hi = hi.astype(BF16)
        a = _dot(lo, w1b_ref[0:HALF_D, :]) + _dot(hi, w1b_ref[HALF_D:, :])
        b = _dot(lo, w3b_ref[0:HALF_D, :]) + _dot(hi, w3b_ref[HALF_D:, :])
        y = _dot((_silu(a) * b).astype(BF16), w2b_ref[...])
        o_ref[...] = _pack_bf16_pair(y[:, :HALF_D], y[:, HALF_D:])


def _experts(xs, tile_expert, n_active, w1, w3, w2, layer, tm, tag):
    r = xs.shape[0]
    n_tiles = r // tm

    def xmap(j, te, na):
        return (jnp.minimum(j, na[0] - 1), 0)

    def wmap(j, te, na):
        return (layer, te[jnp.minimum(j, na[0] - 1)], 0, 0)

    grid_spec = pltpu.PrefetchScalarGridSpec(
        num_scalar_prefetch=2,
        grid=(n_tiles,),
        in_specs=[
            pl.BlockSpec((tm, HALF_D), xmap),
            pl.BlockSpec((1, 1, D_MODEL, EXPERT_FF), wmap),
            pl.BlockSpec((1, 1, D_MODEL, EXPERT_FF), wmap),
            pl.BlockSpec((1, 1, EXPERT_FF, D_MODEL), wmap),
        ],
        out_specs=pl.BlockSpec((tm, HALF_D), xmap),
        scratch_shapes=[
            pltpu.VMEM((D_MODEL, EXPERT_FF), BF16),
            pltpu.VMEM((D_MODEL, EXPERT_FF), BF16),
            pltpu.VMEM((EXPERT_FF, D_MODEL), BF16),
        ],
    )
    return pl.pallas_call(
        _experts_kernel,
        grid_spec=grid_spec,
        out_shape=jax.ShapeDtypeStruct((r, HALF_D), I32),
        compiler_params=_cparams(("arbitrary",)),
        name="experts" + tag,
    )(tile_expert, n_active, xs, w1, w3, w2)


def _combine_kernel(yp_ref, gate_ref, hp_ref, sw1_ref, sw3_ref, sw2_ref, x_ref, mod_ref, g_ref, b_ref, *refs,
                    n_ctx_tiles):
    outs = refs[:-3]
    w1b_ref, w3b_ref, w2b_ref = refs[-3:]

    @pl.when(pl.program_id(0) == 0)
    def _():
        w1b_ref[...] = sw1_ref[...].astype(BF16)
        w3b_ref[...] = sw3_ref[...].astype(BF16)
        w2b_ref[...] = sw2_ref[...].astype(BF16)

    lo, hi = _unpack_bf16_pair(hp_ref[...])
    lo = lo.astype(BF16)
    hi = hi.astype(BF16)
    a = _dot(lo, w1b_ref[0:HALF_D, :]) + _dot(hi, w1b_ref[HALF_D:, :])
    b = _dot(lo, w3b_ref[0:HALF_D, :]) + _dot(hi, w3b_ref[HALF_D:, :])
    shared = _dot((_silu(a) * b).astype(BF16), w2b_ref[...])
    acc_lo = shared[:, :HALF_D]
    acc_hi = shared[:, HALF_D:]
    gates = gate_ref[...].T
    for k in range(TOP_K):
        ylo, yhi = _unpack_bf16_pair(yp_ref[k])
        gk = gates[:, k:k + 1]
        acc_lo = acc_lo + gk * ylo
        acc_hi = acc_hi + gk * yhi
    moe = jnp.concatenate([acc_lo, acc_hi], axis=1)
    y = DEEPNORM_ALPHA * x_ref[...] + mod_ref[0, 5:6, :] * moe
    res = _ln_plain(y, LN_EPS) * g_ref[...] + b_ref[...]
    if len(outs) == 1:
        outs[0][...] = res
    else:
        @pl.when(pl.program_id(0) < n_ctx_tiles)
        def _():
            outs[0][...] = res

        @pl.when(pl.program_id(0) >= n_ctx_tiles)
        def _():
            outs[1][...] = res


def _combine(yp, gate8, hp, sw1, sw3, sw2, x1, mod_l, g2, b2, lay, tm, split_out):
    n_tiles = lay.n // tm
    n_ctx_tiles = lay.n_ctx // tm
    row = lambda i: (i, 0)
    const = lambda i: (0, 0)
    if split_out:
        out_specs = _group_specs(2, tm, D_MODEL, n_ctx_tiles)
        out_shape = [jax.ShapeDtypeStruct((lay.n_ctx, D_MODEL), F32), jax.ShapeDtypeStruct((lay.n_lat, D_MODEL), F32)]
    else:
        out_specs = pl.BlockSpec((tm, D_MODEL), row)
        out_shape = jax.ShapeDtypeStruct((lay.n, D_MODEL), F32)
    return pl.pallas_call(
        functools.partial(_combine_kernel, n_ctx_tiles=n_ctx_tiles),
        grid=(n_tiles,),
        in_specs=[
            pl.BlockSpec((TOP_K, tm, HALF_D), lambda i: (0, i, 0)),
            pl.BlockSpec((TOP_K, tm), lambda i: (0, i)),
            pl.BlockSpec((tm, HALF_D), row),
            pl.BlockSpec((D_MODEL, EXPERT_FF), const),
            pl.BlockSpec((D_MODEL, EXPERT_FF), const),
            pl.BlockSpec((EXPERT_FF, D_MODEL), const),
            pl.BlockSpec((tm, D_MODEL), row),
            pl.BlockSpec((1, N_MOD, D_MODEL), lambda i: (lay.cond_row(i, tm), 0, 0)),
            pl.BlockSpec((1, D_MODEL), const),
            pl.BlockSpec((1, D_MODEL), const),
        ],
        out_specs=out_specs,
        out_shape=out_shape,
        scratch_shapes=[
            pltpu.VMEM((D_MODEL, EXPERT_FF), BF16),
            pltpu.VMEM((D_MODEL, EXPERT_FF), BF16),
            pltpu.VMEM((EXPERT_FF, D_MODEL), BF16),
        ],
        compiler_params=_cparams(("arbitrary",)),
        name="combine_norm" + lay.tag,
    )(yp, gate8, hp, sw1, sw3, sw2, x1, mod_l, g2, b2)


def _moe_dispatch(hp, meta, counts, lay, tile):
    n = lay.n
    r_max = n * TOP_K + N_EXPERTS * tile
    n_tiles = r_max // tile
    cnt = counts.reshape(N_EXPERTS).astype(I32)
    padded = ((cnt + tile - 1) // tile) * tile
    ends = jnp.cumsum(padded)
    offsets = ends - padded
    idx8 = meta[:TOP_K]
    base8 = jnp.sum(jnp.where(idx8[:, :, None] == jnp.arange(N_EXPERTS, dtype=I32), offsets, 0), axis=-1)
    pos = (base8 + meta[TOP_K:]).astype(I32)
    tile_start = jnp.arange(n_tiles, dtype=I32) * tile
    tile_expert = jnp.minimum(jnp.sum(tile_start[:, None] >= ends[None, :], axis=1), N_EXPERTS - 1).astype(I32)
    n_active = (ends[-1] // tile).astype(I32).reshape(1)
    pos_b = pos.reshape(TOP_K, n // SC_BATCH, SC_BATCH).transpose(1, 0, 2)
    xs = _sc_scatter_rows(hp, pos_b, r_max, lay.tag)
    return xs, tile_expert, n_active, pos


def _moe_combine(ys, pos, gate8, hp, sw1, sw3, sw2, x1, mod_l, g2, b2, lay, split_out):
    n = lay.n
    yp = _sc_gather_rows(ys, pos.reshape(n * TOP_K), lay.tag).reshape(TOP_K, n, HALF_D)
    return _combine(yp, gate8, hp, sw1, sw3, sw2, x1, mod_l, g2, b2, lay, TOKEN_TILE, split_out)


def kernel(x_prompt, x_sample, cache_k, cache_v, state_hgrn, c, c_ctx, w_ada, b_ada, w_in, w_out, attn_sink, hgrn_lb, hgrn_norm, ln1_g, ln1_b, ln2_g, ln2_b, router_w, router_b, moe_w1, moe_w3, moe_w2, shared_w1, shared_w3, shared_w2):
    b_ctx, t_ctx, _ = x_prompt.shape
    b_lat, t_lat, _ = x_sample.shape
    past = cache_k.shape[2]
    tm = TOKEN_TILE
    assert 1 + b_lat <= COND_ROWS
    lay = _Layout(b_ctx, t_ctx, b_lat, t_lat)
    assert lay.n_ctx % tm == 0 and t_lat % tm == 0 and lay.n_ctx % t_lat == 0

    cond = jnp.concatenate([c_ctx[None, :], c, jnp.zeros((COND_ROWS - 1 - b_lat, D_MODEL), F32)], axis=0)
    mod = _adaln(cond, w_ada, b_ada).reshape(DEPTH, COND_ROWS, N_MOD, D_MODEL)

    lb_all = jnp.cumsum(jax.nn.softmax(hgrn_lb.astype(F32), axis=0), axis=0)
    lb_all = lb_all - lb_all[:1]
    lbp = jnp.stack([jnp.log(lb_all), jnp.log1p(-lb_all), 1.0 - lb_all], axis=2)

    cos_t, sin_t = _rope_tables(lay, tm)
    zero_state = jnp.zeros((b_ctx, 2, HGRN_HEADS, HGRN_DK, HGRN_DK), F32)

    def layer(l, lay, x, split_out):
        lat = slice(lay.lat_first, lay.lat_first + lay.b_lat)
        outs = _inproj(x, mod[l], w_in, l, cos_t, sin_t, lay, tm)
        proj, proj_h = outs[0], outs[1]
        sink_l = attn_sink[l].reshape(1, N_HEADS)
        gn_row = jnp.tile(hgrn_norm[l], HGRN_HEADS).reshape(1, HGRN_WIDTH)
        attn, four, rec, extras = [], [], [], None
        if lay.b_ctx:
            attn.append(_attn_context(proj, sink_l, lay))
            four.append(_fourier(proj, 0, lay.b_ctx, t_ctx, t_ctx, "fourier_ctx"))
            rec_c, s_fin = _hgrn(proj_h, 0, lay.b_ctx, t_ctx, lbp[l], gn_row, zero_state, "hgrn_ctx")
            rec.append(rec_c)
            extras = (outs[2], outs[3], s_fin)
        attn.append(_attn_latent(proj, cache_k[lat, l].reshape(lay.b_lat, past, KV_WIDTH),
                                 cache_v[lat, l].reshape(lay.b_lat, past, KV_WIDTH), sink_l, lay))
        four.append(_fourier(proj, lay.n_ctx, lay.b_lat, t_lat, min(t_lat, 512), "fourier_lat" + lay.tag))
        s0t = jnp.swapaxes(state_hgrn[lat, l].astype(F32), -1, -2)
        rec.append(_hgrn(proj_h, lay.n_ctx, lay.b_lat, t_lat, lbp[l], gn_row, s0t, "hgrn_lat" + lay.tag)[0])
        x1, hp, meta, gate8, counts = _outproj(
            tuple(attn), tuple(four), tuple(rec), x, mod[l], w_out, l, ln1_g[l].reshape(1, -1),
            ln1_b[l].reshape(1, -1), router_w[l].T, router_b[l].reshape(-1, 1), lay, tm)
        xs, tile_expert, n_active, pos = _moe_dispatch(hp, meta, counts, lay, EXPERT_TILE)
        ys = _experts(xs, tile_expert, n_active, moe_w1, moe_w3, moe_w2, l, EXPERT_TILE, lay.tag)
        x = _moe_combine(ys, pos, gate8, hp, shared_w1[l], shared_w3[l], shared_w2[l], x1, mod[l],
                         ln2_g[l].reshape(1, -1), ln2_b[l].reshape(1, -1), lay, split_out)
        return x, extras

    x = (x_prompt.reshape(lay.n_ctx, D_MODEL), x_sample.reshape(lay.n_lat, D_MODEL))
    ks_out, vs_out, ss_out = [], [], []
    for l in range(DEPTH):
        x, (k_new, v_new, s_fin) = layer(l, lay, x, split_out=(l == DEPTH - 1))
        ks_out.append(k_new.reshape(b_ctx, t_ctx, N_KV_HEADS, HEAD_DIM))
        vs_out.append(v_new.reshape(b_ctx, t_ctx, N_KV_HEADS, HEAD_DIM))
        ss_out.append(jnp.swapaxes(s_fin, -1, -2))

    y_prompt = x[0].reshape(b_ctx, t_ctx, D_MODEL)
    y_sample = x[1].reshape(b_lat, t_lat, D_MODEL)
    new_cache_k = jnp.stack(ks_out, axis=1)
    new_cache_v = jnp.stack(vs_out, axis=1)
    new_state = jnp.stack(ss_out, axis=1).astype(x_prompt.dtype)
    return (y_prompt, y_sample, new_cache_k, new_cache_v, new_state)
```

```python
import functools
import math

import numpy as np
import jax
import jax.numpy as jnp
from jax import lax
from jax.experimental import pallas as pl
from jax.experimental.pallas import tpu as pltpu
from jax.experimental.pallas import tpu_sc as plsc

F32 = jnp.float32
BF16 = jnp.bfloat16
I32 = jnp.int32

D_MODEL = 1024
HALF_D = D_MODEL // 2
DEPTH = 2
GRID_W = 64
ROPE_BASE = 10000.0
HEAD_DIM = 64
ATTN_WIDTH = 512
N_HEADS = 8
N_KV_HEADS = 2
KV_GROUP = 4
KV_WIDTH = N_KV_HEADS * HEAD_DIM
WINDOW = 128
ATTN_BLOCK = 128
FOURIER_WIDTH = 256
FOURIER_GROUPS = 4
HGRN_WIDTH = 256
HGRN_HEADS = 4
HGRN_DK = 64
HGRN_CHUNK = 64
IN_WIDTH = 2304
N_EXPERTS = 64
TOP_K = 8
EXPERT_FF = 256
ROUTED_SCALE = 2.5
N_MOD = 6
LN_EPS = 1e-5
ADA_EPS = 1e-6
GN_EPS = 1e-6
DEEPNORM_ALPHA = (2 * DEPTH) ** 0.25

COL_Q = 0
COL_K = 512
COL_V = 640
COL_U = 768
COL_HQ = 1024
COL_FF = 1280
COL_FB = 1536
COL_HI = 1792
COL_HG = 2048
ROPE_COLS = COL_V
MIX_IN_WIDTH = COL_HQ

V7X_LANES = 128
COND_ROWS = 16
NEG_BIG = -1e30
TOKEN_TILE = 512
EXPERT_TILE = 1024
SC_BATCH = 64

VMEM_LIMIT = 56 * 1024 * 1024


def _cparams(sem):
    return pltpu.CompilerParams(dimension_semantics=sem, vmem_limit_bytes=VMEM_LIMIT)


def _dot(a, b):
    return jnp.dot(a, b, preferred_element_type=F32)


def _dot_nt(a, b):
    return lax.dot_general(a, b, (((1,), (1,)), ((), ())), preferred_element_type=F32)


def _dot_tn(a, b):
    return lax.dot_general(a, b, (((0,), (0,)), ((), ())), preferred_element_type=F32)


def _split3(x):
    hi = x.astype(BF16)
    r1 = x - hi.astype(F32)
    mid = r1.astype(BF16)
    lo = (r1 - mid.astype(F32)).astype(BF16)
    return hi, mid, lo


def _dot_exact_lhs(m_bf16, x):
    hi, mid, lo = _split3(x)
    return _dot(m_bf16, hi) + _dot(m_bf16, mid) + _dot(m_bf16, lo)


def _dot_exact_rhs(x, m_bf16):
    hi, mid, lo = _split3(x)
    return _dot(hi, m_bf16) + _dot(mid, m_bf16) + _dot(lo, m_bf16)


def _dot_hp(a, b):
    a_hi = a.astype(BF16)
    a_lo = (a - a_hi.astype(F32)).astype(BF16)
    b_hi = b.astype(BF16)
    b_lo = (b - b_hi.astype(F32)).astype(BF16)
    return _dot(a_hi, b_hi) + _dot(a_hi, b_lo) + _dot(a_lo, b_hi)


def _pack_bf16_pair(lo, hi):
    return lax.bitcast_convert_type(pltpu.pack_elementwise([lo, hi], packed_dtype=BF16), I32)


def _unpack_bf16_pair(w):
    u = lax.bitcast_convert_type(w, jnp.uint32)
    lo = pltpu.unpack_elementwise(u, index=0, packed_dtype=BF16, unpacked_dtype=F32)
    hi = pltpu.unpack_elementwise(u, index=1, packed_dtype=BF16, unpacked_dtype=F32)
    return lo, hi


def _ln_plain(x, eps):
    mu = jnp.mean(x, axis=-1, keepdims=True)
    xc = x - mu
    var = jnp.mean(xc * xc, axis=-1, keepdims=True)
    return xc * lax.rsqrt(var + eps)


def _silu(x):
    return x * jax.nn.sigmoid(x)


def _adaln_kernel(c_ref, w_ref, b_ref, o_ref):
    s = _silu(c_ref[...])
    o_ref[0] = _dot_hp(s, w_ref[0]) + b_ref[0]


def _adaln(cond, w_ada, b_ada):
    return pl.pallas_call(
        _adaln_kernel,
        grid=(DEPTH, N_MOD),
        in_specs=[
            pl.BlockSpec((COND_ROWS, D_MODEL), lambda l, j: (0, 0)),
            pl.BlockSpec((1, D_MODEL, D_MODEL), lambda l, j: (l, 0, j)),
            pl.BlockSpec((1, 1, D_MODEL), lambda l, j: (l, 0, j)),
        ],
        out_specs=pl.BlockSpec((1, COND_ROWS, D_MODEL), lambda l, j: (l, 0, j)),
        out_shape=jax.ShapeDtypeStruct((DEPTH, COND_ROWS, N_MOD * D_MODEL), F32),
        compiler_params=_cparams(("arbitrary", "arbitrary")),
        name="adaln",
    )(cond, w_ada, b_ada.reshape(DEPTH, 1, N_MOD * D_MODEL))


class _Layout:
    def __init__(self, b_ctx, t_ctx, b_lat, t_lat, lat_first=0, tag=""):
        self.b_ctx, self.t_ctx, self.b_lat, self.t_lat = b_ctx, t_ctx, b_lat, t_lat
        self.lat_first = lat_first
        self.n_ctx = b_ctx * t_ctx
        self.n_lat = b_lat * t_lat
        self.n = self.n_ctx + self.n_lat
        self.tag = tag

    def cond_row(self, tile, tm):
        n_ctx_tiles = self.n_ctx // tm
        per_batch = self.t_lat // tm
        return jnp.where(tile < n_ctx_tiles, 0, 1 + self.lat_first + (tile - n_ctx_tiles) // per_batch)


def _group_specs(n_arrays, tm, width, n_ctx_tiles):
    if n_arrays == 1:
        return [pl.BlockSpec((tm, width), lambda i: (i, 0))]
    return [pl.BlockSpec((tm, width), lambda i: (jnp.minimum(i, n_ctx_tiles - 1), 0)),
            pl.BlockSpec((tm, width), lambda i: (jnp.maximum(i - n_ctx_tiles, 0), 0))]


def _inproj_kernel(*refs, n_ctx_tiles):
    n_tail = 9 if n_ctx_tiles > 0 else 7
    xs = refs[:-n_tail]
    mod_ref, w_ref, cos_ref, sin_ref, oa_ref, oh_ref = refs[-n_tail:-n_tail + 6]
    wb_ref = refs[-1]

    @pl.when(pl.program_id(0) == 0)
    def _():
        wb_ref[...] = w_ref[0].astype(BF16)

    if len(xs) == 2:
        x = jnp.where(pl.program_id(0) < n_ctx_tiles, xs[0][...], xs[1][...])
    else:
        x = xs[0][...]
    shift = mod_ref[0, 0:1, :]
    scale = mod_ref[0, 1:2, :]
    h = (_ln_plain(x, ADA_EPS) * (1.0 + scale) + shift).astype(BF16)
    p = _dot(h, wb_ref[...])
    cos = cos_ref[...]
    sin = sin_ref[...]
    lane = lax.broadcasted_iota(I32, cos.shape, 1)
    first_half = (lane & 31) < 16
    for cb in range(ROPE_COLS // V7X_LANES):
        seg = p[:, cb * V7X_LANES:(cb + 1) * V7X_LANES]
        partner = jnp.where(first_half, pltpu.roll(seg, V7X_LANES - 16, 1), pltpu.roll(seg, 16, 1))
        oa_ref[:, cb * V7X_LANES:(cb + 1) * V7X_LANES] = (seg * cos + partner * sin).astype(BF16)
    oa_ref[:, ROPE_COLS:] = p[:, ROPE_COLS:MIX_IN_WIDTH].astype(BF16)
    oh_ref[...] = p[:, MIX_IN_WIDTH:]

    if n_ctx_tiles > 0:
        kc_ref, vc_ref = refs[-3], refs[-2]

        @pl.when(pl.program_id(0) < n_ctx_tiles)
        def _():
            kc_ref[...] = p[:, COL_K:COL_K + KV_WIDTH]
            vc_ref[...] = p[:, COL_V:COL_V + KV_WIDTH]


def _rope_tables(lay, tm):
    t = lay.t_lat
    pos = jnp.arange(t)
    row = (pos // GRID_W).astype(F32)
    col = (pos % GRID_W).astype(F32)
    n_freq = HEAD_DIM // 4
    inv = ROPE_BASE ** (-jnp.arange(n_freq, dtype=F32) / n_freq)
    ang_r = row[:, None] * inv
    ang_c = col[:, None] * inv
    ang = jnp.concatenate([ang_r, ang_r, ang_c, ang_c], axis=1)
    sign = jnp.concatenate([-jnp.ones(n_freq), jnp.ones(n_freq), -jnp.ones(n_freq), jnp.ones(n_freq)]).astype(F32)
    cos = jnp.cos(ang)
    sin = jnp.sin(ang) * sign
    cos = jnp.concatenate([jnp.ones((tm, HEAD_DIM), F32), cos], axis=0)
    sin = jnp.concatenate([jnp.zeros((tm, HEAD_DIM), F32), sin], axis=0)
    return jnp.tile(cos, (1, 2)), jnp.tile(sin, (1, 2))


def _inproj(x, mod_l, w_in, layer, cos_t, sin_t, lay, tm):
    n_tiles = lay.n // tm
    n_ctx_tiles = lay.n_ctx // tm
    per_batch = lay.t_lat // tm

    def tbl(i):
        return jnp.where(i < n_ctx_tiles, 0, 1 + (i - n_ctx_tiles) % per_batch)

    xs = x if isinstance(x, tuple) else (x,)
    kv_specs, kv_shapes = [], []
    if n_ctx_tiles > 0:
        kv_specs = [pl.BlockSpec((tm, KV_WIDTH), lambda i: (jnp.minimum(i, n_ctx_tiles - 1), 0))] * 2
        kv_shapes = [jax.ShapeDtypeStruct((lay.n_ctx, KV_WIDTH), F32)] * 2
    return pl.pallas_call(
        functools.partial(_inproj_kernel, n_ctx_tiles=n_ctx_tiles),
        grid=(n_tiles,),
        in_specs=[
            *_group_specs(len(xs), tm, D_MODEL, n_ctx_tiles),
            pl.BlockSpec((1, N_MOD, D_MODEL), lambda i: (lay.cond_row(i, tm), 0, 0)),
            pl.BlockSpec((1, D_MODEL, IN_WIDTH), lambda i: (layer, 0, 0), pipeline_mode=pl.Buffered(1)),
            pl.BlockSpec((tm, V7X_LANES), lambda i: (tbl(i), 0)),
            pl.BlockSpec((tm, V7X_LANES), lambda i: (tbl(i), 0)),
        ],
        out_specs=[pl.BlockSpec((tm, MIX_IN_WIDTH), lambda i: (i, 0)),
                   pl.BlockSpec((tm, IN_WIDTH - MIX_IN_WIDTH), lambda i: (i, 0))] + kv_specs,
        out_shape=[jax.ShapeDtypeStruct((lay.n, MIX_IN_WIDTH), BF16),
                   jax.ShapeDtypeStruct((lay.n, IN_WIDTH - MIX_IN_WIDTH), F32)] + kv_shapes,
        scratch_shapes=[pltpu.VMEM((D_MODEL, IN_WIDTH), BF16)],
        compiler_params=_cparams(("arbitrary",)),
        name="inproj" + lay.tag,
    )(*xs, mod_l, w_in, cos_t, sin_t)


def _attn_kernel(sink_ref, q_ref, *refs, n_local, has_ctx, t_total):
    o_ref = refs[-1]
    k_refs = refs[:n_local]
    v_refs = refs[n_local:2 * n_local]
    tq = q_ref.shape[0]
    scale = HEAD_DIM ** -0.5
    k_parts = [kr[...] for kr in k_refs]
    v_parts = [vr[...] for vr in v_refs]
    if has_ctx:
        k_parts.append(refs[2 * n_local][0].astype(k_parts[0].dtype))
        v_parts.append(refs[2 * n_local + 1][0].astype(v_parts[0].dtype))
    kall = (jnp.concatenate(k_parts, axis=0) if len(k_parts) > 1 else k_parts[0]).astype(F32)
    vall = (jnp.concatenate(v_parts, axis=0) if len(v_parts) > 1 else v_parts[0]).astype(F32)
    nk = kall.shape[0]
    k_sw = pltpu.roll(kall, HEAD_DIM, 1)
    v_sw = pltpu.roll(vall, HEAD_DIM, 1)
    lo_half = lax.broadcasted_iota(I32, (1, V7X_LANES), 1) < HEAD_DIM
    er = jnp.where(lax.broadcasted_iota(I32, (2 * nk, V7X_LANES), 0) < nk, 0, 1)
    el = jnp.where(lax.broadcasted_iota(I32, (2 * nk, V7X_LANES), 1) < HEAD_DIM, 0, 1)
    ones_blk = jnp.where(er == el, 1.0, 0.0).astype(BF16)
    if n_local > 1:
        i = pl.program_id(1)
        band = refs[-2][...]
        first_blk = jnp.where(i == 0, NEG_BIG, 0.0)
        last_blk = jnp.where(i == t_total // tq - 1, NEG_BIG, 0.0)

        def mask_local(sc):
            loc = sc[:, :n_local * tq] + band
            parts = [loc[:, :tq] + first_blk, loc[:, tq:(n_local - 1) * tq], loc[:, (n_local - 1) * tq:] + last_blk]
            return jnp.concatenate(parts + [sc[:, n_local * tq:]], axis=1)
    else:
        mask_local = None
    v2es, sinks, scores = [], [], []
    for g in range(N_KV_HEADS):
        k_own, k_oth = (kall, k_sw) if g == 0 else (k_sw, kall)
        v_own, v_oth = (vall, v_sw) if g == 0 else (v_sw, vall)
        k2 = jnp.concatenate([jnp.where(lo_half, k_own, 0.0), jnp.where(lo_half, 0.0, k_oth)], axis=0).astype(BF16)
        v2 = jnp.concatenate([jnp.where(lo_half, v_own, 0.0), jnp.where(lo_half, 0.0, v_oth)], axis=0).astype(BF16)
        v2es.append(jnp.concatenate([v2, ones_blk], axis=1))
        pairs = [2 * g, 2 * g + 1]
        qq = jnp.concatenate([q_ref[:, p * V7X_LANES:(p + 1) * V7X_LANES] for p in pairs], axis=0)
        qq = (qq.astype(F32) * scale).astype(BF16)
        sinks.append((jnp.concatenate([jnp.full((tq, 1), sink_ref[0, 2 * p], F32) for p in pairs], axis=0),
                      jnp.concatenate([jnp.full((tq, 1), sink_ref[0, 2 * p + 1], F32) for p in pairs], axis=0)))
        scores.append(_dot_nt(qq, k2))
    pes, maxes = [], []
    for g in range(N_KV_HEADS):
        s_a = scores[g][:, :nk]
        s_b = scores[g][:, nk:]
        if mask_local is not None:
            s_a = mask_local(s_a)
            s_b = mask_local(s_b)
        m_a = jnp.maximum(jnp.max(s_a, axis=1, keepdims=True), sinks[g][0])
        m_b = jnp.maximum(jnp.max(s_b, axis=1, keepdims=True), sinks[g][1])
        pes.append(jnp.concatenate([jnp.exp(s_a - m_a).astype(BF16), jnp.exp(s_b - m_b).astype(BF16)], axis=1))
        maxes.append((m_a, m_b))
    for g in range(N_KV_HEADS):
        acc = _dot(pes[g], v2es[g])
        (m_a, m_b), (sink_a, sink_b) = maxes[g], sinks[g]
        sink_term = jnp.where(lo_half, jnp.exp(sink_a - m_a), jnp.exp(sink_b - m_b))
        o = acc[:, :V7X_LANES] / (acc[:, V7X_LANES:] + sink_term)
        for j, p in enumerate([2 * g, 2 * g + 1]):
            o_ref[:, p * V7X_LANES:(p + 1) * V7X_LANES] = o[j * tq:(j + 1) * tq].astype(o_ref.dtype)


def _attn_context(proj, sink_l, lay):
    t = lay.t_ctx
    kb, vb = COL_K // KV_WIDTH, COL_V // KV_WIDTH
    body = functools.partial(_attn_kernel, n_local=1, has_ctx=False, t_total=t)
    return pl.pallas_call(
        body,
        grid=(lay.b_ctx,),
        in_specs=[
            pl.BlockSpec(memory_space=pltpu.SMEM),
            pl.BlockSpec((t, ATTN_WIDTH), lambda b: (b, 0)),
            pl.BlockSpec((t, KV_WIDTH), lambda b: (b, kb)),
            pl.BlockSpec((t, KV_WIDTH), lambda b: (b, vb)),
        ],
        out_specs=pl.BlockSpec((t, ATTN_WIDTH), lambda b: (b, 0)),
        out_shape=jax.ShapeDtypeStruct((lay.n_ctx, ATTN_WIDTH), BF16),
        compiler_params=_cparams(("arbitrary",)),
        name="attn_ctx",
    )(sink_l, proj, proj, proj)


def _attn_latent(proj, k_ctx, v_ctx, sink_l, lay):
    t = lay.t_lat
    tq = ATTN_BLOCK
    nq = t // tq
    base = lay.n_ctx // tq
    kb, vb = COL_K // KV_WIDTH, COL_V // KV_WIDTH
    past = k_ctx.shape[1]

    def rows(off):
        return lambda b, i: base + b * nq + jnp.clip(i + off, 0, nq - 1)

    def kv_specs(col):
        return [pl.BlockSpec((tq, KV_WIDTH), (lambda b, i, f=rows(off): (f(b, i), col))) for off in (-1, 0, 1)]

    body = functools.partial(_attn_kernel, n_local=3, has_ctx=True, t_total=t)
    rel = np.arange(3 * tq)[None, :] - tq - (np.arange(2 * tq)[:, None] % tq)
    band = jnp.asarray(np.where(np.abs(rel) <= WINDOW, 0.0, NEG_BIG).astype(np.float32))
    return pl.pallas_call(
        body,
        grid=(lay.b_lat, nq),
        in_specs=[
            pl.BlockSpec(memory_space=pltpu.SMEM),
            pl.BlockSpec((tq, ATTN_WIDTH), lambda b, i: (base + b * nq + i, 0)),
            *kv_specs(kb),
            *kv_specs(vb),
            pl.BlockSpec((1, past, KV_WIDTH), lambda b, i: (b, 0, 0)),
            pl.BlockSpec((1, past, KV_WIDTH), lambda b, i: (b, 0, 0)),
            pl.BlockSpec(band.shape, lambda b, i: (0, 0)),
        ],
        out_specs=pl.BlockSpec((tq, ATTN_WIDTH), lambda b, i: (b * nq + i, 0)),
        out_shape=jax.ShapeDtypeStruct((lay.n_lat, ATTN_WIDTH), BF16),
        compiler_params=_cparams(("arbitrary", "arbitrary")),
        name="attn_lat" + lay.tag,
    )(sink_l, proj, proj, proj, proj, proj, proj, proj, k_ctx, v_ctx, band)


def _fourier_kernel(cs_ref, u_ref, cc_ref, sc_ref, o_ref, csb_ref, *, scale):
    @pl.when(pl.program_id(1) == 0)
    def _():
        csb_ref[...] = cs_ref[...].astype(BF16)

    z = u_ref[...].astype(BF16)
    zc = _dot(z, cc_ref[...].astype(BF16)).astype(BF16)
    zs = _dot(z, sc_ref[...].astype(BF16)).astype(BF16)
    zz = jnp.concatenate([zc, zs], axis=0)
    o_ref[...] = (_dot(csb_ref[...], zz) * scale).astype(o_ref.dtype)


@functools.lru_cache(maxsize=None)
def _dft_tables(t):
    idx = np.arange(t, dtype=np.int64)
    ang = 2.0 * np.pi * ((idx[:, None] * idx[None, :]) % t).astype(np.float64) / t
    cs = np.concatenate([np.cos(ang), -np.sin(ang)], axis=1).astype(np.float32)
    cw = FOURIER_WIDTH // FOURIER_GROUPS
    cidx = np.arange(cw, dtype=np.int64)
    cang = 2.0 * np.pi * ((cidx[:, None] * cidx[None, :]) % cw).astype(np.float64) / cw
    eye = np.eye(FOURIER_GROUPS)
    cc = np.kron(eye, np.cos(cang)).astype(np.float32)
    sc = np.kron(eye, np.sin(cang)).astype(np.float32)
    return cs, cc, sc


def _fourier(proj, row0, b, t, tm, name):
    cs, cc, sc = _dft_tables(t)
    cw = FOURIER_WIDTH // FOURIER_GROUPS
    nt = t // tm
    ub = COL_U // FOURIER_WIDTH
    base = row0 // t
    body = functools.partial(_fourier_kernel, scale=1.0 / math.sqrt(t * cw))
    return pl.pallas_call(
        body,
        grid=(nt, b),
        in_specs=[
            pl.BlockSpec((tm, 2 * t), lambda i, bb: (i, 0)),
            pl.BlockSpec((t, FOURIER_WIDTH), lambda i, bb: (base + bb, ub)),
            pl.BlockSpec((FOURIER_WIDTH, FOURIER_WIDTH), lambda i, bb: (0, 0)),
            pl.BlockSpec((FOURIER_WIDTH, FOURIER_WIDTH), lambda i, bb: (0, 0)),
        ],
        out_specs=pl.BlockSpec((tm, FOURIER_WIDTH), lambda i, bb: (bb * nt + i, 0)),
        out_shape=jax.ShapeDtypeStruct((b * t, FOURIER_WIDTH), BF16),
        scratch_shapes=[pltpu.VMEM((tm, 2 * t), BF16)],
        compiler_params=_cparams(("arbitrary", "arbitrary")),
        name=name,
    )(jnp.asarray(cs), proj, jnp.asarray(cc), jnp.asarray(sc))


HGRN_LEVELS = (64, 32, 16, 8, 4, 2)
HGRN_SAFE_RANGE = 80.0


@functools.lru_cache(maxsize=None)
def _hgrn_tables():
    c = HGRN_CHUNK
    return np.stack([np.tril(np.ones((c, c))), np.triu(np.ones((c, c)))]).astype(np.float32)


def _boundary_rows(b, m, reverse):
    c, w = b.shape
    half = m // 2
    off = half if reverse else half - 1
    if m >= 16:
        return jnp.concatenate(
            [jnp.broadcast_to(b[s + off:s + off + 1], (m, w)) for s in range(0, c, m)], axis=0)
    sub = lax.broadcasted_iota(I32, (c, w), 0) & 7
    b3 = b.reshape(c // 8, 8, w)

    def bcast(j):
        return jnp.broadcast_to(b3[:, j:j + 1, :], (c // 8, 8, w)).reshape(c, w)

    if m == 8:
        return bcast(off)
    if m == 4:
        return jnp.where(sub < 4, bcast(off), bcast(4 + off))
    assert m == 2
    if reverse:
        return jnp.where((sub & 1) == 1, b, pltpu.roll(b, c - 1, 0))
    return jnp.where((sub & 1) == 0, b, pltpu.roll(b, 1, 0))


def _hgrn_gates(q, z, v, loglb, log1mlb, onemlb, cum, reverse):
    c = HGRN_CHUNK
    log_sig = jnp.minimum(z, 0.0) - jnp.log1p(jnp.exp(-jnp.abs(z)))
    bb = log1mlb + log_sig
    mx = jnp.maximum(loglb, bb)
    lf = mx + jnp.log1p(jnp.exp(-jnp.abs(loglb - bb)))
    kk = onemlb * jax.nn.sigmoid(-z)
    b = _dot_exact_lhs(cum, lf)
    b_end = b[0:1] if reverse else b[c - 1:c]
    qt = (q * jnp.exp(b)).astype(BF16)
    kt = (kk * jnp.exp(b_end - b)).astype(BF16)
    return (q, kk, b), (qt, kt, v.astype(BF16))


def _head_stack(x):
    lane = lax.broadcasted_iota(I32, x.shape, 1)
    zero = jnp.zeros_like(x)
    return jnp.concatenate([jnp.where((lane >= h * HGRN_DK) & (lane < (h + 1) * HGRN_DK), x, zero)
                            for h in range(HGRN_HEADS)], axis=0)


class _HgrnDir:
    def __init__(self, f32_parts, bf16_parts, reverse):
        c = HGRN_CHUNK
        self.q, self.kk, self.b = f32_parts
        self.qt, self.kt, self.vb = bf16_parts
        self.reverse = reverse
        b_end = self.b[0:1] if reverse else self.b[c - 1:c]
        self.decay = jnp.exp(b_end)
        mid = c // 2 if reverse else c // 2 - 1
        self.rel = self.b - self.b[mid:mid + 1]
        self.span = jnp.max(jnp.abs(self.rel))

    def tree_decay_matrices(self):
        c = HGRN_CHUNK
        q, kk, b = self.q, self.kk, self.b
        row = lax.broadcasted_iota(I32, (c, 1), 0)
        ti = lax.broadcasted_iota(I32, (c, c), 0)
        si = lax.broadcasted_iota(I32, (c, c), 1)
        qb = q.astype(BF16)
        kb = kk.astype(BF16)
        heads = [slice(h * HGRN_DK, (h + 1) * HGRN_DK) for h in range(HGRN_HEADS)]
        acc = [jnp.where(ti == si, _dot_nt(qb[:, sl], kb[:, sl]), 0.0) for sl in heads]
        for m in HGRN_LEVELS:
            r = _boundary_rows(b, m, self.reverse)
            upper = (row & (m - 1)) >= (m // 2)
            q_side = jnp.logical_not(upper) if self.reverse else upper
            e = jnp.exp(jnp.where(q_side, b - r, r - b))
            qf = jnp.where(q_side, q * e, 0.0).astype(BF16)
            kf = jnp.where(q_side, 0.0, kk * e).astype(BF16)
            same_block = (ti & -m) == (si & -m)
            for h, sl in enumerate(heads):
                acc[h] = acc[h] + jnp.where(same_block, _dot_nt(qf[:, sl], kf[:, sl]), 0.0)
        return jnp.concatenate(acc, axis=1)

    def midpoint_decay_matrices(self):
        c = HGRN_CHUNK
        ti = lax.broadcasted_iota(I32, (c, HGRN_HEADS * c), 0)
        si = lax.broadcasted_iota(I32, (c, HGRN_HEADS * c), 1) & (c - 1)
        qm = (self.q * jnp.exp(self.rel)).astype(BF16)
        km = (self.kk * jnp.exp(-self.rel)).astype(BF16)
        causal = (si >= ti) if self.reverse else (si <= ti)
        return jnp.where(causal, _dot_nt(qm, _head_stack(km)), 0.0)

    def outputs(self, a_cat, st_ref, same_head, d):
        bd = st_ref[d]
        o = _dot_nt(self.qt, bd.astype(BF16)) + _dot(a_cat.astype(BF16), _head_stack(self.vb))
        st_ref[d] = jnp.where(same_head, bd * self.decay + _dot_tn(self.vb, self.kt), 0.0)
        return o


def _hgrn_kernel(hq_ref, ff_ref, fb_ref, hi_ref, hg_ref, lbp_ref, gn_ref, mall_ref, ones_ref, s0_ref, *refs, t,
                 layer):
    c = HGRN_CHUNK
    n = t // c
    rec_ref, sfin_ref, st_ref, of_ref, ob_ref, gf_ref, gb_ref = refs[-7:]
    blocks = [slice(h * HGRN_DK, (h + 1) * HGRN_DK) for h in range(HGRN_HEADS)]
    zero_blk = jnp.zeros((HGRN_DK, HGRN_DK), F32)

    def init_state():
        for d in range(2):
            st_ref[d] = jnp.concatenate(
                [jnp.concatenate([s0_ref[0, d, h] if j == h else zero_blk for j in range(HGRN_HEADS)], axis=1)
                 for h in range(HGRN_HEADS)], axis=0)

    def chunk_rows(ci):
        return pl.ds(pl.multiple_of(ci * c, c), c), pl.ds(pl.multiple_of((n - 1 - ci) * c, c), c)

    def gates_to(slot, ci):
        rf, rb = chunk_rows(ci)
        for d, (rows, f_ref) in enumerate(((rf, ff_ref), (rb, fb_ref))):
            f32_parts, bf16_parts = _hgrn_gates(
                hq_ref[rows, :], f_ref[rows, :], hi_ref[rows, :], lbp_ref[d, 0:1, :], lbp_ref[d, 1:2, :],
                lbp_ref[d, 2:3, :], mall_ref[d].astype(BF16), d == 1)
            for j in range(3):
                gf_ref[slot, d, j] = f32_parts[j]
                gb_ref[slot, d, j] = bf16_parts[j]

    def run(decay_matrices):
        init_state()
        gates_to(0, 0)

        def body(ci, widest):
            slot = ci & 1
            rf, rb = chunk_rows(ci)
            fwd = _HgrnDir([gf_ref[slot, 0, j] for j in range(3)], [gb_ref[slot, 0, j] for j in range(3)], False)
            bwd = _HgrnDir([gf_ref[slot, 1, j] for j in range(3)], [gb_ref[slot, 1, j] for j in range(3)], True)
            a_f, a_b = decay_matrices(fwd), decay_matrices(bwd)
            gates_to(1 - slot, jnp.minimum(ci + 1, n - 1))
            same_head = ones_ref[...] != 0.0
            of_ref[rf, :] = fwd.outputs(a_f, st_ref, same_head, 0)
            ob_ref[rb, :] = bwd.outputs(a_b, st_ref, same_head, 1)
            return jnp.maximum(widest, jnp.maximum(fwd.span, bwd.span))

        return lax.fori_loop(0, n, body, jnp.float32(0.0), unroll=2)

    widest = run(_HgrnDir.midpoint_decay_matrices)

    @pl.when(widest > HGRN_SAFE_RANGE)
    def _():
        run(_HgrnDir.tree_decay_matrices)

    slot = layer if sfin_ref.shape[1] > 1 else 0
    for j in range(sfin_ref.shape[1]):
        if j != slot:
            sfin_ref[0, j] = jnp.zeros(sfin_ref.shape[2:], F32)
    for d in range(2):
        bd = st_ref[d].T
        for h, sl in enumerate(blocks):
            sfin_ref[0, slot, d, h] = bd[sl, sl]
    o = of_ref[...] + ob_ref[...]
    ms = _dot_exact_rhs(o * o, ones_ref[...].astype(BF16)) * (1.0 / HGRN_DK)
    o = o * lax.rsqrt(ms + GN_EPS) * gn_ref[...]
    rec_ref[...] = (o * _silu(hg_ref[...])).astype(rec_ref.dtype)


def _hgrn(proj, row0, b, t, lbp, gn_row, s0t, name, layer=0, depth=1, states=None):
    base = row0 // t
    m_all = jnp.asarray(_hgrn_tables())
    ones_bd = jnp.asarray(np.kron(np.eye(HGRN_HEADS), np.ones((HGRN_DK, HGRN_DK))).astype(np.float32))

    def col(cstart):
        return pl.BlockSpec((t, HGRN_WIDTH), lambda bb, cb=(cstart - MIX_IN_WIDTH) // HGRN_WIDTH: (base + bb, cb))

    const2 = lambda bb: (0, 0)
    const3 = lambda bb: (0, 0, 0)
    st_shape = (2, HGRN_HEADS, HGRN_DK, HGRN_DK)
    body = functools.partial(_hgrn_kernel, t=t, layer=layer)
    first = states is None
    operands = [proj, proj, proj, proj, proj, lbp, gn_row, m_all, ones_bd, s0t]
    in_specs = [
        col(COL_HQ), col(COL_FF), col(COL_FB), col(COL_HI), col(COL_HG),
        pl.BlockSpec((2, 3, HGRN_WIDTH), const3),
        pl.BlockSpec((1, HGRN_WIDTH), const2),
        pl.BlockSpec(m_all.shape, const3),
        pl.BlockSpec(ones_bd.shape, const2),
        pl.BlockSpec((1,) + st_shape, lambda bb: (bb, 0, 0, 0, 0)),
    ]
    aliases = {}
    if states is not None:
        aliases = {len(operands): 1}
        operands.append(states)
        in_specs.append(pl.BlockSpec(memory_space=pl.ANY))
    return pl.pallas_call(
        body,
        grid=(b,),
        in_specs=in_specs,
        out_specs=[
            pl.BlockSpec((t, HGRN_WIDTH), lambda bb: (bb, 0)),
            pl.BlockSpec((1, depth if first else 1) + st_shape, lambda bb: (bb, 0 if first else layer, 0, 0, 0, 0)),
        ],
        out_shape=[
            jax.ShapeDtypeStruct((b * t, HGRN_WIDTH), BF16),
            jax.ShapeDtypeStruct((b, depth) + st_shape, F32),
        ],
        input_output_aliases=aliases,
        scratch_shapes=[
            pltpu.VMEM((2, HGRN_WIDTH, HGRN_WIDTH), F32),
            pltpu.VMEM((t, HGRN_WIDTH), F32),
            pltpu.VMEM((t, HGRN_WIDTH), F32),
            pltpu.VMEM((2, 2, 3, HGRN_CHUNK, HGRN_WIDTH), F32),
            pltpu.VMEM((2, 2, 3, HGRN_CHUNK, HGRN_WIDTH), BF16),
        ],
        compiler_params=_cparams(("arbitrary",)),
        name=name,
    )(*operands)


def _outproj_kernel(*refs, n_ctx_tiles, n_mix):
    mix = refs[:3 * n_mix]
    refs = refs[3 * n_mix:]
    xs = refs[:-14]
    (mod_ref, w_ref, g_ref, b_ref, rw_ref, rb_ref, x1_ref, hp_ref, meta_ref, gate_ref, cnt_ref, wb_ref, tri_ref,
     run_ref) = refs[-14:]
    tm = x1_ref.shape[0]
    is_ctx = pl.program_id(0) < n_ctx_tiles
    x_in = jnp.where(is_ctx, xs[0][...], xs[1][...]) if len(xs) == 2 else xs[0][...]
    if n_mix == 2:
        attn, four, rec = [jnp.where(is_ctx, mix[2 * j][...], mix[2 * j + 1][...]) for j in range(3)]
    else:
        attn, four, rec = [r[...] for r in mix]

    @pl.when(pl.program_id(0) == 0)
    def _():
        wb_ref[...] = w_ref[0].astype(BF16)
        r = lax.broadcasted_iota(I32, (tm, tm), 0)
        c = lax.broadcasted_iota(I32, (tm, tm), 1)
        tri_ref[...] = jnp.where(r < c, 1.0, 0.0).astype(BF16)
        run_ref[...] = jnp.zeros_like(run_ref)

    out = _dot(attn, wb_ref[0:ATTN_WIDTH, :])
    out = out + _dot(four, wb_ref[ATTN_WIDTH:ATTN_WIDTH + FOURIER_WIDTH, :])
    out = out + _dot(rec, wb_ref[ATTN_WIDTH + FOURIER_WIDTH:, :])
    gate1 = mod_ref[0, 2:3, :]
    y = DEEPNORM_ALPHA * x_in + gate1 * out
    x1 = _ln_plain(y, LN_EPS) * g_ref[...] + b_ref[...]
    x1_ref[...] = x1
    h2 = _ln_plain(x1, ADA_EPS) * (1.0 + mod_ref[0, 4:5, :]) + mod_ref[0, 3:4, :]
    hp_ref[...] = _pack_bf16_pair(h2[:, :HALF_D], h2[:, HALF_D:])

    h_hi = h2.astype(BF16)
    h_lo = (h2 - h_hi.astype(F32)).astype(BF16)
    rwt = rw_ref[...]
    w_hi = rwt.astype(BF16)
    w_lo = (rwt - w_hi.astype(F32)).astype(BF16)
    scores = jax.nn.sigmoid(_dot_nt(w_hi, h_hi) + _dot_nt(w_hi, h_lo) + _dot_nt(w_lo, h_hi))
    remaining = scores + rb_ref[...]
    eidx = lax.broadcasted_iota(I32, scores.shape, 0).astype(F32)
    chosen = jnp.zeros(scores.shape, jnp.bool_)
    picks = []
    for _ in range(TOP_K):
        mx = jnp.max(remaining, axis=0, keepdims=True)
        first = jnp.min(jnp.where(remaining == mx, eidx, float(N_EXPERTS)), axis=0, keepdims=True)
        pick = eidx == first
        picks.append((pick, first))
        chosen = jnp.logical_or(chosen, pick)
        remaining = jnp.where(pick, -jnp.inf, remaining)
    sel = jnp.where(chosen, scores, 0.0)
    gates = sel / jnp.sum(sel, axis=0, keepdims=True) * ROUTED_SCALE

    onehot = jnp.where(chosen, 1.0, 0.0)
    rank = run_ref[...] + _dot(onehot.astype(BF16), tri_ref[...])
    run_ref[...] += jnp.sum(onehot, axis=1, keepdims=True)
    cnt_ref[...] = run_ref[...]

    ids, rks, gks = [], [], []
    for pick, first in picks:
        ids.append(first.astype(I32))
        rks.append(jnp.sum(jnp.where(pick, rank, 0.0), axis=0, keepdims=True).astype(I32))
        gks.append(jnp.sum(jnp.where(pick, gates, 0.0), axis=0, keepdims=True))
    meta_ref[...] = jnp.concatenate(ids + rks, axis=0)
    gate_ref[...] = jnp.concatenate(gks, axis=0)


def _outproj(attn, four, rec, x, mod_l, w_out, layer, g1, b1, rw, rb, lay, tm):
    n_tiles = lay.n // tm
    n_ctx_tiles = lay.n_ctx // tm
    row = lambda i: (i, 0)
    const = lambda i: (0, 0)
    xs = x if isinstance(x, tuple) else (x,)
    return pl.pallas_call(
        functools.partial(_outproj_kernel, n_ctx_tiles=n_ctx_tiles, n_mix=len(attn)),
        grid=(n_tiles,),
        in_specs=[
            *_group_specs(len(attn), tm, ATTN_WIDTH, n_ctx_tiles),
            *_group_specs(len(four), tm, FOURIER_WIDTH, n_ctx_tiles),
            *_group_specs(len(rec), tm, HGRN_WIDTH, n_ctx_tiles),
            *_group_specs(len(xs), tm, D_MODEL, n_ctx_tiles),
            pl.BlockSpec((1, N_MOD, D_MODEL), lambda i: (lay.cond_row(i, tm), 0, 0)),
            pl.BlockSpec((1, D_MODEL, D_MODEL), lambda i: (layer, 0, 0)),
            pl.BlockSpec((1, D_MODEL), const),
            pl.BlockSpec((1, D_MODEL), const),
            pl.BlockSpec((N_EXPERTS, D_MODEL), const),
            pl.BlockSpec((N_EXPERTS, 1), const),
        ],
        out_specs=[
            pl.BlockSpec((tm, D_MODEL), row),
            pl.BlockSpec((tm, HALF_D), row),
            pl.BlockSpec((2 * TOP_K, tm), lambda i: (0, i)),
            pl.BlockSpec((TOP_K, tm), lambda i: (0, i)),
            pl.BlockSpec((N_EXPERTS, 1), const),
        ],
        out_shape=[
            jax.ShapeDtypeStruct((lay.n, D_MODEL), F32),
            jax.ShapeDtypeStruct((lay.n, HALF_D), I32),
            jax.ShapeDtypeStruct((2 * TOP_K, lay.n), I32),
            jax.ShapeDtypeStruct((TOP_K, lay.n), F32),
            jax.ShapeDtypeStruct((N_EXPERTS, 1), F32),
        ],
        scratch_shapes=[
            pltpu.VMEM((D_MODEL, D_MODEL), BF16),
            pltpu.VMEM((tm, tm), BF16),
            pltpu.VMEM((N_EXPERTS, 1), F32),
        ],
        compiler_params=_cparams(("arbitrary",)),
        name="outproj_router" + lay.tag,
    )(*attn, *four, *rec, *xs, mod_l, w_out, g1, b1, rw, rb)


def _sc_workers():
    info = plsc.get_sparse_core_info()
    return info.num_cores, info.num_cores * info.num_subcores


def _sc_scatter_rows(rows, pos_b, r_out, tag=""):
    nc, nw = _sc_workers()
    n, w = rows.shape
    nbt, copies, _ = pos_b.shape
    assert nbt * SC_BATCH == n and nbt % (2 * nw) == 0
    per_w = nbt // nw
    mesh = plsc.VectorSubcoreMesh(core_axis_name="c", subcore_axis_name="s")

    @functools.partial(
        pl.kernel, mesh=mesh, out_type=jax.ShapeDtypeStruct((r_out, w), rows.dtype),
        scratch_types=[pltpu.VMEM((copies, SC_BATCH), I32), pltpu.VMEM((copies, SC_BATCH), I32),
                       pltpu.VMEM((SC_BATCH, w), rows.dtype), pltpu.VMEM((SC_BATCH, w), rows.dtype),
                       pltpu.SemaphoreType.DMA, pltpu.SemaphoreType.DMA,
                       pltpu.SemaphoreType.DMA, pltpu.SemaphoreType.DMA],
        name="sc_dispatch" + tag)
    def k(rows_hbm, pos_hbm, out_hbm, idx_a, idx_b, rows_a, rows_b, sem_ra, sem_rb, sem_sa, sem_sb):
        wid = lax.axis_index("s") * nc + lax.axis_index("c")
        first = wid * per_w

        def reads(j, idx_v, rows_v, sem):
            bt = first + j
            return (pltpu.make_async_copy(pos_hbm.at[bt], idx_v, sem),
                    pltpu.make_async_copy(rows_hbm.at[pl.ds(bt * SC_BATCH, SC_BATCH)], rows_v, sem))

        def scatters(idx_v, rows_v, sem):
            return [pltpu.make_async_copy(rows_v, out_hbm.at[idx_v.at[q]], sem) for q in range(copies)]

        def start(descs):
            for d in descs:
                d.start()

        def wait(descs):
            for d in descs:
                d.wait()

        start(reads(0, idx_a, rows_a, sem_ra))

        @pl.loop(0, per_w // 2)
        def _(p):
            j0 = 2 * p
            j1 = j0 + 1

            @pl.when(p > 0)
            def _():
                wait(scatters(idx_b, rows_b, sem_sb))

            start(reads(j1, idx_b, rows_b, sem_rb))
            wait(reads(j0, idx_a, rows_a, sem_ra))
            start(scatters(idx_a, rows_a, sem_sa))
            wait(reads(j1, idx_b, rows_b, sem_rb))
            start(scatters(idx_b, rows_b, sem_sb))
            wait(scatters(idx_a, rows_a, sem_sa))

            @pl.when(p + 1 < per_w // 2)
            def _():
                start(reads(j0 + 2, idx_a, rows_a, sem_ra))

        wait(scatters(idx_b, rows_b, sem_sb))

    return k(rows, pos_b)


def _sc_gather_rows(table, idx, tag=""):
    nc, nw = _sc_workers()
    r = idx.shape[0]
    w = table.shape[1]
    assert r % (2 * nw * SC_BATCH) == 0
    per_w = r // nw
    nb = per_w // SC_BATCH
    mesh = plsc.VectorSubcoreMesh(core_axis_name="c", subcore_axis_name="s")

    @functools.partial(
        pl.kernel, mesh=mesh, out_type=jax.ShapeDtypeStruct((r, w), table.dtype),
        scratch_types=[pltpu.VMEM((per_w,), I32),
                       pltpu.VMEM((SC_BATCH, w), table.dtype), pltpu.VMEM((SC_BATCH, w), table.dtype),
                       pltpu.SemaphoreType.DMA, pltpu.SemaphoreType.DMA,
                       pltpu.SemaphoreType.DMA, pltpu.SemaphoreType.DMA],
        name="sc_combine" + tag)
    def k(table_hbm, idx_hbm, out_hbm, idx_v, rows_a, rows_b, sem_ga, sem_gb, sem_wa, sem_wb):
        wid = lax.axis_index("s") * nc + lax.axis_index("c")
        base = wid * per_w
        pltpu.sync_copy(idx_hbm.at[pl.ds(base, per_w)], idx_v)

        def gather(j, rows_v, sem):
            return pltpu.make_async_copy(table_hbm.at[idx_v.at[pl.ds(j * SC_BATCH, SC_BATCH)]], rows_v, sem)

        def write(j, rows_v, sem):
            return pltpu.make_async_copy(rows_v, out_hbm.at[pl.ds(base + j * SC_BATCH, SC_BATCH)], sem)

        gather(0, rows_a, sem_ga).start()

        @pl.loop(0, nb // 2)
        def _(p):
            j0 = 2 * p
            j1 = j0 + 1

            @pl.when(p > 0)
            def _():
                write(j1 - 2, rows_b, sem_wb).wait()

            gather(j1, rows_b, sem_gb).start()
            gather(j0, rows_a, sem_ga).wait()
            write(j0, rows_a, sem_wa).start()
            gather(j1, rows_b, sem_gb).wait()
            write(j1, rows_b, sem_wb).start()
            write(j0, rows_a, sem_wa).wait()

            @pl.when(p + 1 < nb // 2)
            def _():
                gather(j0 + 2, rows_a, sem_ga).start()

        write(nb - 1, rows_b, sem_wb).wait()

    return k(table, idx)


def _experts_kernel(te_ref, na_ref, x_ref, w1_ref, w3_ref, w2_ref, o_ref, w1b_ref, w3b_ref, w2b_ref):
    del te_ref

    @pl.when(pl.program_id(0) < na_ref[0])
    def _():
        w1b_ref[...] = w1_ref[0, 0].astype(BF16)
        w3b_ref[...] = w3_ref[0, 0].astype(BF16)
        w2b_ref[...] = w2_ref[0, 0].astype(BF16)
        lo, hi = _unpack_bf16_pair(x_ref[...])
        lo = lo.astype(BF16)
        hi = hi.astype(BF16)
        a = _dot(lo, w1b_ref[0:HALF_D, :]) + _dot(hi, w1b_ref[HALF_D:, :])
        b = _dot(lo, w3b_ref[0:HALF_D, :]) + _dot(hi, w3b_ref[HALF_D:, :])
        y = _dot((_silu(a) * b).astype(BF16), w2b_ref[...])
        o_ref[...] = _pack_bf16_pair(y[:, :HALF_D], y[:, HALF_D:])


def _experts(xs, tile_expert, n_active, w1, w3, w2, layer, tm, tag):
    r = xs.shape[0]
    n_tiles = r // tm

    def xmap(j, te, na):
        return (jnp.minimum(j, na[0] - 1), 0)

    def wmap(j, te, na):
        return (layer, te[jnp.minimum(j, na[0] - 1)], 0, 0)

    grid_spec = pltpu.PrefetchScalarGridSpec(
        num_scalar_prefetch=2,
        grid=(n_tiles,),
        in_specs=[
            pl.BlockSpec((tm, HALF_D), xmap),
            pl.BlockSpec((1, 1, D_MODEL, EXPERT_FF), wmap),
            pl.BlockSpec((1, 1, D_MODEL, EXPERT_FF), wmap),
            pl.BlockSpec((1, 1, EXPERT_FF, D_MODEL), wmap),
        ],
        out_specs=pl.BlockSpec((tm, HALF_D), xmap),
        scratch_shapes=[
            pltpu.VMEM((D_MODEL, EXPERT_FF), BF16),
            pltpu.VMEM((D_MODEL, EXPERT_FF), BF16),
            pltpu.VMEM((EXPERT_FF, D_MODEL), BF16),
        ],
    )
    return pl.pallas_call(
        _experts_kernel,
        grid_spec=grid_spec,
        out_shape=jax.ShapeDtypeStruct((r, HALF_D), I32),
        compiler_params=_cparams(("arbitrary",)),
        name="experts" + tag,
    )(tile_expert, n_active, xs, w1, w3, w2)


def _combine_kernel(yp_ref, gate_ref, hp_ref, sw1_ref, sw3_ref, sw2_ref, x_ref, mod_ref, g_ref, b_ref, *refs,
                    n_ctx_tiles):
    outs = refs[:-3]
    w1b_ref, w3b_ref, w2b_ref = refs[-3:]

    @pl.when(pl.program_id(0) == 0)
    def _():
        w1b_ref[...] = sw1_ref[...].astype(BF16)
        w3b_ref[...] = sw3_ref[...].astype(BF16)
        w2b_ref[...] = sw2_ref[...].astype(BF16)

    lo, hi = _unpack_bf16_pair(hp_ref[...])
    lo = lo.astype(BF16)
    hi = hi.astype(BF16)
    a = _dot(lo, w1b_ref[0:HALF_D, :]) + _dot(hi, w1b_ref[HALF_D:, :])
    b = _dot(lo, w3b_ref[0:HALF_D, :]) + _dot(hi, w3b_ref[HALF_D:, :])
    shared = _dot((_silu(a) * b).astype(BF16), w2b_ref[...])
    acc_lo = shared[:, :HALF_D]
    acc_hi = shared[:, HALF_D:]
    gates = gate_ref[...].T
    for k in range(TOP_K):
        ylo, yhi = _unpack_bf16_pair(yp_ref[k])
        gk = gates[:, k:k + 1]
        acc_lo = acc_lo + gk * ylo
        acc_hi = acc_hi + gk * yhi
    moe = jnp.concatenate([acc_lo, acc_hi], axis=1)
    y = DEEPNORM_ALPHA * x_ref[...] + mod_ref[0, 5:6, :] * moe
    res = _ln_plain(y, LN_EPS) * g_ref[...] + b_ref[...]
    if len(outs) == 1:
        outs[0][...] = res
    else:
        @pl.when(pl.program_id(0) < n_ctx_tiles)
        def _():
            outs[0][...] = res

        @pl.when(pl.program_id(0) >= n_ctx_tiles)
        def _():
            outs[1][...] = res


def _combine(yp, gate8, hp, sw1, sw3, sw2, x1, mod_l, g2, b2, lay, tm, split_out):
    n_tiles = lay.n // tm
    n_ctx_tiles = lay.n_ctx // tm
    row = lambda i: (i, 0)
    const = lambda i: (0, 0)
    if split_out:
        out_specs = _group_specs(2, tm, D_MODEL, n_ctx_tiles)
        out_shape = [jax.ShapeDtypeStruct((lay.n_ctx, D_MODEL), F32), jax.ShapeDtypeStruct((lay.n_lat, D_MODEL), F32)]
    else:
        out_specs = pl.BlockSpec((tm, D_MODEL), row)
        out_shape = jax.ShapeDtypeStruct((lay.n, D_MODEL), F32)
    return pl.pallas_call(
        functools.partial(_combine_kernel, n_ctx_tiles=n_ctx_tiles),
        grid=(n_tiles,),
        in_specs=[
            pl.BlockSpec((TOP_K, tm, HALF_D), lambda i: (0, i, 0)),
            pl.BlockSpec((TOP_K, tm), lambda i: (0, i)),
            pl.BlockSpec((tm, HALF_D), row),
            pl.BlockSpec((D_MODEL, EXPERT_FF), const),
            pl.BlockSpec((D_MODEL, EXPERT_FF), const),
            pl.BlockSpec((EXPERT_FF, D_MODEL), const),
            pl.BlockSpec((tm, D_MODEL), row),
            pl.BlockSpec((1, N_MOD, D_MODEL), lambda i: (lay.cond_row(i, tm), 0, 0)),
            pl.BlockSpec((1, D_MODEL), const),
            pl.BlockSpec((1, D_MODEL), const),
        ],
        out_specs=out_specs,
        out_shape=out_shape,
        scratch_shapes=[
            pltpu.VMEM((D_MODEL, EXPERT_FF), BF16),
            pltpu.VMEM((D_MODEL, EXPERT_FF), BF16),
            pltpu.VMEM((EXPERT_FF, D_MODEL), BF16),
        ],
        compiler_params=_cparams(("arbitrary",)),
        name="combine_norm" + lay.tag,
    )(yp, gate8, hp, sw1, sw3, sw2, x1, mod_l, g2, b2)


def _moe_dispatch(hp, meta, counts, lay, tile):
    n = lay.n
    r_max = n * TOP_K + N_EXPERTS * tile
    n_tiles = r_max // tile
    cnt = counts.reshape(N_EXPERTS).astype(I32)
    padded = ((cnt + tile - 1) // tile) * tile
    ends = jnp.cumsum(padded)
    offsets = ends - padded
    idx8 = meta[:TOP_K]
    base8 = jnp.sum(jnp.where(idx8[:, :, None] == jnp.arange(N_EXPERTS, dtype=I32), offsets, 0), axis=-1)
    pos = (base8 + meta[TOP_K:]).astype(I32)
    tile_start = jnp.arange(n_tiles, dtype=I32) * tile
    tile_expert = jnp.minimum(jnp.sum(tile_start[:, None] >= ends[None, :], axis=1), N_EXPERTS - 1).astype(I32)
    n_active = (ends[-1] // tile).astype(I32).reshape(1)
    pos_b = pos.reshape(TOP_K, n // SC_BATCH, SC_BATCH).transpose(1, 0, 2)
    xs = _sc_scatter_rows(hp, pos_b, r_max, lay.tag)
    return xs, tile_expert, n_active, pos


def _moe_combine(ys, pos, gate8, hp, sw1, sw3, sw2, x1, mod_l, g2, b2, lay, split_out):
    n = lay.n
    yp = _sc_gather_rows(ys, pos.reshape(n * TOP_K), lay.tag).reshape(TOP_K, n, HALF_D)
    return _combine(yp, gate8, hp, sw1, sw3, sw2, x1, mod_l, g2, b2, lay, TOKEN_TILE, split_out)


def kernel(x_prompt, x_sample, cache_k, cache_v, state_hgrn, c, c_ctx, w_ada, b_ada, w_in, w_out, attn_sink, hgrn_lb, hgrn_norm, ln1_g, ln1_b, ln2_g, ln2_b, router_w, router_b, moe_w1, moe_w3, moe_w2, shared_w1, shared_w3, shared_w2):
    b_ctx, t_ctx, _ = x_prompt.shape
    b_lat, t_lat, _ = x_sample.shape
    past = cache_k.shape[2]
    tm = TOKEN_TILE
    assert 1 + b_lat <= COND_ROWS
    lay = _Layout(b_ctx, t_ctx, b_lat, t_lat)
    assert lay.n_ctx % tm == 0 and t_lat % tm == 0 and lay.n_ctx % t_lat == 0

    cond = jnp.concatenate([c_ctx[None, :], c, jnp.zeros((COND_ROWS - 1 - b_lat, D_MODEL), F32)], axis=0)
    mod = _adaln(cond, w_ada, b_ada).reshape(DEPTH, COND_ROWS, N_MOD, D_MODEL)

    lb_all = jnp.cumsum(jax.nn.softmax(hgrn_lb.astype(F32), axis=0), axis=0)
    lb_all = lb_all - lb_all[:1]
    lbp = jnp.stack([jnp.log(lb_all), jnp.log1p(-lb_all), 1.0 - lb_all], axis=2)

    cos_t, sin_t = _rope_tables(lay, tm)
    zero_state = jnp.zeros((b_ctx, 2, HGRN_HEADS, HGRN_DK, HGRN_DK), F32)

    def layer(l, lay, x, split_out, states):
        lat = slice(lay.lat_first, lay.lat_first + lay.b_lat)
        outs = _inproj(x, mod[l], w_in, l, cos_t, sin_t, lay, tm)
        proj, proj_h = outs[0], outs[1]
        sink_l = attn_sink[l].reshape(1, N_HEADS)
        gn_row = jnp.tile(hgrn_norm[l], HGRN_HEADS).reshape(1, HGRN_WIDTH)
        attn, four, rec, extras = [], [], [], None
        if lay.b_ctx:
            attn.append(_attn_context(proj, sink_l, lay))
            four.append(_fourier(proj, 0, lay.b_ctx, t_ctx, t_ctx, "fourier_ctx"))
            rec_c, states = _hgrn(proj_h, 0, lay.b_ctx, t_ctx, lbp[l], gn_row, zero_state, "hgrn_ctx", l, DEPTH,
                                  states)
            rec.append(rec_c)
            extras = (outs[2], outs[3], states)
        attn.append(_attn_latent(proj, cache_k[lat, l].reshape(lay.b_lat, past, KV_WIDTH),
                                 cache_v[lat, l].reshape(lay.b_lat, past, KV_WIDTH), sink_l, lay))
        four.append(_fourier(proj, lay.n_ctx, lay.b_lat, t_lat, min(t_lat, 512), "fourier_lat" + lay.tag))
        s0t = jnp.swapaxes(state_hgrn[lat, l].astype(F32), -1, -2)
        rec.append(_hgrn(proj_h, lay.n_ctx, lay.b_lat, t_lat, lbp[l], gn_row, s0t, "hgrn_lat" + lay.tag)[0])
        x1, hp, meta, gate8, counts = _outproj(
            tuple(attn), tuple(four), tuple(rec), x, mod[l], w_out, l, ln1_g[l].reshape(1, -1),
            ln1_b[l].reshape(1, -1), router_w[l].T, router_b[l].reshape(-1, 1), lay, tm)
        xs, tile_expert, n_active, pos = _moe_dispatch(hp, meta, counts, lay, EXPERT_TILE)
        ys = _experts(xs, tile_expert, n_active, moe_w1, moe_w3, moe_w2, l, EXPERT_TILE, lay.tag)
        x = _moe_combine(ys, pos, gate8, hp, shared_w1[l], shared_w3[l], shared_w2[l], x1, mod[l],
                         ln2_g[l].reshape(1, -1), ln2_b[l].reshape(1, -1), lay, split_out)
        return x, extras

    x = (x_prompt.reshape(lay.n_ctx, D_MODEL), x_sample.reshape(lay.n_lat, D_MODEL))
    ks_out, vs_out, states = [], [], None
    for l in range(DEPTH):
        x, (k_new, v_new, states) = layer(l, lay, x, l == DEPTH - 1, states)
        ks_out.append(k_new.reshape(b_ctx, t_ctx, N_KV_HEADS, HEAD_DIM))
        vs_out.append(v_new.reshape(b_ctx, t_ctx, N_KV_HEADS, HEAD_DIM))

    y_prompt = x[0].reshape(b_ctx, t_ctx, D_MODEL)
    y_sample = x[1].reshape(b_lat, t_lat, D_MODEL)
    new_cache_k = jnp.stack(ks_out, axis=1)
    new_cache_v = jnp.stack(vs_out, axis=1)
    new_state = states.astype(x_prompt.dtype)
    return (y_prompt, y_sample, new_cache_k, new_cache_v, new_state)
```

```python
import functools
import math

import numpy as np
import jax
import jax.numpy as jnp
from jax import lax
from jax.experimental import pallas as pl
from jax.experimental.pallas import tpu as pltpu
from jax.experimental.pallas import tpu_sc as plsc

F32 = jnp.float32
BF16 = jnp.bfloat16
I32 = jnp.int32

D_MODEL = 1024
HALF_D = D_MODEL // 2
DEPTH = 2
GRID_W = 64
ROPE_BASE = 10000.0
HEAD_DIM = 64
ATTN_WIDTH = 512
N_HEADS = 8
N_KV_HEADS = 2
KV_GROUP = 4
KV_WIDTH = N_KV_HEADS * HEAD_DIM
WINDOW = 128
ATTN_BLOCK = 128
FOURIER_WIDTH = 256
FOURIER_GROUPS = 4
HGRN_WIDTH = 256
HGRN_HEADS = 4
HGRN_DK = 64
HGRN_CHUNK = 64
IN_WIDTH = 2304
N_EXPERTS = 64
TOP_K = 8
EXPERT_FF = 256
ROUTED_SCALE = 2.5
N_MOD = 6
LN_EPS = 1e-5
ADA_EPS = 1e-6
GN_EPS = 1e-6
DEEPNORM_ALPHA = (2 * DEPTH) ** 0.25

COL_Q = 0
COL_K = 512
COL_V = 640
COL_U = 768
COL_HQ = 1024
COL_FF = 1280
COL_FB = 1536
COL_HI = 1792
COL_HG = 2048
ROPE_COLS = COL_V
MIX_IN_WIDTH = COL_HQ

V7X_LANES = 128
COND_ROWS = 16
NEG_BIG = -1e30
TOKEN_TILE = 512
EXPERT_TILE = 1024
SC_BATCH = 64

VMEM_LIMIT = 56 * 1024 * 1024


def _cparams(sem):
    return pltpu.CompilerParams(dimension_semantics=sem, vmem_limit_bytes=VMEM_LIMIT)


def _dot(a, b):
    return jnp.dot(a, b, preferred_element_type=F32)


def _dot_nt(a, b):
    return lax.dot_general(a, b, (((1,), (1,)), ((), ())), preferred_element_type=F32)


def _dot_tn(a, b):
    return lax.dot_general(a, b, (((0,), (0,)), ((), ())), preferred_element_type=F32)


def _split3(x):
    hi = x.astype(BF16)
    r1 = x - hi.astype(F32)
    mid = r1.astype(BF16)
    lo = (r1 - mid.astype(F32)).astype(BF16)
    return hi, mid, lo


def _dot_exact_lhs(m_bf16, x):
    hi, mid, lo = _split3(x)
    return _dot(m_bf16, hi) + _dot(m_bf16, mid) + _dot(m_bf16, lo)


def _dot_exact_rhs(x, m_bf16):
    hi, mid, lo = _split3(x)
    return _dot(hi, m_bf16) + _dot(mid, m_bf16) + _dot(lo, m_bf16)


def _dot_hp(a, b):
    a_hi = a.astype(BF16)
    a_lo = (a - a_hi.astype(F32)).astype(BF16)
    b_hi = b.astype(BF16)
    b_lo = (b - b_hi.astype(F32)).astype(BF16)
    return _dot(a_hi, b_hi) + _dot(a_hi, b_lo) + _dot(a_lo, b_hi)


def _pack_bf16_pair(lo, hi):
    return lax.bitcast_convert_type(pltpu.pack_elementwise([lo, hi], packed_dtype=BF16), I32)


def _unpack_bf16_pair(w):
    u = lax.bitcast_convert_type(w, jnp.uint32)
    lo = pltpu.unpack_elementwise(u, index=0, packed_dtype=BF16, unpacked_dtype=F32)
    hi = pltpu.unpack_elementwise(u, index=1, packed_dtype=BF16, unpacked_dtype=F32)
    return lo, hi


def _ln_plain(x, eps):
    mu = jnp.mean(x, axis=-1, keepdims=True)
    xc = x - mu
    var = jnp.mean(xc * xc, axis=-1, keepdims=True)
    return xc * lax.rsqrt(var + eps)


def _silu(x):
    return x * jax.nn.sigmoid(x)


def _adaln_kernel(c_ref, w_ref, b_ref, o_ref):
    s = _silu(c_ref[...])
    o_ref[0] = _dot_hp(s, w_ref[0]) + b_ref[0]


def _adaln(cond, w_ada, b_ada):
    return pl.pallas_call(
        _adaln_kernel,
        grid=(DEPTH, N_MOD),
        in_specs=[
            pl.BlockSpec((COND_ROWS, D_MODEL), lambda l, j: (0, 0)),
            pl.BlockSpec((1, D_MODEL, D_MODEL), lambda l, j: (l, 0, j)),
            pl.BlockSpec((1, 1, D_MODEL), lambda l, j: (l, 0, j)),
        ],
        out_specs=pl.BlockSpec((1, COND_ROWS, D_MODEL), lambda l, j: (l, 0, j)),
        out_shape=jax.ShapeDtypeStruct((DEPTH, COND_ROWS, N_MOD * D_MODEL), F32),
        compiler_params=_cparams(("arbitrary", "arbitrary")),
        name="adaln",
    )(cond, w_ada, b_ada.reshape(DEPTH, 1, N_MOD * D_MODEL))


class _Layout:
    def __init__(self, b_ctx, t_ctx, b_lat, t_lat, lat_first=0, tag=""):
        self.b_ctx, self.t_ctx, self.b_lat, self.t_lat = b_ctx, t_ctx, b_lat, t_lat
        self.lat_first = lat_first
        self.n_ctx = b_ctx * t_ctx
        self.n_lat = b_lat * t_lat
        self.n = self.n_ctx + self.n_lat
        self.tag = tag

    def cond_row(self, tile, tm):
        n_ctx_tiles = self.n_ctx // tm
        per_batch = self.t_lat // tm
        return jnp.where(tile < n_ctx_tiles, 0, 1 + self.lat_first + (tile - n_ctx_tiles) // per_batch)


def _group_specs(n_arrays, tm, width, n_ctx_tiles):
    if n_arrays == 1:
        return [pl.BlockSpec((tm, width), lambda i: (i, 0))]
    return [pl.BlockSpec((tm, width), lambda i: (jnp.minimum(i, n_ctx_tiles - 1), 0)),
            pl.BlockSpec((tm, width), lambda i: (jnp.maximum(i - n_ctx_tiles, 0), 0))]


def _inproj_kernel(*refs, n_ctx_tiles):
    n_tail = 9 if n_ctx_tiles > 0 else 7
    xs = refs[:-n_tail]
    mod_ref, w_ref, cos_ref, sin_ref, oa_ref, oh_ref = refs[-n_tail:-n_tail + 6]
    wb_ref = refs[-1]

    @pl.when(pl.program_id(0) == 0)
    def _():
        wb_ref[...] = w_ref[0].astype(BF16)

    if len(xs) == 2:
        x = jnp.where(pl.program_id(0) < n_ctx_tiles, xs[0][...], xs[1][...])
    else:
        x = xs[0][...]
    shift = mod_ref[0, 0:1, :]
    scale = mod_ref[0, 1:2, :]
    h = (_ln_plain(x, ADA_EPS) * (1.0 + scale) + shift).astype(BF16)
    p = _dot(h, wb_ref[...])
    cos = cos_ref[...]
    sin = sin_ref[...]
    lane = lax.broadcasted_iota(I32, cos.shape, 1)
    first_half = (lane & 31) < 16
    for cb in range(ROPE_COLS // V7X_LANES):
        seg = p[:, cb * V7X_LANES:(cb + 1) * V7X_LANES]
        partner = jnp.where(first_half, pltpu.roll(seg, V7X_LANES - 16, 1), pltpu.roll(seg, 16, 1))
        oa_ref[:, cb * V7X_LANES:(cb + 1) * V7X_LANES] = (seg * cos + partner * sin).astype(BF16)
    oa_ref[:, ROPE_COLS:] = p[:, ROPE_COLS:MIX_IN_WIDTH].astype(BF16)
    oh_ref[...] = p[:, MIX_IN_WIDTH:]

    if n_ctx_tiles > 0:
        kc_ref, vc_ref = refs[-3], refs[-2]

        @pl.when(pl.program_id(0) < n_ctx_tiles)
        def _():
            kc_ref[...] = p[:, COL_K:COL_K + KV_WIDTH]
            vc_ref[...] = p[:, COL_V:COL_V + KV_WIDTH]


def _rope_tables(lay, tm):
    t = lay.t_lat
    pos = jnp.arange(t)
    row = (pos // GRID_W).astype(F32)
    col = (pos % GRID_W).astype(F32)
    n_freq = HEAD_DIM // 4
    inv = ROPE_BASE ** (-jnp.arange(n_freq, dtype=F32) / n_freq)
    ang_r = row[:, None] * inv
    ang_c = col[:, None] * inv
    ang = jnp.concatenate([ang_r, ang_r, ang_c, ang_c], axis=1)
    sign = jnp.concatenate([-jnp.ones(n_freq), jnp.ones(n_freq), -jnp.ones(n_freq), jnp.ones(n_freq)]).astype(F32)
    cos = jnp.cos(ang)
    sin = jnp.sin(ang) * sign
    cos = jnp.concatenate([jnp.ones((tm, HEAD_DIM), F32), cos], axis=0)
    sin = jnp.concatenate([jnp.zeros((tm, HEAD_DIM), F32), sin], axis=0)
    return jnp.tile(cos, (1, 2)), jnp.tile(sin, (1, 2))


def _inproj(x, mod_l, w_in, layer, cos_t, sin_t, lay, tm):
    n_tiles = lay.n // tm
    n_ctx_tiles = lay.n_ctx // tm
    per_batch = lay.t_lat // tm

    def tbl(i):
        return jnp.where(i < n_ctx_tiles, 0, 1 + (i - n_ctx_tiles) % per_batch)

    xs = x if isinstance(x, tuple) else (x,)
    kv_specs, kv_shapes = [], []
    if n_ctx_tiles > 0:
        kv_specs = [pl.BlockSpec((tm, KV_WIDTH), lambda i: (jnp.minimum(i, n_ctx_tiles - 1), 0))] * 2
        kv_shapes = [jax.ShapeDtypeStruct((lay.n_ctx, KV_WIDTH), F32)] * 2
    return pl.pallas_call(
        functools.partial(_inproj_kernel, n_ctx_tiles=n_ctx_tiles),
        grid=(n_tiles,),
        in_specs=[
            *_group_specs(len(xs), tm, D_MODEL, n_ctx_tiles),
            pl.BlockSpec((1, N_MOD, D_MODEL), lambda i: (lay.cond_row(i, tm), 0, 0)),
            pl.BlockSpec((1, D_MODEL, IN_WIDTH), lambda i: (layer, 0, 0), pipeline_mode=pl.Buffered(1)),
            pl.BlockSpec((tm, V7X_LANES), lambda i: (tbl(i), 0)),
            pl.BlockSpec((tm, V7X_LANES), lambda i: (tbl(i), 0)),
        ],
        out_specs=[pl.BlockSpec((tm, MIX_IN_WIDTH), lambda i: (i, 0)),
                   pl.BlockSpec((tm, IN_WIDTH - MIX_IN_WIDTH), lambda i: (i, 0))] + kv_specs,
        out_shape=[jax.ShapeDtypeStruct((lay.n, MIX_IN_WIDTH), BF16),
                   jax.ShapeDtypeStruct((lay.n, IN_WIDTH - MIX_IN_WIDTH), F32)] + kv_shapes,
        scratch_shapes=[pltpu.VMEM((D_MODEL, IN_WIDTH), BF16)],
        compiler_params=_cparams(("arbitrary",)),
        name="inproj" + lay.tag,
    )(*xs, mod_l, w_in, cos_t, sin_t)


def _attn_kernel(sink_ref, q_ref, *refs, n_local, has_ctx, t_total):
    o_ref = refs[-1]
    k_refs = refs[:n_local]
    v_refs = refs[n_local:2 * n_local]
    tq = q_ref.shape[0]
    scale = HEAD_DIM ** -0.5
    k_parts = [kr[...] for kr in k_refs]
    v_parts = [vr[...] for vr in v_refs]
    if has_ctx:
        k_parts.append(refs[2 * n_local][0].astype(k_parts[0].dtype))
        v_parts.append(refs[2 * n_local + 1][0].astype(v_parts[0].dtype))
    kall = (jnp.concatenate(k_parts, axis=0) if len(k_parts) > 1 else k_parts[0]).astype(F32)
    vall = (jnp.concatenate(v_parts, axis=0) if len(v_parts) > 1 else v_parts[0]).astype(F32)
    nk = kall.shape[0]
    k_sw = pltpu.roll(kall, HEAD_DIM, 1)
    v_sw = pltpu.roll(vall, HEAD_DIM, 1)
    lo_half = lax.broadcasted_iota(I32, (1, V7X_LANES), 1) < HEAD_DIM
    er = jnp.where(lax.broadcasted_iota(I32, (2 * nk, V7X_LANES), 0) < nk, 0, 1)
    el = jnp.where(lax.broadcasted_iota(I32, (2 * nk, V7X_LANES), 1) < HEAD_DIM, 0, 1)
    ones_blk = jnp.where(er == el, 1.0, 0.0).astype(BF16)
    if n_local > 1:
        i = pl.program_id(1)
        band = refs[-2][...]
        first_blk = jnp.where(i == 0, NEG_BIG, 0.0)
        last_blk = jnp.where(i == t_total // tq - 1, NEG_BIG, 0.0)

        def mask_local(sc):
            loc = sc[:, :n_local * tq] + band
            parts = [loc[:, :tq] + first_blk, loc[:, tq:(n_local - 1) * tq], loc[:, (n_local - 1) * tq:] + last_blk]
            return jnp.concatenate(parts + [sc[:, n_local * tq:]], axis=1)
    else:
        mask_local = None
    v2es, sinks, scores = [], [], []
    for g in range(N_KV_HEADS):
        k_own, k_oth = (kall, k_sw) if g == 0 else (k_sw, kall)
        v_own, v_oth = (vall, v_sw) if g == 0 else (v_sw, vall)
        k2 = jnp.concatenate([jnp.where(lo_half, k_own, 0.0), jnp.where(lo_half, 0.0, k_oth)], axis=0).astype(BF16)
        v2 = jnp.concatenate([jnp.where(lo_half, v_own, 0.0), jnp.where(lo_half, 0.0, v_oth)], axis=0).astype(BF16)
        v2es.append(jnp.concatenate([v2, ones_blk], axis=1))
        pairs = [2 * g, 2 * g + 1]
        qq = jnp.concatenate([q_ref[:, p * V7X_LANES:(p + 1) * V7X_LANES] for p in pairs], axis=0)
        qq = (qq.astype(F32) * scale).astype(BF16)
        sinks.append((jnp.concatenate([jnp.full((tq, 1), sink_ref[0, 2 * p], F32) for p in pairs], axis=0),
                      jnp.concatenate([jnp.full((tq, 1), sink_ref[0, 2 * p + 1], F32) for p in pairs], axis=0)))
        scores.append(_dot_nt(qq, k2))
    pes, maxes = [], []
    for g in range(N_KV_HEADS):
        s_a = scores[g][:, :nk]
        s_b = scores[g][:, nk:]
        if mask_local is not None:
            s_a = mask_local(s_a)
            s_b = mask_local(s_b)
        m_a = jnp.maximum(jnp.max(s_a, axis=1, keepdims=True), sinks[g][0])
        m_b = jnp.maximum(jnp.max(s_b, axis=1, keepdims=True), sinks[g][1])
        pes.append(jnp.concatenate([jnp.exp(s_a - m_a).astype(BF16), jnp.exp(s_b - m_b).astype(BF16)], axis=1))
        maxes.append((m_a, m_b))
    for g in range(N_KV_HEADS):
        acc = _dot(pes[g], v2es[g])
        (m_a, m_b), (sink_a, sink_b) = maxes[g], sinks[g]
        sink_term = jnp.where(lo_half, jnp.exp(sink_a - m_a), jnp.exp(sink_b - m_b))
        o = acc[:, :V7X_LANES] / (acc[:, V7X_LANES:] + sink_term)
        for j, p in enumerate([2 * g, 2 * g + 1]):
            o_ref[:, p * V7X_LANES:(p + 1) * V7X_LANES] = o[j * tq:(j + 1) * tq].astype(o_ref.dtype)


def _attn_context(proj, sink_l, lay):
    t = lay.t_ctx
    kb, vb = COL_K // KV_WIDTH, COL_V // KV_WIDTH
    body = functools.partial(_attn_kernel, n_local=1, has_ctx=False, t_total=t)
    return pl.pallas_call(
        body,
        grid=(lay.b_ctx,),
        in_specs=[
            pl.BlockSpec(memory_space=pltpu.SMEM),
            pl.BlockSpec((t, ATTN_WIDTH), lambda b: (b, 0)),
            pl.BlockSpec((t, KV_WIDTH), lambda b: (b, kb)),
            pl.BlockSpec((t, KV_WIDTH), lambda b: (b, vb)),
        ],
        out_specs=pl.BlockSpec((t, ATTN_WIDTH), lambda b: (b, 0)),
        out_shape=jax.ShapeDtypeStruct((lay.n_ctx, ATTN_WIDTH), BF16),
        compiler_params=_cparams(("arbitrary",)),
        name="attn_ctx",
    )(sink_l, proj, proj, proj)


def _attn_latent(proj, k_ctx, v_ctx, sink_l, lay):
    t = lay.t_lat
    tq = ATTN_BLOCK
    nq = t // tq
    base = lay.n_ctx // tq
    kb, vb = COL_K // KV_WIDTH, COL_V // KV_WIDTH
    past = k_ctx.shape[1]

    def rows(off):
        return lambda b, i: base + b * nq + jnp.clip(i + off, 0, nq - 1)

    def kv_specs(col):
        return [pl.BlockSpec((tq, KV_WIDTH), (lambda b, i, f=rows(off): (f(b, i), col))) for off in (-1, 0, 1)]

    body = functools.partial(_attn_kernel, n_local=3, has_ctx=True, t_total=t)
    rel = np.arange(3 * tq)[None, :] - tq - (np.arange(2 * tq)[:, None] % tq)
    band = jnp.asarray(np.where(np.abs(rel) <= WINDOW, 0.0, NEG_BIG).astype(np.float32))
    return pl.pallas_call(
        body,
        grid=(lay.b_lat, nq),
        in_specs=[
            pl.BlockSpec(memory_space=pltpu.SMEM),
            pl.BlockSpec((tq, ATTN_WIDTH), lambda b, i: (base + b * nq + i, 0)),
            *kv_specs(kb),
            *kv_specs(vb),
            pl.BlockSpec((1, past, KV_WIDTH), lambda b, i: (b, 0, 0)),
            pl.BlockSpec((1, past, KV_WIDTH), lambda b, i: (b, 0, 0)),
            pl.BlockSpec(band.shape, lambda b, i: (0, 0)),
        ],
        out_specs=pl.BlockSpec((tq, ATTN_WIDTH), lambda b, i: (b * nq + i, 0)),
        out_shape=jax.ShapeDtypeStruct((lay.n_lat, ATTN_WIDTH), BF16),
        compiler_params=_cparams(("arbitrary", "arbitrary")),
        name="attn_lat" + lay.tag,
    )(sink_l, proj, proj, proj, proj, proj, proj, proj, k_ctx, v_ctx, band)


def _fourier_kernel(cs_ref, u_ref, cc_ref, sc_ref, o_ref, csb_ref, *, scale):
    @pl.when(pl.program_id(1) == 0)
    def _():
        csb_ref[...] = cs_ref[...].astype(BF16)

    z = u_ref[...].astype(BF16)
    zc = _dot(z, cc_ref[...].astype(BF16)).astype(BF16)
    zs = _dot(z, sc_ref[...].astype(BF16)).astype(BF16)
    zz = jnp.concatenate([zc, zs], axis=0)
    o_ref[...] = (_dot(csb_ref[...], zz) * scale).astype(o_ref.dtype)


@functools.lru_cache(maxsize=None)
def _dft_tables(t):
    idx = np.arange(t, dtype=np.int64)
    ang = 2.0 * np.pi * ((idx[:, None] * idx[None, :]) % t).astype(np.float64) / t
    cs = np.concatenate([np.cos(ang), -np.sin(ang)], axis=1).astype(np.float32)
    cw = FOURIER_WIDTH // FOURIER_GROUPS
    cidx = np.arange(cw, dtype=np.int64)
    cang = 2.0 * np.pi * ((cidx[:, None] * cidx[None, :]) % cw).astype(np.float64) / cw
    eye = np.eye(FOURIER_GROUPS)
    cc = np.kron(eye, np.cos(cang)).astype(np.float32)
    sc = np.kron(eye, np.sin(cang)).astype(np.float32)
    return cs, cc, sc


def _fourier(proj, row0, b, t, tm, name):
    cs, cc, sc = _dft_tables(t)
    cw = FOURIER_WIDTH // FOURIER_GROUPS
    nt = t // tm
    ub = COL_U // FOURIER_WIDTH
    base = row0 // t
    body = functools.partial(_fourier_kernel, scale=1.0 / math.sqrt(t * cw))
    return pl.pallas_call(
        body,
        grid=(nt, b),
        in_specs=[
            pl.BlockSpec((tm, 2 * t), lambda i, bb: (i, 0)),
            pl.BlockSpec((t, FOURIER_WIDTH), lambda i, bb: (base + bb, ub)),
            pl.BlockSpec((FOURIER_WIDTH, FOURIER_WIDTH), lambda i, bb: (0, 0)),
            pl.BlockSpec((FOURIER_WIDTH, FOURIER_WIDTH), lambda i, bb: (0, 0)),
        ],
        out_specs=pl.BlockSpec((tm, FOURIER_WIDTH), lambda i, bb: (bb * nt + i, 0)),
        out_shape=jax.ShapeDtypeStruct((b * t, FOURIER_WIDTH), BF16),
        scratch_shapes=[pltpu.VMEM((tm, 2 * t), BF16)],
        compiler_params=_cparams(("arbitrary", "arbitrary")),
        name=name,
    )(jnp.asarray(cs), proj, jnp.asarray(cc), jnp.asarray(sc))


HGRN_LEVELS = (64, 32, 16, 8, 4, 2)
HGRN_SAFE_RANGE = 80.0


@functools.lru_cache(maxsize=None)
def _hgrn_tables():
    c = HGRN_CHUNK
    return np.stack([np.tril(np.ones((c, c))), np.triu(np.ones((c, c)))]).astype(np.float32)


def _boundary_rows(b, m, reverse):
    c, w = b.shape
    half = m // 2
    off = half if reverse else half - 1
    if m >= 16:
        return jnp.concatenate(
            [jnp.broadcast_to(b[s + off:s + off + 1], (m, w)) for s in range(0, c, m)], axis=0)
    sub = lax.broadcasted_iota(I32, (c, w), 0) & 7
    b3 = b.reshape(c // 8, 8, w)

    def bcast(j):
        return jnp.broadcast_to(b3[:, j:j + 1, :], (c // 8, 8, w)).reshape(c, w)

    if m == 8:
        return bcast(off)
    if m == 4:
        return jnp.where(sub < 4, bcast(off), bcast(4 + off))
    assert m == 2
    if reverse:
        return jnp.where((sub & 1) == 1, b, pltpu.roll(b, c - 1, 0))
    return jnp.where((sub & 1) == 0, b, pltpu.roll(b, 1, 0))


def _hgrn_gates(q, z, v, loglb, log1mlb, onemlb, cum, reverse):
    c = HGRN_CHUNK
    log_sig = jnp.minimum(z, 0.0) - jnp.log1p(jnp.exp(-jnp.abs(z)))
    bb = log1mlb + log_sig
    mx = jnp.maximum(loglb, bb)
    lf = mx + jnp.log1p(jnp.exp(-jnp.abs(loglb - bb)))
    kk = onemlb * jax.nn.sigmoid(-z)
    b = _dot_exact_lhs(cum, lf)
    b_end = b[0:1] if reverse else b[c - 1:c]
    qt = (q * jnp.exp(b)).astype(BF16)
    kt = (kk * jnp.exp(b_end - b)).astype(BF16)
    return (q, kk, b), (qt, kt, v.astype(BF16))


def _head_stack(x):
    lane = lax.broadcasted_iota(I32, x.shape, 1)
    zero = jnp.zeros_like(x)
    return jnp.concatenate([jnp.where((lane >= h * HGRN_DK) & (lane < (h + 1) * HGRN_DK), x, zero)
                            for h in range(HGRN_HEADS)], axis=0)


class _HgrnDir:
    def __init__(self, f32_parts, bf16_parts, reverse):
        c = HGRN_CHUNK
        self.q, self.kk, self.b = f32_parts
        self.qt, self.kt, self.vb = bf16_parts
        self.reverse = reverse
        b_end = self.b[0:1] if reverse else self.b[c - 1:c]
        self.decay = jnp.exp(b_end)
        mid = c // 2 if reverse else c // 2 - 1
        self.rel = self.b - self.b[mid:mid + 1]
        self.span = jnp.max(jnp.abs(self.rel))

    def tree_decay_matrices(self):
        c = HGRN_CHUNK
        q, kk, b = self.q, self.kk, self.b
        row = lax.broadcasted_iota(I32, (c, 1), 0)
        ti = lax.broadcasted_iota(I32, (c, c), 0)
        si = lax.broadcasted_iota(I32, (c, c), 1)
        qb = q.astype(BF16)
        kb = kk.astype(BF16)
        heads = [slice(h * HGRN_DK, (h + 1) * HGRN_DK) for h in range(HGRN_HEADS)]
        acc = [jnp.where(ti == si, _dot_nt(qb[:, sl], kb[:, sl]), 0.0) for sl in heads]
        for m in HGRN_LEVELS:
            r = _boundary_rows(b, m, self.reverse)
            upper = (row & (m - 1)) >= (m // 2)
            q_side = jnp.logical_not(upper) if self.reverse else upper
            e = jnp.exp(jnp.where(q_side, b - r, r - b))
            qf = jnp.where(q_side, q * e, 0.0).astype(BF16)
            kf = jnp.where(q_side, 0.0, kk * e).astype(BF16)
            same_block = (ti & -m) == (si & -m)
            for h, sl in enumerate(heads):
                acc[h] = acc[h] + jnp.where(same_block, _dot_nt(qf[:, sl], kf[:, sl]), 0.0)
        return jnp.concatenate(acc, axis=1)

    def midpoint_decay_matrices(self):
        c = HGRN_CHUNK
        ti = lax.broadcasted_iota(I32, (c, HGRN_HEADS * c), 0)
        si = lax.broadcasted_iota(I32, (c, HGRN_HEADS * c), 1) & (c - 1)
        qm = (self.q * jnp.exp(self.rel)).astype(BF16)
        km = (self.kk * jnp.exp(-self.rel)).astype(BF16)
        causal = (si >= ti) if self.reverse else (si <= ti)
        return jnp.where(causal, _dot_nt(qm, _head_stack(km)), 0.0)

    def outputs(self, a_cat, st_ref, same_head, d):
        bd = st_ref[d]
        o = _dot_nt(self.qt, bd.astype(BF16)) + _dot(a_cat.astype(BF16), _head_stack(self.vb))
        st_ref[d] = jnp.where(same_head, bd * self.decay + _dot_tn(self.vb, self.kt), 0.0)
        return o


def _hgrn_kernel(hq_ref, ff_ref, fb_ref, hi_ref, hg_ref, lbp_ref, gn_ref, mall_ref, ones_ref, s0_ref, *refs, t,
                 layer):
    c = HGRN_CHUNK
    n = t // c
    rec_ref, sfin_ref, st_ref, of_ref, ob_ref, gf_ref, gb_ref = refs[-7:]
    blocks = [slice(h * HGRN_DK, (h + 1) * HGRN_DK) for h in range(HGRN_HEADS)]
    zero_blk = jnp.zeros((HGRN_DK, HGRN_DK), F32)

    def init_state():
        for d in range(2):
            st_ref[d] = jnp.concatenate(
                [jnp.concatenate([s0_ref[0, d, h] if j == h else zero_blk for j in range(HGRN_HEADS)], axis=1)
                 for h in range(HGRN_HEADS)], axis=0)

    def chunk_rows(ci):
        return pl.ds(pl.multiple_of(ci * c, c), c), pl.ds(pl.multiple_of((n - 1 - ci) * c, c), c)

    def gates_to(slot, ci):
        rf, rb = chunk_rows(ci)
        for d, (rows, f_ref) in enumerate(((rf, ff_ref), (rb, fb_ref))):
            f32_parts, bf16_parts = _hgrn_gates(
                hq_ref[rows, :], f_ref[rows, :], hi_ref[rows, :], lbp_ref[d, 0:1, :], lbp_ref[d, 1:2, :],
                lbp_ref[d, 2:3, :], mall_ref[d].astype(BF16), d == 1)
            for j in range(3):
                gf_ref[slot, d, j] = f32_parts[j]
                gb_ref[slot, d, j] = bf16_parts[j]

    def run(decay_matrices):
        init_state()
        gates_to(0, 0)

        def body(ci, widest):
            slot = ci & 1
            rf, rb = chunk_rows(ci)
            fwd = _HgrnDir([gf_ref[slot, 0, j] for j in range(3)], [gb_ref[slot, 0, j] for j in range(3)], False)
            bwd = _HgrnDir([gf_ref[slot, 1, j] for j in range(3)], [gb_ref[slot, 1, j] for j in range(3)], True)
            a_f, a_b = decay_matrices(fwd), decay_matrices(bwd)
            gates_to(1 - slot, jnp.minimum(ci + 1, n - 1))
            same_head = ones_ref[...] != 0.0
            of_ref[rf, :] = fwd.outputs(a_f, st_ref, same_head, 0)
            ob_ref[rb, :] = bwd.outputs(a_b, st_ref, same_head, 1)
            return jnp.maximum(widest, jnp.maximum(fwd.span, bwd.span))

        return lax.fori_loop(0, n, body, jnp.float32(0.0), unroll=4)

    widest = run(_HgrnDir.midpoint_decay_matrices)

    @pl.when(widest > HGRN_SAFE_RANGE)
    def _():
        run(_HgrnDir.tree_decay_matrices)

    slot = layer if sfin_ref.shape[1] > 1 else 0
    for j in range(sfin_ref.shape[1]):
        if j != slot:
            sfin_ref[0, j] = jnp.zeros(sfin_ref.shape[2:], F32)
    for d in range(2):
        bd = st_ref[d].T
        for h, sl in enumerate(blocks):
            sfin_ref[0, slot, d, h] = bd[sl, sl]
    o = of_ref[...] + ob_ref[...]
    ms = _dot_exact_rhs(o * o, ones_ref[...].astype(BF16)) * (1.0 / HGRN_DK)
    o = o * lax.rsqrt(ms + GN_EPS) * gn_ref[...]
    rec_ref[...] = (o * _silu(hg_ref[...])).astype(rec_ref.dtype)


def _hgrn(proj, row0, b, t, lbp, gn_row, s0t, name, layer=0, depth=1, states=None):
    base = row0 // t
    m_all = jnp.asarray(_hgrn_tables())
    ones_bd = jnp.asarray(np.kron(np.eye(HGRN_HEADS), np.ones((HGRN_DK, HGRN_DK))).astype(np.float32))

    def col(cstart):
        return pl.BlockSpec((t, HGRN_WIDTH), lambda bb, cb=(cstart - MIX_IN_WIDTH) // HGRN_WIDTH: (base + bb, cb))

    const2 = lambda bb: (0, 0)
    const3 = lambda bb: (0, 0, 0)
    st_shape = (2, HGRN_HEADS, HGRN_DK, HGRN_DK)
    body = functools.partial(_hgrn_kernel, t=t, layer=layer)
    first = states is None
    operands = [proj, proj, proj, proj, proj, lbp, gn_row, m_all, ones_bd, s0t]
    in_specs = [
        col(COL_HQ), col(COL_FF), col(COL_FB), col(COL_HI), col(COL_HG),
        pl.BlockSpec((2, 3, HGRN_WIDTH), const3),
        pl.BlockSpec((1, HGRN_WIDTH), const2),
        pl.BlockSpec(m_all.shape, const3),
        pl.BlockSpec(ones_bd.shape, const2),
        pl.BlockSpec((1,) + st_shape, lambda bb: (bb, 0, 0, 0, 0)),
    ]
    aliases = {}
    if states is not None:
        aliases = {len(operands): 1}
        operands.append(states)
        in_specs.append(pl.BlockSpec(memory_space=pl.ANY))
    return pl.pallas_call(
        body,
        grid=(b,),
        in_specs=in_specs,
        out_specs=[
            pl.BlockSpec((t, HGRN_WIDTH), lambda bb: (bb, 0)),
            pl.BlockSpec((1, depth if first else 1) + st_shape, lambda bb: (bb, 0 if first else layer, 0, 0, 0, 0)),
        ],
        out_shape=[
            jax.ShapeDtypeStruct((b * t, HGRN_WIDTH), BF16),
            jax.ShapeDtypeStruct((b, depth) + st_shape, F32),
        ],
        input_output_aliases=aliases,
        scratch_shapes=[
            pltpu.VMEM((2, HGRN_WIDTH, HGRN_WIDTH), F32),
            pltpu.VMEM((t, HGRN_WIDTH), F32),
            pltpu.VMEM((t, HGRN_WIDTH), F32),
            pltpu.VMEM((2, 2, 3, HGRN_CHUNK, HGRN_WIDTH), F32),
            pltpu.VMEM((2, 2, 3, HGRN_CHUNK, HGRN_WIDTH), BF16),
        ],
        compiler_params=_cparams(("arbitrary",)),
        name=name,
    )(*operands)


def _outproj_kernel(*refs, n_ctx_tiles, n_mix):
    mix = refs[:3 * n_mix]
    refs = refs[3 * n_mix:]
    xs = refs[:-14]
    (mod_ref, w_ref, g_ref, b_ref, rw_ref, rb_ref, x1_ref, hp_ref, meta_ref, gate_ref, cnt_ref, wb_ref, tri_ref,
     run_ref) = refs[-14:]
    tm = x1_ref.shape[0]
    is_ctx = pl.program_id(0) < n_ctx_tiles
    x_in = jnp.where(is_ctx, xs[0][...], xs[1][...]) if len(xs) == 2 else xs[0][...]
    if n_mix == 2:
        attn, four, rec = [jnp.where(is_ctx, mix[2 * j][...], mix[2 * j + 1][...]) for j in range(3)]
    else:
        attn, four, rec = [r[...] for r in mix]

    @pl.when(pl.program_id(0) == 0)
    def _():
        wb_ref[...] = w_ref[0].astype(BF16)
        r = lax.broadcasted_iota(I32, (tm, tm), 0)
        c = lax.broadcasted_iota(I32, (tm, tm), 1)
        tri_ref[...] = jnp.where(r < c, 1.0, 0.0).astype(BF16)
        run_ref[...] = jnp.zeros_like(run_ref)

    out = _dot(attn, wb_ref[0:ATTN_WIDTH, :])
    out = out + _dot(four, wb_ref[ATTN_WIDTH:ATTN_WIDTH + FOURIER_WIDTH, :])
    out = out + _dot(rec, wb_ref[ATTN_WIDTH + FOURIER_WIDTH:, :])
    gate1 = mod_ref[0, 2:3, :]
    y = DEEPNORM_ALPHA * x_in + gate1 * out
    x1 = _ln_plain(y, LN_EPS) * g_ref[...] + b_ref[...]
    x1_ref[...] = x1
    h2 = _ln_plain(x1, ADA_EPS) * (1.0 + mod_ref[0, 4:5, :]) + mod_ref[0, 3:4, :]
    hp_ref[...] = _pack_bf16_pair(h2[:, :HALF_D], h2[:, HALF_D:])

    h_hi = h2.astype(BF16)
    h_lo = (h2 - h_hi.astype(F32)).astype(BF16)
    rwt = rw_ref[...]
    w_hi = rwt.astype(BF16)
    w_lo = (rwt - w_hi.astype(F32)).astype(BF16)
    scores = jax.nn.sigmoid(_dot_nt(w_hi, h_hi) + _dot_nt(w_hi, h_lo) + _dot_nt(w_lo, h_hi))
    remaining = scores + rb_ref[...]
    eidx = lax.broadcasted_iota(I32, scores.shape, 0).astype(F32)
    chosen = jnp.zeros(scores.shape, jnp.bool_)
    picks = []
    for _ in range(TOP_K):
        mx = jnp.max(remaining, axis=0, keepdims=True)
        first = jnp.min(jnp.where(remaining == mx, eidx, float(N_EXPERTS)), axis=0, keepdims=True)
        pick = eidx == first
        picks.append((pick, first))
        chosen = jnp.logical_or(chosen, pick)
        remaining = jnp.where(pick, -jnp.inf, remaining)
    sel = jnp.where(chosen, scores, 0.0)
    gates = sel / jnp.sum(sel, axis=0, keepdims=True) * ROUTED_SCALE

    onehot = jnp.where(chosen, 1.0, 0.0)
    rank = run_ref[...] + _dot(onehot.astype(BF16), tri_ref[...])
    run_ref[...] += jnp.sum(onehot, axis=1, keepdims=True)
    cnt_ref[...] = run_ref[...]

    ids, rks, gks = [], [], []
    for pick, first in picks:
        ids.append(first.astype(I32))
        rks.append(jnp.sum(jnp.where(pick, rank, 0.0), axis=0, keepdims=True).astype(I32))
        gks.append(jnp.sum(jnp.where(pick, gates, 0.0), axis=0, keepdims=True))
    meta_ref[...] = jnp.concatenate(ids + rks, axis=0)
    gate_ref[...] = jnp.concatenate(gks, axis=0)


def _outproj(attn, four, rec, x, mod_l, w_out, layer, g1, b1, rw, rb, lay, tm):
    n_tiles = lay.n // tm
    n_ctx_tiles = lay.n_ctx // tm
    row = lambda i: (i, 0)
    const = lambda i: (0, 0)
    xs = x if isinstance(x, tuple) else (x,)
    return pl.pallas_call(
        functools.partial(_outproj_kernel, n_ctx_tiles=n_ctx_tiles, n_mix=len(attn)),
        grid=(n_tiles,),
        in_specs=[
            *_group_specs(len(attn), tm, ATTN_WIDTH, n_ctx_tiles),
            *_group_specs(len(four), tm, FOURIER_WIDTH, n_ctx_tiles),
            *_group_specs(len(rec), tm, HGRN_WIDTH, n_ctx_tiles),
            *_group_specs(len(xs), tm, D_MODEL, n_ctx_tiles),
            pl.BlockSpec((1, N_MOD, D_MODEL), lambda i: (lay.cond_row(i, tm), 0, 0)),
            pl.BlockSpec((1, D_MODEL, D_MODEL), lambda i: (layer, 0, 0)),
            pl.BlockSpec((1, D_MODEL), const),
            pl.BlockSpec((1, D_MODEL), const),
            pl.BlockSpec((N_EXPERTS, D_MODEL), const),
            pl.BlockSpec((N_EXPERTS, 1), const),
        ],
        out_specs=[
            pl.BlockSpec((tm, D_MODEL), row),
            pl.BlockSpec((tm, HALF_D), row),
            pl.BlockSpec((2 * TOP_K, tm), lambda i: (0, i)),
            pl.BlockSpec((TOP_K, tm), lambda i: (0, i)),
            pl.BlockSpec((N_EXPERTS, 1), const),
        ],
        out_shape=[
            jax.ShapeDtypeStruct((lay.n, D_MODEL), F32),
            jax.ShapeDtypeStruct((lay.n, HALF_D), I32),
            jax.ShapeDtypeStruct((2 * TOP_K, lay.n), I32),
            jax.ShapeDtypeStruct((TOP_K, lay.n), F32),
            jax.ShapeDtypeStruct((N_EXPERTS, 1), F32),
        ],
        scratch_shapes=[
            pltpu.VMEM((D_MODEL, D_MODEL), BF16),
            pltpu.VMEM((tm, tm), BF16),
            pltpu.VMEM((N_EXPERTS, 1), F32),
        ],
        compiler_params=_cparams(("arbitrary",)),
        name="outproj_router" + lay.tag,
    )(*attn, *four, *rec, *xs, mod_l, w_out, g1, b1, rw, rb)


def _sc_workers():
    info = plsc.get_sparse_core_info()
    return info.num_cores, info.num_cores * info.num_subcores


def _sc_scatter_rows(rows, pos_b, r_out, tag=""):
    nc, nw = _sc_workers()
    n, w = rows.shape
    nbt, copies, _ = pos_b.shape
    assert nbt * SC_BATCH == n and nbt % (2 * nw) == 0
    per_w = nbt // nw
    mesh = plsc.VectorSubcoreMesh(core_axis_name="c", subcore_axis_name="s")

    @functools.partial(
        pl.kernel, mesh=mesh, out_type=jax.ShapeDtypeStruct((r_out, w), rows.dtype),
        scratch_types=[pltpu.VMEM((copies, SC_BATCH), I32), pltpu.VMEM((copies, SC_BATCH), I32),
                       pltpu.VMEM((SC_BATCH, w), rows.dtype), pltpu.VMEM((SC_BATCH, w), rows.dtype),
                       pltpu.SemaphoreType.DMA, pltpu.SemaphoreType.DMA,
                       pltpu.SemaphoreType.DMA, pltpu.SemaphoreType.DMA],
        name="sc_dispatch" + tag)
    def k(rows_hbm, pos_hbm, out_hbm, idx_a, idx_b, rows_a, rows_b, sem_ra, sem_rb, sem_sa, sem_sb):
        wid = lax.axis_index("s") * nc + lax.axis_index("c")
        first = wid * per_w

        def reads(j, idx_v, rows_v, sem):
            bt = first + j
            return (pltpu.make_async_copy(pos_hbm.at[bt], idx_v, sem),
                    pltpu.make_async_copy(rows_hbm.at[pl.ds(bt * SC_BATCH, SC_BATCH)], rows_v, sem))

        def scatters(idx_v, rows_v, sem):
            return [pltpu.make_async_copy(rows_v, out_hbm.at[idx_v.at[q]], sem) for q in range(copies)]

        def start(descs):
            for d in descs:
                d.start()

        def wait(descs):
            for d in descs:
                d.wait()

        start(reads(0, idx_a, rows_a, sem_ra))

        @pl.loop(0, per_w // 2)
        def _(p):
            j0 = 2 * p
            j1 = j0 + 1

            @pl.when(p > 0)
            def _():
                wait(scatters(idx_b, rows_b, sem_sb))

            start(reads(j1, idx_b, rows_b, sem_rb))
            wait(reads(j0, idx_a, rows_a, sem_ra))
            start(scatters(idx_a, rows_a, sem_sa))
            wait(reads(j1, idx_b, rows_b, sem_rb))
            start(scatters(idx_b, rows_b, sem_sb))
            wait(scatters(idx_a, rows_a, sem_sa))

            @pl.when(p + 1 < per_w // 2)
            def _():
                start(reads(j0 + 2, idx_a, rows_a, sem_ra))

        wait(scatters(idx_b, rows_b, sem_sb))

    return k(rows, pos_b)


def _sc_gather_rows(table, idx, tag=""):
    nc, nw = _sc_workers()
    r = idx.shape[0]
    w = table.shape[1]
    assert r % (2 * nw * SC_BATCH) == 0
    per_w = r // nw
    nb = per_w // SC_BATCH
    mesh = plsc.VectorSubcoreMesh(core_axis_name="c", subcore_axis_name="s")

    @functools.partial(
        pl.kernel, mesh=mesh, out_type=jax.ShapeDtypeStruct((r, w), table.dtype),
        scratch_types=[pltpu.VMEM((per_w,), I32),
                       pltpu.VMEM((SC_BATCH, w), table.dtype), pltpu.VMEM((SC_BATCH, w), table.dtype),
                       pltpu.SemaphoreType.DMA, pltpu.SemaphoreType.DMA,
                       pltpu.SemaphoreType.DMA, pltpu.SemaphoreType.DMA],
        name="sc_combine" + tag)
    def k(table_hbm, idx_hbm, out_hbm, idx_v, rows_a, rows_b, sem_ga, sem_gb, sem_wa, sem_wb):
        wid = lax.axis_index("s") * nc + lax.axis_index("c")
        base = wid * per_w
        pltpu.sync_copy(idx_hbm.at[pl.ds(base, per_w)], idx_v)

        def gather(j, rows_v, sem):
            return pltpu.make_async_copy(table_hbm.at[idx_v.at[pl.ds(j * SC_BATCH, SC_BATCH)]], rows_v, sem)

        def write(j, rows_v, sem):
            return pltpu.make_async_copy(rows_v, out_hbm.at[pl.ds(base + j * SC_BATCH, SC_BATCH)], sem)

        gather(0, rows_a, sem_ga).start()

        @pl.loop(0, nb // 2)
        def _(p):
            j0 = 2 * p
            j1 = j0 + 1

            @pl.when(p > 0)
            def _():
                write(j1 - 2, rows_b, sem_wb).wait()

            gather(j1, rows_b, sem_gb).start()
            gather(j0, rows_a, sem_ga).wait()
            write(j0, rows_a, sem_wa).start()
            gather(j1, rows_b, sem_gb).wait()
            write(j1, rows_b, sem_wb).start()
            write(j0, rows_a, sem_wa).wait()

            @pl.when(p + 1 < nb // 2)
            def _():
                gather(j0 + 2, rows_a, sem_ga).start()

        write(nb - 1, rows_b, sem_wb).wait()

    return k(table, idx)


def _experts_kernel(te_ref, na_ref, x_ref, w1_ref, w3_ref, w2_ref, o_ref, w1b_ref, w3b_ref, w2b_ref):
    del te_ref

    @pl.when(pl.program_id(0) < na_ref[0])
    def _():
        w1b_ref[...] = w1_ref[0, 0].astype(BF16)
        w3b_ref[...] = w3_ref[0, 0].astype(BF16)
        w2b_ref[...] = w2_ref[0, 0].astype(BF16)
        lo, hi = _unpack_bf16_pair(x_ref[...])
        lo = lo.astype(BF16)
        hi = hi.astype(BF16)
        a = _dot(lo, w1b_ref[0:HALF_D, :]) + _dot(hi, w1b_ref[HALF_D:, :])
        b = _dot(lo, w3b_ref[0:HALF_D, :]) + _dot(hi, w3b_ref[HALF_D:, :])
        y = _dot((_silu(a) * b).astype(BF16), w2b_ref[...])
        o_ref[...] = _pack_bf16_pair(y[:, :HALF_D], y[:, HALF_D:])


def _experts(xs, tile_expert, n_active, w1, w3, w2, layer, tm, tag):
    r = xs.shape[0]
    n_tiles = r // tm

    def xmap(j, te, na):
        return (jnp.minimum(j, na[0] - 1), 0)

    def wmap(j, te, na):
        return (layer, te[jnp.minimum(j, na[0] - 1)], 0, 0)

    grid_spec = pltpu.PrefetchScalarGridSpec(
        num_scalar_prefetch=2,
        grid=(n_tiles,),
        in_specs=[
            pl.BlockSpec((tm, HALF_D), xmap),
            pl.BlockSpec((1, 1, D_MODEL, EXPERT_FF), wmap),
            pl.BlockSpec((1, 1, D_MODEL, EXPERT_FF), wmap),
            pl.BlockSpec((1, 1, EXPERT_FF, D_MODEL), wmap),
        ],
        out_specs=pl.BlockSpec((tm, HALF_D), xmap),
        scratch_shapes=[
            pltpu.VMEM((D_MODEL, EXPERT_FF), BF16),
            pltpu.VMEM((D_MODEL, EXPERT_FF), BF16),
            pltpu.VMEM((EXPERT_FF, D_MODEL), BF16),
        ],
    )
    return pl.pallas_call(
        _experts_kernel,
        grid_spec=grid_spec,
        out_shape=jax.ShapeDtypeStruct((r, HALF_D), I32),
        compiler_params=_cparams(("arbitrary",)),
        name="experts" + tag,
    )(tile_expert, n_active, xs, w1, w3, w2)


def _combine_kernel(yp_ref, gate_ref, hp_ref, sw1_ref, sw3_ref, sw2_ref, x_ref, mod_ref, g_ref, b_ref, *refs,
                    n_ctx_tiles):
    outs = refs[:-3]
    w1b_ref, w3b_ref, w2b_ref = refs[-3:]

    @pl.when(pl.program_id(0) == 0)
    def _():
        w1b_ref[...] = sw1_ref[...].astype(BF16)
        w3b_ref[...] = sw3_ref[...].astype(BF16)
        w2b_ref[...] = sw2_ref[...].astype(BF16)

    lo, hi = _unpack_bf16_pair(hp_ref[...])
    lo = lo.astype(BF16)
    hi = hi.astype(BF16)
    a = _dot(lo, w1b_ref[0:HALF_D, :]) + _dot(hi, w1b_ref[HALF_D:, :])
    b = _dot(lo, w3b_ref[0:HALF_D, :]) + _dot(hi, w3b_ref[HALF_D:, :])
    shared = _dot((_silu(a) * b).astype(BF16), w2b_ref[...])
    acc_lo = shared[:, :HALF_D]
    acc_hi = shared[:, HALF_D:]
    gates = gate_ref[...].T
    for k in range(TOP_K):
        ylo, yhi = _unpack_bf16_pair(yp_ref[k])
        gk = gates[:, k:k + 1]
        acc_lo = acc_lo + gk * ylo
        acc_hi = acc_hi + gk * yhi
    moe = jnp.concatenate([acc_lo, acc_hi], axis=1)
    y = DEEPNORM_ALPHA * x_ref[...] + mod_ref[0, 5:6, :] * moe
    res = _ln_plain(y, LN_EPS) * g_ref[...] + b_ref[...]
    if len(outs) == 1:
        outs[0][...] = res
    else:
        @pl.when(pl.program_id(0) < n_ctx_tiles)
        def _():
            outs[0][...] = res

        @pl.when(pl.program_id(0) >= n_ctx_tiles)
        def _():
            outs[1][...] = res


def _combine(yp, gate8, hp, sw1, sw3, sw2, x1, mod_l, g2, b2, lay, tm, split_out):
    n_tiles = lay.n // tm
    n_ctx_tiles = lay.n_ctx // tm
    row = lambda i: (i, 0)
    const = lambda i: (0, 0)
    if split_out:
        out_specs = _group_specs(2, tm, D_MODEL, n_ctx_tiles)
        out_shape = [jax.ShapeDtypeStruct((lay.n_ctx, D_MODEL), F32), jax.ShapeDtypeStruct((lay.n_lat, D_MODEL), F32)]
    else:
        out_specs = pl.BlockSpec((tm, D_MODEL), row)
        out_shape = jax.ShapeDtypeStruct((lay.n, D_MODEL), F32)
    return pl.pallas_call(
        functools.partial(_combine_kernel, n_ctx_tiles=n_ctx_tiles),
        grid=(n_tiles,),
        in_specs=[
            pl.BlockSpec((TOP_K, tm, HALF_D), lambda i: (0, i, 0)),
            pl.BlockSpec((TOP_K, tm), lambda i: (0, i)),
            pl.BlockSpec((tm, HALF_D), row),
            pl.BlockSpec((D_MODEL, EXPERT_FF), const),
            pl.BlockSpec((D_MODEL, EXPERT_FF), const),
            pl.BlockSpec((EXPERT_FF, D_MODEL), const),
            pl.BlockSpec((tm, D_MODEL), row),
            pl.BlockSpec((1, N_MOD, D_MODEL), lambda i: (lay.cond_row(i, tm), 0, 0)),
            pl.BlockSpec((1, D_MODEL), const),
            pl.BlockSpec((1, D_MODEL), const),
        ],
        out_specs=out_specs,
        out_shape=out_shape,
        scratch_shapes=[
            pltpu.VMEM((D_MODEL, EXPERT_FF), BF16),
            pltpu.VMEM((D_MODEL, EXPERT_FF), BF16),
            pltpu.VMEM((EXPERT_FF, D_MODEL), BF16),
        ],
        compiler_params=_cparams(("arbitrary",)),
        name="combine_norm" + lay.tag,
    )(yp, gate8, hp, sw1, sw3, sw2, x1, mod_l, g2, b2)


def _moe_dispatch(hp, meta, counts, lay, tile):
    n = lay.n
    r_max = n * TOP_K + N_EXPERTS * tile
    n_tiles = r_max // tile
    cnt = counts.reshape(N_EXPERTS).astype(I32)
    padded = ((cnt + tile - 1) // tile) * tile
    ends = jnp.cumsum(padded)
    offsets = ends - padded
    idx8 = meta[:TOP_K]
    base8 = jnp.sum(jnp.where(idx8[:, :, None] == jnp.arange(N_EXPERTS, dtype=I32), offsets, 0), axis=-1)
    pos = (base8 + meta[TOP_K:]).astype(I32)
    tile_start = jnp.arange(n_tiles, dtype=I32) * tile
    tile_expert = jnp.minimum(jnp.sum(tile_start[:, None] >= ends[None, :], axis=1), N_EXPERTS - 1).astype(I32)
    n_active = (ends[-1] // tile).astype(I32).reshape(1)
    pos_b = pos.reshape(TOP_K, n // SC_BATCH, SC_BATCH).transpose(1, 0, 2)
    xs = _sc_scatter_rows(hp, pos_b, r_max, lay.tag)
    return xs, tile_expert, n_active, pos


def _moe_combine(ys, pos, gate8, hp, sw1, sw3, sw2, x1, mod_l, g2, b2, lay, split_out):
    n = lay.n
    yp = _sc_gather_rows(ys, pos.reshape(n * TOP_K), lay.tag).reshape(TOP_K, n, HALF_D)
    return _combine(yp, gate8, hp, sw1, sw3, sw2, x1, mod_l, g2, b2, lay, TOKEN_TILE, split_out)


def kernel(x_prompt, x_sample, cache_k, cache_v, state_hgrn, c, c_ctx, w_ada, b_ada, w_in, w_out, attn_sink, hgrn_lb, hgrn_norm, ln1_g, ln1_b, ln2_g, ln2_b, router_w, router_b, moe_w1, moe_w3, moe_w2, shared_w1, shared_w3, shared_w2):
    b_ctx, t_ctx, _ = x_prompt.shape
    b_lat, t_lat, _ = x_sample.shape
    past = cache_k.shape[2]
    tm = TOKEN_TILE
    assert 1 + b_lat <= COND_ROWS
    lay = _Layout(b_ctx, t_ctx, b_lat, t_lat)
    assert lay.n_ctx % tm == 0 and t_lat % tm == 0 and lay.n_ctx % t_lat == 0

    cond = jnp.concatenate([c_ctx[None, :], c, jnp.zeros((COND_ROWS - 1 - b_lat, D_MODEL), F32)], axis=0)
    mod = _adaln(cond, w_ada, b_ada).reshape(DEPTH, COND_ROWS, N_MOD, D_MODEL)

    lb_all = jnp.cumsum(jax.nn.softmax(hgrn_lb.astype(F32), axis=0), axis=0)
    lb_all = lb_all - lb_all[:1]
    lbp = jnp.stack([jnp.log(lb_all), jnp.log1p(-lb_all), 1.0 - lb_all], axis=2)

    cos_t, sin_t = _rope_tables(lay, tm)
    zero_state = jnp.zeros((b_ctx, 2, HGRN_HEADS, HGRN_DK, HGRN_DK), F32)

    def layer(l, lay, x, split_out, states):
        lat = slice(lay.lat_first, lay.lat_first + lay.b_lat)
        outs = _inproj(x, mod[l], w_in, l, cos_t, sin_t, lay, tm)
        proj, proj_h = outs[0], outs[1]
        sink_l = attn_sink[l].reshape(1, N_HEADS)
        gn_row = jnp.tile(hgrn_norm[l], HGRN_HEADS).reshape(1, HGRN_WIDTH)
        attn, four, rec, extras = [], [], [], None
        if lay.b_ctx:
            attn.append(_attn_context(proj, sink_l, lay))
            four.append(_fourier(proj, 0, lay.b_ctx, t_ctx, t_ctx, "fourier_ctx"))
            rec_c, states = _hgrn(proj_h, 0, lay.b_ctx, t_ctx, lbp[l], gn_row, zero_state, "hgrn_ctx", l, DEPTH,
                                  states)
            rec.append(rec_c)
            extras = (outs[2], outs[3], states)
        attn.append(_attn_latent(proj, cache_k[lat, l].reshape(lay.b_lat, past, KV_WIDTH),
                                 cache_v[lat, l].reshape(lay.b_lat, past, KV_WIDTH), sink_l, lay))
        four.append(_fourier(proj, lay.n_ctx, lay.b_lat, t_lat, min(t_lat, 512), "fourier_lat" + lay.tag))
        s0t = jnp.swapaxes(state_hgrn[lat, l].astype(F32), -1, -2)
        rec.append(_hgrn(proj_h, lay.n_ctx, lay.b_lat, t_lat, lbp[l], gn_row, s0t, "hgrn_lat" + lay.tag)[0])
        x1, hp, meta, gate8, counts = _outproj(
            tuple(attn), tuple(four), tuple(rec), x, mod[l], w_out, l, ln1_g[l].reshape(1, -1),
            ln1_b[l].reshape(1, -1), router_w[l].T, router_b[l].reshape(-1, 1), lay, tm)
        xs, tile_expert, n_active, pos = _moe_dispatch(hp, meta, counts, lay, EXPERT_TILE)
        ys = _experts(xs, tile_expert, n_active, moe_w1, moe_w3, moe_w2, l, EXPERT_TILE, lay.tag)
        x = _moe_combine(ys, pos, gate8, hp, shared_w1[l], shared_w3[l], shared_w2[l], x1, mod[l],
                         ln2_g[l].reshape(1, -1), ln2_b[l].reshape(1, -1), lay, split_out)
        return x, extras

    x = (x_prompt.reshape(lay.n_ctx, D_MODEL), x_sample.reshape(lay.n_lat, D_MODEL))
    ks_out, vs_out, states = [], [], None
    for l in range(DEPTH):
        x, (k_new, v_new, states) = layer(l, lay, x, l == DEPTH - 1, states)
        ks_out.append(k_new.reshape(b_ctx, t_ctx, N_KV_HEADS, HEAD_DIM))
        vs_out.append(v_new.reshape(b_ctx, t_ctx, N_KV_HEADS, HEAD_DIM))

    y_prompt = x[0].reshape(b_ctx, t_ctx, D_MODEL)
    y_sample = x[1].reshape(b_lat, t_lat, D_MODEL)
    new_cache_k = jnp.stack(ks_out, axis=1)
    new_cache_v = jnp.stack(vs_out, axis=1)
    new_state = states.astype(x_prompt.dtype)
    return (y_prompt, y_sample, new_cache_k, new_cache_v, new_state)
```

```python
import functools
import math

import numpy as np
import jax
import jax.numpy as jnp
from jax import lax
from jax.experimental import pallas as pl
from jax.experimental.pallas import tpu as pltpu
from jax.experimental.pallas import tpu_sc as plsc

F32 = jnp.float32
BF16 = jnp.bfloat16
I32 = jnp.int32

D_MODEL = 1024
HALF_D = D_MODEL // 2
DEPTH = 2
GRID_W = 64
ROPE_BASE = 10000.0
HEAD_DIM = 64
ATTN_WIDTH = 512
N_HEADS = 8
N_KV_HEADS = 2
KV_GROUP = 4
KV_WIDTH = N_KV_HEADS * HEAD_DIM
WINDOW = 128
ATTN_BLOCK = 128
FOURIER_WIDTH = 256
FOURIER_GROUPS = 4
HGRN_WIDTH = 256
HGRN_HEADS = 4
HGRN_DK = 64
HGRN_CHUNK = 64
IN_WIDTH = 2304
N_EXPERTS = 64
TOP_K = 8
EXPERT_FF = 256
ROUTED_SCALE = 2.5
N_MOD = 6
LN_EPS = 1e-5
ADA_EPS = 1e-6
GN_EPS = 1e-6
DEEPNORM_ALPHA = (2 * DEPTH) ** 0.25

COL_Q = 0
COL_K = 512
COL_V = 640
COL_U = 768
COL_HQ = 1024
COL_FF = 1280
COL_FB = 1536
COL_HI = 1792
COL_HG = 2048
ROPE_COLS = COL_V
MIX_IN_WIDTH = COL_HQ

V7X_LANES = 128
COND_ROWS = 16
NEG_BIG = -1e30
TOKEN_TILE = 512
EXPERT_TILE = 1024
SC_BATCH = 64

VMEM_LIMIT = 56 * 1024 * 1024


def _cparams(sem):
    return pltpu.CompilerParams(dimension_semantics=sem, vmem_limit_bytes=VMEM_LIMIT)


def _dot(a, b):
    return jnp.dot(a, b, preferred_element_type=F32)


def _dot_nt(a, b):
    return lax.dot_general(a, b, (((1,), (1,)), ((), ())), preferred_element_type=F32)


def _dot_tn(a, b):
    return lax.dot_general(a, b, (((0,), (0,)), ((), ())), preferred_element_type=F32)


def _split3(x):
    hi = x.astype(BF16)
    r1 = x - hi.astype(F32)
    mid = r1.astype(BF16)
    lo = (r1 - mid.astype(F32)).astype(BF16)
    return hi, mid, lo


def _dot_exact_lhs(m_bf16, x):
    hi, mid, lo = _split3(x)
    return _dot(m_bf16, hi) + _dot(m_bf16, mid) + _dot(m_bf16, lo)


def _dot_exact_rhs(x, m_bf16):
    hi, mid, lo = _split3(x)
    return _dot(hi, m_bf16) + _dot(mid, m_bf16) + _dot(lo, m_bf16)


def _dot_hp(a, b):
    a_hi = a.astype(BF16)
    a_lo = (a - a_hi.astype(F32)).astype(BF16)
    b_hi = b.astype(BF16)
    b_lo = (b - b_hi.astype(F32)).astype(BF16)
    return _dot(a_hi, b_hi) + _dot(a_hi, b_lo) + _dot(a_lo, b_hi)


def _pack_bf16_pair(lo, hi):
    return lax.bitcast_convert_type(pltpu.pack_elementwise([lo, hi], packed_dtype=BF16), I32)


def _unpack_bf16_pair(w):
    u = lax.bitcast_convert_type(w, jnp.uint32)
    lo = pltpu.unpack_elementwise(u, index=0, packed_dtype=BF16, unpacked_dtype=F32)
    hi = pltpu.unpack_elementwise(u, index=1, packed_dtype=BF16, unpacked_dtype=F32)
    return lo, hi


def _ln_plain(x, eps):
    mu = jnp.mean(x, axis=-1, keepdims=True)
    xc = x - mu
    var = jnp.mean(xc * xc, axis=-1, keepdims=True)
    return xc * lax.rsqrt(var + eps)


def _silu(x):
    return x * jax.nn.sigmoid(x)


def _adaln_kernel(c_ref, w_ref, b_ref, o_ref):
    s = _silu(c_ref[...])
    o_ref[0] = _dot_hp(s, w_ref[0]) + b_ref[0]


def _adaln(cond, w_ada, b_ada):
    return pl.pallas_call(
        _adaln_kernel,
        grid=(DEPTH, N_MOD),
        in_specs=[
            pl.BlockSpec((COND_ROWS, D_MODEL), lambda l, j: (0, 0)),
            pl.BlockSpec((1, D_MODEL, D_MODEL), lambda l, j: (l, 0, j)),
            pl.BlockSpec((1, 1, D_MODEL), lambda l, j: (l, 0, j)),
        ],
        out_specs=pl.BlockSpec((1, COND_ROWS, D_MODEL), lambda l, j: (l, 0, j)),
        out_shape=jax.ShapeDtypeStruct((DEPTH, COND_ROWS, N_MOD * D_MODEL), F32),
        compiler_params=_cparams(("arbitrary", "arbitrary")),
        name="adaln",
    )(cond, w_ada, b_ada.reshape(DEPTH, 1, N_MOD * D_MODEL))


class _Layout:
    def __init__(self, b_ctx, t_ctx, b_lat, t_lat, lat_first=0, tag=""):
        self.b_ctx, self.t_ctx, self.b_lat, self.t_lat = b_ctx, t_ctx, b_lat, t_lat
        self.lat_first = lat_first
        self.n_ctx = b_ctx * t_ctx
        self.n_lat = b_lat * t_lat
        self.n = self.n_ctx + self.n_lat
        self.tag = tag

    def cond_row(self, tile, tm):
        n_ctx_tiles = self.n_ctx // tm
        per_batch = self.t_lat // tm
        return jnp.where(tile < n_ctx_tiles, 0, 1 + self.lat_first + (tile - n_ctx_tiles) // per_batch)


def _group_specs(n_arrays, tm, width, n_ctx_tiles):
    if n_arrays == 1:
        return [pl.BlockSpec((tm, width), lambda i: (i, 0))]
    return [pl.BlockSpec((tm, width), lambda i: (jnp.minimum(i, n_ctx_tiles - 1), 0)),
            pl.BlockSpec((tm, width), lambda i: (jnp.maximum(i - n_ctx_tiles, 0), 0))]


def _inproj_kernel(*refs, n_ctx_tiles):
    n_tail = 9 if n_ctx_tiles > 0 else 7
    xs = refs[:-n_tail]
    mod_ref, w_ref, cos_ref, sin_ref, oa_ref, oh_ref = refs[-n_tail:-n_tail + 6]
    wb_ref = refs[-1]

    @pl.when(pl.program_id(0) == 0)
    def _():
        wb_ref[...] = w_ref[0].astype(BF16)

    if len(xs) == 2:
        x = jnp.where(pl.program_id(0) < n_ctx_tiles, xs[0][...], xs[1][...])
    else:
        x = xs[0][...]
    shift = mod_ref[0, 0:1, :]
    scale = mod_ref[0, 1:2, :]
    h = (_ln_plain(x, ADA_EPS) * (1.0 + scale) + shift).astype(BF16)
    p = _dot(h, wb_ref[...])
    cos = cos_ref[...]
    sin = sin_ref[...]
    lane = lax.broadcasted_iota(I32, cos.shape, 1)
    first_half = (lane & 31) < 16
    for cb in range(ROPE_COLS // V7X_LANES):
        seg = p[:, cb * V7X_LANES:(cb + 1) * V7X_LANES]
        partner = jnp.where(first_half, pltpu.roll(seg, V7X_LANES - 16, 1), pltpu.roll(seg, 16, 1))
        oa_ref[:, cb * V7X_LANES:(cb + 1) * V7X_LANES] = (seg * cos + partner * sin).astype(BF16)
    oa_ref[:, ROPE_COLS:] = p[:, ROPE_COLS:MIX_IN_WIDTH].astype(BF16)
    oh_ref[...] = p[:, MIX_IN_WIDTH:]

    if n_ctx_tiles > 0:
        kc_ref, vc_ref = refs[-3], refs[-2]

        @pl.when(pl.program_id(0) < n_ctx_tiles)
        def _():
            kc_ref[...] = p[:, COL_K:COL_K + KV_WIDTH]
            vc_ref[...] = p[:, COL_V:COL_V + KV_WIDTH]


def _rope_tables(lay, tm):
    t = lay.t_lat
    pos = jnp.arange(t)
    row = (pos // GRID_W).astype(F32)
    col = (pos % GRID_W).astype(F32)
    n_freq = HEAD_DIM // 4
    inv = ROPE_BASE ** (-jnp.arange(n_freq, dtype=F32) / n_freq)
    ang_r = row[:, None] * inv
    ang_c = col[:, None] * inv
    ang = jnp.concatenate([ang_r, ang_r, ang_c, ang_c], axis=1)
    sign = jnp.concatenate([-jnp.ones(n_freq), jnp.ones(n_freq), -jnp.ones(n_freq), jnp.ones(n_freq)]).astype(F32)
    cos = jnp.cos(ang)
    sin = jnp.sin(ang) * sign
    cos = jnp.concatenate([jnp.ones((tm, HEAD_DIM), F32), cos], axis=0)
    sin = jnp.concatenate([jnp.zeros((tm, HEAD_DIM), F32), sin], axis=0)
    return jnp.tile(cos, (1, 2)), jnp.tile(sin, (1, 2))


def _inproj(x, mod_l, w_in, layer, cos_t, sin_t, lay, tm):
    n_tiles = lay.n // tm
    n_ctx_tiles = lay.n_ctx // tm
    per_batch = lay.t_lat // tm

    def tbl(i):
        return jnp.where(i < n_ctx_tiles, 0, 1 + (i - n_ctx_tiles) % per_batch)

    xs = x if isinstance(x, tuple) else (x,)
    kv_specs, kv_shapes = [], []
    if n_ctx_tiles > 0:
        kv_specs = [pl.BlockSpec((tm, KV_WIDTH), lambda i: (jnp.minimum(i, n_ctx_tiles - 1), 0))] * 2
        kv_shapes = [jax.ShapeDtypeStruct((lay.n_ctx, KV_WIDTH), F32)] * 2
    return pl.pallas_call(
        functools.partial(_inproj_kernel, n_ctx_tiles=n_ctx_tiles),
        grid=(n_tiles,),
        in_specs=[
            *_group_specs(len(xs), tm, D_MODEL, n_ctx_tiles),
            pl.BlockSpec((1, N_MOD, D_MODEL), lambda i: (lay.cond_row(i, tm), 0, 0)),
            pl.BlockSpec((1, D_MODEL, IN_WIDTH), lambda i: (layer, 0, 0), pipeline_mode=pl.Buffered(1)),
            pl.BlockSpec((tm, V7X_LANES), lambda i: (tbl(i), 0)),
            pl.BlockSpec((tm, V7X_LANES), lambda i: (tbl(i), 0)),
        ],
        out_specs=[pl.BlockSpec((tm, MIX_IN_WIDTH), lambda i: (i, 0)),
                   pl.BlockSpec((tm, IN_WIDTH - MIX_IN_WIDTH), lambda i: (i, 0))] + kv_specs,
        out_shape=[jax.ShapeDtypeStruct((lay.n, MIX_IN_WIDTH), BF16),
                   jax.ShapeDtypeStruct((lay.n, IN_WIDTH - MIX_IN_WIDTH), F32)] + kv_shapes,
        scratch_shapes=[pltpu.VMEM((D_MODEL, IN_WIDTH), BF16)],
        compiler_params=_cparams(("arbitrary",)),
        name="inproj" + lay.tag,
    )(*xs, mod_l, w_in, cos_t, sin_t)


def _attn_kernel(sink_ref, q_ref, *refs, n_local, has_ctx, t_total):
    o_ref = refs[-1]
    k_refs = refs[:n_local]
    v_refs = refs[n_local:2 * n_local]
    tq = q_ref.shape[0]
    scale = HEAD_DIM ** -0.5
    k_parts = [kr[...] for kr in k_refs]
    v_parts = [vr[...] for vr in v_refs]
    if has_ctx:
        k_parts.append(refs[2 * n_local][0].astype(k_parts[0].dtype))
        v_parts.append(refs[2 * n_local + 1][0].astype(v_parts[0].dtype))
    kall = (jnp.concatenate(k_parts, axis=0) if len(k_parts) > 1 else k_parts[0]).astype(F32)
    vall = (jnp.concatenate(v_parts, axis=0) if len(v_parts) > 1 else v_parts[0]).astype(F32)
    nk = kall.shape[0]
    k_sw = pltpu.roll(kall, HEAD_DIM, 1)
    v_sw = pltpu.roll(vall, HEAD_DIM, 1)
    lo_half = lax.broadcasted_iota(I32, (1, V7X_LANES), 1) < HEAD_DIM
    er = jnp.where(lax.broadcasted_iota(I32, (2 * nk, V7X_LANES), 0) < nk, 0, 1)
    el = jnp.where(lax.broadcasted_iota(I32, (2 * nk, V7X_LANES), 1) < HEAD_DIM, 0, 1)
    ones_blk = jnp.where(er == el, 1.0, 0.0).astype(BF16)
    if n_local > 1:
        i = pl.program_id(1)
        band = refs[-2][...]
        first_blk = jnp.where(i == 0, NEG_BIG, 0.0)
        last_blk = jnp.where(i == t_total // tq - 1, NEG_BIG, 0.0)

        def mask_local(sc):
            loc = sc[:, :n_local * tq] + band
            parts = [loc[:, :tq] + first_blk, loc[:, tq:(n_local - 1) * tq], loc[:, (n_local - 1) * tq:] + last_blk]
            return jnp.concatenate(parts + [sc[:, n_local * tq:]], axis=1)
    else:
        mask_local = None
    v2es, sinks, scores = [], [], []
    for g in range(N_KV_HEADS):
        k_own, k_oth = (kall, k_sw) if g == 0 else (k_sw, kall)
        v_own, v_oth = (vall, v_sw) if g == 0 else (v_sw, vall)
        k2 = jnp.concatenate([jnp.where(lo_half, k_own, 0.0), jnp.where(lo_half, 0.0, k_oth)], axis=0).astype(BF16)
        v2 = jnp.concatenate([jnp.where(lo_half, v_own, 0.0), jnp.where(lo_half, 0.0, v_oth)], axis=0).astype(BF16)
        v2es.append(jnp.concatenate([v2, ones_blk], axis=1))
        pairs = [2 * g, 2 * g + 1]
        qq = jnp.concatenate([q_ref[:, p * V7X_LANES:(p + 1) * V7X_LANES] for p in pairs], axis=0)
        qq = (qq.astype(F32) * scale).astype(BF16)
        sinks.append((jnp.concatenate([jnp.full((tq, 1), sink_ref[0, 2 * p], F32) for p in pairs], axis=0),
                      jnp.concatenate([jnp.full((tq, 1), sink_ref[0, 2 * p + 1], F32) for p in pairs], axis=0)))
        scores.append(_dot_nt(qq, k2))
    pes, maxes = [], []
    for g in range(N_KV_HEADS):
        s_a = scores[g][:, :nk]
        s_b = scores[g][:, nk:]
        if mask_local is not None:
            s_a = mask_local(s_a)
            s_b = mask_local(s_b)
        m_a = jnp.maximum(jnp.max(s_a, axis=1, keepdims=True), sinks[g][0])
        m_b = jnp.maximum(jnp.max(s_b, axis=1, keepdims=True), sinks[g][1])
        pes.append(jnp.concatenate([jnp.exp(s_a - m_a).astype(BF16), jnp.exp(s_b - m_b).astype(BF16)], axis=1))
        maxes.append((m_a, m_b))
    for g in range(N_KV_HEADS):
        acc = _dot(pes[g], v2es[g])
        (m_a, m_b), (sink_a, sink_b) = maxes[g], sinks[g]
        sink_term = jnp.where(lo_half, jnp.exp(sink_a - m_a), jnp.exp(sink_b - m_b))
        o = acc[:, :V7X_LANES] / (acc[:, V7X_LANES:] + sink_term)
        for j, p in enumerate([2 * g, 2 * g + 1]):
            o_ref[:, p * V7X_LANES:(p + 1) * V7X_LANES] = o[j * tq:(j + 1) * tq].astype(o_ref.dtype)


def _attn_context(proj, sink_l, lay):
    t = lay.t_ctx
    kb, vb = COL_K // KV_WIDTH, COL_V // KV_WIDTH
    body = functools.partial(_attn_kernel, n_local=1, has_ctx=False, t_total=t)
    return pl.pallas_call(
        body,
        grid=(lay.b_ctx,),
        in_specs=[
            pl.BlockSpec(memory_space=pltpu.SMEM),
            pl.BlockSpec((t, ATTN_WIDTH), lambda b: (b, 0)),
            pl.BlockSpec((t, KV_WIDTH), lambda b: (b, kb)),
            pl.BlockSpec((t, KV_WIDTH), lambda b: (b, vb)),
        ],
        out_specs=pl.BlockSpec((t, ATTN_WIDTH), lambda b: (b, 0)),
        out_shape=jax.ShapeDtypeStruct((lay.n_ctx, ATTN_WIDTH), BF16),
        compiler_params=_cparams(("arbitrary",)),
        name="attn_ctx",
    )(sink_l, proj, proj, proj)


def _attn_latent(proj, k_ctx, v_ctx, sink_l, lay):
    t = lay.t_lat
    tq = ATTN_BLOCK
    nq = t // tq
    base = lay.n_ctx // tq
    kb, vb = COL_K // KV_WIDTH, COL_V // KV_WIDTH
    past = k_ctx.shape[1]

    def rows(off):
        return lambda b, i: base + b * nq + jnp.clip(i + off, 0, nq - 1)

    def kv_specs(col):
        return [pl.BlockSpec((tq, KV_WIDTH), (lambda b, i, f=rows(off): (f(b, i), col))) for off in (-1, 0, 1)]

    body = functools.partial(_attn_kernel, n_local=3, has_ctx=True, t_total=t)
    rel = np.arange(3 * tq)[None, :] - tq - (np.arange(2 * tq)[:, None] % tq)
    band = jnp.asarray(np.where(np.abs(rel) <= WINDOW, 0.0, NEG_BIG).astype(np.float32))
    return pl.pallas_call(
        body,
        grid=(lay.b_lat, nq),
        in_specs=[
            pl.BlockSpec(memory_space=pltpu.SMEM),
            pl.BlockSpec((tq, ATTN_WIDTH), lambda b, i: (base + b * nq + i, 0)),
            *kv_specs(kb),
            *kv_specs(vb),
            pl.BlockSpec((1, past, KV_WIDTH), lambda b, i: (b, 0, 0)),
            pl.BlockSpec((1, past, KV_WIDTH), lambda b, i: (b, 0, 0)),
            pl.BlockSpec(band.shape, lambda b, i: (0, 0)),
        ],
        out_specs=pl.BlockSpec((tq, ATTN_WIDTH), lambda b, i: (b * nq + i, 0)),
        out_shape=jax.ShapeDtypeStruct((lay.n_lat, ATTN_WIDTH), BF16),
        compiler_params=_cparams(("arbitrary", "arbitrary")),
        name="attn_lat" + lay.tag,
    )(sink_l, proj, proj, proj, proj, proj, proj, proj, k_ctx, v_ctx, band)


def _fourier_kernel(cs_ref, u_ref, cc_ref, sc_ref, o_ref, csb_ref, *, scale):
    @pl.when(pl.program_id(1) == 0)
    def _():
        csb_ref[...] = cs_ref[...].astype(BF16)

    z = u_ref[...].astype(BF16)
    zc = _dot(z, cc_ref[...].astype(BF16)).astype(BF16)
    zs = _dot(z, sc_ref[...].astype(BF16)).astype(BF16)
    zz = jnp.concatenate([zc, zs], axis=0)
    o_ref[...] = (_dot(csb_ref[...], zz) * scale).astype(o_ref.dtype)


@functools.lru_cache(maxsize=None)
def _dft_tables(t):
    idx = np.arange(t, dtype=np.int64)
    ang = 2.0 * np.pi * ((idx[:, None] * idx[None, :]) % t).astype(np.float64) / t
    cs = np.concatenate([np.cos(ang), -np.sin(ang)], axis=1).astype(np.float32)
    cw = FOURIER_WIDTH // FOURIER_GROUPS
    cidx = np.arange(cw, dtype=np.int64)
    cang = 2.0 * np.pi * ((cidx[:, None] * cidx[None, :]) % cw).astype(np.float64) / cw
    eye = np.eye(FOURIER_GROUPS)
    cc = np.kron(eye, np.cos(cang)).astype(np.float32)
    sc = np.kron(eye, np.sin(cang)).astype(np.float32)
    return cs, cc, sc


def _fourier(proj, row0, b, t, tm, name):
    cs, cc, sc = _dft_tables(t)
    cw = FOURIER_WIDTH // FOURIER_GROUPS
    nt = t // tm
    ub = COL_U // FOURIER_WIDTH
    base = row0 // t
    body = functools.partial(_fourier_kernel, scale=1.0 / math.sqrt(t * cw))
    return pl.pallas_call(
        body,
        grid=(nt, b),
        in_specs=[
            pl.BlockSpec((tm, 2 * t), lambda i, bb: (i, 0)),
            pl.BlockSpec((t, FOURIER_WIDTH), lambda i, bb: (base + bb, ub)),
            pl.BlockSpec((FOURIER_WIDTH, FOURIER_WIDTH), lambda i, bb: (0, 0)),
            pl.BlockSpec((FOURIER_WIDTH, FOURIER_WIDTH), lambda i, bb: (0, 0)),
        ],
        out_specs=pl.BlockSpec((tm, FOURIER_WIDTH), lambda i, bb: (bb * nt + i, 0)),
        out_shape=jax.ShapeDtypeStruct((b * t, FOURIER_WIDTH), BF16),
        scratch_shapes=[pltpu.VMEM((tm, 2 * t), BF16)],
        compiler_params=_cparams(("arbitrary", "arbitrary")),
        name=name,
    )(jnp.asarray(cs), proj, jnp.asarray(cc), jnp.asarray(sc))


HGRN_LEVELS = (64, 32, 16, 8, 4, 2)
HGRN_SAFE_RANGE = 80.0


@functools.lru_cache(maxsize=None)
def _hgrn_tables():
    c = HGRN_CHUNK
    return np.stack([np.tril(np.ones((c, c))), np.triu(np.ones((c, c)))]).astype(np.float32)


def _boundary_rows(b, m, reverse):
    c, w = b.shape
    half = m // 2
    off = half if reverse else half - 1
    if m >= 16:
        return jnp.concatenate(
            [jnp.broadcast_to(b[s + off:s + off + 1], (m, w)) for s in range(0, c, m)], axis=0)
    sub = lax.broadcasted_iota(I32, (c, w), 0) & 7
    b3 = b.reshape(c // 8, 8, w)

    def bcast(j):
        return jnp.broadcast_to(b3[:, j:j + 1, :], (c // 8, 8, w)).reshape(c, w)

    if m == 8:
        return bcast(off)
    if m == 4:
        return jnp.where(sub < 4, bcast(off), bcast(4 + off))
    assert m == 2
    if reverse:
        return jnp.where((sub & 1) == 1, b, pltpu.roll(b, c - 1, 0))
    return jnp.where((sub & 1) == 0, b, pltpu.roll(b, 1, 0))


def _hgrn_gates(q, z, v, loglb, log1mlb, onemlb, cum, reverse):
    c = HGRN_CHUNK
    log_sig = jnp.minimum(z, 0.0) - jnp.log1p(jnp.exp(-jnp.abs(z)))
    bb = log1mlb + log_sig
    mx = jnp.maximum(loglb, bb)
    lf = mx + jnp.log1p(jnp.exp(-jnp.abs(loglb - bb)))
    kk = onemlb * jax.nn.sigmoid(-z)
    b = _dot_exact_lhs(cum, lf)
    b_end = b[0:1] if reverse else b[c - 1:c]
    qt = (q * jnp.exp(b)).astype(BF16)
    kt = (kk * jnp.exp(b_end - b)).astype(BF16)
    return (q, kk, b), (qt, kt, v.astype(BF16))


def _head_stack(x):
    lane = lax.broadcasted_iota(I32, x.shape, 1)
    zero = jnp.zeros_like(x)
    return jnp.concatenate([jnp.where((lane >= h * HGRN_DK) & (lane < (h + 1) * HGRN_DK), x, zero)
                            for h in range(HGRN_HEADS)], axis=0)


class _HgrnDir:
    def __init__(self, f32_parts, bf16_parts, reverse):
        c = HGRN_CHUNK
        self.q, self.kk, self.b = f32_parts
        self.qt, self.kt, self.vb = bf16_parts
        self.reverse = reverse
        b_end = self.b[0:1] if reverse else self.b[c - 1:c]
        self.decay = jnp.exp(b_end)
        mid = c // 2 if reverse else c // 2 - 1
        self.rel = self.b - self.b[mid:mid + 1]
        self.span = jnp.max(jnp.abs(self.rel))

    def tree_decay_matrices(self):
        c = HGRN_CHUNK
        q, kk, b = self.q, self.kk, self.b
        row = lax.broadcasted_iota(I32, (c, 1), 0)
        ti = lax.broadcasted_iota(I32, (c, c), 0)
        si = lax.broadcasted_iota(I32, (c, c), 1)
        qb = q.astype(BF16)
        kb = kk.astype(BF16)
        heads = [slice(h * HGRN_DK, (h + 1) * HGRN_DK) for h in range(HGRN_HEADS)]
        acc = [jnp.where(ti == si, _dot_nt(qb[:, sl], kb[:, sl]), 0.0) for sl in heads]
        for m in HGRN_LEVELS:
            r = _boundary_rows(b, m, self.reverse)
            upper = (row & (m - 1)) >= (m // 2)
            q_side = jnp.logical_not(upper) if self.reverse else upper
            e = jnp.exp(jnp.where(q_side, b - r, r - b))
            qf = jnp.where(q_side, q * e, 0.0).astype(BF16)
            kf = jnp.where(q_side, 0.0, kk * e).astype(BF16)
            same_block = (ti & -m) == (si & -m)
            for h, sl in enumerate(heads):
                acc[h] = acc[h] + jnp.where(same_block, _dot_nt(qf[:, sl], kf[:, sl]), 0.0)
        return jnp.concatenate(acc, axis=1)

    def midpoint_decay_matrices(self):
        c = HGRN_CHUNK
        ti = lax.broadcasted_iota(I32, (c, HGRN_HEADS * c), 0)
        si = lax.broadcasted_iota(I32, (c, HGRN_HEADS * c), 1) & (c - 1)
        qm = (self.q * jnp.exp(self.rel)).astype(BF16)
        km = (self.kk * jnp.exp(-self.rel)).astype(BF16)
        causal = (si >= ti) if self.reverse else (si <= ti)
        return jnp.where(causal, _dot_nt(qm, _head_stack(km)), 0.0)

    def outputs(self, a_cat, st_ref, same_head, d):
        bd = st_ref[d]
        o = _dot_nt(self.qt, bd.astype(BF16)) + _dot(a_cat.astype(BF16), _head_stack(self.vb))
        st_ref[d] = jnp.where(same_head, bd * self.decay + _dot_tn(self.vb, self.kt), 0.0)
        return o


def _hgrn_kernel(hq_ref, ff_ref, fb_ref, hi_ref, hg_ref, lbp_ref, gn_ref, mall_ref, ones_ref, s0_ref, *refs, t,
                 layer):
    c = HGRN_CHUNK
    n = t // c
    rec_ref, sfin_ref, st_ref, of_ref, ob_ref, gf_ref, gb_ref = refs[-7:]
    blocks = [slice(h * HGRN_DK, (h + 1) * HGRN_DK) for h in range(HGRN_HEADS)]
    zero_blk = jnp.zeros((HGRN_DK, HGRN_DK), F32)

    def init_state():
        for d in range(2):
            st_ref[d] = jnp.concatenate(
                [jnp.concatenate([s0_ref[0, d, h] if j == h else zero_blk for j in range(HGRN_HEADS)], axis=1)
                 for h in range(HGRN_HEADS)], axis=0)

    def chunk_rows(ci):
        return pl.ds(pl.multiple_of(ci * c, c), c), pl.ds(pl.multiple_of((n - 1 - ci) * c, c), c)

    def gates_to(slot, ci):
        rf, rb = chunk_rows(ci)
        for d, (rows, f_ref) in enumerate(((rf, ff_ref), (rb, fb_ref))):
            f32_parts, bf16_parts = _hgrn_gates(
                hq_ref[rows, :], f_ref[rows, :], hi_ref[rows, :], lbp_ref[d, 0:1, :], lbp_ref[d, 1:2, :],
                lbp_ref[d, 2:3, :], mall_ref[d].astype(BF16), d == 1)
            for j in range(3):
                gf_ref[slot, d, j] = f32_parts[j]
                gb_ref[slot, d, j] = bf16_parts[j]

    def run(decay_matrices):
        init_state()
        gates_to(0, 0)

        def body(ci, widest):
            slot = ci & 1
            rf, rb = chunk_rows(ci)
            fwd = _HgrnDir([gf_ref[slot, 0, j] for j in range(3)], [gb_ref[slot, 0, j] for j in range(3)], False)
            bwd = _HgrnDir([gf_ref[slot, 1, j] for j in range(3)], [gb_ref[slot, 1, j] for j in range(3)], True)
            a_f, a_b = decay_matrices(fwd), decay_matrices(bwd)
            gates_to(1 - slot, jnp.minimum(ci + 1, n - 1))
            same_head = ones_ref[...] != 0.0
            of_ref[rf, :] = fwd.outputs(a_f, st_ref, same_head, 0)
            ob_ref[rb, :] = bwd.outputs(a_b, st_ref, same_head, 1)
            return jnp.maximum(widest, jnp.maximum(fwd.span, bwd.span))

        return lax.fori_loop(0, n, body, jnp.float32(0.0), unroll=8)

    widest = run(_HgrnDir.midpoint_decay_matrices)

    @pl.when(widest > HGRN_SAFE_RANGE)
    def _():
        run(_HgrnDir.tree_decay_matrices)

    slot = layer if sfin_ref.shape[1] > 1 else 0
    for j in range(sfin_ref.shape[1]):
        if j != slot:
            sfin_ref[0, j] = jnp.zeros(sfin_ref.shape[2:], F32)
    for d in range(2):
        bd = st_ref[d].T
        for h, sl in enumerate(blocks):
            sfin_ref[0, slot, d, h] = bd[sl, sl]
    o = of_ref[...] + ob_ref[...]
    ms = _dot_exact_rhs(o * o, ones_ref[...].astype(BF16)) * (1.0 / HGRN_DK)
    o = o * lax.rsqrt(ms + GN_EPS) * gn_ref[...]
    rec_ref[...] = (o * _silu(hg_ref[...])).astype(rec_ref.dtype)


def _hgrn(proj, row0, b, t, lbp, gn_row, s0t, name, layer=0, depth=1, states=None):
    base = row0 // t
    m_all = jnp.asarray(_hgrn_tables())
    ones_bd = jnp.asarray(np.kron(np.eye(HGRN_HEADS), np.ones((HGRN_DK, HGRN_DK))).astype(np.float32))

    def col(cstart):
        return pl.BlockSpec((t, HGRN_WIDTH), lambda bb, cb=(cstart - MIX_IN_WIDTH) // HGRN_WIDTH: (base + bb, cb))

    const2 = lambda bb: (0, 0)
    const3 = lambda bb: (0, 0, 0)
    st_shape = (2, HGRN_HEADS, HGRN_DK, HGRN_DK)
    body = functools.partial(_hgrn_kernel, t=t, layer=layer)
    first = states is None
    operands = [proj, proj, proj, proj, proj, lbp, gn_row, m_all, ones_bd, s0t]
    in_specs = [
        col(COL_HQ), col(COL_FF), col(COL_FB), col(COL_HI), col(COL_HG),
        pl.BlockSpec((2, 3, HGRN_WIDTH), const3),
        pl.BlockSpec((1, HGRN_WIDTH), const2),
        pl.BlockSpec(m_all.shape, const3),
        pl.BlockSpec(ones_bd.shape, const2),
        pl.BlockSpec((1,) + st_shape, lambda bb: (bb, 0, 0, 0, 0)),
    ]
    aliases = {}
    if states is not None:
        aliases = {len(operands): 1}
        operands.append(states)
        in_specs.append(pl.BlockSpec(memory_space=pl.ANY))
    return pl.pallas_call(
        body,
        grid=(b,),
        in_specs=in_specs,
        out_specs=[
            pl.BlockSpec((t, HGRN_WIDTH), lambda bb: (bb, 0)),
            pl.BlockSpec((1, depth if first else 1) + st_shape, lambda bb: (bb, 0 if first else layer, 0, 0, 0, 0)),
        ],
        out_shape=[
            jax.ShapeDtypeStruct((b * t, HGRN_WIDTH), BF16),
            jax.ShapeDtypeStruct((b, depth) + st_shape, F32),
        ],
        input_output_aliases=aliases,
        scratch_shapes=[
            pltpu.VMEM((2, HGRN_WIDTH, HGRN_WIDTH), F32),
            pltpu.VMEM((t, HGRN_WIDTH), F32),
            pltpu.VMEM((t, HGRN_WIDTH), F32),
            pltpu.VMEM((2, 2, 3, HGRN_CHUNK, HGRN_WIDTH), F32),
            pltpu.VMEM((2, 2, 3, HGRN_CHUNK, HGRN_WIDTH), BF16),
        ],
        compiler_params=_cparams(("arbitrary",)),
        name=name,
    )(*operands)


def _outproj_kernel(*refs, n_ctx_tiles, n_mix):
    mix = refs[:3 * n_mix]
    refs = refs[3 * n_mix:]
    xs = refs[:-14]
    (mod_ref, w_ref, g_ref, b_ref, rw_ref, rb_ref, x1_ref, hp_ref, meta_ref, gate_ref, cnt_ref, wb_ref, tri_ref,
     run_ref) = refs[-14:]
    tm = x1_ref.shape[0]
    is_ctx = pl.program_id(0) < n_ctx_tiles
    x_in = jnp.where(is_ctx, xs[0][...], xs[1][...]) if len(xs) == 2 else xs[0][...]
    if n_mix == 2:
        attn, four, rec = [jnp.where(is_ctx, mix[2 * j][...], mix[2 * j + 1][...]) for j in range(3)]
    else:
        attn, four, rec = [r[...] for r in mix]

    @pl.when(pl.program_id(0) == 0)
    def _():
        wb_ref[...] = w_ref[0].astype(BF16)
        r = lax.broadcasted_iota(I32, (tm, tm), 0)
        c = lax.broadcasted_iota(I32, (tm, tm), 1)
        tri_ref[...] = jnp.where(r < c, 1.0, 0.0).astype(BF16)
        run_ref[...] = jnp.zeros_like(run_ref)

    out = _dot(attn, wb_ref[0:ATTN_WIDTH, :])
    out = out + _dot(four, wb_ref[ATTN_WIDTH:ATTN_WIDTH + FOURIER_WIDTH, :])
    out = out + _dot(rec, wb_ref[ATTN_WIDTH + FOURIER_WIDTH:, :])
    gate1 = mod_ref[0, 2:3, :]
    y = DEEPNORM_ALPHA * x_in + gate1 * out
    x1 = _ln_plain(y, LN_EPS) * g_ref[...] + b_ref[...]
    x1_ref[...] = x1
    h2 = _ln_plain(x1, ADA_EPS) * (1.0 + mod_ref[0, 4:5, :]) + mod_ref[0, 3:4, :]
    hp_ref[...] = _pack_bf16_pair(h2[:, :HALF_D], h2[:, HALF_D:])

    h_hi = h2.astype(BF16)
    h_lo = (h2 - h_hi.astype(F32)).astype(BF16)
    rwt = rw_ref[...]
    w_hi = rwt.astype(BF16)
    w_lo = (rwt - w_hi.astype(F32)).astype(BF16)
    scores = jax.nn.sigmoid(_dot_nt(w_hi, h_hi) + _dot_nt(w_hi, h_lo) + _dot_nt(w_lo, h_hi))
    remaining = scores + rb_ref[...]
    eidx = lax.broadcasted_iota(I32, scores.shape, 0).astype(F32)
    chosen = jnp.zeros(scores.shape, jnp.bool_)
    picks = []
    for _ in range(TOP_K):
        mx = jnp.max(remaining, axis=0, keepdims=True)
        first = jnp.min(jnp.where(remaining == mx, eidx, float(N_EXPERTS)), axis=0, keepdims=True)
        pick = eidx == first
        picks.append((pick, first))
        chosen = jnp.logical_or(chosen, pick)
        remaining = jnp.where(pick, -jnp.inf, remaining)
    sel = jnp.where(chosen, scores, 0.0)
    gates = sel / jnp.sum(sel, axis=0, keepdims=True) * ROUTED_SCALE

    onehot = jnp.where(chosen, 1.0, 0.0)
    rank = run_ref[...] + _dot(onehot.astype(BF16), tri_ref[...])
    run_ref[...] += jnp.sum(onehot, axis=1, keepdims=True)
    cnt_ref[...] = run_ref[...]

    ids, rks, gks = [], [], []
    for pick, first in picks:
        ids.append(first.astype(I32))
        rks.append(jnp.sum(jnp.where(pick, rank, 0.0), axis=0, keepdims=True).astype(I32))
        gks.append(jnp.sum(jnp.where(pick, gates, 0.0), axis=0, keepdims=True))
    meta_ref[...] = jnp.concatenate(ids + rks, axis=0)
    gate_ref[...] = jnp.concatenate(gks, axis=0)


def _outproj(attn, four, rec, x, mod_l, w_out, layer, g1, b1, rw, rb, lay, tm):
    n_tiles = lay.n // tm
    n_ctx_tiles = lay.n_ctx // tm
    row = lambda i: (i, 0)
    const = lambda i: (0, 0)
    xs = x if isinstance(x, tuple) else (x,)
    return pl.pallas_call(
        functools.partial(_outproj_kernel, n_ctx_tiles=n_ctx_tiles, n_mix=len(attn)),
        grid=(n_tiles,),
        in_specs=[
            *_group_specs(len(attn), tm, ATTN_WIDTH, n_ctx_tiles),
            *_group_specs(len(four), tm, FOURIER_WIDTH, n_ctx_tiles),
            *_group_specs(len(rec), tm, HGRN_WIDTH, n_ctx_tiles),
            *_group_specs(len(xs), tm, D_MODEL, n_ctx_tiles),
            pl.BlockSpec((1, N_MOD, D_MODEL), lambda i: (lay.cond_row(i, tm), 0, 0)),
            pl.BlockSpec((1, D_MODEL, D_MODEL), lambda i: (layer, 0, 0)),
            pl.BlockSpec((1, D_MODEL), const),
            pl.BlockSpec((1, D_MODEL), const),
            pl.BlockSpec((N_EXPERTS, D_MODEL), const),
            pl.BlockSpec((N_EXPERTS, 1), const),
        ],
        out_specs=[
            pl.BlockSpec((tm, D_MODEL), row),
            pl.BlockSpec((tm, HALF_D), row),
            pl.BlockSpec((2 * TOP_K, tm), lambda i: (0, i)),
            pl.BlockSpec((TOP_K, tm), lambda i: (0, i)),
            pl.BlockSpec((N_EXPERTS, 1), const),
        ],
        out_shape=[
            jax.ShapeDtypeStruct((lay.n, D_MODEL), F32),
            jax.ShapeDtypeStruct((lay.n, HALF_D), I32),
            jax.ShapeDtypeStruct((2 * TOP_K, lay.n), I32),
            jax.ShapeDtypeStruct((TOP_K, lay.n), F32),
            jax.ShapeDtypeStruct((N_EXPERTS, 1), F32),
        ],
        scratch_shapes=[
            pltpu.VMEM((D_MODEL, D_MODEL), BF16),
            pltpu.VMEM((tm, tm), BF16),
            pltpu.VMEM((N_EXPERTS, 1), F32),
        ],
        compiler_params=_cparams(("arbitrary",)),
        name="outproj_router" + lay.tag,
    )(*attn, *four, *rec, *xs, mod_l, w_out, g1, b1, rw, rb)


def _sc_workers():
    info = plsc.get_sparse_core_info()
    return info.num_cores, info.num_cores * info.num_subcores


def _sc_scatter_rows(rows, pos_b, r_out, tag=""):
    nc, nw = _sc_workers()
    n, w = rows.shape
    nbt, copies, _ = pos_b.shape
    assert nbt * SC_BATCH == n and nbt % (2 * nw) == 0
    per_w = nbt // nw
    mesh = plsc.VectorSubcoreMesh(core_axis_name="c", subcore_axis_name="s")

    @functools.partial(
        pl.kernel, mesh=mesh, out_type=jax.ShapeDtypeStruct((r_out, w), rows.dtype),
        scratch_types=[pltpu.VMEM((copies, SC_BATCH), I32), pltpu.VMEM((copies, SC_BATCH), I32),
                       pltpu.VMEM((SC_BATCH, w), rows.dtype), pltpu.VMEM((SC_BATCH, w), rows.dtype),
                       pltpu.SemaphoreType.DMA, pltpu.SemaphoreType.DMA,
                       pltpu.SemaphoreType.DMA, pltpu.SemaphoreType.DMA],
        name="sc_dispatch" + tag)
    def k(rows_hbm, pos_hbm, out_hbm, idx_a, idx_b, rows_a, rows_b, sem_ra, sem_rb, sem_sa, sem_sb):
        wid = lax.axis_index("s") * nc + lax.axis_index("c")
        first = wid * per_w

        def reads(j, idx_v, rows_v, sem):
            bt = first + j
            return (pltpu.make_async_copy(pos_hbm.at[bt], idx_v, sem),
                    pltpu.make_async_copy(rows_hbm.at[pl.ds(bt * SC_BATCH, SC_BATCH)], rows_v, sem))

        def scatters(idx_v, rows_v, sem):
            return [pltpu.make_async_copy(rows_v, out_hbm.at[idx_v.at[q]], sem) for q in range(copies)]

        def start(descs):
            for d in descs:
                d.start()

        def wait(descs):
            for d in descs:
                d.wait()

        start(reads(0, idx_a, rows_a, sem_ra))

        @pl.loop(0, per_w // 2)
        def _(p):
            j0 = 2 * p
            j1 = j0 + 1

            @pl.when(p > 0)
            def _():
                wait(scatters(idx_b, rows_b, sem_sb))

            start(reads(j1, idx_b, rows_b, sem_rb))
            wait(reads(j0, idx_a, rows_a, sem_ra))
            start(scatters(idx_a, rows_a, sem_sa))
            wait(reads(j1, idx_b, rows_b, sem_rb))
            start(scatters(idx_b, rows_b, sem_sb))
            wait(scatters(idx_a, rows_a, sem_sa))

            @pl.when(p + 1 < per_w // 2)
            def _():
                start(reads(j0 + 2, idx_a, rows_a, sem_ra))

        wait(scatters(idx_b, rows_b, sem_sb))

    return k(rows, pos_b)


def _sc_gather_rows(table, idx, tag=""):
    nc, nw = _sc_workers()
    r = idx.shape[0]
    w = table.shape[1]
    assert r % (2 * nw * SC_BATCH) == 0
    per_w = r // nw
    nb = per_w // SC_BATCH
    mesh = plsc.VectorSubcoreMesh(core_axis_name="c", subcore_axis_name="s")

    @functools.partial(
        pl.kernel, mesh=mesh, out_type=jax.ShapeDtypeStruct((r, w), table.dtype),
        scratch_types=[pltpu.VMEM((per_w,), I32),
                       pltpu.VMEM((SC_BATCH, w), table.dtype), pltpu.VMEM((SC_BATCH, w), table.dtype),
                       pltpu.SemaphoreType.DMA, pltpu.SemaphoreType.DMA,
                       pltpu.SemaphoreType.DMA, pltpu.SemaphoreType.DMA],
        name="sc_combine" + tag)
    def k(table_hbm, idx_hbm, out_hbm, idx_v, rows_a, rows_b, sem_ga, sem_gb, sem_wa, sem_wb):
        wid = lax.axis_index("s") * nc + lax.axis_index("c")
        base = wid * per_w
        pltpu.sync_copy(idx_hbm.at[pl.ds(base, per_w)], idx_v)

        def gather(j, rows_v, sem):
            return pltpu.make_async_copy(table_hbm.at[idx_v.at[pl.ds(j * SC_BATCH, SC_BATCH)]], rows_v, sem)

        def write(j, rows_v, sem):
            return pltpu.make_async_copy(rows_v, out_hbm.at[pl.ds(base + j * SC_BATCH, SC_BATCH)], sem)

        gather(0, rows_a, sem_ga).start()

        @pl.loop(0, nb // 2)
        def _(p):
            j0 = 2 * p
            j1 = j0 + 1

            @pl.when(p > 0)
            def _():
                write(j1 - 2, rows_b, sem_wb).wait()

            gather(j1, rows_b, sem_gb).start()
            gather(j0, rows_a, sem_ga).wait()
            write(j0, rows_a, sem_wa).start()
            gather(j1, rows_b, sem_gb).wait()
            write(j1, rows_b, sem_wb).start()
            write(j0, rows_a, sem_wa).wait()

            @pl.when(p + 1 < nb // 2)
            def _():
                gather(j0 + 2, rows_a, sem_ga).start()

        write(nb - 1, rows_b, sem_wb).wait()

    return k(table, idx)


def _experts_kernel(te_ref, na_ref, x_ref, w1_ref, w3_ref, w2_ref, o_ref, w1b_ref, w3b_ref, w2b_ref):
    del te_ref

    @pl.when(pl.program_id(0) < na_ref[0])
    def _():
        w1b_ref[...] = w1_ref[0, 0].astype(BF16)
        w3b_ref[...] = w3_ref[0, 0].astype(BF16)
        w2b_ref[...] = w2_ref[0, 0].astype(BF16)
        lo, hi = _unpack_bf16_pair(x_ref[...])
        lo = lo.astype(BF16)
        hi = hi.astype(BF16)
        a = _dot(lo, w1b_ref[0:HALF_D, :]) + _dot(hi, w1b_ref[HALF_D:, :])
        b = _dot(lo, w3b_ref[0:HALF_D, :]) + _dot(hi, w3b_ref[HALF_D:, :])
        y = _dot((_silu(a) * b).astype(BF16), w2b_ref[...])
        o_ref[...] = _pack_bf16_pair(y[:, :HALF_D], y[:, HALF_D:])


def _experts(xs, tile_expert, n_active, w1, w3, w2, layer, tm, tag):
    r = xs.shape[0]
    n_tiles = r // tm

    def xmap(j, te, na):
        return (jnp.minimum(j, na[0] - 1), 0)

    def wmap(j, te, na):
        return (layer, te[jnp.minimum(j, na[0] - 1)], 0, 0)

    grid_spec = pltpu.PrefetchScalarGridSpec(
        num_scalar_prefetch=2,
        grid=(n_tiles,),
        in_specs=[
            pl.BlockSpec((tm, HALF_D), xmap),
            pl.BlockSpec((1, 1, D_MODEL, EXPERT_FF), wmap),
            pl.BlockSpec((1, 1, D_MODEL, EXPERT_FF), wmap),
            pl.BlockSpec((1, 1, EXPERT_FF, D_MODEL), wmap),
        ],
        out_specs=pl.BlockSpec((tm, HALF_D), xmap),
        scratch_shapes=[
            pltpu.VMEM((D_MODEL, EXPERT_FF), BF16),
            pltpu.VMEM((D_MODEL, EXPERT_FF), BF16),
            pltpu.VMEM((EXPERT_FF, D_MODEL), BF16),
        ],
    )
    return pl.pallas_call(
        _experts_kernel,
        grid_spec=grid_spec,
        out_shape=jax.ShapeDtypeStruct((r, HALF_D), I32),
        compiler_params=_cparams(("arbitrary",)),
        name="experts" + tag,
    )(tile_expert, n_active, xs, w1, w3, w2)


def _combine_kernel(yp_ref, gate_ref, hp_ref, sw1_ref, sw3_ref, sw2_ref, x_ref, mod_ref, g_ref, b_ref, *refs,
                    n_ctx_tiles):
    outs = refs[:-3]
    w1b_ref, w3b_ref, w2b_ref = refs[-3:]

    @pl.when(pl.program_id(0) == 0)
    def _():
        w1b_ref[...] = sw1_ref[...].astype(BF16)
        w3b_ref[...] = sw3_ref[...].astype(BF16)
        w2b_ref[...] = sw2_ref[...].astype(BF16)

    lo, hi = _unpack_bf16_pair(hp_ref[...])
    lo = lo.astype(BF16)
    hi = hi.astype(BF16)
    a = _dot(lo, w1b_ref[0:HALF_D, :]) + _dot(hi, w1b_ref[HALF_D:, :])
    b = _dot(lo, w3b_ref[0:HALF_D, :]) + _dot(hi, w3b_ref[HALF_D:, :])
    shared = _dot((_silu(a) * b).astype(BF16), w2b_ref[...])
    acc_lo = shared[:, :HALF_D]
    acc_hi = shared[:, HALF_D:]
    gates = gate_ref[...].T
    for k in range(TOP_K):
        ylo, yhi = _unpack_bf16_pair(yp_ref[k])
        gk = gates[:, k:k + 1]
        acc_lo = acc_lo + gk * ylo
        acc_hi = acc_hi + gk * yhi
    moe = jnp.concatenate([acc_lo, acc_hi], axis=1)
    y = DEEPNORM_ALPHA * x_ref[...] + mod_ref[0, 5:6, :] * moe
    res = _ln_plain(y, LN_EPS) * g_ref[...] + b_ref[...]
    if len(outs) == 1:
        outs[0][...] = res
    else:
        @pl.when(pl.program_id(0) < n_ctx_tiles)
        def _():
            outs[0][...] = res

        @pl.when(pl.program_id(0) >= n_ctx_tiles)
        def _():
            outs[1][...] = res


def _combine(yp, gate8, hp, sw1, sw3, sw2, x1, mod_l, g2, b2, lay, tm, split_out):
    n_tiles = lay.n // tm
    n_ctx_tiles = lay.n_ctx // tm
    row = lambda i: (i, 0)
    const = lambda i: (0, 0)
    if split_out:
        out_specs = _group_specs(2, tm, D_MODEL, n_ctx_tiles)
        out_shape = [jax.ShapeDtypeStruct((lay.n_ctx, D_MODEL), F32), jax.ShapeDtypeStruct((lay.n_lat, D_MODEL), F32)]
    else:
        out_specs = pl.BlockSpec((tm, D_MODEL), row)
        out_shape = jax.ShapeDtypeStruct((lay.n, D_MODEL), F32)
    return pl.pallas_call(
        functools.partial(_combine_kernel, n_ctx_tiles=n_ctx_tiles),
        grid=(n_tiles,),
        in_specs=[
            pl.BlockSpec((TOP_K, tm, HALF_D), lambda i: (0, i, 0)),
            pl.BlockSpec((TOP_K, tm), lambda i: (0, i)),
            pl.BlockSpec((tm, HALF_D), row),
            pl.BlockSpec((D_MODEL, EXPERT_FF), const),
            pl.BlockSpec((D_MODEL, EXPERT_FF), const),
            pl.BlockSpec((EXPERT_FF, D_MODEL), const),
            pl.BlockSpec((tm, D_MODEL), row),
            pl.BlockSpec((1, N_MOD, D_MODEL), lambda i: (lay.cond_row(i, tm), 0, 0)),
            pl.BlockSpec((1, D_MODEL), const),
            pl.BlockSpec((1, D_MODEL), const),
        ],
        out_specs=out_specs,
        out_shape=out_shape,
        scratch_shapes=[
            pltpu.VMEM((D_MODEL, EXPERT_FF), BF16),
            pltpu.VMEM((D_MODEL, EXPERT_FF), BF16),
            pltpu.VMEM((EXPERT_FF, D_MODEL), BF16),
        ],
        compiler_params=_cparams(("arbitrary",)),
        name="combine_norm" + lay.tag,
    )(yp, gate8, hp, sw1, sw3, sw2, x1, mod_l, g2, b2)


def _moe_dispatch(hp, meta, counts, lay, tile):
    n = lay.n
    r_max = n * TOP_K + N_EXPERTS * tile
    n_tiles = r_max // tile
    cnt = counts.reshape(N_EXPERTS).astype(I32)
    padded = ((cnt + tile - 1) // tile) * tile
    ends = jnp.cumsum(padded)
    offsets = ends - padded
    idx8 = meta[:TOP_K]
    base8 = jnp.sum(jnp.where(idx8[:, :, None] == jnp.arange(N_EXPERTS, dtype=I32), offsets, 0), axis=-1)
    pos = (base8 + meta[TOP_K:]).astype(I32)
    tile_start = jnp.arange(n_tiles, dtype=I32) * tile
    tile_expert = jnp.minimum(jnp.sum(tile_start[:, None] >= ends[None, :], axis=1), N_EXPERTS - 1).astype(I32)
    n_active = (ends[-1] // tile).astype(I32).reshape(1)
    pos_b = pos.reshape(TOP_K, n // SC_BATCH, SC_BATCH).transpose(1, 0, 2)
    xs = _sc_scatter_rows(hp, pos_b, r_max, lay.tag)
    return xs, tile_expert, n_active, pos


def _moe_combine(ys, pos, gate8, hp, sw1, sw3, sw2, x1, mod_l, g2, b2, lay, split_out):
    n = lay.n
    yp = _sc_gather_rows(ys, pos.reshape(n * TOP_K), lay.tag).reshape(TOP_K, n, HALF_D)
    return _combine(yp, gate8, hp, sw1, sw3, sw2, x1, mod_l, g2, b2, lay, TOKEN_TILE, split_out)


def kernel(x_prompt, x_sample, cache_k, cache_v, state_hgrn, c, c_ctx, w_ada, b_ada, w_in, w_out, attn_sink, hgrn_lb, hgrn_norm, ln1_g, ln1_b, ln2_g, ln2_b, router_w, router_b, moe_w1, moe_w3, moe_w2, shared_w1, shared_w3, shared_w2):
    b_ctx, t_ctx, _ = x_prompt.shape
    b_lat, t_lat, _ = x_sample.shape
    past = cache_k.shape[2]
    tm = TOKEN_TILE
    assert 1 + b_lat <= COND_ROWS
    lay = _Layout(b_ctx, t_ctx, b_lat, t_lat)
    assert lay.n_ctx % tm == 0 and t_lat % tm == 0 and lay.n_ctx % t_lat == 0

    cond = jnp.concatenate([c_ctx[None, :], c, jnp.zeros((COND_ROWS - 1 - b_lat, D_MODEL), F32)], axis=0)
    mod = _adaln(cond, w_ada, b_ada).reshape(DEPTH, COND_ROWS, N_MOD, D_MODEL)

    lb_all = jnp.cumsum(jax.nn.softmax(hgrn_lb.astype(F32), axis=0), axis=0)
    lb_all = lb_all - lb_all[:1]
    lbp = jnp.stack([jnp.log(lb_all), jnp.log1p(-lb_all), 1.0 - lb_all], axis=2)

    cos_t, sin_t = _rope_tables(lay, tm)
    zero_state = jnp.zeros((b_ctx, 2, HGRN_HEADS, HGRN_DK, HGRN_DK), F32)

    def layer(l, lay, x, split_out, states):
        lat = slice(lay.lat_first, lay.lat_first + lay.b_lat)
        outs = _inproj(x, mod[l], w_in, l, cos_t, sin_t, lay, tm)
        proj, proj_h = outs[0], outs[1]
        sink_l = attn_sink[l].reshape(1, N_HEADS)
        gn_row = jnp.tile(hgrn_norm[l], HGRN_HEADS).reshape(1, HGRN_WIDTH)
        attn, four, rec, extras = [], [], [], None
        if lay.b_ctx:
            attn.append(_attn_context(proj, sink_l, lay))
            four.append(_fourier(proj, 0, lay.b_ctx, t_ctx, t_ctx, "fourier_ctx"))
            rec_c, states = _hgrn(proj_h, 0, lay.b_ctx, t_ctx, lbp[l], gn_row, zero_state, "hgrn_ctx", l, DEPTH,
                                  states)
            rec.append(rec_c)
            extras = (outs[2], outs[3], states)
        attn.append(_attn_latent(proj, cache_k[lat, l].reshape(lay.b_lat, past, KV_WIDTH),
                                 cache_v[lat, l].reshape(lay.b_lat, past, KV_WIDTH), sink_l, lay))
        four.append(_fourier(proj, lay.n_ctx, lay.b_lat, t_lat, min(t_lat, 512), "fourier_lat" + lay.tag))
        s0t = jnp.swapaxes(state_hgrn[lat, l].astype(F32), -1, -2)
        rec.append(_hgrn(proj_h, lay.n_ctx, lay.b_lat, t_lat, lbp[l], gn_row, s0t, "hgrn_lat" + lay.tag)[0])
        x1, hp, meta, gate8, counts = _outproj(
            tuple(attn), tuple(four), tuple(rec), x, mod[l], w_out, l, ln1_g[l].reshape(1, -1),
            ln1_b[l].reshape(1, -1), router_w[l].T, router_b[l].reshape(-1, 1), lay, tm)
        xs, tile_expert, n_active, pos = _moe_dispatch(hp, meta, counts, lay, EXPERT_TILE)
        ys = _experts(xs, tile_expert, n_active, moe_w1, moe_w3, moe_w2, l, EXPERT_TILE, lay.tag)
        x = _moe_combine(ys, pos, gate8, hp, shared_w1[l], shared_w3[l], shared_w2[l], x1, mod[l],
                         ln2_g[l].reshape(1, -1), ln2_b[l].reshape(1, -1), lay, split_out)
        return x, extras

    x = (x_prompt.reshape(lay.n_ctx, D_MODEL), x_sample.reshape(lay.n_lat, D_MODEL))
    ks_out, vs_out, states = [], [], None
    for l in range(DEPTH):
        x, (k_new, v_new, states) = layer(l, lay, x, l == DEPTH - 1, states)
        ks_out.append(k_new.reshape(b_ctx, t_ctx, N_KV_HEADS, HEAD_DIM))
        vs_out.append(v_new.reshape(b_ctx, t_ctx, N_KV_HEADS, HEAD_DIM))

    y_prompt = x[0].reshape(b_ctx, t_ctx, D_MODEL)
    y_sample = x[1].reshape(b_lat, t_lat, D_MODEL)
    new_cache_k = jnp.stack(ks_out, axis=1)
    new_cache_v = jnp.stack(vs_out, axis=1)
    new_state = states.astype(x_prompt.dtype)
    return (y_prompt, y_sample, new_cache_k, new_cache_v, new_state)
```
